```python
import jax, jax.numpy as jnp
from jax import lax
import numpy as np

D_MODEL = 1024
BATCH = 8
SEQ = 2048
DEPTH = 2

D_CONV = D_MODEL
CONV_WIDTH = 3
SGU_HEADS = 8
SGU_HEAD_DIM = 128
SGU_CHUNK = 128
D_SGU = SGU_HEADS * SGU_HEAD_DIM
ATTN_HEADS = 8
ATTN_HEAD_DIM = 128
D_ATTN = ATTN_HEADS * ATTN_HEAD_DIM
Q_BLOCK = 128
N_BRANCH = 3
D_FF = 2816
RMS_EPS = 1e-6
LN_EPS = 1e-5
FORGET_W_SCALE = 0.1
FORGET_BIAS = 3.0
SPLIT_SIZES = (D_CONV, D_CONV, D_CONV, D_SGU, D_SGU, D_ATTN, D_ATTN, D_ATTN, ATTN_HEADS, N_BRANCH * D_MODEL)
N_IN = 3 * D_CONV + 2 * D_SGU + 3 * D_ATTN + ATTN_HEADS + N_BRANCH * D_MODEL
FORGET_OFFSET = 3 * D_CONV + 2 * D_SGU + 3 * D_ATTN

kernel_name = "hybrid_conv_sgu_fox_macaron"


def rms_norm(x, g):
    xf = x.astype(jnp.float32)
    y = xf * lax.rsqrt(jnp.mean(xf * xf, axis=-1, keepdims=True) + RMS_EPS)
    return (y * g.astype(jnp.float32)).astype(x.dtype)


def layer_norm(x, g, b):
    xf = x.astype(jnp.float32)
    mu = jnp.mean(xf, axis=-1, keepdims=True)
    var = jnp.mean(jnp.square(xf - mu), axis=-1, keepdims=True)
    y = (xf - mu) * lax.rsqrt(var + LN_EPS)
    return (y * g.astype(jnp.float32) + b.astype(jnp.float32)).astype(x.dtype)


def swiglu(x, w_gu, w_down):
    g, u = jnp.split(x @ w_gu, 2, axis=-1)
    return (jax.nn.silu(g) * u) @ w_down


def causal_depthwise_conv(z, w):
    K = w.shape[0]
    zp = jnp.pad(z, ((0, 0), (K - 1, 0), (0, 0)))
    return lax.conv_general_dilated(
        zp, w[:, None, :], window_strides=(1,), padding='VALID',
        dimension_numbers=('NWC', 'WIO', 'NWC'), feature_group_count=z.shape[-1])


def chunked_spatial_gating(u, v, ln_g, ln_b, w_s, b_s):
    bsz, s_len, _ = v.shape
    n_chunks = s_len // SGU_CHUNK
    vn = layer_norm(v, ln_g, ln_b).reshape(bsz, n_chunks, SGU_CHUNK, SGU_HEADS, SGU_HEAD_DIM)
    causal = jnp.tril(jnp.ones((SGU_CHUNK, SGU_CHUNK), dtype=bool))
    w = jnp.where(causal[None], w_s, jnp.zeros((), w_s.dtype))
    s = jnp.einsum('gts,bnsgc->bntgc', w, vn) + b_s.T[None, None, :, :, None]
    return u * s.reshape(bsz, s_len, D_SGU)


def forgetting_attention(q, k, v, log_f):
    c = jnp.cumsum(log_f, axis=1).transpose(0, 2, 1)
    scale = ATTN_HEAD_DIM ** -0.5
    s_len = q.shape[1]
    outs = []
    for i in range(s_len // Q_BLOCK):
        q0 = i * Q_BLOCK
        q1 = q0 + Q_BLOCK
        logits = jnp.einsum('bqhd,bkhd->bhqk', q[:, q0:q1], k[:, :q1]).astype(jnp.float32) * scale
        decay = c[:, :, q0:q1, None] - c[:, :, None, :q1]
        causal = jnp.arange(q1)[None, :] <= jnp.arange(q0, q1)[:, None]
        logits = jnp.where(causal, logits + decay, -jnp.inf)
        p = jax.nn.softmax(logits, axis=-1)
        outs.append(jnp.einsum('bhqk,bkhd->bqhd', p.astype(v.dtype), v[:, :q1]))
    return jnp.concatenate(outs, axis=1)


def _fwd_setup_inputs(seed: int = 0) -> dict:
    key = jax.random.key(seed)
    ks = jax.random.split(key, 24)
    L, D = DEPTH, D_MODEL

    def dense(k, shape, fan_in):
        return jax.random.normal(k, shape, jnp.float32) * (fan_in ** -0.5)

    def gain(k, shape):
        return 1.0 + 0.02 * jax.random.normal(k, shape, jnp.float32)

    w_in = dense(ks[5], (L, D, N_IN), D)
    w_in = w_in.at[:, :, FORGET_OFFSET:FORGET_OFFSET + ATTN_HEADS].multiply(FORGET_W_SCALE)
    return {
        "x": jax.random.normal(ks[0], (BATCH, SEQ, D), jnp.float32),
        "ffn1_norm": gain(ks[1], (L, D)),
        "ffn1_w_gu": dense(ks[2], (L, D, 2 * D_FF), D),
        "ffn1_w_down": dense(ks[3], (L, D_FF, D), D_FF),
        "mix_norm": gain(ks[4], (L, D)),
        "w_in": w_in,
        "b_forget": FORGET_BIAS + 0.5 * jax.random.normal(ks[6], (L, ATTN_HEADS), jnp.float32),
        "b_gate": 0.02 * jax.random.normal(ks[7], (L, N_BRANCH, D), jnp.float32),
        "conv_w": dense(ks[8], (L, CONV_WIDTH, D_CONV), CONV_WIDTH),
        "sgu_ln_g": gain(ks[9], (L, D_SGU)),
        "sgu_ln_b": 0.02 * jax.random.normal(ks[10], (L, D_SGU), jnp.float32),
        "sgu_w": dense(ks[11], (L, SGU_HEADS, SGU_CHUNK, SGU_CHUNK), SGU_CHUNK),
        "sgu_b": gain(ks[12], (L, SGU_HEADS, SGU_CHUNK)),
        "q_norm_g": gain(ks[13], (L, ATTN_HEADS, ATTN_HEAD_DIM)),
        "k_norm_g": gain(ks[14], (L, ATTN_HEADS, ATTN_HEAD_DIM)),
        "w_out_conv": dense(ks[15], (L, D_CONV, D), D_CONV),
        "w_out_sgu": dense(ks[16], (L, D_SGU, D), D_SGU),
        "w_out_attn": dense(ks[17], (L, D_ATTN, D), D_ATTN),
        "w_o": dense(ks[18], (L, D, D), D),
        "ffn2_norm": gain(ks[19], (L, D)),
        "ffn2_w_gu": dense(ks[20], (L, D, 2 * D_FF), D),
        "ffn2_w_down": dense(ks[21], (L, D_FF, D), D_FF),
    }


def _fwd_reference(x, ffn1_norm, ffn1_w_gu, ffn1_w_down, mix_norm, w_in, b_forget, b_gate, conv_w,
              sgu_ln_g, sgu_ln_b, sgu_w, sgu_b, q_norm_g, k_norm_g, w_out_conv, w_out_sgu,
              w_out_attn, w_o, ffn2_norm, ffn2_w_gu, ffn2_w_down):
    bsz, s_len, _ = x.shape
    split_idx = []
    acc = 0
    for sz in SPLIT_SIZES[:-1]:
        acc += sz
        split_idx.append(acc)

    for l in range(DEPTH):
        x = x + 0.5 * swiglu(rms_norm(x, ffn1_norm[l]), ffn1_w_gu[l], ffn1_w_down[l])

        h = rms_norm(x, mix_norm[l])
        a_b, a_c, a_x, s_u, s_v, q, k, v, f_logit, g_logit = jnp.split(h @ w_in[l], split_idx, axis=-1)

        y_a = (a_b * causal_depthwise_conv(a_c * a_x, conv_w[l])) @ w_out_conv[l]

        u = jax.nn.gelu(s_u, approximate=False)
        vv = jax.nn.gelu(s_v, approximate=False)
        y_b = chunked_spatial_gating(u, vv, sgu_ln_g[l], sgu_ln_b[l], sgu_w[l], sgu_b[l]) @ w_out_sgu[l]

        q = rms_norm(q.reshape(bsz, s_len, ATTN_HEADS, ATTN_HEAD_DIM), q_norm_g[l])
        k = rms_norm(k.reshape(bsz, s_len, ATTN_HEADS, ATTN_HEAD_DIM), k_norm_g[l])
        v = v.reshape(bsz, s_len, ATTN_HEADS, ATTN_HEAD_DIM)
        log_f = jax.nn.log_sigmoid(f_logit.astype(jnp.float32) + b_forget[l].astype(jnp.float32))
        y_c = forgetting_attention(q, k, v, log_f).reshape(bsz, s_len, D_ATTN) @ w_out_attn[l]

        gates = jax.nn.sigmoid(g_logit.reshape(bsz, s_len, N_BRANCH, D_MODEL) + b_gate[l])
        merged = gates[:, :, 0] * y_a + gates[:, :, 1] * y_b + gates[:, :, 2] * y_c
        x = x + merged @ w_o[l]

        x = x + 0.5 * swiglu(rms_norm(x, ffn2_norm[l]), ffn2_w_gu[l], ffn2_w_down[l])
    return x


import jax as _jax
import jax.numpy as _jnp

TWIN_FORMAT = 'train_step'
FWD_PARAMS = ['x', 'ffn1_norm', 'ffn1_w_gu', 'ffn1_w_down', 'mix_norm', 'w_in', 'b_forget', 'b_gate', 'conv_w', 'sgu_ln_g', 'sgu_ln_b', 'sgu_w', 'sgu_b', 'q_norm_g', 'k_norm_g', 'w_out_conv', 'w_out_sgu', 'w_out_attn', 'w_o', 'ffn2_norm', 'ffn2_w_gu', 'ffn2_w_down']
TWIN_WEIGHTS = ['ffn1_norm', 'ffn1_w_gu', 'ffn1_w_down', 'mix_norm', 'w_in', 'b_forget', 'b_gate', 'conv_w', 'sgu_ln_g', 'sgu_ln_b', 'sgu_w', 'sgu_b', 'q_norm_g', 'k_norm_g', 'w_out_conv', 'w_out_sgu', 'w_out_attn', 'w_o', 'ffn2_norm', 'ffn2_w_gu', 'ffn2_w_down']
TWIN_DIFF_INPUT = 'x'
TWIN_INPUTS = ['x', 'ffn1_norm', 'ffn1_w_gu', 'ffn1_w_down', 'mix_norm', 'w_in', 'b_forget', 'b_gate', 'conv_w', 'sgu_ln_g', 'sgu_ln_b', 'sgu_w', 'sgu_b', 'q_norm_g', 'k_norm_g', 'w_out_conv', 'w_out_sgu', 'w_out_attn', 'w_o', 'ffn2_norm', 'ffn2_w_gu', 'ffn2_w_down', 'loss_target', 'm_ffn1_norm', 'm_ffn1_w_gu', 'm_ffn1_w_down', 'm_mix_norm', 'm_w_in', 'm_b_forget', 'm_b_gate', 'm_conv_w', 'm_sgu_ln_g', 'm_sgu_ln_b', 'm_sgu_w', 'm_sgu_b', 'm_q_norm_g', 'm_k_norm_g', 'm_w_out_conv', 'm_w_out_sgu', 'm_w_out_attn', 'm_w_o', 'm_ffn2_norm', 'm_ffn2_w_gu', 'm_ffn2_w_down', 'v_ffn1_norm', 'v_ffn1_w_gu', 'v_ffn1_w_down', 'v_mix_norm', 'v_w_in', 'v_b_forget', 'v_b_gate', 'v_conv_w', 'v_sgu_ln_g', 'v_sgu_ln_b', 'v_sgu_w', 'v_sgu_b', 'v_q_norm_g', 'v_k_norm_g', 'v_w_out_conv', 'v_w_out_sgu', 'v_w_out_attn', 'v_w_o', 'v_ffn2_norm', 'v_ffn2_w_gu', 'v_ffn2_w_down']
TWIN_OUTPUTS = ['loss', 'grad_x', 'grad_ffn1_norm', 'grad_ffn1_w_gu', 'grad_ffn1_w_down', 'grad_mix_norm', 'grad_w_in', 'grad_b_forget', 'grad_b_gate', 'grad_conv_w', 'grad_sgu_ln_g', 'grad_sgu_ln_b', 'grad_sgu_w', 'grad_sgu_b', 'grad_q_norm_g', 'grad_k_norm_g', 'grad_w_out_conv', 'grad_w_out_sgu', 'grad_w_out_attn', 'grad_w_o', 'grad_ffn2_norm', 'grad_ffn2_w_gu', 'grad_ffn2_w_down', 'delta_ffn1_norm', 'delta_ffn1_w_gu', 'delta_ffn1_w_down', 'delta_mix_norm', 'delta_w_in', 'delta_b_forget', 'delta_b_gate', 'delta_conv_w', 'delta_sgu_ln_g', 'delta_sgu_ln_b', 'delta_sgu_w', 'delta_sgu_b', 'delta_q_norm_g', 'delta_k_norm_g', 'delta_w_out_conv', 'delta_w_out_sgu', 'delta_w_out_attn', 'delta_w_o', 'delta_ffn2_norm', 'delta_ffn2_w_gu', 'delta_ffn2_w_down', 'new_m_ffn1_norm', 'new_m_ffn1_w_gu', 'new_m_ffn1_w_down', 'new_m_mix_norm', 'new_m_w_in', 'new_m_b_forget', 'new_m_b_gate', 'new_m_conv_w', 'new_m_sgu_ln_g', 'new_m_sgu_ln_b', 'new_m_sgu_w', 'new_m_sgu_b', 'new_m_q_norm_g', 'new_m_k_norm_g', 'new_m_w_out_conv', 'new_m_w_out_sgu', 'new_m_w_out_attn', 'new_m_w_o', 'new_m_ffn2_norm', 'new_m_ffn2_w_gu', 'new_m_ffn2_w_down', 'new_v_ffn1_norm', 'new_v_ffn1_w_gu', 'new_v_ffn1_w_down', 'new_v_mix_norm', 'new_v_w_in', 'new_v_b_forget', 'new_v_b_gate', 'new_v_conv_w', 'new_v_sgu_ln_g', 'new_v_sgu_ln_b', 'new_v_sgu_w', 'new_v_sgu_b', 'new_v_q_norm_g', 'new_v_k_norm_g', 'new_v_w_out_conv', 'new_v_w_out_sgu', 'new_v_w_out_attn', 'new_v_w_o', 'new_v_ffn2_norm', 'new_v_ffn2_w_gu', 'new_v_ffn2_w_down']
TWIN_LEAF_KINDS = {'loss': 'loss', 'grad_x': 'grad_x', 'grad_ffn1_norm': 'grad_w', 'grad_ffn1_w_gu': 'grad_w', 'grad_ffn1_w_down': 'grad_w', 'grad_mix_norm': 'grad_w', 'grad_w_in': 'grad_w', 'grad_b_forget': 'grad_w', 'grad_b_gate': 'grad_w', 'grad_conv_w': 'grad_w', 'grad_sgu_ln_g': 'grad_w', 'grad_sgu_ln_b': 'grad_w', 'grad_sgu_w': 'grad_w', 'grad_sgu_b': 'grad_w', 'grad_q_norm_g': 'grad_w', 'grad_k_norm_g': 'grad_w', 'grad_w_out_conv': 'grad_w', 'grad_w_out_sgu': 'grad_w', 'grad_w_out_attn': 'grad_w', 'grad_w_o': 'grad_w', 'grad_ffn2_norm': 'grad_w', 'grad_ffn2_w_gu': 'grad_w', 'grad_ffn2_w_down': 'grad_w', 'delta_ffn1_norm': 'delta_w', 'delta_ffn1_w_gu': 'delta_w', 'delta_ffn1_w_down': 'delta_w', 'delta_mix_norm': 'delta_w', 'delta_w_in': 'delta_w', 'delta_b_forget': 'delta_w', 'delta_b_gate': 'delta_w', 'delta_conv_w': 'delta_w', 'delta_sgu_ln_g': 'delta_w', 'delta_sgu_ln_b': 'delta_w', 'delta_sgu_w': 'delta_w', 'delta_sgu_b': 'delta_w', 'delta_q_norm_g': 'delta_w', 'delta_k_norm_g': 'delta_w', 'delta_w_out_conv': 'delta_w', 'delta_w_out_sgu': 'delta_w', 'delta_w_out_attn': 'delta_w', 'delta_w_o': 'delta_w', 'delta_ffn2_norm': 'delta_w', 'delta_ffn2_w_gu': 'delta_w', 'delta_ffn2_w_down': 'delta_w', 'new_m_ffn1_norm': 'new_m', 'new_m_ffn1_w_gu': 'new_m', 'new_m_ffn1_w_down': 'new_m', 'new_m_mix_norm': 'new_m', 'new_m_w_in': 'new_m', 'new_m_b_forget': 'new_m', 'new_m_b_gate': 'new_m', 'new_m_conv_w': 'new_m', 'new_m_sgu_ln_g': 'new_m', 'new_m_sgu_ln_b': 'new_m', 'new_m_sgu_w': 'new_m', 'new_m_sgu_b': 'new_m', 'new_m_q_norm_g': 'new_m', 'new_m_k_norm_g': 'new_m', 'new_m_w_out_conv': 'new_m', 'new_m_w_out_sgu': 'new_m', 'new_m_w_out_attn': 'new_m', 'new_m_w_o': 'new_m', 'new_m_ffn2_norm': 'new_m', 'new_m_ffn2_w_gu': 'new_m', 'new_m_ffn2_w_down': 'new_m', 'new_v_ffn1_norm': 'new_v', 'new_v_ffn1_w_gu': 'new_v', 'new_v_ffn1_w_down': 'new_v', 'new_v_mix_norm': 'new_v', 'new_v_w_in': 'new_v', 'new_v_b_forget': 'new_v', 'new_v_b_gate': 'new_v', 'new_v_conv_w': 'new_v', 'new_v_sgu_ln_g': 'new_v', 'new_v_sgu_ln_b': 'new_v', 'new_v_sgu_w': 'new_v', 'new_v_sgu_b': 'new_v', 'new_v_q_norm_g': 'new_v', 'new_v_k_norm_g': 'new_v', 'new_v_w_out_conv': 'new_v', 'new_v_w_out_sgu': 'new_v', 'new_v_w_out_attn': 'new_v', 'new_v_w_o': 'new_v', 'new_v_ffn2_norm': 'new_v', 'new_v_ffn2_w_gu': 'new_v', 'new_v_ffn2_w_down': 'new_v'}


def _forward(args):
    return _fwd_reference(*[args[k] for k in FWD_PARAMS])


def _output_shape():
    out = _jax.eval_shape(lambda: _forward(_fwd_setup_inputs(0)))
    return out.shape, out.dtype

N_MICROBATCH = 1
ADAM_LR = 0.001
ADAM_B1 = 0.9
ADAM_B2 = 0.999
ADAM_EPS = 1e-08
ADAM_WD = 0.01
ADAM_STEP = 10
PER_EXAMPLE_BATCH_AXIS = {'x': 0, 'loss_target': 0}
SHARED_INPUTS = []
_WEIGHT_DTYPES = {'ffn1_norm': _jnp.float32, 'ffn1_w_gu': _jnp.float32, 'ffn1_w_down': _jnp.float32, 'mix_norm': _jnp.float32, 'w_in': _jnp.float32, 'b_forget': _jnp.float32, 'b_gate': _jnp.float32, 'conv_w': _jnp.float32, 'sgu_ln_g': _jnp.float32, 'sgu_ln_b': _jnp.float32, 'sgu_w': _jnp.float32, 'sgu_b': _jnp.float32, 'q_norm_g': _jnp.float32, 'k_norm_g': _jnp.float32, 'w_out_conv': _jnp.float32, 'w_out_sgu': _jnp.float32, 'w_out_attn': _jnp.float32, 'w_o': _jnp.float32, 'ffn2_norm': _jnp.float32, 'ffn2_w_gu': _jnp.float32, 'ffn2_w_down': _jnp.float32}
MOMENT_SCALE = {'ffn1_norm': 3.033505e+00, 'ffn1_w_gu': 1.047840e-01, 'ffn1_w_down': 1.738953e-01, 'mix_norm': 1.835102e+01, 'w_in': 1.762952e-01, 'b_forget': 1.112003e+01, 'b_gate': 1.131747e+00, 'conv_w': 2.643510e+00, 'sgu_ln_g': 1.006321e+00, 'sgu_ln_b': 1.414054e-01, 'sgu_w': 1.327236e-01, 'sgu_b': 1.957917e+00, 'q_norm_g': 1.326776e-01, 'k_norm_g': 1.327655e-01, 'w_out_conv': 2.587799e-01, 'w_out_sgu': 6.780420e-01, 'w_out_attn': 1.279211e-01, 'w_o': 6.101213e-01, 'ffn2_norm': 3.092736e+00, 'ffn2_w_gu': 7.573906e-02, 'ffn2_w_down': 1.256894e-01}


def _to_microbatches(a, axis):
    t = _jnp.moveaxis(a, axis, 0)
    t = t.reshape((N_MICROBATCH, t.shape[0] // N_MICROBATCH) + t.shape[1:])
    return _jnp.moveaxis(t, 1, axis + 1)


def setup_inputs(seed: int = 0) -> dict:
    inp = _fwd_setup_inputs(seed)
    key = _jax.random.fold_in(_jax.random.key(seed), 7919)
    shape, _ = _output_shape()
    out = dict(inp)
    out["loss_target"] = _jax.random.normal(_jax.random.fold_in(key, 0), shape, _jnp.float32)
    for i, name in enumerate(TWIN_WEIGHTS):
        w = inp[name].astype(_jnp.float32)
        if MOMENT_SCALE is None:
            s = _jnp.sqrt(_jnp.mean(_jnp.square(w)) + 1e-30)
        else:
            s = MOMENT_SCALE[name]
        km, kv = _jax.random.split(_jax.random.fold_in(key, i + 1))
        out[name] = w
        out["m_" + name] = s * _jax.random.normal(km, w.shape, _jnp.float32)
        out["v_" + name] = (s * s) * _jax.random.uniform(kv, w.shape, _jnp.float32, 0.5, 1.5)
    if N_MICROBATCH > 1:
        for name, axis in PER_EXAMPLE_BATCH_AXIS.items():
            out[name] = _to_microbatches(out[name], axis)
    return {'x': out['x'], 'ffn1_norm': out['ffn1_norm'], 'ffn1_w_gu': out['ffn1_w_gu'], 'ffn1_w_down': out['ffn1_w_down'], 'mix_norm': out['mix_norm'], 'w_in': out['w_in'], 'b_forget': out['b_forget'], 'b_gate': out['b_gate'], 'conv_w': out['conv_w'], 'sgu_ln_g': out['sgu_ln_g'], 'sgu_ln_b': out['sgu_ln_b'], 'sgu_w': out['sgu_w'], 'sgu_b': out['sgu_b'], 'q_norm_g': out['q_norm_g'], 'k_norm_g': out['k_norm_g'], 'w_out_conv': out['w_out_conv'], 'w_out_sgu': out['w_out_sgu'], 'w_out_attn': out['w_out_attn'], 'w_o': out['w_o'], 'ffn2_norm': out['ffn2_norm'], 'ffn2_w_gu': out['ffn2_w_gu'], 'ffn2_w_down': out['ffn2_w_down'], 'loss_target': out['loss_target'], 'm_ffn1_norm': out['m_ffn1_norm'], 'm_ffn1_w_gu': out['m_ffn1_w_gu'], 'm_ffn1_w_down': out['m_ffn1_w_down'], 'm_mix_norm': out['m_mix_norm'], 'm_w_in': out['m_w_in'], 'm_b_forget': out['m_b_forget'], 'm_b_gate': out['m_b_gate'], 'm_conv_w': out['m_conv_w'], 'm_sgu_ln_g': out['m_sgu_ln_g'], 'm_sgu_ln_b': out['m_sgu_ln_b'], 'm_sgu_w': out['m_sgu_w'], 'm_sgu_b': out['m_sgu_b'], 'm_q_norm_g': out['m_q_norm_g'], 'm_k_norm_g': out['m_k_norm_g'], 'm_w_out_conv': out['m_w_out_conv'], 'm_w_out_sgu': out['m_w_out_sgu'], 'm_w_out_attn': out['m_w_out_attn'], 'm_w_o': out['m_w_o'], 'm_ffn2_norm': out['m_ffn2_norm'], 'm_ffn2_w_gu': out['m_ffn2_w_gu'], 'm_ffn2_w_down': out['m_ffn2_w_down'], 'v_ffn1_norm': out['v_ffn1_norm'], 'v_ffn1_w_gu': out['v_ffn1_w_gu'], 'v_ffn1_w_down': out['v_ffn1_w_down'], 'v_mix_norm': out['v_mix_norm'], 'v_w_in': out['v_w_in'], 'v_b_forget': out['v_b_forget'], 'v_b_gate': out['v_b_gate'], 'v_conv_w': out['v_conv_w'], 'v_sgu_ln_g': out['v_sgu_ln_g'], 'v_sgu_ln_b': out['v_sgu_ln_b'], 'v_sgu_w': out['v_sgu_w'], 'v_sgu_b': out['v_sgu_b'], 'v_q_norm_g': out['v_q_norm_g'], 'v_k_norm_g': out['v_k_norm_g'], 'v_w_out_conv': out['v_w_out_conv'], 'v_w_out_sgu': out['v_w_out_sgu'], 'v_w_out_attn': out['v_w_out_attn'], 'v_w_o': out['v_w_o'], 'v_ffn2_norm': out['v_ffn2_norm'], 'v_ffn2_w_gu': out['v_ffn2_w_gu'], 'v_ffn2_w_down': out['v_ffn2_w_down']}


def _loss(weights, diff, rest, loss_target):
    with _jax.named_scope("forward"):
        args = {**rest, TWIN_DIFF_INPUT: diff, **{k: w.astype(_WEIGHT_DTYPES[k]) for k, w in weights.items()}}
        y = _forward(args)
    with _jax.named_scope("loss_head"):
        err = _jnp.square(y.astype(_jnp.float32) - loss_target)
        return 0.5 * _jnp.sum(_jnp.mean(err, axis=-1)) if err.ndim else 0.5 * err


def _adamw(w, g, m, v):
    m = ADAM_B1 * m + (1.0 - ADAM_B1) * g
    v = ADAM_B2 * v + (1.0 - ADAM_B2) * _jnp.square(g)
    m_hat = m / (1.0 - ADAM_B1 ** ADAM_STEP)
    v_hat = v / (1.0 - ADAM_B2 ** ADAM_STEP)
    delta = -ADAM_LR * (m_hat / (_jnp.sqrt(v_hat) + ADAM_EPS) + ADAM_WD * w)
    return delta, m, v


def reference(x, ffn1_norm, ffn1_w_gu, ffn1_w_down, mix_norm, w_in, b_forget, b_gate, conv_w, sgu_ln_g, sgu_ln_b, sgu_w, sgu_b, q_norm_g, k_norm_g, w_out_conv, w_out_sgu, w_out_attn, w_o, ffn2_norm, ffn2_w_gu, ffn2_w_down, loss_target, m_ffn1_norm, m_ffn1_w_gu, m_ffn1_w_down, m_mix_norm, m_w_in, m_b_forget, m_b_gate, m_conv_w, m_sgu_ln_g, m_sgu_ln_b, m_sgu_w, m_sgu_b, m_q_norm_g, m_k_norm_g, m_w_out_conv, m_w_out_sgu, m_w_out_attn, m_w_o, m_ffn2_norm, m_ffn2_w_gu, m_ffn2_w_down, v_ffn1_norm, v_ffn1_w_gu, v_ffn1_w_down, v_mix_norm, v_w_in, v_b_forget, v_b_gate, v_conv_w, v_sgu_ln_g, v_sgu_ln_b, v_sgu_w, v_sgu_b, v_q_norm_g, v_k_norm_g, v_w_out_conv, v_w_out_sgu, v_w_out_attn, v_w_o, v_ffn2_norm, v_ffn2_w_gu, v_ffn2_w_down):
    given = dict(x=x, ffn1_norm=ffn1_norm, ffn1_w_gu=ffn1_w_gu, ffn1_w_down=ffn1_w_down, mix_norm=mix_norm, w_in=w_in, b_forget=b_forget, b_gate=b_gate, conv_w=conv_w, sgu_ln_g=sgu_ln_g, sgu_ln_b=sgu_ln_b, sgu_w=sgu_w, sgu_b=sgu_b, q_norm_g=q_norm_g, k_norm_g=k_norm_g, w_out_conv=w_out_conv, w_out_sgu=w_out_sgu, w_out_attn=w_out_attn, w_o=w_o, ffn2_norm=ffn2_norm, ffn2_w_gu=ffn2_w_gu, ffn2_w_down=ffn2_w_down, loss_target=loss_target, m_ffn1_norm=m_ffn1_norm, m_ffn1_w_gu=m_ffn1_w_gu, m_ffn1_w_down=m_ffn1_w_down, m_mix_norm=m_mix_norm, m_w_in=m_w_in, m_b_forget=m_b_forget, m_b_gate=m_b_gate, m_conv_w=m_conv_w, m_sgu_ln_g=m_sgu_ln_g, m_sgu_ln_b=m_sgu_ln_b, m_sgu_w=m_sgu_w, m_sgu_b=m_sgu_b, m_q_norm_g=m_q_norm_g, m_k_norm_g=m_k_norm_g, m_w_out_conv=m_w_out_conv, m_w_out_sgu=m_w_out_sgu, m_w_out_attn=m_w_out_attn, m_w_o=m_w_o, m_ffn2_norm=m_ffn2_norm, m_ffn2_w_gu=m_ffn2_w_gu, m_ffn2_w_down=m_ffn2_w_down, v_ffn1_norm=v_ffn1_norm, v_ffn1_w_gu=v_ffn1_w_gu, v_ffn1_w_down=v_ffn1_w_down, v_mix_norm=v_mix_norm, v_w_in=v_w_in, v_b_forget=v_b_forget, v_b_gate=v_b_gate, v_conv_w=v_conv_w, v_sgu_ln_g=v_sgu_ln_g, v_sgu_ln_b=v_sgu_ln_b, v_sgu_w=v_sgu_w, v_sgu_b=v_sgu_b, v_q_norm_g=v_q_norm_g, v_k_norm_g=v_k_norm_g, v_w_out_conv=v_w_out_conv, v_w_out_sgu=v_w_out_sgu, v_w_out_attn=v_w_out_attn, v_w_o=v_w_o, v_ffn2_norm=v_ffn2_norm, v_ffn2_w_gu=v_ffn2_w_gu, v_ffn2_w_down=v_ffn2_w_down)
    weights = {n: given[n] for n in TWIN_WEIGHTS}
    shared = {n: given[n] for n in SHARED_INPUTS}
    per_example = {n: given[n] for n in ['x']}
    grad_fn = _jax.value_and_grad(_loss, argnums=(0, 1))

    def one_microbatch(ex, loss_target):
        ex = dict(ex)
        diff = ex.pop(TWIN_DIFF_INPUT)
        return grad_fn(weights, diff, {**shared, **ex}, loss_target)

    if N_MICROBATCH == 1:
        loss, (grad_w, grad_x) = one_microbatch(per_example, given["loss_target"])
    else:
        def body(carry, xs):
            loss_sum, grad_sum = carry
            l_k, (gw_k, gx_k) = one_microbatch(xs[0], xs[1])
            with _jax.named_scope("update"):
                return (loss_sum + l_k, _jax.tree.map(_jnp.add, grad_sum, gw_k)), gx_k

        init = (_jnp.zeros((), _jnp.float32), _jax.tree.map(_jnp.zeros_like, weights))
        (loss, grad_w), grad_x = _jax.lax.scan(body, init, (per_example, given["loss_target"]))
    with _jax.named_scope("update"):
        delta_w, new_m, new_v = {}, {}, {}
        for n in TWIN_WEIGHTS:
            delta_w[n], new_m[n], new_v[n] = _adamw(weights[n], grad_w[n], given["m_" + n], given["v_" + n])
    return (loss, grad_x, *[grad_w[n] for n in TWIN_WEIGHTS], *[delta_w[n] for n in TWIN_WEIGHTS],
            *[new_m[n] for n in TWIN_WEIGHTS], *[new_v[n] for n in TWIN_WEIGHTS])
```

```python
import functools

import jax
import jax.numpy as jnp
from jax import lax
from jax.experimental import pallas as pl
from jax.experimental.pallas import tpu as pltpu

f32 = jnp.float32
bf16 = jnp.bfloat16
S = jax.ShapeDtypeStruct
MESH = pl.DeviceIdType.MESH

D = 1024
NH = 8
HD = 128
NDEV = 8
GU = 704
NIN = 11272
INB = 1409
GATE_OFF = 8192
F_OFF = 11264
NZ = 11776
RMS_EPS = 1e-6
LN_EPS = 1e-5
ATT_SCALE = HD ** -0.5
NEG = -1e30
INV_SQRT2 = 0.7071067811865476
INV_SQRT2PI = 0.3989422804014327

ADAM_LR = 0.001
ADAM_B1 = 0.9
ADAM_B2 = 0.999
ADAM_EPS = 1e-08
ADAM_WD = 0.01
ADAM_STEP = 10

TT = 256
VMEM_LIMIT = 56 * 1024 * 1024


def _cp(*sem):
    return pltpu.CompilerParams(dimension_semantics=sem, vmem_limit_bytes=VMEM_LIMIT)


def _bs(shape, fn):
    return pl.BlockSpec(shape, fn)


NN = (((1,), (0,)), ((), ()))
NT = (((1,), (1,)), ((), ()))
TN = (((0,), (0,)), ((), ()))


def _mm(name, a, b, *, grid, a_spec, b_spec, out_shape, out_spec, dims, acc_shape, res=None, res_spec=None,
        alpha=1.0, alias=None, split_rows=None):
    nk = grid[2]

    def body(*refs):
        a_ref, b_ref = refs[0], refs[1]
        pos = 2
        res_ref = None
        if res is not None:
            res_ref = refs[pos]
            pos += 1
        if alias is not None:
            pos += 1
        o_ref = refs[pos]
        acc_ref = refs[pos + 1] if nk > 1 else None
        part = lax.dot_general(a_ref[...].astype(bf16), b_ref[...].astype(bf16), dims, preferred_element_type=f32)

        def finish(acc):
            if alpha != 1.0:
                acc = alpha * acc
            if res_ref is not None:
                acc = res_ref[...] + acc
            if split_rows is None:
                o_ref[...] = acc.astype(o_ref.dtype)
            else:
                o_ref[0] = acc[:split_rows].astype(o_ref.dtype)
                o_ref[1] = acc[split_rows:].astype(o_ref.dtype)

        if nk == 1:
            finish(part)
        else:
            k = pl.program_id(2)

            @pl.when(k == 0)
            def _():
                acc_ref[...] = part

            @pl.when(k > 0)
            def _():
                acc_ref[...] += part

            @pl.when(k == nk - 1)
            def _():
                finish(acc_ref[...])

    operands = [a, b]
    in_specs = [a_spec, b_spec]
    if res is not None:
        operands.append(res)
        in_specs.append(res_spec)
    aliases = {}
    if alias is not None:
        aliases = {len(operands): 0}
        operands.append(alias)
        in_specs.append(pl.BlockSpec(memory_space=pl.ANY))
    return pl.pallas_call(
        body, name=name, grid=grid, in_specs=in_specs, out_specs=out_spec, out_shape=out_shape,
        scratch_shapes=[pltpu.VMEM(acc_shape, f32)] if nk > 1 else [],
        input_output_aliases=aliases,
        compiler_params=_cp("parallel", "parallel", "arbitrary"),
    )(*operands)


def _tile(n, t):
    return t if n % t == 0 and n >= t else n


def _row(cb=0, w=D):
    return _bs((TT, w), lambda i: (i, cb))


def _vec(rows=1, w=D):
    return _bs((rows, w), lambda i: (0, 0))


def _acc_store(i, ref, val):
    @pl.when(i == 0)
    def _():
        ref[...] = val

    @pl.when(i > 0)
    def _():
        ref[...] += val


def _rms_fwd(name, x, g):
    T = x.shape[0]

    def body(x_ref, g_ref, o_ref):
        xv = x_ref[...]
        r = lax.rsqrt(jnp.mean(xv * xv, axis=-1, keepdims=True) + RMS_EPS)
        o_ref[...] = (xv * r * g_ref[...]).astype(bf16)

    return pl.pallas_call(body, name=name, grid=(T // TT,), in_specs=[_row(), _vec()], out_specs=_row(),
                          out_shape=S((T, D), bf16), compiler_params=_cp("parallel"))(x, g)


def _rms_bwd(name, dh, x, g, dres):
    T = x.shape[0]

    def body(dh_ref, x_ref, g_ref, dres_ref, dx_ref, dg_ref):
        i = pl.program_id(0)
        xv = x_ref[...]
        r = lax.rsqrt(jnp.mean(xv * xv, axis=-1, keepdims=True) + RMS_EPS)
        xhat = xv * r
        dh_v = dh_ref[...]
        dyg = dh_v * g_ref[...]
        m = jnp.mean(dyg * xhat, axis=-1, keepdims=True)
        dx_ref[...] = dres_ref[...] + r * (dyg - xhat * m)
        _acc_store(i, dg_ref, jnp.sum(dh_v * xhat, axis=0, keepdims=True))

    return pl.pallas_call(body, name=name, grid=(T // TT,), in_specs=[_row(), _row(), _vec(), _row()],
                          out_specs=[_row(), _vec()], out_shape=[S((T, D), f32), S((1, D), f32)],
                          compiler_params=_cp("arbitrary"))(dh, x, g, dres)


def _sigmoid(x):
    return 1.0 / (1.0 + jnp.exp(-x))


def _swiglu_fwd(name, gu):
    T = gu.shape[1]

    def body(g_ref, u_ref, o_ref):
        g = g_ref[...]
        o_ref[...] = (g * _sigmoid(g) * u_ref[...]).astype(bf16)

    return pl.pallas_call(
        body, name=name, grid=(4, T // TT),
        in_specs=[_bs((None, TT, GU), lambda j, i: (j, i, 0)), _bs((None, TT, GU), lambda j, i: (j + 4, i, 0))],
        out_specs=_bs((None, TT, GU), lambda j, i: (j, i, 0)), out_shape=S((4, T, GU), bf16),
        compiler_params=_cp("parallel", "parallel"))(gu, gu)


def _swiglu_bwd(name, gu, da):
    T = gu.shape[1]

    def body(g_ref, u_ref, da_ref, o_ref):
        g = g_ref[...]
        u = u_ref[...]
        da_v = da_ref[...]
        sg = _sigmoid(g)
        o_ref[0] = (da_v * u * (sg + g * sg * (1.0 - sg))).astype(bf16)
        o_ref[1] = (da_v * g * sg).astype(bf16)

    return pl.pallas_call(
        body, name=name, grid=(4, T // TT),
        in_specs=[_bs((None, TT, GU), lambda j, i: (j, i, 0)), _bs((None, TT, GU), lambda j, i: (j + 4, i, 0)),
                  _bs((None, TT, GU), lambda j, i: (j, i, 0))],
        out_specs=_bs((2, None, TT, GU), lambda j, i: (0, j, i, 0)), out_shape=S((2, 4, T, GU), bf16),
        compiler_params=_cp("parallel", "parallel"))(gu, gu, da)


def _loss(name, y, tgt):
    T = y.shape[0]

    def body(y_ref, t_ref, l_ref, dy_ref):
        i = pl.program_id(0)
        e = y_ref[...] - t_ref[...]
        dy_ref[...] = e * (1.0 / D)
        s = 0.5 * jnp.sum(jnp.mean(e * e, axis=-1, keepdims=True))
        _acc_store(i, l_ref, jnp.broadcast_to(s, (1, 128)))

    return pl.pallas_call(body, name=name, grid=(T // TT,), in_specs=[_row(), _row()],
                          out_specs=[_vec(1, 128), _row()], out_shape=[S((1, 128), f32), S((T, D), f32)],
                          compiler_params=_cp("arbitrary"))(y, tgt)


def _prev8(T, cb):
    return _bs((8, D), lambda i: (jnp.maximum(i * (TT // 8) - 1, 0), cb))


def _next8(T, cb):
    return _bs((8, D), lambda i: (jnp.minimum((i + 1) * (TT // 8), T // 8 - 1), cb))


def _conv_taps(i, ac_ref, ax_ref, pc_ref, px_ref):
    ca = ac_ref[...] * ax_ref[...]
    keep = (i > 0).astype(f32)
    p1 = pc_ref[7:8, :] * px_ref[7:8, :] * keep
    p2 = pc_ref[6:7, :] * px_ref[6:7, :] * keep
    row = lax.broadcasted_iota(jnp.int32, ca.shape, 0)
    s1 = jnp.where(row == 0, p1, pltpu.roll(ca, 1, 0))
    s2 = jnp.where(row == 0, p2, jnp.where(row == 1, p1, pltpu.roll(ca, 2, 0)))
    return ca, s1, s2


def _conv_fwd(name, z, cw):
    T = z.shape[0]

    def body(ab_ref, ac_ref, ax_ref, pc_ref, px_ref, w_ref, o_ref):
        i = pl.program_id(0)
        ca, s1, s2 = _conv_taps(i, ac_ref, ax_ref, pc_ref, px_ref)
        cv = w_ref[0:1, :] * s2 + w_ref[1:2, :] * s1 + w_ref[2:3, :] * ca
        o_ref[...] = (ab_ref[...] * cv).astype(bf16)

    return pl.pallas_call(
        body, name=name, grid=(T // TT,),
        in_specs=[_row(0), _row(1), _row(2), _prev8(T, 1), _prev8(T, 2), _vec(3)],
        out_specs=_row(), out_shape=S((T, D), bf16), compiler_params=_cp("parallel"))(z, z, z, z, z, cw)


def _conv_bwd(name, dya, z, cw):
    T = z.shape[0]
    n = T // TT

    def body(dya_ref, ab_ref, ac_ref, ax_ref, pc_ref, px_ref, ndya_ref, nab_ref, w_ref, dz_ref, dw_ref):
        i = pl.program_id(0)
        ca, s1, s2 = _conv_taps(i, ac_ref, ax_ref, pc_ref, px_ref)
        w0, w1, w2 = w_ref[0:1, :], w_ref[1:2, :], w_ref[2:3, :]
        cv = w0 * s2 + w1 * s1 + w2 * ca
        dya_v = dya_ref[...]
        ab = ab_ref[...]
        dcv = dya_v * ab
        keep = (i < n - 1).astype(f32)
        n1 = ndya_ref[0:1, :] * nab_ref[0:1, :] * keep
        n2 = ndya_ref[1:2, :] * nab_ref[1:2, :] * keep
        row = lax.broadcasted_iota(jnp.int32, dcv.shape, 0)
        f1 = jnp.where(row == TT - 1, n1, pltpu.roll(dcv, TT - 1, 0))
        f2 = jnp.where(row == TT - 1, n2, jnp.where(row == TT - 2, n1, pltpu.roll(dcv, TT - 2, 0)))
        dca = w2 * dcv + w1 * f1 + w0 * f2
        dz_ref[:, 0:D] = (dya_v * cv).astype(bf16)
        dz_ref[:, D:2 * D] = (dca * ax_ref[...]).astype(bf16)
        dz_ref[:, 2 * D:3 * D] = (dca * ac_ref[...]).astype(bf16)
        dw = jnp.concatenate([jnp.sum(dcv * s2, axis=0, keepdims=True), jnp.sum(dcv * s1, axis=0, keepdims=True),
                              jnp.sum(dcv * ca, axis=0, keepdims=True)], axis=0)
        _acc_store(i, dw_ref, dw)

    return pl.pallas_call(
        body, name=name, grid=(n,),
        in_specs=[_row(), _row(0), _row(1), _row(2), _prev8(T, 1), _prev8(T, 2), _next8(T, 0), _next8(T, 0), _vec(3)],
        out_specs=[_row(0, 3 * D), _vec(3)], out_shape=[S((T, 3 * D), bf16), S((3, D), f32)],
        compiler_params=_cp("arbitrary"))(dya, z, z, z, z, z, dya, z, cw)


def _gelu(x):
    return 0.5 * x * (1.0 + lax.erf(x * INV_SQRT2))


def _gelu_grad(x):
    return 0.5 * (1.0 + lax.erf(x * INV_SQRT2)) + x * jnp.exp(-0.5 * x * x) * INV_SQRT2PI


def _ln_stats(vv):
    mu = jnp.mean(vv, axis=-1, keepdims=True)
    xc = vv - mu
    rstd = lax.rsqrt(jnp.mean(xc * xc, axis=-1, keepdims=True) + LN_EPS)
    return xc * rstd, rstd


def _tril_w(w_ref, g):
    r = lax.broadcasted_iota(jnp.int32, (HD, HD), 0)
    c = lax.broadcasted_iota(jnp.int32, (HD, HD), 1)
    return jnp.where(c <= r, w_ref[g], 0.0).astype(bf16)


def _sgu_fwd(name, z, ln_g, ln_b, w_s, bmap):
    T = z.shape[0]

    def body(su_ref, sv_ref, lg_ref, lb_ref, w_ref, bm_ref, o_ref, vn_ref):
        xhat, _ = _ln_stats(_gelu(sv_ref[...]))
        vn_ref[...] = (xhat * lg_ref[...] + lb_ref[...]).astype(bf16)
        for g in range(NH):
            w = _tril_w(w_ref, g)
            cs = slice(g * HD, (g + 1) * HD)
            for c in range(TT // HD):
                rs = slice(c * HD, (c + 1) * HD)
                s = jnp.dot(w, vn_ref[rs, cs], preferred_element_type=f32) + bm_ref[:, cs]
                o_ref[rs, cs] = (_gelu(su_ref[rs, cs]) * s).astype(bf16)

    return pl.pallas_call(
        body, name=name, grid=(T // TT,),
        in_specs=[_row(3), _row(4), _vec(), _vec(), _bs((NH, HD, HD), lambda i: (0, 0, 0)), _vec(HD)],
        out_specs=_row(), out_shape=S((T, D), bf16), scratch_shapes=[pltpu.VMEM((TT, D), bf16)],
        compiler_params=_cp("parallel"))(z, z, ln_g, ln_b, w_s, bmap)


def _sgu_bwd(name, dyb, z, ln_g, ln_b, w_s, bmap):
    T = z.shape[0]

    def body(dyb_ref, su_ref, sv_ref, lg_ref, lb_ref, w_ref, bm_ref, dz_ref, dlg_ref, dlb_ref, dw_ref, db_ref,
             vn_ref, du_ref, dvn_ref):
        i = pl.program_id(0)
        sv = sv_ref[...]
        xhat, rstd = _ln_stats(_gelu(sv))
        vn_ref[...] = (xhat * lg_ref[...] + lb_ref[...]).astype(bf16)
        r = lax.broadcasted_iota(jnp.int32, (HD, HD), 0)
        cc = lax.broadcasted_iota(jnp.int32, (HD, HD), 1)
        for g in range(NH):
            w = _tril_w(w_ref, g)
            cs = slice(g * HD, (g + 1) * HD)
            dw = jnp.zeros((HD, HD), f32)
            db = jnp.zeros((HD, 1), f32)
            for c in range(TT // HD):
                rs = slice(c * HD, (c + 1) * HD)
                vnb = vn_ref[rs, cs]
                s = jnp.dot(w, vnb, preferred_element_type=f32) + bm_ref[:, cs]
                dy = dyb_ref[rs, cs]
                du_ref[rs, cs] = dy * s
                ds = dy * _gelu(su_ref[rs, cs])
                ds16 = ds.astype(bf16)
                dvn_ref[rs, cs] = lax.dot_general(w, ds16, TN, preferred_element_type=f32)
                dw = dw + lax.dot_general(ds16, vnb, NT, preferred_element_type=f32)
                db = db + jnp.sum(ds, axis=1, keepdims=True)
            dw = jnp.where(cc <= r, dw, 0.0)

            @pl.when(i == 0)
            def _():
                dw_ref[g] = dw
                db_ref[:, g:g + 1] = db

            @pl.when(i > 0)
            def _():
                dw_ref[g] += dw
                db_ref[:, g:g + 1] += db

        dvn = dvn_ref[...]
        dxh = dvn * lg_ref[...]
        m1 = jnp.mean(dxh, axis=-1, keepdims=True)
        m2 = jnp.mean(dxh * xhat, axis=-1, keepdims=True)
        dvv = rstd * (dxh - m1 - xhat * m2)
        dz_ref[:, 0:D] = (du_ref[...] * _gelu_grad(su_ref[...])).astype(bf16)
        dz_ref[:, D:2 * D] = (dvv * _gelu_grad(sv)).astype(bf16)
        _acc_store(i, dlg_ref, jnp.sum(dvn * xhat, axis=0, keepdims=True))
        _acc_store(i, dlb_ref, jnp.sum(dvn, axis=0, keepdims=True))

    return pl.pallas_call(
        body, name=name, grid=(T // TT,),
        in_specs=[_row(), _row(3), _row(4), _vec(), _vec(), _bs((NH, HD, HD), lambda i: (0, 0, 0)), _vec(HD)],
        out_specs=[_row(0, 2 * D), _vec(), _vec(), _bs((NH, HD, HD), lambda i: (0, 0, 0)), _bs((HD, NH), lambda i: (0, 0))],
        out_shape=[S((T, 2 * D), bf16), S((1, D), f32), S((1, D), f32), S((NH, HD, HD), f32), S((HD, NH), f32)],
        scratch_shapes=[pltpu.VMEM((TT, D), bf16), pltpu.VMEM((TT, D), f32), pltpu.VMEM((TT, D), f32)],
        compiler_params=_cp("arbitrary"))(dyb, z, z, ln_g, ln_b, w_s, bmap)


def _qk_fwd(name, z, qg, kg, bf):
    T = z.shape[0]

    def body(q_ref, k_ref, v_ref, zf_ref, qg_ref, kg_ref, bf_ref, qn_ref, kn_ref, vb_ref, lf_ref):
        for h in range(NH):
            cs = slice(h * HD, (h + 1) * HD)
            for src, gain, dst in ((q_ref, qg_ref, qn_ref), (k_ref, kg_ref, kn_ref)):
                xv = src[:, cs]
                r = lax.rsqrt(jnp.mean(xv * xv, axis=-1, keepdims=True) + RMS_EPS)
                dst[:, cs] = (xv * r * gain[:, cs]).astype(bf16)
        vb_ref[...] = v_ref[...].astype(bf16)
        xf = zf_ref[...] + bf_ref[...]
        lf_ref[...] = jnp.minimum(xf, 0.0) - jnp.log1p(jnp.exp(-jnp.abs(xf)))

    return pl.pallas_call(
        body, name=name, grid=(T // TT,),
        in_specs=[_row(5), _row(6), _row(7), _bs((TT, 128), lambda i: (i, F_OFF // 128)), _vec(), _vec(), _vec(1, 128)],
        out_specs=[_row(), _row(), _row(), _bs((TT, 128), lambda i: (i, 0))],
        out_shape=[S((T, D), bf16), S((T, D), bf16), S((T, D), bf16), S((T, 128), f32)],
        compiler_params=_cp("parallel"))(z, z, z, z, qg, kg, bf)


def _cum_fwd(name, logf):
    T = logf.shape[0]

    def body(lf_ref, ccol_ref, crow_ref, c_ref):
        c = lf_ref[...]
        row = lax.broadcasted_iota(jnp.int32, c.shape, 0)
        s = 1
        while s < T:
            c = c + jnp.where(row >= s, pltpu.roll(c, s, 0), 0.0)
            s *= 2
        c_ref[...] = c
        crow_ref[...] = c.T[0:NH, :]
        for h in range(NH):
            ccol_ref[h] = jnp.broadcast_to(c_ref[:, h:h + 1], (T, 128))

    return pl.pallas_call(body, name=name, out_shape=[S((NH, T, 128), f32), S((NH, T), f32)],
                          scratch_shapes=[pltpu.VMEM((T, 128), f32)],
                          compiler_params=pltpu.CompilerParams(vmem_limit_bytes=VMEM_LIMIT))(logf)


def _attn_fwd(name, qn, kn, vb, ccol, crow3):
    T = qn.shape[0]
    tq = _tile(T, 256)
    nq = T // tq

    def body(q_ref, k_ref, v_ref, cc_ref, cr_ref, o_ref, lse_ref):
        qi = pl.program_id(1)
        q = q_ref[...]
        cq = cc_ref[:, 0:1]
        rows = qi * tq + lax.broadcasted_iota(jnp.int32, (tq, tq), 0)
        lane = lax.broadcasted_iota(jnp.int32, (tq, tq), 1)

        def step(j, carry):
            m, l, acc = carry
            off = pl.multiple_of(j * tq, tq)
            s = lax.dot_general(q, k_ref[pl.ds(off, tq), :], NT, preferred_element_type=f32) * ATT_SCALE
            s = s + cq - cr_ref[:, pl.ds(off, tq)]
            s = jnp.where(off + lane <= rows, s, NEG)
            mn = jnp.maximum(m, jnp.max(s, axis=1, keepdims=True))
            p = jnp.exp(s - mn)
            al = jnp.exp(m - mn)
            l = al * l + jnp.sum(p, axis=1, keepdims=True)
            acc = al * acc + jnp.dot(p.astype(bf16), v_ref[pl.ds(off, tq), :], preferred_element_type=f32)
            return mn, l, acc

        init = (jnp.full((tq, 1), NEG, f32), jnp.zeros((tq, 1), f32), jnp.zeros((tq, HD), f32))
        m, l, acc = lax.fori_loop(0, qi + 1, step, init)
        o_ref[...] = acc / l
        lse_ref[...] = jnp.broadcast_to(m + jnp.log(l), (tq, 128))

    return pl.pallas_call(
        body, name=name, grid=(NH, nq),
        in_specs=[_bs((tq, HD), lambda h, i: (i, h)), _bs((T, HD), lambda h, i: (0, h)), _bs((T, HD), lambda h, i: (0, h)),
                  _bs((None, tq, 128), lambda h, i: (h, i, 0)), _bs((None, 1, T), lambda h, i: (h, 0, 0))],
        out_specs=[_bs((tq, HD), lambda h, i: (i, h)), _bs((None, tq, 128), lambda h, i: (h, i, 0))],
        out_shape=[S((T, D), f32), S((NH, T, 128), f32)],
        compiler_params=_cp("parallel", "parallel"))(qn, kn, vb, ccol, crow3)


def _attn_bwd(name, qn, kn, vb, do, o, lse, ccol, crow3):
    T = qn.shape[0]
    tq = _tile(T, 256)
    nq = T // tq

    def body(q_ref, k_ref, v_ref, do_ref, o_ref, lse_ref, cc_ref, cr_ref, dq_ref, dk_ref, dv_ref, rs_ref, cs_ref):
        kj = pl.program_id(1)

        @pl.when(kj == 0)
        def _():
            dq_ref[...] = jnp.zeros_like(dq_ref)
            rs_ref[...] = jnp.zeros_like(rs_ref)

        kb = k_ref[...]
        vv = v_ref[...]
        crj = cr_ref[...]
        cols = kj * tq + lax.broadcasted_iota(jnp.int32, (tq, tq), 1)
        sub = lax.broadcasted_iota(jnp.int32, (tq, tq), 0)

        def step(i, carry):
            dk, dv, cs = carry
            off = pl.multiple_of(i * tq, tq)
            rows = pl.ds(off, tq)
            qb = q_ref[rows, :]
            dob = do_ref[rows, :]
            delta = jnp.sum(dob * o_ref[rows, :], axis=1, keepdims=True)
            s = lax.dot_general(qb, kb, NT, preferred_element_type=f32) * ATT_SCALE
            s = s + cc_ref[rows, 0:1] - crj
            p = jnp.where(cols <= off + sub, jnp.exp(s - lse_ref[rows, 0:1]), 0.0)
            do16 = dob.astype(bf16)
            dv = dv + lax.dot_general(p.astype(bf16), do16, TN, preferred_element_type=f32)
            dp = lax.dot_general(do16, vv, NT, preferred_element_type=f32)
            ds = p * (dp - delta)
            ds16 = ds.astype(bf16)
            dk = dk + lax.dot_general(ds16, qb, TN, preferred_element_type=f32)
            dq_ref[rows, :] += jnp.dot(ds16, kb, preferred_element_type=f32) * ATT_SCALE
            rs_ref[rows, :] += jnp.broadcast_to(jnp.sum(ds, axis=1, keepdims=True), (tq, 128))
            cs = cs + jnp.sum(ds, axis=0, keepdims=True)
            return dk, dv, cs

        init = (jnp.zeros((tq, HD), f32), jnp.zeros((tq, HD), f32), jnp.zeros((1, tq), f32))
        dk, dv, cs = lax.fori_loop(kj, nq, step, init)
        dk_ref[...] = dk * ATT_SCALE
        dv_ref[...] = dv
        cs_ref[...] = cs

    full = lambda h, j: (0, h)
    full3 = lambda h, j: (h, 0, 0)
    blk = lambda h, j: (j, h)
    return pl.pallas_call(
        body, name=name, grid=(NH, nq),
        in_specs=[_bs((T, HD), full), _bs((tq, HD), blk), _bs((tq, HD), blk), _bs((T, HD), full), _bs((T, HD), full),
                  _bs((None, T, 128), full3), _bs((None, T, 128), full3), _bs((None, 1, tq), lambda h, j: (h, 0, j))],
        out_specs=[_bs((T, HD), full), _bs((tq, HD), blk), _bs((tq, HD), blk), _bs((None, T, 128), full3),
                   _bs((None, 1, tq), lambda h, j: (h, 0, j))],
        out_shape=[S((T, D), f32), S((T, D), f32), S((T, D), f32), S((NH, T, 128), f32), S((NH, 1, T), f32)],
        compiler_params=_cp("parallel", "arbitrary"))(qn, kn, vb, do, o, lse, ccol, crow3)


def _forget_bwd(name, rs, cs, z, bf):
    T = rs.shape[0]

    def body(rs_ref, cs_ref, zf_ref, bf_ref, dz_ref, db_ref):
        c = rs_ref[...] - cs_ref[...]
        row = lax.broadcasted_iota(jnp.int32, c.shape, 0)
        s = 1
        while s < T:
            c = c + jnp.where(row + s < T, pltpu.roll(c, T - s, 0), 0.0)
            s *= 2
        xf = zf_ref[...] + bf_ref[...]
        lane = lax.broadcasted_iota(jnp.int32, c.shape, 1)
        dxf = jnp.where(lane < NH, c / (1.0 + jnp.exp(xf)), 0.0)
        dz_ref[...] = jnp.zeros_like(dz_ref)
        dz_ref[:, 0:128] = dxf.astype(bf16)
        db_ref[...] = jnp.sum(dxf, axis=0, keepdims=True)

    return pl.pallas_call(
        body, name=name, grid=(1,),
        in_specs=[_bs((T, 128), lambda i: (0, 0)), _bs((T, 128), lambda i: (0, 0)), _bs((T, 128), lambda i: (0, F_OFF // 128)),
                  _vec(1, 128)],
        out_specs=[_bs((T, NZ - F_OFF), lambda i: (0, 0)), _vec(1, 128)],
        out_shape=[S((T, NZ - F_OFF), bf16), S((1, 128), f32)], compiler_params=_cp("arbitrary"))(rs, cs, z, bf)


def _qk_bwd(name, dqn, dkn, dv, z, qg, kg):
    T = z.shape[0]

    def body(dq_ref, dk_ref, dv_ref, q_ref, k_ref, qg_ref, kg_ref, dz_ref, dqg_ref, dkg_ref, g_ref):
        i = pl.program_id(0)
        for n, (src, dsrc, gain, dgain) in enumerate(((q_ref, dq_ref, qg_ref, dqg_ref), (k_ref, dk_ref, kg_ref, dkg_ref))):
            for h in range(NH):
                cs = slice(h * HD, (h + 1) * HD)
                xv = src[:, cs]
                r = lax.rsqrt(jnp.mean(xv * xv, axis=-1, keepdims=True) + RMS_EPS)
                xhat = xv * r
                dy = dsrc[:, cs]
                dyg = dy * gain[:, cs]
                m = jnp.mean(dyg * xhat, axis=-1, keepdims=True)
                dz_ref[:, n * D + h * HD:n * D + (h + 1) * HD] = (r * (dyg - xhat * m)).astype(bf16)
                g_ref[:, cs] = jnp.sum(dy * xhat, axis=0, keepdims=True)
            _acc_store(i, dgain, g_ref[...])
        dz_ref[:, 2 * D:3 * D] = dv_ref[...].astype(bf16)

    return pl.pallas_call(
        body, name=name, grid=(T // TT,),
        in_specs=[_row(), _row(), _row(), _row(5), _row(6), _vec(), _vec()],
        out_specs=[_row(0, 3 * D), _vec(), _vec()], out_shape=[S((T, 3 * D), bf16), S((1, D), f32), S((1, D), f32)],
        scratch_shapes=[pltpu.VMEM((1, D), f32)], compiler_params=_cp("arbitrary"))(dqn, dkn, dv, z, z, qg, kg)


GB = GATE_OFF // D


def _merge_fwd(name, ya, yb, yc, z, bg):
    T = z.shape[0]

    def body(ya_ref, yb_ref, yc_ref, g0_ref, g1_ref, g2_ref, bg_ref, o_ref):
        acc = _sigmoid(g0_ref[...] + bg_ref[0:1, :]) * ya_ref[...]
        acc = acc + _sigmoid(g1_ref[...] + bg_ref[1:2, :]) * yb_ref[...]
        acc = acc + _sigmoid(g2_ref[...] + bg_ref[2:3, :]) * yc_ref[...]
        o_ref[...] = acc.astype(bf16)

    return pl.pallas_call(
        body, name=name, grid=(T // TT,),
        in_specs=[_row(), _row(), _row(), _row(GB), _row(GB + 1), _row(GB + 2), _vec(3)],
        out_specs=_row(), out_shape=S((T, D), bf16), compiler_params=_cp("parallel"))(ya, yb, yc, z, z, z, bg)


def _merge_bwd(name, dm, ya, yb, yc, z, bg):
    T = z.shape[0]

    def body(dm_ref, ya_ref, yb_ref, yc_ref, g0_ref, g1_ref, g2_ref, bg_ref, dya_ref, dyb_ref, dyc_ref, dz_ref, db_ref):
        i = pl.program_id(0)
        dm_v = dm_ref[...]
        dbs = []
        for n, (y_ref, g_ref, dy_ref) in enumerate(((ya_ref, g0_ref, dya_ref), (yb_ref, g1_ref, dyb_ref),
                                                    (yc_ref, g2_ref, dyc_ref))):
            gate = _sigmoid(g_ref[...] + bg_ref[n:n + 1, :])
            dy_ref[...] = (dm_v * gate).astype(bf16)
            dl = dm_v * y_ref[...] * gate * (1.0 - gate)
            dz_ref[:, n * D:(n + 1) * D] = dl.astype(bf16)
            dbs.append(jnp.sum(dl, axis=0, keepdims=True))
        _acc_store(i, db_ref, jnp.concatenate(dbs, axis=0))

    return pl.pallas_call(
        body, name=name, grid=(T // TT,),
        in_specs=[_row(), _row(), _row(), _row(), _row(GB), _row(GB + 1), _row(GB + 2), _vec(3)],
        out_specs=[_row(), _row(), _row(), _row(0, 3 * D), _vec(3)],
        out_shape=[S((T, D), bf16), S((T, D), bf16), S((T, D), bf16), S((T, 3 * D), bf16), S((3, D), f32)],
        compiler_params=_cp("arbitrary"))(dm, ya, yb, yc, z, z, z, bg)


SMALL_NAMES = ("ffn1_norm", "mix_norm", "b_forget", "b_gate", "conv_w", "sgu_ln_g", "sgu_ln_b", "sgu_w", "sgu_b",
               "q_norm_g", "k_norm_g", "ffn2_norm")


def _small_params(p):
    out = {n: p[n].reshape(1, D) for n in ("ffn1_norm", "mix_norm", "ffn2_norm", "sgu_ln_g", "sgu_ln_b", "q_norm_g", "k_norm_g")}
    out["b_forget"] = jnp.pad(p["b_forget"].reshape(1, NH), ((0, 0), (0, 128 - NH)))
    out["b_gate"] = p["b_gate"]
    out["conv_w"] = p["conv_w"]
    out["sgu_w"] = p["sgu_w"]
    out["bmap"] = jnp.repeat(p["sgu_b"].T, HD, axis=1)
    return out


def _small_grads_natural(sg):
    out = {n: sg[n].reshape(D) for n in ("ffn1_norm", "mix_norm", "ffn2_norm", "sgu_ln_g", "sgu_ln_b")}
    out["q_norm_g"] = sg["q_norm_g"].reshape(NH, HD)
    out["k_norm_g"] = sg["k_norm_g"].reshape(NH, HD)
    out["b_forget"] = sg["b_forget"][0, :NH]
    out["b_gate"] = sg["b_gate"]
    out["conv_w"] = sg["conv_w"]
    out["sgu_w"] = sg["sgu_w"]
    out["sgu_b"] = sg["sgu_b"]
    return out


def _sq_fwd(name, a, wsq, l, n, res=None):
    T = a.shape[0]
    tm = _tile(T, 512)
    return _mm(name, a, wsq, grid=(T // tm, 1, 1), a_spec=_bs((tm, D), lambda i, j, k: (i, 0)),
               b_spec=_bs((None, None, D, D), lambda i, j, k: (l, n, 0, 0)),
               out_shape=S((T, D), f32), out_spec=_bs((tm, D), lambda i, j, k: (i, 0)), dims=NN, acc_shape=None,
               res=res, res_spec=_bs((tm, D), lambda i, j, k: (i, 0)))


def _sq_bwd_in(name, dy, wsq, l, n):
    T = dy.shape[0]
    tm = _tile(T, 512)
    return _mm(name, dy, wsq, grid=(T // tm, 1, 1), a_spec=_bs((tm, D), lambda i, j, k: (i, 0)),
               b_spec=_bs((None, None, D, D), lambda i, j, k: (l, n, 0, 0)),
               out_shape=S((T, D), f32), out_spec=_bs((tm, D), lambda i, j, k: (i, 0)), dims=NT, acc_shape=None)


def _sq_bwd_w(name, a, dy, gbuf, l):
    T = a.shape[0]
    return _mm(name, a, dy, grid=(NDEV, 1, 1), a_spec=_bs((T, 128), lambda i, j, k: (0, i)),
               b_spec=_bs((T, D), lambda i, j, k: (0, 0)), out_shape=S(gbuf.shape, bf16),
               out_spec=_bs((None, None, None, 128, D), lambda i, j, k: (i % 2, i // 2, l, 0, 0)),
               dims=TN, acc_shape=None, alias=gbuf)


def _ffn_fwd(tag, x, g, wgu, wd, l):
    T = x.shape[0]
    tm = _tile(T, 1024)
    h = _rms_fwd(tag + "_rms", x, g)
    gu = _mm(tag + "_gu", h, wgu, grid=(T // tm, NDEV, 1), a_spec=_bs((tm, D), lambda i, j, k: (i, 0)),
             b_spec=_bs((None, None, D, GU), lambda i, j, k: (l, j, 0, 0)), out_shape=S((NDEV, T, GU), f32),
             out_spec=_bs((None, tm, GU), lambda i, j, k: (j, i, 0)), dims=NN, acc_shape=None)
    a = _swiglu_fwd(tag + "_act", gu)
    tm = _tile(T, 512)
    xo = _mm(tag + "_down", a, wd, grid=(T // tm, 1, 4), a_spec=_bs((None, tm, GU), lambda i, j, k: (k, i, 0)),
             b_spec=_bs((None, None, GU, D), lambda i, j, k: (l, k, 0, 0)), out_shape=S((T, D), f32),
             out_spec=_bs((tm, D), lambda i, j, k: (i, 0)), dims=NN, acc_shape=(tm, D), res=x,
             res_spec=_bs((tm, D), lambda i, j, k: (i, 0)), alpha=0.5)
    return xo, (h, gu, a)


def _ffn_bwd(tag, dxo, x, g, wgu, wd, l, saved, g_gu, g_d):
    h, gu, a = saved
    T = x.shape[0]
    tm = _tile(T, 512)
    da = _mm(tag + "_dact", dxo, wd, grid=(T // tm, 4, 1), a_spec=_bs((tm, D), lambda i, j, k: (i, 0)),
             b_spec=_bs((None, None, GU, D), lambda i, j, k: (l, j, 0, 0)), out_shape=S((4, T, GU), f32),
             out_spec=_bs((None, tm, GU), lambda i, j, k: (j, i, 0)), dims=NT, acc_shape=None, alpha=0.5)
    g_d = _mm(tag + "_dwd", a, dxo, grid=(4, 1, 1), a_spec=_bs((None, T, GU), lambda i, j, k: (i, 0, 0)),
              b_spec=_bs((T, D), lambda i, j, k: (0, 0)), out_shape=S(g_d.shape, bf16),
              out_spec=_bs((2, None, None, GU // 2, D), lambda i, j, k: (0, i, l, 0, 0)), dims=TN, acc_shape=None,
              alpha=0.5, alias=g_d, split_rows=GU // 2)
    dgu = _swiglu_bwd(tag + "_dgu", gu, da).reshape(NDEV, T, GU)
    dh = _mm(tag + "_dh", dgu, wgu, grid=(T // tm, 1, NDEV), a_spec=_bs((None, tm, GU), lambda i, j, k: (k, i, 0)),
             b_spec=_bs((None, None, D, GU), lambda i, j, k: (l, k, 0, 0)), out_shape=S((T, D), f32),
             out_spec=_bs((tm, D), lambda i, j, k: (i, 0)), dims=NT, acc_shape=(tm, D))
    g_gu = _mm(tag + "_dwgu", h, dgu, grid=(1, NDEV, 1), a_spec=_bs((T, D), lambda i, j, k: (0, 0)),
               b_spec=_bs((None, T, GU), lambda i, j, k: (j, 0, 0)), out_shape=S(g_gu.shape, bf16),
               out_spec=_bs((None, None, None, D, GU), lambda i, j, k: (j % 2, j // 2, l, 0, 0)), dims=TN,
               acc_shape=None, alias=g_gu)
    dx, dg = _rms_bwd(tag + "_drms", dh, x, g, dxo)
    return dx, dg, g_gu, g_d


def _mixer_fwd(tag, x, p, win, wsq, l):
    T = x.shape[0]
    h = _rms_fwd(tag + "_rms", x, p["mix_norm"])
    tn = 512
    z = _mm(tag + "_in", h, win, grid=(1, NZ // tn, 1), a_spec=_bs((T, D), lambda i, j, k: (0, 0)),
            b_spec=_bs((None, D, tn), lambda i, j, k: (l, 0, j)), out_shape=S((T, NZ), f32),
            out_spec=_bs((T, tn), lambda i, j, k: (0, j)), dims=NN, acc_shape=None)
    ya_in = _conv_fwd(tag + "_conv", z, p["conv_w"])
    yb_in = _sgu_fwd(tag + "_sgu", z, p["sgu_ln_g"], p["sgu_ln_b"], p["sgu_w"], p["bmap"])
    qn, kn, vb, logf = _qk_fwd(tag + "_qk", z, p["q_norm_g"], p["k_norm_g"], p["b_forget"])
    ccol, crow = _cum_fwd(tag + "_cum", logf)
    crow3 = crow.reshape(NH, 1, T)
    o, lse = _attn_fwd(tag + "_attn", qn, kn, vb, ccol, crow3)
    ya = _sq_fwd(tag + "_oconv", ya_in, wsq, l, 0)
    yb = _sq_fwd(tag + "_osgu", yb_in, wsq, l, 1)
    yc = _sq_fwd(tag + "_oattn", o, wsq, l, 2)
    merged = _merge_fwd(tag + "_merge", ya, yb, yc, z, p["b_gate"])
    xo = _sq_fwd(tag + "_o", merged, wsq, l, 3, res=x)
    return xo, (h, z, ya_in, yb_in, qn, kn, vb, ccol, crow3, o, lse, ya, yb, yc, merged)


def _mixer_bwd(tag, dxo, x, p, win, wsq, l, saved, gsq):
    h, z, ya_in, yb_in, qn, kn, vb, ccol, crow3, o, lse, ya, yb, yc, merged = saved
    T = x.shape[0]
    sg = {}
    dm = _sq_bwd_in(tag + "_dmerged", dxo, wsq, l, 3)
    gsq[3] = _sq_bwd_w(tag + "_dwo", merged, dxo, gsq[3], l)
    dya, dyb, dyc, dz_g, sg["b_gate"] = _merge_bwd(tag + "_dmerge", dm, ya, yb, yc, z, p["b_gate"])
    d_ya_in = _sq_bwd_in(tag + "_dconv_in", dya, wsq, l, 0)
    gsq[0] = _sq_bwd_w(tag + "_dwoc", ya_in, dya, gsq[0], l)
    d_yb_in = _sq_bwd_in(tag + "_dsgu_in", dyb, wsq, l, 1)
    gsq[1] = _sq_bwd_w(tag + "_dwos", yb_in, dyb, gsq[1], l)
    d_o = _sq_bwd_in(tag + "_dattn_in", dyc, wsq, l, 2)
    gsq[2] = _sq_bwd_w(tag + "_dwoa", o, dyc, gsq[2], l)
    dz_c, sg["conv_w"] = _conv_bwd(tag + "_dconv", d_ya_in, z, p["conv_w"])
    dz_s, sg["sgu_ln_g"], sg["sgu_ln_b"], sg["sgu_w"], db_t = _sgu_bwd(
        tag + "_dsgu", d_yb_in, z, p["sgu_ln_g"], p["sgu_ln_b"], p["sgu_w"], p["bmap"])
    sg["sgu_b"] = db_t.T
    dqn, dkn, dv, rs, cs = _attn_bwd(tag + "_dattn", qn, kn, vb, d_o, o, lse, ccol, crow3)
    rs_t = jnp.pad(rs[:, :, 0].T, ((0, 0), (0, 128 - NH)))
    cs_t = jnp.pad(cs[:, 0, :].T, ((0, 0), (0, 128 - NH)))
    dz_f, sg["b_forget"] = _forget_bwd(tag + "_dforget", rs_t, cs_t, z, p["b_forget"])
    dz_q, sg["q_norm_g"], sg["k_norm_g"] = _qk_bwd(tag + "_dqk", dqn, dkn, dv, z, p["q_norm_g"], p["k_norm_g"])
    dz = jnp.concatenate([dz_c, dz_s, dz_q, dz_g, dz_f], axis=1)
    tm = _tile(T, 512)
    tk = 512
    dh = _mm(tag + "_dh", dz, win, grid=(T // tm, 1, NZ // tk), a_spec=_bs((tm, tk), lambda i, j, k: (i, k)),
             b_spec=_bs((None, D, tk), lambda i, j, k: (l, 0, k)), out_shape=S((T, D), f32),
             out_spec=_bs((tm, D), lambda i, j, k: (i, 0)), dims=NT, acc_shape=(tm, D))
    tn = 512
    dwin = _mm(tag + "_dwin", h, dz, grid=(1, NZ // tn, 1), a_spec=_bs((T, D), lambda i, j, k: (0, 0)),
               b_spec=_bs((T, tn), lambda i, j, k: (0, j)), out_shape=S((D, NZ), bf16),
               out_spec=_bs((D, tn), lambda i, j, k: (0, j)), dims=TN, acc_shape=None)
    dx, sg["mix_norm"] = _rms_bwd(tag + "_drms", dh, x, p["mix_norm"], dxo)
    return dx, sg, dwin, gsq


def _local_step(x, tgt, small, wts, gbufs):
    saved = []
    for l in range(2):
        p = small[l]
        x1, s1 = _ffn_fwd("ffn1", x, p["ffn1_norm"], wts["gu1"], wts["d1"], l)
        x2, sm = _mixer_fwd("mix", x1, p, wts["win"], wts["sq"], l)
        x3, s2 = _ffn_fwd("ffn2", x2, p["ffn2_norm"], wts["gu2"], wts["d2"], l)
        saved.append((x, x1, x2, s1, sm, s2))
        x = x3
    loss_row, dx = _loss("loss", x, tgt)
    sgrads = [None, None]
    dwin = [None, None]
    gsq = [gbufs["oc"], gbufs["os"], gbufs["oa"], gbufs["o"]]
    g_gu1, g_d1, g_gu2, g_d2 = gbufs["gu1"], gbufs["d1"], gbufs["gu2"], gbufs["d2"]
    for l in (1, 0):
        p = small[l]
        x0, x1, x2, s1, sm, s2 = saved[l]
        dx, dn2, g_gu2, g_d2 = _ffn_bwd("ffn2", dx, x2, p["ffn2_norm"], wts["gu2"], wts["d2"], l, s2, g_gu2, g_d2)
        dx, sg, dwin[l], gsq = _mixer_bwd("mix", dx, x1, p, wts["win"], wts["sq"], l, sm, gsq)
        dx, dn1, g_gu1, g_d1 = _ffn_bwd("ffn1", dx, x0, p["ffn1_norm"], wts["gu1"], wts["d1"], l, s1, g_gu1, g_d1)
        sg["ffn1_norm"] = dn1
        sg["ffn2_norm"] = dn2
        sgrads[l] = sg
    gout = {"gu1": g_gu1, "d1": g_d1, "gu2": g_gu2, "d2": g_d2, "oc": gsq[0], "os": gsq[1], "oa": gsq[2], "o": gsq[3]}
    return loss_row, dx, sgrads, dwin, gout


ANY = pl.BlockSpec(memory_space=pl.ANY)


def _place():
    return lax.axis_index("x"), lax.axis_index("y"), lax.axis_index("c")


def _all_gather(name, arrs):
    n = len(arrs)
    units = [(a, l) for a in range(n) for l in range(2)]
    nu = len(units)

    def body(*refs):
        ins, outs = refs[:n], refs[n:2 * n]
        send, recv, lsem = refs[2 * n:]
        x, y, c = _place()
        me, sib = (x, y, c), (x, y, 1 - c)
        chips = [(1 - x, y), (x, 1 - y), (1 - x, 1 - y)]

        def blk(u, px, py, pc):
            a, l = units[u]
            return outs[a].at[l, 4 * px + 2 * py + pc]

        def copy(u, k, block, to, src=None):
            dst = blk(u, *block)
            return pltpu.make_async_remote_copy(src_ref=dst if src is None else src, dst_ref=dst,
                                                send_sem=send.at[u * 7 + k], recv_sem=recv.at[u * 7 + k],
                                                device_id=to, device_id_type=MESH)

        mine = [pltpu.make_async_copy(ins[a].at[l], blk(u, *me), lsem.at[u]) for u, (a, l) in enumerate(units)]
        for cp in mine:
            cp.start()
        first = []
        for u, (a, l) in enumerate(units):
            first.append(copy(u, 0, me, sib, ins[a].at[l]))
            for j, chip in enumerate(chips):
                first.append(copy(u, 1 + j, me, (*chip, c), ins[a].at[l]))
        for cp in first:
            cp.start()
        passed = []
        for u in range(nu):
            for j, chip in enumerate(chips):
                copy(u, 1 + j, (*chip, c), me).wait_recv()
                fwd = copy(u, 4 + j, (*chip, c), sib)
                fwd.start()
                passed.append(fwd)
        for u in range(nu):
            copy(u, 0, sib, me).wait_recv()
            for j, chip in enumerate(chips):
                copy(u, 4 + j, (*chip, 1 - c), me).wait_recv()
        for cp in first + passed:
            cp.wait_send()
        for cp in mine:
            cp.wait()

    return pl.pallas_call(
        body, name=name, in_specs=[ANY] * n, out_specs=[ANY] * n,
        out_shape=[S((2, NDEV) + a.shape[1:], a.dtype) for a in arrs],
        scratch_shapes=[pltpu.SemaphoreType.DMA((nu * 7,)), pltpu.SemaphoreType.DMA((nu * 7,)),
                        pltpu.SemaphoreType.DMA((nu,))],
    )(*arrs)


def _rs_pair(name, gs):
    n = len(gs)

    def body(*refs):
        ins, outs = refs[:n], refs[n:2 * n]
        send, recv = refs[2 * n:]
        x, y, c = _place()
        cps = [pltpu.make_async_remote_copy(src_ref=ins[u].at[1 - c], dst_ref=outs[u], send_sem=send.at[u],
                                            recv_sem=recv.at[u], device_id=(x, y, 1 - c), device_id_type=MESH)
               for u in range(n)]
        for cp in cps:
            cp.start()
        for cp in cps:
            cp.wait()

    return pl.pallas_call(
        body, name=name, in_specs=[ANY] * n, out_specs=[ANY] * n, out_shape=[S(g.shape[1:], g.dtype) for g in gs],
        scratch_shapes=[pltpu.SemaphoreType.DMA((n,)), pltpu.SemaphoreType.DMA((n,))],
    )(*gs)


def _row_tile(r, c):
    return 128 if (r % 128 == 0 and c > D) else (256 if r % 256 == 0 else r)


def _pair_sum(name, core, g, r1):
    _, nq, nl, r, c = g.shape
    tr = _row_tile(r, c)
    g4 = g.reshape(2, nq * nl, r, c)
    r3 = r1.reshape(nq * nl, r, c)

    def body(core_ref, g_ref, r_ref, o_ref):
        o_ref[...] = (g_ref[...].astype(f32) + r_ref[...].astype(f32)).astype(bf16)

    out = pl.pallas_call(
        body, name=name,
        grid_spec=pltpu.PrefetchScalarGridSpec(
            num_scalar_prefetch=1, grid=(nq * nl, r // tr),
            in_specs=[_bs((None, None, tr, c), lambda b, i, cr: (cr[0], b, i, 0)), _bs((None, tr, c), lambda b, i, cr: (b, i, 0))],
            out_specs=_bs((None, tr, c), lambda b, i, cr: (b, i, 0))),
        out_shape=S((nq * nl, r, c), bf16), compiler_params=_cp("parallel", "parallel"))(core, g4, r3)
    return out.reshape(nq, nl, r, c)


def _rs_chips(name, ss):
    n = len(ss)

    def body(*refs):
        ins, outs = refs[:n], refs[n:2 * n]
        send, recv = refs[2 * n:]
        x, y, c = _place()
        chips = [(1 - x, y), (x, 1 - y), (1 - x, 1 - y)]
        cps = []
        for u in range(n):
            for k, chip in enumerate(chips):
                cps.append(pltpu.make_async_remote_copy(
                    src_ref=ins[u].at[2 * chip[0] + chip[1]], dst_ref=outs[u].at[k], send_sem=send.at[u * 3 + k],
                    recv_sem=recv.at[u * 3 + k], device_id=(*chip, c), device_id_type=MESH))
        for cp in cps:
            cp.start()
        for cp in cps:
            cp.wait()

    return pl.pallas_call(
        body, name=name, in_specs=[ANY] * n, out_specs=[ANY] * n, out_shape=[S((3,) + s.shape[1:], s.dtype) for s in ss],
        scratch_shapes=[pltpu.SemaphoreType.DMA((n * 3,)), pltpu.SemaphoreType.DMA((n * 3,))],
    )(*ss)


def _all_reduce_small(name, v):
    r = v.shape[0]

    def body(v_ref, o_ref, buf, send, recv, lsem):
        x, y, c = _place()
        me, sib = (x, y, c), (x, y, 1 - c)
        chips = [(1 - x, y), (x, 1 - y), (1 - x, 1 - y)]

        def slot(px, py, pc):
            return buf.at[4 * px + 2 * py + pc]

        def copy(k, block, to, src=None):
            return pltpu.make_async_remote_copy(src_ref=slot(*block) if src is None else src, dst_ref=slot(*block),
                                                send_sem=send.at[k], recv_sem=recv.at[k], device_id=to,
                                                device_id_type=MESH)

        mine = pltpu.make_async_copy(v_ref, slot(*me), lsem)
        mine.start()
        first = [copy(0, me, sib, v_ref)] + [copy(1 + j, me, (*chip, c), v_ref) for j, chip in enumerate(chips)]
        for cp in first:
            cp.start()
        passed = [copy(4 + j, (*chip, c), sib) for j, chip in enumerate(chips)]
        for j, chip in enumerate(chips):
            copy(1 + j, (*chip, c), me).wait_recv()
            passed[j].start()
        copy(0, sib, me).wait_recv()
        for j, chip in enumerate(chips):
            copy(4 + j, (*chip, 1 - c), me).wait_recv()
        for cp in first + passed:
            cp.wait_send()
        mine.wait()
        acc = buf[0]
        for d in range(1, NDEV):
            acc = acc + buf[d]
        o_ref[...] = acc

    vm = pl.BlockSpec(memory_space=pltpu.VMEM)
    return pl.pallas_call(
        body, name=name, in_specs=[vm], out_specs=vm, out_shape=S((r, 128), f32),
        scratch_shapes=[pltpu.VMEM((NDEV, r, 128), f32), pltpu.SemaphoreType.DMA((7,)), pltpu.SemaphoreType.DMA((7,)),
                        pltpu.SemaphoreType.DMA],
        compiler_params=pltpu.CompilerParams(vmem_limit_bytes=VMEM_LIMIT),
    )(v)


def _adam_math(w, g, m, v):
    m = ADAM_B1 * m + (1.0 - ADAM_B1) * g
    v = ADAM_B2 * v + (1.0 - ADAM_B2) * (g * g)
    m_hat = m / (1.0 - ADAM_B1 ** ADAM_STEP)
    v_hat = v / (1.0 - ADAM_B2 ** ADAM_STEP)
    delta = -ADAM_LR * (m_hat / (jnp.sqrt(v_hat) + ADAM_EPS) + ADAM_WD * w)
    return delta, m, v


def _adamw(name, w, m, v, s_mine, r2):
    _, r, c = w.shape
    tr = _row_tile(r, c)

    def body(w_ref, m_ref, v_ref, s_ref, r0_ref, r1_ref, r2_ref, g_ref, d_ref, mo_ref, vo_ref):
        g = ((s_ref[...].astype(f32) + r0_ref[...].astype(f32)) + r1_ref[...].astype(f32)) + r2_ref[...].astype(f32)
        g_ref[...] = g
        d_ref[...], mo_ref[...], vo_ref[...] = _adam_math(w_ref[...], g, m_ref[...], v_ref[...])

    blk = _bs((None, tr, c), lambda l, i: (l, i, 0))
    part = lambda k: _bs((None, None, tr, c), lambda l, i: (k, l, i, 0))
    return pl.pallas_call(
        body, name=name, grid=(2, r // tr), in_specs=[blk, blk, blk, blk, part(0), part(1), part(2)],
        out_specs=[blk] * 4, out_shape=[S(w.shape, f32)] * 4, compiler_params=_cp("parallel", "parallel"),
    )(w, m, v, s_mine, r2, r2, r2)


def _adamw_small(name, w, g, m, v):
    def body(w_ref, g_ref, m_ref, v_ref, d_ref, mo_ref, vo_ref):
        d_ref[...], mo_ref[...], vo_ref[...] = _adam_math(w_ref[...], g_ref[...], m_ref[...], v_ref[...])

    return pl.pallas_call(body, name=name, out_shape=[S(w.shape, f32)] * 3,
                          compiler_params=pltpu.CompilerParams(vmem_limit_bytes=VMEM_LIMIT))(w, g, m, v)


WEIGHT_NAMES = ("ffn1_norm", "ffn1_w_gu", "ffn1_w_down", "mix_norm", "w_in", "b_forget", "b_gate", "conv_w", "sgu_ln_g",
                "sgu_ln_b", "sgu_w", "sgu_b", "q_norm_g", "k_norm_g", "w_out_conv", "w_out_sgu", "w_out_attn", "w_o",
                "ffn2_norm", "ffn2_w_gu", "ffn2_w_down")
BIG = {"ffn1_w_gu": "gu1", "ffn2_w_gu": "gu2", "ffn1_w_down": "d1", "ffn2_w_down": "d2", "w_in": "in",
       "w_out_conv": "oc", "w_out_sgu": "os", "w_out_attn": "oa", "w_o": "o"}
BIG_KEYS = ("gu1", "gu2", "d1", "d2", "in", "oc", "os", "oa", "o")
REPLICATED_SMALL = ("ffn1_norm", "mix_norm", "b_forget", "sgu_ln_g", "sgu_ln_b", "sgu_w", "sgu_b", "q_norm_g",
                    "k_norm_g", "ffn2_norm")
SHARDED_SMALL = ("b_gate", "conv_w")


def _pack(arrays):
    flat = jnp.concatenate([a.reshape(-1).astype(f32) for a in arrays])
    rows = -(-flat.shape[0] // 1024) * 8
    return jnp.pad(flat, (0, rows * 128 - flat.shape[0])).reshape(rows, 128)


def _unpack(packed, shapes):
    flat = packed.reshape(-1)
    out, pos = [], 0
    for shp in shapes:
        size = 1
        for s_ in shp:
            size *= s_
        out.append(flat[pos:pos + size].reshape(shp))
        pos += size
    return out


def _win_kernel_layout(w):
    return jnp.concatenate([w[:, :GATE_OFF], w[:, GATE_OFF + NH:], w[:, GATE_OFF:GATE_OFF + NH],
                            jnp.zeros((D, NZ - NIN), w.dtype)], axis=1)


def _win_natural_layout(dw):
    return jnp.concatenate([dw[:, :GATE_OFF], dw[:, F_OFF:F_OFF + NH], dw[:, GATE_OFF:F_OFF]], axis=1)


def kernel(x, ffn1_norm, ffn1_w_gu, ffn1_w_down, mix_norm, w_in, b_forget, b_gate, conv_w, sgu_ln_g, sgu_ln_b, sgu_w, sgu_b, q_norm_g, k_norm_g, w_out_conv, w_out_sgu, w_out_attn, w_o, ffn2_norm, ffn2_w_gu, ffn2_w_down, loss_target, m_ffn1_norm, m_ffn1_w_gu, m_ffn1_w_down, m_mix_norm, m_w_in, m_b_forget, m_b_gate, m_conv_w, m_sgu_ln_g, m_sgu_ln_b, m_sgu_w, m_sgu_b, m_q_norm_g, m_k_norm_g, m_w_out_conv, m_w_out_sgu, m_w_out_attn, m_w_o, m_ffn2_norm, m_ffn2_w_gu, m_ffn2_w_down, v_ffn1_norm, v_ffn1_w_gu, v_ffn1_w_down, v_mix_norm, v_w_in, v_b_forget, v_b_gate, v_conv_w, v_sgu_ln_g, v_sgu_ln_b, v_sgu_w, v_sgu_b, v_q_norm_g, v_k_norm_g, v_w_out_conv, v_w_out_sgu, v_w_out_attn, v_w_o, v_ffn2_norm, v_ffn2_w_gu, v_ffn2_w_down):
    w = dict(zip(WEIGHT_NAMES, (ffn1_norm, ffn1_w_gu, ffn1_w_down, mix_norm, w_in, b_forget, b_gate, conv_w, sgu_ln_g,
                                sgu_ln_b, sgu_w, sgu_b, q_norm_g, k_norm_g, w_out_conv, w_out_sgu, w_out_attn, w_o,
                                ffn2_norm, ffn2_w_gu, ffn2_w_down)))
    mom = dict(zip(WEIGHT_NAMES, (m_ffn1_norm, m_ffn1_w_gu, m_ffn1_w_down, m_mix_norm, m_w_in, m_b_forget, m_b_gate,
                                  m_conv_w, m_sgu_ln_g, m_sgu_ln_b, m_sgu_w, m_sgu_b, m_q_norm_g, m_k_norm_g,
                                  m_w_out_conv, m_w_out_sgu, m_w_out_attn, m_w_o, m_ffn2_norm, m_ffn2_w_gu,
                                  m_ffn2_w_down)))
    var = dict(zip(WEIGHT_NAMES, (v_ffn1_norm, v_ffn1_w_gu, v_ffn1_w_down, v_mix_norm, v_w_in, v_b_forget, v_b_gate,
                                  v_conv_w, v_sgu_ln_g, v_sgu_ln_b, v_sgu_w, v_sgu_b, v_q_norm_g, v_k_norm_g,
                                  v_w_out_conv, v_w_out_sgu, v_w_out_attn, v_w_o, v_ffn2_norm, v_ffn2_w_gu,
                                  v_ffn2_w_down)))
    px, py, pc = _place()
    dev = 4 * px + 2 * py + pc
    chip = 2 * px + py

    big_names = [n for n in WEIGHT_NAMES if n in BIG]
    local = [w[n].astype(bf16) for n in big_names]
    local.append(jnp.concatenate([w["b_gate"], w["conv_w"], jnp.zeros((2, 2, 128), f32)], axis=1))
    gathered = _all_gather("gather_weights", local)
    gw = {BIG[n]: g for n, g in zip(big_names, gathered[:-1])}
    gsmall = gathered[-1]
    wts = {
        "gu1": gw["gu1"], "gu2": gw["gu2"],
        "d1": gw["d1"].reshape(2, 4, GU, D), "d2": gw["d2"].reshape(2, 4, GU, D),
        "win": jnp.stack([_win_kernel_layout(jnp.transpose(gw["in"][l], (1, 0, 2)).reshape(D, NIN)) for l in range(2)]),
        "sq": jnp.stack([gw[k].reshape(2, D, D) for k in ("oc", "os", "oa", "o")], axis=1),
    }
    full_small = {"b_gate": jnp.transpose(gsmall[:, :, 0:3, :], (0, 2, 1, 3)).reshape(2, 3, D),
                  "conv_w": jnp.transpose(gsmall[:, :, 3:6, :], (0, 2, 1, 3)).reshape(2, 3, D)}
    small = []
    for l in range(2):
        p = {n: w[n][l] for n in REPLICATED_SMALL}
        p.update({n: full_small[n][l] for n in SHARDED_SMALL})
        small.append(_small_params(p))

    shapes = {"gu1": (D, GU), "gu2": (D, GU), "d1": (GU // 2, D), "d2": (GU // 2, D), "oc": (128, D), "os": (128, D),
              "oa": (128, D), "o": (128, D)}
    gbufs = {k: lax.empty((2, 4, 2) + shp, bf16) for k, shp in shapes.items()}
    loss_row, dx, sgrads, dwin, gout = _local_step(x[0], loss_target[0], small, wts, gbufs)
    gin = [jnp.transpose(_win_natural_layout(dwin[l]).reshape(D, 4, 2, INB), (2, 1, 0, 3)) for l in range(2)]
    gout["in"] = jnp.stack(gin, axis=2)

    gs = [gout[k] for k in BIG_KEYS]
    r1 = _rs_pair("rs_pair", gs)
    core = pc.reshape(1).astype(jnp.int32)
    ss = [_pair_sum("pair_sum_" + k, core, g, r) for k, g, r in zip(BIG_KEYS, gs, r1)]
    r2 = _rs_chips("rs_chips", ss)

    grads, delta, new_m, new_v = {}, {}, {}, {}
    for n in big_names:
        i = BIG_KEYS.index(BIG[n])
        s_mine = lax.dynamic_index_in_dim(ss[i], chip, 0, keepdims=False)
        grads[n], delta[n], new_m[n], new_v[n] = _adamw("adamw_" + BIG[n], w[n], mom[n], var[n], s_mine, r2[i])

    nat = [_small_grads_natural(sgrads[l]) for l in range(2)]
    order = REPLICATED_SMALL + SHARDED_SMALL
    part = _pack([jnp.stack([nat[0][n], nat[1][n]]) for n in order] + [loss_row[0, 0:1]])
    total = _all_reduce_small("reduce_small", part)
    full_shapes = [(2,) + tuple(nat[0][n].shape) for n in order] + [(1,)]
    summed = dict(zip(order + ("loss",), _unpack(total, full_shapes)))
    for n in REPLICATED_SMALL:
        grads[n] = summed[n]
    for n in SHARDED_SMALL:
        grads[n] = lax.dynamic_slice_in_dim(summed[n], dev * 128, 128, axis=2)
    wp = _pack([w[n] for n in order])
    gp = _pack([grads[n] for n in order])
    mp = _pack([mom[n] for n in order])
    vp = _pack([var[n] for n in order])
    dpk, mpk, vpk = _adamw_small("adamw_small", wp, gp, mp, vp)
    local_shapes = [tuple(w[n].shape) for n in order]
    for dst, packed in ((delta, dpk), (new_m, mpk), (new_v, vpk)):
        dst.update(zip(order, _unpack(packed, local_shapes)))

    loss = summed["loss"][0]
    return (loss, dx[None], *[grads[n] for n in WEIGHT_NAMES], *[delta[n] for n in WEIGHT_NAMES],
            *[new_m[n] for n in WEIGHT_NAMES], *[new_v[n] for n in WEIGHT_NAMES])
```

```python
import functools

import jax
import jax.numpy as jnp
from jax import lax
from jax.experimental import pallas as pl
from jax.experimental.pallas import tpu as pltpu

f32 = jnp.float32
bf16 = jnp.bfloat16
S = jax.ShapeDtypeStruct
MESH = pl.DeviceIdType.MESH

D = 1024
NH = 8
HD = 128
NDEV = 8
GU = 704
NIN = 11272
INB = 1409
GATE_OFF = 8192
F_OFF = 11264
NZ = 11776
RMS_EPS = 1e-6
LN_EPS = 1e-5
ATT_SCALE = HD ** -0.5
NEG = -1e30
INV_SQRT2 = 0.7071067811865476
INV_SQRT2PI = 0.3989422804014327

ADAM_LR = 0.001
ADAM_B1 = 0.9
ADAM_B2 = 0.999
ADAM_EPS = 1e-08
ADAM_WD = 0.01
ADAM_STEP = 10

TT = 256
VMEM_LIMIT = 56 * 1024 * 1024


def _cp(*sem):
    return pltpu.CompilerParams(dimension_semantics=sem, vmem_limit_bytes=VMEM_LIMIT)


def _bs(shape, fn):
    return pl.BlockSpec(shape, fn)


NN = (((1,), (0,)), ((), ()))
NT = (((1,), (1,)), ((), ()))
TN = (((0,), (0,)), ((), ()))


def _mm(name, a, b, *, grid, a_spec, b_spec, out_shape, out_spec, dims, acc_shape, res=None, res_spec=None,
        alpha=1.0, alias=None, split_rows=None):
    nk = grid[2]

    def body(*refs):
        a_ref, b_ref = refs[0], refs[1]
        pos = 2
        res_ref = None
        if res is not None:
            res_ref = refs[pos]
            pos += 1
        if alias is not None:
            pos += 1
        o_ref = refs[pos]
        acc_ref = refs[pos + 1] if nk > 1 else None
        part = lax.dot_general(a_ref[...].astype(bf16), b_ref[...].astype(bf16), dims, preferred_element_type=f32)

        def finish(acc):
            if alpha != 1.0:
                acc = alpha * acc
            if res_ref is not None:
                acc = res_ref[...] + acc
            if split_rows is None:
                o_ref[...] = acc.astype(o_ref.dtype)
            else:
                o_ref[0] = acc[:split_rows].astype(o_ref.dtype)
                o_ref[1] = acc[split_rows:].astype(o_ref.dtype)

        if nk == 1:
            finish(part)
        else:
            k = pl.program_id(2)

            @pl.when(k == 0)
            def _():
                acc_ref[...] = part

            @pl.when(k > 0)
            def _():
                acc_ref[...] += part

            @pl.when(k == nk - 1)
            def _():
                finish(acc_ref[...])

    operands = [a, b]
    in_specs = [a_spec, b_spec]
    if res is not None:
        operands.append(res)
        in_specs.append(res_spec)
    aliases = {}
    if alias is not None:
        aliases = {len(operands): 0}
        operands.append(alias)
        in_specs.append(pl.BlockSpec(memory_space=pl.ANY))
    return pl.pallas_call(
        body, name=name, grid=grid, in_specs=in_specs, out_specs=out_spec, out_shape=out_shape,
        scratch_shapes=[pltpu.VMEM(acc_shape, f32)] if nk > 1 else [],
        input_output_aliases=aliases,
        compiler_params=_cp("parallel", "parallel", "arbitrary"),
    )(*operands)


def _tile(n, t):
    return t if n % t == 0 and n >= t else n


def _row(cb=0, w=D):
    return _bs((TT, w), lambda i: (i, cb))


def _vec(rows=1, w=D):
    return _bs((rows, w), lambda i: (0, 0))


def _acc_store(i, ref, val):
    @pl.when(i == 0)
    def _():
        ref[...] = val

    @pl.when(i > 0)
    def _():
        ref[...] += val


def _rms_fwd(name, x, g):
    T = x.shape[0]

    def body(x_ref, g_ref, o_ref):
        xv = x_ref[...]
        r = lax.rsqrt(jnp.mean(xv * xv, axis=-1, keepdims=True) + RMS_EPS)
        o_ref[...] = (xv * r * g_ref[...]).astype(bf16)

    return pl.pallas_call(body, name=name, grid=(T // TT,), in_specs=[_row(), _vec()], out_specs=_row(),
                          out_shape=S((T, D), bf16), compiler_params=_cp("parallel"))(x, g)


def _rms_bwd(name, dh, x, g, dres):
    T = x.shape[0]

    def body(dh_ref, x_ref, g_ref, dres_ref, dx_ref, dg_ref):
        i = pl.program_id(0)
        xv = x_ref[...]
        r = lax.rsqrt(jnp.mean(xv * xv, axis=-1, keepdims=True) + RMS_EPS)
        xhat = xv * r
        dh_v = dh_ref[...]
        dyg = dh_v * g_ref[...]
        m = jnp.mean(dyg * xhat, axis=-1, keepdims=True)
        dx_ref[...] = dres_ref[...] + r * (dyg - xhat * m)
        _acc_store(i, dg_ref, jnp.sum(dh_v * xhat, axis=0, keepdims=True))

    return pl.pallas_call(body, name=name, grid=(T // TT,), in_specs=[_row(), _row(), _vec(), _row()],
                          out_specs=[_row(), _vec()], out_shape=[S((T, D), f32), S((1, D), f32)],
                          compiler_params=_cp("arbitrary"))(dh, x, g, dres)


def _sigmoid(x):
    return 1.0 / (1.0 + jnp.exp(-x))


def _swiglu_fwd(name, gu):
    T = gu.shape[1]

    def body(g_ref, u_ref, o_ref):
        g = g_ref[...]
        o_ref[...] = (g * _sigmoid(g) * u_ref[...]).astype(bf16)

    return pl.pallas_call(
        body, name=name, grid=(4, T // TT),
        in_specs=[_bs((None, TT, GU), lambda j, i: (j, i, 0)), _bs((None, TT, GU), lambda j, i: (j + 4, i, 0))],
        out_specs=_bs((None, TT, GU), lambda j, i: (j, i, 0)), out_shape=S((4, T, GU), bf16),
        compiler_params=_cp("parallel", "parallel"))(gu, gu)


def _swiglu_bwd(name, gu, da):
    T = gu.shape[1]

    def body(g_ref, u_ref, da_ref, o_ref):
        g = g_ref[...]
        u = u_ref[...]
        da_v = da_ref[...]
        sg = _sigmoid(g)
        o_ref[0] = (da_v * u * (sg + g * sg * (1.0 - sg))).astype(bf16)
        o_ref[1] = (da_v * g * sg).astype(bf16)

    return pl.pallas_call(
        body, name=name, grid=(4, T // TT),
        in_specs=[_bs((None, TT, GU), lambda j, i: (j, i, 0)), _bs((None, TT, GU), lambda j, i: (j + 4, i, 0)),
                  _bs((None, TT, GU), lambda j, i: (j, i, 0))],
        out_specs=_bs((2, None, TT, GU), lambda j, i: (0, j, i, 0)), out_shape=S((2, 4, T, GU), bf16),
        compiler_params=_cp("parallel", "parallel"))(gu, gu, da)


def _loss(name, y, tgt):
    T = y.shape[0]

    def body(y_ref, t_ref, l_ref, dy_ref):
        i = pl.program_id(0)
        e = y_ref[...] - t_ref[...]
        dy_ref[...] = e * (1.0 / D)
        s = 0.5 * jnp.sum(jnp.mean(e * e, axis=-1, keepdims=True))
        _acc_store(i, l_ref, jnp.broadcast_to(s, (1, 128)))

    return pl.pallas_call(body, name=name, grid=(T // TT,), in_specs=[_row(), _row()],
                          out_specs=[_vec(1, 128), _row()], out_shape=[S((1, 128), f32), S((T, D), f32)],
                          compiler_params=_cp("arbitrary"))(y, tgt)


def _prev8(T, cb):
    return _bs((8, D), lambda i: (jnp.maximum(i * (TT // 8) - 1, 0), cb))


def _next8(T, cb):
    return _bs((8, D), lambda i: (jnp.minimum((i + 1) * (TT // 8), T // 8 - 1), cb))


def _conv_taps(i, ac_ref, ax_ref, pc_ref, px_ref):
    ca = ac_ref[...] * ax_ref[...]
    keep = (i > 0).astype(f32)
    p1 = pc_ref[7:8, :] * px_ref[7:8, :] * keep
    p2 = pc_ref[6:7, :] * px_ref[6:7, :] * keep
    row = lax.broadcasted_iota(jnp.int32, ca.shape, 0)
    s1 = jnp.where(row == 0, p1, pltpu.roll(ca, 1, 0))
    s2 = jnp.where(row == 0, p2, jnp.where(row == 1, p1, pltpu.roll(ca, 2, 0)))
    return ca, s1, s2


def _conv_fwd(name, z, cw):
    T = z.shape[0]

    def body(ab_ref, ac_ref, ax_ref, pc_ref, px_ref, w_ref, o_ref):
        i = pl.program_id(0)
        ca, s1, s2 = _conv_taps(i, ac_ref, ax_ref, pc_ref, px_ref)
        cv = w_ref[0:1, :] * s2 + w_ref[1:2, :] * s1 + w_ref[2:3, :] * ca
        o_ref[...] = (ab_ref[...] * cv).astype(bf16)

    return pl.pallas_call(
        body, name=name, grid=(T // TT,),
        in_specs=[_row(0), _row(1), _row(2), _prev8(T, 1), _prev8(T, 2), _vec(3)],
        out_specs=_row(), out_shape=S((T, D), bf16), compiler_params=_cp("parallel"))(z, z, z, z, z, cw)


def _conv_bwd(name, dya, z, cw):
    T = z.shape[0]
    n = T // TT

    def body(dya_ref, ab_ref, ac_ref, ax_ref, pc_ref, px_ref, ndya_ref, nab_ref, w_ref, dz_ref, dw_ref):
        i = pl.program_id(0)
        ca, s1, s2 = _conv_taps(i, ac_ref, ax_ref, pc_ref, px_ref)
        w0, w1, w2 = w_ref[0:1, :], w_ref[1:2, :], w_ref[2:3, :]
        cv = w0 * s2 + w1 * s1 + w2 * ca
        dya_v = dya_ref[...]
        ab = ab_ref[...]
        dcv = dya_v * ab
        keep = (i < n - 1).astype(f32)
        n1 = ndya_ref[0:1, :] * nab_ref[0:1, :] * keep
        n2 = ndya_ref[1:2, :] * nab_ref[1:2, :] * keep
        row = lax.broadcasted_iota(jnp.int32, dcv.shape, 0)
        f1 = jnp.where(row == TT - 1, n1, pltpu.roll(dcv, TT - 1, 0))
        f2 = jnp.where(row == TT - 1, n2, jnp.where(row == TT - 2, n1, pltpu.roll(dcv, TT - 2, 0)))
        dca = w2 * dcv + w1 * f1 + w0 * f2
        dz_ref[:, 0:D] = (dya_v * cv).astype(bf16)
        dz_ref[:, D:2 * D] = (dca * ax_ref[...]).astype(bf16)
        dz_ref[:, 2 * D:3 * D] = (dca * ac_ref[...]).astype(bf16)
        dw = jnp.concatenate([jnp.sum(dcv * s2, axis=0, keepdims=True), jnp.sum(dcv * s1, axis=0, keepdims=True),
                              jnp.sum(dcv * ca, axis=0, keepdims=True)], axis=0)
        _acc_store(i, dw_ref, dw)

    return pl.pallas_call(
        body, name=name, grid=(n,),
        in_specs=[_row(), _row(0), _row(1), _row(2), _prev8(T, 1), _prev8(T, 2), _next8(T, 0), _next8(T, 0), _vec(3)],
        out_specs=[_row(0, 3 * D), _vec(3)], out_shape=[S((T, 3 * D), bf16), S((3, D), f32)],
        compiler_params=_cp("arbitrary"))(dya, z, z, z, z, z, dya, z, cw)


def _gelu(x):
    return 0.5 * x * (1.0 + lax.erf(x * INV_SQRT2))


def _gelu_grad(x):
    return 0.5 * (1.0 + lax.erf(x * INV_SQRT2)) + x * jnp.exp(-0.5 * x * x) * INV_SQRT2PI


def _ln_stats(vv):
    mu = jnp.mean(vv, axis=-1, keepdims=True)
    xc = vv - mu
    rstd = lax.rsqrt(jnp.mean(xc * xc, axis=-1, keepdims=True) + LN_EPS)
    return xc * rstd, rstd


def _tril_w(w_ref, g):
    r = lax.broadcasted_iota(jnp.int32, (HD, HD), 0)
    c = lax.broadcasted_iota(jnp.int32, (HD, HD), 1)
    return jnp.where(c <= r, w_ref[g], 0.0).astype(bf16)


def _sgu_fwd(name, z, ln_g, ln_b, w_s, bmap):
    T = z.shape[0]

    def body(su_ref, sv_ref, lg_ref, lb_ref, w_ref, bm_ref, o_ref, vn_ref):
        xhat, _ = _ln_stats(_gelu(sv_ref[...]))
        vn_ref[...] = (xhat * lg_ref[...] + lb_ref[...]).astype(bf16)
        for g in range(NH):
            w = _tril_w(w_ref, g)
            cs = slice(g * HD, (g + 1) * HD)
            for c in range(TT // HD):
                rs = slice(c * HD, (c + 1) * HD)
                s = jnp.dot(w, vn_ref[rs, cs], preferred_element_type=f32) + bm_ref[:, cs]
                o_ref[rs, cs] = (_gelu(su_ref[rs, cs]) * s).astype(bf16)

    return pl.pallas_call(
        body, name=name, grid=(T // TT,),
        in_specs=[_row(3), _row(4), _vec(), _vec(), _bs((NH, HD, HD), lambda i: (0, 0, 0)), _vec(HD)],
        out_specs=_row(), out_shape=S((T, D), bf16), scratch_shapes=[pltpu.VMEM((TT, D), bf16)],
        compiler_params=_cp("parallel"))(z, z, ln_g, ln_b, w_s, bmap)


def _sgu_bwd(name, dyb, z, ln_g, ln_b, w_s, bmap):
    T = z.shape[0]

    def body(dyb_ref, su_ref, sv_ref, lg_ref, lb_ref, w_ref, bm_ref, dz_ref, dlg_ref, dlb_ref, dw_ref, db_ref,
             vn_ref, du_ref, dvn_ref):
        i = pl.program_id(0)
        sv = sv_ref[...]
        xhat, rstd = _ln_stats(_gelu(sv))
        vn_ref[...] = (xhat * lg_ref[...] + lb_ref[...]).astype(bf16)
        r = lax.broadcasted_iota(jnp.int32, (HD, HD), 0)
        cc = lax.broadcasted_iota(jnp.int32, (HD, HD), 1)
        for g in range(NH):
            w = _tril_w(w_ref, g)
            cs = slice(g * HD, (g + 1) * HD)
            dw = jnp.zeros((HD, HD), f32)
            db = jnp.zeros((HD, 1), f32)
            for c in range(TT // HD):
                rs = slice(c * HD, (c + 1) * HD)
                vnb = vn_ref[rs, cs]
                s = jnp.dot(w, vnb, preferred_element_type=f32) + bm_ref[:, cs]
                dy = dyb_ref[rs, cs]
                du_ref[rs, cs] = dy * s
                ds = dy * _gelu(su_ref[rs, cs])
                ds16 = ds.astype(bf16)
                dvn_ref[rs, cs] = lax.dot_general(w, ds16, TN, preferred_element_type=f32)
                dw = dw + lax.dot_general(ds16, vnb, NT, preferred_element_type=f32)
                db = db + jnp.sum(ds, axis=1, keepdims=True)
            dw = jnp.where(cc <= r, dw, 0.0)

            @pl.when(i == 0)
            def _():
                dw_ref[g] = dw
                db_ref[:, g:g + 1] = db

            @pl.when(i > 0)
            def _():
                dw_ref[g] += dw
                db_ref[:, g:g + 1] += db

        dvn = dvn_ref[...]
        dxh = dvn * lg_ref[...]
        m1 = jnp.mean(dxh, axis=-1, keepdims=True)
        m2 = jnp.mean(dxh * xhat, axis=-1, keepdims=True)
        dvv = rstd * (dxh - m1 - xhat * m2)
        dz_ref[:, 0:D] = (du_ref[...] * _gelu_grad(su_ref[...])).astype(bf16)
        dz_ref[:, D:2 * D] = (dvv * _gelu_grad(sv)).astype(bf16)
        _acc_store(i, dlg_ref, jnp.sum(dvn * xhat, axis=0, keepdims=True))
        _acc_store(i, dlb_ref, jnp.sum(dvn, axis=0, keepdims=True))

    return pl.pallas_call(
        body, name=name, grid=(T // TT,),
        in_specs=[_row(), _row(3), _row(4), _vec(), _vec(), _bs((NH, HD, HD), lambda i: (0, 0, 0)), _vec(HD)],
        out_specs=[_row(0, 2 * D), _vec(), _vec(), _bs((NH, HD, HD), lambda i: (0, 0, 0)), _bs((HD, NH), lambda i: (0, 0))],
        out_shape=[S((T, 2 * D), bf16), S((1, D), f32), S((1, D), f32), S((NH, HD, HD), f32), S((HD, NH), f32)],
        scratch_shapes=[pltpu.VMEM((TT, D), bf16), pltpu.VMEM((TT, D), f32), pltpu.VMEM((TT, D), f32)],
        compiler_params=_cp("arbitrary"))(dyb, z, z, ln_g, ln_b, w_s, bmap)


def _qk_fwd(name, z, qg, kg, bf):
    T = z.shape[0]

    def body(q_ref, k_ref, v_ref, zf_ref, qg_ref, kg_ref, bf_ref, qn_ref, kn_ref, vb_ref, lf_ref):
        for h in range(NH):
            cs = slice(h * HD, (h + 1) * HD)
            for src, gain, dst in ((q_ref, qg_ref, qn_ref), (k_ref, kg_ref, kn_ref)):
                xv = src[:, cs]
                r = lax.rsqrt(jnp.mean(xv * xv, axis=-1, keepdims=True) + RMS_EPS)
                dst[:, cs] = (xv * r * gain[:, cs]).astype(bf16)
        vb_ref[...] = v_ref[...].astype(bf16)
        xf = zf_ref[...] + bf_ref[...]
        lf_ref[...] = jnp.minimum(xf, 0.0) - jnp.log1p(jnp.exp(-jnp.abs(xf)))

    return pl.pallas_call(
        body, name=name, grid=(T // TT,),
        in_specs=[_row(5), _row(6), _row(7), _bs((TT, 128), lambda i: (i, F_OFF // 128)), _vec(), _vec(), _vec(1, 128)],
        out_specs=[_row(), _row(), _row(), _bs((TT, 128), lambda i: (i, 0))],
        out_shape=[S((T, D), bf16), S((T, D), bf16), S((T, D), bf16), S((T, 128), f32)],
        compiler_params=_cp("parallel"))(z, z, z, z, qg, kg, bf)


def _cum_fwd(name, logf):
    T = logf.shape[0]

    def body(lf_ref, ccol_ref, crow_ref, c_ref):
        c = lf_ref[...]
        row = lax.broadcasted_iota(jnp.int32, c.shape, 0)
        s = 1
        while s < T:
            c = c + jnp.where(row >= s, pltpu.roll(c, s, 0), 0.0)
            s *= 2
        c_ref[...] = c
        crow_ref[...] = c.T[0:NH, :]
        for h in range(NH):
            ccol_ref[h] = jnp.broadcast_to(c_ref[:, h:h + 1], (T, 128))

    return pl.pallas_call(body, name=name, out_shape=[S((NH, T, 128), f32), S((NH, T), f32)],
                          scratch_shapes=[pltpu.VMEM((T, 128), f32)],
                          compiler_params=pltpu.CompilerParams(vmem_limit_bytes=VMEM_LIMIT))(logf)


def _attn_fwd(name, qn, kn, vb, ccol, crow3):
    T = qn.shape[0]
    tq = _tile(T, 256)
    nq = T // tq

    def body(q_ref, k_ref, v_ref, cc_ref, cr_ref, o_ref, lse_ref):
        qi = pl.program_id(1)
        q = q_ref[...]
        cq = cc_ref[:, 0:1]
        rows = qi * tq + lax.broadcasted_iota(jnp.int32, (tq, tq), 0)
        lane = lax.broadcasted_iota(jnp.int32, (tq, tq), 1)

        def step(j, carry):
            m, l, acc = carry
            off = pl.multiple_of(j * tq, tq)
            s = lax.dot_general(q, k_ref[pl.ds(off, tq), :], NT, preferred_element_type=f32) * ATT_SCALE
            s = s + cq - cr_ref[:, pl.ds(off, tq)]
            s = jnp.where(off + lane <= rows, s, NEG)
            mn = jnp.maximum(m, jnp.max(s, axis=1, keepdims=True))
            p = jnp.exp(s - mn)
            al = jnp.exp(m - mn)
            l = al * l + jnp.sum(p, axis=1, keepdims=True)
            acc = al * acc + jnp.dot(p.astype(bf16), v_ref[pl.ds(off, tq), :], preferred_element_type=f32)
            return mn, l, acc

        init = (jnp.full((tq, 1), NEG, f32), jnp.zeros((tq, 1), f32), jnp.zeros((tq, HD), f32))
        m, l, acc = lax.fori_loop(0, qi + 1, step, init)
        o_ref[...] = acc / l
        lse_ref[...] = jnp.broadcast_to(m + jnp.log(l), (tq, 128))

    return pl.pallas_call(
        body, name=name, grid=(NH, nq),
        in_specs=[_bs((tq, HD), lambda h, i: (i, h)), _bs((T, HD), lambda h, i: (0, h)), _bs((T, HD), lambda h, i: (0, h)),
                  _bs((None, tq, 128), lambda h, i: (h, i, 0)), _bs((None, 1, T), lambda h, i: (h, 0, 0))],
        out_specs=[_bs((tq, HD), lambda h, i: (i, h)), _bs((None, tq, 128), lambda h, i: (h, i, 0))],
        out_shape=[S((T, D), f32), S((NH, T, 128), f32)],
        compiler_params=_cp("parallel", "parallel"))(qn, kn, vb, ccol, crow3)


def _attn_bwd(name, qn, kn, vb, do, o, lse, ccol, crow3):
    T = qn.shape[0]
    tq = _tile(T, 256)
    nq = T // tq

    def body(q_ref, k_ref, v_ref, do_ref, o_ref, lse_ref, cc_ref, cr_ref, dq_ref, dk_ref, dv_ref, cs_ref):
        kj = pl.program_id(1)

        @pl.when(kj == 0)
        def _():
            dq_ref[...] = jnp.zeros_like(dq_ref)

        kb = k_ref[...]
        vv = v_ref[...]
        crj = cr_ref[...]
        cols = kj * tq + lax.broadcasted_iota(jnp.int32, (tq, tq), 1)
        sub = lax.broadcasted_iota(jnp.int32, (tq, tq), 0)

        def step(i, carry):
            dk, dv, cs = carry
            off = pl.multiple_of(i * tq, tq)
            rows = pl.ds(off, tq)
            qb = q_ref[rows, :]
            dob = do_ref[rows, :]
            delta = jnp.sum(dob * o_ref[rows, :], axis=1, keepdims=True)
            s = lax.dot_general(qb, kb, NT, preferred_element_type=f32) * ATT_SCALE
            s = s + cc_ref[rows, 0:1] - crj
            p = jnp.where(cols <= off + sub, jnp.exp(s - lse_ref[rows, 0:1]), 0.0)
            do16 = dob.astype(bf16)
            dv = dv + lax.dot_general(p.astype(bf16), do16, TN, preferred_element_type=f32)
            dp = lax.dot_general(do16, vv, NT, preferred_element_type=f32)
            ds = p * (dp - delta)
            ds16 = ds.astype(bf16)
            dk = dk + lax.dot_general(ds16, qb, TN, preferred_element_type=f32)
            dq_ref[rows, :] += jnp.dot(ds16, kb, preferred_element_type=f32) * ATT_SCALE
            cs = cs + jnp.sum(ds, axis=0, keepdims=True)
            return dk, dv, cs

        init = (jnp.zeros((tq, HD), f32), jnp.zeros((tq, HD), f32), jnp.zeros((1, tq), f32))
        dk, dv, cs = lax.fori_loop(kj, nq, step, init)
        dk_ref[...] = dk * ATT_SCALE
        dv_ref[...] = dv
        cs_ref[...] = cs

    full = lambda h, j: (0, h)
    full3 = lambda h, j: (h, 0, 0)
    blk = lambda h, j: (j, h)
    return pl.pallas_call(
        body, name=name, grid=(NH, nq),
        in_specs=[_bs((T, HD), full), _bs((tq, HD), blk), _bs((tq, HD), blk), _bs((T, HD), full), _bs((T, HD), full),
                  _bs((None, T, 128), full3), _bs((None, T, 128), full3), _bs((None, 1, tq), lambda h, j: (h, 0, j))],
        out_specs=[_bs((T, HD), full), _bs((tq, HD), blk), _bs((tq, HD), blk), _bs((None, 1, tq), lambda h, j: (h, 0, j))],
        out_shape=[S((T, D), f32), S((T, D), f32), S((T, D), f32), S((NH, 1, T), f32)],
        compiler_params=_cp("parallel", "arbitrary"))(qn, kn, vb, do, o, lse, ccol, crow3)


def _forget_bwd(name, cs, z, bf):
    T = cs.shape[0]

    def body(cs_ref, zf_ref, bf_ref, dz_ref, db_ref):
        c = -cs_ref[...]
        row = lax.broadcasted_iota(jnp.int32, c.shape, 0)
        s = 1
        while s < T:
            c = c + jnp.where(row + s < T, pltpu.roll(c, T - s, 0), 0.0)
            s *= 2
        xf = zf_ref[...] + bf_ref[...]
        lane = lax.broadcasted_iota(jnp.int32, c.shape, 1)
        dxf = jnp.where(lane < NH, c / (1.0 + jnp.exp(xf)), 0.0)
        dz_ref[...] = jnp.zeros_like(dz_ref)
        dz_ref[:, 0:128] = dxf.astype(bf16)
        db_ref[...] = jnp.sum(dxf, axis=0, keepdims=True)

    return pl.pallas_call(
        body, name=name, grid=(1,),
        in_specs=[_bs((T, 128), lambda i: (0, 0)), _bs((T, 128), lambda i: (0, F_OFF // 128)), _vec(1, 128)],
        out_specs=[_bs((T, NZ - F_OFF), lambda i: (0, 0)), _vec(1, 128)],
        out_shape=[S((T, NZ - F_OFF), bf16), S((1, 128), f32)], compiler_params=_cp("arbitrary"))(cs, z, bf)


def _qk_bwd(name, dqn, dkn, dv, z, qg, kg):
    T = z.shape[0]

    def body(dq_ref, dk_ref, dv_ref, q_ref, k_ref, qg_ref, kg_ref, dz_ref, dqg_ref, dkg_ref, g_ref):
        i = pl.program_id(0)
        for n, (src, dsrc, gain, dgain) in enumerate(((q_ref, dq_ref, qg_ref, dqg_ref), (k_ref, dk_ref, kg_ref, dkg_ref))):
            for h in range(NH):
                cs = slice(h * HD, (h + 1) * HD)
                xv = src[:, cs]
                r = lax.rsqrt(jnp.mean(xv * xv, axis=-1, keepdims=True) + RMS_EPS)
                xhat = xv * r
                dy = dsrc[:, cs]
                dyg = dy * gain[:, cs]
                m = jnp.mean(dyg * xhat, axis=-1, keepdims=True)
                dz_ref[:, n * D + h * HD:n * D + (h + 1) * HD] = (r * (dyg - xhat * m)).astype(bf16)
                g_ref[:, cs] = jnp.sum(dy * xhat, axis=0, keepdims=True)
            _acc_store(i, dgain, g_ref[...])
        dz_ref[:, 2 * D:3 * D] = dv_ref[...].astype(bf16)

    return pl.pallas_call(
        body, name=name, grid=(T // TT,),
        in_specs=[_row(), _row(), _row(), _row(5), _row(6), _vec(), _vec()],
        out_specs=[_row(0, 3 * D), _vec(), _vec()], out_shape=[S((T, 3 * D), bf16), S((1, D), f32), S((1, D), f32)],
        scratch_shapes=[pltpu.VMEM((1, D), f32)], compiler_params=_cp("arbitrary"))(dqn, dkn, dv, z, z, qg, kg)


GB = GATE_OFF // D


def _merge_fwd(name, ya, yb, yc, z, bg):
    T = z.shape[0]

    def body(ya_ref, yb_ref, yc_ref, g0_ref, g1_ref, g2_ref, bg_ref, o_ref):
        acc = _sigmoid(g0_ref[...] + bg_ref[0:1, :]) * ya_ref[...]
        acc = acc + _sigmoid(g1_ref[...] + bg_ref[1:2, :]) * yb_ref[...]
        acc = acc + _sigmoid(g2_ref[...] + bg_ref[2:3, :]) * yc_ref[...]
        o_ref[...] = acc.astype(bf16)

    return pl.pallas_call(
        body, name=name, grid=(T // TT,),
        in_specs=[_row(), _row(), _row(), _row(GB), _row(GB + 1), _row(GB + 2), _vec(3)],
        out_specs=_row(), out_shape=S((T, D), bf16), compiler_params=_cp("parallel"))(ya, yb, yc, z, z, z, bg)


def _merge_bwd(name, dm, ya, yb, yc, z, bg):
    T = z.shape[0]

    def body(dm_ref, ya_ref, yb_ref, yc_ref, g0_ref, g1_ref, g2_ref, bg_ref, dya_ref, dyb_ref, dyc_ref, dz_ref, db_ref):
        i = pl.program_id(0)
        dm_v = dm_ref[...]
        dbs = []
        for n, (y_ref, g_ref, dy_ref) in enumerate(((ya_ref, g0_ref, dya_ref), (yb_ref, g1_ref, dyb_ref),
                                                    (yc_ref, g2_ref, dyc_ref))):
            gate = _sigmoid(g_ref[...] + bg_ref[n:n + 1, :])
            dy_ref[...] = (dm_v * gate).astype(bf16)
            dl = dm_v * y_ref[...] * gate * (1.0 - gate)
            dz_ref[:, n * D:(n + 1) * D] = dl.astype(bf16)
            dbs.append(jnp.sum(dl, axis=0, keepdims=True))
        _acc_store(i, db_ref, jnp.concatenate(dbs, axis=0))

    return pl.pallas_call(
        body, name=name, grid=(T // TT,),
        in_specs=[_row(), _row(), _row(), _row(), _row(GB), _row(GB + 1), _row(GB + 2), _vec(3)],
        out_specs=[_row(), _row(), _row(), _row(0, 3 * D), _vec(3)],
        out_shape=[S((T, D), bf16), S((T, D), bf16), S((T, D), bf16), S((T, 3 * D), bf16), S((3, D), f32)],
        compiler_params=_cp("arbitrary"))(dm, ya, yb, yc, z, z, z, bg)


SMALL_NAMES = ("ffn1_norm", "mix_norm", "b_forget", "b_gate", "conv_w", "sgu_ln_g", "sgu_ln_b", "sgu_w", "sgu_b",
               "q_norm_g", "k_norm_g", "ffn2_norm")


def _small_params(p):
    out = {n: p[n].reshape(1, D) for n in ("ffn1_norm", "mix_norm", "ffn2_norm", "sgu_ln_g", "sgu_ln_b", "q_norm_g", "k_norm_g")}
    out["b_forget"] = jnp.pad(p["b_forget"].reshape(1, NH), ((0, 0), (0, 128 - NH)))
    out["b_gate"] = p["b_gate"]
    out["conv_w"] = p["conv_w"]
    out["sgu_w"] = p["sgu_w"]
    out["bmap"] = jnp.repeat(p["sgu_b"].T, HD, axis=1)
    return out


def _small_grads_natural(sg):
    out = {n: sg[n].reshape(D) for n in ("ffn1_norm", "mix_norm", "ffn2_norm", "sgu_ln_g", "sgu_ln_b")}
    out["q_norm_g"] = sg["q_norm_g"].reshape(NH, HD)
    out["k_norm_g"] = sg["k_norm_g"].reshape(NH, HD)
    out["b_forget"] = sg["b_forget"][0, :NH]
    out["b_gate"] = sg["b_gate"]
    out["conv_w"] = sg["conv_w"]
    out["sgu_w"] = sg["sgu_w"]
    out["sgu_b"] = sg["sgu_b"]
    return out


def _sq_fwd(name, a, wsq, l, n, res=None):
    T = a.shape[0]
    tm = _tile(T, 512)
    return _mm(name, a, wsq, grid=(T // tm, 1, 1), a_spec=_bs((tm, D), lambda i, j, k: (i, 0)),
               b_spec=_bs((None, None, D, D), lambda i, j, k: (l, n, 0, 0)),
               out_shape=S((T, D), f32), out_spec=_bs((tm, D), lambda i, j, k: (i, 0)), dims=NN, acc_shape=None,
               res=res, res_spec=_bs((tm, D), lambda i, j, k: (i, 0)))


def _sq_bwd_in(name, dy, wsq, l, n):
    T = dy.shape[0]
    tm = _tile(T, 512)
    return _mm(name, dy, wsq, grid=(T // tm, 1, 1), a_spec=_bs((tm, D), lambda i, j, k: (i, 0)),
               b_spec=_bs((None, None, D, D), lambda i, j, k: (l, n, 0, 0)),
               out_shape=S((T, D), f32), out_spec=_bs((tm, D), lambda i, j, k: (i, 0)), dims=NT, acc_shape=None)


def _sq_bwd_w(name, a, dy, gbuf, l):
    T = a.shape[0]
    return _mm(name, a, dy, grid=(NDEV, 1, 1), a_spec=_bs((T, 128), lambda i, j, k: (0, i)),
               b_spec=_bs((T, D), lambda i, j, k: (0, 0)), out_shape=S(gbuf.shape, bf16),
               out_spec=_bs((None, None, None, 128, D), lambda i, j, k: (i % 2, i // 2, l, 0, 0)),
               dims=TN, acc_shape=None, alias=gbuf)


def _ffn_fwd(tag, x, g, wgu, wd, l):
    T = x.shape[0]
    tm = _tile(T, 1024)
    h = _rms_fwd(tag + "_rms", x, g)
    gu = _mm(tag + "_gu", h, wgu, grid=(T // tm, NDEV, 1), a_spec=_bs((tm, D), lambda i, j, k: (i, 0)),
             b_spec=_bs((None, None, D, GU), lambda i, j, k: (l, j, 0, 0)), out_shape=S((NDEV, T, GU), f32),
             out_spec=_bs((None, tm, GU), lambda i, j, k: (j, i, 0)), dims=NN, acc_shape=None)
    a = _swiglu_fwd(tag + "_act", gu)
    tm = _tile(T, 1024)
    xo = _mm(tag + "_down", a, wd, grid=(T // tm, 1, 4), a_spec=_bs((None, tm, GU), lambda i, j, k: (k, i, 0)),
             b_spec=_bs((None, None, GU, D), lambda i, j, k: (l, k, 0, 0)), out_shape=S((T, D), f32),
             out_spec=_bs((tm, D), lambda i, j, k: (i, 0)), dims=NN, acc_shape=(tm, D), res=x,
             res_spec=_bs((tm, D), lambda i, j, k: (i, 0)), alpha=0.5)
    return xo, (h, gu, a)


def _ffn_bwd(tag, dxo, x, g, wgu, wd, l, saved, g_gu, g_d):
    h, gu, a = saved
    T = x.shape[0]
    tm = _tile(T, 512)
    da = _mm(tag + "_dact", dxo, wd, grid=(T // tm, 4, 1), a_spec=_bs((tm, D), lambda i, j, k: (i, 0)),
             b_spec=_bs((None, None, GU, D), lambda i, j, k: (l, j, 0, 0)), out_shape=S((4, T, GU), f32),
             out_spec=_bs((None, tm, GU), lambda i, j, k: (j, i, 0)), dims=NT, acc_shape=None, alpha=0.5)
    g_d = _mm(tag + "_dwd", a, dxo, grid=(4, 1, 1), a_spec=_bs((None, T, GU), lambda i, j, k: (i, 0, 0)),
              b_spec=_bs((T, D), lambda i, j, k: (0, 0)), out_shape=S(g_d.shape, bf16),
              out_spec=_bs((2, None, None, GU // 2, D), lambda i, j, k: (0, i, l, 0, 0)), dims=TN, acc_shape=None,
              alpha=0.5, alias=g_d, split_rows=GU // 2)
    dgu = _swiglu_bwd(tag + "_dgu", gu, da).reshape(NDEV, T, GU)
    dh = _mm(tag + "_dh", dgu, wgu, grid=(1, 1, NDEV), a_spec=_bs((None, T, GU), lambda i, j, k: (k, 0, 0)),
             b_spec=_bs((None, None, D, GU), lambda i, j, k: (l, k, 0, 0)), out_shape=S((T, D), f32),
             out_spec=_bs((T, D), lambda i, j, k: (0, 0)), dims=NT, acc_shape=(T, D))
    g_gu = _mm(tag + "_dwgu", h, dgu, grid=(1, NDEV, 1), a_spec=_bs((T, D), lambda i, j, k: (0, 0)),
               b_spec=_bs((None, T, GU), lambda i, j, k: (j, 0, 0)), out_shape=S(g_gu.shape, bf16),
               out_spec=_bs((None, None, None, D, GU), lambda i, j, k: (j % 2, j // 2, l, 0, 0)), dims=TN,
               acc_shape=None, alias=g_gu)
    dx, dg = _rms_bwd(tag + "_drms", dh, x, g, dxo)
    return dx, dg, g_gu, g_d


def _mixer_fwd(tag, x, p, win, wsq, l):
    T = x.shape[0]
    h = _rms_fwd(tag + "_rms", x, p["mix_norm"])
    tn = 512
    z = _mm(tag + "_in", h, win, grid=(1, NZ // tn, 1), a_spec=_bs((T, D), lambda i, j, k: (0, 0)),
            b_spec=_bs((D, tn), lambda i, j, k: (0, j)), out_shape=S((T, NZ), f32),
            out_spec=_bs((T, tn), lambda i, j, k: (0, j)), dims=NN, acc_shape=None)
    ya_in = _conv_fwd(tag + "_conv", z, p["conv_w"])
    yb_in = _sgu_fwd(tag + "_sgu", z, p["sgu_ln_g"], p["sgu_ln_b"], p["sgu_w"], p["bmap"])
    qn, kn, vb, logf = _qk_fwd(tag + "_qk", z, p["q_norm_g"], p["k_norm_g"], p["b_forget"])
    ccol, crow = _cum_fwd(tag + "_cum", logf)
    crow3 = crow.reshape(NH, 1, T)
    o, lse = _attn_fwd(tag + "_attn", qn, kn, vb, ccol, crow3)
    ya = _sq_fwd(tag + "_oconv", ya_in, wsq, l, 0)
    yb = _sq_fwd(tag + "_osgu", yb_in, wsq, l, 1)
    yc = _sq_fwd(tag + "_oattn", o, wsq, l, 2)
    merged = _merge_fwd(tag + "_merge", ya, yb, yc, z, p["b_gate"])
    xo = _sq_fwd(tag + "_o", merged, wsq, l, 3, res=x)
    return xo, (h, z, ya_in, yb_in, qn, kn, vb, ccol, crow3, o, lse, ya, yb, yc, merged)


def _mixer_bwd(tag, dxo, x, p, win, wsq, l, saved, gsq):
    h, z, ya_in, yb_in, qn, kn, vb, ccol, crow3, o, lse, ya, yb, yc, merged = saved
    T = x.shape[0]
    sg = {}
    dm = _sq_bwd_in(tag + "_dmerged", dxo, wsq, l, 3)
    gsq[3] = _sq_bwd_w(tag + "_dwo", merged, dxo, gsq[3], l)
    dya, dyb, dyc, dz_g, sg["b_gate"] = _merge_bwd(tag + "_dmerge", dm, ya, yb, yc, z, p["b_gate"])
    d_ya_in = _sq_bwd_in(tag + "_dconv_in", dya, wsq, l, 0)
    gsq[0] = _sq_bwd_w(tag + "_dwoc", ya_in, dya, gsq[0], l)
    d_yb_in = _sq_bwd_in(tag + "_dsgu_in", dyb, wsq, l, 1)
    gsq[1] = _sq_bwd_w(tag + "_dwos", yb_in, dyb, gsq[1], l)
    d_o = _sq_bwd_in(tag + "_dattn_in", dyc, wsq, l, 2)
    gsq[2] = _sq_bwd_w(tag + "_dwoa", o, dyc, gsq[2], l)
    dz_c, sg["conv_w"] = _conv_bwd(tag + "_dconv", d_ya_in, z, p["conv_w"])
    dz_s, sg["sgu_ln_g"], sg["sgu_ln_b"], sg["sgu_w"], db_t = _sgu_bwd(
        tag + "_dsgu", d_yb_in, z, p["sgu_ln_g"], p["sgu_ln_b"], p["sgu_w"], p["bmap"])
    sg["sgu_b"] = db_t.T
    dqn, dkn, dv, cs = _attn_bwd(tag + "_dattn", qn, kn, vb, d_o, o, lse, ccol, crow3)
    cs_t = jnp.pad(cs[:, 0, :].T, ((0, 0), (0, 128 - NH)))
    dz_f, sg["b_forget"] = _forget_bwd(tag + "_dforget", cs_t, z, p["b_forget"])
    dz_q, sg["q_norm_g"], sg["k_norm_g"] = _qk_bwd(tag + "_dqk", dqn, dkn, dv, z, p["q_norm_g"], p["k_norm_g"])
    dz = jnp.concatenate([dz_c, dz_s, dz_q, dz_g, dz_f], axis=1)
    tk = 512
    dh = _mm(tag + "_dh", dz, win, grid=(1, 1, NZ // tk), a_spec=_bs((T, tk), lambda i, j, k: (0, k)),
             b_spec=_bs((D, tk), lambda i, j, k: (0, k)), out_shape=S((T, D), f32),
             out_spec=_bs((T, D), lambda i, j, k: (0, 0)), dims=NT, acc_shape=(T, D))
    tn = 512
    dwin = _mm(tag + "_dwin", h, dz, grid=(1, NZ // tn, 1), a_spec=_bs((T, D), lambda i, j, k: (0, 0)),
               b_spec=_bs((T, tn), lambda i, j, k: (0, j)), out_shape=S((D, NZ), bf16),
               out_spec=_bs((D, tn), lambda i, j, k: (0, j)), dims=TN, acc_shape=None)
    dx, sg["mix_norm"] = _rms_bwd(tag + "_drms", dh, x, p["mix_norm"], dxo)
    return dx, sg, dwin, gsq


def _local_step(x, tgt, small, wts, gbufs):
    saved = []
    for l in range(2):
        p = small[l]
        x1, s1 = _ffn_fwd("ffn1", x, p["ffn1_norm"], wts["gu1"], wts["d1"], l)
        x2, sm = _mixer_fwd("mix", x1, p, wts["win"][l], wts["sq"], l)
        x3, s2 = _ffn_fwd("ffn2", x2, p["ffn2_norm"], wts["gu2"], wts["d2"], l)
        saved.append((x, x1, x2, s1, sm, s2))
        x = x3
    loss_row, dx = _loss("loss", x, tgt)
    sgrads = [None, None]
    dwin = [None, None]
    gsq = [gbufs["oc"], gbufs["os"], gbufs["oa"], gbufs["o"]]
    g_gu1, g_d1, g_gu2, g_d2 = gbufs["gu1"], gbufs["d1"], gbufs["gu2"], gbufs["d2"]
    for l in (1, 0):
        p = small[l]
        x0, x1, x2, s1, sm, s2 = saved[l]
        dx, dn2, g_gu2, g_d2 = _ffn_bwd("ffn2", dx, x2, p["ffn2_norm"], wts["gu2"], wts["d2"], l, s2, g_gu2, g_d2)
        dx, sg, dwin[l], gsq = _mixer_bwd("mix", dx, x1, p, wts["win"][l], wts["sq"], l, sm, gsq)
        dx, dn1, g_gu1, g_d1 = _ffn_bwd("ffn1", dx, x0, p["ffn1_norm"], wts["gu1"], wts["d1"], l, s1, g_gu1, g_d1)
        sg["ffn1_norm"] = dn1
        sg["ffn2_norm"] = dn2
        sgrads[l] = sg
    gout = {"gu1": g_gu1, "d1": g_d1, "gu2": g_gu2, "d2": g_d2, "oc": gsq[0], "os": gsq[1], "oa": gsq[2], "o": gsq[3]}
    return loss_row, dx, sgrads, dwin, gout


ANY = pl.BlockSpec(memory_space=pl.ANY)


def _place():
    return lax.axis_index("x"), lax.axis_index("y"), lax.axis_index("c")


def _all_gather(name, arrs):
    n = len(arrs)
    units = [(a, l) for a in range(n) for l in range(2)]
    nu = len(units)

    def body(*refs):
        ins, outs = refs[:n], refs[n:2 * n]
        send, recv, lsem = refs[2 * n:]
        x, y, c = _place()
        me, sib = (x, y, c), (x, y, 1 - c)
        chips = [(1 - x, y), (x, 1 - y), (1 - x, 1 - y)]

        def blk(u, px, py, pc):
            a, l = units[u]
            return outs[a].at[l, 4 * px + 2 * py + pc]

        def copy(u, k, block, to, src=None):
            dst = blk(u, *block)
            return pltpu.make_async_remote_copy(src_ref=dst if src is None else src, dst_ref=dst,
                                                send_sem=send.at[u * 7 + k], recv_sem=recv.at[u * 7 + k],
                                                device_id=to, device_id_type=MESH)

        mine = [pltpu.make_async_copy(ins[a].at[l], blk(u, *me), lsem.at[u]) for u, (a, l) in enumerate(units)]
        for cp in mine:
            cp.start()
        first = []
        for u, (a, l) in enumerate(units):
            first.append(copy(u, 0, me, sib, ins[a].at[l]))
            for j, chip in enumerate(chips):
                first.append(copy(u, 1 + j, me, (*chip, c), ins[a].at[l]))
        for cp in first:
            cp.start()
        passed = []
        for u in range(nu):
            for j, chip in enumerate(chips):
                copy(u, 1 + j, (*chip, c), me).wait_recv()
                fwd = copy(u, 4 + j, (*chip, c), sib)
                fwd.start()
                passed.append(fwd)
        for u in range(nu):
            copy(u, 0, sib, me).wait_recv()
            for j, chip in enumerate(chips):
                copy(u, 4 + j, (*chip, 1 - c), me).wait_recv()
        for cp in first + passed:
            cp.wait_send()
        for cp in mine:
            cp.wait()

    return pl.pallas_call(
        body, name=name, in_specs=[ANY] * n, out_specs=[ANY] * n,
        out_shape=[S((2, NDEV) + a.shape[1:], a.dtype) for a in arrs],
        scratch_shapes=[pltpu.SemaphoreType.DMA((nu * 7,)), pltpu.SemaphoreType.DMA((nu * 7,)),
                        pltpu.SemaphoreType.DMA((nu,))],
    )(*arrs)


def _rs_pair(name, gs):
    n = len(gs)

    def body(*refs):
        ins, outs = refs[:n], refs[n:2 * n]
        send, recv = refs[2 * n:]
        x, y, c = _place()
        cps = [pltpu.make_async_remote_copy(src_ref=ins[u].at[1 - c], dst_ref=outs[u], send_sem=send.at[u],
                                            recv_sem=recv.at[u], device_id=(x, y, 1 - c), device_id_type=MESH)
               for u in range(n)]
        for cp in cps:
            cp.start()
        for cp in cps:
            cp.wait()

    return pl.pallas_call(
        body, name=name, in_specs=[ANY] * n, out_specs=[ANY] * n, out_shape=[S(g.shape[1:], g.dtype) for g in gs],
        scratch_shapes=[pltpu.SemaphoreType.DMA((n,)), pltpu.SemaphoreType.DMA((n,))],
    )(*gs)


def _row_tile(r, c):
    return 128 if (r % 128 == 0 and c > D) else (256 if r % 256 == 0 else r)


def _pair_sum(name, core, g, r1):
    _, nq, nl, r, c = g.shape
    tr = _row_tile(r, c)
    g4 = g.reshape(2, nq * nl, r, c)
    r3 = r1.reshape(nq * nl, r, c)

    def body(core_ref, g_ref, r_ref, o_ref):
        o_ref[...] = (g_ref[...].astype(f32) + r_ref[...].astype(f32)).astype(bf16)

    out = pl.pallas_call(
        body, name=name,
        grid_spec=pltpu.PrefetchScalarGridSpec(
            num_scalar_prefetch=1, grid=(nq * nl, r // tr),
            in_specs=[_bs((None, None, tr, c), lambda b, i, cr: (cr[0], b, i, 0)), _bs((None, tr, c), lambda b, i, cr: (b, i, 0))],
            out_specs=_bs((None, tr, c), lambda b, i, cr: (b, i, 0))),
        out_shape=S((nq * nl, r, c), bf16), compiler_params=_cp("parallel", "parallel"))(core, g4, r3)
    return out.reshape(nq, nl, r, c)


def _rs_chips(name, ss):
    n = len(ss)

    def body(*refs):
        ins, outs = refs[:n], refs[n:2 * n]
        send, recv = refs[2 * n:]
        x, y, c = _place()
        chips = [(1 - x, y), (x, 1 - y), (1 - x, 1 - y)]
        cps = []
        for u in range(n):
            for k, chip in enumerate(chips):
                cps.append(pltpu.make_async_remote_copy(
                    src_ref=ins[u].at[2 * chip[0] + chip[1]], dst_ref=outs[u].at[k], send_sem=send.at[u * 3 + k],
                    recv_sem=recv.at[u * 3 + k], device_id=(*chip, c), device_id_type=MESH))
        for cp in cps:
            cp.start()
        for cp in cps:
            cp.wait()

    return pl.pallas_call(
        body, name=name, in_specs=[ANY] * n, out_specs=[ANY] * n, out_shape=[S((3,) + s.shape[1:], s.dtype) for s in ss],
        scratch_shapes=[pltpu.SemaphoreType.DMA((n * 3,)), pltpu.SemaphoreType.DMA((n * 3,))],
    )(*ss)


def _all_reduce_small(name, v):
    r = v.shape[0]

    def body(v_ref, o_ref, buf, send, recv, lsem):
        x, y, c = _place()
        me, sib = (x, y, c), (x, y, 1 - c)
        chips = [(1 - x, y), (x, 1 - y), (1 - x, 1 - y)]

        def slot(px, py, pc):
            return buf.at[4 * px + 2 * py + pc]

        def copy(k, block, to, src=None):
            return pltpu.make_async_remote_copy(src_ref=slot(*block) if src is None else src, dst_ref=slot(*block),
                                                send_sem=send.at[k], recv_sem=recv.at[k], device_id=to,
                                                device_id_type=MESH)

        mine = pltpu.make_async_copy(v_ref, slot(*me), lsem)
        mine.start()
        first = [copy(0, me, sib, v_ref)] + [copy(1 + j, me, (*chip, c), v_ref) for j, chip in enumerate(chips)]
        for cp in first:
            cp.start()
        passed = [copy(4 + j, (*chip, c), sib) for j, chip in enumerate(chips)]
        for j, chip in enumerate(chips):
            copy(1 + j, (*chip, c), me).wait_recv()
            passed[j].start()
        copy(0, sib, me).wait_recv()
        for j, chip in enumerate(chips):
            copy(4 + j, (*chip, 1 - c), me).wait_recv()
        for cp in first + passed:
            cp.wait_send()
        mine.wait()
        acc = buf[0]
        for d in range(1, NDEV):
            acc = acc + buf[d]
        o_ref[...] = acc

    vm = pl.BlockSpec(memory_space=pltpu.VMEM)
    return pl.pallas_call(
        body, name=name, in_specs=[vm], out_specs=vm, out_shape=S((r, 128), f32),
        scratch_shapes=[pltpu.VMEM((NDEV, r, 128), f32), pltpu.SemaphoreType.DMA((7,)), pltpu.SemaphoreType.DMA((7,)),
                        pltpu.SemaphoreType.DMA],
        compiler_params=pltpu.CompilerParams(vmem_limit_bytes=VMEM_LIMIT),
    )(v)


def _adam_math(w, g, m, v):
    m = ADAM_B1 * m + (1.0 - ADAM_B1) * g
    v = ADAM_B2 * v + (1.0 - ADAM_B2) * (g * g)
    m_hat = m / (1.0 - ADAM_B1 ** ADAM_STEP)
    v_hat = v / (1.0 - ADAM_B2 ** ADAM_STEP)
    delta = -ADAM_LR * (m_hat / (jnp.sqrt(v_hat) + ADAM_EPS) + ADAM_WD * w)
    return delta, m, v


def _adamw(name, w, m, v, s_mine, r2):
    _, r, c = w.shape
    tr = _row_tile(r, c)

    def body(w_ref, m_ref, v_ref, s_ref, r0_ref, r1_ref, r2_ref, g_ref, d_ref, mo_ref, vo_ref):
        g = ((s_ref[...].astype(f32) + r0_ref[...].astype(f32)) + r1_ref[...].astype(f32)) + r2_ref[...].astype(f32)
        g_ref[...] = g
        d_ref[...], mo_ref[...], vo_ref[...] = _adam_math(w_ref[...], g, m_ref[...], v_ref[...])

    blk = _bs((None, tr, c), lambda l, i: (l, i, 0))
    part = lambda k: _bs((None, None, tr, c), lambda l, i: (k, l, i, 0))
    return pl.pallas_call(
        body, name=name, grid=(2, r // tr), in_specs=[blk, blk, blk, blk, part(0), part(1), part(2)],
        out_specs=[blk] * 4, out_shape=[S(w.shape, f32)] * 4, compiler_params=_cp("parallel", "parallel"),
    )(w, m, v, s_mine, r2, r2, r2)


def _adamw_small(name, w, g, m, v):
    def body(w_ref, g_ref, m_ref, v_ref, d_ref, mo_ref, vo_ref):
        d_ref[...], mo_ref[...], vo_ref[...] = _adam_math(w_ref[...], g_ref[...], m_ref[...], v_ref[...])

    return pl.pallas_call(body, name=name, out_shape=[S(w.shape, f32)] * 3,
                          compiler_params=pltpu.CompilerParams(vmem_limit_bytes=VMEM_LIMIT))(w, g, m, v)


WEIGHT_NAMES = ("ffn1_norm", "ffn1_w_gu", "ffn1_w_down", "mix_norm", "w_in", "b_forget", "b_gate", "conv_w", "sgu_ln_g",
                "sgu_ln_b", "sgu_w", "sgu_b", "q_norm_g", "k_norm_g", "w_out_conv", "w_out_sgu", "w_out_attn", "w_o",
                "ffn2_norm", "ffn2_w_gu", "ffn2_w_down")
BIG = {"ffn1_w_gu": "gu1", "ffn2_w_gu": "gu2", "ffn1_w_down": "d1", "ffn2_w_down": "d2", "w_in": "in",
       "w_out_conv": "oc", "w_out_sgu": "os", "w_out_attn": "oa", "w_o": "o"}
BIG_KEYS = ("gu1", "gu2", "d1", "d2", "in", "oc", "os", "oa", "o")
REPLICATED_SMALL = ("ffn1_norm", "mix_norm", "b_forget", "sgu_ln_g", "sgu_ln_b", "sgu_w", "sgu_b", "q_norm_g",
                    "k_norm_g", "ffn2_norm")
SHARDED_SMALL = ("b_gate", "conv_w")


def _pack(arrays):
    flat = jnp.concatenate([a.reshape(-1).astype(f32) for a in arrays])
    rows = -(-flat.shape[0] // 1024) * 8
    return jnp.pad(flat, (0, rows * 128 - flat.shape[0])).reshape(rows, 128)


def _unpack(packed, shapes):
    flat = packed.reshape(-1)
    out, pos = [], 0
    for shp in shapes:
        size = 1
        for s_ in shp:
            size *= s_
        out.append(flat[pos:pos + size].reshape(shp))
        pos += size
    return out


def _natural_runs(a, b):
    runs = []
    while a < b:
        d = a // INB
        e = min(b, (d + 1) * INB)
        runs.append((d, a - d * INB, e - d * INB))
        a = e
    return runs


def _win_kernel_layout(wg):
    runs = _natural_runs(0, GATE_OFF) + _natural_runs(GATE_OFF + NH, NIN) + _natural_runs(GATE_OFF, GATE_OFF + NH)
    return jnp.concatenate([wg[d, :, a:b] for d, a, b in runs] + [jnp.zeros((D, NZ - NIN), wg.dtype)], axis=1)


def _kernel_column(n):
    return n if n < GATE_OFF else (F_OFF + n - GATE_OFF if n < GATE_OFF + NH else n - NH)


def _win_device_block(dw, d):
    cuts = sorted({d * INB, (d + 1) * INB} | {c for c in (GATE_OFF, GATE_OFF + NH) if d * INB < c < (d + 1) * INB})
    parts = [dw[:, _kernel_column(a):_kernel_column(a) + (b - a)] for a, b in zip(cuts[:-1], cuts[1:])]
    return parts[0] if len(parts) == 1 else jnp.concatenate(parts, axis=1)


def kernel(x, ffn1_norm, ffn1_w_gu, ffn1_w_down, mix_norm, w_in, b_forget, b_gate, conv_w, sgu_ln_g, sgu_ln_b, sgu_w, sgu_b, q_norm_g, k_norm_g, w_out_conv, w_out_sgu, w_out_attn, w_o, ffn2_norm, ffn2_w_gu, ffn2_w_down, loss_target, m_ffn1_norm, m_ffn1_w_gu, m_ffn1_w_down, m_mix_norm, m_w_in, m_b_forget, m_b_gate, m_conv_w, m_sgu_ln_g, m_sgu_ln_b, m_sgu_w, m_sgu_b, m_q_norm_g, m_k_norm_g, m_w_out_conv, m_w_out_sgu, m_w_out_attn, m_w_o, m_ffn2_norm, m_ffn2_w_gu, m_ffn2_w_down, v_ffn1_norm, v_ffn1_w_gu, v_ffn1_w_down, v_mix_norm, v_w_in, v_b_forget, v_b_gate, v_conv_w, v_sgu_ln_g, v_sgu_ln_b, v_sgu_w, v_sgu_b, v_q_norm_g, v_k_norm_g, v_w_out_conv, v_w_out_sgu, v_w_out_attn, v_w_o, v_ffn2_norm, v_ffn2_w_gu, v_ffn2_w_down):
    w = dict(zip(WEIGHT_NAMES, (ffn1_norm, ffn1_w_gu, ffn1_w_down, mix_norm, w_in, b_forget, b_gate, conv_w, sgu_ln_g,
                                sgu_ln_b, sgu_w, sgu_b, q_norm_g, k_norm_g, w_out_conv, w_out_sgu, w_out_attn, w_o,
                                ffn2_norm, ffn2_w_gu, ffn2_w_down)))
    mom = dict(zip(WEIGHT_NAMES, (m_ffn1_norm, m_ffn1_w_gu, m_ffn1_w_down, m_mix_norm, m_w_in, m_b_forget, m_b_gate,
                                  m_conv_w, m_sgu_ln_g, m_sgu_ln_b, m_sgu_w, m_sgu_b, m_q_norm_g, m_k_norm_g,
                                  m_w_out_conv, m_w_out_sgu, m_w_out_attn, m_w_o, m_ffn2_norm, m_ffn2_w_gu,
                                  m_ffn2_w_down)))
    var = dict(zip(WEIGHT_NAMES, (v_ffn1_norm, v_ffn1_w_gu, v_ffn1_w_down, v_mix_norm, v_w_in, v_b_forget, v_b_gate,
                                  v_conv_w, v_sgu_ln_g, v_sgu_ln_b, v_sgu_w, v_sgu_b, v_q_norm_g, v_k_norm_g,
                                  v_w_out_conv, v_w_out_sgu, v_w_out_attn, v_w_o, v_ffn2_norm, v_ffn2_w_gu,
                                  v_ffn2_w_down)))
    px, py, pc = _place()
    dev = 4 * px + 2 * py + pc
    chip = 2 * px + py

    big_names = [n for n in WEIGHT_NAMES if n in BIG]
    local = [w[n].astype(bf16) for n in big_names]
    local.append(jnp.concatenate([w["b_gate"], w["conv_w"], jnp.zeros((2, 2, 128), f32)], axis=1))
    gathered = _all_gather("gather_weights", local)
    gw = {BIG[n]: g for n, g in zip(big_names, gathered[:-1])}
    gsmall = gathered[-1]
    wts = {
        "gu1": gw["gu1"], "gu2": gw["gu2"],
        "d1": gw["d1"].reshape(2, 4, GU, D), "d2": gw["d2"].reshape(2, 4, GU, D),
        "win": [_win_kernel_layout(gw["in"][l]) for l in range(2)],
        "sq": jnp.stack([gw[k].reshape(2, D, D) for k in ("oc", "os", "oa", "o")], axis=1),
    }
    full_small = {"b_gate": jnp.transpose(gsmall[:, :, 0:3, :], (0, 2, 1, 3)).reshape(2, 3, D),
                  "conv_w": jnp.transpose(gsmall[:, :, 3:6, :], (0, 2, 1, 3)).reshape(2, 3, D)}
    small = []
    for l in range(2):
        p = {n: w[n][l] for n in REPLICATED_SMALL}
        p.update({n: full_small[n][l] for n in SHARDED_SMALL})
        small.append(_small_params(p))

    shapes = {"gu1": (D, GU), "gu2": (D, GU), "d1": (GU // 2, D), "d2": (GU // 2, D), "oc": (128, D), "os": (128, D),
              "oa": (128, D), "o": (128, D)}
    gbufs = {k: lax.empty((2, 4, 2) + shp, bf16) for k, shp in shapes.items()}
    loss_row, dx, sgrads, dwin, gout = _local_step(x[0], loss_target[0], small, wts, gbufs)
    gout["in"] = jnp.stack([jnp.stack([jnp.stack([_win_device_block(dwin[l], 2 * q + c) for l in range(2)])
                                       for q in range(4)]) for c in range(2)])

    gs = [gout[k] for k in BIG_KEYS]
    r1 = _rs_pair("rs_pair", gs)
    core = pc.reshape(1).astype(jnp.int32)
    ss = [_pair_sum("pair_sum_" + k, core, g, r) for k, g, r in zip(BIG_KEYS, gs, r1)]
    r2 = _rs_chips("rs_chips", ss)

    grads, delta, new_m, new_v = {}, {}, {}, {}
    for n in big_names:
        i = BIG_KEYS.index(BIG[n])
        s_mine = lax.dynamic_index_in_dim(ss[i], chip, 0, keepdims=False)
        grads[n], delta[n], new_m[n], new_v[n] = _adamw("adamw_" + BIG[n], w[n], mom[n], var[n], s_mine, r2[i])

    nat = [_small_grads_natural(sgrads[l]) for l in range(2)]
    order = REPLICATED_SMALL + SHARDED_SMALL
    part = _pack([jnp.stack([nat[0][n], nat[1][n]]) for n in order] + [loss_row[0, 0:1]])
    total = _all_reduce_small("reduce_small", part)
    full_shapes = [(2,) + tuple(nat[0][n].shape) for n in order] + [(1,)]
    summed = dict(zip(order + ("loss",), _unpack(total, full_shapes)))
    for n in REPLICATED_SMALL:
        grads[n] = summed[n]
    for n in SHARDED_SMALL:
        grads[n] = lax.dynamic_slice_in_dim(summed[n], dev * 128, 128, axis=2)
    wp = _pack([w[n] for n in order])
    gp = _pack([grads[n] for n in order])
    mp = _pack([mom[n] for n in order])
    vp = _pack([var[n] for n in order])
    dpk, mpk, vpk = _adamw_small("adamw_small", wp, gp, mp, vp)
    local_shapes = [tuple(w[n].shape) for n in order]
    for dst, packed in ((delta, dpk), (new_m, mpk), (new_v, vpk)):
        dst.update(zip(order, _unpack(packed, local_shapes)))

    loss = summed["loss"][0]
    return (loss, dx[None], *[grads[n] for n in WEIGHT_NAMES], *[delta[n] for n in WEIGHT_NAMES],
            *[new_m[n] for n in WEIGHT_NAMES], *[new_v[n] for n in WEIGHT_NAMES])
```

```python
import functools

import jax
import jax.numpy as jnp
from jax import lax
from jax.experimental import pallas as pl
from jax.experimental.pallas import tpu as pltpu

f32 = jnp.float32
bf16 = jnp.bfloat16
S = jax.ShapeDtypeStruct
MESH = pl.DeviceIdType.MESH

D = 1024
NH = 8
HD = 128
NDEV = 8
GU = 704
NIN = 11272
INB = 1409
GATE_OFF = 8192
F_OFF = 11264
NZ = 11776
RMS_EPS = 1e-6
LN_EPS = 1e-5
ATT_SCALE = HD ** -0.5
NEG = -1e30
INV_SQRT2 = 0.7071067811865476
INV_SQRT2PI = 0.3989422804014327

ADAM_LR = 0.001
ADAM_B1 = 0.9
ADAM_B2 = 0.999
ADAM_EPS = 1e-08
ADAM_WD = 0.01
ADAM_STEP = 10

TT = 256
VMEM_LIMIT = 56 * 1024 * 1024


def _cp(*sem):
    return pltpu.CompilerParams(dimension_semantics=sem, vmem_limit_bytes=VMEM_LIMIT)


def _bs(shape, fn):
    return pl.BlockSpec(shape, fn)


NN = (((1,), (0,)), ((), ()))
NT = (((1,), (1,)), ((), ()))
TN = (((0,), (0,)), ((), ()))


def _mm(name, a, b, *, grid, a_spec, b_spec, out_shape, out_spec, dims, acc_shape, res=None, res_spec=None,
        alpha=1.0, alias=None, split_rows=None):
    nk = grid[2]

    def body(*refs):
        a_ref, b_ref = refs[0], refs[1]
        pos = 2
        res_ref = None
        if res is not None:
            res_ref = refs[pos]
            pos += 1
        if alias is not None:
            pos += 1
        o_ref = refs[pos]
        acc_ref = refs[pos + 1] if nk > 1 else None
        part = lax.dot_general(a_ref[...].astype(bf16), b_ref[...].astype(bf16), dims, preferred_element_type=f32)

        def finish(acc):
            if alpha != 1.0:
                acc = alpha * acc
            if res_ref is not None:
                acc = res_ref[...] + acc
            if split_rows is None:
                o_ref[...] = acc.astype(o_ref.dtype)
            else:
                o_ref[0] = acc[:split_rows].astype(o_ref.dtype)
                o_ref[1] = acc[split_rows:].astype(o_ref.dtype)

        if nk == 1:
            finish(part)
        else:
            k = pl.program_id(2)

            @pl.when(k == 0)
            def _():
                acc_ref[...] = part

            @pl.when(k > 0)
            def _():
                acc_ref[...] += part

            @pl.when(k == nk - 1)
            def _():
                finish(acc_ref[...])

    operands = [a, b]
    in_specs = [a_spec, b_spec]
    if res is not None:
        operands.append(res)
        in_specs.append(res_spec)
    aliases = {}
    if alias is not None:
        aliases = {len(operands): 0}
        operands.append(alias)
        in_specs.append(pl.BlockSpec(memory_space=pl.ANY))
    return pl.pallas_call(
        body, name=name, grid=grid, in_specs=in_specs, out_specs=out_spec, out_shape=out_shape,
        scratch_shapes=[pltpu.VMEM(acc_shape, f32)] if nk > 1 else [],
        input_output_aliases=aliases,
        compiler_params=_cp("parallel", "parallel", "arbitrary"),
    )(*operands)


def _tile(n, t):
    return t if n % t == 0 and n >= t else n


def _row(cb=0, w=D):
    return _bs((TT, w), lambda i: (i, cb))


def _vec(rows=1, w=D):
    return _bs((rows, w), lambda i: (0, 0))


def _acc_store(i, ref, val):
    @pl.when(i == 0)
    def _():
        ref[...] = val

    @pl.when(i > 0)
    def _():
        ref[...] += val


def _rms_fwd(name, x, g):
    T = x.shape[0]

    def body(x_ref, g_ref, o_ref):
        xv = x_ref[...]
        r = lax.rsqrt(jnp.mean(xv * xv, axis=-1, keepdims=True) + RMS_EPS)
        o_ref[...] = (xv * r * g_ref[...]).astype(bf16)

    return pl.pallas_call(body, name=name, grid=(T // TT,), in_specs=[_row(), _vec()], out_specs=_row(),
                          out_shape=S((T, D), bf16), compiler_params=_cp("parallel"))(x, g)


def _rms_bwd(name, dh, x, g, dres):
    T = x.shape[0]

    def body(dh_ref, x_ref, g_ref, dres_ref, dx_ref, dg_ref):
        i = pl.program_id(0)
        xv = x_ref[...]
        r = lax.rsqrt(jnp.mean(xv * xv, axis=-1, keepdims=True) + RMS_EPS)
        xhat = xv * r
        dh_v = dh_ref[...]
        dyg = dh_v * g_ref[...]
        m = jnp.mean(dyg * xhat, axis=-1, keepdims=True)
        dx_ref[...] = dres_ref[...] + r * (dyg - xhat * m)
        _acc_store(i, dg_ref, jnp.sum(dh_v * xhat, axis=0, keepdims=True))

    return pl.pallas_call(body, name=name, grid=(T // TT,), in_specs=[_row(), _row(), _vec(), _row()],
                          out_specs=[_row(), _vec()], out_shape=[S((T, D), f32), S((1, D), f32)],
                          compiler_params=_cp("arbitrary"))(dh, x, g, dres)


def _sigmoid(x):
    return 1.0 / (1.0 + jnp.exp(-x))


def _swiglu_fwd(name, gu):
    T = gu.shape[1]

    def body(g_ref, u_ref, o_ref):
        g = g_ref[...]
        o_ref[...] = (g * _sigmoid(g) * u_ref[...]).astype(bf16)

    return pl.pallas_call(
        body, name=name, grid=(4, T // TT),
        in_specs=[_bs((None, TT, GU), lambda j, i: (j, i, 0)), _bs((None, TT, GU), lambda j, i: (j + 4, i, 0))],
        out_specs=_bs((None, TT, GU), lambda j, i: (j, i, 0)), out_shape=S((4, T, GU), bf16),
        compiler_params=_cp("parallel", "parallel"))(gu, gu)


def _swiglu_bwd(name, gu, da):
    T = gu.shape[1]

    def body(g_ref, u_ref, da_ref, o_ref):
        g = g_ref[...]
        u = u_ref[...]
        da_v = da_ref[...]
        sg = _sigmoid(g)
        o_ref[0] = (da_v * u * (sg + g * sg * (1.0 - sg))).astype(bf16)
        o_ref[1] = (da_v * g * sg).astype(bf16)

    return pl.pallas_call(
        body, name=name, grid=(4, T // TT),
        in_specs=[_bs((None, TT, GU), lambda j, i: (j, i, 0)), _bs((None, TT, GU), lambda j, i: (j + 4, i, 0)),
                  _bs((None, TT, GU), lambda j, i: (j, i, 0))],
        out_specs=_bs((2, None, TT, GU), lambda j, i: (0, j, i, 0)), out_shape=S((2, 4, T, GU), bf16),
        compiler_params=_cp("parallel", "parallel"))(gu, gu, da)


def _loss(name, y, tgt):
    T = y.shape[0]

    def body(y_ref, t_ref, l_ref, dy_ref):
        i = pl.program_id(0)
        e = y_ref[...] - t_ref[...]
        dy_ref[...] = e * (1.0 / D)
        s = 0.5 * jnp.sum(jnp.mean(e * e, axis=-1, keepdims=True))
        _acc_store(i, l_ref, jnp.broadcast_to(s, (1, 128)))

    return pl.pallas_call(body, name=name, grid=(T // TT,), in_specs=[_row(), _row()],
                          out_specs=[_vec(1, 128), _row()], out_shape=[S((1, 128), f32), S((T, D), f32)],
                          compiler_params=_cp("arbitrary"))(y, tgt)


def _prev8(T, cb):
    return _bs((8, D), lambda i: (jnp.maximum(i * (TT // 8) - 1, 0), cb))


def _next8(T, cb):
    return _bs((8, D), lambda i: (jnp.minimum((i + 1) * (TT // 8), T // 8 - 1), cb))


def _conv_taps(i, ac_ref, ax_ref, pc_ref, px_ref):
    ca = ac_ref[...] * ax_ref[...]
    keep = (i > 0).astype(f32)
    p1 = pc_ref[7:8, :] * px_ref[7:8, :] * keep
    p2 = pc_ref[6:7, :] * px_ref[6:7, :] * keep
    row = lax.broadcasted_iota(jnp.int32, ca.shape, 0)
    s1 = jnp.where(row == 0, p1, pltpu.roll(ca, 1, 0))
    s2 = jnp.where(row == 0, p2, jnp.where(row == 1, p1, pltpu.roll(ca, 2, 0)))
    return ca, s1, s2


def _conv_fwd(name, z, cw):
    T = z.shape[0]

    def body(ab_ref, ac_ref, ax_ref, pc_ref, px_ref, w_ref, o_ref):
        i = pl.program_id(0)
        ca, s1, s2 = _conv_taps(i, ac_ref, ax_ref, pc_ref, px_ref)
        cv = w_ref[0:1, :] * s2 + w_ref[1:2, :] * s1 + w_ref[2:3, :] * ca
        o_ref[...] = (ab_ref[...] * cv).astype(bf16)

    return pl.pallas_call(
        body, name=name, grid=(T // TT,),
        in_specs=[_row(0), _row(1), _row(2), _prev8(T, 1), _prev8(T, 2), _vec(3)],
        out_specs=_row(), out_shape=S((T, D), bf16), compiler_params=_cp("parallel"))(z, z, z, z, z, cw)


def _conv_bwd(name, dya, z, cw):
    T = z.shape[0]
    n = T // TT

    def body(dya_ref, ab_ref, ac_ref, ax_ref, pc_ref, px_ref, ndya_ref, nab_ref, w_ref, dz_ref, dw_ref):
        i = pl.program_id(0)
        ca, s1, s2 = _conv_taps(i, ac_ref, ax_ref, pc_ref, px_ref)
        w0, w1, w2 = w_ref[0:1, :], w_ref[1:2, :], w_ref[2:3, :]
        cv = w0 * s2 + w1 * s1 + w2 * ca
        dya_v = dya_ref[...]
        ab = ab_ref[...]
        dcv = dya_v * ab
        keep = (i < n - 1).astype(f32)
        n1 = ndya_ref[0:1, :] * nab_ref[0:1, :] * keep
        n2 = ndya_ref[1:2, :] * nab_ref[1:2, :] * keep
        row = lax.broadcasted_iota(jnp.int32, dcv.shape, 0)
        f1 = jnp.where(row == TT - 1, n1, pltpu.roll(dcv, TT - 1, 0))
        f2 = jnp.where(row == TT - 1, n2, jnp.where(row == TT - 2, n1, pltpu.roll(dcv, TT - 2, 0)))
        dca = w2 * dcv + w1 * f1 + w0 * f2
        dz_ref[:, 0:D] = (dya_v * cv).astype(bf16)
        dz_ref[:, D:2 * D] = (dca * ax_ref[...]).astype(bf16)
        dz_ref[:, 2 * D:3 * D] = (dca * ac_ref[...]).astype(bf16)
        dw = jnp.concatenate([jnp.sum(dcv * s2, axis=0, keepdims=True), jnp.sum(dcv * s1, axis=0, keepdims=True),
                              jnp.sum(dcv * ca, axis=0, keepdims=True)], axis=0)
        _acc_store(i, dw_ref, dw)

    return pl.pallas_call(
        body, name=name, grid=(n,),
        in_specs=[_row(), _row(0), _row(1), _row(2), _prev8(T, 1), _prev8(T, 2), _next8(T, 0), _next8(T, 0), _vec(3)],
        out_specs=[_row(0, 3 * D), _vec(3)], out_shape=[S((T, 3 * D), bf16), S((3, D), f32)],
        compiler_params=_cp("arbitrary"))(dya, z, z, z, z, z, dya, z, cw)


def _gelu(x):
    return 0.5 * x * (1.0 + lax.erf(x * INV_SQRT2))


def _gelu_grad(x):
    return 0.5 * (1.0 + lax.erf(x * INV_SQRT2)) + x * jnp.exp(-0.5 * x * x) * INV_SQRT2PI


def _ln_stats(vv):
    mu = jnp.mean(vv, axis=-1, keepdims=True)
    xc = vv - mu
    rstd = lax.rsqrt(jnp.mean(xc * xc, axis=-1, keepdims=True) + LN_EPS)
    return xc * rstd, rstd


def _tril_w(w_ref, g):
    r = lax.broadcasted_iota(jnp.int32, (HD, HD), 0)
    c = lax.broadcasted_iota(jnp.int32, (HD, HD), 1)
    return jnp.where(c <= r, w_ref[g], 0.0).astype(bf16)


def _sgu_fwd(name, z, ln_g, ln_b, w_s, bmap):
    T = z.shape[0]

    def body(su_ref, sv_ref, lg_ref, lb_ref, w_ref, bm_ref, o_ref, vn_ref):
        xhat, _ = _ln_stats(_gelu(sv_ref[...]))
        vn_ref[...] = (xhat * lg_ref[...] + lb_ref[...]).astype(bf16)
        for g in range(NH):
            w = _tril_w(w_ref, g)
            cs = slice(g * HD, (g + 1) * HD)
            for c in range(TT // HD):
                rs = slice(c * HD, (c + 1) * HD)
                s = jnp.dot(w, vn_ref[rs, cs], preferred_element_type=f32) + bm_ref[:, cs]
                o_ref[rs, cs] = (_gelu(su_ref[rs, cs]) * s).astype(bf16)

    return pl.pallas_call(
        body, name=name, grid=(T // TT,),
        in_specs=[_row(3), _row(4), _vec(), _vec(), _bs((NH, HD, HD), lambda i: (0, 0, 0)), _vec(HD)],
        out_specs=_row(), out_shape=S((T, D), bf16), scratch_shapes=[pltpu.VMEM((TT, D), bf16)],
        compiler_params=_cp("parallel"))(z, z, ln_g, ln_b, w_s, bmap)


def _sgu_bwd(name, dyb, z, ln_g, ln_b, w_s, bmap):
    T = z.shape[0]

    def body(dyb_ref, su_ref, sv_ref, lg_ref, lb_ref, w_ref, bm_ref, dz_ref, dlg_ref, dlb_ref, dw_ref, db_ref,
             vn_ref, du_ref, dvn_ref):
        i = pl.program_id(0)
        sv = sv_ref[...]
        xhat, rstd = _ln_stats(_gelu(sv))
        vn_ref[...] = (xhat * lg_ref[...] + lb_ref[...]).astype(bf16)
        r = lax.broadcasted_iota(jnp.int32, (HD, HD), 0)
        cc = lax.broadcasted_iota(jnp.int32, (HD, HD), 1)
        for g in range(NH):
            w = _tril_w(w_ref, g)
            cs = slice(g * HD, (g + 1) * HD)
            dw = jnp.zeros((HD, HD), f32)
            db = jnp.zeros((HD, 1), f32)
            for c in range(TT // HD):
                rs = slice(c * HD, (c + 1) * HD)
                vnb = vn_ref[rs, cs]
                s = jnp.dot(w, vnb, preferred_element_type=f32) + bm_ref[:, cs]
                dy = dyb_ref[rs, cs]
                du_ref[rs, cs] = dy * s
                ds = dy * _gelu(su_ref[rs, cs])
                ds16 = ds.astype(bf16)
                dvn_ref[rs, cs] = lax.dot_general(w, ds16, TN, preferred_element_type=f32)
                dw = dw + lax.dot_general(ds16, vnb, NT, preferred_element_type=f32)
                db = db + jnp.sum(ds, axis=1, keepdims=True)
            dw = jnp.where(cc <= r, dw, 0.0)

            @pl.when(i == 0)
            def _():
                dw_ref[g] = dw
                db_ref[:, g:g + 1] = db

            @pl.when(i > 0)
            def _():
                dw_ref[g] += dw
                db_ref[:, g:g + 1] += db

        dvn = dvn_ref[...]
        dxh = dvn * lg_ref[...]
        m1 = jnp.mean(dxh, axis=-1, keepdims=True)
        m2 = jnp.mean(dxh * xhat, axis=-1, keepdims=True)
        dvv = rstd * (dxh - m1 - xhat * m2)
        dz_ref[:, 0:D] = (du_ref[...] * _gelu_grad(su_ref[...])).astype(bf16)
        dz_ref[:, D:2 * D] = (dvv * _gelu_grad(sv)).astype(bf16)
        _acc_store(i, dlg_ref, jnp.sum(dvn * xhat, axis=0, keepdims=True))
        _acc_store(i, dlb_ref, jnp.sum(dvn, axis=0, keepdims=True))

    return pl.pallas_call(
        body, name=name, grid=(T // TT,),
        in_specs=[_row(), _row(3), _row(4), _vec(), _vec(), _bs((NH, HD, HD), lambda i: (0, 0, 0)), _vec(HD)],
        out_specs=[_row(0, 2 * D), _vec(), _vec(), _bs((NH, HD, HD), lambda i: (0, 0, 0)), _bs((HD, NH), lambda i: (0, 0))],
        out_shape=[S((T, 2 * D), bf16), S((1, D), f32), S((1, D), f32), S((NH, HD, HD), f32), S((HD, NH), f32)],
        scratch_shapes=[pltpu.VMEM((TT, D), bf16), pltpu.VMEM((TT, D), f32), pltpu.VMEM((TT, D), f32)],
        compiler_params=_cp("arbitrary"))(dyb, z, z, ln_g, ln_b, w_s, bmap)


def _qk_fwd(name, z, qg, kg, bf):
    T = z.shape[0]

    def body(q_ref, k_ref, v_ref, zf_ref, qg_ref, kg_ref, bf_ref, qn_ref, kn_ref, vb_ref, lf_ref):
        for h in range(NH):
            cs = slice(h * HD, (h + 1) * HD)
            for src, gain, dst in ((q_ref, qg_ref, qn_ref), (k_ref, kg_ref, kn_ref)):
                xv = src[:, cs]
                r = lax.rsqrt(jnp.mean(xv * xv, axis=-1, keepdims=True) + RMS_EPS)
                dst[:, cs] = (xv * r * gain[:, cs]).astype(bf16)
        vb_ref[...] = v_ref[...].astype(bf16)
        xf = zf_ref[...] + bf_ref[...]
        lf_ref[...] = jnp.minimum(xf, 0.0) - jnp.log1p(jnp.exp(-jnp.abs(xf)))

    return pl.pallas_call(
        body, name=name, grid=(T // TT,),
        in_specs=[_row(5), _row(6), _row(7), _bs((TT, 128), lambda i: (i, F_OFF // 128)), _vec(), _vec(), _vec(1, 128)],
        out_specs=[_row(), _row(), _row(), _bs((TT, 128), lambda i: (i, 0))],
        out_shape=[S((T, D), bf16), S((T, D), bf16), S((T, D), bf16), S((T, 128), f32)],
        compiler_params=_cp("parallel"))(z, z, z, z, qg, kg, bf)


def _cum_fwd(name, logf):
    T = logf.shape[0]

    def body(lf_ref, ccol_ref, crow_ref, c_ref):
        c = lf_ref[...]
        row = lax.broadcasted_iota(jnp.int32, c.shape, 0)
        s = 1
        while s < T:
            c = c + jnp.where(row >= s, pltpu.roll(c, s, 0), 0.0)
            s *= 2
        c_ref[...] = c
        crow_ref[...] = c.T[0:NH, :]
        for h in range(NH):
            ccol_ref[h] = jnp.broadcast_to(c_ref[:, h:h + 1], (T, 128))

    return pl.pallas_call(body, name=name, out_shape=[S((NH, T, 128), f32), S((NH, T), f32)],
                          scratch_shapes=[pltpu.VMEM((T, 128), f32)],
                          compiler_params=pltpu.CompilerParams(vmem_limit_bytes=VMEM_LIMIT))(logf)


ATT_TILE = 512


def _fold(x, op=jnp.add):
    acc = x[:, 0:128]
    for t in range(1, x.shape[1] // 128):
        acc = op(acc, x[:, t * 128:(t + 1) * 128])
    return acc


def _to_row(col):
    return jnp.broadcast_to(col, (col.shape[0], 128)).T[0:1, :]


def _causal(t, keys_down=False):
    r = lax.broadcasted_iota(jnp.int32, (t, t), 0)
    c = lax.broadcasted_iota(jnp.int32, (t, t), 1)
    return r <= c if keys_down else c <= r


def _attn_fwd(name, qn, kn, vb, ccol, crow3):
    T = qn.shape[0]
    tq = _tile(T, ATT_TILE)
    nq = T // tq

    def body(q_ref, k_ref, v_ref, cc_ref, cr_ref, o_ref, lse_ref, lser_ref, s_ref):
        qi = pl.program_id(1)
        q = q_ref[...]
        cq = cc_ref[:, 0:1]

        def logits(off):
            s = lax.dot_general(q, k_ref[pl.ds(off, tq), :], NT, preferred_element_type=f32) * ATT_SCALE
            return s + cq - cr_ref[:, pl.ds(off, tq)]

        def below(j, mvec):
            off = pl.multiple_of(j * tq, tq)
            s = logits(off)
            s_ref[:, pl.ds(off, tq)] = s
            return jnp.maximum(mvec, _fold(s, jnp.maximum))

        mvec = lax.fori_loop(0, qi, below, jnp.full((tq, 128), NEG, f32))
        off = pl.multiple_of(qi * tq, tq)
        s = jnp.where(_causal(tq), logits(off), NEG)
        s_ref[:, pl.ds(off, tq)] = s
        m = jnp.max(jnp.maximum(mvec, _fold(s, jnp.maximum)), axis=1, keepdims=True)

        def weigh(j, carry):
            lvec, acc = carry
            off = pl.multiple_of(j * tq, tq)
            p = jnp.exp(s_ref[:, pl.ds(off, tq)] - m)
            acc = acc + jnp.dot(p.astype(bf16), v_ref[pl.ds(off, tq), :], preferred_element_type=f32)
            return lvec + _fold(p), acc

        lvec, acc = lax.fori_loop(0, qi + 1, weigh, (jnp.zeros((tq, 128), f32), jnp.zeros((tq, HD), f32)))
        l = jnp.sum(lvec, axis=1, keepdims=True)
        o_ref[...] = acc / l
        lse = m + jnp.log(l)
        lse_ref[...] = jnp.broadcast_to(lse, (tq, 128))
        lser_ref[...] = _to_row(lse)

    return pl.pallas_call(
        body, name=name, grid=(NH, nq),
        in_specs=[_bs((tq, HD), lambda h, i: (i, h)), _bs((T, HD), lambda h, i: (0, h)), _bs((T, HD), lambda h, i: (0, h)),
                  _bs((None, tq, 128), lambda h, i: (h, i, 0)), _bs((None, 1, T), lambda h, i: (h, 0, 0))],
        out_specs=[_bs((tq, HD), lambda h, i: (i, h)), _bs((None, tq, 128), lambda h, i: (h, i, 0)),
                   _bs((None, 1, tq), lambda h, i: (h, 0, i))],
        out_shape=[S((T, D), f32), S((NH, T, 128), f32), S((NH, 1, T), f32)],
        scratch_shapes=[pltpu.VMEM((tq, T), f32)],
        compiler_params=_cp("parallel", "parallel"))(qn, kn, vb, ccol, crow3)


def _attn_dq(name, qn, kn, vb, do, lse, ccol, crow3):
    T = qn.shape[0]
    tq = _tile(T, ATT_TILE)
    nq = T // tq

    def body(q_ref, k_ref, v_ref, do_ref, lse_ref, cc_ref, cr_ref, dq_ref, dlr_ref, p_ref, dp_ref):
        qi = pl.program_id(1)
        q = q_ref[...]
        do16 = do_ref[...].astype(bf16)
        base = cc_ref[:, 0:1] - lse_ref[:, 0:1]

        def probs(off):
            s = lax.dot_general(q, k_ref[pl.ds(off, tq), :], NT, preferred_element_type=f32) * ATT_SCALE
            return jnp.exp(s + base - cr_ref[:, pl.ds(off, tq)])

        def keep(off, p, dvec):
            dp = lax.dot_general(do16, v_ref[pl.ds(off, tq), :], NT, preferred_element_type=f32)
            p_ref[:, pl.ds(off, tq)] = p
            dp_ref[:, pl.ds(off, tq)] = dp
            return dvec + _fold(p * dp)

        def below(j, dvec):
            off = pl.multiple_of(j * tq, tq)
            return keep(off, probs(off), dvec)

        dvec = lax.fori_loop(0, qi, below, jnp.zeros((tq, 128), f32))
        off = pl.multiple_of(qi * tq, tq)
        dvec = keep(off, jnp.where(_causal(tq), probs(off), 0.0), dvec)
        delta = jnp.sum(dvec, axis=1, keepdims=True)

        def grad(j, acc):
            off = pl.multiple_of(j * tq, tq)
            ds = p_ref[:, pl.ds(off, tq)] * (dp_ref[:, pl.ds(off, tq)] - delta)
            return acc + jnp.dot(ds.astype(bf16), k_ref[pl.ds(off, tq), :], preferred_element_type=f32)

        dq_ref[...] = lax.fori_loop(0, qi + 1, grad, jnp.zeros((tq, HD), f32)) * ATT_SCALE
        dlr_ref[...] = _to_row(delta)

    qb = lambda h, i: (i, h)
    full = lambda h, i: (0, h)
    col = lambda h, i: (h, i, 0)
    return pl.pallas_call(
        body, name=name, grid=(NH, nq),
        in_specs=[_bs((tq, HD), qb), _bs((T, HD), full), _bs((T, HD), full), _bs((tq, HD), qb),
                  _bs((None, tq, 128), col), _bs((None, tq, 128), col), _bs((None, 1, T), lambda h, i: (h, 0, 0))],
        out_specs=[_bs((tq, HD), qb), _bs((None, 1, tq), lambda h, i: (h, 0, i))],
        out_shape=[S((T, D), f32), S((NH, 1, T), f32)],
        scratch_shapes=[pltpu.VMEM((tq, T), f32), pltpu.VMEM((tq, T), f32)],
        compiler_params=_cp("parallel", "parallel"))(qn, kn, vb, do, lse, ccol, crow3)


def _attn_dkv(name, qn, kn, vb, do, lser3, dlr3, ccol, crow3):
    T = qn.shape[0]
    tk = _tile(T, ATT_TILE)
    nk = T // tk

    def body(q_ref, k_ref, v_ref, do_ref, lser_ref, dlr_ref, cc_ref, cr_ref, dk_ref, dv_ref, cs_ref):
        h = pl.program_id(0)
        kj = pl.program_id(1)

        @pl.when((h == 0) & (kj == 0))
        def _():
            cs_ref[...] = jnp.zeros_like(cs_ref)

        kb = k_ref[...]
        vv = v_ref[...]
        ckey = cc_ref[:, 0:1]

        def block(off, diagonal):
            rows = pl.ds(off, tk)
            qb = q_ref[rows, :]
            do16 = do_ref[rows, :].astype(bf16)
            st = lax.dot_general(kb, qb, NT, preferred_element_type=f32) * ATT_SCALE
            pt = jnp.exp(st + (cr_ref[:, rows] - lser_ref[:, rows]) - ckey)
            if diagonal:
                pt = jnp.where(_causal(tk, keys_down=True), pt, 0.0)
            dpt = lax.dot_general(vv, do16, NT, preferred_element_type=f32)
            dst = pt * (dpt - dlr_ref[:, rows])
            ddv = jnp.dot(pt.astype(bf16), do16, preferred_element_type=f32)
            ddk = jnp.dot(dst.astype(bf16), qb, preferred_element_type=f32)
            return ddk, ddv, _fold(dst)

        def above(i, carry):
            ddk, ddv, dcs = block(pl.multiple_of(i * tk, tk), False)
            return carry[0] + ddk, carry[1] + ddv, carry[2] + dcs

        off = pl.multiple_of(kj * tk, tk)
        dk, dv, cs = lax.fori_loop(kj + 1, nk, above, block(off, True))
        dk_ref[...] = dk * ATT_SCALE
        dv_ref[...] = dv
        lane = lax.broadcasted_iota(jnp.int32, (tk, 128), 1)
        cs_ref[pl.ds(off, tk), :] += jnp.where(lane == h, jnp.sum(cs, axis=1, keepdims=True), 0.0)

    full = lambda h, j: (0, h)
    blk = lambda h, j: (j, h)
    row = lambda h, j: (h, 0, 0)
    return pl.pallas_call(
        body, name=name, grid=(NH, nk),
        in_specs=[_bs((T, HD), full), _bs((tk, HD), blk), _bs((tk, HD), blk), _bs((T, HD), full), _bs((None, 1, T), row),
                  _bs((None, 1, T), row), _bs((None, tk, 128), lambda h, j: (h, j, 0)), _bs((None, 1, T), row)],
        out_specs=[_bs((tk, HD), blk), _bs((tk, HD), blk), _bs((T, 128), lambda h, j: (0, 0))],
        out_shape=[S((T, D), f32), S((T, D), f32), S((T, 128), f32)],
        compiler_params=_cp("arbitrary", "arbitrary"))(qn, kn, vb, do, lser3, dlr3, ccol, crow3)


def _forget_bwd(name, cs, z, bf):
    T = cs.shape[0]

    def body(cs_ref, zf_ref, bf_ref, dz_ref, db_ref):
        c = -cs_ref[...]
        row = lax.broadcasted_iota(jnp.int32, c.shape, 0)
        s = 1
        while s < T:
            c = c + jnp.where(row + s < T, pltpu.roll(c, T - s, 0), 0.0)
            s *= 2
        xf = zf_ref[...] + bf_ref[...]
        lane = lax.broadcasted_iota(jnp.int32, c.shape, 1)
        dxf = jnp.where(lane < NH, c / (1.0 + jnp.exp(xf)), 0.0)
        dz_ref[...] = jnp.zeros_like(dz_ref)
        dz_ref[:, 0:128] = dxf.astype(bf16)
        db_ref[...] = jnp.sum(dxf, axis=0, keepdims=True)

    return pl.pallas_call(
        body, name=name, grid=(1,),
        in_specs=[_bs((T, 128), lambda i: (0, 0)), _bs((T, 128), lambda i: (0, F_OFF // 128)), _vec(1, 128)],
        out_specs=[_bs((T, NZ - F_OFF), lambda i: (0, 0)), _vec(1, 128)],
        out_shape=[S((T, NZ - F_OFF), bf16), S((1, 128), f32)], compiler_params=_cp("arbitrary"))(cs, z, bf)


def _qk_bwd(name, dqn, dkn, dv, z, qg, kg):
    T = z.shape[0]

    def body(dq_ref, dk_ref, dv_ref, q_ref, k_ref, qg_ref, kg_ref, dz_ref, dqg_ref, dkg_ref, g_ref):
        i = pl.program_id(0)
        for n, (src, dsrc, gain, dgain) in enumerate(((q_ref, dq_ref, qg_ref, dqg_ref), (k_ref, dk_ref, kg_ref, dkg_ref))):
            for h in range(NH):
                cs = slice(h * HD, (h + 1) * HD)
                xv = src[:, cs]
                r = lax.rsqrt(jnp.mean(xv * xv, axis=-1, keepdims=True) + RMS_EPS)
                xhat = xv * r
                dy = dsrc[:, cs]
                dyg = dy * gain[:, cs]
                m = jnp.mean(dyg * xhat, axis=-1, keepdims=True)
                dz_ref[:, n * D + h * HD:n * D + (h + 1) * HD] = (r * (dyg - xhat * m)).astype(bf16)
                g_ref[:, cs] = jnp.sum(dy * xhat, axis=0, keepdims=True)
            _acc_store(i, dgain, g_ref[...])
        dz_ref[:, 2 * D:3 * D] = dv_ref[...].astype(bf16)

    return pl.pallas_call(
        body, name=name, grid=(T // TT,),
        in_specs=[_row(), _row(), _row(), _row(5), _row(6), _vec(), _vec()],
        out_specs=[_row(0, 3 * D), _vec(), _vec()], out_shape=[S((T, 3 * D), bf16), S((1, D), f32), S((1, D), f32)],
        scratch_shapes=[pltpu.VMEM((1, D), f32)], compiler_params=_cp("arbitrary"))(dqn, dkn, dv, z, z, qg, kg)


GB = GATE_OFF // D


def _merge_fwd(name, ya, yb, yc, z, bg):
    T = z.shape[0]

    def body(ya_ref, yb_ref, yc_ref, g0_ref, g1_ref, g2_ref, bg_ref, o_ref):
        acc = _sigmoid(g0_ref[...] + bg_ref[0:1, :]) * ya_ref[...]
        acc = acc + _sigmoid(g1_ref[...] + bg_ref[1:2, :]) * yb_ref[...]
        acc = acc + _sigmoid(g2_ref[...] + bg_ref[2:3, :]) * yc_ref[...]
        o_ref[...] = acc.astype(bf16)

    return pl.pallas_call(
        body, name=name, grid=(T // TT,),
        in_specs=[_row(), _row(), _row(), _row(GB), _row(GB + 1), _row(GB + 2), _vec(3)],
        out_specs=_row(), out_shape=S((T, D), bf16), compiler_params=_cp("parallel"))(ya, yb, yc, z, z, z, bg)


def _merge_bwd(name, dm, ya, yb, yc, z, bg):
    T = z.shape[0]

    def body(dm_ref, ya_ref, yb_ref, yc_ref, g0_ref, g1_ref, g2_ref, bg_ref, dya_ref, dyb_ref, dyc_ref, dz_ref, db_ref):
        i = pl.program_id(0)
        dm_v = dm_ref[...]
        dbs = []
        for n, (y_ref, g_ref, dy_ref) in enumerate(((ya_ref, g0_ref, dya_ref), (yb_ref, g1_ref, dyb_ref),
                                                    (yc_ref, g2_ref, dyc_ref))):
            gate = _sigmoid(g_ref[...] + bg_ref[n:n + 1, :])
            dy_ref[...] = (dm_v * gate).astype(bf16)
            dl = dm_v * y_ref[...] * gate * (1.0 - gate)
            dz_ref[:, n * D:(n + 1) * D] = dl.astype(bf16)
            dbs.append(jnp.sum(dl, axis=0, keepdims=True))
        _acc_store(i, db_ref, jnp.concatenate(dbs, axis=0))

    return pl.pallas_call(
        body, name=name, grid=(T // TT,),
        in_specs=[_row(), _row(), _row(), _row(), _row(GB), _row(GB + 1), _row(GB + 2), _vec(3)],
        out_specs=[_row(), _row(), _row(), _row(0, 3 * D), _vec(3)],
        out_shape=[S((T, D), bf16), S((T, D), bf16), S((T, D), bf16), S((T, 3 * D), bf16), S((3, D), f32)],
        compiler_params=_cp("arbitrary"))(dm, ya, yb, yc, z, z, z, bg)


SMALL_NAMES = ("ffn1_norm", "mix_norm", "b_forget", "b_gate", "conv_w", "sgu_ln_g", "sgu_ln_b", "sgu_w", "sgu_b",
               "q_norm_g", "k_norm_g", "ffn2_norm")


def _small_params(p):
    out = {n: p[n].reshape(1, D) for n in ("ffn1_norm", "mix_norm", "ffn2_norm", "sgu_ln_g", "sgu_ln_b", "q_norm_g", "k_norm_g")}
    out["b_forget"] = jnp.pad(p["b_forget"].reshape(1, NH), ((0, 0), (0, 128 - NH)))
    out["b_gate"] = p["b_gate"]
    out["conv_w"] = p["conv_w"]
    out["sgu_w"] = p["sgu_w"]
    out["bmap"] = jnp.repeat(p["sgu_b"].T, HD, axis=1)
    return out


def _small_grads_natural(sg):
    out = {n: sg[n].reshape(D) for n in ("ffn1_norm", "mix_norm", "ffn2_norm", "sgu_ln_g", "sgu_ln_b")}
    out["q_norm_g"] = sg["q_norm_g"].reshape(NH, HD)
    out["k_norm_g"] = sg["k_norm_g"].reshape(NH, HD)
    out["b_forget"] = sg["b_forget"][0, :NH]
    out["b_gate"] = sg["b_gate"]
    out["conv_w"] = sg["conv_w"]
    out["sgu_w"] = sg["sgu_w"]
    out["sgu_b"] = sg["sgu_b"]
    return out


def _sq_fwd(name, a, wsq, l, n, res=None):
    T = a.shape[0]
    tm = _tile(T, 512)
    return _mm(name, a, wsq, grid=(T // tm, 1, 1), a_spec=_bs((tm, D), lambda i, j, k: (i, 0)),
               b_spec=_bs((None, None, D, D), lambda i, j, k: (l, n, 0, 0)),
               out_shape=S((T, D), f32), out_spec=_bs((tm, D), lambda i, j, k: (i, 0)), dims=NN, acc_shape=None,
               res=res, res_spec=_bs((tm, D), lambda i, j, k: (i, 0)))


def _sq_bwd_in(name, dy, wsq, l, n):
    T = dy.shape[0]
    tm = _tile(T, 512)
    return _mm(name, dy, wsq, grid=(T // tm, 1, 1), a_spec=_bs((tm, D), lambda i, j, k: (i, 0)),
               b_spec=_bs((None, None, D, D), lambda i, j, k: (l, n, 0, 0)),
               out_shape=S((T, D), f32), out_spec=_bs((tm, D), lambda i, j, k: (i, 0)), dims=NT, acc_shape=None)


def _sq_bwd_w(name, a, dy, gbuf, l):
    T = a.shape[0]
    return _mm(name, a, dy, grid=(NDEV, 1, 1), a_spec=_bs((T, 128), lambda i, j, k: (0, i)),
               b_spec=_bs((T, D), lambda i, j, k: (0, 0)), out_shape=S(gbuf.shape, bf16),
               out_spec=_bs((None, None, None, 128, D), lambda i, j, k: (i % 2, i // 2, l, 0, 0)),
               dims=TN, acc_shape=None, alias=gbuf)


def _ffn_fwd(tag, x, g, wgu, wd, l):
    T = x.shape[0]
    tm = _tile(T, 1024)
    h = _rms_fwd(tag + "_rms", x, g)
    gu = _mm(tag + "_gu", h, wgu, grid=(T // tm, NDEV, 1), a_spec=_bs((tm, D), lambda i, j, k: (i, 0)),
             b_spec=_bs((None, None, D, GU), lambda i, j, k: (l, j, 0, 0)), out_shape=S((NDEV, T, GU), f32),
             out_spec=_bs((None, tm, GU), lambda i, j, k: (j, i, 0)), dims=NN, acc_shape=None)
    a = _swiglu_fwd(tag + "_act", gu)
    tm = _tile(T, 1024)
    xo = _mm(tag + "_down", a, wd, grid=(T // tm, 1, 4), a_spec=_bs((None, tm, GU), lambda i, j, k: (k, i, 0)),
             b_spec=_bs((None, None, GU, D), lambda i, j, k: (l, k, 0, 0)), out_shape=S((T, D), f32),
             out_spec=_bs((tm, D), lambda i, j, k: (i, 0)), dims=NN, acc_shape=(tm, D), res=x,
             res_spec=_bs((tm, D), lambda i, j, k: (i, 0)), alpha=0.5)
    return xo, (h, gu, a)


def _ffn_bwd(tag, dxo, x, g, wgu, wd, l, saved, g_gu, g_d):
    h, gu, a = saved
    T = x.shape[0]
    tm = _tile(T, 512)
    da = _mm(tag + "_dact", dxo, wd, grid=(T // tm, 4, 1), a_spec=_bs((tm, D), lambda i, j, k: (i, 0)),
             b_spec=_bs((None, None, GU, D), lambda i, j, k: (l, j, 0, 0)), out_shape=S((4, T, GU), f32),
             out_spec=_bs((None, tm, GU), lambda i, j, k: (j, i, 0)), dims=NT, acc_shape=None, alpha=0.5)
    g_d = _mm(tag + "_dwd", a, dxo, grid=(4, 1, 1), a_spec=_bs((None, T, GU), lambda i, j, k: (i, 0, 0)),
              b_spec=_bs((T, D), lambda i, j, k: (0, 0)), out_shape=S(g_d.shape, bf16),
              out_spec=_bs((2, None, None, GU // 2, D), lambda i, j, k: (0, i, l, 0, 0)), dims=TN, acc_shape=None,
              alpha=0.5, alias=g_d, split_rows=GU // 2)
    dgu = _swiglu_bwd(tag + "_dgu", gu, da).reshape(NDEV, T, GU)
    dh = _mm(tag + "_dh", dgu, wgu, grid=(1, 1, NDEV), a_spec=_bs((None, T, GU), lambda i, j, k: (k, 0, 0)),
             b_spec=_bs((None, None, D, GU), lambda i, j, k: (l, k, 0, 0)), out_shape=S((T, D), f32),
             out_spec=_bs((T, D), lambda i, j, k: (0, 0)), dims=NT, acc_shape=(T, D))
    g_gu = _mm(tag + "_dwgu", h, dgu, grid=(1, NDEV, 1), a_spec=_bs((T, D), lambda i, j, k: (0, 0)),
               b_spec=_bs((None, T, GU), lambda i, j, k: (j, 0, 0)), out_shape=S(g_gu.shape, bf16),
               out_spec=_bs((None, None, None, D, GU), lambda i, j, k: (j % 2, j // 2, l, 0, 0)), dims=TN,
               acc_shape=None, alias=g_gu)
    dx, dg = _rms_bwd(tag + "_drms", dh, x, g, dxo)
    return dx, dg, g_gu, g_d


def _mixer_fwd(tag, x, p, win, wsq, l):
    T = x.shape[0]
    h = _rms_fwd(tag + "_rms", x, p["mix_norm"])
    tn = 512
    z = _mm(tag + "_in", h, win, grid=(1, NZ // tn, 1), a_spec=_bs((T, D), lambda i, j, k: (0, 0)),
            b_spec=_bs((D, tn), lambda i, j, k: (0, j)), out_shape=S((T, NZ), f32),
            out_spec=_bs((T, tn), lambda i, j, k: (0, j)), dims=NN, acc_shape=None)
    ya_in = _conv_fwd(tag + "_conv", z, p["conv_w"])
    yb_in = _sgu_fwd(tag + "_sgu", z, p["sgu_ln_g"], p["sgu_ln_b"], p["sgu_w"], p["bmap"])
    qn, kn, vb, logf = _qk_fwd(tag + "_qk", z, p["q_norm_g"], p["k_norm_g"], p["b_forget"])
    ccol, crow = _cum_fwd(tag + "_cum", logf)
    crow3 = crow.reshape(NH, 1, T)
    o, lse, lser = _attn_fwd(tag + "_attn", qn, kn, vb, ccol, crow3)
    ya = _sq_fwd(tag + "_oconv", ya_in, wsq, l, 0)
    yb = _sq_fwd(tag + "_osgu", yb_in, wsq, l, 1)
    yc = _sq_fwd(tag + "_oattn", o, wsq, l, 2)
    merged = _merge_fwd(tag + "_merge", ya, yb, yc, z, p["b_gate"])
    xo = _sq_fwd(tag + "_o", merged, wsq, l, 3, res=x)
    return xo, (h, z, ya_in, yb_in, qn, kn, vb, ccol, crow3, o, lse, lser, ya, yb, yc, merged)


def _mixer_bwd(tag, dxo, x, p, win, wsq, l, saved, gsq):
    h, z, ya_in, yb_in, qn, kn, vb, ccol, crow3, o, lse, lser, ya, yb, yc, merged = saved
    T = x.shape[0]
    sg = {}
    dm = _sq_bwd_in(tag + "_dmerged", dxo, wsq, l, 3)
    gsq[3] = _sq_bwd_w(tag + "_dwo", merged, dxo, gsq[3], l)
    dya, dyb, dyc, dz_g, sg["b_gate"] = _merge_bwd(tag + "_dmerge", dm, ya, yb, yc, z, p["b_gate"])
    d_ya_in = _sq_bwd_in(tag + "_dconv_in", dya, wsq, l, 0)
    gsq[0] = _sq_bwd_w(tag + "_dwoc", ya_in, dya, gsq[0], l)
    d_yb_in = _sq_bwd_in(tag + "_dsgu_in", dyb, wsq, l, 1)
    gsq[1] = _sq_bwd_w(tag + "_dwos", yb_in, dyb, gsq[1], l)
    d_o = _sq_bwd_in(tag + "_dattn_in", dyc, wsq, l, 2)
    gsq[2] = _sq_bwd_w(tag + "_dwoa", o, dyc, gsq[2], l)
    dz_c, sg["conv_w"] = _conv_bwd(tag + "_dconv", d_ya_in, z, p["conv_w"])
    dz_s, sg["sgu_ln_g"], sg["sgu_ln_b"], sg["sgu_w"], db_t = _sgu_bwd(
        tag + "_dsgu", d_yb_in, z, p["sgu_ln_g"], p["sgu_ln_b"], p["sgu_w"], p["bmap"])
    sg["sgu_b"] = db_t.T
    dqn, dlr = _attn_dq(tag + "_dattn_q", qn, kn, vb, d_o, lse, ccol, crow3)
    dkn, dv, cs = _attn_dkv(tag + "_dattn_kv", qn, kn, vb, d_o, lser, dlr, ccol, crow3)
    dz_f, sg["b_forget"] = _forget_bwd(tag + "_dforget", cs, z, p["b_forget"])
    dz_q, sg["q_norm_g"], sg["k_norm_g"] = _qk_bwd(tag + "_dqk", dqn, dkn, dv, z, p["q_norm_g"], p["k_norm_g"])
    dz = jnp.concatenate([dz_c, dz_s, dz_q, dz_g, dz_f], axis=1)
    tk = 512
    dh = _mm(tag + "_dh", dz, win, grid=(1, 1, NZ // tk), a_spec=_bs((T, tk), lambda i, j, k: (0, k)),
             b_spec=_bs((D, tk), lambda i, j, k: (0, k)), out_shape=S((T, D), f32),
             out_spec=_bs((T, D), lambda i, j, k: (0, 0)), dims=NT, acc_shape=(T, D))
    tn = 512
    dwin = _mm(tag + "_dwin", h, dz, grid=(1, NZ // tn, 1), a_spec=_bs((T, D), lambda i, j, k: (0, 0)),
               b_spec=_bs((T, tn), lambda i, j, k: (0, j)), out_shape=S((D, NZ), bf16),
               out_spec=_bs((D, tn), lambda i, j, k: (0, j)), dims=TN, acc_shape=None)
    dx, sg["mix_norm"] = _rms_bwd(tag + "_drms", dh, x, p["mix_norm"], dxo)
    return dx, sg, dwin, gsq


def _local_step(x, tgt, small, wts, gbufs):
    saved = []
    for l in range(2):
        p = small[l]
        x1, s1 = _ffn_fwd("ffn1", x, p["ffn1_norm"], wts["gu1"], wts["d1"], l)
        x2, sm = _mixer_fwd("mix", x1, p, wts["win"][l], wts["sq"], l)
        x3, s2 = _ffn_fwd("ffn2", x2, p["ffn2_norm"], wts["gu2"], wts["d2"], l)
        saved.append((x, x1, x2, s1, sm, s2))
        x = x3
    loss_row, dx = _loss("loss", x, tgt)
    sgrads = [None, None]
    dwin = [None, None]
    gsq = [gbufs["oc"], gbufs["os"], gbufs["oa"], gbufs["o"]]
    g_gu1, g_d1, g_gu2, g_d2 = gbufs["gu1"], gbufs["d1"], gbufs["gu2"], gbufs["d2"]
    for l in (1, 0):
        p = small[l]
        x0, x1, x2, s1, sm, s2 = saved[l]
        dx, dn2, g_gu2, g_d2 = _ffn_bwd("ffn2", dx, x2, p["ffn2_norm"], wts["gu2"], wts["d2"], l, s2, g_gu2, g_d2)
        dx, sg, dwin[l], gsq = _mixer_bwd("mix", dx, x1, p, wts["win"][l], wts["sq"], l, sm, gsq)
        dx, dn1, g_gu1, g_d1 = _ffn_bwd("ffn1", dx, x0, p["ffn1_norm"], wts["gu1"], wts["d1"], l, s1, g_gu1, g_d1)
        sg["ffn1_norm"] = dn1
        sg["ffn2_norm"] = dn2
        sgrads[l] = sg
    gout = {"gu1": g_gu1, "d1": g_d1, "gu2": g_gu2, "d2": g_d2, "oc": gsq[0], "os": gsq[1], "oa": gsq[2], "o": gsq[3]}
    return loss_row, dx, sgrads, dwin, gout


ANY = pl.BlockSpec(memory_space=pl.ANY)


def _place():
    return lax.axis_index("x"), lax.axis_index("y"), lax.axis_index("c")


def _all_gather(name, arrs):
    n = len(arrs)
    units = [(a, l) for a in range(n) for l in range(2)]
    nu = len(units)

    def body(*refs):
        ins, outs = refs[:n], refs[n:2 * n]
        send, recv, lsem = refs[2 * n:]
        x, y, c = _place()
        me, sib = (x, y, c), (x, y, 1 - c)
        chips = [(1 - x, y), (x, 1 - y), (1 - x, 1 - y)]

        def blk(u, px, py, pc):
            a, l = units[u]
            return outs[a].at[l, 4 * px + 2 * py + pc]

        def copy(u, k, block, to, src=None):
            dst = blk(u, *block)
            return pltpu.make_async_remote_copy(src_ref=dst if src is None else src, dst_ref=dst,
                                                send_sem=send.at[u * 7 + k], recv_sem=recv.at[u * 7 + k],
                                                device_id=to, device_id_type=MESH)

        mine = [pltpu.make_async_copy(ins[a].at[l], blk(u, *me), lsem.at[u]) for u, (a, l) in enumerate(units)]
        for cp in mine:
            cp.start()
        first = []
        for u, (a, l) in enumerate(units):
            first.append(copy(u, 0, me, sib, ins[a].at[l]))
            for j, chip in enumerate(chips):
                first.append(copy(u, 1 + j, me, (*chip, c), ins[a].at[l]))
        for cp in first:
            cp.start()
        passed = []
        for u in range(nu):
            for j, chip in enumerate(chips):
                copy(u, 1 + j, (*chip, c), me).wait_recv()
                fwd = copy(u, 4 + j, (*chip, c), sib)
                fwd.start()
                passed.append(fwd)
        for u in range(nu):
            copy(u, 0, sib, me).wait_recv()
            for j, chip in enumerate(chips):
                copy(u, 4 + j, (*chip, 1 - c), me).wait_recv()
        for cp in first + passed:
            cp.wait_send()
        for cp in mine:
            cp.wait()

    return pl.pallas_call(
        body, name=name, in_specs=[ANY] * n, out_specs=[ANY] * n,
        out_shape=[S((2, NDEV) + a.shape[1:], a.dtype) for a in arrs],
        scratch_shapes=[pltpu.SemaphoreType.DMA((nu * 7,)), pltpu.SemaphoreType.DMA((nu * 7,)),
                        pltpu.SemaphoreType.DMA((nu,))],
    )(*arrs)


def _rs_pair(name, gs):
    n = len(gs)

    def body(*refs):
        ins, outs = refs[:n], refs[n:2 * n]
        send, recv = refs[2 * n:]
        x, y, c = _place()
        cps = [pltpu.make_async_remote_copy(src_ref=ins[u].at[1 - c], dst_ref=outs[u], send_sem=send.at[u],
                                            recv_sem=recv.at[u], device_id=(x, y, 1 - c), device_id_type=MESH)
               for u in range(n)]
        for cp in cps:
            cp.start()
        for cp in cps:
            cp.wait()

    return pl.pallas_call(
        body, name=name, in_specs=[ANY] * n, out_specs=[ANY] * n, out_shape=[S(g.shape[1:], g.dtype) for g in gs],
        scratch_shapes=[pltpu.SemaphoreType.DMA((n,)), pltpu.SemaphoreType.DMA((n,))],
    )(*gs)


def _row_tile(r, c):
    return 128 if (r % 128 == 0 and c > D) else (256 if r % 256 == 0 else r)


def _pair_sum(name, core, g, r1):
    _, nq, nl, r, c = g.shape
    tr = _row_tile(r, c)
    g4 = g.reshape(2, nq * nl, r, c)
    r3 = r1.reshape(nq * nl, r, c)

    def body(core_ref, g_ref, r_ref, o_ref):
        o_ref[...] = (g_ref[...].astype(f32) + r_ref[...].astype(f32)).astype(bf16)

    out = pl.pallas_call(
        body, name=name,
        grid_spec=pltpu.PrefetchScalarGridSpec(
            num_scalar_prefetch=1, grid=(nq * nl, r // tr),
            in_specs=[_bs((None, None, tr, c), lambda b, i, cr: (cr[0], b, i, 0)), _bs((None, tr, c), lambda b, i, cr: (b, i, 0))],
            out_specs=_bs((None, tr, c), lambda b, i, cr: (b, i, 0))),
        out_shape=S((nq * nl, r, c), bf16), compiler_params=_cp("parallel", "parallel"))(core, g4, r3)
    return out.reshape(nq, nl, r, c)


def _rs_chips(name, ss):
    n = len(ss)

    def body(*refs):
        ins, outs = refs[:n], refs[n:2 * n]
        send, recv = refs[2 * n:]
        x, y, c = _place()
        chips = [(1 - x, y), (x, 1 - y), (1 - x, 1 - y)]
        cps = []
        for u in range(n):
            for k, chip in enumerate(chips):
                cps.append(pltpu.make_async_remote_copy(
                    src_ref=ins[u].at[2 * chip[0] + chip[1]], dst_ref=outs[u].at[k], send_sem=send.at[u * 3 + k],
                    recv_sem=recv.at[u * 3 + k], device_id=(*chip, c), device_id_type=MESH))
        for cp in cps:
            cp.start()
        for cp in cps:
            cp.wait()

    return pl.pallas_call(
        body, name=name, in_specs=[ANY] * n, out_specs=[ANY] * n, out_shape=[S((3,) + s.shape[1:], s.dtype) for s in ss],
        scratch_shapes=[pltpu.SemaphoreType.DMA((n * 3,)), pltpu.SemaphoreType.DMA((n * 3,))],
    )(*ss)


def _all_reduce_small(name, v):
    r = v.shape[0]

    def body(v_ref, o_ref, buf, send, recv, lsem):
        x, y, c = _place()
        me, sib = (x, y, c), (x, y, 1 - c)
        chips = [(1 - x, y), (x, 1 - y), (1 - x, 1 - y)]

        def slot(px, py, pc):
            return buf.at[4 * px + 2 * py + pc]

        def copy(k, block, to, src=None):
            return pltpu.make_async_remote_copy(src_ref=slot(*block) if src is None else src, dst_ref=slot(*block),
                                                send_sem=send.at[k], recv_sem=recv.at[k], device_id=to,
                                                device_id_type=MESH)

        mine = pltpu.make_async_copy(v_ref, slot(*me), lsem)
        mine.start()
        first = [copy(0, me, sib, v_ref)] + [copy(1 + j, me, (*chip, c), v_ref) for j, chip in enumerate(chips)]
        for cp in first:
            cp.start()
        passed = [copy(4 + j, (*chip, c), sib) for j, chip in enumerate(chips)]
        for j, chip in enumerate(chips):
            copy(1 + j, (*chip, c), me).wait_recv()
            passed[j].start()
        copy(0, sib, me).wait_recv()
        for j, chip in enumerate(chips):
            copy(4 + j, (*chip, 1 - c), me).wait_recv()
        for cp in first + passed:
            cp.wait_send()
        mine.wait()
        acc = buf[0]
        for d in range(1, NDEV):
            acc = acc + buf[d]
        o_ref[...] = acc

    vm = pl.BlockSpec(memory_space=pltpu.VMEM)
    return pl.pallas_call(
        body, name=name, in_specs=[vm], out_specs=vm, out_shape=S((r, 128), f32),
        scratch_shapes=[pltpu.VMEM((NDEV, r, 128), f32), pltpu.SemaphoreType.DMA((7,)), pltpu.SemaphoreType.DMA((7,)),
                        pltpu.SemaphoreType.DMA],
        compiler_params=pltpu.CompilerParams(vmem_limit_bytes=VMEM_LIMIT),
    )(v)


def _adam_math(w, g, m, v):
    m = ADAM_B1 * m + (1.0 - ADAM_B1) * g
    v = ADAM_B2 * v + (1.0 - ADAM_B2) * (g * g)
    m_hat = m / (1.0 - ADAM_B1 ** ADAM_STEP)
    v_hat = v / (1.0 - ADAM_B2 ** ADAM_STEP)
    delta = -ADAM_LR * (m_hat / (jnp.sqrt(v_hat) + ADAM_EPS) + ADAM_WD * w)
    return delta, m, v


def _adamw(name, w, m, v, s_mine, r2):
    _, r, c = w.shape
    tr = _row_tile(r, c)

    def body(w_ref, m_ref, v_ref, s_ref, r0_ref, r1_ref, r2_ref, g_ref, d_ref, mo_ref, vo_ref):
        g = ((s_ref[...].astype(f32) + r0_ref[...].astype(f32)) + r1_ref[...].astype(f32)) + r2_ref[...].astype(f32)
        g_ref[...] = g
        d_ref[...], mo_ref[...], vo_ref[...] = _adam_math(w_ref[...], g, m_ref[...], v_ref[...])

    blk = _bs((None, tr, c), lambda l, i: (l, i, 0))
    part = lambda k: _bs((None, None, tr, c), lambda l, i: (k, l, i, 0))
    return pl.pallas_call(
        body, name=name, grid=(2, r // tr), in_specs=[blk, blk, blk, blk, part(0), part(1), part(2)],
        out_specs=[blk] * 4, out_shape=[S(w.shape, f32)] * 4, compiler_params=_cp("parallel", "parallel"),
    )(w, m, v, s_mine, r2, r2, r2)


def _adamw_small(name, w, g, m, v):
    def body(w_ref, g_ref, m_ref, v_ref, d_ref, mo_ref, vo_ref):
        d_ref[...], mo_ref[...], vo_ref[...] = _adam_math(w_ref[...], g_ref[...], m_ref[...], v_ref[...])

    return pl.pallas_call(body, name=name, out_shape=[S(w.shape, f32)] * 3,
                          compiler_params=pltpu.CompilerParams(vmem_limit_bytes=VMEM_LIMIT))(w, g, m, v)


WEIGHT_NAMES = ("ffn1_norm", "ffn1_w_gu", "ffn1_w_down", "mix_norm", "w_in", "b_forget", "b_gate", "conv_w", "sgu_ln_g",
                "sgu_ln_b", "sgu_w", "sgu_b", "q_norm_g", "k_norm_g", "w_out_conv", "w_out_sgu", "w_out_attn", "w_o",
                "ffn2_norm", "ffn2_w_gu", "ffn2_w_down")
BIG = {"ffn1_w_gu": "gu1", "ffn2_w_gu": "gu2", "ffn1_w_down": "d1", "ffn2_w_down": "d2", "w_in": "in",
       "w_out_conv": "oc", "w_out_sgu": "os", "w_out_attn": "oa", "w_o": "o"}
BIG_KEYS = ("gu1", "gu2", "d1", "d2", "in", "oc", "os", "oa", "o")
REPLICATED_SMALL = ("ffn1_norm", "mix_norm", "b_forget", "sgu_ln_g", "sgu_ln_b", "sgu_w", "sgu_b", "q_norm_g",
                    "k_norm_g", "ffn2_norm")
SHARDED_SMALL = ("b_gate", "conv_w")


def _pack(arrays):
    flat = jnp.concatenate([a.reshape(-1).astype(f32) for a in arrays])
    rows = -(-flat.shape[0] // 1024) * 8
    return jnp.pad(flat, (0, rows * 128 - flat.shape[0])).reshape(rows, 128)


def _unpack(packed, shapes):
    flat = packed.reshape(-1)
    out, pos = [], 0
    for shp in shapes:
        size = 1
        for s_ in shp:
            size *= s_
        out.append(flat[pos:pos + size].reshape(shp))
        pos += size
    return out


def _natural_runs(a, b):
    runs = []
    while a < b:
        d = a // INB
        e = min(b, (d + 1) * INB)
        runs.append((d, a - d * INB, e - d * INB))
        a = e
    return runs


def _win_kernel_layout(wg):
    runs = _natural_runs(0, GATE_OFF) + _natural_runs(GATE_OFF + NH, NIN) + _natural_runs(GATE_OFF, GATE_OFF + NH)
    return jnp.concatenate([wg[d, :, a:b] for d, a, b in runs] + [jnp.zeros((D, NZ - NIN), wg.dtype)], axis=1)


def _kernel_column(n):
    return n if n < GATE_OFF else (F_OFF + n - GATE_OFF if n < GATE_OFF + NH else n - NH)


def _win_device_block(dw, d):
    cuts = sorted({d * INB, (d + 1) * INB} | {c for c in (GATE_OFF, GATE_OFF + NH) if d * INB < c < (d + 1) * INB})
    parts = [dw[:, _kernel_column(a):_kernel_column(a) + (b - a)] for a, b in zip(cuts[:-1], cuts[1:])]
    return parts[0] if len(parts) == 1 else jnp.concatenate(parts, axis=1)


def kernel(x, ffn1_norm, ffn1_w_gu, ffn1_w_down, mix_norm, w_in, b_forget, b_gate, conv_w, sgu_ln_g, sgu_ln_b, sgu_w, sgu_b, q_norm_g, k_norm_g, w_out_conv, w_out_sgu, w_out_attn, w_o, ffn2_norm, ffn2_w_gu, ffn2_w_down, loss_target, m_ffn1_norm, m_ffn1_w_gu, m_ffn1_w_down, m_mix_norm, m_w_in, m_b_forget, m_b_gate, m_conv_w, m_sgu_ln_g, m_sgu_ln_b, m_sgu_w, m_sgu_b, m_q_norm_g, m_k_norm_g, m_w_out_conv, m_w_out_sgu, m_w_out_attn, m_w_o, m_ffn2_norm, m_ffn2_w_gu, m_ffn2_w_down, v_ffn1_norm, v_ffn1_w_gu, v_ffn1_w_down, v_mix_norm, v_w_in, v_b_forget, v_b_gate, v_conv_w, v_sgu_ln_g, v_sgu_ln_b, v_sgu_w, v_sgu_b, v_q_norm_g, v_k_norm_g, v_w_out_conv, v_w_out_sgu, v_w_out_attn, v_w_o, v_ffn2_norm, v_ffn2_w_gu, v_ffn2_w_down):
    w = dict(zip(WEIGHT_NAMES, (ffn1_norm, ffn1_w_gu, ffn1_w_down, mix_norm, w_in, b_forget, b_gate, conv_w, sgu_ln_g,
                                sgu_ln_b, sgu_w, sgu_b, q_norm_g, k_norm_g, w_out_conv, w_out_sgu, w_out_attn, w_o,
                                ffn2_norm, ffn2_w_gu, ffn2_w_down)))
    mom = dict(zip(WEIGHT_NAMES, (m_ffn1_norm, m_ffn1_w_gu, m_ffn1_w_down, m_mix_norm, m_w_in, m_b_forget, m_b_gate,
                                  m_conv_w, m_sgu_ln_g, m_sgu_ln_b, m_sgu_w, m_sgu_b, m_q_norm_g, m_k_norm_g,
                                  m_w_out_conv, m_w_out_sgu, m_w_out_attn, m_w_o, m_ffn2_norm, m_ffn2_w_gu,
                                  m_ffn2_w_down)))
    var = dict(zip(WEIGHT_NAMES, (v_ffn1_norm, v_ffn1_w_gu, v_ffn1_w_down, v_mix_norm, v_w_in, v_b_forget, v_b_gate,
                                  v_conv_w, v_sgu_ln_g, v_sgu_ln_b, v_sgu_w, v_sgu_b, v_q_norm_g, v_k_norm_g,
                                  v_w_out_conv, v_w_out_sgu, v_w_out_attn, v_w_o, v_ffn2_norm, v_ffn2_w_gu,
                                  v_ffn2_w_down)))
    px, py, pc = _place()
    dev = 4 * px + 2 * py + pc
    chip = 2 * px + py

    big_names = [n for n in WEIGHT_NAMES if n in BIG]
    local = [w[n].astype(bf16) for n in big_names]
    local.append(jnp.concatenate([w["b_gate"], w["conv_w"], jnp.zeros((2, 2, 128), f32)], axis=1))
    gathered = _all_gather("gather_weights", local)
    gw = {BIG[n]: g for n, g in zip(big_names, gathered[:-1])}
    gsmall = gathered[-1]
    wts = {
        "gu1": gw["gu1"], "gu2": gw["gu2"],
        "d1": gw["d1"].reshape(2, 4, GU, D), "d2": gw["d2"].reshape(2, 4, GU, D),
        "win": [_win_kernel_layout(gw["in"][l]) for l in range(2)],
        "sq": jnp.stack([gw[k].reshape(2, D, D) for k in ("oc", "os", "oa", "o")], axis=1),
    }
    full_small = {"b_gate": jnp.transpose(gsmall[:, :, 0:3, :], (0, 2, 1, 3)).reshape(2, 3, D),
                  "conv_w": jnp.transpose(gsmall[:, :, 3:6, :], (0, 2, 1, 3)).reshape(2, 3, D)}
    small = []
    for l in range(2):
        p = {n: w[n][l] for n in REPLICATED_SMALL}
        p.update({n: full_small[n][l] for n in SHARDED_SMALL})
        small.append(_small_params(p))

    shapes = {"gu1": (D, GU), "gu2": (D, GU), "d1": (GU // 2, D), "d2": (GU // 2, D), "oc": (128, D), "os": (128, D),
              "oa": (128, D), "o": (128, D)}
    gbufs = {k: lax.empty((2, 4, 2) + shp, bf16) for k, shp in shapes.items()}
    loss_row, dx, sgrads, dwin, gout = _local_step(x[0], loss_target[0], small, wts, gbufs)
    gout["in"] = jnp.stack([jnp.stack([jnp.stack([_win_device_block(dwin[l], 2 * q + c) for l in range(2)])
                                       for q in range(4)]) for c in range(2)])

    gs = [gout[k] for k in BIG_KEYS]
    r1 = _rs_pair("rs_pair", gs)
    core = pc.reshape(1).astype(jnp.int32)
    ss = [_pair_sum("pair_sum_" + k, core, g, r) for k, g, r in zip(BIG_KEYS, gs, r1)]
    r2 = _rs_chips("rs_chips", ss)

    grads, delta, new_m, new_v = {}, {}, {}, {}
    for n in big_names:
        i = BIG_KEYS.index(BIG[n])
        s_mine = lax.dynamic_index_in_dim(ss[i], chip, 0, keepdims=False)
        grads[n], delta[n], new_m[n], new_v[n] = _adamw("adamw_" + BIG[n], w[n], mom[n], var[n], s_mine, r2[i])

    nat = [_small_grads_natural(sgrads[l]) for l in range(2)]
    order = REPLICATED_SMALL + SHARDED_SMALL
    part = _pack([jnp.stack([nat[0][n], nat[1][n]]) for n in order] + [loss_row[0, 0:1]])
    total = _all_reduce_small("reduce_small", part)
    full_shapes = [(2,) + tuple(nat[0][n].shape) for n in order] + [(1,)]
    summed = dict(zip(order + ("loss",), _unpack(total, full_shapes)))
    for n in REPLICATED_SMALL:
        grads[n] = summed[n]
    for n in SHARDED_SMALL:
        grads[n] = lax.dynamic_slice_in_dim(summed[n], dev * 128, 128, axis=2)
    wp = _pack([w[n] for n in order])
    gp = _pack([grads[n] for n in order])
    mp = _pack([mom[n] for n in order])
    vp = _pack([var[n] for n in order])
    dpk, mpk, vpk = _adamw_small("adamw_small", wp, gp, mp, vp)
    local_shapes = [tuple(w[n].shape) for n in order]
    for dst, packed in ((delta, dpk), (new_m, mpk), (new_v, vpk)):
        dst.update(zip(order, _unpack(packed, local_shapes)))

    loss = summed["loss"][0]
    return (loss, dx[None], *[grads[n] for n in WEIGHT_NAMES], *[delta[n] for n in WEIGHT_NAMES],
            *[new_m[n] for n in WEIGHT_NAMES], *[new_v[n] for n in WEIGHT_NAMES])
```

```python
import functools

import jax
import jax.numpy as jnp
from jax import lax
from jax.experimental import pallas as pl
from jax.experimental.pallas import tpu as pltpu

f32 = jnp.float32
bf16 = jnp.bfloat16
S = jax.ShapeDtypeStruct
MESH = pl.DeviceIdType.MESH

D = 1024
NH = 8
HD = 128
NDEV = 8
GU = 704
NIN = 11272
INB = 1409
GATE_OFF = 8192
F_OFF = 11264
NZ = 11776
RMS_EPS = 1e-6
LN_EPS = 1e-5
ATT_SCALE = HD ** -0.5
NEG = -1e30
INV_SQRT2 = 0.7071067811865476
INV_SQRT2PI = 0.3989422804014327

ADAM_LR = 0.001
ADAM_B1 = 0.9
ADAM_B2 = 0.999
ADAM_EPS = 1e-08
ADAM_WD = 0.01
ADAM_STEP = 10

TT = 256
VMEM_LIMIT = 56 * 1024 * 1024


def _cp(*sem):
    return pltpu.CompilerParams(dimension_semantics=sem, vmem_limit_bytes=VMEM_LIMIT)


def _bs(shape, fn):
    return pl.BlockSpec(shape, fn)


NN = (((1,), (0,)), ((), ()))
NT = (((1,), (1,)), ((), ()))
TN = (((0,), (0,)), ((), ()))


def _mm(name, a, b, *, grid, a_spec, b_spec, out_shape, out_spec, dims, acc_shape, res=None, res_spec=None,
        alpha=1.0, alias=None, split_rows=None):
    nk = grid[2]

    def body(*refs):
        a_ref, b_ref = refs[0], refs[1]
        pos = 2
        res_ref = None
        if res is not None:
            res_ref = refs[pos]
            pos += 1
        if alias is not None:
            pos += 1
        o_ref = refs[pos]
        acc_ref = refs[pos + 1] if nk > 1 else None
        part = lax.dot_general(a_ref[...].astype(bf16), b_ref[...].astype(bf16), dims, preferred_element_type=f32)

        def finish(acc):
            if alpha != 1.0:
                acc = alpha * acc
            if res_ref is not None:
                acc = res_ref[...] + acc
            if split_rows is None:
                o_ref[...] = acc.astype(o_ref.dtype)
            else:
                o_ref[0] = acc[:split_rows].astype(o_ref.dtype)
                o_ref[1] = acc[split_rows:].astype(o_ref.dtype)

        if nk == 1:
            finish(part)
        else:
            k = pl.program_id(2)

            @pl.when(k == 0)
            def _():
                acc_ref[...] = part

            @pl.when(k > 0)
            def _():
                acc_ref[...] += part

            @pl.when(k == nk - 1)
            def _():
                finish(acc_ref[...])

    operands = [a, b]
    in_specs = [a_spec, b_spec]
    if res is not None:
        operands.append(res)
        in_specs.append(res_spec)
    aliases = {}
    if alias is not None:
        aliases = {len(operands): 0}
        operands.append(alias)
        in_specs.append(pl.BlockSpec(memory_space=pl.ANY))
    return pl.pallas_call(
        body, name=name, grid=grid, in_specs=in_specs, out_specs=out_spec, out_shape=out_shape,
        scratch_shapes=[pltpu.VMEM(acc_shape, f32)] if nk > 1 else [],
        input_output_aliases=aliases,
        compiler_params=_cp("parallel", "parallel", "arbitrary"),
    )(*operands)


def _tile(n, t):
    return t if n % t == 0 and n >= t else n


def _row(cb=0, w=D):
    return _bs((TT, w), lambda i: (i, cb))


def _vec(rows=1, w=D):
    return _bs((rows, w), lambda i: (0, 0))


def _acc_store(i, ref, val):
    @pl.when(i == 0)
    def _():
        ref[...] = val

    @pl.when(i > 0)
    def _():
        ref[...] += val


def _rms_fwd(name, x, g):
    T = x.shape[0]

    def body(x_ref, g_ref, o_ref):
        xv = x_ref[...]
        r = lax.rsqrt(jnp.mean(xv * xv, axis=-1, keepdims=True) + RMS_EPS)
        o_ref[...] = (xv * r * g_ref[...]).astype(bf16)

    return pl.pallas_call(body, name=name, grid=(T // TT,), in_specs=[_row(), _vec()], out_specs=_row(),
                          out_shape=S((T, D), bf16), compiler_params=_cp("parallel"))(x, g)


def _rms_bwd(name, dh, x, g, dres):
    T = x.shape[0]

    def body(dh_ref, x_ref, g_ref, dres_ref, dx_ref, dg_ref):
        i = pl.program_id(0)
        xv = x_ref[...]
        r = lax.rsqrt(jnp.mean(xv * xv, axis=-1, keepdims=True) + RMS_EPS)
        xhat = xv * r
        dh_v = dh_ref[...]
        dyg = dh_v * g_ref[...]
        m = jnp.mean(dyg * xhat, axis=-1, keepdims=True)
        dx_ref[...] = dres_ref[...] + r * (dyg - xhat * m)
        _acc_store(i, dg_ref, jnp.sum(dh_v * xhat, axis=0, keepdims=True))

    return pl.pallas_call(body, name=name, grid=(T // TT,), in_specs=[_row(), _row(), _vec(), _row()],
                          out_specs=[_row(), _vec()], out_shape=[S((T, D), f32), S((1, D), f32)],
                          compiler_params=_cp("arbitrary"))(dh, x, g, dres)


def _sigmoid(x):
    return 1.0 / (1.0 + jnp.exp(-x))


def _swiglu_fwd(name, gu):
    T = gu.shape[1]

    def body(g_ref, u_ref, o_ref):
        g = g_ref[...]
        o_ref[...] = (g * _sigmoid(g) * u_ref[...]).astype(bf16)

    return pl.pallas_call(
        body, name=name, grid=(4, T // TT),
        in_specs=[_bs((None, TT, GU), lambda j, i: (j, i, 0)), _bs((None, TT, GU), lambda j, i: (j + 4, i, 0))],
        out_specs=_bs((None, TT, GU), lambda j, i: (j, i, 0)), out_shape=S((4, T, GU), bf16),
        compiler_params=_cp("parallel", "parallel"))(gu, gu)


def _swiglu_bwd(name, gu, da):
    T = gu.shape[1]

    def body(g_ref, u_ref, da_ref, o_ref):
        g = g_ref[...]
        u = u_ref[...]
        da_v = da_ref[...]
        sg = _sigmoid(g)
        o_ref[0] = (da_v * u * (sg + g * sg * (1.0 - sg))).astype(bf16)
        o_ref[1] = (da_v * g * sg).astype(bf16)

    return pl.pallas_call(
        body, name=name, grid=(4, T // TT),
        in_specs=[_bs((None, TT, GU), lambda j, i: (j, i, 0)), _bs((None, TT, GU), lambda j, i: (j + 4, i, 0)),
                  _bs((None, TT, GU), lambda j, i: (j, i, 0))],
        out_specs=_bs((2, None, TT, GU), lambda j, i: (0, j, i, 0)), out_shape=S((2, 4, T, GU), bf16),
        compiler_params=_cp("parallel", "parallel"))(gu, gu, da)


def _loss(name, y, tgt):
    T = y.shape[0]

    def body(y_ref, t_ref, l_ref, dy_ref):
        i = pl.program_id(0)
        e = y_ref[...] - t_ref[...]
        dy_ref[...] = e * (1.0 / D)
        s = 0.5 * jnp.sum(jnp.mean(e * e, axis=-1, keepdims=True))
        _acc_store(i, l_ref, jnp.broadcast_to(s, (1, 128)))

    return pl.pallas_call(body, name=name, grid=(T // TT,), in_specs=[_row(), _row()],
                          out_specs=[_vec(1, 128), _row()], out_shape=[S((1, 128), f32), S((T, D), f32)],
                          compiler_params=_cp("arbitrary"))(y, tgt)


def _prev8(T, cb):
    return _bs((8, D), lambda i: (jnp.maximum(i * (TT // 8) - 1, 0), cb))


def _next8(T, cb):
    return _bs((8, D), lambda i: (jnp.minimum((i + 1) * (TT // 8), T // 8 - 1), cb))


def _conv_taps(i, ac_ref, ax_ref, pc_ref, px_ref):
    ca = ac_ref[...] * ax_ref[...]
    keep = (i > 0).astype(f32)
    p1 = pc_ref[7:8, :] * px_ref[7:8, :] * keep
    p2 = pc_ref[6:7, :] * px_ref[6:7, :] * keep
    row = lax.broadcasted_iota(jnp.int32, ca.shape, 0)
    s1 = jnp.where(row == 0, p1, pltpu.roll(ca, 1, 0))
    s2 = jnp.where(row == 0, p2, jnp.where(row == 1, p1, pltpu.roll(ca, 2, 0)))
    return ca, s1, s2


def _conv_fwd(name, z, cw):
    T = z.shape[0]

    def body(ab_ref, ac_ref, ax_ref, pc_ref, px_ref, w_ref, o_ref):
        i = pl.program_id(0)
        ca, s1, s2 = _conv_taps(i, ac_ref, ax_ref, pc_ref, px_ref)
        cv = w_ref[0:1, :] * s2 + w_ref[1:2, :] * s1 + w_ref[2:3, :] * ca
        o_ref[...] = (ab_ref[...] * cv).astype(bf16)

    return pl.pallas_call(
        body, name=name, grid=(T // TT,),
        in_specs=[_row(0), _row(1), _row(2), _prev8(T, 1), _prev8(T, 2), _vec(3)],
        out_specs=_row(), out_shape=S((T, D), bf16), compiler_params=_cp("parallel"))(z, z, z, z, z, cw)


def _conv_bwd(name, dya, z, cw):
    T = z.shape[0]
    n = T // TT

    def body(dya_ref, ab_ref, ac_ref, ax_ref, pc_ref, px_ref, ndya_ref, nab_ref, w_ref, dz_ref, dw_ref):
        i = pl.program_id(0)
        ca, s1, s2 = _conv_taps(i, ac_ref, ax_ref, pc_ref, px_ref)
        w0, w1, w2 = w_ref[0:1, :], w_ref[1:2, :], w_ref[2:3, :]
        cv = w0 * s2 + w1 * s1 + w2 * ca
        dya_v = dya_ref[...]
        ab = ab_ref[...]
        dcv = dya_v * ab
        keep = (i < n - 1).astype(f32)
        n1 = ndya_ref[0:1, :] * nab_ref[0:1, :] * keep
        n2 = ndya_ref[1:2, :] * nab_ref[1:2, :] * keep
        row = lax.broadcasted_iota(jnp.int32, dcv.shape, 0)
        f1 = jnp.where(row == TT - 1, n1, pltpu.roll(dcv, TT - 1, 0))
        f2 = jnp.where(row == TT - 1, n2, jnp.where(row == TT - 2, n1, pltpu.roll(dcv, TT - 2, 0)))
        dca = w2 * dcv + w1 * f1 + w0 * f2
        dz_ref[:, 0:D] = (dya_v * cv).astype(bf16)
        dz_ref[:, D:2 * D] = (dca * ax_ref[...]).astype(bf16)
        dz_ref[:, 2 * D:3 * D] = (dca * ac_ref[...]).astype(bf16)
        dw = jnp.concatenate([jnp.sum(dcv * s2, axis=0, keepdims=True), jnp.sum(dcv * s1, axis=0, keepdims=True),
                              jnp.sum(dcv * ca, axis=0, keepdims=True)], axis=0)
        _acc_store(i, dw_ref, dw)

    return pl.pallas_call(
        body, name=name, grid=(n,),
        in_specs=[_row(), _row(0), _row(1), _row(2), _prev8(T, 1), _prev8(T, 2), _next8(T, 0), _next8(T, 0), _vec(3)],
        out_specs=[_row(0, 3 * D), _vec(3)], out_shape=[S((T, 3 * D), bf16), S((3, D), f32)],
        compiler_params=_cp("arbitrary"))(dya, z, z, z, z, z, dya, z, cw)


def _gelu(x):
    return 0.5 * x * (1.0 + lax.erf(x * INV_SQRT2))


def _gelu_grad(x):
    return 0.5 * (1.0 + lax.erf(x * INV_SQRT2)) + x * jnp.exp(-0.5 * x * x) * INV_SQRT2PI


def _ln_stats(vv):
    mu = jnp.mean(vv, axis=-1, keepdims=True)
    xc = vv - mu
    rstd = lax.rsqrt(jnp.mean(xc * xc, axis=-1, keepdims=True) + LN_EPS)
    return xc * rstd, rstd


def _tril_w(w_ref, g):
    r = lax.broadcasted_iota(jnp.int32, (HD, HD), 0)
    c = lax.broadcasted_iota(jnp.int32, (HD, HD), 1)
    return jnp.where(c <= r, w_ref[g], 0.0).astype(bf16)


def _sgu_fwd(name, z, ln_g, ln_b, w_s, bmap):
    T = z.shape[0]

    def body(su_ref, sv_ref, lg_ref, lb_ref, w_ref, bm_ref, o_ref, vn_ref):
        xhat, _ = _ln_stats(_gelu(sv_ref[...]))
        vn_ref[...] = (xhat * lg_ref[...] + lb_ref[...]).astype(bf16)
        for g in range(NH):
            w = _tril_w(w_ref, g)
            cs = slice(g * HD, (g + 1) * HD)
            for c in range(TT // HD):
                rs = slice(c * HD, (c + 1) * HD)
                s = jnp.dot(w, vn_ref[rs, cs], preferred_element_type=f32) + bm_ref[:, cs]
                o_ref[rs, cs] = (_gelu(su_ref[rs, cs]) * s).astype(bf16)

    return pl.pallas_call(
        body, name=name, grid=(T // TT,),
        in_specs=[_row(3), _row(4), _vec(), _vec(), _bs((NH, HD, HD), lambda i: (0, 0, 0)), _vec(HD)],
        out_specs=_row(), out_shape=S((T, D), bf16), scratch_shapes=[pltpu.VMEM((TT, D), bf16)],
        compiler_params=_cp("parallel"))(z, z, ln_g, ln_b, w_s, bmap)


def _sgu_bwd(name, dyb, z, ln_g, ln_b, w_s, bmap):
    T = z.shape[0]

    def body(dyb_ref, su_ref, sv_ref, lg_ref, lb_ref, w_ref, bm_ref, dz_ref, dlg_ref, dlb_ref, dw_ref, db_ref,
             vn_ref, du_ref, dvn_ref):
        i = pl.program_id(0)
        sv = sv_ref[...]
        xhat, rstd = _ln_stats(_gelu(sv))
        vn_ref[...] = (xhat * lg_ref[...] + lb_ref[...]).astype(bf16)
        r = lax.broadcasted_iota(jnp.int32, (HD, HD), 0)
        cc = lax.broadcasted_iota(jnp.int32, (HD, HD), 1)
        for g in range(NH):
            w = _tril_w(w_ref, g)
            cs = slice(g * HD, (g + 1) * HD)
            dw = jnp.zeros((HD, HD), f32)
            db = jnp.zeros((HD, 1), f32)
            for c in range(TT // HD):
                rs = slice(c * HD, (c + 1) * HD)
                vnb = vn_ref[rs, cs]
                s = jnp.dot(w, vnb, preferred_element_type=f32) + bm_ref[:, cs]
                dy = dyb_ref[rs, cs]
                du_ref[rs, cs] = dy * s
                ds = dy * _gelu(su_ref[rs, cs])
                ds16 = ds.astype(bf16)
                dvn_ref[rs, cs] = lax.dot_general(w, ds16, TN, preferred_element_type=f32)
                dw = dw + lax.dot_general(ds16, vnb, NT, preferred_element_type=f32)
                db = db + jnp.sum(ds, axis=1, keepdims=True)
            dw = jnp.where(cc <= r, dw, 0.0)

            @pl.when(i == 0)
            def _():
                dw_ref[g] = dw
                db_ref[:, g:g + 1] = db

            @pl.when(i > 0)
            def _():
                dw_ref[g] += dw
                db_ref[:, g:g + 1] += db

        dvn = dvn_ref[...]
        dxh = dvn * lg_ref[...]
        m1 = jnp.mean(dxh, axis=-1, keepdims=True)
        m2 = jnp.mean(dxh * xhat, axis=-1, keepdims=True)
        dvv = rstd * (dxh - m1 - xhat * m2)
        dz_ref[:, 0:D] = (du_ref[...] * _gelu_grad(su_ref[...])).astype(bf16)
        dz_ref[:, D:2 * D] = (dvv * _gelu_grad(sv)).astype(bf16)
        _acc_store(i, dlg_ref, jnp.sum(dvn * xhat, axis=0, keepdims=True))
        _acc_store(i, dlb_ref, jnp.sum(dvn, axis=0, keepdims=True))

    return pl.pallas_call(
        body, name=name, grid=(T // TT,),
        in_specs=[_row(), _row(3), _row(4), _vec(), _vec(), _bs((NH, HD, HD), lambda i: (0, 0, 0)), _vec(HD)],
        out_specs=[_row(0, 2 * D), _vec(), _vec(), _bs((NH, HD, HD), lambda i: (0, 0, 0)), _bs((HD, NH), lambda i: (0, 0))],
        out_shape=[S((T, 2 * D), bf16), S((1, D), f32), S((1, D), f32), S((NH, HD, HD), f32), S((HD, NH), f32)],
        scratch_shapes=[pltpu.VMEM((TT, D), bf16), pltpu.VMEM((TT, D), f32), pltpu.VMEM((TT, D), f32)],
        compiler_params=_cp("arbitrary"))(dyb, z, z, ln_g, ln_b, w_s, bmap)


def _qk_fwd(name, z, qg, kg, bf):
    T = z.shape[0]

    def body(q_ref, k_ref, v_ref, zf_ref, qg_ref, kg_ref, bf_ref, qn_ref, kn_ref, vb_ref, lf_ref):
        for h in range(NH):
            cs = slice(h * HD, (h + 1) * HD)
            for src, gain, dst in ((q_ref, qg_ref, qn_ref), (k_ref, kg_ref, kn_ref)):
                xv = src[:, cs]
                r = lax.rsqrt(jnp.mean(xv * xv, axis=-1, keepdims=True) + RMS_EPS)
                dst[:, cs] = (xv * r * gain[:, cs]).astype(bf16)
        vb_ref[...] = v_ref[...].astype(bf16)
        xf = zf_ref[...] + bf_ref[...]
        lf_ref[...] = jnp.minimum(xf, 0.0) - jnp.log1p(jnp.exp(-jnp.abs(xf)))

    return pl.pallas_call(
        body, name=name, grid=(T // TT,),
        in_specs=[_row(5), _row(6), _row(7), _bs((TT, 128), lambda i: (i, F_OFF // 128)), _vec(), _vec(), _vec(1, 128)],
        out_specs=[_row(), _row(), _row(), _bs((TT, 128), lambda i: (i, 0))],
        out_shape=[S((T, D), bf16), S((T, D), bf16), S((T, D), bf16), S((T, 128), f32)],
        compiler_params=_cp("parallel"))(z, z, z, z, qg, kg, bf)


def _cum_fwd(name, logf):
    T = logf.shape[0]

    def body(lf_ref, ccol_ref, crow_ref, c_ref):
        c = lf_ref[...]
        row = lax.broadcasted_iota(jnp.int32, c.shape, 0)
        s = 1
        while s < T:
            c = c + jnp.where(row >= s, pltpu.roll(c, s, 0), 0.0)
            s *= 2
        c_ref[...] = c
        crow_ref[...] = c.T[0:NH, :]
        for h in range(NH):
            ccol_ref[h] = jnp.broadcast_to(c_ref[:, h:h + 1], (T, 128))

    return pl.pallas_call(body, name=name, out_shape=[S((NH, T, 128), f32), S((NH, T), f32)],
                          scratch_shapes=[pltpu.VMEM((T, 128), f32)],
                          compiler_params=pltpu.CompilerParams(vmem_limit_bytes=VMEM_LIMIT))(logf)


ATT_TILE = 512


def _fold(x, op=jnp.add):
    acc = x[:, 0:128]
    for t in range(1, x.shape[1] // 128):
        acc = op(acc, x[:, t * 128:(t + 1) * 128])
    return acc


def _to_row(col):
    return jnp.broadcast_to(col, (col.shape[0], 128)).T[0:1, :]


def _causal(t, keys_down=False):
    r = lax.broadcasted_iota(jnp.int32, (t, t), 0)
    c = lax.broadcasted_iota(jnp.int32, (t, t), 1)
    return r <= c if keys_down else c <= r


def _attn_fwd(name, qn, kn, vb, ccol, crow3):
    T = qn.shape[0]
    tq = _tile(T, ATT_TILE)
    nq = T // tq

    def body(q_ref, k_ref, v_ref, cc_ref, cr_ref, o_ref, lse_ref, lser_ref, s_ref):
        qi = pl.program_id(1)
        q = q_ref[...]
        cq = cc_ref[:, 0:1]

        def logits(off):
            s = lax.dot_general(q, k_ref[pl.ds(off, tq), :], NT, preferred_element_type=f32) * ATT_SCALE
            return s + cq - cr_ref[:, pl.ds(off, tq)]

        def below(j, mvec):
            off = pl.multiple_of(j * tq, tq)
            s = logits(off)
            s_ref[:, pl.ds(off, tq)] = s
            return jnp.maximum(mvec, _fold(s, jnp.maximum))

        mvec = lax.fori_loop(0, qi, below, jnp.full((tq, 128), NEG, f32))
        off = pl.multiple_of(qi * tq, tq)
        s = jnp.where(_causal(tq), logits(off), NEG)
        s_ref[:, pl.ds(off, tq)] = s
        m = jnp.max(jnp.maximum(mvec, _fold(s, jnp.maximum)), axis=1, keepdims=True)

        def weigh(j, carry):
            lvec, acc = carry
            off = pl.multiple_of(j * tq, tq)
            p = jnp.exp(s_ref[:, pl.ds(off, tq)] - m)
            acc = acc + jnp.dot(p.astype(bf16), v_ref[pl.ds(off, tq), :], preferred_element_type=f32)
            return lvec + _fold(p), acc

        lvec, acc = lax.fori_loop(0, qi + 1, weigh, (jnp.zeros((tq, 128), f32), jnp.zeros((tq, HD), f32)))
        l = jnp.sum(lvec, axis=1, keepdims=True)
        o_ref[...] = acc / l
        lse = m + jnp.log(l)
        lse_ref[...] = jnp.broadcast_to(lse, (tq, 128))
        lser_ref[...] = _to_row(lse)

    return pl.pallas_call(
        body, name=name, grid=(NH, nq),
        in_specs=[_bs((tq, HD), lambda h, i: (i, h)), _bs((T, HD), lambda h, i: (0, h)), _bs((T, HD), lambda h, i: (0, h)),
                  _bs((None, tq, 128), lambda h, i: (h, i, 0)), _bs((None, 1, T), lambda h, i: (h, 0, 0))],
        out_specs=[_bs((tq, HD), lambda h, i: (i, h)), _bs((None, tq, 128), lambda h, i: (h, i, 0)),
                   _bs((None, 1, tq), lambda h, i: (h, 0, i))],
        out_shape=[S((T, D), f32), S((NH, T, 128), f32), S((NH, 1, T), f32)],
        scratch_shapes=[pltpu.VMEM((tq, T), f32)],
        compiler_params=_cp("parallel", "parallel"))(qn, kn, vb, ccol, crow3)


def _attn_dq(name, qn, kn, vb, do, lse, ccol, crow3):
    T = qn.shape[0]
    tq = _tile(T, ATT_TILE)
    nq = T // tq

    def body(q_ref, k_ref, v_ref, do_ref, lse_ref, cc_ref, cr_ref, dq_ref, dlr_ref, p_ref, dp_ref):
        qi = pl.program_id(1)
        q = q_ref[...]
        do16 = do_ref[...].astype(bf16)
        base = cc_ref[:, 0:1] - lse_ref[:, 0:1]

        def probs(off):
            s = lax.dot_general(q, k_ref[pl.ds(off, tq), :], NT, preferred_element_type=f32) * ATT_SCALE
            return jnp.exp(s + base - cr_ref[:, pl.ds(off, tq)])

        def keep(off, p, dvec):
            dp = lax.dot_general(do16, v_ref[pl.ds(off, tq), :], NT, preferred_element_type=f32)
            p_ref[:, pl.ds(off, tq)] = p
            dp_ref[:, pl.ds(off, tq)] = dp
            return dvec + _fold(p * dp)

        def below(j, dvec):
            off = pl.multiple_of(j * tq, tq)
            return keep(off, probs(off), dvec)

        dvec = lax.fori_loop(0, qi, below, jnp.zeros((tq, 128), f32))
        off = pl.multiple_of(qi * tq, tq)
        dvec = keep(off, jnp.where(_causal(tq), probs(off), 0.0), dvec)
        delta = jnp.sum(dvec, axis=1, keepdims=True)

        def grad(j, acc):
            off = pl.multiple_of(j * tq, tq)
            ds = p_ref[:, pl.ds(off, tq)] * (dp_ref[:, pl.ds(off, tq)] - delta)
            return acc + jnp.dot(ds.astype(bf16), k_ref[pl.ds(off, tq), :], preferred_element_type=f32)

        dq_ref[...] = lax.fori_loop(0, qi + 1, grad, jnp.zeros((tq, HD), f32)) * ATT_SCALE
        dlr_ref[...] = _to_row(delta)

    qb = lambda h, i: (i, h)
    full = lambda h, i: (0, h)
    col = lambda h, i: (h, i, 0)
    return pl.pallas_call(
        body, name=name, grid=(NH, nq),
        in_specs=[_bs((tq, HD), qb), _bs((T, HD), full), _bs((T, HD), full), _bs((tq, HD), qb),
                  _bs((None, tq, 128), col), _bs((None, tq, 128), col), _bs((None, 1, T), lambda h, i: (h, 0, 0))],
        out_specs=[_bs((tq, HD), qb), _bs((None, 1, tq), lambda h, i: (h, 0, i))],
        out_shape=[S((T, D), f32), S((NH, 1, T), f32)],
        scratch_shapes=[pltpu.VMEM((tq, T), f32), pltpu.VMEM((tq, T), f32)],
        compiler_params=_cp("parallel", "parallel"))(qn, kn, vb, do, lse, ccol, crow3)


def _attn_dkv(name, qn, kn, vb, do, lser3, dlr3, ccol, crow3):
    T = qn.shape[0]
    tk = _tile(T, ATT_TILE)
    nk = T // tk

    def body(q_ref, k_ref, v_ref, do_ref, lser_ref, dlr_ref, cc_ref, cr_ref, dk_ref, dv_ref, cs_ref):
        h = pl.program_id(0)
        kj = pl.program_id(1)

        @pl.when((h == 0) & (kj == 0))
        def _():
            cs_ref[...] = jnp.zeros_like(cs_ref)

        kb = k_ref[...]
        vv = v_ref[...]
        ckey = cc_ref[:, 0:1]

        def block(off, diagonal):
            rows = pl.ds(off, tk)
            qb = q_ref[rows, :]
            do16 = do_ref[rows, :].astype(bf16)
            st = lax.dot_general(kb, qb, NT, preferred_element_type=f32) * ATT_SCALE
            pt = jnp.exp(st + (cr_ref[:, rows] - lser_ref[:, rows]) - ckey)
            if diagonal:
                pt = jnp.where(_causal(tk, keys_down=True), pt, 0.0)
            dpt = lax.dot_general(vv, do16, NT, preferred_element_type=f32)
            dst = pt * (dpt - dlr_ref[:, rows])
            ddv = jnp.dot(pt.astype(bf16), do16, preferred_element_type=f32)
            ddk = jnp.dot(dst.astype(bf16), qb, preferred_element_type=f32)
            return ddk, ddv, _fold(dst)

        def above(i, carry):
            ddk, ddv, dcs = block(pl.multiple_of(i * tk, tk), False)
            return carry[0] + ddk, carry[1] + ddv, carry[2] + dcs

        off = pl.multiple_of(kj * tk, tk)
        dk, dv, cs = lax.fori_loop(kj + 1, nk, above, block(off, True))
        dk_ref[...] = dk * ATT_SCALE
        dv_ref[...] = dv
        lane = lax.broadcasted_iota(jnp.int32, (tk, 128), 1)
        cs_ref[pl.ds(off, tk), :] += jnp.where(lane == h, jnp.sum(cs, axis=1, keepdims=True), 0.0)

    full = lambda h, j: (0, h)
    blk = lambda h, j: (j, h)
    row = lambda h, j: (h, 0, 0)
    return pl.pallas_call(
        body, name=name, grid=(NH, nk),
        in_specs=[_bs((T, HD), full), _bs((tk, HD), blk), _bs((tk, HD), blk), _bs((T, HD), full), _bs((None, 1, T), row),
                  _bs((None, 1, T), row), _bs((None, tk, 128), lambda h, j: (h, j, 0)), _bs((None, 1, T), row)],
        out_specs=[_bs((tk, HD), blk), _bs((tk, HD), blk), _bs((T, 128), lambda h, j: (0, 0))],
        out_shape=[S((T, D), f32), S((T, D), f32), S((T, 128), f32)],
        compiler_params=_cp("arbitrary", "arbitrary"))(qn, kn, vb, do, lser3, dlr3, ccol, crow3)


def _forget_bwd(name, cs, z, bf):
    T = cs.shape[0]

    def body(cs_ref, zf_ref, bf_ref, dz_ref, db_ref):
        c = -cs_ref[...]
        row = lax.broadcasted_iota(jnp.int32, c.shape, 0)
        s = 1
        while s < T:
            c = c + jnp.where(row + s < T, pltpu.roll(c, T - s, 0), 0.0)
            s *= 2
        xf = zf_ref[...] + bf_ref[...]
        lane = lax.broadcasted_iota(jnp.int32, c.shape, 1)
        dxf = jnp.where(lane < NH, c / (1.0 + jnp.exp(xf)), 0.0)
        dz_ref[...] = jnp.zeros_like(dz_ref)
        dz_ref[:, 0:128] = dxf.astype(bf16)
        db_ref[...] = jnp.sum(dxf, axis=0, keepdims=True)

    return pl.pallas_call(
        body, name=name, grid=(1,),
        in_specs=[_bs((T, 128), lambda i: (0, 0)), _bs((T, 128), lambda i: (0, F_OFF // 128)), _vec(1, 128)],
        out_specs=[_bs((T, NZ - F_OFF), lambda i: (0, 0)), _vec(1, 128)],
        out_shape=[S((T, NZ - F_OFF), bf16), S((1, 128), f32)], compiler_params=_cp("arbitrary"))(cs, z, bf)


def _qk_bwd(name, dqn, dkn, dv, z, qg, kg):
    T = z.shape[0]

    def body(dq_ref, dk_ref, dv_ref, q_ref, k_ref, qg_ref, kg_ref, dz_ref, dqg_ref, dkg_ref, g_ref):
        i = pl.program_id(0)
        for n, (src, dsrc, gain, dgain) in enumerate(((q_ref, dq_ref, qg_ref, dqg_ref), (k_ref, dk_ref, kg_ref, dkg_ref))):
            for h in range(NH):
                cs = slice(h * HD, (h + 1) * HD)
                xv = src[:, cs]
                r = lax.rsqrt(jnp.mean(xv * xv, axis=-1, keepdims=True) + RMS_EPS)
                xhat = xv * r
                dy = dsrc[:, cs]
                dyg = dy * gain[:, cs]
                m = jnp.mean(dyg * xhat, axis=-1, keepdims=True)
                dz_ref[:, n * D + h * HD:n * D + (h + 1) * HD] = (r * (dyg - xhat * m)).astype(bf16)
                g_ref[:, cs] = jnp.sum(dy * xhat, axis=0, keepdims=True)
            _acc_store(i, dgain, g_ref[...])
        dz_ref[:, 2 * D:3 * D] = dv_ref[...].astype(bf16)

    return pl.pallas_call(
        body, name=name, grid=(T // TT,),
        in_specs=[_row(), _row(), _row(), _row(5), _row(6), _vec(), _vec()],
        out_specs=[_row(0, 3 * D), _vec(), _vec()], out_shape=[S((T, 3 * D), bf16), S((1, D), f32), S((1, D), f32)],
        scratch_shapes=[pltpu.VMEM((1, D), f32)], compiler_params=_cp("arbitrary"))(dqn, dkn, dv, z, z, qg, kg)


GB = GATE_OFF // D


def _merge_fwd(name, ya, yb, yc, z, bg):
    T = z.shape[0]

    def body(ya_ref, yb_ref, yc_ref, g0_ref, g1_ref, g2_ref, bg_ref, o_ref):
        acc = _sigmoid(g0_ref[...] + bg_ref[0:1, :]) * ya_ref[...]
        acc = acc + _sigmoid(g1_ref[...] + bg_ref[1:2, :]) * yb_ref[...]
        acc = acc + _sigmoid(g2_ref[...] + bg_ref[2:3, :]) * yc_ref[...]
        o_ref[...] = acc.astype(bf16)

    return pl.pallas_call(
        body, name=name, grid=(T // TT,),
        in_specs=[_row(), _row(), _row(), _row(GB), _row(GB + 1), _row(GB + 2), _vec(3)],
        out_specs=_row(), out_shape=S((T, D), bf16), compiler_params=_cp("parallel"))(ya, yb, yc, z, z, z, bg)


def _merge_bwd(name, dm, ya, yb, yc, z, bg):
    T = z.shape[0]

    def body(dm_ref, ya_ref, yb_ref, yc_ref, g0_ref, g1_ref, g2_ref, bg_ref, dya_ref, dyb_ref, dyc_ref, dz_ref, db_ref):
        i = pl.program_id(0)
        dm_v = dm_ref[...]
        dbs = []
        for n, (y_ref, g_ref, dy_ref) in enumerate(((ya_ref, g0_ref, dya_ref), (yb_ref, g1_ref, dyb_ref),
                                                    (yc_ref, g2_ref, dyc_ref))):
            gate = _sigmoid(g_ref[...] + bg_ref[n:n + 1, :])
            dy_ref[...] = (dm_v * gate).astype(bf16)
            dl = dm_v * y_ref[...] * gate * (1.0 - gate)
            dz_ref[:, n * D:(n + 1) * D] = dl.astype(bf16)
            dbs.append(jnp.sum(dl, axis=0, keepdims=True))
        _acc_store(i, db_ref, jnp.concatenate(dbs, axis=0))

    return pl.pallas_call(
        body, name=name, grid=(T // TT,),
        in_specs=[_row(), _row(), _row(), _row(), _row(GB), _row(GB + 1), _row(GB + 2), _vec(3)],
        out_specs=[_row(), _row(), _row(), _row(0, 3 * D), _vec(3)],
        out_shape=[S((T, D), bf16), S((T, D), bf16), S((T, D), bf16), S((T, 3 * D), bf16), S((3, D), f32)],
        compiler_params=_cp("arbitrary"))(dm, ya, yb, yc, z, z, z, bg)


SMALL_NAMES = ("ffn1_norm", "mix_norm", "b_forget", "b_gate", "conv_w", "sgu_ln_g", "sgu_ln_b", "sgu_w", "sgu_b",
               "q_norm_g", "k_norm_g", "ffn2_norm")


def _small_params(p):
    out = {n: p[n].reshape(1, D) for n in ("ffn1_norm", "mix_norm", "ffn2_norm", "sgu_ln_g", "sgu_ln_b", "q_norm_g", "k_norm_g")}
    out["b_forget"] = jnp.pad(p["b_forget"].reshape(1, NH), ((0, 0), (0, 128 - NH)))
    out["b_gate"] = p["b_gate"]
    out["conv_w"] = p["conv_w"]
    out["sgu_w"] = p["sgu_w"]
    out["bmap"] = jnp.repeat(p["sgu_b"].T, HD, axis=1)
    return out


def _small_grads_natural(sg):
    out = {n: sg[n].reshape(D) for n in ("ffn1_norm", "mix_norm", "ffn2_norm", "sgu_ln_g", "sgu_ln_b")}
    out["q_norm_g"] = sg["q_norm_g"].reshape(NH, HD)
    out["k_norm_g"] = sg["k_norm_g"].reshape(NH, HD)
    out["b_forget"] = sg["b_forget"][0, :NH]
    out["b_gate"] = sg["b_gate"]
    out["conv_w"] = sg["conv_w"]
    out["sgu_w"] = sg["sgu_w"]
    out["sgu_b"] = sg["sgu_b"]
    return out


def _sq_fwd(name, a, wsq, l, n, res=None):
    T = a.shape[0]
    tm = _tile(T, 512)
    return _mm(name, a, wsq, grid=(T // tm, 1, 1), a_spec=_bs((tm, D), lambda i, j, k: (i, 0)),
               b_spec=_bs((None, None, D, D), lambda i, j, k: (l, n, 0, 0)),
               out_shape=S((T, D), f32), out_spec=_bs((tm, D), lambda i, j, k: (i, 0)), dims=NN, acc_shape=None,
               res=res, res_spec=_bs((tm, D), lambda i, j, k: (i, 0)))


def _sq_bwd_in(name, dy, wsq, l, n):
    T = dy.shape[0]
    tm = _tile(T, 512)
    return _mm(name, dy, wsq, grid=(T // tm, 1, 1), a_spec=_bs((tm, D), lambda i, j, k: (i, 0)),
               b_spec=_bs((None, None, D, D), lambda i, j, k: (l, n, 0, 0)),
               out_shape=S((T, D), f32), out_spec=_bs((tm, D), lambda i, j, k: (i, 0)), dims=NT, acc_shape=None)


def _sq_bwd_w(name, a, dy, gbuf, l):
    T = a.shape[0]
    return _mm(name, a, dy, grid=(NDEV, 1, 1), a_spec=_bs((T, 128), lambda i, j, k: (0, i)),
               b_spec=_bs((T, D), lambda i, j, k: (0, 0)), out_shape=S(gbuf.shape, bf16),
               out_spec=_bs((None, None, None, 128, D), lambda i, j, k: (i % 2, i // 2, l, 0, 0)),
               dims=TN, acc_shape=None, alias=gbuf)


def _ffn_fwd(tag, x, g, wgu, wd, l):
    T = x.shape[0]
    tm = _tile(T, 1024)
    h = _rms_fwd(tag + "_rms", x, g)
    gu = _mm(tag + "_gu", h, wgu, grid=(T // tm, NDEV, 1), a_spec=_bs((tm, D), lambda i, j, k: (i, 0)),
             b_spec=_bs((None, None, D, GU), lambda i, j, k: (l, j, 0, 0)), out_shape=S((NDEV, T, GU), f32),
             out_spec=_bs((None, tm, GU), lambda i, j, k: (j, i, 0)), dims=NN, acc_shape=None)
    a = _swiglu_fwd(tag + "_act", gu)
    tm = _tile(T, 1024)
    xo = _mm(tag + "_down", a, wd, grid=(T // tm, 1, 4), a_spec=_bs((None, tm, GU), lambda i, j, k: (k, i, 0)),
             b_spec=_bs((None, None, GU, D), lambda i, j, k: (l, k, 0, 0)), out_shape=S((T, D), f32),
             out_spec=_bs((tm, D), lambda i, j, k: (i, 0)), dims=NN, acc_shape=(tm, D), res=x,
             res_spec=_bs((tm, D), lambda i, j, k: (i, 0)), alpha=0.5)
    return xo, (h, gu, a)


def _ffn_bwd(tag, dxo, x, g, wgu, wd, l, saved, g_gu, g_d):
    h, gu, a = saved
    T = x.shape[0]
    tm = _tile(T, 512)
    da = _mm(tag + "_dact", dxo, wd, grid=(T // tm, 4, 1), a_spec=_bs((tm, D), lambda i, j, k: (i, 0)),
             b_spec=_bs((None, None, GU, D), lambda i, j, k: (l, j, 0, 0)), out_shape=S((4, T, GU), f32),
             out_spec=_bs((None, tm, GU), lambda i, j, k: (j, i, 0)), dims=NT, acc_shape=None, alpha=0.5)
    g_d = _mm(tag + "_dwd", a, dxo, grid=(4, 1, 1), a_spec=_bs((None, T, GU), lambda i, j, k: (i, 0, 0)),
              b_spec=_bs((T, D), lambda i, j, k: (0, 0)), out_shape=S(g_d.shape, bf16),
              out_spec=_bs((2, None, None, GU // 2, D), lambda i, j, k: (0, i, l, 0, 0)), dims=TN, acc_shape=None,
              alpha=0.5, alias=g_d, split_rows=GU // 2)
    dgu = _swiglu_bwd(tag + "_dgu", gu, da).reshape(NDEV, T, GU)
    dh = _mm(tag + "_dh", dgu, wgu, grid=(1, 1, NDEV), a_spec=_bs((None, T, GU), lambda i, j, k: (k, 0, 0)),
             b_spec=_bs((None, None, D, GU), lambda i, j, k: (l, k, 0, 0)), out_shape=S((T, D), f32),
             out_spec=_bs((T, D), lambda i, j, k: (0, 0)), dims=NT, acc_shape=(T, D))
    g_gu = _mm(tag + "_dwgu", h, dgu, grid=(1, NDEV, 1), a_spec=_bs((T, D), lambda i, j, k: (0, 0)),
               b_spec=_bs((None, T, GU), lambda i, j, k: (j, 0, 0)), out_shape=S(g_gu.shape, bf16),
               out_spec=_bs((None, None, None, D, GU), lambda i, j, k: (j % 2, j // 2, l, 0, 0)), dims=TN,
               acc_shape=None, alias=g_gu)
    dx, dg = _rms_bwd(tag + "_drms", dh, x, g, dxo)
    return dx, dg, g_gu, g_d


def _mixer_fwd(tag, x, p, win, wsq, l):
    T = x.shape[0]
    h = _rms_fwd(tag + "_rms", x, p["mix_norm"])
    tn = 512
    z = _mm(tag + "_in", h, win, grid=(1, NZ // tn, 1), a_spec=_bs((T, D), lambda i, j, k: (0, 0)),
            b_spec=_bs((D, tn), lambda i, j, k: (0, j)), out_shape=S((T, NZ), f32),
            out_spec=_bs((T, tn), lambda i, j, k: (0, j)), dims=NN, acc_shape=None)
    ya_in = _conv_fwd(tag + "_conv", z, p["conv_w"])
    yb_in = _sgu_fwd(tag + "_sgu", z, p["sgu_ln_g"], p["sgu_ln_b"], p["sgu_w"], p["bmap"])
    qn, kn, vb, logf = _qk_fwd(tag + "_qk", z, p["q_norm_g"], p["k_norm_g"], p["b_forget"])
    ccol, crow = _cum_fwd(tag + "_cum", logf)
    crow3 = crow.reshape(NH, 1, T)
    o, lse, lser = _attn_fwd(tag + "_attn", qn, kn, vb, ccol, crow3)
    ya = _sq_fwd(tag + "_oconv", ya_in, wsq, l, 0)
    yb = _sq_fwd(tag + "_osgu", yb_in, wsq, l, 1)
    yc = _sq_fwd(tag + "_oattn", o, wsq, l, 2)
    merged = _merge_fwd(tag + "_merge", ya, yb, yc, z, p["b_gate"])
    xo = _sq_fwd(tag + "_o", merged, wsq, l, 3, res=x)
    return xo, (h, z, ya_in, yb_in, qn, kn, vb, ccol, crow3, o, lse, lser, ya, yb, yc, merged)


def _mixer_bwd(tag, dxo, x, p, win, wsq, l, saved, gsq):
    h, z, ya_in, yb_in, qn, kn, vb, ccol, crow3, o, lse, lser, ya, yb, yc, merged = saved
    T = x.shape[0]
    sg = {}
    dm = _sq_bwd_in(tag + "_dmerged", dxo, wsq, l, 3)
    gsq[3] = _sq_bwd_w(tag + "_dwo", merged, dxo, gsq[3], l)
    dya, dyb, dyc, dz_g, sg["b_gate"] = _merge_bwd(tag + "_dmerge", dm, ya, yb, yc, z, p["b_gate"])
    d_ya_in = _sq_bwd_in(tag + "_dconv_in", dya, wsq, l, 0)
    gsq[0] = _sq_bwd_w(tag + "_dwoc", ya_in, dya, gsq[0], l)
    d_yb_in = _sq_bwd_in(tag + "_dsgu_in", dyb, wsq, l, 1)
    gsq[1] = _sq_bwd_w(tag + "_dwos", yb_in, dyb, gsq[1], l)
    d_o = _sq_bwd_in(tag + "_dattn_in", dyc, wsq, l, 2)
    gsq[2] = _sq_bwd_w(tag + "_dwoa", o, dyc, gsq[2], l)
    dz_c, sg["conv_w"] = _conv_bwd(tag + "_dconv", d_ya_in, z, p["conv_w"])
    dz_s, sg["sgu_ln_g"], sg["sgu_ln_b"], sg["sgu_w"], db_t = _sgu_bwd(
        tag + "_dsgu", d_yb_in, z, p["sgu_ln_g"], p["sgu_ln_b"], p["sgu_w"], p["bmap"])
    sg["sgu_b"] = db_t.T
    dqn, dlr = _attn_dq(tag + "_dattn_q", qn, kn, vb, d_o, lse, ccol, crow3)
    dkn, dv, cs = _attn_dkv(tag + "_dattn_kv", qn, kn, vb, d_o, lser, dlr, ccol, crow3)
    dz_f, sg["b_forget"] = _forget_bwd(tag + "_dforget", cs, z, p["b_forget"])
    dz_q, sg["q_norm_g"], sg["k_norm_g"] = _qk_bwd(tag + "_dqk", dqn, dkn, dv, z, p["q_norm_g"], p["k_norm_g"])
    dz = jnp.concatenate([dz_c, dz_s, dz_q, dz_g, dz_f], axis=1)
    tk = 512
    dh = _mm(tag + "_dh", dz, win, grid=(1, 1, NZ // tk), a_spec=_bs((T, tk), lambda i, j, k: (0, k)),
             b_spec=_bs((D, tk), lambda i, j, k: (0, k)), out_shape=S((T, D), f32),
             out_spec=_bs((T, D), lambda i, j, k: (0, 0)), dims=NT, acc_shape=(T, D))
    tn = 512
    dwin = _mm(tag + "_dwin", h, dz, grid=(1, NZ // tn, 1), a_spec=_bs((T, D), lambda i, j, k: (0, 0)),
               b_spec=_bs((T, tn), lambda i, j, k: (0, j)), out_shape=S((D, NZ), bf16),
               out_spec=_bs((D, tn), lambda i, j, k: (0, j)), dims=TN, acc_shape=None)
    dx, sg["mix_norm"] = _rms_bwd(tag + "_drms", dh, x, p["mix_norm"], dxo)
    return dx, sg, dwin, gsq


def _layer_bwd(dx, p, wt, saved):
    x0, x1, x2, s1, sm, s2 = saved
    buf = lambda r, c: lax.empty((2, 4, 1, r, c), bf16)
    dx, dn2, g_gu2, g_d2 = _ffn_bwd("ffn2", dx, x2, p["ffn2_norm"], wt["gu2"], wt["d2"], 0, s2, buf(D, GU), buf(GU // 2, D))
    dx, sg, dwin, gsq = _mixer_bwd("mix", dx, x1, p, wt["win"], wt["sq"], 0, sm, [buf(128, D) for _ in range(4)])
    dx, dn1, g_gu1, g_d1 = _ffn_bwd("ffn1", dx, x0, p["ffn1_norm"], wt["gu1"], wt["d1"], 0, s1, buf(D, GU), buf(GU // 2, D))
    sg["ffn1_norm"] = dn1
    sg["ffn2_norm"] = dn2
    gout = {"gu1": g_gu1, "d1": g_d1, "gu2": g_gu2, "d2": g_d2, "oc": gsq[0], "os": gsq[1], "oa": gsq[2], "o": gsq[3]}
    return dx, sg, dwin, gout


ANY = pl.BlockSpec(memory_space=pl.ANY)
HBM = pl.BlockSpec(memory_space=pltpu.HBM)
SEM = pl.BlockSpec(memory_space=pltpu.SEMAPHORE)
EFFECT = pltpu.SideEffectType.DATAFLOW_SIDE_EFFECTING


def _place():
    return lax.axis_index("x"), lax.axis_index("y"), lax.axis_index("c")


def _others(x, y, c):
    flips = [(dx, dy, dc) for dx in (0, 1) for dy in (0, 1) for dc in (0, 1)][1:]
    return [(1 - x if dx else x, 1 - y if dy else y, 1 - c if dc else c) for dx, dy, dc in flips]


def _gather_start(name, groups):
    sizes = [len(g) for g in groups]
    srcs = [s for g in groups for s, _ in g]
    lands = [b for g in groups for _, b in g]
    n, ng = len(srcs), len(groups)

    def body(*refs):
        src_refs, land_refs = refs[:n], refs[n:2 * n]
        send, recv = refs[2 * n:2 * n + ng], refs[2 * n + ng:2 * n + 2 * ng]
        x, y, c = _place()
        me = 4 * x + 2 * y + c
        u = 0
        for g, size in enumerate(sizes):
            for i in range(size):
                for k, peer in enumerate(_others(x, y, c)):
                    pltpu.make_async_remote_copy(src_ref=src_refs[u], dst_ref=land_refs[u].at[me],
                                                 send_sem=send[g].at[i * 7 + k], recv_sem=recv[g].at[i * 7 + k],
                                                 device_id=peer, device_id_type=MESH).start()
                u += 1

    sems = [pltpu.SemaphoreType.DMA((size * 7,)) for size in sizes]
    out = pl.pallas_call(
        body, name=name, in_specs=[HBM] * (2 * n), out_specs=[SEM] * (2 * ng) + [HBM] * (2 * n),
        out_shape=sems + sems + [pltpu.HBM(a.shape, a.dtype) for a in srcs + lands],
        input_output_aliases={i: 2 * ng + i for i in range(2 * n)},
        compiler_params=pltpu.CompilerParams(has_side_effects=EFFECT),
    )(*[pltpu.with_memory_space_constraint(a, pltpu.HBM) for a in srcs + lands])
    res, pos = [], 0
    for g, size in enumerate(sizes):
        res.append((out[g], out[ng + g], out[2 * ng + pos:2 * ng + pos + size], out[2 * ng + n + pos:2 * ng + n + pos + size]))
        pos += size
    return res


def _gather_wait(name, started, after=None):
    send, recv, srcs, lands = started
    n = len(srcs)

    def body(*refs):
        src_refs, land_refs = refs[:n], refs[n:2 * n]
        send_ref, recv_ref = refs[2 * n], refs[2 * n + 1]
        x, y, c = _place()
        for i in range(n):
            for k, (px, py, pc) in enumerate(_others(x, y, c)):
                cp = pltpu.make_async_remote_copy(src_ref=src_refs[i], dst_ref=land_refs[i].at[4 * px + 2 * py + pc],
                                                  send_sem=send_ref.at[i * 7 + k], recv_sem=recv_ref.at[i * 7 + k],
                                                  device_id=(px, py, pc), device_id_type=MESH)
                cp.wait_send()
                cp.wait_recv()

    extra = [] if after is None else [after]
    out = pl.pallas_call(
        body, name=name, in_specs=[HBM] * (2 * n) + [SEM, SEM] + [ANY] * len(extra), out_specs=[HBM] * (2 * n),
        out_shape=[pltpu.HBM(a.shape, a.dtype) for a in list(srcs) + list(lands)],
        input_output_aliases={i: i for i in range(2 * n)},
        compiler_params=pltpu.CompilerParams(has_side_effects=EFFECT),
    )(*srcs, *lands, send, recv, *extra)
    return out[n:]


def _rs_pair(name, gs):
    n = len(gs)

    def body(*refs):
        ins, outs = refs[:n], refs[n:2 * n]
        send, recv = refs[2 * n:]
        x, y, c = _place()
        cps = [pltpu.make_async_remote_copy(src_ref=ins[u].at[1 - c], dst_ref=outs[u], send_sem=send.at[u],
                                            recv_sem=recv.at[u], device_id=(x, y, 1 - c), device_id_type=MESH)
               for u in range(n)]
        for cp in cps:
            cp.start()
        for cp in cps:
            cp.wait()

    return pl.pallas_call(
        body, name=name, in_specs=[ANY] * n, out_specs=[ANY] * n, out_shape=[S(g.shape[1:], g.dtype) for g in gs],
        scratch_shapes=[pltpu.SemaphoreType.DMA((n,)), pltpu.SemaphoreType.DMA((n,))],
    )(*gs)


def _row_tile(r, c):
    return 128 if (r % 128 == 0 and c > D) else (256 if r % 256 == 0 else r)


def _pair_sum(name, core, g, r1):
    _, nq, nl, r, c = g.shape
    tr = _row_tile(r, c)
    g4 = g.reshape(2, nq * nl, r, c)
    r3 = r1.reshape(nq * nl, r, c)

    def body(core_ref, g_ref, r_ref, o_ref):
        o_ref[...] = (g_ref[...].astype(f32) + r_ref[...].astype(f32)).astype(bf16)

    out = pl.pallas_call(
        body, name=name,
        grid_spec=pltpu.PrefetchScalarGridSpec(
            num_scalar_prefetch=1, grid=(nq * nl, r // tr),
            in_specs=[_bs((None, None, tr, c), lambda b, i, cr: (cr[0], b, i, 0)), _bs((None, tr, c), lambda b, i, cr: (b, i, 0))],
            out_specs=_bs((None, tr, c), lambda b, i, cr: (b, i, 0))),
        out_shape=S((nq * nl, r, c), bf16), compiler_params=_cp("parallel", "parallel"))(core, g4, r3)
    return out.reshape(nq, nl, r, c)


def _rs_chips(name, ss):
    n = len(ss)

    def body(*refs):
        ins, outs = refs[:n], refs[n:2 * n]
        send, recv = refs[2 * n:]
        x, y, c = _place()
        chips = [(1 - x, y), (x, 1 - y), (1 - x, 1 - y)]
        cps = []
        for u in range(n):
            for k, chip in enumerate(chips):
                cps.append(pltpu.make_async_remote_copy(
                    src_ref=ins[u].at[2 * chip[0] + chip[1]], dst_ref=outs[u].at[k], send_sem=send.at[u * 3 + k],
                    recv_sem=recv.at[u * 3 + k], device_id=(*chip, c), device_id_type=MESH))
        for cp in cps:
            cp.start()
        for cp in cps:
            cp.wait()

    return pl.pallas_call(
        body, name=name, in_specs=[ANY] * n, out_specs=[ANY] * n, out_shape=[S((3,) + s.shape[1:], s.dtype) for s in ss],
        scratch_shapes=[pltpu.SemaphoreType.DMA((n * 3,)), pltpu.SemaphoreType.DMA((n * 3,))],
    )(*ss)


def _all_reduce_small(name, v):
    r = v.shape[0]

    def body(v_ref, o_ref, buf, send, recv, lsem):
        x, y, c = _place()
        me, sib = (x, y, c), (x, y, 1 - c)
        chips = [(1 - x, y), (x, 1 - y), (1 - x, 1 - y)]

        def slot(px, py, pc):
            return buf.at[4 * px + 2 * py + pc]

        def copy(k, block, to, src=None):
            return pltpu.make_async_remote_copy(src_ref=slot(*block) if src is None else src, dst_ref=slot(*block),
                                                send_sem=send.at[k], recv_sem=recv.at[k], device_id=to,
                                                device_id_type=MESH)

        mine = pltpu.make_async_copy(v_ref, slot(*me), lsem)
        mine.start()
        first = [copy(0, me, sib, v_ref)] + [copy(1 + j, me, (*chip, c), v_ref) for j, chip in enumerate(chips)]
        for cp in first:
            cp.start()
        passed = [copy(4 + j, (*chip, c), sib) for j, chip in enumerate(chips)]
        for j, chip in enumerate(chips):
            copy(1 + j, (*chip, c), me).wait_recv()
            passed[j].start()
        copy(0, sib, me).wait_recv()
        for j, chip in enumerate(chips):
            copy(4 + j, (*chip, 1 - c), me).wait_recv()
        for cp in first + passed:
            cp.wait_send()
        mine.wait()
        acc = buf[0]
        for d in range(1, NDEV):
            acc = acc + buf[d]
        o_ref[...] = acc

    vm = pl.BlockSpec(memory_space=pltpu.VMEM)
    return pl.pallas_call(
        body, name=name, in_specs=[vm], out_specs=vm, out_shape=S((r, 128), f32),
        scratch_shapes=[pltpu.VMEM((NDEV, r, 128), f32), pltpu.SemaphoreType.DMA((7,)), pltpu.SemaphoreType.DMA((7,)),
                        pltpu.SemaphoreType.DMA],
        compiler_params=pltpu.CompilerParams(vmem_limit_bytes=VMEM_LIMIT),
    )(v)


def _adam_math(w, g, m, v):
    m = ADAM_B1 * m + (1.0 - ADAM_B1) * g
    v = ADAM_B2 * v + (1.0 - ADAM_B2) * (g * g)
    m_hat = m / (1.0 - ADAM_B1 ** ADAM_STEP)
    v_hat = v / (1.0 - ADAM_B2 ** ADAM_STEP)
    delta = -ADAM_LR * (m_hat / (jnp.sqrt(v_hat) + ADAM_EPS) + ADAM_WD * w)
    return delta, m, v


def _adamw(name, w, m, v, parts):
    _, r, c = w.shape
    tr = _row_tile(r, c)

    def body(w_ref, m_ref, v_ref, *refs):
        sets, (g_ref, d_ref, mo_ref, vo_ref) = (refs[0:4], refs[4:8]), refs[8:]
        for l in range(2):
            @pl.when(pl.program_id(0) == l)
            def _():
                s_ref, r0_ref, r1_ref, r2_ref = sets[l]
                g = ((s_ref[...].astype(f32) + r0_ref[...].astype(f32)) + r1_ref[...].astype(f32)) + r2_ref[...].astype(f32)
                g_ref[...] = g
                d_ref[...], mo_ref[...], vo_ref[...] = _adam_math(w_ref[...], g, m_ref[...], v_ref[...])

    blk = _bs((None, tr, c), lambda l, i: (l, i, 0))
    operands, specs = [], []
    for n in range(2):
        row = (lambda l, i: i * (1 - l)) if n == 0 else (lambda l, i: i * l)
        s_mine, r2 = parts[n]
        operands += [s_mine, r2, r2, r2]
        specs.append(_bs((tr, c), functools.partial(lambda l, i, row: (row(l, i), 0), row=row)))
        specs += [_bs((None, None, tr, c), functools.partial(lambda l, i, k, row: (k, 0, row(l, i), 0), k=k, row=row))
                  for k in range(3)]
    return pl.pallas_call(
        body, name=name, grid=(2, r // tr), in_specs=[blk, blk, blk] + specs,
        out_specs=[blk] * 4, out_shape=[S(w.shape, f32)] * 4, compiler_params=_cp("arbitrary", "arbitrary"),
    )(w, m, v, *operands)


def _adamw_small(name, w, g, m, v):
    def body(w_ref, g_ref, m_ref, v_ref, d_ref, mo_ref, vo_ref):
        d_ref[...], mo_ref[...], vo_ref[...] = _adam_math(w_ref[...], g_ref[...], m_ref[...], v_ref[...])

    return pl.pallas_call(body, name=name, out_shape=[S(w.shape, f32)] * 3,
                          compiler_params=pltpu.CompilerParams(vmem_limit_bytes=VMEM_LIMIT))(w, g, m, v)


WEIGHT_NAMES = ("ffn1_norm", "ffn1_w_gu", "ffn1_w_down", "mix_norm", "w_in", "b_forget", "b_gate", "conv_w", "sgu_ln_g",
                "sgu_ln_b", "sgu_w", "sgu_b", "q_norm_g", "k_norm_g", "w_out_conv", "w_out_sgu", "w_out_attn", "w_o",
                "ffn2_norm", "ffn2_w_gu", "ffn2_w_down")
BIG = {"ffn1_w_gu": "gu1", "ffn2_w_gu": "gu2", "ffn1_w_down": "d1", "ffn2_w_down": "d2", "w_in": "in",
       "w_out_conv": "oc", "w_out_sgu": "os", "w_out_attn": "oa", "w_o": "o"}
BIG_KEYS = ("gu1", "gu2", "d1", "d2", "in", "oc", "os", "oa", "o")
REPLICATED_SMALL = ("ffn1_norm", "mix_norm", "b_forget", "sgu_ln_g", "sgu_ln_b", "sgu_w", "sgu_b", "q_norm_g",
                    "k_norm_g", "ffn2_norm")
SHARDED_SMALL = ("b_gate", "conv_w")


def _pack(arrays):
    flat = jnp.concatenate([a.reshape(-1).astype(f32) for a in arrays])
    rows = -(-flat.shape[0] // 1024) * 8
    return jnp.pad(flat, (0, rows * 128 - flat.shape[0])).reshape(rows, 128)


def _unpack(packed, shapes):
    flat = packed.reshape(-1)
    out, pos = [], 0
    for shp in shapes:
        size = 1
        for s_ in shp:
            size *= s_
        out.append(flat[pos:pos + size].reshape(shp))
        pos += size
    return out


def _natural_runs(a, b):
    runs = []
    while a < b:
        d = a // INB
        e = min(b, (d + 1) * INB)
        runs.append((d, a - d * INB, e - d * INB))
        a = e
    return runs


def _win_kernel_layout(wg):
    runs = _natural_runs(0, GATE_OFF) + _natural_runs(GATE_OFF + NH, NIN) + _natural_runs(GATE_OFF, GATE_OFF + NH)
    return jnp.concatenate([wg[d, :, a:b] for d, a, b in runs] + [jnp.zeros((D, NZ - NIN), wg.dtype)], axis=1)


def _kernel_column(n):
    return n if n < GATE_OFF else (F_OFF + n - GATE_OFF if n < GATE_OFF + NH else n - NH)


def _win_device_block(dw, d):
    cuts = sorted({d * INB, (d + 1) * INB} | {c for c in (GATE_OFF, GATE_OFF + NH) if d * INB < c < (d + 1) * INB})
    parts = [dw[:, _kernel_column(a):_kernel_column(a) + (b - a)] for a, b in zip(cuts[:-1], cuts[1:])]
    return parts[0] if len(parts) == 1 else jnp.concatenate(parts, axis=1)


def kernel(x, ffn1_norm, ffn1_w_gu, ffn1_w_down, mix_norm, w_in, b_forget, b_gate, conv_w, sgu_ln_g, sgu_ln_b, sgu_w, sgu_b, q_norm_g, k_norm_g, w_out_conv, w_out_sgu, w_out_attn, w_o, ffn2_norm, ffn2_w_gu, ffn2_w_down, loss_target, m_ffn1_norm, m_ffn1_w_gu, m_ffn1_w_down, m_mix_norm, m_w_in, m_b_forget, m_b_gate, m_conv_w, m_sgu_ln_g, m_sgu_ln_b, m_sgu_w, m_sgu_b, m_q_norm_g, m_k_norm_g, m_w_out_conv, m_w_out_sgu, m_w_out_attn, m_w_o, m_ffn2_norm, m_ffn2_w_gu, m_ffn2_w_down, v_ffn1_norm, v_ffn1_w_gu, v_ffn1_w_down, v_mix_norm, v_w_in, v_b_forget, v_b_gate, v_conv_w, v_sgu_ln_g, v_sgu_ln_b, v_sgu_w, v_sgu_b, v_q_norm_g, v_k_norm_g, v_w_out_conv, v_w_out_sgu, v_w_out_attn, v_w_o, v_ffn2_norm, v_ffn2_w_gu, v_ffn2_w_down):
    w = dict(zip(WEIGHT_NAMES, (ffn1_norm, ffn1_w_gu, ffn1_w_down, mix_norm, w_in, b_forget, b_gate, conv_w, sgu_ln_g,
                                sgu_ln_b, sgu_w, sgu_b, q_norm_g, k_norm_g, w_out_conv, w_out_sgu, w_out_attn, w_o,
                                ffn2_norm, ffn2_w_gu, ffn2_w_down)))
    mom = dict(zip(WEIGHT_NAMES, (m_ffn1_norm, m_ffn1_w_gu, m_ffn1_w_down, m_mix_norm, m_w_in, m_b_forget, m_b_gate,
                                  m_conv_w, m_sgu_ln_g, m_sgu_ln_b, m_sgu_w, m_sgu_b, m_q_norm_g, m_k_norm_g,
                                  m_w_out_conv, m_w_out_sgu, m_w_out_attn, m_w_o, m_ffn2_norm, m_ffn2_w_gu,
                                  m_ffn2_w_down)))
    var = dict(zip(WEIGHT_NAMES, (v_ffn1_norm, v_ffn1_w_gu, v_ffn1_w_down, v_mix_norm, v_w_in, v_b_forget, v_b_gate,
                                  v_conv_w, v_sgu_ln_g, v_sgu_ln_b, v_sgu_w, v_sgu_b, v_q_norm_g, v_k_norm_g,
                                  v_w_out_conv, v_w_out_sgu, v_w_out_attn, v_w_o, v_ffn2_norm, v_ffn2_w_gu,
                                  v_ffn2_w_down)))
    px, py, pc = _place()
    dev = 4 * px + 2 * py + pc
    chip = 2 * px + py

    big_names = [n for n in WEIGHT_NAMES if n in BIG]
    key_name = {BIG[n]: n for n in big_names}
    group_keys = (("gu1", "d1", "small"), ("in", "oc", "os", "oa", "o"), ("gu2", "d2"))

    def source(key, l):
        if key == "small":
            return jnp.concatenate([w["b_gate"][l], w["conv_w"][l], jnp.zeros((2, 128), f32)], axis=0)
        return w[key_name[key]][l].astype(bf16)

    def landing(src):
        return lax.dynamic_update_slice(lax.empty((NDEV,) + src.shape, src.dtype), src[None], (dev, 0, 0))

    groups = [[(s, landing(s)) for s in (source(k, l) for k in keys)] for l in range(2) for keys in group_keys]
    started = _gather_start("gather_start", groups)

    def weights(l, part, after):
        got = _gather_wait(f"gather_wait_{l}_{part}", started[3 * l + part], after)
        return dict(zip(group_keys[part], got))

    xl = x[0]
    saved, small, wts = [], [], []
    for l in range(2):
        ga = weights(l, 0, xl if l else None)
        p = {n: w[n][l] for n in REPLICATED_SMALL}
        p["b_gate"] = jnp.transpose(ga["small"][:, 0:3, :], (1, 0, 2)).reshape(3, D)
        p["conv_w"] = jnp.transpose(ga["small"][:, 3:6, :], (1, 0, 2)).reshape(3, D)
        p = _small_params(p)
        wt = {"gu1": ga["gu1"][None], "d1": ga["d1"].reshape(1, 4, GU, D)}
        x1, s1 = _ffn_fwd("ffn1", xl, p["ffn1_norm"], wt["gu1"], wt["d1"], 0)
        gb = weights(l, 1, x1)
        wt["win"] = _win_kernel_layout(gb["in"])
        wt["sq"] = jnp.stack([gb[k].reshape(D, D) for k in ("oc", "os", "oa", "o")])[None]
        x2, sm = _mixer_fwd("mix", x1, p, wt["win"], wt["sq"], 0)
        gc = weights(l, 2, x2)
        wt.update({"gu2": gc["gu2"][None], "d2": gc["d2"].reshape(1, 4, GU, D)})
        x3, s2 = _ffn_fwd("ffn2", x2, p["ffn2_norm"], wt["gu2"], wt["d2"], 0)
        saved.append((xl, x1, x2, s1, sm, s2))
        small.append(p)
        wts.append(wt)
        xl = x3
    loss_row, dx = _loss("loss", xl, loss_target[0])

    core = pc.reshape(1).astype(jnp.int32)
    sgrads, parts = [None, None], [None, None]
    for l in (1, 0):
        dx, sgrads[l], dwin, gout = _layer_bwd(dx, small[l], wts[l], saved[l])
        gout["in"] = jnp.stack([jnp.stack([_win_device_block(dwin, 2 * q + c)[None] for q in range(4)]) for c in range(2)])
        gs = [gout[k] for k in BIG_KEYS]
        r1 = _rs_pair(f"rs_pair_{l}", gs)
        ss = [_pair_sum(f"pair_sum_{k}_{l}", core, g, r) for k, g, r in zip(BIG_KEYS, gs, r1)]
        r2 = _rs_chips(f"rs_chips_{l}", ss)
        parts[l] = [(lax.dynamic_index_in_dim(s, chip, 0, keepdims=False)[0], r) for s, r in zip(ss, r2)]

    grads, delta, new_m, new_v = {}, {}, {}, {}
    for n in big_names:
        i = BIG_KEYS.index(BIG[n])
        grads[n], delta[n], new_m[n], new_v[n] = _adamw("adamw_" + BIG[n], w[n], mom[n], var[n],
                                                        [parts[0][i], parts[1][i]])

    nat = [_small_grads_natural(sgrads[l]) for l in range(2)]
    order = REPLICATED_SMALL + SHARDED_SMALL
    part = _pack([jnp.stack([nat[0][n], nat[1][n]]) for n in order] + [loss_row[0, 0:1]])
    total = _all_reduce_small("reduce_small", part)
    full_shapes = [(2,) + tuple(nat[0][n].shape) for n in order] + [(1,)]
    summed = dict(zip(order + ("loss",), _unpack(total, full_shapes)))
    for n in REPLICATED_SMALL:
        grads[n] = summed[n]
    for n in SHARDED_SMALL:
        grads[n] = lax.dynamic_slice_in_dim(summed[n], dev * 128, 128, axis=2)
    wp = _pack([w[n] for n in order])
    gp = _pack([grads[n] for n in order])
    mp = _pack([mom[n] for n in order])
    vp = _pack([var[n] for n in order])
    dpk, mpk, vpk = _adamw_small("adamw_small", wp, gp, mp, vp)
    local_shapes = [tuple(w[n].shape) for n in order]
    for dst, packed in ((delta, dpk), (new_m, mpk), (new_v, vpk)):
        dst.update(zip(order, _unpack(packed, local_shapes)))

    loss = summed["loss"][0]
    return (loss, dx[None], *[grads[n] for n in WEIGHT_NAMES], *[delta[n] for n in WEIGHT_NAMES],
            *[new_m[n] for n in WEIGHT_NAMES], *[new_v[n] for n in WEIGHT_NAMES])
```

```python
import functools

import jax
import jax.numpy as jnp
from jax import lax
from jax.experimental import pallas as pl
from jax.experimental.pallas import tpu as pltpu

f32 = jnp.float32
bf16 = jnp.bfloat16
S = jax.ShapeDtypeStruct
MESH = pl.DeviceIdType.MESH

D = 1024
NH = 8
HD = 128
NDEV = 8
GU = 704
NIN = 11272
INB = 1409
GATE_OFF = 8192
F_OFF = 11264
NZ = 11776
RMS_EPS = 1e-6
LN_EPS = 1e-5
ATT_SCALE = HD ** -0.5
NEG = -1e30
INV_SQRT2 = 0.7071067811865476
INV_SQRT2PI = 0.3989422804014327

ADAM_LR = 0.001
ADAM_B1 = 0.9
ADAM_B2 = 0.999
ADAM_EPS = 1e-08
ADAM_WD = 0.01
ADAM_STEP = 10

TT = 256
VMEM_LIMIT = 56 * 1024 * 1024


def _cp(*sem):
    return pltpu.CompilerParams(dimension_semantics=sem, vmem_limit_bytes=VMEM_LIMIT)


def _bs(shape, fn):
    return pl.BlockSpec(shape, fn)


NN = (((1,), (0,)), ((), ()))
NT = (((1,), (1,)), ((), ()))
TN = (((0,), (0,)), ((), ()))


def _mm(name, a, b, *, grid, a_spec, b_spec, out_shape, out_spec, dims, acc_shape, res=None, res_spec=None,
        alpha=1.0, alias=None, split_rows=None):
    nk = grid[2]

    def body(*refs):
        a_ref, b_ref = refs[0], refs[1]
        pos = 2
        res_ref = None
        if res is not None:
            res_ref = refs[pos]
            pos += 1
        if alias is not None:
            pos += 1
        o_ref = refs[pos]
        acc_ref = refs[pos + 1] if nk > 1 else None
        part = lax.dot_general(a_ref[...].astype(bf16), b_ref[...].astype(bf16), dims, preferred_element_type=f32)

        def finish(acc):
            if alpha != 1.0:
                acc = alpha * acc
            if res_ref is not None:
                acc = res_ref[...] + acc
            if split_rows is None:
                o_ref[...] = acc.astype(o_ref.dtype)
            else:
                o_ref[0] = acc[:split_rows].astype(o_ref.dtype)
                o_ref[1] = acc[split_rows:].astype(o_ref.dtype)

        if nk == 1:
            finish(part)
        else:
            k = pl.program_id(2)

            @pl.when(k == 0)
            def _():
                acc_ref[...] = part

            @pl.when(k > 0)
            def _():
                acc_ref[...] += part

            @pl.when(k == nk - 1)
            def _():
                finish(acc_ref[...])

    operands = [a, b]
    in_specs = [a_spec, b_spec]
    if res is not None:
        operands.append(res)
        in_specs.append(res_spec)
    aliases = {}
    if alias is not None:
        aliases = {len(operands): 0}
        operands.append(alias)
        in_specs.append(pl.BlockSpec(memory_space=pl.ANY))
    return pl.pallas_call(
        body, name=name, grid=grid, in_specs=in_specs, out_specs=out_spec, out_shape=out_shape,
        scratch_shapes=[pltpu.VMEM(acc_shape, f32)] if nk > 1 else [],
        input_output_aliases=aliases,
        compiler_params=_cp("parallel", "parallel", "arbitrary"),
    )(*operands)


def _tile(n, t):
    return t if n % t == 0 and n >= t else n


def _row(cb=0, w=D):
    return _bs((TT, w), lambda i: (i, cb))


def _vec(rows=1, w=D):
    return _bs((rows, w), lambda i: (0, 0))


def _acc_store(i, ref, val):
    @pl.when(i == 0)
    def _():
        ref[...] = val

    @pl.when(i > 0)
    def _():
        ref[...] += val


def _rms_fwd(name, x, g):
    T = x.shape[0]

    def body(x_ref, g_ref, o_ref):
        xv = x_ref[...]
        r = lax.rsqrt(jnp.mean(xv * xv, axis=-1, keepdims=True) + RMS_EPS)
        o_ref[...] = (xv * r * g_ref[...]).astype(bf16)

    return pl.pallas_call(body, name=name, grid=(T // TT,), in_specs=[_row(), _vec()], out_specs=_row(),
                          out_shape=S((T, D), bf16), compiler_params=_cp("parallel"))(x, g)


def _rms_bwd(name, dh, x, g, dres):
    T = x.shape[0]

    def body(dh_ref, x_ref, g_ref, dres_ref, dx_ref, dg_ref):
        i = pl.program_id(0)
        xv = x_ref[...]
        r = lax.rsqrt(jnp.mean(xv * xv, axis=-1, keepdims=True) + RMS_EPS)
        xhat = xv * r
        dh_v = dh_ref[...]
        dyg = dh_v * g_ref[...]
        m = jnp.mean(dyg * xhat, axis=-1, keepdims=True)
        dx_ref[...] = dres_ref[...] + r * (dyg - xhat * m)
        _acc_store(i, dg_ref, jnp.sum(dh_v * xhat, axis=0, keepdims=True))

    return pl.pallas_call(body, name=name, grid=(T // TT,), in_specs=[_row(), _row(), _vec(), _row()],
                          out_specs=[_row(), _vec()], out_shape=[S((T, D), f32), S((1, D), f32)],
                          compiler_params=_cp("arbitrary"))(dh, x, g, dres)


def _sigmoid(x):
    return 1.0 / (1.0 + jnp.exp(-x))


def _swiglu_fwd(name, gu):
    T = gu.shape[1]

    def body(g_ref, u_ref, o_ref):
        g = g_ref[...]
        o_ref[...] = (g * _sigmoid(g) * u_ref[...]).astype(bf16)

    return pl.pallas_call(
        body, name=name, grid=(4, T // TT),
        in_specs=[_bs((None, TT, GU), lambda j, i: (j, i, 0)), _bs((None, TT, GU), lambda j, i: (j + 4, i, 0))],
        out_specs=_bs((None, TT, GU), lambda j, i: (j, i, 0)), out_shape=S((4, T, GU), bf16),
        compiler_params=_cp("parallel", "parallel"))(gu, gu)


def _swiglu_bwd(name, gu, da):
    T = gu.shape[1]

    def body(g_ref, u_ref, da_ref, o_ref):
        g = g_ref[...]
        u = u_ref[...]
        da_v = da_ref[...]
        sg = _sigmoid(g)
        o_ref[0] = (da_v * u * (sg + g * sg * (1.0 - sg))).astype(bf16)
        o_ref[1] = (da_v * g * sg).astype(bf16)

    return pl.pallas_call(
        body, name=name, grid=(4, T // TT),
        in_specs=[_bs((None, TT, GU), lambda j, i: (j, i, 0)), _bs((None, TT, GU), lambda j, i: (j + 4, i, 0)),
                  _bs((None, TT, GU), lambda j, i: (j, i, 0))],
        out_specs=_bs((2, None, TT, GU), lambda j, i: (0, j, i, 0)), out_shape=S((2, 4, T, GU), bf16),
        compiler_params=_cp("parallel", "parallel"))(gu, gu, da)


def _loss(name, y, tgt):
    T = y.shape[0]

    def body(y_ref, t_ref, l_ref, dy_ref):
        i = pl.program_id(0)
        e = y_ref[...] - t_ref[...]
        dy_ref[...] = e * (1.0 / D)
        s = 0.5 * jnp.sum(jnp.mean(e * e, axis=-1, keepdims=True))
        _acc_store(i, l_ref, jnp.broadcast_to(s, (1, 128)))

    return pl.pallas_call(body, name=name, grid=(T // TT,), in_specs=[_row(), _row()],
                          out_specs=[_vec(1, 128), _row()], out_shape=[S((1, 128), f32), S((T, D), f32)],
                          compiler_params=_cp("arbitrary"))(y, tgt)


def _prev8(T, cb):
    return _bs((8, D), lambda i: (jnp.maximum(i * (TT // 8) - 1, 0), cb))


def _next8(T, cb):
    return _bs((8, D), lambda i: (jnp.minimum((i + 1) * (TT // 8), T // 8 - 1), cb))


def _conv_taps(i, ac_ref, ax_ref, pc_ref, px_ref):
    ca = ac_ref[...] * ax_ref[...]
    keep = (i > 0).astype(f32)
    p1 = pc_ref[7:8, :] * px_ref[7:8, :] * keep
    p2 = pc_ref[6:7, :] * px_ref[6:7, :] * keep
    row = lax.broadcasted_iota(jnp.int32, ca.shape, 0)
    s1 = jnp.where(row == 0, p1, pltpu.roll(ca, 1, 0))
    s2 = jnp.where(row == 0, p2, jnp.where(row == 1, p1, pltpu.roll(ca, 2, 0)))
    return ca, s1, s2


def _conv_fwd(name, z, cw):
    T = z.shape[0]

    def body(ab_ref, ac_ref, ax_ref, pc_ref, px_ref, w_ref, o_ref):
        i = pl.program_id(0)
        ca, s1, s2 = _conv_taps(i, ac_ref, ax_ref, pc_ref, px_ref)
        cv = w_ref[0:1, :] * s2 + w_ref[1:2, :] * s1 + w_ref[2:3, :] * ca
        o_ref[...] = (ab_ref[...] * cv).astype(bf16)

    return pl.pallas_call(
        body, name=name, grid=(T // TT,),
        in_specs=[_row(0), _row(1), _row(2), _prev8(T, 1), _prev8(T, 2), _vec(3)],
        out_specs=_row(), out_shape=S((T, D), bf16), compiler_params=_cp("parallel"))(z, z, z, z, z, cw)


def _conv_bwd(name, dya, z, cw):
    T = z.shape[0]
    n = T // TT

    def body(dya_ref, ab_ref, ac_ref, ax_ref, pc_ref, px_ref, ndya_ref, nab_ref, w_ref, dz_ref, dw_ref):
        i = pl.program_id(0)
        ca, s1, s2 = _conv_taps(i, ac_ref, ax_ref, pc_ref, px_ref)
        w0, w1, w2 = w_ref[0:1, :], w_ref[1:2, :], w_ref[2:3, :]
        cv = w0 * s2 + w1 * s1 + w2 * ca
        dya_v = dya_ref[...]
        ab = ab_ref[...]
        dcv = dya_v * ab
        keep = (i < n - 1).astype(f32)
        n1 = ndya_ref[0:1, :] * nab_ref[0:1, :] * keep
        n2 = ndya_ref[1:2, :] * nab_ref[1:2, :] * keep
        row = lax.broadcasted_iota(jnp.int32, dcv.shape, 0)
        f1 = jnp.where(row == TT - 1, n1, pltpu.roll(dcv, TT - 1, 0))
        f2 = jnp.where(row == TT - 1, n2, jnp.where(row == TT - 2, n1, pltpu.roll(dcv, TT - 2, 0)))
        dca = w2 * dcv + w1 * f1 + w0 * f2
        dz_ref[:, 0:D] = (dya_v * cv).astype(bf16)
        dz_ref[:, D:2 * D] = (dca * ax_ref[...]).astype(bf16)
        dz_ref[:, 2 * D:3 * D] = (dca * ac_ref[...]).astype(bf16)
        dw = jnp.concatenate([jnp.sum(dcv * s2, axis=0, keepdims=True), jnp.sum(dcv * s1, axis=0, keepdims=True),
                              jnp.sum(dcv * ca, axis=0, keepdims=True)], axis=0)
        _acc_store(i, dw_ref, dw)

    return pl.pallas_call(
        body, name=name, grid=(n,),
        in_specs=[_row(), _row(0), _row(1), _row(2), _prev8(T, 1), _prev8(T, 2), _next8(T, 0), _next8(T, 0), _vec(3)],
        out_specs=[_row(0, 3 * D), _vec(3)], out_shape=[S((T, 3 * D), bf16), S((3, D), f32)],
        compiler_params=_cp("arbitrary"))(dya, z, z, z, z, z, dya, z, cw)


def _gelu(x):
    return 0.5 * x * (1.0 + lax.erf(x * INV_SQRT2))


def _gelu_grad(x):
    return 0.5 * (1.0 + lax.erf(x * INV_SQRT2)) + x * jnp.exp(-0.5 * x * x) * INV_SQRT2PI


def _ln_stats(vv):
    mu = jnp.mean(vv, axis=-1, keepdims=True)
    xc = vv - mu
    rstd = lax.rsqrt(jnp.mean(xc * xc, axis=-1, keepdims=True) + LN_EPS)
    return xc * rstd, rstd


def _tril_w(w_ref, g):
    r = lax.broadcasted_iota(jnp.int32, (HD, HD), 0)
    c = lax.broadcasted_iota(jnp.int32, (HD, HD), 1)
    return jnp.where(c <= r, w_ref[g], 0.0).astype(bf16)


def _sgu_fwd(name, z, ln_g, ln_b, w_s, bmap):
    T = z.shape[0]

    def body(su_ref, sv_ref, lg_ref, lb_ref, w_ref, bm_ref, o_ref, vn_ref):
        xhat, _ = _ln_stats(_gelu(sv_ref[...]))
        vn_ref[...] = (xhat * lg_ref[...] + lb_ref[...]).astype(bf16)
        for g in range(NH):
            w = _tril_w(w_ref, g)
            cs = slice(g * HD, (g + 1) * HD)
            for c in range(TT // HD):
                rs = slice(c * HD, (c + 1) * HD)
                s = jnp.dot(w, vn_ref[rs, cs], preferred_element_type=f32) + bm_ref[:, cs]
                o_ref[rs, cs] = (_gelu(su_ref[rs, cs]) * s).astype(bf16)

    return pl.pallas_call(
        body, name=name, grid=(T // TT,),
        in_specs=[_row(3), _row(4), _vec(), _vec(), _bs((NH, HD, HD), lambda i: (0, 0, 0)), _vec(HD)],
        out_specs=_row(), out_shape=S((T, D), bf16), scratch_shapes=[pltpu.VMEM((TT, D), bf16)],
        compiler_params=_cp("parallel"))(z, z, ln_g, ln_b, w_s, bmap)


def _sgu_bwd(name, dyb, z, ln_g, ln_b, w_s, bmap):
    T = z.shape[0]

    def body(dyb_ref, su_ref, sv_ref, lg_ref, lb_ref, w_ref, bm_ref, dz_ref, dlg_ref, dlb_ref, dw_ref, db_ref,
             vn_ref, du_ref, dvn_ref):
        i = pl.program_id(0)
        sv = sv_ref[...]
        xhat, rstd = _ln_stats(_gelu(sv))
        vn_ref[...] = (xhat * lg_ref[...] + lb_ref[...]).astype(bf16)
        r = lax.broadcasted_iota(jnp.int32, (HD, HD), 0)
        cc = lax.broadcasted_iota(jnp.int32, (HD, HD), 1)
        for g in range(NH):
            w = _tril_w(w_ref, g)
            cs = slice(g * HD, (g + 1) * HD)
            dw = jnp.zeros((HD, HD), f32)
            db = jnp.zeros((HD, 1), f32)
            for c in range(TT // HD):
                rs = slice(c * HD, (c + 1) * HD)
                vnb = vn_ref[rs, cs]
                s = jnp.dot(w, vnb, preferred_element_type=f32) + bm_ref[:, cs]
                dy = dyb_ref[rs, cs]
                du_ref[rs, cs] = dy * s
                ds = dy * _gelu(su_ref[rs, cs])
                ds16 = ds.astype(bf16)
                dvn_ref[rs, cs] = lax.dot_general(w, ds16, TN, preferred_element_type=f32)
                dw = dw + lax.dot_general(ds16, vnb, NT, preferred_element_type=f32)
                db = db + jnp.sum(ds, axis=1, keepdims=True)
            dw = jnp.where(cc <= r, dw, 0.0)

            @pl.when(i == 0)
            def _():
                dw_ref[g] = dw
                db_ref[:, g:g + 1] = db

            @pl.when(i > 0)
            def _():
                dw_ref[g] += dw
                db_ref[:, g:g + 1] += db

        dvn = dvn_ref[...]
        dxh = dvn * lg_ref[...]
        m1 = jnp.mean(dxh, axis=-1, keepdims=True)
        m2 = jnp.mean(dxh * xhat, axis=-1, keepdims=True)
        dvv = rstd * (dxh - m1 - xhat * m2)
        dz_ref[:, 0:D] = (du_ref[...] * _gelu_grad(su_ref[...])).astype(bf16)
        dz_ref[:, D:2 * D] = (dvv * _gelu_grad(sv)).astype(bf16)
        _acc_store(i, dlg_ref, jnp.sum(dvn * xhat, axis=0, keepdims=True))
        _acc_store(i, dlb_ref, jnp.sum(dvn, axis=0, keepdims=True))

    return pl.pallas_call(
        body, name=name, grid=(T // TT,),
        in_specs=[_row(), _row(3), _row(4), _vec(), _vec(), _bs((NH, HD, HD), lambda i: (0, 0, 0)), _vec(HD)],
        out_specs=[_row(0, 2 * D), _vec(), _vec(), _bs((NH, HD, HD), lambda i: (0, 0, 0)), _bs((HD, NH), lambda i: (0, 0))],
        out_shape=[S((T, 2 * D), bf16), S((1, D), f32), S((1, D), f32), S((NH, HD, HD), f32), S((HD, NH), f32)],
        scratch_shapes=[pltpu.VMEM((TT, D), bf16), pltpu.VMEM((TT, D), f32), pltpu.VMEM((TT, D), f32)],
        compiler_params=_cp("arbitrary"))(dyb, z, z, ln_g, ln_b, w_s, bmap)


def _qk_fwd(name, z, qg, kg, bf):
    T = z.shape[0]

    def body(q_ref, k_ref, v_ref, zf_ref, qg_ref, kg_ref, bf_ref, qn_ref, kn_ref, vb_ref, lf_ref):
        for h in range(NH):
            cs = slice(h * HD, (h + 1) * HD)
            for src, gain, dst in ((q_ref, qg_ref, qn_ref), (k_ref, kg_ref, kn_ref)):
                xv = src[:, cs]
                r = lax.rsqrt(jnp.mean(xv * xv, axis=-1, keepdims=True) + RMS_EPS)
                dst[:, cs] = (xv * r * gain[:, cs]).astype(bf16)
        vb_ref[...] = v_ref[...].astype(bf16)
        xf = zf_ref[...] + bf_ref[...]
        lf_ref[...] = jnp.minimum(xf, 0.0) - jnp.log1p(jnp.exp(-jnp.abs(xf)))

    return pl.pallas_call(
        body, name=name, grid=(T // TT,),
        in_specs=[_row(5), _row(6), _row(7), _bs((TT, 128), lambda i: (i, F_OFF // 128)), _vec(), _vec(), _vec(1, 128)],
        out_specs=[_row(), _row(), _row(), _bs((TT, 128), lambda i: (i, 0))],
        out_shape=[S((T, D), bf16), S((T, D), bf16), S((T, D), bf16), S((T, 128), f32)],
        compiler_params=_cp("parallel"))(z, z, z, z, qg, kg, bf)


def _cum_fwd(name, logf):
    T = logf.shape[0]

    def body(lf_ref, ccol_ref, crow_ref, c_ref):
        c = lf_ref[...]
        row = lax.broadcasted_iota(jnp.int32, c.shape, 0)
        s = 1
        while s < T:
            c = c + jnp.where(row >= s, pltpu.roll(c, s, 0), 0.0)
            s *= 2
        c_ref[...] = c
        crow_ref[...] = c.T[0:NH, :]
        for h in range(NH):
            ccol_ref[h] = jnp.broadcast_to(c_ref[:, h:h + 1], (T, 128))

    return pl.pallas_call(body, name=name, out_shape=[S((NH, T, 128), f32), S((NH, T), f32)],
                          scratch_shapes=[pltpu.VMEM((T, 128), f32)],
                          compiler_params=pltpu.CompilerParams(vmem_limit_bytes=VMEM_LIMIT))(logf)


ATT_TILE = 512


def _fold(x, op=jnp.add):
    acc = x[:, 0:128]
    for t in range(1, x.shape[1] // 128):
        acc = op(acc, x[:, t * 128:(t + 1) * 128])
    return acc


def _to_row(col):
    return jnp.broadcast_to(col, (col.shape[0], 128)).T[0:1, :]


def _causal(t, keys_down=False):
    r = lax.broadcasted_iota(jnp.int32, (t, t), 0)
    c = lax.broadcasted_iota(jnp.int32, (t, t), 1)
    return r <= c if keys_down else c <= r


def _attn_fwd(name, qn, kn, vb, ccol, crow3):
    T = qn.shape[0]
    tq = _tile(T, ATT_TILE)
    nq = T // tq

    def body(q_ref, k_ref, v_ref, cc_ref, cr_ref, o_ref, lse_ref, lser_ref, s_ref):
        qi = pl.program_id(1)
        q = q_ref[...]
        cq = cc_ref[:, 0:1]

        def logits(off):
            s = lax.dot_general(q, k_ref[pl.ds(off, tq), :], NT, preferred_element_type=f32) * ATT_SCALE
            return s + cq - cr_ref[:, pl.ds(off, tq)]

        def below(j, mvec):
            off = pl.multiple_of(j * tq, tq)
            s = logits(off)
            s_ref[:, pl.ds(off, tq)] = s
            return jnp.maximum(mvec, _fold(s, jnp.maximum))

        mvec = lax.fori_loop(0, qi, below, jnp.full((tq, 128), NEG, f32))
        off = pl.multiple_of(qi * tq, tq)
        s = jnp.where(_causal(tq), logits(off), NEG)
        s_ref[:, pl.ds(off, tq)] = s
        m = jnp.max(jnp.maximum(mvec, _fold(s, jnp.maximum)), axis=1, keepdims=True)

        def weigh(j, carry):
            lvec, acc = carry
            off = pl.multiple_of(j * tq, tq)
            p = jnp.exp(s_ref[:, pl.ds(off, tq)] - m)
            acc = acc + jnp.dot(p.astype(bf16), v_ref[pl.ds(off, tq), :], preferred_element_type=f32)
            return lvec + _fold(p), acc

        lvec, acc = lax.fori_loop(0, qi + 1, weigh, (jnp.zeros((tq, 128), f32), jnp.zeros((tq, HD), f32)))
        l = jnp.sum(lvec, axis=1, keepdims=True)
        o_ref[...] = acc / l
        lse = m + jnp.log(l)
        lse_ref[...] = jnp.broadcast_to(lse, (tq, 128))
        lser_ref[...] = _to_row(lse)

    return pl.pallas_call(
        body, name=name, grid=(NH, nq),
        in_specs=[_bs((tq, HD), lambda h, i: (i, h)), _bs((T, HD), lambda h, i: (0, h)), _bs((T, HD), lambda h, i: (0, h)),
                  _bs((None, tq, 128), lambda h, i: (h, i, 0)), _bs((None, 1, T), lambda h, i: (h, 0, 0))],
        out_specs=[_bs((tq, HD), lambda h, i: (i, h)), _bs((None, tq, 128), lambda h, i: (h, i, 0)),
                   _bs((None, 1, tq), lambda h, i: (h, 0, i))],
        out_shape=[S((T, D), f32), S((NH, T, 128), f32), S((NH, 1, T), f32)],
        scratch_shapes=[pltpu.VMEM((tq, T), f32)],
        compiler_params=_cp("parallel", "parallel"))(qn, kn, vb, ccol, crow3)


def _attn_dq(name, qn, kn, vb, do, lse, ccol, crow3):
    T = qn.shape[0]
    tq = _tile(T, ATT_TILE)
    nq = T // tq

    def body(q_ref, k_ref, v_ref, do_ref, lse_ref, cc_ref, cr_ref, dq_ref, dlr_ref, p_ref, dp_ref):
        qi = pl.program_id(1)
        q = q_ref[...]
        do16 = do_ref[...].astype(bf16)
        base = cc_ref[:, 0:1] - lse_ref[:, 0:1]

        def probs(off):
            s = lax.dot_general(q, k_ref[pl.ds(off, tq), :], NT, preferred_element_type=f32) * ATT_SCALE
            return jnp.exp(s + base - cr_ref[:, pl.ds(off, tq)])

        def keep(off, p, dvec):
            dp = lax.dot_general(do16, v_ref[pl.ds(off, tq), :], NT, preferred_element_type=f32)
            p_ref[:, pl.ds(off, tq)] = p
            dp_ref[:, pl.ds(off, tq)] = dp
            return dvec + _fold(p * dp)

        def below(j, dvec):
            off = pl.multiple_of(j * tq, tq)
            return keep(off, probs(off), dvec)

        dvec = lax.fori_loop(0, qi, below, jnp.zeros((tq, 128), f32))
        off = pl.multiple_of(qi * tq, tq)
        dvec = keep(off, jnp.where(_causal(tq), probs(off), 0.0), dvec)
        delta = jnp.sum(dvec, axis=1, keepdims=True)

        def grad(j, acc):
            off = pl.multiple_of(j * tq, tq)
            ds = p_ref[:, pl.ds(off, tq)] * (dp_ref[:, pl.ds(off, tq)] - delta)
            return acc + jnp.dot(ds.astype(bf16), k_ref[pl.ds(off, tq), :], preferred_element_type=f32)

        dq_ref[...] = lax.fori_loop(0, qi + 1, grad, jnp.zeros((tq, HD), f32)) * ATT_SCALE
        dlr_ref[...] = _to_row(delta)

    qb = lambda h, i: (i, h)
    full = lambda h, i: (0, h)
    col = lambda h, i: (h, i, 0)
    return pl.pallas_call(
        body, name=name, grid=(NH, nq),
        in_specs=[_bs((tq, HD), qb), _bs((T, HD), full), _bs((T, HD), full), _bs((tq, HD), qb),
                  _bs((None, tq, 128), col), _bs((None, tq, 128), col), _bs((None, 1, T), lambda h, i: (h, 0, 0))],
        out_specs=[_bs((tq, HD), qb), _bs((None, 1, tq), lambda h, i: (h, 0, i))],
        out_shape=[S((T, D), f32), S((NH, 1, T), f32)],
        scratch_shapes=[pltpu.VMEM((tq, T), f32), pltpu.VMEM((tq, T), f32)],
        compiler_params=_cp("parallel", "parallel"))(qn, kn, vb, do, lse, ccol, crow3)


def _attn_dkv(name, qn, kn, vb, do, lser3, dlr3, ccol, crow3):
    T = qn.shape[0]
    tk = _tile(T, ATT_TILE)
    nk = T // tk

    def body(q_ref, k_ref, v_ref, do_ref, lser_ref, dlr_ref, cc_ref, cr_ref, dk_ref, dv_ref, cs_ref):
        h = pl.program_id(0)
        kj = pl.program_id(1)

        @pl.when((h == 0) & (kj == 0))
        def _():
            cs_ref[...] = jnp.zeros_like(cs_ref)

        kb = k_ref[...]
        vv = v_ref[...]
        ckey = cc_ref[:, 0:1]

        def block(off, diagonal):
            rows = pl.ds(off, tk)
            qb = q_ref[rows, :]
            do16 = do_ref[rows, :].astype(bf16)
            st = lax.dot_general(kb, qb, NT, preferred_element_type=f32) * ATT_SCALE
            pt = jnp.exp(st + (cr_ref[:, rows] - lser_ref[:, rows]) - ckey)
            if diagonal:
                pt = jnp.where(_causal(tk, keys_down=True), pt, 0.0)
            dpt = lax.dot_general(vv, do16, NT, preferred_element_type=f32)
            dst = pt * (dpt - dlr_ref[:, rows])
            ddv = jnp.dot(pt.astype(bf16), do16, preferred_element_type=f32)
            ddk = jnp.dot(dst.astype(bf16), qb, preferred_element_type=f32)
            return ddk, ddv, _fold(dst)

        def above(i, carry):
            ddk, ddv, dcs = block(pl.multiple_of(i * tk, tk), False)
            return carry[0] + ddk, carry[1] + ddv, carry[2] + dcs

        off = pl.multiple_of(kj * tk, tk)
        dk, dv, cs = lax.fori_loop(kj + 1, nk, above, block(off, True))
        dk_ref[...] = dk * ATT_SCALE
        dv_ref[...] = dv
        lane = lax.broadcasted_iota(jnp.int32, (tk, 128), 1)
        cs_ref[pl.ds(off, tk), :] += jnp.where(lane == h, jnp.sum(cs, axis=1, keepdims=True), 0.0)

    full = lambda h, j: (0, h)
    blk = lambda h, j: (j, h)
    row = lambda h, j: (h, 0, 0)
    return pl.pallas_call(
        body, name=name, grid=(NH, nk),
        in_specs=[_bs((T, HD), full), _bs((tk, HD), blk), _bs((tk, HD), blk), _bs((T, HD), full), _bs((None, 1, T), row),
                  _bs((None, 1, T), row), _bs((None, tk, 128), lambda h, j: (h, j, 0)), _bs((None, 1, T), row)],
        out_specs=[_bs((tk, HD), blk), _bs((tk, HD), blk), _bs((T, 128), lambda h, j: (0, 0))],
        out_shape=[S((T, D), f32), S((T, D), f32), S((T, 128), f32)],
        compiler_params=_cp("arbitrary", "arbitrary"))(qn, kn, vb, do, lser3, dlr3, ccol, crow3)


def _forget_bwd(name, cs, z, bf):
    T = cs.shape[0]

    def body(cs_ref, zf_ref, bf_ref, dz_ref, db_ref):
        c = -cs_ref[...]
        row = lax.broadcasted_iota(jnp.int32, c.shape, 0)
        s = 1
        while s < T:
            c = c + jnp.where(row + s < T, pltpu.roll(c, T - s, 0), 0.0)
            s *= 2
        xf = zf_ref[...] + bf_ref[...]
        lane = lax.broadcasted_iota(jnp.int32, c.shape, 1)
        dxf = jnp.where(lane < NH, c / (1.0 + jnp.exp(xf)), 0.0)
        dz_ref[...] = jnp.zeros_like(dz_ref)
        dz_ref[:, 0:128] = dxf.astype(bf16)
        db_ref[...] = jnp.sum(dxf, axis=0, keepdims=True)

    return pl.pallas_call(
        body, name=name, grid=(1,),
        in_specs=[_bs((T, 128), lambda i: (0, 0)), _bs((T, 128), lambda i: (0, F_OFF // 128)), _vec(1, 128)],
        out_specs=[_bs((T, NZ - F_OFF), lambda i: (0, 0)), _vec(1, 128)],
        out_shape=[S((T, NZ - F_OFF), bf16), S((1, 128), f32)], compiler_params=_cp("arbitrary"))(cs, z, bf)


def _qk_bwd(name, dqn, dkn, dv, z, qg, kg):
    T = z.shape[0]

    def body(dq_ref, dk_ref, dv_ref, q_ref, k_ref, qg_ref, kg_ref, dz_ref, dqg_ref, dkg_ref, g_ref):
        i = pl.program_id(0)
        for n, (src, dsrc, gain, dgain) in enumerate(((q_ref, dq_ref, qg_ref, dqg_ref), (k_ref, dk_ref, kg_ref, dkg_ref))):
            for h in range(NH):
                cs = slice(h * HD, (h + 1) * HD)
                xv = src[:, cs]
                r = lax.rsqrt(jnp.mean(xv * xv, axis=-1, keepdims=True) + RMS_EPS)
                xhat = xv * r
                dy = dsrc[:, cs]
                dyg = dy * gain[:, cs]
                m = jnp.mean(dyg * xhat, axis=-1, keepdims=True)
                dz_ref[:, n * D + h * HD:n * D + (h + 1) * HD] = (r * (dyg - xhat * m)).astype(bf16)
                g_ref[:, cs] = jnp.sum(dy * xhat, axis=0, keepdims=True)
            _acc_store(i, dgain, g_ref[...])
        dz_ref[:, 2 * D:3 * D] = dv_ref[...].astype(bf16)

    return pl.pallas_call(
        body, name=name, grid=(T // TT,),
        in_specs=[_row(), _row(), _row(), _row(5), _row(6), _vec(), _vec()],
        out_specs=[_row(0, 3 * D), _vec(), _vec()], out_shape=[S((T, 3 * D), bf16), S((1, D), f32), S((1, D), f32)],
        scratch_shapes=[pltpu.VMEM((1, D), f32)], compiler_params=_cp("arbitrary"))(dqn, dkn, dv, z, z, qg, kg)


GB = GATE_OFF // D


def _merge_fwd(name, ya, yb, yc, z, bg):
    T = z.shape[0]

    def body(ya_ref, yb_ref, yc_ref, g0_ref, g1_ref, g2_ref, bg_ref, o_ref):
        acc = _sigmoid(g0_ref[...] + bg_ref[0:1, :]) * ya_ref[...]
        acc = acc + _sigmoid(g1_ref[...] + bg_ref[1:2, :]) * yb_ref[...]
        acc = acc + _sigmoid(g2_ref[...] + bg_ref[2:3, :]) * yc_ref[...]
        o_ref[...] = acc.astype(bf16)

    return pl.pallas_call(
        body, name=name, grid=(T // TT,),
        in_specs=[_row(), _row(), _row(), _row(GB), _row(GB + 1), _row(GB + 2), _vec(3)],
        out_specs=_row(), out_shape=S((T, D), bf16), compiler_params=_cp("parallel"))(ya, yb, yc, z, z, z, bg)


def _merge_bwd(name, dm, ya, yb, yc, z, bg):
    T = z.shape[0]

    def body(dm_ref, ya_ref, yb_ref, yc_ref, g0_ref, g1_ref, g2_ref, bg_ref, dya_ref, dyb_ref, dyc_ref, dz_ref, db_ref):
        i = pl.program_id(0)
        dm_v = dm_ref[...]
        dbs = []
        for n, (y_ref, g_ref, dy_ref) in enumerate(((ya_ref, g0_ref, dya_ref), (yb_ref, g1_ref, dyb_ref),
                                                    (yc_ref, g2_ref, dyc_ref))):
            gate = _sigmoid(g_ref[...] + bg_ref[n:n + 1, :])
            dy_ref[...] = (dm_v * gate).astype(bf16)
            dl = dm_v * y_ref[...] * gate * (1.0 - gate)
            dz_ref[:, n * D:(n + 1) * D] = dl.astype(bf16)
            dbs.append(jnp.sum(dl, axis=0, keepdims=True))
        _acc_store(i, db_ref, jnp.concatenate(dbs, axis=0))

    return pl.pallas_call(
        body, name=name, grid=(T // TT,),
        in_specs=[_row(), _row(), _row(), _row(), _row(GB), _row(GB + 1), _row(GB + 2), _vec(3)],
        out_specs=[_row(), _row(), _row(), _row(0, 3 * D), _vec(3)],
        out_shape=[S((T, D), bf16), S((T, D), bf16), S((T, D), bf16), S((T, 3 * D), bf16), S((3, D), f32)],
        compiler_params=_cp("arbitrary"))(dm, ya, yb, yc, z, z, z, bg)


SMALL_NAMES = ("ffn1_norm", "mix_norm", "b_forget", "b_gate", "conv_w", "sgu_ln_g", "sgu_ln_b", "sgu_w", "sgu_b",
               "q_norm_g", "k_norm_g", "ffn2_norm")


def _small_params(p):
    out = {n: p[n].reshape(1, D) for n in ("ffn1_norm", "mix_norm", "ffn2_norm", "sgu_ln_g", "sgu_ln_b", "q_norm_g", "k_norm_g")}
    out["b_forget"] = jnp.pad(p["b_forget"].reshape(1, NH), ((0, 0), (0, 128 - NH)))
    out["b_gate"] = p["b_gate"]
    out["conv_w"] = p["conv_w"]
    out["sgu_w"] = p["sgu_w"]
    out["bmap"] = jnp.repeat(p["sgu_b"].T, HD, axis=1)
    return out


def _small_grads_natural(sg):
    out = {n: sg[n].reshape(D) for n in ("ffn1_norm", "mix_norm", "ffn2_norm", "sgu_ln_g", "sgu_ln_b")}
    out["q_norm_g"] = sg["q_norm_g"].reshape(NH, HD)
    out["k_norm_g"] = sg["k_norm_g"].reshape(NH, HD)
    out["b_forget"] = sg["b_forget"][0, :NH]
    out["b_gate"] = sg["b_gate"]
    out["conv_w"] = sg["conv_w"]
    out["sgu_w"] = sg["sgu_w"]
    out["sgu_b"] = sg["sgu_b"]
    return out


def _sq_fwd(name, a, wsq, l, n, res=None):
    T = a.shape[0]
    tm = _tile(T, 512)
    return _mm(name, a, wsq, grid=(T // tm, 1, 1), a_spec=_bs((tm, D), lambda i, j, k: (i, 0)),
               b_spec=_bs((None, None, D, D), lambda i, j, k: (l, n, 0, 0)),
               out_shape=S((T, D), f32), out_spec=_bs((tm, D), lambda i, j, k: (i, 0)), dims=NN, acc_shape=None,
               res=res, res_spec=_bs((tm, D), lambda i, j, k: (i, 0)))


def _sq_bwd_in(name, dy, wsq, l, n):
    T = dy.shape[0]
    tm = _tile(T, 512)
    return _mm(name, dy, wsq, grid=(T // tm, 1, 1), a_spec=_bs((tm, D), lambda i, j, k: (i, 0)),
               b_spec=_bs((None, None, D, D), lambda i, j, k: (l, n, 0, 0)),
               out_shape=S((T, D), f32), out_spec=_bs((tm, D), lambda i, j, k: (i, 0)), dims=NT, acc_shape=None)


def _sq_bwd_w(name, a, dy, gbuf, l):
    T = a.shape[0]
    return _mm(name, a, dy, grid=(NDEV, 1, 1), a_spec=_bs((T, 128), lambda i, j, k: (0, i)),
               b_spec=_bs((T, D), lambda i, j, k: (0, 0)), out_shape=S(gbuf.shape, bf16),
               out_spec=_bs((None, None, None, 128, D), lambda i, j, k: (i % 2, i // 2, l, 0, 0)),
               dims=TN, acc_shape=None, alias=gbuf)


def _ffn_fwd(tag, x, g, wgu, wd, l):
    T = x.shape[0]
    tm = _tile(T, 1024)
    h = _rms_fwd(tag + "_rms", x, g)
    gu = _mm(tag + "_gu", h, wgu, grid=(T // tm, NDEV, 1), a_spec=_bs((tm, D), lambda i, j, k: (i, 0)),
             b_spec=_bs((None, None, D, GU), lambda i, j, k: (l, j, 0, 0)), out_shape=S((NDEV, T, GU), f32),
             out_spec=_bs((None, tm, GU), lambda i, j, k: (j, i, 0)), dims=NN, acc_shape=None)
    a = _swiglu_fwd(tag + "_act", gu)
    tm = _tile(T, 1024)
    xo = _mm(tag + "_down", a, wd, grid=(T // tm, 1, 4), a_spec=_bs((None, tm, GU), lambda i, j, k: (k, i, 0)),
             b_spec=_bs((None, None, GU, D), lambda i, j, k: (l, k, 0, 0)), out_shape=S((T, D), f32),
             out_spec=_bs((tm, D), lambda i, j, k: (i, 0)), dims=NN, acc_shape=(tm, D), res=x,
             res_spec=_bs((tm, D), lambda i, j, k: (i, 0)), alpha=0.5)
    return xo, (h, gu, a)


def _ffn_bwd(tag, dxo, x, g, wgu, wd, l, saved, g_gu, g_d):
    h, gu, a = saved
    T = x.shape[0]
    tm = _tile(T, 512)
    da = _mm(tag + "_dact", dxo, wd, grid=(T // tm, 4, 1), a_spec=_bs((tm, D), lambda i, j, k: (i, 0)),
             b_spec=_bs((None, None, GU, D), lambda i, j, k: (l, j, 0, 0)), out_shape=S((4, T, GU), f32),
             out_spec=_bs((None, tm, GU), lambda i, j, k: (j, i, 0)), dims=NT, acc_shape=None, alpha=0.5)
    g_d = _mm(tag + "_dwd", a, dxo, grid=(4, 1, 1), a_spec=_bs((None, T, GU), lambda i, j, k: (i, 0, 0)),
              b_spec=_bs((T, D), lambda i, j, k: (0, 0)), out_shape=S(g_d.shape, bf16),
              out_spec=_bs((2, None, None, GU // 2, D), lambda i, j, k: (0, i, l, 0, 0)), dims=TN, acc_shape=None,
              alpha=0.5, alias=g_d, split_rows=GU // 2)
    dgu = _swiglu_bwd(tag + "_dgu", gu, da).reshape(NDEV, T, GU)
    dh = _mm(tag + "_dh", dgu, wgu, grid=(1, 1, NDEV), a_spec=_bs((None, T, GU), lambda i, j, k: (k, 0, 0)),
             b_spec=_bs((None, None, D, GU), lambda i, j, k: (l, k, 0, 0)), out_shape=S((T, D), f32),
             out_spec=_bs((T, D), lambda i, j, k: (0, 0)), dims=NT, acc_shape=(T, D))
    g_gu = _mm(tag + "_dwgu", h, dgu, grid=(1, NDEV, 1), a_spec=_bs((T, D), lambda i, j, k: (0, 0)),
               b_spec=_bs((None, T, GU), lambda i, j, k: (j, 0, 0)), out_shape=S(g_gu.shape, bf16),
               out_spec=_bs((None, None, None, D, GU), lambda i, j, k: (j % 2, j // 2, l, 0, 0)), dims=TN,
               acc_shape=None, alias=g_gu)
    dx, dg = _rms_bwd(tag + "_drms", dh, x, g, dxo)
    return dx, dg, g_gu, g_d


def _mixer_fwd(tag, x, p, win, wsq, l):
    T = x.shape[0]
    h = _rms_fwd(tag + "_rms", x, p["mix_norm"])
    tn = 512
    z = _mm(tag + "_in", h, win, grid=(1, NZ // tn, 1), a_spec=_bs((T, D), lambda i, j, k: (0, 0)),
            b_spec=_bs((D, tn), lambda i, j, k: (0, j)), out_shape=S((T, NZ), f32),
            out_spec=_bs((T, tn), lambda i, j, k: (0, j)), dims=NN, acc_shape=None)
    ya_in = _conv_fwd(tag + "_conv", z, p["conv_w"])
    yb_in = _sgu_fwd(tag + "_sgu", z, p["sgu_ln_g"], p["sgu_ln_b"], p["sgu_w"], p["bmap"])
    qn, kn, vb, logf = _qk_fwd(tag + "_qk", z, p["q_norm_g"], p["k_norm_g"], p["b_forget"])
    ccol, crow = _cum_fwd(tag + "_cum", logf)
    crow3 = crow.reshape(NH, 1, T)
    o, lse, lser = _attn_fwd(tag + "_attn", qn, kn, vb, ccol, crow3)
    ya = _sq_fwd(tag + "_oconv", ya_in, wsq, l, 0)
    yb = _sq_fwd(tag + "_osgu", yb_in, wsq, l, 1)
    yc = _sq_fwd(tag + "_oattn", o, wsq, l, 2)
    merged = _merge_fwd(tag + "_merge", ya, yb, yc, z, p["b_gate"])
    xo = _sq_fwd(tag + "_o", merged, wsq, l, 3, res=x)
    return xo, (h, z, ya_in, yb_in, qn, kn, vb, ccol, crow3, o, lse, lser, ya, yb, yc, merged)


def _mixer_bwd(tag, dxo, x, p, win, wsq, l, saved, gsq):
    h, z, ya_in, yb_in, qn, kn, vb, ccol, crow3, o, lse, lser, ya, yb, yc, merged = saved
    T = x.shape[0]
    sg = {}
    dm = _sq_bwd_in(tag + "_dmerged", dxo, wsq, l, 3)
    gsq[3] = _sq_bwd_w(tag + "_dwo", merged, dxo, gsq[3], l)
    dya, dyb, dyc, dz_g, sg["b_gate"] = _merge_bwd(tag + "_dmerge", dm, ya, yb, yc, z, p["b_gate"])
    d_ya_in = _sq_bwd_in(tag + "_dconv_in", dya, wsq, l, 0)
    gsq[0] = _sq_bwd_w(tag + "_dwoc", ya_in, dya, gsq[0], l)
    d_yb_in = _sq_bwd_in(tag + "_dsgu_in", dyb, wsq, l, 1)
    gsq[1] = _sq_bwd_w(tag + "_dwos", yb_in, dyb, gsq[1], l)
    d_o = _sq_bwd_in(tag + "_dattn_in", dyc, wsq, l, 2)
    gsq[2] = _sq_bwd_w(tag + "_dwoa", o, dyc, gsq[2], l)
    dz_c, sg["conv_w"] = _conv_bwd(tag + "_dconv", d_ya_in, z, p["conv_w"])
    dz_s, sg["sgu_ln_g"], sg["sgu_ln_b"], sg["sgu_w"], db_t = _sgu_bwd(
        tag + "_dsgu", d_yb_in, z, p["sgu_ln_g"], p["sgu_ln_b"], p["sgu_w"], p["bmap"])
    sg["sgu_b"] = db_t.T
    dqn, dlr = _attn_dq(tag + "_dattn_q", qn, kn, vb, d_o, lse, ccol, crow3)
    dkn, dv, cs = _attn_dkv(tag + "_dattn_kv", qn, kn, vb, d_o, lser, dlr, ccol, crow3)
    dz_f, sg["b_forget"] = _forget_bwd(tag + "_dforget", cs, z, p["b_forget"])
    dz_q, sg["q_norm_g"], sg["k_norm_g"] = _qk_bwd(tag + "_dqk", dqn, dkn, dv, z, p["q_norm_g"], p["k_norm_g"])
    dz = jnp.concatenate([dz_c, dz_s, dz_q, dz_g, dz_f], axis=1)
    tk = 512
    dh = _mm(tag + "_dh", dz, win, grid=(1, 1, NZ // tk), a_spec=_bs((T, tk), lambda i, j, k: (0, k)),
             b_spec=_bs((D, tk), lambda i, j, k: (0, k)), out_shape=S((T, D), f32),
             out_spec=_bs((T, D), lambda i, j, k: (0, 0)), dims=NT, acc_shape=(T, D))
    tn = 512
    dwin = _mm(tag + "_dwin", h, dz, grid=(1, NZ // tn, 1), a_spec=_bs((T, D), lambda i, j, k: (0, 0)),
               b_spec=_bs((T, tn), lambda i, j, k: (0, j)), out_shape=S((D, NZ), bf16),
               out_spec=_bs((D, tn), lambda i, j, k: (0, j)), dims=TN, acc_shape=None)
    dx, sg["mix_norm"] = _rms_bwd(tag + "_drms", dh, x, p["mix_norm"], dxo)
    return dx, sg, dwin, gsq


def _layer_bwd(dx, p, wt, saved):
    x0, x1, x2, s1, sm, s2 = saved
    buf = lambda r, c: lax.empty((2, 4, 1, r, c), bf16)
    dx, dn2, g_gu2, g_d2 = _ffn_bwd("ffn2", dx, x2, p["ffn2_norm"], wt["gu2"], wt["d2"], 0, s2, buf(D, GU), buf(GU // 2, D))
    dx, sg, dwin, gsq = _mixer_bwd("mix", dx, x1, p, wt["win"], wt["sq"], 0, sm, [buf(128, D) for _ in range(4)])
    dx, dn1, g_gu1, g_d1 = _ffn_bwd("ffn1", dx, x0, p["ffn1_norm"], wt["gu1"], wt["d1"], 0, s1, buf(D, GU), buf(GU // 2, D))
    sg["ffn1_norm"] = dn1
    sg["ffn2_norm"] = dn2
    gout = {"gu1": g_gu1, "d1": g_d1, "gu2": g_gu2, "d2": g_d2, "oc": gsq[0], "os": gsq[1], "oa": gsq[2], "o": gsq[3]}
    return dx, sg, dwin, gout


ANY = pl.BlockSpec(memory_space=pl.ANY)
HBM = pl.BlockSpec(memory_space=pltpu.HBM)
SEM = pl.BlockSpec(memory_space=pltpu.SEMAPHORE)
EFFECT = pltpu.SideEffectType.DATAFLOW_SIDE_EFFECTING


def _place():
    return lax.axis_index("x"), lax.axis_index("y"), lax.axis_index("c")


NEAR = 4


def _others(x, y, c):
    return [(x, y, 1 - c), (1 - x, y, c), (x, 1 - y, c), (1 - x, 1 - y, c)]


def _gather_start(name, groups):
    sizes = [len(g) for g in groups]
    srcs = [s for g in groups for s, _ in g]
    lands = [b for g in groups for _, b in g]
    n, ng = len(srcs), len(groups)

    def body(*refs):
        src_refs, land_refs = refs[:n], refs[n:2 * n]
        send, recv = refs[2 * n:2 * n + ng], refs[2 * n + ng:2 * n + 2 * ng]
        x, y, c = _place()
        me = 4 * x + 2 * y + c
        u = 0
        for g, size in enumerate(sizes):
            for i in range(size):
                for k, peer in enumerate(_others(x, y, c)):
                    pltpu.make_async_remote_copy(src_ref=src_refs[u], dst_ref=land_refs[u].at[me],
                                                 send_sem=send[g].at[i * NEAR + k], recv_sem=recv[g].at[i * NEAR + k],
                                                 device_id=peer, device_id_type=MESH).start()
                u += 1

    sems = [pltpu.SemaphoreType.DMA((size * NEAR,)) for size in sizes]
    out = pl.pallas_call(
        body, name=name, in_specs=[HBM] * (2 * n), out_specs=[SEM] * (2 * ng) + [HBM] * (2 * n),
        out_shape=sems + sems + [pltpu.HBM(a.shape, a.dtype) for a in srcs + lands],
        input_output_aliases={i: 2 * ng + i for i in range(2 * n)},
        compiler_params=pltpu.CompilerParams(has_side_effects=EFFECT),
    )(*[pltpu.with_memory_space_constraint(a, pltpu.HBM) for a in srcs + lands])
    res, pos = [], 0
    for g, size in enumerate(sizes):
        res.append((out[g], out[ng + g], out[2 * ng + pos:2 * ng + pos + size], out[2 * ng + n + pos:2 * ng + n + pos + size]))
        pos += size
    return res


def _gather_wait(name, started, after=None):
    send, recv, srcs, lands = started
    n = len(srcs)

    def body(*refs):
        src_refs, land_refs = refs[:n], refs[n:2 * n]
        send_ref, recv_ref = refs[2 * n], refs[2 * n + 1]
        x, y, c = _place()
        for i in range(n):
            for k, (px, py, pc) in enumerate(_others(x, y, c)):
                cp = pltpu.make_async_remote_copy(src_ref=src_refs[i], dst_ref=land_refs[i].at[4 * px + 2 * py + pc],
                                                  send_sem=send_ref.at[i * NEAR + k], recv_sem=recv_ref.at[i * NEAR + k],
                                                  device_id=(px, py, pc), device_id_type=MESH)
                cp.wait_send()
                cp.wait_recv()

    extra = [] if after is None else [after]
    out = pl.pallas_call(
        body, name=name, in_specs=[HBM] * (2 * n) + [SEM, SEM] + [ANY] * len(extra), out_specs=[HBM] * (2 * n),
        out_shape=[pltpu.HBM(a.shape, a.dtype) for a in list(srcs) + list(lands)],
        input_output_aliases={i: i for i in range(2 * n)},
        compiler_params=pltpu.CompilerParams(has_side_effects=EFFECT),
    )(*srcs, *lands, send, recv, *extra)
    return out[n:]


def _gather_forward(name, lands):
    n = len(lands)

    def body(*refs):
        have, full = refs[:n], refs[n:2 * n]
        send, recv = refs[2 * n], refs[2 * n + 1]
        x, y, c = _place()
        chips = [(1 - x, y), (x, 1 - y), (1 - x, 1 - y)]

        def copy(i, j, core):
            slot = 4 * chips[j][0] + 2 * chips[j][1] + core
            return pltpu.make_async_remote_copy(src_ref=have[i].at[slot], dst_ref=full[i].at[slot],
                                                send_sem=send.at[i * 3 + j], recv_sem=recv.at[i * 3 + j],
                                                device_id=(x, y, 1 - c), device_id_type=MESH)

        for i in range(n):
            for j in range(3):
                copy(i, j, c).start()
        for i in range(n):
            for j in range(3):
                copy(i, j, c).wait_send()
                copy(i, j, 1 - c).wait_recv()

    return pl.pallas_call(
        body, name=name, in_specs=[ANY] * n, out_specs=[ANY] * n, out_shape=[S(a.shape, a.dtype) for a in lands],
        input_output_aliases={i: i for i in range(n)},
        scratch_shapes=[pltpu.SemaphoreType.DMA((n * 3,)), pltpu.SemaphoreType.DMA((n * 3,))],
    )(*lands)


def _rs_pair(name, gs):
    n = len(gs)

    def body(*refs):
        ins, outs = refs[:n], refs[n:2 * n]
        send, recv = refs[2 * n:]
        x, y, c = _place()
        cps = [pltpu.make_async_remote_copy(src_ref=ins[u].at[1 - c], dst_ref=outs[u], send_sem=send.at[u],
                                            recv_sem=recv.at[u], device_id=(x, y, 1 - c), device_id_type=MESH)
               for u in range(n)]
        for cp in cps:
            cp.start()
        for cp in cps:
            cp.wait()

    return pl.pallas_call(
        body, name=name, in_specs=[ANY] * n, out_specs=[ANY] * n, out_shape=[S(g.shape[1:], g.dtype) for g in gs],
        scratch_shapes=[pltpu.SemaphoreType.DMA((n,)), pltpu.SemaphoreType.DMA((n,))],
    )(*gs)


def _row_tile(r, c):
    return 128 if (r % 128 == 0 and c > D) else (256 if r % 256 == 0 else r)


def _pair_sum(name, core, g, r1):
    _, nq, nl, r, c = g.shape
    tr = _row_tile(r, c)
    g4 = g.reshape(2, nq * nl, r, c)
    r3 = r1.reshape(nq * nl, r, c)

    def body(core_ref, g_ref, r_ref, o_ref):
        o_ref[...] = (g_ref[...].astype(f32) + r_ref[...].astype(f32)).astype(bf16)

    out = pl.pallas_call(
        body, name=name,
        grid_spec=pltpu.PrefetchScalarGridSpec(
            num_scalar_prefetch=1, grid=(nq * nl, r // tr),
            in_specs=[_bs((None, None, tr, c), lambda b, i, cr: (cr[0], b, i, 0)), _bs((None, tr, c), lambda b, i, cr: (b, i, 0))],
            out_specs=_bs((None, tr, c), lambda b, i, cr: (b, i, 0))),
        out_shape=S((nq * nl, r, c), bf16), compiler_params=_cp("parallel", "parallel"))(core, g4, r3)
    return out.reshape(nq, nl, r, c)


def _rs_chips(name, ss):
    n = len(ss)

    def body(*refs):
        cps = _scatter_copies(refs[:n], refs[n:2 * n], refs[2 * n], refs[2 * n + 1])
        for cp in cps:
            cp.start()
        for cp in cps:
            cp.wait()

    return pl.pallas_call(
        body, name=name, in_specs=[ANY] * n, out_specs=[ANY] * n, out_shape=[S((3,) + s.shape[1:], s.dtype) for s in ss],
        scratch_shapes=[pltpu.SemaphoreType.DMA((n * 3,)), pltpu.SemaphoreType.DMA((n * 3,))],
    )(*ss)


def _scatter_copies(ins, outs, send, recv):
    x, y, c = _place()
    chips = [(1 - x, y), (x, 1 - y), (1 - x, 1 - y)]
    return [pltpu.make_async_remote_copy(src_ref=ins[u].at[2 * chip[0] + chip[1]], dst_ref=outs[u].at[k],
                                         send_sem=send.at[u * 3 + k], recv_sem=recv.at[u * 3 + k],
                                         device_id=(*chip, c), device_id_type=MESH)
            for u in range(len(ins)) for k, chip in enumerate(chips)]


def _scatter_start(name, ss):
    n = len(ss)
    lands = [lax.empty((3,) + s.shape[1:], s.dtype) for s in ss]

    def body(*refs):
        for cp in _scatter_copies(refs[:n], refs[n:2 * n], refs[2 * n], refs[2 * n + 1]):
            cp.start()

    sems = [pltpu.SemaphoreType.DMA((n * 3,))] * 2
    out = pl.pallas_call(
        body, name=name, in_specs=[HBM] * (2 * n), out_specs=[SEM, SEM] + [HBM] * (2 * n),
        out_shape=sems + [pltpu.HBM(a.shape, a.dtype) for a in list(ss) + lands],
        input_output_aliases={i: 2 + i for i in range(2 * n)},
        compiler_params=pltpu.CompilerParams(has_side_effects=EFFECT),
    )(*[pltpu.with_memory_space_constraint(a, pltpu.HBM) for a in list(ss) + lands])
    return out[0], out[1], out[2:2 + n], out[2 + n:]


def _scatter_wait(name, started, after):
    send, recv, srcs, lands = started
    n = len(srcs)

    def body(*refs):
        for cp in _scatter_copies(refs[:n], refs[n:2 * n], refs[2 * n], refs[2 * n + 1]):
            cp.wait_send()
            cp.wait_recv()

    out = pl.pallas_call(
        body, name=name, in_specs=[HBM] * (2 * n) + [SEM, SEM, ANY], out_specs=[HBM] * (2 * n),
        out_shape=[pltpu.HBM(a.shape, a.dtype) for a in list(srcs) + list(lands)],
        input_output_aliases={i: i for i in range(2 * n)},
        compiler_params=pltpu.CompilerParams(has_side_effects=EFFECT),
    )(*srcs, *lands, send, recv, after)
    return out[:n], out[n:]


def _all_reduce_small(name, v):
    r = v.shape[0]

    def body(v_ref, o_ref, buf, send, recv, lsem):
        x, y, c = _place()
        me, sib = (x, y, c), (x, y, 1 - c)
        chips = [(1 - x, y), (x, 1 - y), (1 - x, 1 - y)]

        def slot(px, py, pc):
            return buf.at[4 * px + 2 * py + pc]

        def copy(k, block, to, src=None):
            return pltpu.make_async_remote_copy(src_ref=slot(*block) if src is None else src, dst_ref=slot(*block),
                                                send_sem=send.at[k], recv_sem=recv.at[k], device_id=to,
                                                device_id_type=MESH)

        mine = pltpu.make_async_copy(v_ref, slot(*me), lsem)
        mine.start()
        first = [copy(0, me, sib, v_ref)] + [copy(1 + j, me, (*chip, c), v_ref) for j, chip in enumerate(chips)]
        for cp in first:
            cp.start()
        passed = [copy(4 + j, (*chip, c), sib) for j, chip in enumerate(chips)]
        for j, chip in enumerate(chips):
            copy(1 + j, (*chip, c), me).wait_recv()
            passed[j].start()
        copy(0, sib, me).wait_recv()
        for j, chip in enumerate(chips):
            copy(4 + j, (*chip, 1 - c), me).wait_recv()
        for cp in first + passed:
            cp.wait_send()
        mine.wait()
        acc = buf[0]
        for d in range(1, NDEV):
            acc = acc + buf[d]
        o_ref[...] = acc

    vm = pl.BlockSpec(memory_space=pltpu.VMEM)
    return pl.pallas_call(
        body, name=name, in_specs=[vm], out_specs=vm, out_shape=S((r, 128), f32),
        scratch_shapes=[pltpu.VMEM((NDEV, r, 128), f32), pltpu.SemaphoreType.DMA((7,)), pltpu.SemaphoreType.DMA((7,)),
                        pltpu.SemaphoreType.DMA],
        compiler_params=pltpu.CompilerParams(vmem_limit_bytes=VMEM_LIMIT),
    )(v)


def _adam_math(w, g, m, v):
    m = ADAM_B1 * m + (1.0 - ADAM_B1) * g
    v = ADAM_B2 * v + (1.0 - ADAM_B2) * (g * g)
    m_hat = m / (1.0 - ADAM_B1 ** ADAM_STEP)
    v_hat = v / (1.0 - ADAM_B2 ** ADAM_STEP)
    delta = -ADAM_LR * (m_hat / (jnp.sqrt(v_hat) + ADAM_EPS) + ADAM_WD * w)
    return delta, m, v


def _adamw(name, w, m, v, parts):
    _, r, c = w.shape
    tr = _row_tile(r, c)

    def body(w_ref, m_ref, v_ref, *refs):
        sets, (g_ref, d_ref, mo_ref, vo_ref) = (refs[0:4], refs[4:8]), refs[8:]
        for l in range(2):
            @pl.when(pl.program_id(0) == l)
            def _():
                s_ref, r0_ref, r1_ref, r2_ref = sets[l]
                g = ((s_ref[...].astype(f32) + r0_ref[...].astype(f32)) + r1_ref[...].astype(f32)) + r2_ref[...].astype(f32)
                g_ref[...] = g
                d_ref[...], mo_ref[...], vo_ref[...] = _adam_math(w_ref[...], g, m_ref[...], v_ref[...])

    blk = _bs((None, tr, c), lambda l, i: (l, i, 0))
    operands, specs = [], []
    for n in range(2):
        row = (lambda l, i: i * (1 - l)) if n == 0 else (lambda l, i: i * l)
        s_mine, r2 = parts[n]
        operands += [s_mine, r2, r2, r2]
        specs.append(_bs((tr, c), functools.partial(lambda l, i, row: (row(l, i), 0), row=row)))
        specs += [_bs((None, None, tr, c), functools.partial(lambda l, i, k, row: (k, 0, row(l, i), 0), k=k, row=row))
                  for k in range(3)]
    return pl.pallas_call(
        body, name=name, grid=(2, r // tr), in_specs=[blk, blk, blk] + specs,
        out_specs=[blk] * 4, out_shape=[S(w.shape, f32)] * 4, compiler_params=_cp("arbitrary", "arbitrary"),
    )(w, m, v, *operands)


def _adamw_small(name, w, g, m, v):
    def body(w_ref, g_ref, m_ref, v_ref, d_ref, mo_ref, vo_ref):
        d_ref[...], mo_ref[...], vo_ref[...] = _adam_math(w_ref[...], g_ref[...], m_ref[...], v_ref[...])

    return pl.pallas_call(body, name=name, out_shape=[S(w.shape, f32)] * 3,
                          compiler_params=pltpu.CompilerParams(vmem_limit_bytes=VMEM_LIMIT))(w, g, m, v)


WEIGHT_NAMES = ("ffn1_norm", "ffn1_w_gu", "ffn1_w_down", "mix_norm", "w_in", "b_forget", "b_gate", "conv_w", "sgu_ln_g",
                "sgu_ln_b", "sgu_w", "sgu_b", "q_norm_g", "k_norm_g", "w_out_conv", "w_out_sgu", "w_out_attn", "w_o",
                "ffn2_norm", "ffn2_w_gu", "ffn2_w_down")
BIG = {"ffn1_w_gu": "gu1", "ffn2_w_gu": "gu2", "ffn1_w_down": "d1", "ffn2_w_down": "d2", "w_in": "in",
       "w_out_conv": "oc", "w_out_sgu": "os", "w_out_attn": "oa", "w_o": "o"}
BIG_KEYS = ("gu1", "gu2", "d1", "d2", "in", "oc", "os", "oa", "o")
REPLICATED_SMALL = ("ffn1_norm", "mix_norm", "b_forget", "sgu_ln_g", "sgu_ln_b", "sgu_w", "sgu_b", "q_norm_g",
                    "k_norm_g", "ffn2_norm")
SHARDED_SMALL = ("b_gate", "conv_w")


def _pack(arrays):
    flat = jnp.concatenate([a.reshape(-1).astype(f32) for a in arrays])
    rows = -(-flat.shape[0] // 1024) * 8
    return jnp.pad(flat, (0, rows * 128 - flat.shape[0])).reshape(rows, 128)


def _unpack(packed, shapes):
    flat = packed.reshape(-1)
    out, pos = [], 0
    for shp in shapes:
        size = 1
        for s_ in shp:
            size *= s_
        out.append(flat[pos:pos + size].reshape(shp))
        pos += size
    return out


def _natural_runs(a, b):
    runs = []
    while a < b:
        d = a // INB
        e = min(b, (d + 1) * INB)
        runs.append((d, a - d * INB, e - d * INB))
        a = e
    return runs


def _win_kernel_layout(wg):
    runs = _natural_runs(0, GATE_OFF) + _natural_runs(GATE_OFF + NH, NIN) + _natural_runs(GATE_OFF, GATE_OFF + NH)
    return jnp.concatenate([wg[d, :, a:b] for d, a, b in runs] + [jnp.zeros((D, NZ - NIN), wg.dtype)], axis=1)


def _kernel_column(n):
    return n if n < GATE_OFF else (F_OFF + n - GATE_OFF if n < GATE_OFF + NH else n - NH)


def _win_device_block(dw, d):
    cuts = sorted({d * INB, (d + 1) * INB} | {c for c in (GATE_OFF, GATE_OFF + NH) if d * INB < c < (d + 1) * INB})
    parts = [dw[:, _kernel_column(a):_kernel_column(a) + (b - a)] for a, b in zip(cuts[:-1], cuts[1:])]
    return parts[0] if len(parts) == 1 else jnp.concatenate(parts, axis=1)


def kernel(x, ffn1_norm, ffn1_w_gu, ffn1_w_down, mix_norm, w_in, b_forget, b_gate, conv_w, sgu_ln_g, sgu_ln_b, sgu_w, sgu_b, q_norm_g, k_norm_g, w_out_conv, w_out_sgu, w_out_attn, w_o, ffn2_norm, ffn2_w_gu, ffn2_w_down, loss_target, m_ffn1_norm, m_ffn1_w_gu, m_ffn1_w_down, m_mix_norm, m_w_in, m_b_forget, m_b_gate, m_conv_w, m_sgu_ln_g, m_sgu_ln_b, m_sgu_w, m_sgu_b, m_q_norm_g, m_k_norm_g, m_w_out_conv, m_w_out_sgu, m_w_out_attn, m_w_o, m_ffn2_norm, m_ffn2_w_gu, m_ffn2_w_down, v_ffn1_norm, v_ffn1_w_gu, v_ffn1_w_down, v_mix_norm, v_w_in, v_b_forget, v_b_gate, v_conv_w, v_sgu_ln_g, v_sgu_ln_b, v_sgu_w, v_sgu_b, v_q_norm_g, v_k_norm_g, v_w_out_conv, v_w_out_sgu, v_w_out_attn, v_w_o, v_ffn2_norm, v_ffn2_w_gu, v_ffn2_w_down):
    w = dict(zip(WEIGHT_NAMES, (ffn1_norm, ffn1_w_gu, ffn1_w_down, mix_norm, w_in, b_forget, b_gate, conv_w, sgu_ln_g,
                                sgu_ln_b, sgu_w, sgu_b, q_norm_g, k_norm_g, w_out_conv, w_out_sgu, w_out_attn, w_o,
                                ffn2_norm, ffn2_w_gu, ffn2_w_down)))
    mom = dict(zip(WEIGHT_NAMES, (m_ffn1_norm, m_ffn1_w_gu, m_ffn1_w_down, m_mix_norm, m_w_in, m_b_forget, m_b_gate,
                                  m_conv_w, m_sgu_ln_g, m_sgu_ln_b, m_sgu_w, m_sgu_b, m_q_norm_g, m_k_norm_g,
                                  m_w_out_conv, m_w_out_sgu, m_w_out_attn, m_w_o, m_ffn2_norm, m_ffn2_w_gu,
                                  m_ffn2_w_down)))
    var = dict(zip(WEIGHT_NAMES, (v_ffn1_norm, v_ffn1_w_gu, v_ffn1_w_down, v_mix_norm, v_w_in, v_b_forget, v_b_gate,
                                  v_conv_w, v_sgu_ln_g, v_sgu_ln_b, v_sgu_w, v_sgu_b, v_q_norm_g, v_k_norm_g,
                                  v_w_out_conv, v_w_out_sgu, v_w_out_attn, v_w_o, v_ffn2_norm, v_ffn2_w_gu,
                                  v_ffn2_w_down)))
    px, py, pc = _place()
    dev = 4 * px + 2 * py + pc
    chip = 2 * px + py

    big_names = [n for n in WEIGHT_NAMES if n in BIG]
    key_name = {BIG[n]: n for n in big_names}
    group_keys = (("gu1", "d1", "small"), ("in", "oc", "os", "oa", "o"), ("gu2", "d2"))

    def source(key, l):
        if key == "small":
            return jnp.concatenate([w["b_gate"][l], w["conv_w"][l], jnp.zeros((2, 128), f32)], axis=0)
        return w[key_name[key]][l].astype(bf16)

    def landing(src):
        return lax.dynamic_update_slice(lax.empty((NDEV,) + src.shape, src.dtype), src[None], (dev, 0, 0))

    groups = [[(s, landing(s)) for s in (source(k, l) for k in keys)] for l in range(2) for keys in group_keys]
    started = _gather_start("gather_start", groups)

    def weights(l, part, after):
        got = _gather_wait(f"gather_wait_{l}_{part}", started[3 * l + part], after)
        return dict(zip(group_keys[part], _gather_forward(f"gather_forward_{l}_{part}", got)))

    xl = x[0]
    saved, small, wts = [], [], []
    for l in range(2):
        ga = weights(l, 0, xl if l else None)
        p = {n: w[n][l] for n in REPLICATED_SMALL}
        p["b_gate"] = jnp.transpose(ga["small"][:, 0:3, :], (1, 0, 2)).reshape(3, D)
        p["conv_w"] = jnp.transpose(ga["small"][:, 3:6, :], (1, 0, 2)).reshape(3, D)
        p = _small_params(p)
        wt = {"gu1": ga["gu1"][None], "d1": ga["d1"].reshape(1, 4, GU, D)}
        x1, s1 = _ffn_fwd("ffn1", xl, p["ffn1_norm"], wt["gu1"], wt["d1"], 0)
        gb = weights(l, 1, x1)
        wt["win"] = _win_kernel_layout(gb["in"])
        wt["sq"] = jnp.stack([gb[k].reshape(D, D) for k in ("oc", "os", "oa", "o")])[None]
        x2, sm = _mixer_fwd("mix", x1, p, wt["win"], wt["sq"], 0)
        gc = weights(l, 2, x2)
        wt.update({"gu2": gc["gu2"][None], "d2": gc["d2"].reshape(1, 4, GU, D)})
        x3, s2 = _ffn_fwd("ffn2", x2, p["ffn2_norm"], wt["gu2"], wt["d2"], 0)
        saved.append((xl, x1, x2, s1, sm, s2))
        small.append(p)
        wts.append(wt)
        xl = x3
    loss_row, dx = _loss("loss", xl, loss_target[0])

    core = pc.reshape(1).astype(jnp.int32)
    sgrads, parts = [None, None], [None, None]
    for l in (1, 0):
        dx, sgrads[l], dwin, gout = _layer_bwd(dx, small[l], wts[l], saved[l])
        gout["in"] = jnp.stack([jnp.stack([_win_device_block(dwin, 2 * q + c)[None] for q in range(4)]) for c in range(2)])
        gs = [gout[k] for k in BIG_KEYS]
        r1 = _rs_pair(f"rs_pair_{l}", gs)
        ss = [_pair_sum(f"pair_sum_{k}_{l}", core, g, r) for k, g, r in zip(BIG_KEYS, gs, r1)]
        if l == 1:
            in_flight = _scatter_start("scatter_start_1", ss)
        else:
            done = _scatter_wait("scatter_wait_1", in_flight, ss[-1])
            for m_, (s_all, r2_all) in ((1, done), (0, (ss, _rs_chips("rs_chips_0", ss)))):
                parts[m_] = [(lax.dynamic_index_in_dim(s, chip, 0, keepdims=False)[0], r) for s, r in zip(s_all, r2_all)]

    grads, delta, new_m, new_v = {}, {}, {}, {}
    for n in big_names:
        i = BIG_KEYS.index(BIG[n])
        grads[n], delta[n], new_m[n], new_v[n] = _adamw("adamw_" + BIG[n], w[n], mom[n], var[n],
                                                        [parts[0][i], parts[1][i]])

    nat = [_small_grads_natural(sgrads[l]) for l in range(2)]
    order = REPLICATED_SMALL + SHARDED_SMALL
    part = _pack([jnp.stack([nat[0][n], nat[1][n]]) for n in order] + [loss_row[0, 0:1]])
    total = _all_reduce_small("reduce_small", part)
    full_shapes = [(2,) + tuple(nat[0][n].shape) for n in order] + [(1,)]
    summed = dict(zip(order + ("loss",), _unpack(total, full_shapes)))
    for n in REPLICATED_SMALL:
        grads[n] = summed[n]
    for n in SHARDED_SMALL:
        grads[n] = lax.dynamic_slice_in_dim(summed[n], dev * 128, 128, axis=2)
    wp = _pack([w[n] for n in order])
    gp = _pack([grads[n] for n in order])
    mp = _pack([mom[n] for n in order])
    vp = _pack([var[n] for n in order])
    dpk, mpk, vpk = _adamw_small("adamw_small", wp, gp, mp, vp)
    local_shapes = [tuple(w[n].shape) for n in order]
    for dst, packed in ((delta, dpk), (new_m, mpk), (new_v, vpk)):
        dst.update(zip(order, _unpack(packed, local_shapes)))

    loss = summed["loss"][0]
    return (loss, dx[None], *[grads[n] for n in WEIGHT_NAMES], *[delta[n] for n in WEIGHT_NAMES],
            *[new_m[n] for n in WEIGHT_NAMES], *[new_v[n] for n in WEIGHT_NAMES])
```

```python
import functools

import jax
import jax.numpy as jnp
from jax import lax
from jax.experimental import pallas as pl
from jax.experimental.pallas import tpu as pltpu

f32 = jnp.float32
bf16 = jnp.bfloat16
S = jax.ShapeDtypeStruct
MESH = pl.DeviceIdType.MESH

D = 1024
NH = 8
HD = 128
NDEV = 8
GU = 704
NIN = 11272
INB = 1409
GATE_OFF = 8192
F_OFF = 11264
NZ = 11776
RMS_EPS = 1e-6
LN_EPS = 1e-5
ATT_SCALE = HD ** -0.5
NEG = -1e30
INV_SQRT2 = 0.7071067811865476
INV_SQRT2PI = 0.3989422804014327

ADAM_LR = 0.001
ADAM_B1 = 0.9
ADAM_B2 = 0.999
ADAM_EPS = 1e-08
ADAM_WD = 0.01
ADAM_STEP = 10

TT = 256
VMEM_LIMIT = 56 * 1024 * 1024


def _cp(*sem):
    return pltpu.CompilerParams(dimension_semantics=sem, vmem_limit_bytes=VMEM_LIMIT)


def _bs(shape, fn):
    return pl.BlockSpec(shape, fn)


NN = (((1,), (0,)), ((), ()))
NT = (((1,), (1,)), ((), ()))
TN = (((0,), (0,)), ((), ()))


def _mm(name, a, b, *, grid, a_spec, b_spec, out_shape, out_spec, dims, acc_shape, res=None, res_spec=None,
        alpha=1.0, alias=None, split_rows=None):
    nk = grid[2]

    def body(*refs):
        a_ref, b_ref = refs[0], refs[1]
        pos = 2
        res_ref = None
        if res is not None:
            res_ref = refs[pos]
            pos += 1
        if alias is not None:
            pos += 1
        o_ref = refs[pos]
        acc_ref = refs[pos + 1] if nk > 1 else None
        part = lax.dot_general(a_ref[...].astype(bf16), b_ref[...].astype(bf16), dims, preferred_element_type=f32)

        def finish(acc):
            if alpha != 1.0:
                acc = alpha * acc
            if res_ref is not None:
                acc = res_ref[...] + acc
            if split_rows is None:
                o_ref[...] = acc.astype(o_ref.dtype)
            else:
                o_ref[0] = acc[:split_rows].astype(o_ref.dtype)
                o_ref[1] = acc[split_rows:].astype(o_ref.dtype)

        if nk == 1:
            finish(part)
        else:
            k = pl.program_id(2)

            @pl.when(k == 0)
            def _():
                acc_ref[...] = part

            @pl.when(k > 0)
            def _():
                acc_ref[...] += part

            @pl.when(k == nk - 1)
            def _():
                finish(acc_ref[...])

    operands = [a, b]
    in_specs = [a_spec, b_spec]
    if res is not None:
        operands.append(res)
        in_specs.append(res_spec)
    aliases = {}
    if alias is not None:
        aliases = {len(operands): 0}
        operands.append(alias)
        in_specs.append(pl.BlockSpec(memory_space=pl.ANY))
    return pl.pallas_call(
        body, name=name, grid=grid, in_specs=in_specs, out_specs=out_spec, out_shape=out_shape,
        scratch_shapes=[pltpu.VMEM(acc_shape, f32)] if nk > 1 else [],
        input_output_aliases=aliases,
        compiler_params=_cp("parallel", "parallel", "arbitrary"),
    )(*operands)


def _tile(n, t):
    return t if n % t == 0 and n >= t else n


def _row(cb=0, w=D):
    return _bs((TT, w), lambda i: (i, cb))


def _vec(rows=1, w=D):
    return _bs((rows, w), lambda i: (0, 0))


def _acc_store(i, ref, val):
    @pl.when(i == 0)
    def _():
        ref[...] = val

    @pl.when(i > 0)
    def _():
        ref[...] += val


def _rms_fwd(name, x, g):
    T = x.shape[0]

    def body(x_ref, g_ref, o_ref):
        xv = x_ref[...]
        r = lax.rsqrt(jnp.mean(xv * xv, axis=-1, keepdims=True) + RMS_EPS)
        o_ref[...] = (xv * r * g_ref[...]).astype(bf16)

    return pl.pallas_call(body, name=name, grid=(T // TT,), in_specs=[_row(), _vec()], out_specs=_row(),
                          out_shape=S((T, D), bf16), compiler_params=_cp("parallel"))(x, g)


def _rms_bwd(name, dh, x, g, dres):
    T = x.shape[0]

    def body(dh_ref, x_ref, g_ref, dres_ref, dx_ref, dg_ref):
        i = pl.program_id(0)
        xv = x_ref[...]
        r = lax.rsqrt(jnp.mean(xv * xv, axis=-1, keepdims=True) + RMS_EPS)
        xhat = xv * r
        dh_v = dh_ref[...]
        dyg = dh_v * g_ref[...]
        m = jnp.mean(dyg * xhat, axis=-1, keepdims=True)
        dx_ref[...] = dres_ref[...] + r * (dyg - xhat * m)
        _acc_store(i, dg_ref, jnp.sum(dh_v * xhat, axis=0, keepdims=True))

    return pl.pallas_call(body, name=name, grid=(T // TT,), in_specs=[_row(), _row(), _vec(), _row()],
                          out_specs=[_row(), _vec()], out_shape=[S((T, D), f32), S((1, D), f32)],
                          compiler_params=_cp("arbitrary"))(dh, x, g, dres)


def _sigmoid(x):
    return 1.0 / (1.0 + jnp.exp(-x))


def _swiglu_fwd(name, gu):
    T = gu.shape[1]

    def body(g_ref, u_ref, o_ref):
        g = g_ref[...]
        o_ref[...] = (g * _sigmoid(g) * u_ref[...]).astype(bf16)

    return pl.pallas_call(
        body, name=name, grid=(4, T // TT),
        in_specs=[_bs((None, TT, GU), lambda j, i: (j, i, 0)), _bs((None, TT, GU), lambda j, i: (j + 4, i, 0))],
        out_specs=_bs((None, TT, GU), lambda j, i: (j, i, 0)), out_shape=S((4, T, GU), bf16),
        compiler_params=_cp("parallel", "parallel"))(gu, gu)


def _swiglu_bwd(name, gu, da):
    T = gu.shape[1]

    def body(g_ref, u_ref, da_ref, o_ref):
        g = g_ref[...]
        u = u_ref[...]
        da_v = da_ref[...]
        sg = _sigmoid(g)
        o_ref[0] = (da_v * u * (sg + g * sg * (1.0 - sg))).astype(bf16)
        o_ref[1] = (da_v * g * sg).astype(bf16)

    return pl.pallas_call(
        body, name=name, grid=(4, T // TT),
        in_specs=[_bs((None, TT, GU), lambda j, i: (j, i, 0)), _bs((None, TT, GU), lambda j, i: (j + 4, i, 0)),
                  _bs((None, TT, GU), lambda j, i: (j, i, 0))],
        out_specs=_bs((2, None, TT, GU), lambda j, i: (0, j, i, 0)), out_shape=S((2, 4, T, GU), bf16),
        compiler_params=_cp("parallel", "parallel"))(gu, gu, da)


def _loss(name, y, tgt):
    T = y.shape[0]

    def body(y_ref, t_ref, l_ref, dy_ref):
        i = pl.program_id(0)
        e = y_ref[...] - t_ref[...]
        dy_ref[...] = e * (1.0 / D)
        s = 0.5 * jnp.sum(jnp.mean(e * e, axis=-1, keepdims=True))
        _acc_store(i, l_ref, jnp.broadcast_to(s, (1, 128)))

    return pl.pallas_call(body, name=name, grid=(T // TT,), in_specs=[_row(), _row()],
                          out_specs=[_vec(1, 128), _row()], out_shape=[S((1, 128), f32), S((T, D), f32)],
                          compiler_params=_cp("arbitrary"))(y, tgt)


def _prev8(T, cb):
    return _bs((8, D), lambda i: (jnp.maximum(i * (TT // 8) - 1, 0), cb))


def _next8(T, cb):
    return _bs((8, D), lambda i: (jnp.minimum((i + 1) * (TT // 8), T // 8 - 1), cb))


def _conv_taps(i, ac_ref, ax_ref, pc_ref, px_ref):
    ca = ac_ref[...] * ax_ref[...]
    keep = (i > 0).astype(f32)
    p1 = pc_ref[7:8, :] * px_ref[7:8, :] * keep
    p2 = pc_ref[6:7, :] * px_ref[6:7, :] * keep
    row = lax.broadcasted_iota(jnp.int32, ca.shape, 0)
    s1 = jnp.where(row == 0, p1, pltpu.roll(ca, 1, 0))
    s2 = jnp.where(row == 0, p2, jnp.where(row == 1, p1, pltpu.roll(ca, 2, 0)))
    return ca, s1, s2


def _conv_fwd(name, z, cw):
    T = z.shape[0]

    def body(ab_ref, ac_ref, ax_ref, pc_ref, px_ref, w_ref, o_ref):
        i = pl.program_id(0)
        ca, s1, s2 = _conv_taps(i, ac_ref, ax_ref, pc_ref, px_ref)
        cv = w_ref[0:1, :] * s2 + w_ref[1:2, :] * s1 + w_ref[2:3, :] * ca
        o_ref[...] = (ab_ref[...] * cv).astype(bf16)

    return pl.pallas_call(
        body, name=name, grid=(T // TT,),
        in_specs=[_row(0), _row(1), _row(2), _prev8(T, 1), _prev8(T, 2), _vec(3)],
        out_specs=_row(), out_shape=S((T, D), bf16), compiler_params=_cp("parallel"))(z, z, z, z, z, cw)


def _conv_bwd(name, dya, z, cw):
    T = z.shape[0]
    n = T // TT

    def body(dya_ref, ab_ref, ac_ref, ax_ref, pc_ref, px_ref, ndya_ref, nab_ref, w_ref, dz_ref, dw_ref):
        i = pl.program_id(0)
        ca, s1, s2 = _conv_taps(i, ac_ref, ax_ref, pc_ref, px_ref)
        w0, w1, w2 = w_ref[0:1, :], w_ref[1:2, :], w_ref[2:3, :]
        cv = w0 * s2 + w1 * s1 + w2 * ca
        dya_v = dya_ref[...]
        ab = ab_ref[...]
        dcv = dya_v * ab
        keep = (i < n - 1).astype(f32)
        n1 = ndya_ref[0:1, :] * nab_ref[0:1, :] * keep
        n2 = ndya_ref[1:2, :] * nab_ref[1:2, :] * keep
        row = lax.broadcasted_iota(jnp.int32, dcv.shape, 0)
        f1 = jnp.where(row == TT - 1, n1, pltpu.roll(dcv, TT - 1, 0))
        f2 = jnp.where(row == TT - 1, n2, jnp.where(row == TT - 2, n1, pltpu.roll(dcv, TT - 2, 0)))
        dca = w2 * dcv + w1 * f1 + w0 * f2
        dz_ref[:, 0:D] = (dya_v * cv).astype(bf16)
        dz_ref[:, D:2 * D] = (dca * ax_ref[...]).astype(bf16)
        dz_ref[:, 2 * D:3 * D] = (dca * ac_ref[...]).astype(bf16)
        dw = jnp.concatenate([jnp.sum(dcv * s2, axis=0, keepdims=True), jnp.sum(dcv * s1, axis=0, keepdims=True),
                              jnp.sum(dcv * ca, axis=0, keepdims=True)], axis=0)
        _acc_store(i, dw_ref, dw)

    return pl.pallas_call(
        body, name=name, grid=(n,),
        in_specs=[_row(), _row(0), _row(1), _row(2), _prev8(T, 1), _prev8(T, 2), _next8(T, 0), _next8(T, 0), _vec(3)],
        out_specs=[_row(0, 3 * D), _vec(3)], out_shape=[S((T, 3 * D), bf16), S((3, D), f32)],
        compiler_params=_cp("arbitrary"))(dya, z, z, z, z, z, dya, z, cw)


def _gelu(x):
    return 0.5 * x * (1.0 + lax.erf(x * INV_SQRT2))


def _gelu_grad(x):
    return 0.5 * (1.0 + lax.erf(x * INV_SQRT2)) + x * jnp.exp(-0.5 * x * x) * INV_SQRT2PI


def _ln_stats(vv):
    mu = jnp.mean(vv, axis=-1, keepdims=True)
    xc = vv - mu
    rstd = lax.rsqrt(jnp.mean(xc * xc, axis=-1, keepdims=True) + LN_EPS)
    return xc * rstd, rstd


def _tril_w(w_ref, g):
    r = lax.broadcasted_iota(jnp.int32, (HD, HD), 0)
    c = lax.broadcasted_iota(jnp.int32, (HD, HD), 1)
    return jnp.where(c <= r, w_ref[g], 0.0).astype(bf16)


def _sgu_fwd(name, z, ln_g, ln_b, w_s, bmap):
    T = z.shape[0]

    def body(su_ref, sv_ref, lg_ref, lb_ref, w_ref, bm_ref, o_ref, vn_ref):
        xhat, _ = _ln_stats(_gelu(sv_ref[...]))
        vn_ref[...] = (xhat * lg_ref[...] + lb_ref[...]).astype(bf16)
        for g in range(NH):
            w = _tril_w(w_ref, g)
            cs = slice(g * HD, (g + 1) * HD)
            for c in range(TT // HD):
                rs = slice(c * HD, (c + 1) * HD)
                s = jnp.dot(w, vn_ref[rs, cs], preferred_element_type=f32) + bm_ref[:, cs]
                o_ref[rs, cs] = (_gelu(su_ref[rs, cs]) * s).astype(bf16)

    return pl.pallas_call(
        body, name=name, grid=(T // TT,),
        in_specs=[_row(3), _row(4), _vec(), _vec(), _bs((NH, HD, HD), lambda i: (0, 0, 0)), _vec(HD)],
        out_specs=_row(), out_shape=S((T, D), bf16), scratch_shapes=[pltpu.VMEM((TT, D), bf16)],
        compiler_params=_cp("parallel"))(z, z, ln_g, ln_b, w_s, bmap)


def _sgu_bwd(name, dyb, z, ln_g, ln_b, w_s, bmap):
    T = z.shape[0]

    def body(dyb_ref, su_ref, sv_ref, lg_ref, lb_ref, w_ref, bm_ref, dz_ref, dlg_ref, dlb_ref, dw_ref, db_ref,
             vn_ref, du_ref, dvn_ref):
        i = pl.program_id(0)
        sv = sv_ref[...]
        xhat, rstd = _ln_stats(_gelu(sv))
        vn_ref[...] = (xhat * lg_ref[...] + lb_ref[...]).astype(bf16)
        r = lax.broadcasted_iota(jnp.int32, (HD, HD), 0)
        cc = lax.broadcasted_iota(jnp.int32, (HD, HD), 1)
        for g in range(NH):
            w = _tril_w(w_ref, g)
            cs = slice(g * HD, (g + 1) * HD)
            dw = jnp.zeros((HD, HD), f32)
            db = jnp.zeros((HD, 1), f32)
            for c in range(TT // HD):
                rs = slice(c * HD, (c + 1) * HD)
                vnb = vn_ref[rs, cs]
                s = jnp.dot(w, vnb, preferred_element_type=f32) + bm_ref[:, cs]
                dy = dyb_ref[rs, cs]
                du_ref[rs, cs] = dy * s
                ds = dy * _gelu(su_ref[rs, cs])
                ds16 = ds.astype(bf16)
                dvn_ref[rs, cs] = lax.dot_general(w, ds16, TN, preferred_element_type=f32)
                dw = dw + lax.dot_general(ds16, vnb, NT, preferred_element_type=f32)
                db = db + jnp.sum(ds, axis=1, keepdims=True)
            dw = jnp.where(cc <= r, dw, 0.0)

            @pl.when(i == 0)
            def _():
                dw_ref[g] = dw
                db_ref[:, g:g + 1] = db

            @pl.when(i > 0)
            def _():
                dw_ref[g] += dw
                db_ref[:, g:g + 1] += db

        dvn = dvn_ref[...]
        dxh = dvn * lg_ref[...]
        m1 = jnp.mean(dxh, axis=-1, keepdims=True)
        m2 = jnp.mean(dxh * xhat, axis=-1, keepdims=True)
        dvv = rstd * (dxh - m1 - xhat * m2)
        dz_ref[:, 0:D] = (du_ref[...] * _gelu_grad(su_ref[...])).astype(bf16)
        dz_ref[:, D:2 * D] = (dvv * _gelu_grad(sv)).astype(bf16)
        _acc_store(i, dlg_ref, jnp.sum(dvn * xhat, axis=0, keepdims=True))
        _acc_store(i, dlb_ref, jnp.sum(dvn, axis=0, keepdims=True))

    return pl.pallas_call(
        body, name=name, grid=(T // TT,),
        in_specs=[_row(), _row(3), _row(4), _vec(), _vec(), _bs((NH, HD, HD), lambda i: (0, 0, 0)), _vec(HD)],
        out_specs=[_row(0, 2 * D), _vec(), _vec(), _bs((NH, HD, HD), lambda i: (0, 0, 0)), _bs((HD, NH), lambda i: (0, 0))],
        out_shape=[S((T, 2 * D), bf16), S((1, D), f32), S((1, D), f32), S((NH, HD, HD), f32), S((HD, NH), f32)],
        scratch_shapes=[pltpu.VMEM((TT, D), bf16), pltpu.VMEM((TT, D), f32), pltpu.VMEM((TT, D), f32)],
        compiler_params=_cp("arbitrary"))(dyb, z, z, ln_g, ln_b, w_s, bmap)


def _qk_fwd(name, z, qg, kg, bf):
    T = z.shape[0]

    def body(q_ref, k_ref, v_ref, zf_ref, qg_ref, kg_ref, bf_ref, qn_ref, kn_ref, vb_ref, lf_ref):
        for h in range(NH):
            cs = slice(h * HD, (h + 1) * HD)
            for src, gain, dst in ((q_ref, qg_ref, qn_ref), (k_ref, kg_ref, kn_ref)):
                xv = src[:, cs]
                r = lax.rsqrt(jnp.mean(xv * xv, axis=-1, keepdims=True) + RMS_EPS)
                dst[:, cs] = (xv * r * gain[:, cs]).astype(bf16)
        vb_ref[...] = v_ref[...].astype(bf16)
        xf = zf_ref[...] + bf_ref[...]
        lf_ref[...] = jnp.minimum(xf, 0.0) - jnp.log1p(jnp.exp(-jnp.abs(xf)))

    return pl.pallas_call(
        body, name=name, grid=(T // TT,),
        in_specs=[_row(5), _row(6), _row(7), _bs((TT, 128), lambda i: (i, F_OFF // 128)), _vec(), _vec(), _vec(1, 128)],
        out_specs=[_row(), _row(), _row(), _bs((TT, 128), lambda i: (i, 0))],
        out_shape=[S((T, D), bf16), S((T, D), bf16), S((T, D), bf16), S((T, 128), f32)],
        compiler_params=_cp("parallel"))(z, z, z, z, qg, kg, bf)


def _cum_fwd(name, logf):
    T = logf.shape[0]

    def body(lf_ref, ccol_ref, crow_ref, c_ref):
        c = lf_ref[...]
        row = lax.broadcasted_iota(jnp.int32, c.shape, 0)
        s = 1
        while s < T:
            c = c + jnp.where(row >= s, pltpu.roll(c, s, 0), 0.0)
            s *= 2
        c_ref[...] = c
        crow_ref[...] = c.T[0:NH, :]
        for h in range(NH):
            ccol_ref[h] = jnp.broadcast_to(c_ref[:, h:h + 1], (T, 128))

    return pl.pallas_call(body, name=name, out_shape=[S((NH, T, 128), f32), S((NH, T), f32)],
                          scratch_shapes=[pltpu.VMEM((T, 128), f32)],
                          compiler_params=pltpu.CompilerParams(vmem_limit_bytes=VMEM_LIMIT))(logf)


ATT_TILE = 512


def _fold(x, op=jnp.add):
    acc = x[:, 0:128]
    for t in range(1, x.shape[1] // 128):
        acc = op(acc, x[:, t * 128:(t + 1) * 128])
    return acc


def _to_row(col):
    return jnp.broadcast_to(col, (col.shape[0], 128)).T[0:1, :]


def _causal(t, keys_down=False):
    r = lax.broadcasted_iota(jnp.int32, (t, t), 0)
    c = lax.broadcasted_iota(jnp.int32, (t, t), 1)
    return r <= c if keys_down else c <= r


def _attn_fwd(name, qn, kn, vb, ccol, crow3):
    T = qn.shape[0]
    tq = _tile(T, ATT_TILE)
    nq = T // tq

    def body(q_ref, k_ref, v_ref, cc_ref, cr_ref, o_ref, lse_ref, lser_ref, s_ref):
        qi = pl.program_id(1)
        q = q_ref[...]
        cq = cc_ref[:, 0:1]

        def logits(off):
            s = lax.dot_general(q, k_ref[pl.ds(off, tq), :], NT, preferred_element_type=f32) * ATT_SCALE
            return s + cq - cr_ref[:, pl.ds(off, tq)]

        def below(j, mvec):
            off = pl.multiple_of(j * tq, tq)
            s = logits(off)
            s_ref[:, pl.ds(off, tq)] = s
            return jnp.maximum(mvec, _fold(s, jnp.maximum))

        mvec = lax.fori_loop(0, qi, below, jnp.full((tq, 128), NEG, f32))
        off = pl.multiple_of(qi * tq, tq)
        s = jnp.where(_causal(tq), logits(off), NEG)
        s_ref[:, pl.ds(off, tq)] = s
        m = jnp.max(jnp.maximum(mvec, _fold(s, jnp.maximum)), axis=1, keepdims=True)

        def weigh(j, carry):
            lvec, acc = carry
            off = pl.multiple_of(j * tq, tq)
            p = jnp.exp(s_ref[:, pl.ds(off, tq)] - m)
            acc = acc + jnp.dot(p.astype(bf16), v_ref[pl.ds(off, tq), :], preferred_element_type=f32)
            return lvec + _fold(p), acc

        lvec, acc = lax.fori_loop(0, qi + 1, weigh, (jnp.zeros((tq, 128), f32), jnp.zeros((tq, HD), f32)))
        l = jnp.sum(lvec, axis=1, keepdims=True)
        o_ref[...] = acc / l
        lse = m + jnp.log(l)
        lse_ref[...] = jnp.broadcast_to(lse, (tq, 128))
        lser_ref[...] = _to_row(lse)

    return pl.pallas_call(
        body, name=name, grid=(NH, nq),
        in_specs=[_bs((tq, HD), lambda h, i: (i, h)), _bs((T, HD), lambda h, i: (0, h)), _bs((T, HD), lambda h, i: (0, h)),
                  _bs((None, tq, 128), lambda h, i: (h, i, 0)), _bs((None, 1, T), lambda h, i: (h, 0, 0))],
        out_specs=[_bs((tq, HD), lambda h, i: (i, h)), _bs((None, tq, 128), lambda h, i: (h, i, 0)),
                   _bs((None, 1, tq), lambda h, i: (h, 0, i))],
        out_shape=[S((T, D), f32), S((NH, T, 128), f32), S((NH, 1, T), f32)],
        scratch_shapes=[pltpu.VMEM((tq, T), f32)],
        compiler_params=_cp("parallel", "parallel"))(qn, kn, vb, ccol, crow3)


def _attn_dq(name, qn, kn, vb, do, lse, ccol, crow3):
    T = qn.shape[0]
    tq = _tile(T, ATT_TILE)
    nq = T // tq

    def body(q_ref, k_ref, v_ref, do_ref, lse_ref, cc_ref, cr_ref, dq_ref, dlr_ref, p_ref, dp_ref):
        qi = pl.program_id(1)
        q = q_ref[...]
        do16 = do_ref[...].astype(bf16)
        base = cc_ref[:, 0:1] - lse_ref[:, 0:1]

        def probs(off):
            s = lax.dot_general(q, k_ref[pl.ds(off, tq), :], NT, preferred_element_type=f32) * ATT_SCALE
            return jnp.exp(s + base - cr_ref[:, pl.ds(off, tq)])

        def keep(off, p, dvec):
            dp = lax.dot_general(do16, v_ref[pl.ds(off, tq), :], NT, preferred_element_type=f32)
            p_ref[:, pl.ds(off, tq)] = p
            dp_ref[:, pl.ds(off, tq)] = dp
            return dvec + _fold(p * dp)

        def below(j, dvec):
            off = pl.multiple_of(j * tq, tq)
            return keep(off, probs(off), dvec)

        dvec = lax.fori_loop(0, qi, below, jnp.zeros((tq, 128), f32))
        off = pl.multiple_of(qi * tq, tq)
        dvec = keep(off, jnp.where(_causal(tq), probs(off), 0.0), dvec)
        delta = jnp.sum(dvec, axis=1, keepdims=True)

        def grad(j, acc):
            off = pl.multiple_of(j * tq, tq)
            ds = p_ref[:, pl.ds(off, tq)] * (dp_ref[:, pl.ds(off, tq)] - delta)
            return acc + jnp.dot(ds.astype(bf16), k_ref[pl.ds(off, tq), :], preferred_element_type=f32)

        dq_ref[...] = lax.fori_loop(0, qi + 1, grad, jnp.zeros((tq, HD), f32)) * ATT_SCALE
        dlr_ref[...] = _to_row(delta)

    qb = lambda h, i: (i, h)
    full = lambda h, i: (0, h)
    col = lambda h, i: (h, i, 0)
    return pl.pallas_call(
        body, name=name, grid=(NH, nq),
        in_specs=[_bs((tq, HD), qb), _bs((T, HD), full), _bs((T, HD), full), _bs((tq, HD), qb),
                  _bs((None, tq, 128), col), _bs((None, tq, 128), col), _bs((None, 1, T), lambda h, i: (h, 0, 0))],
        out_specs=[_bs((tq, HD), qb), _bs((None, 1, tq), lambda h, i: (h, 0, i))],
        out_shape=[S((T, D), f32), S((NH, 1, T), f32)],
        scratch_shapes=[pltpu.VMEM((tq, T), f32), pltpu.VMEM((tq, T), f32)],
        compiler_params=_cp("parallel", "parallel"))(qn, kn, vb, do, lse, ccol, crow3)


def _attn_dkv(name, qn, kn, vb, do, lser3, dlr3, ccol, crow3):
    T = qn.shape[0]
    tk = _tile(T, ATT_TILE)
    nk = T // tk

    def body(q_ref, k_ref, v_ref, do_ref, lser_ref, dlr_ref, cc_ref, cr_ref, dk_ref, dv_ref, cs_ref):
        h = pl.program_id(0)
        kj = pl.program_id(1)

        @pl.when((h == 0) & (kj == 0))
        def _():
            cs_ref[...] = jnp.zeros_like(cs_ref)

        kb = k_ref[...]
        vv = v_ref[...]
        ckey = cc_ref[:, 0:1]

        def block(off, diagonal):
            rows = pl.ds(off, tk)
            qb = q_ref[rows, :]
            do16 = do_ref[rows, :].astype(bf16)
            st = lax.dot_general(kb, qb, NT, preferred_element_type=f32) * ATT_SCALE
            pt = jnp.exp(st + (cr_ref[:, rows] - lser_ref[:, rows]) - ckey)
            if diagonal:
                pt = jnp.where(_causal(tk, keys_down=True), pt, 0.0)
            dpt = lax.dot_general(vv, do16, NT, preferred_element_type=f32)
            dst = pt * (dpt - dlr_ref[:, rows])
            ddv = jnp.dot(pt.astype(bf16), do16, preferred_element_type=f32)
            ddk = jnp.dot(dst.astype(bf16), qb, preferred_element_type=f32)
            return ddk, ddv, _fold(dst)

        def above(i, carry):
            ddk, ddv, dcs = block(pl.multiple_of(i * tk, tk), False)
            return carry[0] + ddk, carry[1] + ddv, carry[2] + dcs

        off = pl.multiple_of(kj * tk, tk)
        dk, dv, cs = lax.fori_loop(kj + 1, nk, above, block(off, True))
        dk_ref[...] = dk * ATT_SCALE
        dv_ref[...] = dv
        lane = lax.broadcasted_iota(jnp.int32, (tk, 128), 1)
        cs_ref[pl.ds(off, tk), :] += jnp.where(lane == h, jnp.sum(cs, axis=1, keepdims=True), 0.0)

    full = lambda h, j: (0, h)
    blk = lambda h, j: (j, h)
    row = lambda h, j: (h, 0, 0)
    return pl.pallas_call(
        body, name=name, grid=(NH, nk),
        in_specs=[_bs((T, HD), full), _bs((tk, HD), blk), _bs((tk, HD), blk), _bs((T, HD), full), _bs((None, 1, T), row),
                  _bs((None, 1, T), row), _bs((None, tk, 128), lambda h, j: (h, j, 0)), _bs((None, 1, T), row)],
        out_specs=[_bs((tk, HD), blk), _bs((tk, HD), blk), _bs((T, 128), lambda h, j: (0, 0))],
        out_shape=[S((T, D), f32), S((T, D), f32), S((T, 128), f32)],
        compiler_params=_cp("arbitrary", "arbitrary"))(qn, kn, vb, do, lser3, dlr3, ccol, crow3)


def _forget_bwd(name, cs, z, bf):
    T = cs.shape[0]

    def body(cs_ref, zf_ref, bf_ref, dz_ref, db_ref):
        c = -cs_ref[...]
        row = lax.broadcasted_iota(jnp.int32, c.shape, 0)
        s = 1
        while s < T:
            c = c + jnp.where(row + s < T, pltpu.roll(c, T - s, 0), 0.0)
            s *= 2
        xf = zf_ref[...] + bf_ref[...]
        lane = lax.broadcasted_iota(jnp.int32, c.shape, 1)
        dxf = jnp.where(lane < NH, c / (1.0 + jnp.exp(xf)), 0.0)
        dz_ref[...] = jnp.zeros_like(dz_ref)
        dz_ref[:, 0:128] = dxf.astype(bf16)
        db_ref[...] = jnp.sum(dxf, axis=0, keepdims=True)

    return pl.pallas_call(
        body, name=name, grid=(1,),
        in_specs=[_bs((T, 128), lambda i: (0, 0)), _bs((T, 128), lambda i: (0, F_OFF // 128)), _vec(1, 128)],
        out_specs=[_bs((T, NZ - F_OFF), lambda i: (0, 0)), _vec(1, 128)],
        out_shape=[S((T, NZ - F_OFF), bf16), S((1, 128), f32)], compiler_params=_cp("arbitrary"))(cs, z, bf)


def _qk_bwd(name, dqn, dkn, dv, z, qg, kg):
    T = z.shape[0]

    def body(dq_ref, dk_ref, dv_ref, q_ref, k_ref, qg_ref, kg_ref, dz_ref, dqg_ref, dkg_ref, g_ref):
        i = pl.program_id(0)
        for n, (src, dsrc, gain, dgain) in enumerate(((q_ref, dq_ref, qg_ref, dqg_ref), (k_ref, dk_ref, kg_ref, dkg_ref))):
            for h in range(NH):
                cs = slice(h * HD, (h + 1) * HD)
                xv = src[:, cs]
                r = lax.rsqrt(jnp.mean(xv * xv, axis=-1, keepdims=True) + RMS_EPS)
                xhat = xv * r
                dy = dsrc[:, cs]
                dyg = dy * gain[:, cs]
                m = jnp.mean(dyg * xhat, axis=-1, keepdims=True)
                dz_ref[:, n * D + h * HD:n * D + (h + 1) * HD] = (r * (dyg - xhat * m)).astype(bf16)
                g_ref[:, cs] = jnp.sum(dy * xhat, axis=0, keepdims=True)
            _acc_store(i, dgain, g_ref[...])
        dz_ref[:, 2 * D:3 * D] = dv_ref[...].astype(bf16)

    return pl.pallas_call(
        body, name=name, grid=(T // TT,),
        in_specs=[_row(), _row(), _row(), _row(5), _row(6), _vec(), _vec()],
        out_specs=[_row(0, 3 * D), _vec(), _vec()], out_shape=[S((T, 3 * D), bf16), S((1, D), f32), S((1, D), f32)],
        scratch_shapes=[pltpu.VMEM((1, D), f32)], compiler_params=_cp("arbitrary"))(dqn, dkn, dv, z, z, qg, kg)


GB = GATE_OFF // D


def _merge_fwd(name, ya, yb, yc, z, bg):
    T = z.shape[0]

    def body(ya_ref, yb_ref, yc_ref, g0_ref, g1_ref, g2_ref, bg_ref, o_ref):
        acc = _sigmoid(g0_ref[...] + bg_ref[0:1, :]) * ya_ref[...]
        acc = acc + _sigmoid(g1_ref[...] + bg_ref[1:2, :]) * yb_ref[...]
        acc = acc + _sigmoid(g2_ref[...] + bg_ref[2:3, :]) * yc_ref[...]
        o_ref[...] = acc.astype(bf16)

    return pl.pallas_call(
        body, name=name, grid=(T // TT,),
        in_specs=[_row(), _row(), _row(), _row(GB), _row(GB + 1), _row(GB + 2), _vec(3)],
        out_specs=_row(), out_shape=S((T, D), bf16), compiler_params=_cp("parallel"))(ya, yb, yc, z, z, z, bg)


def _merge_bwd(name, dm, ya, yb, yc, z, bg):
    T = z.shape[0]

    def body(dm_ref, ya_ref, yb_ref, yc_ref, g0_ref, g1_ref, g2_ref, bg_ref, dya_ref, dyb_ref, dyc_ref, dz_ref, db_ref):
        i = pl.program_id(0)
        dm_v = dm_ref[...]
        dbs = []
        for n, (y_ref, g_ref, dy_ref) in enumerate(((ya_ref, g0_ref, dya_ref), (yb_ref, g1_ref, dyb_ref),
                                                    (yc_ref, g2_ref, dyc_ref))):
            gate = _sigmoid(g_ref[...] + bg_ref[n:n + 1, :])
            dy_ref[...] = (dm_v * gate).astype(bf16)
            dl = dm_v * y_ref[...] * gate * (1.0 - gate)
            dz_ref[:, n * D:(n + 1) * D] = dl.astype(bf16)
            dbs.append(jnp.sum(dl, axis=0, keepdims=True))
        _acc_store(i, db_ref, jnp.concatenate(dbs, axis=0))

    return pl.pallas_call(
        body, name=name, grid=(T // TT,),
        in_specs=[_row(), _row(), _row(), _row(), _row(GB), _row(GB + 1), _row(GB + 2), _vec(3)],
        out_specs=[_row(), _row(), _row(), _row(0, 3 * D), _vec(3)],
        out_shape=[S((T, D), bf16), S((T, D), bf16), S((T, D), bf16), S((T, 3 * D), bf16), S((3, D), f32)],
        compiler_params=_cp("arbitrary"))(dm, ya, yb, yc, z, z, z, bg)


SMALL_NAMES = ("ffn1_norm", "mix_norm", "b_forget", "b_gate", "conv_w", "sgu_ln_g", "sgu_ln_b", "sgu_w", "sgu_b",
               "q_norm_g", "k_norm_g", "ffn2_norm")


def _small_params(p):
    out = {n: p[n].reshape(1, D) for n in ("ffn1_norm", "mix_norm", "ffn2_norm", "sgu_ln_g", "sgu_ln_b", "q_norm_g", "k_norm_g")}
    out["b_forget"] = jnp.pad(p["b_forget"].reshape(1, NH), ((0, 0), (0, 128 - NH)))
    out["b_gate"] = p["b_gate"]
    out["conv_w"] = p["conv_w"]
    out["sgu_w"] = p["sgu_w"]
    out["bmap"] = jnp.repeat(p["sgu_b"].T, HD, axis=1)
    return out


def _small_grads_natural(sg):
    out = {n: sg[n].reshape(D) for n in ("ffn1_norm", "mix_norm", "ffn2_norm", "sgu_ln_g", "sgu_ln_b")}
    out["q_norm_g"] = sg["q_norm_g"].reshape(NH, HD)
    out["k_norm_g"] = sg["k_norm_g"].reshape(NH, HD)
    out["b_forget"] = sg["b_forget"][0, :NH]
    out["b_gate"] = sg["b_gate"]
    out["conv_w"] = sg["conv_w"]
    out["sgu_w"] = sg["sgu_w"]
    out["sgu_b"] = sg["sgu_b"]
    return out


def _sq_fwd(name, a, wsq, l, n, res=None):
    T = a.shape[0]
    tm = _tile(T, 512)
    return _mm(name, a, wsq, grid=(T // tm, 1, 1), a_spec=_bs((tm, D), lambda i, j, k: (i, 0)),
               b_spec=_bs((None, None, D, D), lambda i, j, k: (l, n, 0, 0)),
               out_shape=S((T, D), f32), out_spec=_bs((tm, D), lambda i, j, k: (i, 0)), dims=NN, acc_shape=None,
               res=res, res_spec=_bs((tm, D), lambda i, j, k: (i, 0)))


def _sq_bwd_in(name, dy, wsq, l, n):
    T = dy.shape[0]
    tm = _tile(T, 512)
    return _mm(name, dy, wsq, grid=(T // tm, 1, 1), a_spec=_bs((tm, D), lambda i, j, k: (i, 0)),
               b_spec=_bs((None, None, D, D), lambda i, j, k: (l, n, 0, 0)),
               out_shape=S((T, D), f32), out_spec=_bs((tm, D), lambda i, j, k: (i, 0)), dims=NT, acc_shape=None)


def _sq_bwd_w(name, a, dy, gbuf, l):
    T = a.shape[0]
    return _mm(name, a, dy, grid=(NDEV, 1, 1), a_spec=_bs((T, 128), lambda i, j, k: (0, i)),
               b_spec=_bs((T, D), lambda i, j, k: (0, 0)), out_shape=S(gbuf.shape, bf16),
               out_spec=_bs((None, None, None, 128, D), lambda i, j, k: (i % 2, i // 2, l, 0, 0)),
               dims=TN, acc_shape=None, alias=gbuf)


def _ffn_fwd(tag, x, g, wgu, wd, l):
    T = x.shape[0]
    tm = _tile(T, 1024)
    h = _rms_fwd(tag + "_rms", x, g)
    gu = _mm(tag + "_gu", h, wgu, grid=(T // tm, NDEV, 1), a_spec=_bs((tm, D), lambda i, j, k: (i, 0)),
             b_spec=_bs((None, None, D, GU), lambda i, j, k: (l, j, 0, 0)), out_shape=S((NDEV, T, GU), f32),
             out_spec=_bs((None, tm, GU), lambda i, j, k: (j, i, 0)), dims=NN, acc_shape=None)
    a = _swiglu_fwd(tag + "_act", gu)
    tm = _tile(T, 1024)
    xo = _mm(tag + "_down", a, wd, grid=(T // tm, 1, 4), a_spec=_bs((None, tm, GU), lambda i, j, k: (k, i, 0)),
             b_spec=_bs((None, None, GU, D), lambda i, j, k: (l, k, 0, 0)), out_shape=S((T, D), f32),
             out_spec=_bs((tm, D), lambda i, j, k: (i, 0)), dims=NN, acc_shape=(tm, D), res=x,
             res_spec=_bs((tm, D), lambda i, j, k: (i, 0)), alpha=0.5)
    return xo, (h, gu, a)


def _ffn_bwd(tag, dxo, x, g, wgu, wd, l, saved, g_gu, g_d):
    h, gu, a = saved
    T = x.shape[0]
    tm = _tile(T, 512)
    da = _mm(tag + "_dact", dxo, wd, grid=(T // tm, 4, 1), a_spec=_bs((tm, D), lambda i, j, k: (i, 0)),
             b_spec=_bs((None, None, GU, D), lambda i, j, k: (l, j, 0, 0)), out_shape=S((4, T, GU), f32),
             out_spec=_bs((None, tm, GU), lambda i, j, k: (j, i, 0)), dims=NT, acc_shape=None, alpha=0.5)
    g_d = _mm(tag + "_dwd", a, dxo, grid=(4, 1, 1), a_spec=_bs((None, T, GU), lambda i, j, k: (i, 0, 0)),
              b_spec=_bs((T, D), lambda i, j, k: (0, 0)), out_shape=S(g_d.shape, bf16),
              out_spec=_bs((2, None, None, GU // 2, D), lambda i, j, k: (0, i, l, 0, 0)), dims=TN, acc_shape=None,
              alpha=0.5, alias=g_d, split_rows=GU // 2)
    dgu = _swiglu_bwd(tag + "_dgu", gu, da).reshape(NDEV, T, GU)
    dh = _mm(tag + "_dh", dgu, wgu, grid=(1, 1, NDEV), a_spec=_bs((None, T, GU), lambda i, j, k: (k, 0, 0)),
             b_spec=_bs((None, None, D, GU), lambda i, j, k: (l, k, 0, 0)), out_shape=S((T, D), f32),
             out_spec=_bs((T, D), lambda i, j, k: (0, 0)), dims=NT, acc_shape=(T, D))
    g_gu = _mm(tag + "_dwgu", h, dgu, grid=(1, NDEV, 1), a_spec=_bs((T, D), lambda i, j, k: (0, 0)),
               b_spec=_bs((None, T, GU), lambda i, j, k: (j, 0, 0)), out_shape=S(g_gu.shape, bf16),
               out_spec=_bs((None, None, None, D, GU), lambda i, j, k: (j % 2, j // 2, l, 0, 0)), dims=TN,
               acc_shape=None, alias=g_gu)
    dx, dg = _rms_bwd(tag + "_drms", dh, x, g, dxo)
    return dx, dg, g_gu, g_d


def _mixer_fwd(tag, x, p, win, wsq, l):
    T = x.shape[0]
    h = _rms_fwd(tag + "_rms", x, p["mix_norm"])
    tn = 512
    z = _mm(tag + "_in", h, win, grid=(1, NZ // tn, 1), a_spec=_bs((T, D), lambda i, j, k: (0, 0)),
            b_spec=_bs((D, tn), lambda i, j, k: (0, j)), out_shape=S((T, NZ), f32),
            out_spec=_bs((T, tn), lambda i, j, k: (0, j)), dims=NN, acc_shape=None)
    ya_in = _conv_fwd(tag + "_conv", z, p["conv_w"])
    yb_in = _sgu_fwd(tag + "_sgu", z, p["sgu_ln_g"], p["sgu_ln_b"], p["sgu_w"], p["bmap"])
    qn, kn, vb, logf = _qk_fwd(tag + "_qk", z, p["q_norm_g"], p["k_norm_g"], p["b_forget"])
    ccol, crow = _cum_fwd(tag + "_cum", logf)
    crow3 = crow.reshape(NH, 1, T)
    o, lse, lser = _attn_fwd(tag + "_attn", qn, kn, vb, ccol, crow3)
    ya = _sq_fwd(tag + "_oconv", ya_in, wsq, l, 0)
    yb = _sq_fwd(tag + "_osgu", yb_in, wsq, l, 1)
    yc = _sq_fwd(tag + "_oattn", o, wsq, l, 2)
    merged = _merge_fwd(tag + "_merge", ya, yb, yc, z, p["b_gate"])
    xo = _sq_fwd(tag + "_o", merged, wsq, l, 3, res=x)
    return xo, (h, z, ya_in, yb_in, qn, kn, vb, ccol, crow3, o, lse, lser, ya, yb, yc, merged)


def _mixer_bwd(tag, dxo, x, p, win, wsq, l, saved, gsq):
    h, z, ya_in, yb_in, qn, kn, vb, ccol, crow3, o, lse, lser, ya, yb, yc, merged = saved
    T = x.shape[0]
    sg = {}
    dm = _sq_bwd_in(tag + "_dmerged", dxo, wsq, l, 3)
    gsq[3] = _sq_bwd_w(tag + "_dwo", merged, dxo, gsq[3], l)
    dya, dyb, dyc, dz_g, sg["b_gate"] = _merge_bwd(tag + "_dmerge", dm, ya, yb, yc, z, p["b_gate"])
    d_ya_in = _sq_bwd_in(tag + "_dconv_in", dya, wsq, l, 0)
    gsq[0] = _sq_bwd_w(tag + "_dwoc", ya_in, dya, gsq[0], l)
    d_yb_in = _sq_bwd_in(tag + "_dsgu_in", dyb, wsq, l, 1)
    gsq[1] = _sq_bwd_w(tag + "_dwos", yb_in, dyb, gsq[1], l)
    d_o = _sq_bwd_in(tag + "_dattn_in", dyc, wsq, l, 2)
    gsq[2] = _sq_bwd_w(tag + "_dwoa", o, dyc, gsq[2], l)
    dz_c, sg["conv_w"] = _conv_bwd(tag + "_dconv", d_ya_in, z, p["conv_w"])
    dz_s, sg["sgu_ln_g"], sg["sgu_ln_b"], sg["sgu_w"], db_t = _sgu_bwd(
        tag + "_dsgu", d_yb_in, z, p["sgu_ln_g"], p["sgu_ln_b"], p["sgu_w"], p["bmap"])
    sg["sgu_b"] = db_t.T
    dqn, dlr = _attn_dq(tag + "_dattn_q", qn, kn, vb, d_o, lse, ccol, crow3)
    dkn, dv, cs = _attn_dkv(tag + "_dattn_kv", qn, kn, vb, d_o, lser, dlr, ccol, crow3)
    dz_f, sg["b_forget"] = _forget_bwd(tag + "_dforget", cs, z, p["b_forget"])
    dz_q, sg["q_norm_g"], sg["k_norm_g"] = _qk_bwd(tag + "_dqk", dqn, dkn, dv, z, p["q_norm_g"], p["k_norm_g"])
    dz = jnp.concatenate([dz_c, dz_s, dz_q, dz_g, dz_f], axis=1)
    tk = 512
    dh = _mm(tag + "_dh", dz, win, grid=(1, 1, NZ // tk), a_spec=_bs((T, tk), lambda i, j, k: (0, k)),
             b_spec=_bs((D, tk), lambda i, j, k: (0, k)), out_shape=S((T, D), f32),
             out_spec=_bs((T, D), lambda i, j, k: (0, 0)), dims=NT, acc_shape=(T, D))
    tn = 512
    dwin = _mm(tag + "_dwin", h, dz, grid=(1, NZ // tn, 1), a_spec=_bs((T, D), lambda i, j, k: (0, 0)),
               b_spec=_bs((T, tn), lambda i, j, k: (0, j)), out_shape=S((D, NZ), bf16),
               out_spec=_bs((D, tn), lambda i, j, k: (0, j)), dims=TN, acc_shape=None)
    dx, sg["mix_norm"] = _rms_bwd(tag + "_drms", dh, x, p["mix_norm"], dxo)
    return dx, sg, dwin, gsq


ANY = pl.BlockSpec(memory_space=pl.ANY)
HBM = pl.BlockSpec(memory_space=pltpu.HBM)
SEM = pl.BlockSpec(memory_space=pltpu.SEMAPHORE)
EFFECT = pltpu.SideEffectType.DATAFLOW_SIDE_EFFECTING


def _place():
    return lax.axis_index("x"), lax.axis_index("y"), lax.axis_index("c")


NEAR = 4


def _others(x, y, c):
    return [(x, y, 1 - c), (1 - x, y, c), (x, 1 - y, c), (1 - x, 1 - y, c)]


def _gather_start(name, groups):
    sizes = [len(g) for g in groups]
    srcs = [s for g in groups for s, _ in g]
    lands = [b for g in groups for _, b in g]
    n, ng = len(srcs), len(groups)

    def body(*refs):
        src_refs, land_refs = refs[:n], refs[n:2 * n]
        send, recv = refs[2 * n:2 * n + ng], refs[2 * n + ng:2 * n + 2 * ng]
        x, y, c = _place()
        me = 4 * x + 2 * y + c
        u = 0
        for g, size in enumerate(sizes):
            for i in range(size):
                for k, peer in enumerate(_others(x, y, c)):
                    pltpu.make_async_remote_copy(src_ref=src_refs[u], dst_ref=land_refs[u].at[me],
                                                 send_sem=send[g].at[i * NEAR + k], recv_sem=recv[g].at[i * NEAR + k],
                                                 device_id=peer, device_id_type=MESH).start()
                u += 1

    sems = [pltpu.SemaphoreType.DMA((size * NEAR,)) for size in sizes]
    out = pl.pallas_call(
        body, name=name, in_specs=[HBM] * (2 * n), out_specs=[SEM] * (2 * ng) + [HBM] * (2 * n),
        out_shape=sems + sems + [pltpu.HBM(a.shape, a.dtype) for a in srcs + lands],
        input_output_aliases={i: 2 * ng + i for i in range(2 * n)},
        compiler_params=pltpu.CompilerParams(has_side_effects=EFFECT),
    )(*[pltpu.with_memory_space_constraint(a, pltpu.HBM) for a in srcs + lands])
    res, pos = [], 0
    for g, size in enumerate(sizes):
        res.append((out[g], out[ng + g], out[2 * ng + pos:2 * ng + pos + size], out[2 * ng + n + pos:2 * ng + n + pos + size]))
        pos += size
    return res


def _gather_wait(name, started, after=None):
    send, recv, srcs, lands = started
    n = len(srcs)

    def body(*refs):
        src_refs, land_refs = refs[:n], refs[n:2 * n]
        send_ref, recv_ref = refs[2 * n], refs[2 * n + 1]
        x, y, c = _place()
        for i in range(n):
            for k, (px, py, pc) in enumerate(_others(x, y, c)):
                cp = pltpu.make_async_remote_copy(src_ref=src_refs[i], dst_ref=land_refs[i].at[4 * px + 2 * py + pc],
                                                  send_sem=send_ref.at[i * NEAR + k], recv_sem=recv_ref.at[i * NEAR + k],
                                                  device_id=(px, py, pc), device_id_type=MESH)
                cp.wait_send()
                cp.wait_recv()

    extra = [] if after is None else [after]
    out = pl.pallas_call(
        body, name=name, in_specs=[HBM] * (2 * n) + [SEM, SEM] + [ANY] * len(extra), out_specs=[HBM] * (2 * n),
        out_shape=[pltpu.HBM(a.shape, a.dtype) for a in list(srcs) + list(lands)],
        input_output_aliases={i: i for i in range(2 * n)},
        compiler_params=pltpu.CompilerParams(has_side_effects=EFFECT),
    )(*srcs, *lands, send, recv, *extra)
    return out[n:]


def _gather_forward(name, lands):
    n = len(lands)

    def body(*refs):
        have, full = refs[:n], refs[n:2 * n]
        send, recv = refs[2 * n], refs[2 * n + 1]
        x, y, c = _place()
        chips = [(1 - x, y), (x, 1 - y), (1 - x, 1 - y)]

        def copy(i, j, core):
            slot = 4 * chips[j][0] + 2 * chips[j][1] + core
            return pltpu.make_async_remote_copy(src_ref=have[i].at[slot], dst_ref=full[i].at[slot],
                                                send_sem=send.at[i * 3 + j], recv_sem=recv.at[i * 3 + j],
                                                device_id=(x, y, 1 - c), device_id_type=MESH)

        for i in range(n):
            for j in range(3):
                copy(i, j, c).start()
        for i in range(n):
            for j in range(3):
                copy(i, j, c).wait_send()
                copy(i, j, 1 - c).wait_recv()

    return pl.pallas_call(
        body, name=name, in_specs=[ANY] * n, out_specs=[ANY] * n, out_shape=[S(a.shape, a.dtype) for a in lands],
        input_output_aliases={i: i for i in range(n)},
        scratch_shapes=[pltpu.SemaphoreType.DMA((n * 3,)), pltpu.SemaphoreType.DMA((n * 3,))],
    )(*lands)


def _rs_pair(name, gs):
    n = len(gs)

    def body(*refs):
        ins, outs = refs[:n], refs[n:2 * n]
        send, recv = refs[2 * n:]
        x, y, c = _place()
        cps = [pltpu.make_async_remote_copy(src_ref=ins[u].at[1 - c], dst_ref=outs[u], send_sem=send.at[u],
                                            recv_sem=recv.at[u], device_id=(x, y, 1 - c), device_id_type=MESH)
               for u in range(n)]
        for cp in cps:
            cp.start()
        for cp in cps:
            cp.wait()

    return pl.pallas_call(
        body, name=name, in_specs=[ANY] * n, out_specs=[ANY] * n, out_shape=[S(g.shape[1:], g.dtype) for g in gs],
        scratch_shapes=[pltpu.SemaphoreType.DMA((n,)), pltpu.SemaphoreType.DMA((n,))],
    )(*gs)


def _row_tile(r, c):
    return 128 if (r % 128 == 0 and c > D) else (256 if r % 256 == 0 else r)


def _pair_sum(name, core, g, r1):
    _, nq, nl, r, c = g.shape
    tr = _row_tile(r, c)
    g4 = g.reshape(2, nq * nl, r, c)
    r3 = r1.reshape(nq * nl, r, c)

    def body(core_ref, g_ref, r_ref, o_ref):
        o_ref[...] = (g_ref[...].astype(f32) + r_ref[...].astype(f32)).astype(bf16)

    out = pl.pallas_call(
        body, name=name,
        grid_spec=pltpu.PrefetchScalarGridSpec(
            num_scalar_prefetch=1, grid=(nq * nl, r // tr),
            in_specs=[_bs((None, None, tr, c), lambda b, i, cr: (cr[0], b, i, 0)), _bs((None, tr, c), lambda b, i, cr: (b, i, 0))],
            out_specs=_bs((None, tr, c), lambda b, i, cr: (b, i, 0))),
        out_shape=S((nq * nl, r, c), bf16), compiler_params=_cp("parallel", "parallel"))(core, g4, r3)
    return out.reshape(nq, nl, r, c)


def _scatter_copies(ins, outs, send, recv):
    x, y, c = _place()
    chips = [(1 - x, y), (x, 1 - y), (1 - x, 1 - y)]
    return [pltpu.make_async_remote_copy(src_ref=ins[u].at[2 * chip[0] + chip[1]], dst_ref=outs[u].at[k],
                                         send_sem=send.at[u * 3 + k], recv_sem=recv.at[u * 3 + k],
                                         device_id=(*chip, c), device_id_type=MESH)
            for u in range(len(ins)) for k, chip in enumerate(chips)]


def _scatter_start(name, ss):
    n = len(ss)
    lands = [lax.empty((3,) + s.shape[1:], s.dtype) for s in ss]

    def body(*refs):
        for cp in _scatter_copies(refs[:n], refs[n:2 * n], refs[2 * n], refs[2 * n + 1]):
            cp.start()

    sems = [pltpu.SemaphoreType.DMA((n * 3,))] * 2
    out = pl.pallas_call(
        body, name=name, in_specs=[HBM] * (2 * n), out_specs=[SEM, SEM] + [HBM] * (2 * n),
        out_shape=sems + [pltpu.HBM(a.shape, a.dtype) for a in list(ss) + lands],
        input_output_aliases={i: 2 + i for i in range(2 * n)},
        compiler_params=pltpu.CompilerParams(has_side_effects=EFFECT),
    )(*[pltpu.with_memory_space_constraint(a, pltpu.HBM) for a in list(ss) + lands])
    return out[0], out[1], out[2:2 + n], out[2 + n:]


def _scatter_wait(name, started, after):
    send, recv, srcs, lands = started
    n = len(srcs)

    def body(*refs):
        for cp in _scatter_copies(refs[:n], refs[n:2 * n], refs[2 * n], refs[2 * n + 1]):
            cp.wait_send()
            cp.wait_recv()

    out = pl.pallas_call(
        body, name=name, in_specs=[HBM] * (2 * n) + [SEM, SEM, ANY], out_specs=[HBM] * (2 * n),
        out_shape=[pltpu.HBM(a.shape, a.dtype) for a in list(srcs) + list(lands)],
        input_output_aliases={i: i for i in range(2 * n)},
        compiler_params=pltpu.CompilerParams(has_side_effects=EFFECT),
    )(*srcs, *lands, send, recv, after)
    return out[:n], out[n:]


def _all_reduce_small(name, v):
    r = v.shape[0]

    def body(v_ref, o_ref, buf, send, recv, lsem):
        x, y, c = _place()
        me, sib = (x, y, c), (x, y, 1 - c)
        chips = [(1 - x, y), (x, 1 - y), (1 - x, 1 - y)]

        def slot(px, py, pc):
            return buf.at[4 * px + 2 * py + pc]

        def copy(k, block, to, src=None):
            return pltpu.make_async_remote_copy(src_ref=slot(*block) if src is None else src, dst_ref=slot(*block),
                                                send_sem=send.at[k], recv_sem=recv.at[k], device_id=to,
                                                device_id_type=MESH)

        mine = pltpu.make_async_copy(v_ref, slot(*me), lsem)
        mine.start()
        first = [copy(0, me, sib, v_ref)] + [copy(1 + j, me, (*chip, c), v_ref) for j, chip in enumerate(chips)]
        for cp in first:
            cp.start()
        passed = [copy(4 + j, (*chip, c), sib) for j, chip in enumerate(chips)]
        for j, chip in enumerate(chips):
            copy(1 + j, (*chip, c), me).wait_recv()
            passed[j].start()
        copy(0, sib, me).wait_recv()
        for j, chip in enumerate(chips):
            copy(4 + j, (*chip, 1 - c), me).wait_recv()
        for cp in first + passed:
            cp.wait_send()
        mine.wait()
        acc = buf[0]
        for d in range(1, NDEV):
            acc = acc + buf[d]
        o_ref[...] = acc

    vm = pl.BlockSpec(memory_space=pltpu.VMEM)
    return pl.pallas_call(
        body, name=name, in_specs=[vm], out_specs=vm, out_shape=S((r, 128), f32),
        scratch_shapes=[pltpu.VMEM((NDEV, r, 128), f32), pltpu.SemaphoreType.DMA((7,)), pltpu.SemaphoreType.DMA((7,)),
                        pltpu.SemaphoreType.DMA],
        compiler_params=pltpu.CompilerParams(vmem_limit_bytes=VMEM_LIMIT),
    )(v)


def _adam_math(w, g, m, v):
    m = ADAM_B1 * m + (1.0 - ADAM_B1) * g
    v = ADAM_B2 * v + (1.0 - ADAM_B2) * (g * g)
    m_hat = m / (1.0 - ADAM_B1 ** ADAM_STEP)
    v_hat = v / (1.0 - ADAM_B2 ** ADAM_STEP)
    delta = -ADAM_LR * (m_hat / (jnp.sqrt(v_hat) + ADAM_EPS) + ADAM_WD * w)
    return delta, m, v


def _adamw(name, w, m, v, parts):
    _, r, c = w.shape
    tr = _row_tile(r, c)

    def body(w_ref, m_ref, v_ref, *refs):
        sets, (g_ref, d_ref, mo_ref, vo_ref) = (refs[0:4], refs[4:8]), refs[8:]
        for l in range(2):
            @pl.when(pl.program_id(0) == l)
            def _():
                s_ref, r0_ref, r1_ref, r2_ref = sets[l]
                g = ((s_ref[...].astype(f32) + r0_ref[...].astype(f32)) + r1_ref[...].astype(f32)) + r2_ref[...].astype(f32)
                g_ref[...] = g
                d_ref[...], mo_ref[...], vo_ref[...] = _adam_math(w_ref[...], g, m_ref[...], v_ref[...])

    blk = _bs((None, tr, c), lambda l, i: (l, i, 0))
    operands, specs = [], []
    for n in range(2):
        row = (lambda l, i: i * (1 - l)) if n == 0 else (lambda l, i: i * l)
        s_mine, r2 = parts[n]
        operands += [s_mine, r2, r2, r2]
        specs.append(_bs((tr, c), functools.partial(lambda l, i, row: (row(l, i), 0), row=row)))
        specs += [_bs((None, None, tr, c), functools.partial(lambda l, i, k, row: (k, 0, row(l, i), 0), k=k, row=row))
                  for k in range(3)]
    return pl.pallas_call(
        body, name=name, grid=(2, r // tr), in_specs=[blk, blk, blk] + specs,
        out_specs=[blk] * 4, out_shape=[S(w.shape, f32)] * 4, compiler_params=_cp("arbitrary", "arbitrary"),
    )(w, m, v, *operands)


def _adamw_small(name, w, g, m, v):
    def body(w_ref, g_ref, m_ref, v_ref, d_ref, mo_ref, vo_ref):
        d_ref[...], mo_ref[...], vo_ref[...] = _adam_math(w_ref[...], g_ref[...], m_ref[...], v_ref[...])

    return pl.pallas_call(body, name=name, out_shape=[S(w.shape, f32)] * 3,
                          compiler_params=pltpu.CompilerParams(vmem_limit_bytes=VMEM_LIMIT))(w, g, m, v)


WEIGHT_NAMES = ("ffn1_norm", "ffn1_w_gu", "ffn1_w_down", "mix_norm", "w_in", "b_forget", "b_gate", "conv_w", "sgu_ln_g",
                "sgu_ln_b", "sgu_w", "sgu_b", "q_norm_g", "k_norm_g", "w_out_conv", "w_out_sgu", "w_out_attn", "w_o",
                "ffn2_norm", "ffn2_w_gu", "ffn2_w_down")
BIG = {"ffn1_w_gu": "gu1", "ffn2_w_gu": "gu2", "ffn1_w_down": "d1", "ffn2_w_down": "d2", "w_in": "in",
       "w_out_conv": "oc", "w_out_sgu": "os", "w_out_attn": "oa", "w_o": "o"}
BIG_KEYS = ("gu1", "gu2", "d1", "d2", "in", "oc", "os", "oa", "o")
REPLICATED_SMALL = ("ffn1_norm", "mix_norm", "b_forget", "sgu_ln_g", "sgu_ln_b", "sgu_w", "sgu_b", "q_norm_g",
                    "k_norm_g", "ffn2_norm")
SHARDED_SMALL = ("b_gate", "conv_w")


def _pack(arrays):
    flat = jnp.concatenate([a.reshape(-1).astype(f32) for a in arrays])
    rows = -(-flat.shape[0] // 1024) * 8
    return jnp.pad(flat, (0, rows * 128 - flat.shape[0])).reshape(rows, 128)


def _unpack(packed, shapes):
    flat = packed.reshape(-1)
    out, pos = [], 0
    for shp in shapes:
        size = 1
        for s_ in shp:
            size *= s_
        out.append(flat[pos:pos + size].reshape(shp))
        pos += size
    return out


def _natural_runs(a, b):
    runs = []
    while a < b:
        d = a // INB
        e = min(b, (d + 1) * INB)
        runs.append((d, a - d * INB, e - d * INB))
        a = e
    return runs


def _win_kernel_layout(wg):
    runs = _natural_runs(0, GATE_OFF) + _natural_runs(GATE_OFF + NH, NIN) + _natural_runs(GATE_OFF, GATE_OFF + NH)
    return jnp.concatenate([wg[d, :, a:b] for d, a, b in runs] + [jnp.zeros((D, NZ - NIN), wg.dtype)], axis=1)


def _kernel_column(n):
    return n if n < GATE_OFF else (F_OFF + n - GATE_OFF if n < GATE_OFF + NH else n - NH)


def _win_device_block(dw, d):
    cuts = sorted({d * INB, (d + 1) * INB} | {c for c in (GATE_OFF, GATE_OFF + NH) if d * INB < c < (d + 1) * INB})
    parts = [dw[:, _kernel_column(a):_kernel_column(a) + (b - a)] for a, b in zip(cuts[:-1], cuts[1:])]
    return parts[0] if len(parts) == 1 else jnp.concatenate(parts, axis=1)


def kernel(x, ffn1_norm, ffn1_w_gu, ffn1_w_down, mix_norm, w_in, b_forget, b_gate, conv_w, sgu_ln_g, sgu_ln_b, sgu_w, sgu_b, q_norm_g, k_norm_g, w_out_conv, w_out_sgu, w_out_attn, w_o, ffn2_norm, ffn2_w_gu, ffn2_w_down, loss_target, m_ffn1_norm, m_ffn1_w_gu, m_ffn1_w_down, m_mix_norm, m_w_in, m_b_forget, m_b_gate, m_conv_w, m_sgu_ln_g, m_sgu_ln_b, m_sgu_w, m_sgu_b, m_q_norm_g, m_k_norm_g, m_w_out_conv, m_w_out_sgu, m_w_out_attn, m_w_o, m_ffn2_norm, m_ffn2_w_gu, m_ffn2_w_down, v_ffn1_norm, v_ffn1_w_gu, v_ffn1_w_down, v_mix_norm, v_w_in, v_b_forget, v_b_gate, v_conv_w, v_sgu_ln_g, v_sgu_ln_b, v_sgu_w, v_sgu_b, v_q_norm_g, v_k_norm_g, v_w_out_conv, v_w_out_sgu, v_w_out_attn, v_w_o, v_ffn2_norm, v_ffn2_w_gu, v_ffn2_w_down):
    w = dict(zip(WEIGHT_NAMES, (ffn1_norm, ffn1_w_gu, ffn1_w_down, mix_norm, w_in, b_forget, b_gate, conv_w, sgu_ln_g,
                                sgu_ln_b, sgu_w, sgu_b, q_norm_g, k_norm_g, w_out_conv, w_out_sgu, w_out_attn, w_o,
                                ffn2_norm, ffn2_w_gu, ffn2_w_down)))
    mom = dict(zip(WEIGHT_NAMES, (m_ffn1_norm, m_ffn1_w_gu, m_ffn1_w_down, m_mix_norm, m_w_in, m_b_forget, m_b_gate,
                                  m_conv_w, m_sgu_ln_g, m_sgu_ln_b, m_sgu_w, m_sgu_b, m_q_norm_g, m_k_norm_g,
                                  m_w_out_conv, m_w_out_sgu, m_w_out_attn, m_w_o, m_ffn2_norm, m_ffn2_w_gu,
                                  m_ffn2_w_down)))
    var = dict(zip(WEIGHT_NAMES, (v_ffn1_norm, v_ffn1_w_gu, v_ffn1_w_down, v_mix_norm, v_w_in, v_b_forget, v_b_gate,
                                  v_conv_w, v_sgu_ln_g, v_sgu_ln_b, v_sgu_w, v_sgu_b, v_q_norm_g, v_k_norm_g,
                                  v_w_out_conv, v_w_out_sgu, v_w_out_attn, v_w_o, v_ffn2_norm, v_ffn2_w_gu,
                                  v_ffn2_w_down)))
    px, py, pc = _place()
    dev = 4 * px + 2 * py + pc
    chip = 2 * px + py

    big_names = [n for n in WEIGHT_NAMES if n in BIG]
    key_name = {BIG[n]: n for n in big_names}
    group_keys = (("gu1", "d1"), ("in", "oc", "os", "oa", "o", "small"), ("gu2", "d2"))

    def source(key, l):
        if key == "small":
            return jnp.concatenate([w["b_gate"][l], w["conv_w"][l], jnp.zeros((2, 128), f32)], axis=0)
        return w[key_name[key]][l].astype(bf16)

    def landing(src):
        return lax.dynamic_update_slice(lax.empty((NDEV,) + src.shape, src.dtype), src[None], (dev, 0, 0))

    groups = [[(s, landing(s)) for s in (source(k, l) for k in keys)] for l in range(2) for keys in group_keys]
    started = _gather_start("gather_start", groups)

    def weights(l, part, after):
        got = _gather_wait(f"gather_wait_{l}_{part}", started[3 * l + part], after)
        return dict(zip(group_keys[part], _gather_forward(f"gather_forward_{l}_{part}", got)))

    xl = x[0]
    saved, small, wts = [], [], []
    for l in range(2):
        ga = weights(l, 0, xl if l else None)
        wt = {"gu1": ga["gu1"][None], "d1": ga["d1"].reshape(1, 4, GU, D)}
        x1, s1 = _ffn_fwd("ffn1", xl, w["ffn1_norm"][l].reshape(1, D), wt["gu1"], wt["d1"], 0)
        gb = weights(l, 1, x1)
        p = {n: w[n][l] for n in REPLICATED_SMALL}
        p["b_gate"] = jnp.transpose(gb["small"][:, 0:3, :], (1, 0, 2)).reshape(3, D)
        p["conv_w"] = jnp.transpose(gb["small"][:, 3:6, :], (1, 0, 2)).reshape(3, D)
        p = _small_params(p)
        wt["win"] = _win_kernel_layout(gb["in"])
        wt["sq"] = jnp.stack([gb[k].reshape(D, D) for k in ("oc", "os", "oa", "o")])[None]
        x2, sm = _mixer_fwd("mix", x1, p, wt["win"], wt["sq"], 0)
        gc = weights(l, 2, x2)
        wt.update({"gu2": gc["gu2"][None], "d2": gc["d2"].reshape(1, 4, GU, D)})
        x3, s2 = _ffn_fwd("ffn2", x2, p["ffn2_norm"], wt["gu2"], wt["d2"], 0)
        saved.append((xl, x1, x2, s1, sm, s2))
        small.append(p)
        wts.append(wt)
        xl = x3
    loss_row, dx = _loss("loss", xl, loss_target[0])

    core = pc.reshape(1).astype(jnp.int32)
    buf = lambda r, c: lax.empty((2, 4, 1, r, c), bf16)
    flights = {}

    def scatter(l, part, bufs):
        r1 = _rs_pair(f"rs_pair_{l}_{part}", bufs)
        ss = [_pair_sum(f"pair_sum_{l}_{k}", core, g, r) for k, g, r in zip(group_keys[part], bufs, r1)]
        flights[l, part] = _scatter_start(f"scatter_start_{l}_{part}", ss)

    sgrads = [None, None]
    for l in (1, 0):
        p, wt = small[l], wts[l]
        x0, x1, x2, s1, sm, s2 = saved[l]
        dx, dn2, g_gu2, g_d2 = _ffn_bwd("ffn2", dx, x2, p["ffn2_norm"], wt["gu2"], wt["d2"], 0, s2, buf(D, GU), buf(GU // 2, D))
        scatter(l, 2, [g_gu2, g_d2])
        dx, sg, dwin, gsq = _mixer_bwd("mix", dx, x1, p, wt["win"], wt["sq"], 0, sm, [buf(128, D) for _ in range(4)])
        g_in = jnp.stack([jnp.stack([_win_device_block(dwin, 2 * q + c)[None] for q in range(4)]) for c in range(2)])
        scatter(l, 1, [g_in] + gsq)
        dx, dn1, g_gu1, g_d1 = _ffn_bwd("ffn1", dx, x0, p["ffn1_norm"], wt["gu1"], wt["d1"], 0, s1, buf(D, GU), buf(GU // 2, D))
        scatter(l, 0, [g_gu1, g_d1])
        sg["ffn1_norm"] = dn1
        sg["ffn2_norm"] = dn2
        sgrads[l] = sg

    grads, delta, new_m, new_v = {}, {}, {}, {}
    after = dx
    for part in (2, 1, 0):
        sets = []
        for l in (1, 0):
            s_all, r2_all = _scatter_wait(f"scatter_wait_{l}_{part}", flights[l, part], after)
            sets.append([(lax.dynamic_index_in_dim(s, chip, 0, keepdims=False)[0], r) for s, r in zip(s_all, r2_all)])
        for i, k in enumerate(k for k in group_keys[part] if k != "small"):
            n = key_name[k]
            grads[n], delta[n], new_m[n], new_v[n] = _adamw("adamw_" + k, w[n], mom[n], var[n], [sets[1][i], sets[0][i]])
            after = delta[n]

    nat = [_small_grads_natural(sgrads[l]) for l in range(2)]
    order = REPLICATED_SMALL + SHARDED_SMALL
    part = _pack([jnp.stack([nat[0][n], nat[1][n]]) for n in order] + [loss_row[0, 0:1]])
    total = _all_reduce_small("reduce_small", part)
    full_shapes = [(2,) + tuple(nat[0][n].shape) for n in order] + [(1,)]
    summed = dict(zip(order + ("loss",), _unpack(total, full_shapes)))
    for n in REPLICATED_SMALL:
        grads[n] = summed[n]
    for n in SHARDED_SMALL:
        grads[n] = lax.dynamic_slice_in_dim(summed[n], dev * 128, 128, axis=2)
    wp = _pack([w[n] for n in order])
    gp = _pack([grads[n] for n in order])
    mp = _pack([mom[n] for n in order])
    vp = _pack([var[n] for n in order])
    dpk, mpk, vpk = _adamw_small("adamw_small", wp, gp, mp, vp)
    local_shapes = [tuple(w[n].shape) for n in order]
    for dst, packed in ((delta, dpk), (new_m, mpk), (new_v, vpk)):
        dst.update(zip(order, _unpack(packed, local_shapes)))

    loss = summed["loss"][0]
    return (loss, dx[None], *[grads[n] for n in WEIGHT_NAMES], *[delta[n] for n in WEIGHT_NAMES],
            *[new_m[n] for n in WEIGHT_NAMES], *[new_v[n] for n in WEIGHT_NAMES])
```

```python
import functools

import jax
import jax.numpy as jnp
from jax import lax
from jax.experimental import pallas as pl
from jax.experimental.pallas import tpu as pltpu

f32 = jnp.float32
bf16 = jnp.bfloat16
S = jax.ShapeDtypeStruct
MESH = pl.DeviceIdType.MESH

D = 1024
NH = 8
HD = 128
NDEV = 8
GU = 704
NIN = 11272
INB = 1409
GATE_OFF = 8192
F_OFF = 11264
NZ = 11776
RMS_EPS = 1e-6
LN_EPS = 1e-5
ATT_SCALE = HD ** -0.5
NEG = -1e30
INV_SQRT2 = 0.7071067811865476
INV_SQRT2PI = 0.3989422804014327

ADAM_LR = 0.001
ADAM_B1 = 0.9
ADAM_B2 = 0.999
ADAM_EPS = 1e-08
ADAM_WD = 0.01
ADAM_STEP = 10

TT = 256
VMEM_LIMIT = 56 * 1024 * 1024


def _cp(*sem):
    return pltpu.CompilerParams(dimension_semantics=sem, vmem_limit_bytes=VMEM_LIMIT)


def _bs(shape, fn):
    return pl.BlockSpec(shape, fn)


NN = (((1,), (0,)), ((), ()))
NT = (((1,), (1,)), ((), ()))
TN = (((0,), (0,)), ((), ()))


def _mm(name, a, b, *, grid, a_spec, b_spec, out_shape, out_spec, dims, acc_shape, res=None, res_spec=None,
        alpha=1.0, alias=None, split_rows=None):
    nk = grid[2]

    def body(*refs):
        a_ref, b_ref = refs[0], refs[1]
        pos = 2
        res_ref = None
        if res is not None:
            res_ref = refs[pos]
            pos += 1
        if alias is not None:
            pos += 1
        o_ref = refs[pos]
        acc_ref = refs[pos + 1] if nk > 1 else None
        part = lax.dot_general(a_ref[...].astype(bf16), b_ref[...].astype(bf16), dims, preferred_element_type=f32)

        def finish(acc):
            if alpha != 1.0:
                acc = alpha * acc
            if res_ref is not None:
                acc = res_ref[...] + acc
            if split_rows is None:
                o_ref[...] = acc.astype(o_ref.dtype)
            else:
                o_ref[0] = acc[:split_rows].astype(o_ref.dtype)
                o_ref[1] = acc[split_rows:].astype(o_ref.dtype)

        if nk == 1:
            finish(part)
        else:
            k = pl.program_id(2)

            @pl.when(k == 0)
            def _():
                acc_ref[...] = part

            @pl.when(k > 0)
            def _():
                acc_ref[...] += part

            @pl.when(k == nk - 1)
            def _():
                finish(acc_ref[...])

    operands = [a, b]
    in_specs = [a_spec, b_spec]
    if res is not None:
        operands.append(res)
        in_specs.append(res_spec)
    aliases = {}
    if alias is not None:
        aliases = {len(operands): 0}
        operands.append(alias)
        in_specs.append(pl.BlockSpec(memory_space=pl.ANY))
    return pl.pallas_call(
        body, name=name, grid=grid, in_specs=in_specs, out_specs=out_spec, out_shape=out_shape,
        scratch_shapes=[pltpu.VMEM(acc_shape, f32)] if nk > 1 else [],
        input_output_aliases=aliases,
        compiler_params=_cp("parallel", "parallel", "arbitrary"),
    )(*operands)


def _tile(n, t):
    return t if n % t == 0 and n >= t else n


def _row(cb=0, w=D):
    return _bs((TT, w), lambda i: (i, cb))


def _vec(rows=1, w=D):
    return _bs((rows, w), lambda i: (0, 0))


def _acc_store(i, ref, val):
    @pl.when(i == 0)
    def _():
        ref[...] = val

    @pl.when(i > 0)
    def _():
        ref[...] += val


def _rms_fwd(name, x, g):
    T = x.shape[0]

    def body(x_ref, g_ref, o_ref):
        xv = x_ref[...]
        r = lax.rsqrt(jnp.mean(xv * xv, axis=-1, keepdims=True) + RMS_EPS)
        o_ref[...] = (xv * r * g_ref[...]).astype(bf16)

    return pl.pallas_call(body, name=name, grid=(T // TT,), in_specs=[_row(), _vec()], out_specs=_row(),
                          out_shape=S((T, D), bf16), compiler_params=_cp("parallel"))(x, g)


def _rms_bwd(name, dh, x, g, dres):
    T = x.shape[0]

    def body(dh_ref, x_ref, g_ref, dres_ref, dx_ref, dg_ref):
        i = pl.program_id(0)
        xv = x_ref[...]
        r = lax.rsqrt(jnp.mean(xv * xv, axis=-1, keepdims=True) + RMS_EPS)
        xhat = xv * r
        dh_v = dh_ref[...]
        dyg = dh_v * g_ref[...]
        m = jnp.mean(dyg * xhat, axis=-1, keepdims=True)
        dx_ref[...] = dres_ref[...] + r * (dyg - xhat * m)
        _acc_store(i, dg_ref, jnp.sum(dh_v * xhat, axis=0, keepdims=True))

    return pl.pallas_call(body, name=name, grid=(T // TT,), in_specs=[_row(), _row(), _vec(), _row()],
                          out_specs=[_row(), _vec()], out_shape=[S((T, D), f32), S((1, D), f32)],
                          compiler_params=_cp("arbitrary"))(dh, x, g, dres)


def _sigmoid(x):
    return 1.0 / (1.0 + jnp.exp(-x))


def _swiglu_fwd(name, gu):
    T = gu.shape[1]

    def body(g_ref, u_ref, o_ref):
        g = g_ref[...]
        o_ref[...] = (g * _sigmoid(g) * u_ref[...]).astype(bf16)

    return pl.pallas_call(
        body, name=name, grid=(4, T // TT),
        in_specs=[_bs((None, TT, GU), lambda j, i: (j, i, 0)), _bs((None, TT, GU), lambda j, i: (j + 4, i, 0))],
        out_specs=_bs((None, TT, GU), lambda j, i: (j, i, 0)), out_shape=S((4, T, GU), bf16),
        compiler_params=_cp("parallel", "parallel"))(gu, gu)


def _swiglu_bwd(name, gu, da):
    T = gu.shape[1]

    def body(g_ref, u_ref, da_ref, o_ref):
        g = g_ref[...]
        u = u_ref[...]
        da_v = da_ref[...]
        sg = _sigmoid(g)
        o_ref[0] = (da_v * u * (sg + g * sg * (1.0 - sg))).astype(bf16)
        o_ref[1] = (da_v * g * sg).astype(bf16)

    return pl.pallas_call(
        body, name=name, grid=(4, T // TT),
        in_specs=[_bs((None, TT, GU), lambda j, i: (j, i, 0)), _bs((None, TT, GU), lambda j, i: (j + 4, i, 0)),
                  _bs((None, TT, GU), lambda j, i: (j, i, 0))],
        out_specs=_bs((2, None, TT, GU), lambda j, i: (0, j, i, 0)), out_shape=S((2, 4, T, GU), bf16),
        compiler_params=_cp("parallel", "parallel"))(gu, gu, da)


def _loss(name, y, tgt):
    T = y.shape[0]

    def body(y_ref, t_ref, l_ref, dy_ref):
        i = pl.program_id(0)
        e = y_ref[...] - t_ref[...]
        dy_ref[...] = e * (1.0 / D)
        s = 0.5 * jnp.sum(jnp.mean(e * e, axis=-1, keepdims=True))
        _acc_store(i, l_ref, jnp.broadcast_to(s, (1, 128)))

    return pl.pallas_call(body, name=name, grid=(T // TT,), in_specs=[_row(), _row()],
                          out_specs=[_vec(1, 128), _row()], out_shape=[S((1, 128), f32), S((T, D), f32)],
                          compiler_params=_cp("arbitrary"))(y, tgt)


def _prev8(T, cb):
    return _bs((8, D), lambda i: (jnp.maximum(i * (TT // 8) - 1, 0), cb))


def _next8(T, cb):
    return _bs((8, D), lambda i: (jnp.minimum((i + 1) * (TT // 8), T // 8 - 1), cb))


def _conv_taps(i, ac_ref, ax_ref, pc_ref, px_ref):
    ca = ac_ref[...] * ax_ref[...]
    keep = (i > 0).astype(f32)
    p1 = pc_ref[7:8, :] * px_ref[7:8, :] * keep
    p2 = pc_ref[6:7, :] * px_ref[6:7, :] * keep
    row = lax.broadcasted_iota(jnp.int32, ca.shape, 0)
    s1 = jnp.where(row == 0, p1, pltpu.roll(ca, 1, 0))
    s2 = jnp.where(row == 0, p2, jnp.where(row == 1, p1, pltpu.roll(ca, 2, 0)))
    return ca, s1, s2


def _conv_fwd(name, z, cw):
    T = z.shape[0]

    def body(ab_ref, ac_ref, ax_ref, pc_ref, px_ref, w_ref, o_ref):
        i = pl.program_id(0)
        ca, s1, s2 = _conv_taps(i, ac_ref, ax_ref, pc_ref, px_ref)
        cv = w_ref[0:1, :] * s2 + w_ref[1:2, :] * s1 + w_ref[2:3, :] * ca
        o_ref[...] = (ab_ref[...] * cv).astype(bf16)

    return pl.pallas_call(
        body, name=name, grid=(T // TT,),
        in_specs=[_row(0), _row(1), _row(2), _prev8(T, 1), _prev8(T, 2), _vec(3)],
        out_specs=_row(), out_shape=S((T, D), bf16), compiler_params=_cp("parallel"))(z, z, z, z, z, cw)


def _conv_bwd(name, dya, z, cw):
    T = z.shape[0]
    n = T // TT

    def body(dya_ref, ab_ref, ac_ref, ax_ref, pc_ref, px_ref, ndya_ref, nab_ref, w_ref, dz_ref, dw_ref):
        i = pl.program_id(0)
        ca, s1, s2 = _conv_taps(i, ac_ref, ax_ref, pc_ref, px_ref)
        w0, w1, w2 = w_ref[0:1, :], w_ref[1:2, :], w_ref[2:3, :]
        cv = w0 * s2 + w1 * s1 + w2 * ca
        dya_v = dya_ref[...]
        ab = ab_ref[...]
        dcv = dya_v * ab
        keep = (i < n - 1).astype(f32)
        n1 = ndya_ref[0:1, :] * nab_ref[0:1, :] * keep
        n2 = ndya_ref[1:2, :] * nab_ref[1:2, :] * keep
        row = lax.broadcasted_iota(jnp.int32, dcv.shape, 0)
        f1 = jnp.where(row == TT - 1, n1, pltpu.roll(dcv, TT - 1, 0))
        f2 = jnp.where(row == TT - 1, n2, jnp.where(row == TT - 2, n1, pltpu.roll(dcv, TT - 2, 0)))
        dca = w2 * dcv + w1 * f1 + w0 * f2
        dz_ref[:, 0:D] = (dya_v * cv).astype(bf16)
        dz_ref[:, D:2 * D] = (dca * ax_ref[...]).astype(bf16)
        dz_ref[:, 2 * D:3 * D] = (dca * ac_ref[...]).astype(bf16)
        dw = jnp.concatenate([jnp.sum(dcv * s2, axis=0, keepdims=True), jnp.sum(dcv * s1, axis=0, keepdims=True),
                              jnp.sum(dcv * ca, axis=0, keepdims=True)], axis=0)
        _acc_store(i, dw_ref, dw)

    return pl.pallas_call(
        body, name=name, grid=(n,),
        in_specs=[_row(), _row(0), _row(1), _row(2), _prev8(T, 1), _prev8(T, 2), _next8(T, 0), _next8(T, 0), _vec(3)],
        out_specs=[_row(0, 3 * D), _vec(3)], out_shape=[S((T, 3 * D), bf16), S((3, D), f32)],
        compiler_params=_cp("arbitrary"))(dya, z, z, z, z, z, dya, z, cw)


def _gelu(x):
    return 0.5 * x * (1.0 + lax.erf(x * INV_SQRT2))


def _gelu_grad(x):
    return 0.5 * (1.0 + lax.erf(x * INV_SQRT2)) + x * jnp.exp(-0.5 * x * x) * INV_SQRT2PI


def _ln_stats(vv):
    mu = jnp.mean(vv, axis=-1, keepdims=True)
    xc = vv - mu
    rstd = lax.rsqrt(jnp.mean(xc * xc, axis=-1, keepdims=True) + LN_EPS)
    return xc * rstd, rstd


def _tril_w(w_ref, g):
    r = lax.broadcasted_iota(jnp.int32, (HD, HD), 0)
    c = lax.broadcasted_iota(jnp.int32, (HD, HD), 1)
    return jnp.where(c <= r, w_ref[g], 0.0).astype(bf16)


def _sgu_fwd(name, z, ln_g, ln_b, w_s, bmap):
    T = z.shape[0]

    def body(su_ref, sv_ref, lg_ref, lb_ref, w_ref, bm_ref, o_ref, vn_ref):
        xhat, _ = _ln_stats(_gelu(sv_ref[...]))
        vn_ref[...] = (xhat * lg_ref[...] + lb_ref[...]).astype(bf16)
        for g in range(NH):
            w = _tril_w(w_ref, g)
            cs = slice(g * HD, (g + 1) * HD)
            for c in range(TT // HD):
                rs = slice(c * HD, (c + 1) * HD)
                s = jnp.dot(w, vn_ref[rs, cs], preferred_element_type=f32) + bm_ref[:, cs]
                o_ref[rs, cs] = (_gelu(su_ref[rs, cs]) * s).astype(bf16)

    return pl.pallas_call(
        body, name=name, grid=(T // TT,),
        in_specs=[_row(3), _row(4), _vec(), _vec(), _bs((NH, HD, HD), lambda i: (0, 0, 0)), _vec(HD)],
        out_specs=_row(), out_shape=S((T, D), bf16), scratch_shapes=[pltpu.VMEM((TT, D), bf16)],
        compiler_params=_cp("parallel"))(z, z, ln_g, ln_b, w_s, bmap)


def _sgu_bwd(name, dyb, z, ln_g, ln_b, w_s, bmap):
    T = z.shape[0]

    def body(dyb_ref, su_ref, sv_ref, lg_ref, lb_ref, w_ref, bm_ref, dz_ref, dlg_ref, dlb_ref, dw_ref, db_ref,
             vn_ref, du_ref, dvn_ref):
        i = pl.program_id(0)
        sv = sv_ref[...]
        xhat, rstd = _ln_stats(_gelu(sv))
        vn_ref[...] = (xhat * lg_ref[...] + lb_ref[...]).astype(bf16)
        r = lax.broadcasted_iota(jnp.int32, (HD, HD), 0)
        cc = lax.broadcasted_iota(jnp.int32, (HD, HD), 1)
        for g in range(NH):
            w = _tril_w(w_ref, g)
            cs = slice(g * HD, (g + 1) * HD)
            dw = jnp.zeros((HD, HD), f32)
            db = jnp.zeros((HD, 1), f32)
            for c in range(TT // HD):
                rs = slice(c * HD, (c + 1) * HD)
                vnb = vn_ref[rs, cs]
                s = jnp.dot(w, vnb, preferred_element_type=f32) + bm_ref[:, cs]
                dy = dyb_ref[rs, cs]
                du_ref[rs, cs] = dy * s
                ds = dy * _gelu(su_ref[rs, cs])
                ds16 = ds.astype(bf16)
                dvn_ref[rs, cs] = lax.dot_general(w, ds16, TN, preferred_element_type=f32)
                dw = dw + lax.dot_general(ds16, vnb, NT, preferred_element_type=f32)
                db = db + jnp.sum(ds, axis=1, keepdims=True)
            dw = jnp.where(cc <= r, dw, 0.0)

            @pl.when(i == 0)
            def _():
                dw_ref[g] = dw
                db_ref[:, g:g + 1] = db

            @pl.when(i > 0)
            def _():
                dw_ref[g] += dw
                db_ref[:, g:g + 1] += db

        dvn = dvn_ref[...]
        dxh = dvn * lg_ref[...]
        m1 = jnp.mean(dxh, axis=-1, keepdims=True)
        m2 = jnp.mean(dxh * xhat, axis=-1, keepdims=True)
        dvv = rstd * (dxh - m1 - xhat * m2)
        dz_ref[:, 0:D] = (du_ref[...] * _gelu_grad(su_ref[...])).astype(bf16)
        dz_ref[:, D:2 * D] = (dvv * _gelu_grad(sv)).astype(bf16)
        _acc_store(i, dlg_ref, jnp.sum(dvn * xhat, axis=0, keepdims=True))
        _acc_store(i, dlb_ref, jnp.sum(dvn, axis=0, keepdims=True))

    return pl.pallas_call(
        body, name=name, grid=(T // TT,),
        in_specs=[_row(), _row(3), _row(4), _vec(), _vec(), _bs((NH, HD, HD), lambda i: (0, 0, 0)), _vec(HD)],
        out_specs=[_row(0, 2 * D), _vec(), _vec(), _bs((NH, HD, HD), lambda i: (0, 0, 0)), _bs((HD, NH), lambda i: (0, 0))],
        out_shape=[S((T, 2 * D), bf16), S((1, D), f32), S((1, D), f32), S((NH, HD, HD), f32), S((HD, NH), f32)],
        scratch_shapes=[pltpu.VMEM((TT, D), bf16), pltpu.VMEM((TT, D), f32), pltpu.VMEM((TT, D), f32)],
        compiler_params=_cp("arbitrary"))(dyb, z, z, ln_g, ln_b, w_s, bmap)


def _qk_fwd(name, z, qg, kg, bf):
    T = z.shape[0]

    def body(q_ref, k_ref, v_ref, zf_ref, qg_ref, kg_ref, bf_ref, qn_ref, kn_ref, vb_ref, lf_ref):
        for h in range(NH):
            cs = slice(h * HD, (h + 1) * HD)
            for src, gain, dst in ((q_ref, qg_ref, qn_ref), (k_ref, kg_ref, kn_ref)):
                xv = src[:, cs]
                r = lax.rsqrt(jnp.mean(xv * xv, axis=-1, keepdims=True) + RMS_EPS)
                dst[:, cs] = (xv * r * gain[:, cs]).astype(bf16)
        vb_ref[...] = v_ref[...].astype(bf16)
        xf = zf_ref[...] + bf_ref[...]
        lf_ref[...] = jnp.minimum(xf, 0.0) - jnp.log1p(jnp.exp(-jnp.abs(xf)))

    return pl.pallas_call(
        body, name=name, grid=(T // TT,),
        in_specs=[_row(5), _row(6), _row(7), _bs((TT, 128), lambda i: (i, F_OFF // 128)), _vec(), _vec(), _vec(1, 128)],
        out_specs=[_row(), _row(), _row(), _bs((TT, 128), lambda i: (i, 0))],
        out_shape=[S((T, D), bf16), S((T, D), bf16), S((T, D), bf16), S((T, 128), f32)],
        compiler_params=_cp("parallel"))(z, z, z, z, qg, kg, bf)


def _cum_fwd(name, logf):
    T = logf.shape[0]

    def body(lf_ref, ccol_ref, crow_ref, c_ref):
        c = lf_ref[...]
        row = lax.broadcasted_iota(jnp.int32, c.shape, 0)
        s = 1
        while s < T:
            c = c + jnp.where(row >= s, pltpu.roll(c, s, 0), 0.0)
            s *= 2
        c_ref[...] = c
        crow_ref[...] = c.T[0:NH, :]
        for h in range(NH):
            ccol_ref[h] = jnp.broadcast_to(c_ref[:, h:h + 1], (T, 128))

    return pl.pallas_call(body, name=name, out_shape=[S((NH, T, 128), f32), S((NH, T), f32)],
                          scratch_shapes=[pltpu.VMEM((T, 128), f32)],
                          compiler_params=pltpu.CompilerParams(vmem_limit_bytes=VMEM_LIMIT))(logf)


ATT_TILE = 512


def _fold(x, op=jnp.add):
    acc = x[:, 0:128]
    for t in range(1, x.shape[1] // 128):
        acc = op(acc, x[:, t * 128:(t + 1) * 128])
    return acc


def _to_row(col):
    return jnp.broadcast_to(col, (col.shape[0], 128)).T[0:1, :]


def _causal(t, keys_down=False):
    r = lax.broadcasted_iota(jnp.int32, (t, t), 0)
    c = lax.broadcasted_iota(jnp.int32, (t, t), 1)
    return r <= c if keys_down else c <= r


def _attn_fwd(name, qn, kn, vb, ccol, crow3):
    T = qn.shape[0]
    tq = _tile(T, ATT_TILE)
    nq = T // tq

    def body(q_ref, k_ref, v_ref, cc_ref, cr_ref, o_ref, lse_ref, lser_ref, s_ref):
        qi = pl.program_id(1)
        q = q_ref[...]
        cq = cc_ref[:, 0:1]

        def logits(off):
            s = lax.dot_general(q, k_ref[pl.ds(off, tq), :], NT, preferred_element_type=f32) * ATT_SCALE
            return s + cq - cr_ref[:, pl.ds(off, tq)]

        def below(j, mvec):
            off = pl.multiple_of(j * tq, tq)
            s = logits(off)
            s_ref[:, pl.ds(off, tq)] = s
            return jnp.maximum(mvec, _fold(s, jnp.maximum))

        mvec = lax.fori_loop(0, qi, below, jnp.full((tq, 128), NEG, f32))
        off = pl.multiple_of(qi * tq, tq)
        s = jnp.where(_causal(tq), logits(off), NEG)
        s_ref[:, pl.ds(off, tq)] = s
        m = jnp.max(jnp.maximum(mvec, _fold(s, jnp.maximum)), axis=1, keepdims=True)

        def weigh(j, carry):
            lvec, acc = carry
            off = pl.multiple_of(j * tq, tq)
            p = jnp.exp(s_ref[:, pl.ds(off, tq)] - m)
            acc = acc + jnp.dot(p.astype(bf16), v_ref[pl.ds(off, tq), :], preferred_element_type=f32)
            return lvec + _fold(p), acc

        lvec, acc = lax.fori_loop(0, qi + 1, weigh, (jnp.zeros((tq, 128), f32), jnp.zeros((tq, HD), f32)))
        l = jnp.sum(lvec, axis=1, keepdims=True)
        o_ref[...] = acc / l
        lse = m + jnp.log(l)
        lse_ref[...] = jnp.broadcast_to(lse, (tq, 128))
        lser_ref[...] = _to_row(lse)

    return pl.pallas_call(
        body, name=name, grid=(NH, nq),
        in_specs=[_bs((tq, HD), lambda h, i: (i, h)), _bs((T, HD), lambda h, i: (0, h)), _bs((T, HD), lambda h, i: (0, h)),
                  _bs((None, tq, 128), lambda h, i: (h, i, 0)), _bs((None, 1, T), lambda h, i: (h, 0, 0))],
        out_specs=[_bs((tq, HD), lambda h, i: (i, h)), _bs((None, tq, 128), lambda h, i: (h, i, 0)),
                   _bs((None, 1, tq), lambda h, i: (h, 0, i))],
        out_shape=[S((T, D), f32), S((NH, T, 128), f32), S((NH, 1, T), f32)],
        scratch_shapes=[pltpu.VMEM((tq, T), f32)],
        compiler_params=_cp("parallel", "parallel"))(qn, kn, vb, ccol, crow3)


def _attn_dq(name, qn, kn, vb, do, lse, ccol, crow3):
    T = qn.shape[0]
    tq = _tile(T, ATT_TILE)
    nq = T // tq

    def body(q_ref, k_ref, v_ref, do_ref, lse_ref, cc_ref, cr_ref, dq_ref, dlr_ref, p_ref, dp_ref):
        qi = pl.program_id(1)
        q = q_ref[...]
        do16 = do_ref[...].astype(bf16)
        base = cc_ref[:, 0:1] - lse_ref[:, 0:1]

        def probs(off):
            s = lax.dot_general(q, k_ref[pl.ds(off, tq), :], NT, preferred_element_type=f32) * ATT_SCALE
            return jnp.exp(s + base - cr_ref[:, pl.ds(off, tq)])

        def keep(off, p, dvec):
            dp = lax.dot_general(do16, v_ref[pl.ds(off, tq), :], NT, preferred_element_type=f32)
            p_ref[:, pl.ds(off, tq)] = p
            dp_ref[:, pl.ds(off, tq)] = dp
            return dvec + _fold(p * dp)

        def below(j, dvec):
            off = pl.multiple_of(j * tq, tq)
            return keep(off, probs(off), dvec)

        dvec = lax.fori_loop(0, qi, below, jnp.zeros((tq, 128), f32))
        off = pl.multiple_of(qi * tq, tq)
        dvec = keep(off, jnp.where(_causal(tq), probs(off), 0.0), dvec)
        delta = jnp.sum(dvec, axis=1, keepdims=True)

        def grad(j, acc):
            off = pl.multiple_of(j * tq, tq)
            ds = p_ref[:, pl.ds(off, tq)] * (dp_ref[:, pl.ds(off, tq)] - delta)
            return acc + jnp.dot(ds.astype(bf16), k_ref[pl.ds(off, tq), :], preferred_element_type=f32)

        dq_ref[...] = lax.fori_loop(0, qi + 1, grad, jnp.zeros((tq, HD), f32)) * ATT_SCALE
        dlr_ref[...] = _to_row(delta)

    qb = lambda h, i: (i, h)
    full = lambda h, i: (0, h)
    col = lambda h, i: (h, i, 0)
    return pl.pallas_call(
        body, name=name, grid=(NH, nq),
        in_specs=[_bs((tq, HD), qb), _bs((T, HD), full), _bs((T, HD), full), _bs((tq, HD), qb),
                  _bs((None, tq, 128), col), _bs((None, tq, 128), col), _bs((None, 1, T), lambda h, i: (h, 0, 0))],
        out_specs=[_bs((tq, HD), qb), _bs((None, 1, tq), lambda h, i: (h, 0, i))],
        out_shape=[S((T, D), f32), S((NH, 1, T), f32)],
        scratch_shapes=[pltpu.VMEM((tq, T), f32), pltpu.VMEM((tq, T), f32)],
        compiler_params=_cp("parallel", "parallel"))(qn, kn, vb, do, lse, ccol, crow3)


def _attn_dkv(name, qn, kn, vb, do, lser3, dlr3, ccol, crow3):
    T = qn.shape[0]
    tk = _tile(T, ATT_TILE)
    nk = T // tk

    def body(q_ref, k_ref, v_ref, do_ref, lser_ref, dlr_ref, cc_ref, cr_ref, dk_ref, dv_ref, cs_ref):
        h = pl.program_id(0)
        kj = pl.program_id(1)

        @pl.when((h == 0) & (kj == 0))
        def _():
            cs_ref[...] = jnp.zeros_like(cs_ref)

        kb = k_ref[...]
        vv = v_ref[...]
        ckey = cc_ref[:, 0:1]

        def block(off, diagonal):
            rows = pl.ds(off, tk)
            qb = q_ref[rows, :]
            do16 = do_ref[rows, :].astype(bf16)
            st = lax.dot_general(kb, qb, NT, preferred_element_type=f32) * ATT_SCALE
            pt = jnp.exp(st + (cr_ref[:, rows] - lser_ref[:, rows]) - ckey)
            if diagonal:
                pt = jnp.where(_causal(tk, keys_down=True), pt, 0.0)
            dpt = lax.dot_general(vv, do16, NT, preferred_element_type=f32)
            dst = pt * (dpt - dlr_ref[:, rows])
            ddv = jnp.dot(pt.astype(bf16), do16, preferred_element_type=f32)
            ddk = jnp.dot(dst.astype(bf16), qb, preferred_element_type=f32)
            return ddk, ddv, _fold(dst)

        def above(i, carry):
            ddk, ddv, dcs = block(pl.multiple_of(i * tk, tk), False)
            return carry[0] + ddk, carry[1] + ddv, carry[2] + dcs

        off = pl.multiple_of(kj * tk, tk)
        dk, dv, cs = lax.fori_loop(kj + 1, nk, above, block(off, True))
        dk_ref[...] = dk * ATT_SCALE
        dv_ref[...] = dv
        lane = lax.broadcasted_iota(jnp.int32, (tk, 128), 1)
        cs_ref[pl.ds(off, tk), :] += jnp.where(lane == h, jnp.sum(cs, axis=1, keepdims=True), 0.0)

    full = lambda h, j: (0, h)
    blk = lambda h, j: (j, h)
    row = lambda h, j: (h, 0, 0)
    return pl.pallas_call(
        body, name=name, grid=(NH, nk),
        in_specs=[_bs((T, HD), full), _bs((tk, HD), blk), _bs((tk, HD), blk), _bs((T, HD), full), _bs((None, 1, T), row),
                  _bs((None, 1, T), row), _bs((None, tk, 128), lambda h, j: (h, j, 0)), _bs((None, 1, T), row)],
        out_specs=[_bs((tk, HD), blk), _bs((tk, HD), blk), _bs((T, 128), lambda h, j: (0, 0))],
        out_shape=[S((T, D), f32), S((T, D), f32), S((T, 128), f32)],
        compiler_params=_cp("arbitrary", "arbitrary"))(qn, kn, vb, do, lser3, dlr3, ccol, crow3)


def _forget_bwd(name, cs, z, bf):
    T = cs.shape[0]

    def body(cs_ref, zf_ref, bf_ref, dz_ref, db_ref):
        c = -cs_ref[...]
        row = lax.broadcasted_iota(jnp.int32, c.shape, 0)
        s = 1
        while s < T:
            c = c + jnp.where(row + s < T, pltpu.roll(c, T - s, 0), 0.0)
            s *= 2
        xf = zf_ref[...] + bf_ref[...]
        lane = lax.broadcasted_iota(jnp.int32, c.shape, 1)
        dxf = jnp.where(lane < NH, c / (1.0 + jnp.exp(xf)), 0.0)
        dz_ref[...] = jnp.zeros_like(dz_ref)
        dz_ref[:, 0:128] = dxf.astype(bf16)
        db_ref[...] = jnp.sum(dxf, axis=0, keepdims=True)

    return pl.pallas_call(
        body, name=name, grid=(1,),
        in_specs=[_bs((T, 128), lambda i: (0, 0)), _bs((T, 128), lambda i: (0, F_OFF // 128)), _vec(1, 128)],
        out_specs=[_bs((T, NZ - F_OFF), lambda i: (0, 0)), _vec(1, 128)],
        out_shape=[S((T, NZ - F_OFF), bf16), S((1, 128), f32)], compiler_params=_cp("arbitrary"))(cs, z, bf)


def _qk_bwd(name, dqn, dkn, dv, z, qg, kg):
    T = z.shape[0]

    def body(dq_ref, dk_ref, dv_ref, q_ref, k_ref, qg_ref, kg_ref, dz_ref, dqg_ref, dkg_ref, g_ref):
        i = pl.program_id(0)
        for n, (src, dsrc, gain, dgain) in enumerate(((q_ref, dq_ref, qg_ref, dqg_ref), (k_ref, dk_ref, kg_ref, dkg_ref))):
            for h in range(NH):
                cs = slice(h * HD, (h + 1) * HD)
                xv = src[:, cs]
                r = lax.rsqrt(jnp.mean(xv * xv, axis=-1, keepdims=True) + RMS_EPS)
                xhat = xv * r
                dy = dsrc[:, cs]
                dyg = dy * gain[:, cs]
                m = jnp.mean(dyg * xhat, axis=-1, keepdims=True)
                dz_ref[:, n * D + h * HD:n * D + (h + 1) * HD] = (r * (dyg - xhat * m)).astype(bf16)
                g_ref[:, cs] = jnp.sum(dy * xhat, axis=0, keepdims=True)
            _acc_store(i, dgain, g_ref[...])
        dz_ref[:, 2 * D:3 * D] = dv_ref[...].astype(bf16)

    return pl.pallas_call(
        body, name=name, grid=(T // TT,),
        in_specs=[_row(), _row(), _row(), _row(5), _row(6), _vec(), _vec()],
        out_specs=[_row(0, 3 * D), _vec(), _vec()], out_shape=[S((T, 3 * D), bf16), S((1, D), f32), S((1, D), f32)],
        scratch_shapes=[pltpu.VMEM((1, D), f32)], compiler_params=_cp("arbitrary"))(dqn, dkn, dv, z, z, qg, kg)


GB = GATE_OFF // D


def _merge_fwd(name, ya, yb, yc, z, bg):
    T = z.shape[0]

    def body(ya_ref, yb_ref, yc_ref, g0_ref, g1_ref, g2_ref, bg_ref, o_ref):
        acc = _sigmoid(g0_ref[...] + bg_ref[0:1, :]) * ya_ref[...]
        acc = acc + _sigmoid(g1_ref[...] + bg_ref[1:2, :]) * yb_ref[...]
        acc = acc + _sigmoid(g2_ref[...] + bg_ref[2:3, :]) * yc_ref[...]
        o_ref[...] = acc.astype(bf16)

    return pl.pallas_call(
        body, name=name, grid=(T // TT,),
        in_specs=[_row(), _row(), _row(), _row(GB), _row(GB + 1), _row(GB + 2), _vec(3)],
        out_specs=_row(), out_shape=S((T, D), bf16), compiler_params=_cp("parallel"))(ya, yb, yc, z, z, z, bg)


def _merge_bwd(name, dm, ya, yb, yc, z, bg):
    T = z.shape[0]

    def body(dm_ref, ya_ref, yb_ref, yc_ref, g0_ref, g1_ref, g2_ref, bg_ref, dya_ref, dyb_ref, dyc_ref, dz_ref, db_ref):
        i = pl.program_id(0)
        dm_v = dm_ref[...]
        dbs = []
        for n, (y_ref, g_ref, dy_ref) in enumerate(((ya_ref, g0_ref, dya_ref), (yb_ref, g1_ref, dyb_ref),
                                                    (yc_ref, g2_ref, dyc_ref))):
            gate = _sigmoid(g_ref[...] + bg_ref[n:n + 1, :])
            dy_ref[...] = (dm_v * gate).astype(bf16)
            dl = dm_v * y_ref[...] * gate * (1.0 - gate)
            dz_ref[:, n * D:(n + 1) * D] = dl.astype(bf16)
            dbs.append(jnp.sum(dl, axis=0, keepdims=True))
        _acc_store(i, db_ref, jnp.concatenate(dbs, axis=0))

    return pl.pallas_call(
        body, name=name, grid=(T // TT,),
        in_specs=[_row(), _row(), _row(), _row(), _row(GB), _row(GB + 1), _row(GB + 2), _vec(3)],
        out_specs=[_row(), _row(), _row(), _row(0, 3 * D), _vec(3)],
        out_shape=[S((T, D), bf16), S((T, D), bf16), S((T, D), bf16), S((T, 3 * D), bf16), S((3, D), f32)],
        compiler_params=_cp("arbitrary"))(dm, ya, yb, yc, z, z, z, bg)


SMALL_NAMES = ("ffn1_norm", "mix_norm", "b_forget", "b_gate", "conv_w", "sgu_ln_g", "sgu_ln_b", "sgu_w", "sgu_b",
               "q_norm_g", "k_norm_g", "ffn2_norm")


def _small_params(p):
    out = {n: p[n].reshape(1, D) for n in ("ffn1_norm", "mix_norm", "ffn2_norm", "sgu_ln_g", "sgu_ln_b", "q_norm_g", "k_norm_g")}
    out["b_forget"] = jnp.pad(p["b_forget"].reshape(1, NH), ((0, 0), (0, 128 - NH)))
    out["b_gate"] = p["b_gate"]
    out["conv_w"] = p["conv_w"]
    out["sgu_w"] = p["sgu_w"]
    out["bmap"] = jnp.repeat(p["sgu_b"].T, HD, axis=1)
    return out


def _small_grads_natural(sg):
    out = {n: sg[n].reshape(D) for n in ("ffn1_norm", "mix_norm", "ffn2_norm", "sgu_ln_g", "sgu_ln_b")}
    out["q_norm_g"] = sg["q_norm_g"].reshape(NH, HD)
    out["k_norm_g"] = sg["k_norm_g"].reshape(NH, HD)
    out["b_forget"] = sg["b_forget"][0, :NH]
    out["b_gate"] = sg["b_gate"]
    out["conv_w"] = sg["conv_w"]
    out["sgu_w"] = sg["sgu_w"]
    out["sgu_b"] = sg["sgu_b"]
    return out


def _sq_fwd(name, a, wsq, l, n, res=None):
    T = a.shape[0]
    tm = _tile(T, 512)
    return _mm(name, a, wsq, grid=(T // tm, 1, 1), a_spec=_bs((tm, D), lambda i, j, k: (i, 0)),
               b_spec=_bs((None, None, D, D), lambda i, j, k: (l, n, 0, 0)),
               out_shape=S((T, D), f32), out_spec=_bs((tm, D), lambda i, j, k: (i, 0)), dims=NN, acc_shape=None,
               res=res, res_spec=_bs((tm, D), lambda i, j, k: (i, 0)))


def _sq_bwd_in(name, dy, wsq, l, n):
    T = dy.shape[0]
    tm = _tile(T, 512)
    return _mm(name, dy, wsq, grid=(T // tm, 1, 1), a_spec=_bs((tm, D), lambda i, j, k: (i, 0)),
               b_spec=_bs((None, None, D, D), lambda i, j, k: (l, n, 0, 0)),
               out_shape=S((T, D), f32), out_spec=_bs((tm, D), lambda i, j, k: (i, 0)), dims=NT, acc_shape=None)


def _sq_bwd_w(name, a, dy, gbuf, l):
    T = a.shape[0]
    return _mm(name, a, dy, grid=(NDEV, 1, 1), a_spec=_bs((T, 128), lambda i, j, k: (0, i)),
               b_spec=_bs((T, D), lambda i, j, k: (0, 0)), out_shape=S(gbuf.shape, bf16),
               out_spec=_bs((None, None, None, 128, D), lambda i, j, k: (i % 2, i // 2, l, 0, 0)),
               dims=TN, acc_shape=None, alias=gbuf)


def _ffn_fwd(tag, x, g, wgu, wd, l):
    T = x.shape[0]
    tm = _tile(T, 1024)
    h = _rms_fwd(tag + "_rms", x, g)
    gu = _mm(tag + "_gu", h, wgu, grid=(T // tm, NDEV, 1), a_spec=_bs((tm, D), lambda i, j, k: (i, 0)),
             b_spec=_bs((None, None, D, GU), lambda i, j, k: (l, j, 0, 0)), out_shape=S((NDEV, T, GU), f32),
             out_spec=_bs((None, tm, GU), lambda i, j, k: (j, i, 0)), dims=NN, acc_shape=None)
    a = _swiglu_fwd(tag + "_act", gu)
    tm = _tile(T, 1024)
    xo = _mm(tag + "_down", a, wd, grid=(T // tm, 1, 4), a_spec=_bs((None, tm, GU), lambda i, j, k: (k, i, 0)),
             b_spec=_bs((None, None, GU, D), lambda i, j, k: (l, k, 0, 0)), out_shape=S((T, D), f32),
             out_spec=_bs((tm, D), lambda i, j, k: (i, 0)), dims=NN, acc_shape=(tm, D), res=x,
             res_spec=_bs((tm, D), lambda i, j, k: (i, 0)), alpha=0.5)
    return xo, (h, gu, a)


def _ffn_bwd(tag, dxo, x, g, wgu, wd, l, saved, g_gu, g_d):
    h, gu, a = saved
    T = x.shape[0]
    tm = _tile(T, 512)
    da = _mm(tag + "_dact", dxo, wd, grid=(T // tm, 4, 1), a_spec=_bs((tm, D), lambda i, j, k: (i, 0)),
             b_spec=_bs((None, None, GU, D), lambda i, j, k: (l, j, 0, 0)), out_shape=S((4, T, GU), f32),
             out_spec=_bs((None, tm, GU), lambda i, j, k: (j, i, 0)), dims=NT, acc_shape=None, alpha=0.5)
    g_d = _mm(tag + "_dwd", a, dxo, grid=(4, 1, 1), a_spec=_bs((None, T, GU), lambda i, j, k: (i, 0, 0)),
              b_spec=_bs((T, D), lambda i, j, k: (0, 0)), out_shape=S(g_d.shape, bf16),
              out_spec=_bs((2, None, None, GU // 2, D), lambda i, j, k: (0, i, l, 0, 0)), dims=TN, acc_shape=None,
              alpha=0.5, alias=g_d, split_rows=GU // 2)
    dgu = _swiglu_bwd(tag + "_dgu", gu, da).reshape(NDEV, T, GU)
    dh = _mm(tag + "_dh", dgu, wgu, grid=(1, 1, NDEV), a_spec=_bs((None, T, GU), lambda i, j, k: (k, 0, 0)),
             b_spec=_bs((None, None, D, GU), lambda i, j, k: (l, k, 0, 0)), out_shape=S((T, D), f32),
             out_spec=_bs((T, D), lambda i, j, k: (0, 0)), dims=NT, acc_shape=(T, D))
    g_gu = _mm(tag + "_dwgu", h, dgu, grid=(1, NDEV, 1), a_spec=_bs((T, D), lambda i, j, k: (0, 0)),
               b_spec=_bs((None, T, GU), lambda i, j, k: (j, 0, 0)), out_shape=S(g_gu.shape, bf16),
               out_spec=_bs((None, None, None, D, GU), lambda i, j, k: (j % 2, j // 2, l, 0, 0)), dims=TN,
               acc_shape=None, alias=g_gu)
    dx, dg = _rms_bwd(tag + "_drms", dh, x, g, dxo)
    return dx, dg, g_gu, g_d


def _mixer_fwd(tag, x, p, win, wsq, l):
    T = x.shape[0]
    h = _rms_fwd(tag + "_rms", x, p["mix_norm"])
    tn = 512
    z = _mm(tag + "_in", h, win, grid=(1, NZ // tn, 1), a_spec=_bs((T, D), lambda i, j, k: (0, 0)),
            b_spec=_bs((D, tn), lambda i, j, k: (0, j)), out_shape=S((T, NZ), f32),
            out_spec=_bs((T, tn), lambda i, j, k: (0, j)), dims=NN, acc_shape=None)
    ya_in = _conv_fwd(tag + "_conv", z, p["conv_w"])
    yb_in = _sgu_fwd(tag + "_sgu", z, p["sgu_ln_g"], p["sgu_ln_b"], p["sgu_w"], p["bmap"])
    qn, kn, vb, logf = _qk_fwd(tag + "_qk", z, p["q_norm_g"], p["k_norm_g"], p["b_forget"])
    ccol, crow = _cum_fwd(tag + "_cum", logf)
    crow3 = crow.reshape(NH, 1, T)
    o, lse, lser = _attn_fwd(tag + "_attn", qn, kn, vb, ccol, crow3)
    ya = _sq_fwd(tag + "_oconv", ya_in, wsq, l, 0)
    yb = _sq_fwd(tag + "_osgu", yb_in, wsq, l, 1)
    yc = _sq_fwd(tag + "_oattn", o, wsq, l, 2)
    merged = _merge_fwd(tag + "_merge", ya, yb, yc, z, p["b_gate"])
    xo = _sq_fwd(tag + "_o", merged, wsq, l, 3, res=x)
    return xo, (h, z, ya_in, yb_in, qn, kn, vb, ccol, crow3, o, lse, lser, ya, yb, yc, merged)


def _mixer_bwd(tag, dxo, x, p, win, wsq, l, saved, gsq):
    h, z, ya_in, yb_in, qn, kn, vb, ccol, crow3, o, lse, lser, ya, yb, yc, merged = saved
    T = x.shape[0]
    sg = {}
    dm = _sq_bwd_in(tag + "_dmerged", dxo, wsq, l, 3)
    gsq[3] = _sq_bwd_w(tag + "_dwo", merged, dxo, gsq[3], l)
    dya, dyb, dyc, dz_g, sg["b_gate"] = _merge_bwd(tag + "_dmerge", dm, ya, yb, yc, z, p["b_gate"])
    d_ya_in = _sq_bwd_in(tag + "_dconv_in", dya, wsq, l, 0)
    gsq[0] = _sq_bwd_w(tag + "_dwoc", ya_in, dya, gsq[0], l)
    d_yb_in = _sq_bwd_in(tag + "_dsgu_in", dyb, wsq, l, 1)
    gsq[1] = _sq_bwd_w(tag + "_dwos", yb_in, dyb, gsq[1], l)
    d_o = _sq_bwd_in(tag + "_dattn_in", dyc, wsq, l, 2)
    gsq[2] = _sq_bwd_w(tag + "_dwoa", o, dyc, gsq[2], l)
    dz_c, sg["conv_w"] = _conv_bwd(tag + "_dconv", d_ya_in, z, p["conv_w"])
    dz_s, sg["sgu_ln_g"], sg["sgu_ln_b"], sg["sgu_w"], db_t = _sgu_bwd(
        tag + "_dsgu", d_yb_in, z, p["sgu_ln_g"], p["sgu_ln_b"], p["sgu_w"], p["bmap"])
    sg["sgu_b"] = db_t.T
    dqn, dlr = _attn_dq(tag + "_dattn_q", qn, kn, vb, d_o, lse, ccol, crow3)
    dkn, dv, cs = _attn_dkv(tag + "_dattn_kv", qn, kn, vb, d_o, lser, dlr, ccol, crow3)
    dz_f, sg["b_forget"] = _forget_bwd(tag + "_dforget", cs, z, p["b_forget"])
    dz_q, sg["q_norm_g"], sg["k_norm_g"] = _qk_bwd(tag + "_dqk", dqn, dkn, dv, z, p["q_norm_g"], p["k_norm_g"])
    dz = jnp.concatenate([dz_c, dz_s, dz_q, dz_g, dz_f], axis=1)
    tk = 512
    dh = _mm(tag + "_dh", dz, win, grid=(1, 1, NZ // tk), a_spec=_bs((T, tk), lambda i, j, k: (0, k)),
             b_spec=_bs((D, tk), lambda i, j, k: (0, k)), out_shape=S((T, D), f32),
             out_spec=_bs((T, D), lambda i, j, k: (0, 0)), dims=NT, acc_shape=(T, D))
    tn = 512
    dwin = _mm(tag + "_dwin", h, dz, grid=(1, NZ // tn, 1), a_spec=_bs((T, D), lambda i, j, k: (0, 0)),
               b_spec=_bs((T, tn), lambda i, j, k: (0, j)), out_shape=S((D, NZ), bf16),
               out_spec=_bs((D, tn), lambda i, j, k: (0, j)), dims=TN, acc_shape=None)
    dx, sg["mix_norm"] = _rms_bwd(tag + "_drms", dh, x, p["mix_norm"], dxo)
    return dx, sg, dwin, gsq


ANY = pl.BlockSpec(memory_space=pl.ANY)
HBM = pl.BlockSpec(memory_space=pltpu.HBM)
SEM = pl.BlockSpec(memory_space=pltpu.SEMAPHORE)
EFFECT = pltpu.SideEffectType.DATAFLOW_SIDE_EFFECTING


def _place():
    return lax.axis_index("x"), lax.axis_index("y"), lax.axis_index("c")


NEAR = 4


def _others(x, y, c):
    return [(x, y, 1 - c), (1 - x, y, c), (x, 1 - y, c), (1 - x, 1 - y, c)]


def _gather_start(name, groups):
    sizes = [len(g) for g in groups]
    srcs = [s for g in groups for s, _ in g]
    lands = [b for g in groups for _, b in g]
    n, ng = len(srcs), len(groups)

    def body(*refs):
        src_refs, land_refs = refs[:n], refs[n:2 * n]
        send, recv = refs[2 * n:2 * n + ng], refs[2 * n + ng:2 * n + 2 * ng]
        x, y, c = _place()
        me = 4 * x + 2 * y + c
        u = 0
        for g, size in enumerate(sizes):
            for i in range(size):
                for k, peer in enumerate(_others(x, y, c)):
                    pltpu.make_async_remote_copy(src_ref=src_refs[u], dst_ref=land_refs[u].at[me],
                                                 send_sem=send[g].at[i * NEAR + k], recv_sem=recv[g].at[i * NEAR + k],
                                                 device_id=peer, device_id_type=MESH).start()
                u += 1

    sems = [pltpu.SemaphoreType.DMA((size * NEAR,)) for size in sizes]
    out = pl.pallas_call(
        body, name=name, in_specs=[HBM] * (2 * n), out_specs=[SEM] * (2 * ng) + [HBM] * (2 * n),
        out_shape=sems + sems + [pltpu.HBM(a.shape, a.dtype) for a in srcs + lands],
        input_output_aliases={i: 2 * ng + i for i in range(2 * n)},
        compiler_params=pltpu.CompilerParams(has_side_effects=EFFECT),
    )(*[pltpu.with_memory_space_constraint(a, pltpu.HBM) for a in srcs + lands])
    res, pos = [], 0
    for g, size in enumerate(sizes):
        res.append((out[g], out[ng + g], out[2 * ng + pos:2 * ng + pos + size], out[2 * ng + n + pos:2 * ng + n + pos + size]))
        pos += size
    return res


def _gather_wait(name, started, after=None):
    send, recv, srcs, lands = started
    n = len(srcs)

    def body(*refs):
        src_refs, land_refs = refs[:n], refs[n:2 * n]
        send_ref, recv_ref = refs[2 * n], refs[2 * n + 1]
        x, y, c = _place()
        for i in range(n):
            for k, (px, py, pc) in enumerate(_others(x, y, c)):
                cp = pltpu.make_async_remote_copy(src_ref=src_refs[i], dst_ref=land_refs[i].at[4 * px + 2 * py + pc],
                                                  send_sem=send_ref.at[i * NEAR + k], recv_sem=recv_ref.at[i * NEAR + k],
                                                  device_id=(px, py, pc), device_id_type=MESH)
                cp.wait_send()
                cp.wait_recv()

    extra = [] if after is None else [after]
    out = pl.pallas_call(
        body, name=name, in_specs=[HBM] * (2 * n) + [SEM, SEM] + [ANY] * len(extra), out_specs=[HBM] * (2 * n),
        out_shape=[pltpu.HBM(a.shape, a.dtype) for a in list(srcs) + list(lands)],
        input_output_aliases={i: i for i in range(2 * n)},
        compiler_params=pltpu.CompilerParams(has_side_effects=EFFECT),
    )(*srcs, *lands, send, recv, *extra)
    return out[n:]


def _gather_forward(name, lands):
    n = len(lands)

    def body(*refs):
        have, full = refs[:n], refs[n:2 * n]
        send, recv = refs[2 * n], refs[2 * n + 1]
        x, y, c = _place()
        chips = [(1 - x, y), (x, 1 - y), (1 - x, 1 - y)]

        def copy(i, j, core):
            slot = 4 * chips[j][0] + 2 * chips[j][1] + core
            return pltpu.make_async_remote_copy(src_ref=have[i].at[slot], dst_ref=full[i].at[slot],
                                                send_sem=send.at[i * 3 + j], recv_sem=recv.at[i * 3 + j],
                                                device_id=(x, y, 1 - c), device_id_type=MESH)

        for i in range(n):
            for j in range(3):
                copy(i, j, c).start()
        for i in range(n):
            for j in range(3):
                copy(i, j, c).wait_send()
                copy(i, j, 1 - c).wait_recv()

    return pl.pallas_call(
        body, name=name, in_specs=[ANY] * n, out_specs=[ANY] * n, out_shape=[S(a.shape, a.dtype) for a in lands],
        input_output_aliases={i: i for i in range(n)},
        scratch_shapes=[pltpu.SemaphoreType.DMA((n * 3,)), pltpu.SemaphoreType.DMA((n * 3,))],
    )(*lands)


def _rs_pair(name, gs):
    n = len(gs)

    def body(*refs):
        ins, outs = refs[:n], refs[n:2 * n]
        send, recv = refs[2 * n:]
        x, y, c = _place()
        cps = [pltpu.make_async_remote_copy(src_ref=ins[u].at[1 - c], dst_ref=outs[u], send_sem=send.at[u],
                                            recv_sem=recv.at[u], device_id=(x, y, 1 - c), device_id_type=MESH)
               for u in range(n)]
        for cp in cps:
            cp.start()
        for cp in cps:
            cp.wait()

    return pl.pallas_call(
        body, name=name, in_specs=[ANY] * n, out_specs=[ANY] * n, out_shape=[S(g.shape[1:], g.dtype) for g in gs],
        scratch_shapes=[pltpu.SemaphoreType.DMA((n,)), pltpu.SemaphoreType.DMA((n,))],
    )(*gs)


def _row_tile(r, c):
    return 128 if (r % 128 == 0 and c > D) else (256 if r % 256 == 0 else r)


def _pair_sum(name, core, g, r1):
    _, nq, nl, r, c = g.shape
    tr = _row_tile(r, c)
    g4 = g.reshape(2, nq * nl, r, c)
    r3 = r1.reshape(nq * nl, r, c)

    def body(core_ref, g_ref, r_ref, o_ref):
        o_ref[...] = (g_ref[...].astype(f32) + r_ref[...].astype(f32)).astype(bf16)

    out = pl.pallas_call(
        body, name=name,
        grid_spec=pltpu.PrefetchScalarGridSpec(
            num_scalar_prefetch=1, grid=(nq * nl, r // tr),
            in_specs=[_bs((None, None, tr, c), lambda b, i, cr: (cr[0], b, i, 0)), _bs((None, tr, c), lambda b, i, cr: (b, i, 0))],
            out_specs=_bs((None, tr, c), lambda b, i, cr: (b, i, 0))),
        out_shape=S((nq * nl, r, c), bf16), compiler_params=_cp("parallel", "parallel"))(core, g4, r3)
    return out.reshape(nq, nl, r, c)


def _scatter_copies(ins, outs, send, recv):
    x, y, c = _place()
    chips = [(1 - x, y), (x, 1 - y), (1 - x, 1 - y)]
    return [pltpu.make_async_remote_copy(src_ref=ins[u].at[2 * chip[0] + chip[1]], dst_ref=outs[u].at[k],
                                         send_sem=send.at[u * 3 + k], recv_sem=recv.at[u * 3 + k],
                                         device_id=(*chip, c), device_id_type=MESH)
            for u in range(len(ins)) for k, chip in enumerate(chips)]


def _scatter_start(name, ss, carry):
    n = len(ss)
    lands = [lax.empty((3,) + s.shape[1:], s.dtype) for s in ss]
    held = list(ss) + lands + [carry]

    def body(*refs):
        for cp in _scatter_copies(refs[:n], refs[n:2 * n], refs[2 * n + 1], refs[2 * n + 2]):
            cp.start()

    sems = [pltpu.SemaphoreType.DMA((n * 3,))] * 2
    out = pl.pallas_call(
        body, name=name, in_specs=[HBM] * len(held), out_specs=[SEM, SEM] + [HBM] * len(held),
        out_shape=sems + [pltpu.HBM(a.shape, a.dtype) for a in held],
        input_output_aliases={i: 2 + i for i in range(len(held))},
        compiler_params=pltpu.CompilerParams(has_side_effects=EFFECT),
    )(*[pltpu.with_memory_space_constraint(a, pltpu.HBM) for a in held])
    return (out[0], out[1], out[2:2 + n], out[2 + n:2 + 2 * n]), out[2 + 2 * n]


def _scatter_wait(name, started, after):
    send, recv, srcs, lands = started
    n = len(srcs)

    def body(*refs):
        for cp in _scatter_copies(refs[:n], refs[n:2 * n], refs[2 * n], refs[2 * n + 1]):
            cp.wait_send()
            cp.wait_recv()

    out = pl.pallas_call(
        body, name=name, in_specs=[HBM] * (2 * n) + [SEM, SEM, ANY], out_specs=[HBM] * (2 * n),
        out_shape=[pltpu.HBM(a.shape, a.dtype) for a in list(srcs) + list(lands)],
        input_output_aliases={i: i for i in range(2 * n)},
        compiler_params=pltpu.CompilerParams(has_side_effects=EFFECT),
    )(*srcs, *lands, send, recv, after)
    return out[:n], out[n:]


def _all_reduce_small(name, v):
    r = v.shape[0]

    def body(v_ref, o_ref, buf, send, recv, lsem):
        x, y, c = _place()
        me, sib = (x, y, c), (x, y, 1 - c)
        chips = [(1 - x, y), (x, 1 - y), (1 - x, 1 - y)]

        def slot(px, py, pc):
            return buf.at[4 * px + 2 * py + pc]

        def copy(k, block, to, src=None):
            return pltpu.make_async_remote_copy(src_ref=slot(*block) if src is None else src, dst_ref=slot(*block),
                                                send_sem=send.at[k], recv_sem=recv.at[k], device_id=to,
                                                device_id_type=MESH)

        mine = pltpu.make_async_copy(v_ref, slot(*me), lsem)
        mine.start()
        first = [copy(0, me, sib, v_ref)] + [copy(1 + j, me, (*chip, c), v_ref) for j, chip in enumerate(chips)]
        for cp in first:
            cp.start()
        passed = [copy(4 + j, (*chip, c), sib) for j, chip in enumerate(chips)]
        for j, chip in enumerate(chips):
            copy(1 + j, (*chip, c), me).wait_recv()
            passed[j].start()
        copy(0, sib, me).wait_recv()
        for j, chip in enumerate(chips):
            copy(4 + j, (*chip, 1 - c), me).wait_recv()
        for cp in first + passed:
            cp.wait_send()
        mine.wait()
        acc = buf[0]
        for d in range(1, NDEV):
            acc = acc + buf[d]
        o_ref[...] = acc

    vm = pl.BlockSpec(memory_space=pltpu.VMEM)
    return pl.pallas_call(
        body, name=name, in_specs=[vm], out_specs=vm, out_shape=S((r, 128), f32),
        scratch_shapes=[pltpu.VMEM((NDEV, r, 128), f32), pltpu.SemaphoreType.DMA((7,)), pltpu.SemaphoreType.DMA((7,)),
                        pltpu.SemaphoreType.DMA],
        compiler_params=pltpu.CompilerParams(vmem_limit_bytes=VMEM_LIMIT),
    )(v)


def _adam_math(w, g, m, v):
    m = ADAM_B1 * m + (1.0 - ADAM_B1) * g
    v = ADAM_B2 * v + (1.0 - ADAM_B2) * (g * g)
    m_hat = m / (1.0 - ADAM_B1 ** ADAM_STEP)
    v_hat = v / (1.0 - ADAM_B2 ** ADAM_STEP)
    delta = -ADAM_LR * (m_hat / (jnp.sqrt(v_hat) + ADAM_EPS) + ADAM_WD * w)
    return delta, m, v


def _adamw(name, w, m, v, parts):
    _, r, c = w.shape
    tr = _row_tile(r, c)

    def body(w_ref, m_ref, v_ref, *refs):
        sets, (g_ref, d_ref, mo_ref, vo_ref) = (refs[0:4], refs[4:8]), refs[8:]
        for l in range(2):
            @pl.when(pl.program_id(0) == l)
            def _():
                s_ref, r0_ref, r1_ref, r2_ref = sets[l]
                g = ((s_ref[...].astype(f32) + r0_ref[...].astype(f32)) + r1_ref[...].astype(f32)) + r2_ref[...].astype(f32)
                g_ref[...] = g
                d_ref[...], mo_ref[...], vo_ref[...] = _adam_math(w_ref[...], g, m_ref[...], v_ref[...])

    blk = _bs((None, tr, c), lambda l, i: (l, i, 0))
    operands, specs = [], []
    for n in range(2):
        row = (lambda l, i: i * (1 - l)) if n == 0 else (lambda l, i: i * l)
        s_mine, r2 = parts[n]
        operands += [s_mine, r2, r2, r2]
        specs.append(_bs((tr, c), functools.partial(lambda l, i, row: (row(l, i), 0), row=row)))
        specs += [_bs((None, None, tr, c), functools.partial(lambda l, i, k, row: (k, 0, row(l, i), 0), k=k, row=row))
                  for k in range(3)]
    return pl.pallas_call(
        body, name=name, grid=(2, r // tr), in_specs=[blk, blk, blk] + specs,
        out_specs=[blk] * 4, out_shape=[S(w.shape, f32)] * 4, compiler_params=_cp("arbitrary", "arbitrary"),
    )(w, m, v, *operands)


def _adamw_small(name, w, g, m, v):
    def body(w_ref, g_ref, m_ref, v_ref, d_ref, mo_ref, vo_ref):
        d_ref[...], mo_ref[...], vo_ref[...] = _adam_math(w_ref[...], g_ref[...], m_ref[...], v_ref[...])

    return pl.pallas_call(body, name=name, out_shape=[S(w.shape, f32)] * 3,
                          compiler_params=pltpu.CompilerParams(vmem_limit_bytes=VMEM_LIMIT))(w, g, m, v)


WEIGHT_NAMES = ("ffn1_norm", "ffn1_w_gu", "ffn1_w_down", "mix_norm", "w_in", "b_forget", "b_gate", "conv_w", "sgu_ln_g",
                "sgu_ln_b", "sgu_w", "sgu_b", "q_norm_g", "k_norm_g", "w_out_conv", "w_out_sgu", "w_out_attn", "w_o",
                "ffn2_norm", "ffn2_w_gu", "ffn2_w_down")
BIG = {"ffn1_w_gu": "gu1", "ffn2_w_gu": "gu2", "ffn1_w_down": "d1", "ffn2_w_down": "d2", "w_in": "in",
       "w_out_conv": "oc", "w_out_sgu": "os", "w_out_attn": "oa", "w_o": "o"}
BIG_KEYS = ("gu1", "gu2", "d1", "d2", "in", "oc", "os", "oa", "o")
REPLICATED_SMALL = ("ffn1_norm", "mix_norm", "b_forget", "sgu_ln_g", "sgu_ln_b", "sgu_w", "sgu_b", "q_norm_g",
                    "k_norm_g", "ffn2_norm")
SHARDED_SMALL = ("b_gate", "conv_w")


def _pack(arrays):
    flat = jnp.concatenate([a.reshape(-1).astype(f32) for a in arrays])
    rows = -(-flat.shape[0] // 1024) * 8
    return jnp.pad(flat, (0, rows * 128 - flat.shape[0])).reshape(rows, 128)


def _unpack(packed, shapes):
    flat = packed.reshape(-1)
    out, pos = [], 0
    for shp in shapes:
        size = 1
        for s_ in shp:
            size *= s_
        out.append(flat[pos:pos + size].reshape(shp))
        pos += size
    return out


def _natural_runs(a, b):
    runs = []
    while a < b:
        d = a // INB
        e = min(b, (d + 1) * INB)
        runs.append((d, a - d * INB, e - d * INB))
        a = e
    return runs


def _win_kernel_layout(wg):
    runs = _natural_runs(0, GATE_OFF) + _natural_runs(GATE_OFF + NH, NIN) + _natural_runs(GATE_OFF, GATE_OFF + NH)
    return jnp.concatenate([wg[d, :, a:b] for d, a, b in runs] + [jnp.zeros((D, NZ - NIN), wg.dtype)], axis=1)


def _kernel_column(n):
    return n if n < GATE_OFF else (F_OFF + n - GATE_OFF if n < GATE_OFF + NH else n - NH)


def _win_device_block(dw, d):
    cuts = sorted({d * INB, (d + 1) * INB} | {c for c in (GATE_OFF, GATE_OFF + NH) if d * INB < c < (d + 1) * INB})
    parts = [dw[:, _kernel_column(a):_kernel_column(a) + (b - a)] for a, b in zip(cuts[:-1], cuts[1:])]
    return parts[0] if len(parts) == 1 else jnp.concatenate(parts, axis=1)


def kernel(x, ffn1_norm, ffn1_w_gu, ffn1_w_down, mix_norm, w_in, b_forget, b_gate, conv_w, sgu_ln_g, sgu_ln_b, sgu_w, sgu_b, q_norm_g, k_norm_g, w_out_conv, w_out_sgu, w_out_attn, w_o, ffn2_norm, ffn2_w_gu, ffn2_w_down, loss_target, m_ffn1_norm, m_ffn1_w_gu, m_ffn1_w_down, m_mix_norm, m_w_in, m_b_forget, m_b_gate, m_conv_w, m_sgu_ln_g, m_sgu_ln_b, m_sgu_w, m_sgu_b, m_q_norm_g, m_k_norm_g, m_w_out_conv, m_w_out_sgu, m_w_out_attn, m_w_o, m_ffn2_norm, m_ffn2_w_gu, m_ffn2_w_down, v_ffn1_norm, v_ffn1_w_gu, v_ffn1_w_down, v_mix_norm, v_w_in, v_b_forget, v_b_gate, v_conv_w, v_sgu_ln_g, v_sgu_ln_b, v_sgu_w, v_sgu_b, v_q_norm_g, v_k_norm_g, v_w_out_conv, v_w_out_sgu, v_w_out_attn, v_w_o, v_ffn2_norm, v_ffn2_w_gu, v_ffn2_w_down):
    w = dict(zip(WEIGHT_NAMES, (ffn1_norm, ffn1_w_gu, ffn1_w_down, mix_norm, w_in, b_forget, b_gate, conv_w, sgu_ln_g,
                                sgu_ln_b, sgu_w, sgu_b, q_norm_g, k_norm_g, w_out_conv, w_out_sgu, w_out_attn, w_o,
                                ffn2_norm, ffn2_w_gu, ffn2_w_down)))
    mom = dict(zip(WEIGHT_NAMES, (m_ffn1_norm, m_ffn1_w_gu, m_ffn1_w_down, m_mix_norm, m_w_in, m_b_forget, m_b_gate,
                                  m_conv_w, m_sgu_ln_g, m_sgu_ln_b, m_sgu_w, m_sgu_b, m_q_norm_g, m_k_norm_g,
                                  m_w_out_conv, m_w_out_sgu, m_w_out_attn, m_w_o, m_ffn2_norm, m_ffn2_w_gu,
                                  m_ffn2_w_down)))
    var = dict(zip(WEIGHT_NAMES, (v_ffn1_norm, v_ffn1_w_gu, v_ffn1_w_down, v_mix_norm, v_w_in, v_b_forget, v_b_gate,
                                  v_conv_w, v_sgu_ln_g, v_sgu_ln_b, v_sgu_w, v_sgu_b, v_q_norm_g, v_k_norm_g,
                                  v_w_out_conv, v_w_out_sgu, v_w_out_attn, v_w_o, v_ffn2_norm, v_ffn2_w_gu,
                                  v_ffn2_w_down)))
    px, py, pc = _place()
    dev = 4 * px + 2 * py + pc
    chip = 2 * px + py

    big_names = [n for n in WEIGHT_NAMES if n in BIG]
    key_name = {BIG[n]: n for n in big_names}
    group_keys = (("gu1", "d1"), ("in", "oc", "os", "oa", "o", "small"), ("gu2", "d2"))

    def source(key, l):
        if key == "small":
            return jnp.concatenate([w["b_gate"][l], w["conv_w"][l], jnp.zeros((2, 128), f32)], axis=0)
        return w[key_name[key]][l].astype(bf16)

    def landing(src):
        return lax.dynamic_update_slice(lax.empty((NDEV,) + src.shape, src.dtype), src[None], (dev, 0, 0))

    groups = [[(s, landing(s)) for s in (source(k, l) for k in keys)] for l in range(2) for keys in group_keys]
    started = _gather_start("gather_start", groups)

    def weights(l, part, after):
        got = _gather_wait(f"gather_wait_{l}_{part}", started[3 * l + part], after)
        return dict(zip(group_keys[part], _gather_forward(f"gather_forward_{l}_{part}", got)))

    xl = x[0]
    saved, small, wts = [], [], []
    for l in range(2):
        ga = weights(l, 0, xl if l else None)
        wt = {"gu1": ga["gu1"][None], "d1": ga["d1"].reshape(1, 4, GU, D)}
        x1, s1 = _ffn_fwd("ffn1", xl, w["ffn1_norm"][l].reshape(1, D), wt["gu1"], wt["d1"], 0)
        gb = weights(l, 1, x1)
        p = {n: w[n][l] for n in REPLICATED_SMALL}
        p["b_gate"] = jnp.transpose(gb["small"][:, 0:3, :], (1, 0, 2)).reshape(3, D)
        p["conv_w"] = jnp.transpose(gb["small"][:, 3:6, :], (1, 0, 2)).reshape(3, D)
        p = _small_params(p)
        wt["win"] = _win_kernel_layout(gb["in"])
        wt["sq"] = jnp.stack([gb[k].reshape(D, D) for k in ("oc", "os", "oa", "o")])[None]
        x2, sm = _mixer_fwd("mix", x1, p, wt["win"], wt["sq"], 0)
        gc = weights(l, 2, x2)
        wt.update({"gu2": gc["gu2"][None], "d2": gc["d2"].reshape(1, 4, GU, D)})
        x3, s2 = _ffn_fwd("ffn2", x2, p["ffn2_norm"], wt["gu2"], wt["d2"], 0)
        saved.append((xl, x1, x2, s1, sm, s2))
        small.append(p)
        wts.append(wt)
        xl = x3
    loss_row, dx = _loss("loss", xl, loss_target[0])

    core = pc.reshape(1).astype(jnp.int32)
    buf = lambda r, c: lax.empty((2, 4, 1, r, c), bf16)
    flights = {}

    def scatter(l, part, bufs, dx):
        r1 = _rs_pair(f"rs_pair_{l}_{part}", bufs)
        ss = [_pair_sum(f"pair_sum_{l}_{k}", core, g, r) for k, g, r in zip(group_keys[part], bufs, r1)]
        flights[l, part], dx = _scatter_start(f"scatter_start_{l}_{part}", ss, dx)
        return dx

    sgrads = [None, None]
    for l in (1, 0):
        p, wt = small[l], wts[l]
        x0, x1, x2, s1, sm, s2 = saved[l]
        dx, dn2, g_gu2, g_d2 = _ffn_bwd("ffn2", dx, x2, p["ffn2_norm"], wt["gu2"], wt["d2"], 0, s2, buf(D, GU), buf(GU // 2, D))
        dx = scatter(l, 2, [g_gu2, g_d2], dx)
        dx, sg, dwin, gsq = _mixer_bwd("mix", dx, x1, p, wt["win"], wt["sq"], 0, sm, [buf(128, D) for _ in range(4)])
        g_in = jnp.stack([jnp.stack([_win_device_block(dwin, 2 * q + c)[None] for q in range(4)]) for c in range(2)])
        dx = scatter(l, 1, [g_in] + gsq, dx)
        dx, dn1, g_gu1, g_d1 = _ffn_bwd("ffn1", dx, x0, p["ffn1_norm"], wt["gu1"], wt["d1"], 0, s1, buf(D, GU), buf(GU // 2, D))
        dx = scatter(l, 0, [g_gu1, g_d1], dx)
        sg["ffn1_norm"] = dn1
        sg["ffn2_norm"] = dn2
        sgrads[l] = sg

    grads, delta, new_m, new_v = {}, {}, {}, {}
    after = dx
    for part in (2, 1, 0):
        sets = []
        for l in (1, 0):
            s_all, r2_all = _scatter_wait(f"scatter_wait_{l}_{part}", flights[l, part], after)
            sets.append([(lax.dynamic_index_in_dim(s, chip, 0, keepdims=False)[0], r) for s, r in zip(s_all, r2_all)])
        for i, k in enumerate(k for k in group_keys[part] if k != "small"):
            n = key_name[k]
            grads[n], delta[n], new_m[n], new_v[n] = _adamw("adamw_" + k, w[n], mom[n], var[n], [sets[1][i], sets[0][i]])
            after = delta[n]

    nat = [_small_grads_natural(sgrads[l]) for l in range(2)]
    order = REPLICATED_SMALL + SHARDED_SMALL
    part = _pack([jnp.stack([nat[0][n], nat[1][n]]) for n in order] + [loss_row[0, 0:1]])
    total = _all_reduce_small("reduce_small", part)
    full_shapes = [(2,) + tuple(nat[0][n].shape) for n in order] + [(1,)]
    summed = dict(zip(order + ("loss",), _unpack(total, full_shapes)))
    for n in REPLICATED_SMALL:
        grads[n] = summed[n]
    for n in SHARDED_SMALL:
        grads[n] = lax.dynamic_slice_in_dim(summed[n], dev * 128, 128, axis=2)
    wp = _pack([w[n] for n in order])
    gp = _pack([grads[n] for n in order])
    mp = _pack([mom[n] for n in order])
    vp = _pack([var[n] for n in order])
    dpk, mpk, vpk = _adamw_small("adamw_small", wp, gp, mp, vp)
    local_shapes = [tuple(w[n].shape) for n in order]
    for dst, packed in ((delta, dpk), (new_m, mpk), (new_v, vpk)):
        dst.update(zip(order, _unpack(packed, local_shapes)))

    loss = summed["loss"][0]
    return (loss, dx[None], *[grads[n] for n in WEIGHT_NAMES], *[delta[n] for n in WEIGHT_NAMES],
            *[new_m[n] for n in WEIGHT_NAMES], *[new_v[n] for n in WEIGHT_NAMES])
```

```python
import functools

import jax
import jax.numpy as jnp
from jax import lax
from jax.experimental import pallas as pl
from jax.experimental.pallas import tpu as pltpu

f32 = jnp.float32
bf16 = jnp.bfloat16
S = jax.ShapeDtypeStruct
MESH = pl.DeviceIdType.MESH

D = 1024
NH = 8
HD = 128
NDEV = 8
GU = 704
NIN = 11272
INB = 1409
GATE_OFF = 8192
F_OFF = 11264
NZ = 11776
RMS_EPS = 1e-6
LN_EPS = 1e-5
ATT_SCALE = HD ** -0.5
NEG = -1e30
INV_SQRT2 = 0.7071067811865476
INV_SQRT2PI = 0.3989422804014327

ADAM_LR = 0.001
ADAM_B1 = 0.9
ADAM_B2 = 0.999
ADAM_EPS = 1e-08
ADAM_WD = 0.01
ADAM_STEP = 10

TT = 256
VMEM_LIMIT = 56 * 1024 * 1024


def _cp(*sem):
    return pltpu.CompilerParams(dimension_semantics=sem, vmem_limit_bytes=VMEM_LIMIT)


def _bs(shape, fn):
    return pl.BlockSpec(shape, fn)


NN = (((1,), (0,)), ((), ()))
NT = (((1,), (1,)), ((), ()))
TN = (((0,), (0,)), ((), ()))


def _mm(name, a, b, *, grid, a_spec, b_spec, out_shape, out_spec, dims, acc_shape, res=None, res_spec=None,
        alpha=1.0, alias=None, split_rows=None):
    nk = grid[2]

    def body(*refs):
        a_ref, b_ref = refs[0], refs[1]
        pos = 2
        res_ref = None
        if res is not None:
            res_ref = refs[pos]
            pos += 1
        if alias is not None:
            pos += 1
        o_ref = refs[pos]
        acc_ref = refs[pos + 1] if nk > 1 else None
        part = lax.dot_general(a_ref[...].astype(bf16), b_ref[...].astype(bf16), dims, preferred_element_type=f32)

        def finish(acc):
            if alpha != 1.0:
                acc = alpha * acc
            if res_ref is not None:
                acc = res_ref[...] + acc
            if split_rows is None:
                o_ref[...] = acc.astype(o_ref.dtype)
            else:
                o_ref[0] = acc[:split_rows].astype(o_ref.dtype)
                o_ref[1] = acc[split_rows:].astype(o_ref.dtype)

        if nk == 1:
            finish(part)
        else:
            k = pl.program_id(2)

            @pl.when(k == 0)
            def _():
                acc_ref[...] = part

            @pl.when(k > 0)
            def _():
                acc_ref[...] += part

            @pl.when(k == nk - 1)
            def _():
                finish(acc_ref[...])

    operands = [a, b]
    in_specs = [a_spec, b_spec]
    if res is not None:
        operands.append(res)
        in_specs.append(res_spec)
    aliases = {}
    if alias is not None:
        aliases = {len(operands): 0}
        operands.append(alias)
        in_specs.append(pl.BlockSpec(memory_space=pl.ANY))
    return pl.pallas_call(
        body, name=name, grid=grid, in_specs=in_specs, out_specs=out_spec, out_shape=out_shape,
        scratch_shapes=[pltpu.VMEM(acc_shape, f32)] if nk > 1 else [],
        input_output_aliases=aliases,
        compiler_params=_cp("parallel", "parallel", "arbitrary"),
    )(*operands)


def _tile(n, t):
    return t if n % t == 0 and n >= t else n


def _row(cb=0, w=D):
    return _bs((TT, w), lambda i: (i, cb))


def _vec(rows=1, w=D):
    return _bs((rows, w), lambda i: (0, 0))


def _acc_store(i, ref, val):
    @pl.when(i == 0)
    def _():
        ref[...] = val

    @pl.when(i > 0)
    def _():
        ref[...] += val


def _rms_fwd(name, x, g):
    T = x.shape[0]

    def body(x_ref, g_ref, o_ref):
        xv = x_ref[...]
        r = lax.rsqrt(jnp.mean(xv * xv, axis=-1, keepdims=True) + RMS_EPS)
        o_ref[...] = (xv * r * g_ref[...]).astype(bf16)

    return pl.pallas_call(body, name=name, grid=(T // TT,), in_specs=[_row(), _vec()], out_specs=_row(),
                          out_shape=S((T, D), bf16), compiler_params=_cp("parallel"))(x, g)


def _rms_bwd(name, dh, x, g, dres):
    T = x.shape[0]

    def body(dh_ref, x_ref, g_ref, dres_ref, dx_ref, dg_ref):
        i = pl.program_id(0)
        xv = x_ref[...]
        r = lax.rsqrt(jnp.mean(xv * xv, axis=-1, keepdims=True) + RMS_EPS)
        xhat = xv * r
        dh_v = dh_ref[...]
        dyg = dh_v * g_ref[...]
        m = jnp.mean(dyg * xhat, axis=-1, keepdims=True)
        dx_ref[...] = dres_ref[...] + r * (dyg - xhat * m)
        _acc_store(i, dg_ref, jnp.sum(dh_v * xhat, axis=0, keepdims=True))

    return pl.pallas_call(body, name=name, grid=(T // TT,), in_specs=[_row(), _row(), _vec(), _row()],
                          out_specs=[_row(), _vec()], out_shape=[S((T, D), f32), S((1, D), f32)],
                          compiler_params=_cp("arbitrary"))(dh, x, g, dres)


def _sigmoid(x):
    return 1.0 / (1.0 + jnp.exp(-x))


def _swiglu_fwd(name, gu):
    T = gu.shape[1]

    def body(g_ref, u_ref, o_ref):
        g = g_ref[...]
        o_ref[...] = (g * _sigmoid(g) * u_ref[...]).astype(bf16)

    return pl.pallas_call(
        body, name=name, grid=(4, T // TT),
        in_specs=[_bs((None, TT, GU), lambda j, i: (j, i, 0)), _bs((None, TT, GU), lambda j, i: (j + 4, i, 0))],
        out_specs=_bs((None, TT, GU), lambda j, i: (j, i, 0)), out_shape=S((4, T, GU), bf16),
        compiler_params=_cp("parallel", "parallel"))(gu, gu)


def _swiglu_bwd(name, gu, da):
    T = gu.shape[1]

    def body(g_ref, u_ref, da_ref, o_ref):
        g = g_ref[...]
        u = u_ref[...]
        da_v = da_ref[...]
        sg = _sigmoid(g)
        o_ref[0] = (da_v * u * (sg + g * sg * (1.0 - sg))).astype(bf16)
        o_ref[1] = (da_v * g * sg).astype(bf16)

    return pl.pallas_call(
        body, name=name, grid=(4, T // TT),
        in_specs=[_bs((None, TT, GU), lambda j, i: (j, i, 0)), _bs((None, TT, GU), lambda j, i: (j + 4, i, 0)),
                  _bs((None, TT, GU), lambda j, i: (j, i, 0))],
        out_specs=_bs((2, None, TT, GU), lambda j, i: (0, j, i, 0)), out_shape=S((2, 4, T, GU), bf16),
        compiler_params=_cp("parallel", "parallel"))(gu, gu, da)


def _loss(name, y, tgt):
    T = y.shape[0]

    def body(y_ref, t_ref, l_ref, dy_ref):
        i = pl.program_id(0)
        e = y_ref[...] - t_ref[...]
        dy_ref[...] = e * (1.0 / D)
        s = 0.5 * jnp.sum(jnp.mean(e * e, axis=-1, keepdims=True))
        _acc_store(i, l_ref, jnp.broadcast_to(s, (1, 128)))

    return pl.pallas_call(body, name=name, grid=(T // TT,), in_specs=[_row(), _row()],
                          out_specs=[_vec(1, 128), _row()], out_shape=[S((1, 128), f32), S((T, D), f32)],
                          compiler_params=_cp("arbitrary"))(y, tgt)


def _prev8(T, cb):
    return _bs((8, D), lambda i: (jnp.maximum(i * (TT // 8) - 1, 0), cb))


def _next8(T, cb):
    return _bs((8, D), lambda i: (jnp.minimum((i + 1) * (TT // 8), T // 8 - 1), cb))


def _conv_taps(i, ac_ref, ax_ref, pc_ref, px_ref):
    ca = ac_ref[...] * ax_ref[...]
    keep = (i > 0).astype(f32)
    p1 = pc_ref[7:8, :] * px_ref[7:8, :] * keep
    p2 = pc_ref[6:7, :] * px_ref[6:7, :] * keep
    row = lax.broadcasted_iota(jnp.int32, ca.shape, 0)
    s1 = jnp.where(row == 0, p1, pltpu.roll(ca, 1, 0))
    s2 = jnp.where(row == 0, p2, jnp.where(row == 1, p1, pltpu.roll(ca, 2, 0)))
    return ca, s1, s2


def _conv_fwd(name, z, cw):
    T = z.shape[0]

    def body(ab_ref, ac_ref, ax_ref, pc_ref, px_ref, w_ref, o_ref):
        i = pl.program_id(0)
        ca, s1, s2 = _conv_taps(i, ac_ref, ax_ref, pc_ref, px_ref)
        cv = w_ref[0:1, :] * s2 + w_ref[1:2, :] * s1 + w_ref[2:3, :] * ca
        o_ref[...] = (ab_ref[...] * cv).astype(bf16)

    return pl.pallas_call(
        body, name=name, grid=(T // TT,),
        in_specs=[_row(0), _row(1), _row(2), _prev8(T, 1), _prev8(T, 2), _vec(3)],
        out_specs=_row(), out_shape=S((T, D), bf16), compiler_params=_cp("parallel"))(z, z, z, z, z, cw)


def _conv_bwd(name, dya, z, cw):
    T = z.shape[0]
    n = T // TT

    def body(dya_ref, ab_ref, ac_ref, ax_ref, pc_ref, px_ref, ndya_ref, nab_ref, w_ref, dz_ref, dw_ref):
        i = pl.program_id(0)
        ca, s1, s2 = _conv_taps(i, ac_ref, ax_ref, pc_ref, px_ref)
        w0, w1, w2 = w_ref[0:1, :], w_ref[1:2, :], w_ref[2:3, :]
        cv = w0 * s2 + w1 * s1 + w2 * ca
        dya_v = dya_ref[...]
        ab = ab_ref[...]
        dcv = dya_v * ab
        keep = (i < n - 1).astype(f32)
        n1 = ndya_ref[0:1, :] * nab_ref[0:1, :] * keep
        n2 = ndya_ref[1:2, :] * nab_ref[1:2, :] * keep
        row = lax.broadcasted_iota(jnp.int32, dcv.shape, 0)
        f1 = jnp.where(row == TT - 1, n1, pltpu.roll(dcv, TT - 1, 0))
        f2 = jnp.where(row == TT - 1, n2, jnp.where(row == TT - 2, n1, pltpu.roll(dcv, TT - 2, 0)))
        dca = w2 * dcv + w1 * f1 + w0 * f2
        dz_ref[:, 0:D] = (dya_v * cv).astype(bf16)
        dz_ref[:, D:2 * D] = (dca * ax_ref[...]).astype(bf16)
        dz_ref[:, 2 * D:3 * D] = (dca * ac_ref[...]).astype(bf16)
        dw = jnp.concatenate([jnp.sum(dcv * s2, axis=0, keepdims=True), jnp.sum(dcv * s1, axis=0, keepdims=True),
                              jnp.sum(dcv * ca, axis=0, keepdims=True)], axis=0)
        _acc_store(i, dw_ref, dw)

    return pl.pallas_call(
        body, name=name, grid=(n,),
        in_specs=[_row(), _row(0), _row(1), _row(2), _prev8(T, 1), _prev8(T, 2), _next8(T, 0), _next8(T, 0), _vec(3)],
        out_specs=[_row(0, 3 * D), _vec(3)], out_shape=[S((T, 3 * D), bf16), S((3, D), f32)],
        compiler_params=_cp("arbitrary"))(dya, z, z, z, z, z, dya, z, cw)


def _gelu(x):
    return 0.5 * x * (1.0 + lax.erf(x * INV_SQRT2))


def _gelu_grad(x):
    return 0.5 * (1.0 + lax.erf(x * INV_SQRT2)) + x * jnp.exp(-0.5 * x * x) * INV_SQRT2PI


def _ln_stats(vv):
    mu = jnp.mean(vv, axis=-1, keepdims=True)
    xc = vv - mu
    rstd = lax.rsqrt(jnp.mean(xc * xc, axis=-1, keepdims=True) + LN_EPS)
    return xc * rstd, rstd


def _tril_w(w_ref, g):
    r = lax.broadcasted_iota(jnp.int32, (HD, HD), 0)
    c = lax.broadcasted_iota(jnp.int32, (HD, HD), 1)
    return jnp.where(c <= r, w_ref[g], 0.0).astype(bf16)


def _sgu_fwd(name, z, ln_g, ln_b, w_s, bmap):
    T = z.shape[0]

    def body(su_ref, sv_ref, lg_ref, lb_ref, w_ref, bm_ref, o_ref, vn_ref):
        xhat, _ = _ln_stats(_gelu(sv_ref[...]))
        vn_ref[...] = (xhat * lg_ref[...] + lb_ref[...]).astype(bf16)
        for g in range(NH):
            w = _tril_w(w_ref, g)
            cs = slice(g * HD, (g + 1) * HD)
            for c in range(TT // HD):
                rs = slice(c * HD, (c + 1) * HD)
                s = jnp.dot(w, vn_ref[rs, cs], preferred_element_type=f32) + bm_ref[:, cs]
                o_ref[rs, cs] = (_gelu(su_ref[rs, cs]) * s).astype(bf16)

    return pl.pallas_call(
        body, name=name, grid=(T // TT,),
        in_specs=[_row(3), _row(4), _vec(), _vec(), _bs((NH, HD, HD), lambda i: (0, 0, 0)), _vec(HD)],
        out_specs=_row(), out_shape=S((T, D), bf16), scratch_shapes=[pltpu.VMEM((TT, D), bf16)],
        compiler_params=_cp("parallel"))(z, z, ln_g, ln_b, w_s, bmap)


def _sgu_bwd(name, dyb, z, ln_g, ln_b, w_s, bmap):
    T = z.shape[0]

    def body(dyb_ref, su_ref, sv_ref, lg_ref, lb_ref, w_ref, bm_ref, dz_ref, dlg_ref, dlb_ref, dw_ref, db_ref,
             vn_ref, du_ref, dvn_ref):
        i = pl.program_id(0)
        sv = sv_ref[...]
        xhat, rstd = _ln_stats(_gelu(sv))
        vn_ref[...] = (xhat * lg_ref[...] + lb_ref[...]).astype(bf16)
        r = lax.broadcasted_iota(jnp.int32, (HD, HD), 0)
        cc = lax.broadcasted_iota(jnp.int32, (HD, HD), 1)
        for g in range(NH):
            w = _tril_w(w_ref, g)
            cs = slice(g * HD, (g + 1) * HD)
            dw = jnp.zeros((HD, HD), f32)
            db = jnp.zeros((HD, 1), f32)
            for c in range(TT // HD):
                rs = slice(c * HD, (c + 1) * HD)
                vnb = vn_ref[rs, cs]
                s = jnp.dot(w, vnb, preferred_element_type=f32) + bm_ref[:, cs]
                dy = dyb_ref[rs, cs]
                du_ref[rs, cs] = dy * s
                ds = dy * _gelu(su_ref[rs, cs])
                ds16 = ds.astype(bf16)
                dvn_ref[rs, cs] = lax.dot_general(w, ds16, TN, preferred_element_type=f32)
                dw = dw + lax.dot_general(ds16, vnb, NT, preferred_element_type=f32)
                db = db + jnp.sum(ds, axis=1, keepdims=True)
            dw = jnp.where(cc <= r, dw, 0.0)

            @pl.when(i == 0)
            def _():
                dw_ref[g] = dw
                db_ref[:, g:g + 1] = db

            @pl.when(i > 0)
            def _():
                dw_ref[g] += dw
                db_ref[:, g:g + 1] += db

        dvn = dvn_ref[...]
        dxh = dvn * lg_ref[...]
        m1 = jnp.mean(dxh, axis=-1, keepdims=True)
        m2 = jnp.mean(dxh * xhat, axis=-1, keepdims=True)
        dvv = rstd * (dxh - m1 - xhat * m2)
        dz_ref[:, 0:D] = (du_ref[...] * _gelu_grad(su_ref[...])).astype(bf16)
        dz_ref[:, D:2 * D] = (dvv * _gelu_grad(sv)).astype(bf16)
        _acc_store(i, dlg_ref, jnp.sum(dvn * xhat, axis=0, keepdims=True))
        _acc_store(i, dlb_ref, jnp.sum(dvn, axis=0, keepdims=True))

    return pl.pallas_call(
        body, name=name, grid=(T // TT,),
        in_specs=[_row(), _row(3), _row(4), _vec(), _vec(), _bs((NH, HD, HD), lambda i: (0, 0, 0)), _vec(HD)],
        out_specs=[_row(0, 2 * D), _vec(), _vec(), _bs((NH, HD, HD), lambda i: (0, 0, 0)), _bs((HD, NH), lambda i: (0, 0))],
        out_shape=[S((T, 2 * D), bf16), S((1, D), f32), S((1, D), f32), S((NH, HD, HD), f32), S((HD, NH), f32)],
        scratch_shapes=[pltpu.VMEM((TT, D), bf16), pltpu.VMEM((TT, D), f32), pltpu.VMEM((TT, D), f32)],
        compiler_params=_cp("arbitrary"))(dyb, z, z, ln_g, ln_b, w_s, bmap)


def _qk_fwd(name, z, qg, kg, bf):
    T = z.shape[0]

    def body(q_ref, k_ref, v_ref, zf_ref, qg_ref, kg_ref, bf_ref, qn_ref, kn_ref, vb_ref, lf_ref):
        for h in range(NH):
            cs = slice(h * HD, (h + 1) * HD)
            for src, gain, dst in ((q_ref, qg_ref, qn_ref), (k_ref, kg_ref, kn_ref)):
                xv = src[:, cs]
                r = lax.rsqrt(jnp.mean(xv * xv, axis=-1, keepdims=True) + RMS_EPS)
                dst[:, cs] = (xv * r * gain[:, cs]).astype(bf16)
        vb_ref[...] = v_ref[...].astype(bf16)
        xf = zf_ref[...] + bf_ref[...]
        lf_ref[...] = jnp.minimum(xf, 0.0) - jnp.log1p(jnp.exp(-jnp.abs(xf)))

    return pl.pallas_call(
        body, name=name, grid=(T // TT,),
        in_specs=[_row(5), _row(6), _row(7), _bs((TT, 128), lambda i: (i, F_OFF // 128)), _vec(), _vec(), _vec(1, 128)],
        out_specs=[_row(), _row(), _row(), _bs((TT, 128), lambda i: (i, 0))],
        out_shape=[S((T, D), bf16), S((T, D), bf16), S((T, D), bf16), S((T, 128), f32)],
        compiler_params=_cp("parallel"))(z, z, z, z, qg, kg, bf)


def _cum_fwd(name, logf):
    T = logf.shape[0]

    def body(lf_ref, ccol_ref, crow_ref, c_ref):
        c = lf_ref[...]
        row = lax.broadcasted_iota(jnp.int32, c.shape, 0)
        s = 1
        while s < T:
            c = c + jnp.where(row >= s, pltpu.roll(c, s, 0), 0.0)
            s *= 2
        c_ref[...] = c
        crow_ref[...] = c.T[0:NH, :]
        for h in range(NH):
            ccol_ref[h] = jnp.broadcast_to(c_ref[:, h:h + 1], (T, 128))

    return pl.pallas_call(body, name=name, out_shape=[S((NH, T, 128), f32), S((NH, T), f32)],
                          scratch_shapes=[pltpu.VMEM((T, 128), f32)],
                          compiler_params=pltpu.CompilerParams(vmem_limit_bytes=VMEM_LIMIT))(logf)


ATT_TILE = 512


def _fold(x, op=jnp.add):
    acc = x[:, 0:128]
    for t in range(1, x.shape[1] // 128):
        acc = op(acc, x[:, t * 128:(t + 1) * 128])
    return acc


def _to_row(col):
    return jnp.broadcast_to(col, (col.shape[0], 128)).T[0:1, :]


def _causal(t, keys_down=False):
    r = lax.broadcasted_iota(jnp.int32, (t, t), 0)
    c = lax.broadcasted_iota(jnp.int32, (t, t), 1)
    return r <= c if keys_down else c <= r


def _attn_fwd(name, qn, kn, vb, ccol, crow3):
    T = qn.shape[0]
    tq = _tile(T, ATT_TILE)
    nq = T // tq

    def body(q_ref, k_ref, v_ref, cc_ref, cr_ref, o_ref, lse_ref, lser_ref, s_ref):
        qi = pl.program_id(1)
        q = q_ref[...]
        cq = cc_ref[:, 0:1]

        def logits(off):
            s = lax.dot_general(q, k_ref[pl.ds(off, tq), :], NT, preferred_element_type=f32) * ATT_SCALE
            return s + cq - cr_ref[:, pl.ds(off, tq)]

        def below(j, mvec):
            off = pl.multiple_of(j * tq, tq)
            s = logits(off)
            s_ref[:, pl.ds(off, tq)] = s
            return jnp.maximum(mvec, _fold(s, jnp.maximum))

        mvec = lax.fori_loop(0, qi, below, jnp.full((tq, 128), NEG, f32))
        off = pl.multiple_of(qi * tq, tq)
        s = jnp.where(_causal(tq), logits(off), NEG)
        s_ref[:, pl.ds(off, tq)] = s
        m = jnp.max(jnp.maximum(mvec, _fold(s, jnp.maximum)), axis=1, keepdims=True)

        def weigh(j, carry):
            lvec, acc = carry
            off = pl.multiple_of(j * tq, tq)
            p = jnp.exp(s_ref[:, pl.ds(off, tq)] - m)
            acc = acc + jnp.dot(p.astype(bf16), v_ref[pl.ds(off, tq), :], preferred_element_type=f32)
            return lvec + _fold(p), acc

        lvec, acc = lax.fori_loop(0, qi + 1, weigh, (jnp.zeros((tq, 128), f32), jnp.zeros((tq, HD), f32)))
        l = jnp.sum(lvec, axis=1, keepdims=True)
        o_ref[...] = acc / l
        lse = m + jnp.log(l)
        lse_ref[...] = jnp.broadcast_to(lse, (tq, 128))
        lser_ref[...] = _to_row(lse)

    return pl.pallas_call(
        body, name=name, grid=(NH, nq),
        in_specs=[_bs((tq, HD), lambda h, i: (i, h)), _bs((T, HD), lambda h, i: (0, h)), _bs((T, HD), lambda h, i: (0, h)),
                  _bs((None, tq, 128), lambda h, i: (h, i, 0)), _bs((None, 1, T), lambda h, i: (h, 0, 0))],
        out_specs=[_bs((tq, HD), lambda h, i: (i, h)), _bs((None, tq, 128), lambda h, i: (h, i, 0)),
                   _bs((None, 1, tq), lambda h, i: (h, 0, i))],
        out_shape=[S((T, D), f32), S((NH, T, 128), f32), S((NH, 1, T), f32)],
        scratch_shapes=[pltpu.VMEM((tq, T), f32)],
        compiler_params=_cp("parallel", "parallel"))(qn, kn, vb, ccol, crow3)


def _attn_dq(name, qn, kn, vb, do, lse, ccol, crow3):
    T = qn.shape[0]
    tq = _tile(T, ATT_TILE)
    nq = T // tq

    def body(q_ref, k_ref, v_ref, do_ref, lse_ref, cc_ref, cr_ref, dq_ref, dlr_ref, p_ref, dp_ref):
        qi = pl.program_id(1)
        q = q_ref[...]
        do16 = do_ref[...].astype(bf16)
        base = cc_ref[:, 0:1] - lse_ref[:, 0:1]

        def probs(off):
            s = lax.dot_general(q, k_ref[pl.ds(off, tq), :], NT, preferred_element_type=f32) * ATT_SCALE
            return jnp.exp(s + base - cr_ref[:, pl.ds(off, tq)])

        def keep(off, p, dvec):
            dp = lax.dot_general(do16, v_ref[pl.ds(off, tq), :], NT, preferred_element_type=f32)
            p_ref[:, pl.ds(off, tq)] = p
            dp_ref[:, pl.ds(off, tq)] = dp
            return dvec + _fold(p * dp)

        def below(j, dvec):
            off = pl.multiple_of(j * tq, tq)
            return keep(off, probs(off), dvec)

        dvec = lax.fori_loop(0, qi, below, jnp.zeros((tq, 128), f32))
        off = pl.multiple_of(qi * tq, tq)
        dvec = keep(off, jnp.where(_causal(tq), probs(off), 0.0), dvec)
        delta = jnp.sum(dvec, axis=1, keepdims=True)

        def grad(j, acc):
            off = pl.multiple_of(j * tq, tq)
            ds = p_ref[:, pl.ds(off, tq)] * (dp_ref[:, pl.ds(off, tq)] - delta)
            return acc + jnp.dot(ds.astype(bf16), k_ref[pl.ds(off, tq), :], preferred_element_type=f32)

        dq_ref[...] = lax.fori_loop(0, qi + 1, grad, jnp.zeros((tq, HD), f32)) * ATT_SCALE
        dlr_ref[...] = _to_row(delta)

    qb = lambda h, i: (i, h)
    full = lambda h, i: (0, h)
    col = lambda h, i: (h, i, 0)
    return pl.pallas_call(
        body, name=name, grid=(NH, nq),
        in_specs=[_bs((tq, HD), qb), _bs((T, HD), full), _bs((T, HD), full), _bs((tq, HD), qb),
                  _bs((None, tq, 128), col), _bs((None, tq, 128), col), _bs((None, 1, T), lambda h, i: (h, 0, 0))],
        out_specs=[_bs((tq, HD), qb), _bs((None, 1, tq), lambda h, i: (h, 0, i))],
        out_shape=[S((T, D), f32), S((NH, 1, T), f32)],
        scratch_shapes=[pltpu.VMEM((tq, T), f32), pltpu.VMEM((tq, T), f32)],
        compiler_params=_cp("parallel", "parallel"))(qn, kn, vb, do, lse, ccol, crow3)


def _attn_dkv(name, qn, kn, vb, do, lser3, dlr3, ccol, crow3):
    T = qn.shape[0]
    tk = _tile(T, ATT_TILE)
    nk = T // tk

    def body(q_ref, k_ref, v_ref, do_ref, lser_ref, dlr_ref, cc_ref, cr_ref, dk_ref, dv_ref, cs_ref):
        h = pl.program_id(0)
        kj = pl.program_id(1)

        @pl.when((h == 0) & (kj == 0))
        def _():
            cs_ref[...] = jnp.zeros_like(cs_ref)

        kb = k_ref[...]
        vv = v_ref[...]
        ckey = cc_ref[:, 0:1]

        def block(off, diagonal):
            rows = pl.ds(off, tk)
            qb = q_ref[rows, :]
            do16 = do_ref[rows, :].astype(bf16)
            st = lax.dot_general(kb, qb, NT, preferred_element_type=f32) * ATT_SCALE
            pt = jnp.exp(st + (cr_ref[:, rows] - lser_ref[:, rows]) - ckey)
            if diagonal:
                pt = jnp.where(_causal(tk, keys_down=True), pt, 0.0)
            dpt = lax.dot_general(vv, do16, NT, preferred_element_type=f32)
            dst = pt * (dpt - dlr_ref[:, rows])
            ddv = jnp.dot(pt.astype(bf16), do16, preferred_element_type=f32)
            ddk = jnp.dot(dst.astype(bf16), qb, preferred_element_type=f32)
            return ddk, ddv, _fold(dst)

        def above(i, carry):
            ddk, ddv, dcs = block(pl.multiple_of(i * tk, tk), False)
            return carry[0] + ddk, carry[1] + ddv, carry[2] + dcs

        off = pl.multiple_of(kj * tk, tk)
        dk, dv, cs = lax.fori_loop(kj + 1, nk, above, block(off, True))
        dk_ref[...] = dk * ATT_SCALE
        dv_ref[...] = dv
        lane = lax.broadcasted_iota(jnp.int32, (tk, 128), 1)
        cs_ref[pl.ds(off, tk), :] += jnp.where(lane == h, jnp.sum(cs, axis=1, keepdims=True), 0.0)

    full = lambda h, j: (0, h)
    blk = lambda h, j: (j, h)
    row = lambda h, j: (h, 0, 0)
    return pl.pallas_call(
        body, name=name, grid=(NH, nk),
        in_specs=[_bs((T, HD), full), _bs((tk, HD), blk), _bs((tk, HD), blk), _bs((T, HD), full), _bs((None, 1, T), row),
                  _bs((None, 1, T), row), _bs((None, tk, 128), lambda h, j: (h, j, 0)), _bs((None, 1, T), row)],
        out_specs=[_bs((tk, HD), blk), _bs((tk, HD), blk), _bs((T, 128), lambda h, j: (0, 0))],
        out_shape=[S((T, D), f32), S((T, D), f32), S((T, 128), f32)],
        compiler_params=_cp("arbitrary", "arbitrary"))(qn, kn, vb, do, lser3, dlr3, ccol, crow3)


def _forget_bwd(name, cs, z, bf):
    T = cs.shape[0]

    def body(cs_ref, zf_ref, bf_ref, dz_ref, db_ref):
        c = -cs_ref[...]
        row = lax.broadcasted_iota(jnp.int32, c.shape, 0)
        s = 1
        while s < T:
            c = c + jnp.where(row + s < T, pltpu.roll(c, T - s, 0), 0.0)
            s *= 2
        xf = zf_ref[...] + bf_ref[...]
        lane = lax.broadcasted_iota(jnp.int32, c.shape, 1)
        dxf = jnp.where(lane < NH, c / (1.0 + jnp.exp(xf)), 0.0)
        dz_ref[...] = jnp.zeros_like(dz_ref)
        dz_ref[:, 0:128] = dxf.astype(bf16)
        db_ref[...] = jnp.sum(dxf, axis=0, keepdims=True)

    return pl.pallas_call(
        body, name=name, grid=(1,),
        in_specs=[_bs((T, 128), lambda i: (0, 0)), _bs((T, 128), lambda i: (0, F_OFF // 128)), _vec(1, 128)],
        out_specs=[_bs((T, NZ - F_OFF), lambda i: (0, 0)), _vec(1, 128)],
        out_shape=[S((T, NZ - F_OFF), bf16), S((1, 128), f32)], compiler_params=_cp("arbitrary"))(cs, z, bf)


def _qk_bwd(name, dqn, dkn, dv, z, qg, kg):
    T = z.shape[0]

    def body(dq_ref, dk_ref, dv_ref, q_ref, k_ref, qg_ref, kg_ref, dz_ref, dqg_ref, dkg_ref, g_ref):
        i = pl.program_id(0)
        for n, (src, dsrc, gain, dgain) in enumerate(((q_ref, dq_ref, qg_ref, dqg_ref), (k_ref, dk_ref, kg_ref, dkg_ref))):
            for h in range(NH):
                cs = slice(h * HD, (h + 1) * HD)
                xv = src[:, cs]
                r = lax.rsqrt(jnp.mean(xv * xv, axis=-1, keepdims=True) + RMS_EPS)
                xhat = xv * r
                dy = dsrc[:, cs]
                dyg = dy * gain[:, cs]
                m = jnp.mean(dyg * xhat, axis=-1, keepdims=True)
                dz_ref[:, n * D + h * HD:n * D + (h + 1) * HD] = (r * (dyg - xhat * m)).astype(bf16)
                g_ref[:, cs] = jnp.sum(dy * xhat, axis=0, keepdims=True)
            _acc_store(i, dgain, g_ref[...])
        dz_ref[:, 2 * D:3 * D] = dv_ref[...].astype(bf16)

    return pl.pallas_call(
        body, name=name, grid=(T // TT,),
        in_specs=[_row(), _row(), _row(), _row(5), _row(6), _vec(), _vec()],
        out_specs=[_row(0, 3 * D), _vec(), _vec()], out_shape=[S((T, 3 * D), bf16), S((1, D), f32), S((1, D), f32)],
        scratch_shapes=[pltpu.VMEM((1, D), f32)], compiler_params=_cp("arbitrary"))(dqn, dkn, dv, z, z, qg, kg)


GB = GATE_OFF // D


def _merge_fwd(name, ya, yb, yc, z, bg):
    T = z.shape[0]

    def body(ya_ref, yb_ref, yc_ref, g0_ref, g1_ref, g2_ref, bg_ref, o_ref):
        acc = _sigmoid(g0_ref[...] + bg_ref[0:1, :]) * ya_ref[...]
        acc = acc + _sigmoid(g1_ref[...] + bg_ref[1:2, :]) * yb_ref[...]
        acc = acc + _sigmoid(g2_ref[...] + bg_ref[2:3, :]) * yc_ref[...]
        o_ref[...] = acc.astype(bf16)

    return pl.pallas_call(
        body, name=name, grid=(T // TT,),
        in_specs=[_row(), _row(), _row(), _row(GB), _row(GB + 1), _row(GB + 2), _vec(3)],
        out_specs=_row(), out_shape=S((T, D), bf16), compiler_params=_cp("parallel"))(ya, yb, yc, z, z, z, bg)


def _merge_bwd(name, dm, ya, yb, yc, z, bg):
    T = z.shape[0]

    def body(dm_ref, ya_ref, yb_ref, yc_ref, g0_ref, g1_ref, g2_ref, bg_ref, dya_ref, dyb_ref, dyc_ref, dz_ref, db_ref):
        i = pl.program_id(0)
        dm_v = dm_ref[...]
        dbs = []
        for n, (y_ref, g_ref, dy_ref) in enumerate(((ya_ref, g0_ref, dya_ref), (yb_ref, g1_ref, dyb_ref),
                                                    (yc_ref, g2_ref, dyc_ref))):
            gate = _sigmoid(g_ref[...] + bg_ref[n:n + 1, :])
            dy_ref[...] = (dm_v * gate).astype(bf16)
            dl = dm_v * y_ref[...] * gate * (1.0 - gate)
            dz_ref[:, n * D:(n + 1) * D] = dl.astype(bf16)
            dbs.append(jnp.sum(dl, axis=0, keepdims=True))
        _acc_store(i, db_ref, jnp.concatenate(dbs, axis=0))

    return pl.pallas_call(
        body, name=name, grid=(T // TT,),
        in_specs=[_row(), _row(), _row(), _row(), _row(GB), _row(GB + 1), _row(GB + 2), _vec(3)],
        out_specs=[_row(), _row(), _row(), _row(0, 3 * D), _vec(3)],
        out_shape=[S((T, D), bf16), S((T, D), bf16), S((T, D), bf16), S((T, 3 * D), bf16), S((3, D), f32)],
        compiler_params=_cp("arbitrary"))(dm, ya, yb, yc, z, z, z, bg)


SMALL_NAMES = ("ffn1_norm", "mix_norm", "b_forget", "b_gate", "conv_w", "sgu_ln_g", "sgu_ln_b", "sgu_w", "sgu_b",
               "q_norm_g", "k_norm_g", "ffn2_norm")


def _small_params(p):
    out = {n: p[n].reshape(1, D) for n in ("ffn1_norm", "mix_norm", "ffn2_norm", "sgu_ln_g", "sgu_ln_b", "q_norm_g", "k_norm_g")}
    out["b_forget"] = jnp.pad(p["b_forget"].reshape(1, NH), ((0, 0), (0, 128 - NH)))
    out["b_gate"] = p["b_gate"]
    out["conv_w"] = p["conv_w"]
    out["sgu_w"] = p["sgu_w"]
    out["bmap"] = jnp.repeat(p["sgu_b"].T, HD, axis=1)
    return out


def _small_grads_natural(sg):
    out = {n: sg[n].reshape(D) for n in ("ffn1_norm", "mix_norm", "ffn2_norm", "sgu_ln_g", "sgu_ln_b")}
    out["q_norm_g"] = sg["q_norm_g"].reshape(NH, HD)
    out["k_norm_g"] = sg["k_norm_g"].reshape(NH, HD)
    out["b_forget"] = sg["b_forget"][0, :NH]
    out["b_gate"] = sg["b_gate"]
    out["conv_w"] = sg["conv_w"]
    out["sgu_w"] = sg["sgu_w"]
    out["sgu_b"] = sg["sgu_b"]
    return out


SQ_TM = 1024


def _sq_fwd(name, a, wsq, l, n, res=None):
    T = a.shape[0]
    tm = _tile(T, SQ_TM)
    return _mm(name, a, wsq, grid=(T // tm, 1, 1), a_spec=_bs((tm, D), lambda i, j, k: (i, 0)),
               b_spec=_bs((None, None, D, D), lambda i, j, k: (l, n, 0, 0)),
               out_shape=S((T, D), f32), out_spec=_bs((tm, D), lambda i, j, k: (i, 0)), dims=NN, acc_shape=None,
               res=res, res_spec=_bs((tm, D), lambda i, j, k: (i, 0)))


def _sq_bwd_in(name, dy, wsq, l, n):
    T = dy.shape[0]
    tm = _tile(T, SQ_TM)
    return _mm(name, dy, wsq, grid=(T // tm, 1, 1), a_spec=_bs((tm, D), lambda i, j, k: (i, 0)),
               b_spec=_bs((None, None, D, D), lambda i, j, k: (l, n, 0, 0)),
               out_shape=S((T, D), f32), out_spec=_bs((tm, D), lambda i, j, k: (i, 0)), dims=NT, acc_shape=None)


def _sq_bwd_w(name, a, dy, gbuf, l):
    T = a.shape[0]
    return _mm(name, a, dy, grid=(NDEV // 2, 1, 1), a_spec=_bs((T, 256), lambda i, j, k: (0, i)),
               b_spec=_bs((T, D), lambda i, j, k: (0, 0)), out_shape=S(gbuf.shape, bf16),
               out_spec=_bs((2, None, None, 128, D), lambda i, j, k: (0, i, l, 0, 0)),
               dims=TN, acc_shape=None, alias=gbuf, split_rows=128)


def _ffn_fwd(tag, x, g, wgu, wd, l):
    T = x.shape[0]
    tm = _tile(T, 2048)
    h = _rms_fwd(tag + "_rms", x, g)
    gu = _mm(tag + "_gu", h, wgu, grid=(T // tm, NDEV, 1), a_spec=_bs((tm, D), lambda i, j, k: (i, 0)),
             b_spec=_bs((None, None, D, GU), lambda i, j, k: (l, j, 0, 0)), out_shape=S((NDEV, T, GU), f32),
             out_spec=_bs((None, tm, GU), lambda i, j, k: (j, i, 0)), dims=NN, acc_shape=None)
    a = _swiglu_fwd(tag + "_act", gu)
    tm = _tile(T, 1024)
    xo = _mm(tag + "_down", a, wd, grid=(T // tm, 1, 4), a_spec=_bs((None, tm, GU), lambda i, j, k: (k, i, 0)),
             b_spec=_bs((None, None, GU, D), lambda i, j, k: (l, k, 0, 0)), out_shape=S((T, D), f32),
             out_spec=_bs((tm, D), lambda i, j, k: (i, 0)), dims=NN, acc_shape=(tm, D), res=x,
             res_spec=_bs((tm, D), lambda i, j, k: (i, 0)), alpha=0.5)
    return xo, (h, gu, a)


def _ffn_bwd(tag, dxo, x, g, wgu, wd, l, saved, g_gu, g_d):
    h, gu, a = saved
    T = x.shape[0]
    tm = _tile(T, 2048)
    da = _mm(tag + "_dact", dxo, wd, grid=(T // tm, 4, 1), a_spec=_bs((tm, D), lambda i, j, k: (i, 0)),
             b_spec=_bs((None, None, GU, D), lambda i, j, k: (l, j, 0, 0)), out_shape=S((4, T, GU), f32),
             out_spec=_bs((None, tm, GU), lambda i, j, k: (j, i, 0)), dims=NT, acc_shape=None, alpha=0.5)
    g_d = _mm(tag + "_dwd", a, dxo, grid=(4, 1, 1), a_spec=_bs((None, T, GU), lambda i, j, k: (i, 0, 0)),
              b_spec=_bs((T, D), lambda i, j, k: (0, 0)), out_shape=S(g_d.shape, bf16),
              out_spec=_bs((2, None, None, GU // 2, D), lambda i, j, k: (0, i, l, 0, 0)), dims=TN, acc_shape=None,
              alpha=0.5, alias=g_d, split_rows=GU // 2)
    dgu = _swiglu_bwd(tag + "_dgu", gu, da).reshape(NDEV, T, GU)
    dh = _mm(tag + "_dh", dgu, wgu, grid=(1, 1, NDEV), a_spec=_bs((None, T, GU), lambda i, j, k: (k, 0, 0)),
             b_spec=_bs((None, None, D, GU), lambda i, j, k: (l, k, 0, 0)), out_shape=S((T, D), f32),
             out_spec=_bs((T, D), lambda i, j, k: (0, 0)), dims=NT, acc_shape=(T, D))
    g_gu = _mm(tag + "_dwgu", h, dgu, grid=(1, NDEV, 1), a_spec=_bs((T, D), lambda i, j, k: (0, 0)),
               b_spec=_bs((None, T, GU), lambda i, j, k: (j, 0, 0)), out_shape=S(g_gu.shape, bf16),
               out_spec=_bs((None, None, None, D, GU), lambda i, j, k: (j % 2, j // 2, l, 0, 0)), dims=TN,
               acc_shape=None, alias=g_gu)
    dx, dg = _rms_bwd(tag + "_drms", dh, x, g, dxo)
    return dx, dg, g_gu, g_d


def _mixer_fwd(tag, x, p, win, wsq, l):
    T = x.shape[0]
    h = _rms_fwd(tag + "_rms", x, p["mix_norm"])
    tn = 512
    z = _mm(tag + "_in", h, win, grid=(1, NZ // tn, 1), a_spec=_bs((T, D), lambda i, j, k: (0, 0)),
            b_spec=_bs((D, tn), lambda i, j, k: (0, j)), out_shape=S((T, NZ), f32),
            out_spec=_bs((T, tn), lambda i, j, k: (0, j)), dims=NN, acc_shape=None)
    ya_in = _conv_fwd(tag + "_conv", z, p["conv_w"])
    yb_in = _sgu_fwd(tag + "_sgu", z, p["sgu_ln_g"], p["sgu_ln_b"], p["sgu_w"], p["bmap"])
    qn, kn, vb, logf = _qk_fwd(tag + "_qk", z, p["q_norm_g"], p["k_norm_g"], p["b_forget"])
    ccol, crow = _cum_fwd(tag + "_cum", logf)
    crow3 = crow.reshape(NH, 1, T)
    o, lse, lser = _attn_fwd(tag + "_attn", qn, kn, vb, ccol, crow3)
    ya = _sq_fwd(tag + "_oconv", ya_in, wsq, l, 0)
    yb = _sq_fwd(tag + "_osgu", yb_in, wsq, l, 1)
    yc = _sq_fwd(tag + "_oattn", o, wsq, l, 2)
    merged = _merge_fwd(tag + "_merge", ya, yb, yc, z, p["b_gate"])
    xo = _sq_fwd(tag + "_o", merged, wsq, l, 3, res=x)
    return xo, (h, z, ya_in, yb_in, qn, kn, vb, ccol, crow3, o, lse, lser, ya, yb, yc, merged)


def _mixer_bwd(tag, dxo, x, p, win, wsq, l, saved, gsq):
    h, z, ya_in, yb_in, qn, kn, vb, ccol, crow3, o, lse, lser, ya, yb, yc, merged = saved
    T = x.shape[0]
    sg = {}
    dm = _sq_bwd_in(tag + "_dmerged", dxo, wsq, l, 3)
    gsq[3] = _sq_bwd_w(tag + "_dwo", merged, dxo, gsq[3], l)
    dya, dyb, dyc, dz_g, sg["b_gate"] = _merge_bwd(tag + "_dmerge", dm, ya, yb, yc, z, p["b_gate"])
    d_ya_in = _sq_bwd_in(tag + "_dconv_in", dya, wsq, l, 0)
    gsq[0] = _sq_bwd_w(tag + "_dwoc", ya_in, dya, gsq[0], l)
    d_yb_in = _sq_bwd_in(tag + "_dsgu_in", dyb, wsq, l, 1)
    gsq[1] = _sq_bwd_w(tag + "_dwos", yb_in, dyb, gsq[1], l)
    d_o = _sq_bwd_in(tag + "_dattn_in", dyc, wsq, l, 2)
    gsq[2] = _sq_bwd_w(tag + "_dwoa", o, dyc, gsq[2], l)
    dz_c, sg["conv_w"] = _conv_bwd(tag + "_dconv", d_ya_in, z, p["conv_w"])
    dz_s, sg["sgu_ln_g"], sg["sgu_ln_b"], sg["sgu_w"], db_t = _sgu_bwd(
        tag + "_dsgu", d_yb_in, z, p["sgu_ln_g"], p["sgu_ln_b"], p["sgu_w"], p["bmap"])
    sg["sgu_b"] = db_t.T
    dqn, dlr = _attn_dq(tag + "_dattn_q", qn, kn, vb, d_o, lse, ccol, crow3)
    dkn, dv, cs = _attn_dkv(tag + "_dattn_kv", qn, kn, vb, d_o, lser, dlr, ccol, crow3)
    dz_f, sg["b_forget"] = _forget_bwd(tag + "_dforget", cs, z, p["b_forget"])
    dz_q, sg["q_norm_g"], sg["k_norm_g"] = _qk_bwd(tag + "_dqk", dqn, dkn, dv, z, p["q_norm_g"], p["k_norm_g"])
    dz = jnp.concatenate([dz_c, dz_s, dz_q, dz_g, dz_f], axis=1)
    tk = 512
    dh = _mm(tag + "_dh", dz, win, grid=(1, 1, NZ // tk), a_spec=_bs((T, tk), lambda i, j, k: (0, k)),
             b_spec=_bs((D, tk), lambda i, j, k: (0, k)), out_shape=S((T, D), f32),
             out_spec=_bs((T, D), lambda i, j, k: (0, 0)), dims=NT, acc_shape=(T, D))
    tn = 512
    dwin = _mm(tag + "_dwin", h, dz, grid=(1, NZ // tn, 1), a_spec=_bs((T, D), lambda i, j, k: (0, 0)),
               b_spec=_bs((T, tn), lambda i, j, k: (0, j)), out_shape=S((D, NZ), bf16),
               out_spec=_bs((D, tn), lambda i, j, k: (0, j)), dims=TN, acc_shape=None)
    dx, sg["mix_norm"] = _rms_bwd(tag + "_drms", dh, x, p["mix_norm"], dxo)
    return dx, sg, dwin, gsq


ANY = pl.BlockSpec(memory_space=pl.ANY)
HBM = pl.BlockSpec(memory_space=pltpu.HBM)
SEM = pl.BlockSpec(memory_space=pltpu.SEMAPHORE)
EFFECT = pltpu.SideEffectType.DATAFLOW_SIDE_EFFECTING


def _place():
    return lax.axis_index("x"), lax.axis_index("y"), lax.axis_index("c")


NEAR = 4


def _others(x, y, c):
    return [(x, y, 1 - c), (1 - x, y, c), (x, 1 - y, c), (1 - x, 1 - y, c)]


def _gather_start(name, groups):
    sizes = [len(g) for g in groups]
    srcs = [s for g in groups for s, _ in g]
    lands = [b for g in groups for _, b in g]
    n, ng = len(srcs), len(groups)

    def body(*refs):
        src_refs, land_refs = refs[:n], refs[n:2 * n]
        send, recv = refs[2 * n:2 * n + ng], refs[2 * n + ng:2 * n + 2 * ng]
        x, y, c = _place()
        me = 4 * x + 2 * y + c
        u = 0
        for g, size in enumerate(sizes):
            for i in range(size):
                for k, peer in enumerate(_others(x, y, c)):
                    pltpu.make_async_remote_copy(src_ref=src_refs[u], dst_ref=land_refs[u].at[me],
                                                 send_sem=send[g].at[i * NEAR + k], recv_sem=recv[g].at[i * NEAR + k],
                                                 device_id=peer, device_id_type=MESH).start()
                u += 1

    sems = [pltpu.SemaphoreType.DMA((size * NEAR,)) for size in sizes]
    out = pl.pallas_call(
        body, name=name, in_specs=[HBM] * (2 * n), out_specs=[SEM] * (2 * ng) + [HBM] * (2 * n),
        out_shape=sems + sems + [pltpu.HBM(a.shape, a.dtype) for a in srcs + lands],
        input_output_aliases={i: 2 * ng + i for i in range(2 * n)},
        compiler_params=pltpu.CompilerParams(has_side_effects=EFFECT),
    )(*[pltpu.with_memory_space_constraint(a, pltpu.HBM) for a in srcs + lands])
    res, pos = [], 0
    for g, size in enumerate(sizes):
        res.append((out[g], out[ng + g], out[2 * ng + pos:2 * ng + pos + size], out[2 * ng + n + pos:2 * ng + n + pos + size]))
        pos += size
    return res


def _gather_wait(name, started, after=None):
    send, recv, srcs, lands = started
    n = len(srcs)

    def body(*refs):
        src_refs, land_refs = refs[:n], refs[n:2 * n]
        send_ref, recv_ref = refs[2 * n], refs[2 * n + 1]
        x, y, c = _place()
        for i in range(n):
            for k, (px, py, pc) in enumerate(_others(x, y, c)):
                cp = pltpu.make_async_remote_copy(src_ref=src_refs[i], dst_ref=land_refs[i].at[4 * px + 2 * py + pc],
                                                  send_sem=send_ref.at[i * NEAR + k], recv_sem=recv_ref.at[i * NEAR + k],
                                                  device_id=(px, py, pc), device_id_type=MESH)
                cp.wait_send()
                cp.wait_recv()

    extra = [] if after is None else [after]
    out = pl.pallas_call(
        body, name=name, in_specs=[HBM] * (2 * n) + [SEM, SEM] + [ANY] * len(extra), out_specs=[HBM] * (2 * n),
        out_shape=[pltpu.HBM(a.shape, a.dtype) for a in list(srcs) + list(lands)],
        input_output_aliases={i: i for i in range(2 * n)},
        compiler_params=pltpu.CompilerParams(has_side_effects=EFFECT),
    )(*srcs, *lands, send, recv, *extra)
    return out[n:]


def _gather_forward(name, lands):
    n = len(lands)

    def body(*refs):
        have, full = refs[:n], refs[n:2 * n]
        send, recv = refs[2 * n], refs[2 * n + 1]
        x, y, c = _place()
        chips = [(1 - x, y), (x, 1 - y), (1 - x, 1 - y)]

        def copy(i, j, core):
            slot = 4 * chips[j][0] + 2 * chips[j][1] + core
            return pltpu.make_async_remote_copy(src_ref=have[i].at[slot], dst_ref=full[i].at[slot],
                                                send_sem=send.at[i * 3 + j], recv_sem=recv.at[i * 3 + j],
                                                device_id=(x, y, 1 - c), device_id_type=MESH)

        for i in range(n):
            for j in range(3):
                copy(i, j, c).start()
        for i in range(n):
            for j in range(3):
                copy(i, j, c).wait_send()
                copy(i, j, 1 - c).wait_recv()

    return pl.pallas_call(
        body, name=name, in_specs=[ANY] * n, out_specs=[ANY] * n, out_shape=[S(a.shape, a.dtype) for a in lands],
        input_output_aliases={i: i for i in range(n)},
        scratch_shapes=[pltpu.SemaphoreType.DMA((n * 3,)), pltpu.SemaphoreType.DMA((n * 3,))],
    )(*lands)


def _rs_pair(name, gs):
    n = len(gs)

    def body(*refs):
        ins, outs = refs[:n], refs[n:2 * n]
        send, recv = refs[2 * n:]
        x, y, c = _place()
        cps = [pltpu.make_async_remote_copy(src_ref=ins[u].at[1 - c], dst_ref=outs[u], send_sem=send.at[u],
                                            recv_sem=recv.at[u], device_id=(x, y, 1 - c), device_id_type=MESH)
               for u in range(n)]
        for cp in cps:
            cp.start()
        for cp in cps:
            cp.wait()

    return pl.pallas_call(
        body, name=name, in_specs=[ANY] * n, out_specs=[ANY] * n, out_shape=[S(g.shape[1:], g.dtype) for g in gs],
        scratch_shapes=[pltpu.SemaphoreType.DMA((n,)), pltpu.SemaphoreType.DMA((n,))],
    )(*gs)


def _row_tile(r, c):
    return 128 if (r % 128 == 0 and c > D) else (256 if r % 256 == 0 else r)


def _pair_sum(name, core, g, r1):
    _, nq, nl, r, c = g.shape
    tr = _row_tile(r, c)
    g4 = g.reshape(2, nq * nl, r, c)
    r3 = r1.reshape(nq * nl, r, c)

    def body(core_ref, g_ref, r_ref, o_ref):
        o_ref[...] = (g_ref[...].astype(f32) + r_ref[...].astype(f32)).astype(bf16)

    out = pl.pallas_call(
        body, name=name,
        grid_spec=pltpu.PrefetchScalarGridSpec(
            num_scalar_prefetch=1, grid=(nq * nl, r // tr),
            in_specs=[_bs((None, None, tr, c), lambda b, i, cr: (cr[0], b, i, 0)), _bs((None, tr, c), lambda b, i, cr: (b, i, 0))],
            out_specs=_bs((None, tr, c), lambda b, i, cr: (b, i, 0))),
        out_shape=S((nq * nl, r, c), bf16), compiler_params=_cp("parallel", "parallel"))(core, g4, r3)
    return out.reshape(nq, nl, r, c)


def _scatter_copies(ins, outs, send, recv):
    x, y, c = _place()
    chips = [(1 - x, y), (x, 1 - y), (1 - x, 1 - y)]
    return [pltpu.make_async_remote_copy(src_ref=ins[u].at[2 * chip[0] + chip[1]], dst_ref=outs[u].at[k],
                                         send_sem=send.at[u * 3 + k], recv_sem=recv.at[u * 3 + k],
                                         device_id=(*chip, c), device_id_type=MESH)
            for u in range(len(ins)) for k, chip in enumerate(chips)]


def _scatter_start(name, ss, carry):
    n = len(ss)
    lands = [lax.empty((3,) + s.shape[1:], s.dtype) for s in ss]
    held = list(ss) + lands + [carry]

    def body(*refs):
        for cp in _scatter_copies(refs[:n], refs[n:2 * n], refs[2 * n + 1], refs[2 * n + 2]):
            cp.start()

    sems = [pltpu.SemaphoreType.DMA((n * 3,))] * 2
    out = pl.pallas_call(
        body, name=name, in_specs=[HBM] * len(held), out_specs=[SEM, SEM] + [HBM] * len(held),
        out_shape=sems + [pltpu.HBM(a.shape, a.dtype) for a in held],
        input_output_aliases={i: 2 + i for i in range(len(held))},
        compiler_params=pltpu.CompilerParams(has_side_effects=EFFECT),
    )(*[pltpu.with_memory_space_constraint(a, pltpu.HBM) for a in held])
    return (out[0], out[1], out[2:2 + n], out[2 + n:2 + 2 * n]), out[2 + 2 * n]


def _scatter_wait(name, started, after):
    send, recv, srcs, lands = started
    n = len(srcs)

    def body(*refs):
        for cp in _scatter_copies(refs[:n], refs[n:2 * n], refs[2 * n], refs[2 * n + 1]):
            cp.wait_send()
            cp.wait_recv()

    out = pl.pallas_call(
        body, name=name, in_specs=[HBM] * (2 * n) + [SEM, SEM, ANY], out_specs=[HBM] * (2 * n),
        out_shape=[pltpu.HBM(a.shape, a.dtype) for a in list(srcs) + list(lands)],
        input_output_aliases={i: i for i in range(2 * n)},
        compiler_params=pltpu.CompilerParams(has_side_effects=EFFECT),
    )(*srcs, *lands, send, recv, after)
    return out[:n], out[n:]


def _all_reduce_small(name, v):
    r = v.shape[0]

    def body(v_ref, o_ref, buf, send, recv, lsem):
        x, y, c = _place()
        me, sib = (x, y, c), (x, y, 1 - c)
        chips = [(1 - x, y), (x, 1 - y), (1 - x, 1 - y)]

        def slot(px, py, pc):
            return buf.at[4 * px + 2 * py + pc]

        def copy(k, block, to, src=None):
            return pltpu.make_async_remote_copy(src_ref=slot(*block) if src is None else src, dst_ref=slot(*block),
                                                send_sem=send.at[k], recv_sem=recv.at[k], device_id=to,
                                                device_id_type=MESH)

        mine = pltpu.make_async_copy(v_ref, slot(*me), lsem)
        mine.start()
        first = [copy(0, me, sib, v_ref)] + [copy(1 + j, me, (*chip, c), v_ref) for j, chip in enumerate(chips)]
        for cp in first:
            cp.start()
        passed = [copy(4 + j, (*chip, c), sib) for j, chip in enumerate(chips)]
        for j, chip in enumerate(chips):
            copy(1 + j, (*chip, c), me).wait_recv()
            passed[j].start()
        copy(0, sib, me).wait_recv()
        for j, chip in enumerate(chips):
            copy(4 + j, (*chip, 1 - c), me).wait_recv()
        for cp in first + passed:
            cp.wait_send()
        mine.wait()
        acc = buf[0]
        for d in range(1, NDEV):
            acc = acc + buf[d]
        o_ref[...] = acc

    vm = pl.BlockSpec(memory_space=pltpu.VMEM)
    return pl.pallas_call(
        body, name=name, in_specs=[vm], out_specs=vm, out_shape=S((r, 128), f32),
        scratch_shapes=[pltpu.VMEM((NDEV, r, 128), f32), pltpu.SemaphoreType.DMA((7,)), pltpu.SemaphoreType.DMA((7,)),
                        pltpu.SemaphoreType.DMA],
        compiler_params=pltpu.CompilerParams(vmem_limit_bytes=VMEM_LIMIT),
    )(v)


def _adam_math(w, g, m, v):
    m = ADAM_B1 * m + (1.0 - ADAM_B1) * g
    v = ADAM_B2 * v + (1.0 - ADAM_B2) * (g * g)
    m_hat = m / (1.0 - ADAM_B1 ** ADAM_STEP)
    v_hat = v / (1.0 - ADAM_B2 ** ADAM_STEP)
    delta = -ADAM_LR * (m_hat / (jnp.sqrt(v_hat) + ADAM_EPS) + ADAM_WD * w)
    return delta, m, v


def _adamw(name, w, m, v, parts):
    _, r, c = w.shape
    tr = _row_tile(r, c)

    def body(w_ref, m_ref, v_ref, *refs):
        sets, (g_ref, d_ref, mo_ref, vo_ref) = (refs[0:4], refs[4:8]), refs[8:]
        for l in range(2):
            @pl.when(pl.program_id(0) == l)
            def _():
                s_ref, r0_ref, r1_ref, r2_ref = sets[l]
                g = ((s_ref[...].astype(f32) + r0_ref[...].astype(f32)) + r1_ref[...].astype(f32)) + r2_ref[...].astype(f32)
                g_ref[...] = g
                d_ref[...], mo_ref[...], vo_ref[...] = _adam_math(w_ref[...], g, m_ref[...], v_ref[...])

    blk = _bs((None, tr, c), lambda l, i: (l, i, 0))
    operands, specs = [], []
    for n in range(2):
        row = (lambda l, i: i * (1 - l)) if n == 0 else (lambda l, i: i * l)
        s_mine, r2 = parts[n]
        operands += [s_mine, r2, r2, r2]
        specs.append(_bs((tr, c), functools.partial(lambda l, i, row: (row(l, i), 0), row=row)))
        specs += [_bs((None, None, tr, c), functools.partial(lambda l, i, k, row: (k, 0, row(l, i), 0), k=k, row=row))
                  for k in range(3)]
    return pl.pallas_call(
        body, name=name, grid=(2, r // tr), in_specs=[blk, blk, blk] + specs,
        out_specs=[blk] * 4, out_shape=[S(w.shape, f32)] * 4, compiler_params=_cp("arbitrary", "arbitrary"),
    )(w, m, v, *operands)


def _adamw_small(name, w, g, m, v):
    def body(w_ref, g_ref, m_ref, v_ref, d_ref, mo_ref, vo_ref):
        d_ref[...], mo_ref[...], vo_ref[...] = _adam_math(w_ref[...], g_ref[...], m_ref[...], v_ref[...])

    return pl.pallas_call(body, name=name, out_shape=[S(w.shape, f32)] * 3,
                          compiler_params=pltpu.CompilerParams(vmem_limit_bytes=VMEM_LIMIT))(w, g, m, v)


WEIGHT_NAMES = ("ffn1_norm", "ffn1_w_gu", "ffn1_w_down", "mix_norm", "w_in", "b_forget", "b_gate", "conv_w", "sgu_ln_g",
                "sgu_ln_b", "sgu_w", "sgu_b", "q_norm_g", "k_norm_g", "w_out_conv", "w_out_sgu", "w_out_attn", "w_o",
                "ffn2_norm", "ffn2_w_gu", "ffn2_w_down")
BIG = {"ffn1_w_gu": "gu1", "ffn2_w_gu": "gu2", "ffn1_w_down": "d1", "ffn2_w_down": "d2", "w_in": "in",
       "w_out_conv": "oc", "w_out_sgu": "os", "w_out_attn": "oa", "w_o": "o"}
BIG_KEYS = ("gu1", "gu2", "d1", "d2", "in", "oc", "os", "oa", "o")
REPLICATED_SMALL = ("ffn1_norm", "mix_norm", "b_forget", "sgu_ln_g", "sgu_ln_b", "sgu_w", "sgu_b", "q_norm_g",
                    "k_norm_g", "ffn2_norm")
SHARDED_SMALL = ("b_gate", "conv_w")


def _pack(arrays):
    flat = jnp.concatenate([a.reshape(-1).astype(f32) for a in arrays])
    rows = -(-flat.shape[0] // 1024) * 8
    return jnp.pad(flat, (0, rows * 128 - flat.shape[0])).reshape(rows, 128)


def _unpack(packed, shapes):
    flat = packed.reshape(-1)
    out, pos = [], 0
    for shp in shapes:
        size = 1
        for s_ in shp:
            size *= s_
        out.append(flat[pos:pos + size].reshape(shp))
        pos += size
    return out


def _natural_runs(a, b):
    runs = []
    while a < b:
        d = a // INB
        e = min(b, (d + 1) * INB)
        runs.append((d, a - d * INB, e - d * INB))
        a = e
    return runs


def _win_kernel_layout(wg):
    runs = _natural_runs(0, GATE_OFF) + _natural_runs(GATE_OFF + NH, NIN) + _natural_runs(GATE_OFF, GATE_OFF + NH)
    return jnp.concatenate([wg[d, :, a:b] for d, a, b in runs] + [jnp.zeros((D, NZ - NIN), wg.dtype)], axis=1)


def _kernel_column(n):
    return n if n < GATE_OFF else (F_OFF + n - GATE_OFF if n < GATE_OFF + NH else n - NH)


def _win_device_block(dw, d):
    cuts = sorted({d * INB, (d + 1) * INB} | {c for c in (GATE_OFF, GATE_OFF + NH) if d * INB < c < (d + 1) * INB})
    parts = [dw[:, _kernel_column(a):_kernel_column(a) + (b - a)] for a, b in zip(cuts[:-1], cuts[1:])]
    return parts[0] if len(parts) == 1 else jnp.concatenate(parts, axis=1)


def kernel(x, ffn1_norm, ffn1_w_gu, ffn1_w_down, mix_norm, w_in, b_forget, b_gate, conv_w, sgu_ln_g, sgu_ln_b, sgu_w, sgu_b, q_norm_g, k_norm_g, w_out_conv, w_out_sgu, w_out_attn, w_o, ffn2_norm, ffn2_w_gu, ffn2_w_down, loss_target, m_ffn1_norm, m_ffn1_w_gu, m_ffn1_w_down, m_mix_norm, m_w_in, m_b_forget, m_b_gate, m_conv_w, m_sgu_ln_g, m_sgu_ln_b, m_sgu_w, m_sgu_b, m_q_norm_g, m_k_norm_g, m_w_out_conv, m_w_out_sgu, m_w_out_attn, m_w_o, m_ffn2_norm, m_ffn2_w_gu, m_ffn2_w_down, v_ffn1_norm, v_ffn1_w_gu, v_ffn1_w_down, v_mix_norm, v_w_in, v_b_forget, v_b_gate, v_conv_w, v_sgu_ln_g, v_sgu_ln_b, v_sgu_w, v_sgu_b, v_q_norm_g, v_k_norm_g, v_w_out_conv, v_w_out_sgu, v_w_out_attn, v_w_o, v_ffn2_norm, v_ffn2_w_gu, v_ffn2_w_down):
    w = dict(zip(WEIGHT_NAMES, (ffn1_norm, ffn1_w_gu, ffn1_w_down, mix_norm, w_in, b_forget, b_gate, conv_w, sgu_ln_g,
                                sgu_ln_b, sgu_w, sgu_b, q_norm_g, k_norm_g, w_out_conv, w_out_sgu, w_out_attn, w_o,
                                ffn2_norm, ffn2_w_gu, ffn2_w_down)))
    mom = dict(zip(WEIGHT_NAMES, (m_ffn1_norm, m_ffn1_w_gu, m_ffn1_w_down, m_mix_norm, m_w_in, m_b_forget, m_b_gate,
                                  m_conv_w, m_sgu_ln_g, m_sgu_ln_b, m_sgu_w, m_sgu_b, m_q_norm_g, m_k_norm_g,
                                  m_w_out_conv, m_w_out_sgu, m_w_out_attn, m_w_o, m_ffn2_norm, m_ffn2_w_gu,
                                  m_ffn2_w_down)))
    var = dict(zip(WEIGHT_NAMES, (v_ffn1_norm, v_ffn1_w_gu, v_ffn1_w_down, v_mix_norm, v_w_in, v_b_forget, v_b_gate,
                                  v_conv_w, v_sgu_ln_g, v_sgu_ln_b, v_sgu_w, v_sgu_b, v_q_norm_g, v_k_norm_g,
                                  v_w_out_conv, v_w_out_sgu, v_w_out_attn, v_w_o, v_ffn2_norm, v_ffn2_w_gu,
                                  v_ffn2_w_down)))
    px, py, pc = _place()
    dev = 4 * px + 2 * py + pc
    chip = 2 * px + py

    big_names = [n for n in WEIGHT_NAMES if n in BIG]
    key_name = {BIG[n]: n for n in big_names}
    group_keys = (("gu1", "d1"), ("in", "oc", "os", "oa", "o", "small"), ("gu2", "d2"))

    def source(key, l):
        if key == "small":
            return jnp.concatenate([w["b_gate"][l], w["conv_w"][l], jnp.zeros((2, 128), f32)], axis=0)
        return w[key_name[key]][l].astype(bf16)

    def landing(src):
        return lax.dynamic_update_slice(lax.empty((NDEV,) + src.shape, src.dtype), src[None], (dev, 0, 0))

    groups = [[(s, landing(s)) for s in (source(k, l) for k in keys)] for l in range(2) for keys in group_keys]
    started = _gather_start("gather_start", groups)

    def weights(l, part, after):
        got = _gather_wait(f"gather_wait_{l}_{part}", started[3 * l + part], after)
        return dict(zip(group_keys[part], _gather_forward(f"gather_forward_{l}_{part}", got)))

    xl = x[0]
    saved, small, wts = [], [], []
    for l in range(2):
        ga = weights(l, 0, xl if l else None)
        wt = {"gu1": ga["gu1"][None], "d1": ga["d1"].reshape(1, 4, GU, D)}
        x1, s1 = _ffn_fwd("ffn1", xl, w["ffn1_norm"][l].reshape(1, D), wt["gu1"], wt["d1"], 0)
        gb = weights(l, 1, x1)
        p = {n: w[n][l] for n in REPLICATED_SMALL}
        p["b_gate"] = jnp.transpose(gb["small"][:, 0:3, :], (1, 0, 2)).reshape(3, D)
        p["conv_w"] = jnp.transpose(gb["small"][:, 3:6, :], (1, 0, 2)).reshape(3, D)
        p = _small_params(p)
        wt["win"] = _win_kernel_layout(gb["in"])
        wt["sq"] = jnp.stack([gb[k].reshape(D, D) for k in ("oc", "os", "oa", "o")])[None]
        x2, sm = _mixer_fwd("mix", x1, p, wt["win"], wt["sq"], 0)
        gc = weights(l, 2, x2)
        wt.update({"gu2": gc["gu2"][None], "d2": gc["d2"].reshape(1, 4, GU, D)})
        x3, s2 = _ffn_fwd("ffn2", x2, p["ffn2_norm"], wt["gu2"], wt["d2"], 0)
        saved.append((xl, x1, x2, s1, sm, s2))
        small.append(p)
        wts.append(wt)
        xl = x3
    loss_row, dx = _loss("loss", xl, loss_target[0])

    core = pc.reshape(1).astype(jnp.int32)
    buf = lambda r, c: lax.empty((2, 4, 1, r, c), bf16)
    flights = {}

    def scatter(l, part, bufs, dx):
        r1 = _rs_pair(f"rs_pair_{l}_{part}", bufs)
        ss = [_pair_sum(f"pair_sum_{l}_{k}", core, g, r) for k, g, r in zip(group_keys[part], bufs, r1)]
        flights[l, part], dx = _scatter_start(f"scatter_start_{l}_{part}", ss, dx)
        return dx

    sgrads = [None, None]
    for l in (1, 0):
        p, wt = small[l], wts[l]
        x0, x1, x2, s1, sm, s2 = saved[l]
        dx, dn2, g_gu2, g_d2 = _ffn_bwd("ffn2", dx, x2, p["ffn2_norm"], wt["gu2"], wt["d2"], 0, s2, buf(D, GU), buf(GU // 2, D))
        dx = scatter(l, 2, [g_gu2, g_d2], dx)
        dx, sg, dwin, gsq = _mixer_bwd("mix", dx, x1, p, wt["win"], wt["sq"], 0, sm, [buf(128, D) for _ in range(4)])
        g_in = jnp.stack([jnp.stack([_win_device_block(dwin, 2 * q + c)[None] for q in range(4)]) for c in range(2)])
        dx = scatter(l, 1, [g_in] + gsq, dx)
        dx, dn1, g_gu1, g_d1 = _ffn_bwd("ffn1", dx, x0, p["ffn1_norm"], wt["gu1"], wt["d1"], 0, s1, buf(D, GU), buf(GU // 2, D))
        dx = scatter(l, 0, [g_gu1, g_d1], dx)
        sg["ffn1_norm"] = dn1
        sg["ffn2_norm"] = dn2
        sgrads[l] = sg

    grads, delta, new_m, new_v = {}, {}, {}, {}
    after = dx
    for part in (2, 1, 0):
        sets = []
        for l in (1, 0):
            s_all, r2_all = _scatter_wait(f"scatter_wait_{l}_{part}", flights[l, part], after)
            sets.append([(lax.dynamic_index_in_dim(s, chip, 0, keepdims=False)[0], r) for s, r in zip(s_all, r2_all)])
        for i, k in enumerate(k for k in group_keys[part] if k != "small"):
            n = key_name[k]
            grads[n], delta[n], new_m[n], new_v[n] = _adamw("adamw_" + k, w[n], mom[n], var[n], [sets[1][i], sets[0][i]])
            after = delta[n]

    nat = [_small_grads_natural(sgrads[l]) for l in range(2)]
    order = REPLICATED_SMALL + SHARDED_SMALL
    part = _pack([jnp.stack([nat[0][n], nat[1][n]]) for n in order] + [loss_row[0, 0:1]])
    total = _all_reduce_small("reduce_small", part)
    full_shapes = [(2,) + tuple(nat[0][n].shape) for n in order] + [(1,)]
    summed = dict(zip(order + ("loss",), _unpack(total, full_shapes)))
    for n in REPLICATED_SMALL:
        grads[n] = summed[n]
    for n in SHARDED_SMALL:
        grads[n] = lax.dynamic_slice_in_dim(summed[n], dev * 128, 128, axis=2)
    wp = _pack([w[n] for n in order])
    gp = _pack([grads[n] for n in order])
    mp = _pack([mom[n] for n in order])
    vp = _pack([var[n] for n in order])
    dpk, mpk, vpk = _adamw_small("adamw_small", wp, gp, mp, vp)
    local_shapes = [tuple(w[n].shape) for n in order]
    for dst, packed in ((delta, dpk), (new_m, mpk), (new_v, vpk)):
        dst.update(zip(order, _unpack(packed, local_shapes)))

    loss = summed["loss"][0]
    return (loss, dx[None], *[grads[n] for n in WEIGHT_NAMES], *[delta[n] for n in WEIGHT_NAMES],
            *[new_m[n] for n in WEIGHT_NAMES], *[new_v[n] for n in WEIGHT_NAMES])
```

```python
import functools

import jax
import jax.numpy as jnp
from jax import lax
from jax.experimental import pallas as pl
from jax.experimental.pallas import tpu as pltpu

f32 = jnp.float32
bf16 = jnp.bfloat16
S = jax.ShapeDtypeStruct
MESH = pl.DeviceIdType.MESH

D = 1024
NH = 8
HD = 128
NDEV = 8
GU = 704
NIN = 11272
INB = 1409
GATE_OFF = 8192
F_OFF = 11264
NZ = 11776
RMS_EPS = 1e-6
LN_EPS = 1e-5
ATT_SCALE = HD ** -0.5
NEG = -1e30
INV_SQRT2 = 0.7071067811865476
INV_SQRT2PI = 0.3989422804014327

ADAM_LR = 0.001
ADAM_B1 = 0.9
ADAM_B2 = 0.999
ADAM_EPS = 1e-08
ADAM_WD = 0.01
ADAM_STEP = 10

TT = 512
VMEM_LIMIT = 56 * 1024 * 1024


def _cp(*sem):
    return pltpu.CompilerParams(dimension_semantics=sem, vmem_limit_bytes=VMEM_LIMIT)


def _bs(shape, fn):
    return pl.BlockSpec(shape, fn)


NN = (((1,), (0,)), ((), ()))
NT = (((1,), (1,)), ((), ()))
TN = (((0,), (0,)), ((), ()))


def _mm(name, a, b, *, grid, a_spec, b_spec, out_shape, out_spec, dims, acc_shape, res=None, res_spec=None,
        alpha=1.0, alias=None, split_rows=None):
    nk = grid[2]

    def body(*refs):
        a_ref, b_ref = refs[0], refs[1]
        pos = 2
        res_ref = None
        if res is not None:
            res_ref = refs[pos]
            pos += 1
        if alias is not None:
            pos += 1
        o_ref = refs[pos]
        acc_ref = refs[pos + 1] if nk > 1 else None
        part = lax.dot_general(a_ref[...].astype(bf16), b_ref[...].astype(bf16), dims, preferred_element_type=f32)

        def finish(acc):
            if alpha != 1.0:
                acc = alpha * acc
            if res_ref is not None:
                acc = res_ref[...] + acc
            if split_rows is None:
                o_ref[...] = acc.astype(o_ref.dtype)
            else:
                o_ref[0] = acc[:split_rows].astype(o_ref.dtype)
                o_ref[1] = acc[split_rows:].astype(o_ref.dtype)

        if nk == 1:
            finish(part)
        else:
            k = pl.program_id(2)

            @pl.when(k == 0)
            def _():
                acc_ref[...] = part

            @pl.when(k > 0)
            def _():
                acc_ref[...] += part

            @pl.when(k == nk - 1)
            def _():
                finish(acc_ref[...])

    operands = [a, b]
    in_specs = [a_spec, b_spec]
    if res is not None:
        operands.append(res)
        in_specs.append(res_spec)
    aliases = {}
    if alias is not None:
        aliases = {len(operands): 0}
        operands.append(alias)
        in_specs.append(pl.BlockSpec(memory_space=pl.ANY))
    return pl.pallas_call(
        body, name=name, grid=grid, in_specs=in_specs, out_specs=out_spec, out_shape=out_shape,
        scratch_shapes=[pltpu.VMEM(acc_shape, f32)] if nk > 1 else [],
        input_output_aliases=aliases,
        compiler_params=_cp("parallel", "parallel", "arbitrary"),
    )(*operands)


def _tile(n, t):
    return t if n % t == 0 and n >= t else n


def _row(cb=0, w=D):
    return _bs((TT, w), lambda i: (i, cb))


def _vec(rows=1, w=D):
    return _bs((rows, w), lambda i: (0, 0))


def _acc_store(i, ref, val):
    @pl.when(i == 0)
    def _():
        ref[...] = val

    @pl.when(i > 0)
    def _():
        ref[...] += val


def _rms_fwd(name, x, g):
    T = x.shape[0]

    def body(x_ref, g_ref, o_ref):
        xv = x_ref[...]
        r = lax.rsqrt(jnp.mean(xv * xv, axis=-1, keepdims=True) + RMS_EPS)
        o_ref[...] = (xv * r * g_ref[...]).astype(bf16)

    return pl.pallas_call(body, name=name, grid=(T // TT,), in_specs=[_row(), _vec()], out_specs=_row(),
                          out_shape=S((T, D), bf16), compiler_params=_cp("parallel"))(x, g)


def _rms_bwd(name, dh, x, g, dres):
    T = x.shape[0]

    def body(dh_ref, x_ref, g_ref, dres_ref, dx_ref, dg_ref):
        i = pl.program_id(0)
        xv = x_ref[...]
        r = lax.rsqrt(jnp.mean(xv * xv, axis=-1, keepdims=True) + RMS_EPS)
        xhat = xv * r
        dh_v = dh_ref[...]
        dyg = dh_v * g_ref[...]
        m = jnp.mean(dyg * xhat, axis=-1, keepdims=True)
        dx_ref[...] = dres_ref[...] + r * (dyg - xhat * m)
        _acc_store(i, dg_ref, jnp.sum(dh_v * xhat, axis=0, keepdims=True))

    return pl.pallas_call(body, name=name, grid=(T // TT,), in_specs=[_row(), _row(), _vec(), _row()],
                          out_specs=[_row(), _vec()], out_shape=[S((T, D), f32), S((1, D), f32)],
                          compiler_params=_cp("arbitrary"))(dh, x, g, dres)


def _sigmoid(x):
    return 1.0 / (1.0 + jnp.exp(-x))


def _swiglu_fwd(name, h, wgu, l):
    T = h.shape[0]

    def body(h_ref, wg_ref, wu_ref, gu_ref, a_ref):
        hv = h_ref[...]
        g = jnp.dot(hv, wg_ref[...], preferred_element_type=f32)
        u = jnp.dot(hv, wu_ref[...], preferred_element_type=f32)
        gu_ref[0] = g
        gu_ref[1] = u
        a_ref[...] = (g * _sigmoid(g) * u).astype(bf16)

    return pl.pallas_call(
        body, name=name, grid=(4,),
        in_specs=[_bs((T, D), lambda j: (0, 0)), _bs((None, None, D, GU), lambda j: (l, j, 0, 0)),
                  _bs((None, None, D, GU), lambda j: (l, j + 4, 0, 0))],
        out_specs=[_bs((2, None, T, GU), lambda j: (0, j, 0, 0)), _bs((None, T, GU), lambda j: (j, 0, 0))],
        out_shape=[S((2, 4, T, GU), f32), S((4, T, GU), bf16)], compiler_params=_cp("parallel"))(h, wgu, wgu)


def _swiglu_bwd(name, dxo, wd, gu, l):
    T = dxo.shape[0]
    tm = _tile(T, 1024)

    def body(dx_ref, wd_ref, g_ref, u_ref, o_ref):
        da = 0.5 * lax.dot_general(dx_ref[...].astype(bf16), wd_ref[...], NT, preferred_element_type=f32)
        g = g_ref[...]
        sg = _sigmoid(g)
        o_ref[0] = (da * u_ref[...] * (sg + g * sg * (1.0 - sg))).astype(bf16)
        o_ref[1] = (da * g * sg).astype(bf16)

    return pl.pallas_call(
        body, name=name, grid=(T // tm, 4),
        in_specs=[_bs((tm, D), lambda i, j: (i, 0)), _bs((None, None, GU, D), lambda i, j: (l, j, 0, 0)),
                  _bs((None, None, tm, GU), lambda i, j: (0, j, i, 0)), _bs((None, None, tm, GU), lambda i, j: (1, j, i, 0))],
        out_specs=_bs((2, None, tm, GU), lambda i, j: (0, j, i, 0)), out_shape=S((2, 4, T, GU), bf16),
        compiler_params=_cp("parallel", "parallel"))(dxo, wd, gu, gu)


def _loss(name, y, tgt):
    T = y.shape[0]

    def body(y_ref, t_ref, l_ref, dy_ref):
        i = pl.program_id(0)
        e = y_ref[...] - t_ref[...]
        dy_ref[...] = e * (1.0 / D)
        s = 0.5 * jnp.sum(jnp.mean(e * e, axis=-1, keepdims=True))
        _acc_store(i, l_ref, jnp.broadcast_to(s, (1, 128)))

    return pl.pallas_call(body, name=name, grid=(T // TT,), in_specs=[_row(), _row()],
                          out_specs=[_vec(1, 128), _row()], out_shape=[S((1, 128), f32), S((T, D), f32)],
                          compiler_params=_cp("arbitrary"))(y, tgt)


def _prev8(T, cb):
    return _bs((8, D), lambda i: (jnp.maximum(i * (TT // 8) - 1, 0), cb))


def _next8(T, cb):
    return _bs((8, D), lambda i: (jnp.minimum((i + 1) * (TT // 8), T // 8 - 1), cb))


def _conv_taps(i, ac_ref, ax_ref, pc_ref, px_ref):
    ca = ac_ref[...] * ax_ref[...]
    keep = (i > 0).astype(f32)
    p1 = pc_ref[7:8, :] * px_ref[7:8, :] * keep
    p2 = pc_ref[6:7, :] * px_ref[6:7, :] * keep
    row = lax.broadcasted_iota(jnp.int32, ca.shape, 0)
    s1 = jnp.where(row == 0, p1, pltpu.roll(ca, 1, 0))
    s2 = jnp.where(row == 0, p2, jnp.where(row == 1, p1, pltpu.roll(ca, 2, 0)))
    return ca, s1, s2


def _conv_fwd(name, z, cw):
    T = z.shape[0]

    def body(ab_ref, ac_ref, ax_ref, pc_ref, px_ref, w_ref, o_ref):
        i = pl.program_id(0)
        ca, s1, s2 = _conv_taps(i, ac_ref, ax_ref, pc_ref, px_ref)
        cv = w_ref[0:1, :] * s2 + w_ref[1:2, :] * s1 + w_ref[2:3, :] * ca
        o_ref[...] = (ab_ref[...] * cv).astype(bf16)

    return pl.pallas_call(
        body, name=name, grid=(T // TT,),
        in_specs=[_row(0), _row(1), _row(2), _prev8(T, 1), _prev8(T, 2), _vec(3)],
        out_specs=_row(), out_shape=S((T, D), bf16), compiler_params=_cp("parallel"))(z, z, z, z, z, cw)


def _conv_bwd(name, dya, z, cw):
    T = z.shape[0]
    n = T // TT

    def body(dya_ref, ab_ref, ac_ref, ax_ref, pc_ref, px_ref, ndya_ref, nab_ref, w_ref, dz_ref, dw_ref):
        i = pl.program_id(0)
        ca, s1, s2 = _conv_taps(i, ac_ref, ax_ref, pc_ref, px_ref)
        w0, w1, w2 = w_ref[0:1, :], w_ref[1:2, :], w_ref[2:3, :]
        cv = w0 * s2 + w1 * s1 + w2 * ca
        dya_v = dya_ref[...]
        ab = ab_ref[...]
        dcv = dya_v * ab
        keep = (i < n - 1).astype(f32)
        n1 = ndya_ref[0:1, :] * nab_ref[0:1, :] * keep
        n2 = ndya_ref[1:2, :] * nab_ref[1:2, :] * keep
        row = lax.broadcasted_iota(jnp.int32, dcv.shape, 0)
        f1 = jnp.where(row == TT - 1, n1, pltpu.roll(dcv, TT - 1, 0))
        f2 = jnp.where(row == TT - 1, n2, jnp.where(row == TT - 2, n1, pltpu.roll(dcv, TT - 2, 0)))
        dca = w2 * dcv + w1 * f1 + w0 * f2
        dz_ref[:, 0:D] = (dya_v * cv).astype(bf16)
        dz_ref[:, D:2 * D] = (dca * ax_ref[...]).astype(bf16)
        dz_ref[:, 2 * D:3 * D] = (dca * ac_ref[...]).astype(bf16)
        dw = jnp.concatenate([jnp.sum(dcv * s2, axis=0, keepdims=True), jnp.sum(dcv * s1, axis=0, keepdims=True),
                              jnp.sum(dcv * ca, axis=0, keepdims=True)], axis=0)
        _acc_store(i, dw_ref, dw)

    return pl.pallas_call(
        body, name=name, grid=(n,),
        in_specs=[_row(), _row(0), _row(1), _row(2), _prev8(T, 1), _prev8(T, 2), _next8(T, 0), _next8(T, 0), _vec(3)],
        out_specs=[_row(0, 3 * D), _vec(3)], out_shape=[S((T, 3 * D), bf16), S((3, D), f32)],
        compiler_params=_cp("arbitrary"))(dya, z, z, z, z, z, dya, z, cw)


def _gelu(x):
    return 0.5 * x * (1.0 + lax.erf(x * INV_SQRT2))


def _gelu_grad(x):
    return 0.5 * (1.0 + lax.erf(x * INV_SQRT2)) + x * jnp.exp(-0.5 * x * x) * INV_SQRT2PI


def _ln_stats(vv):
    mu = jnp.mean(vv, axis=-1, keepdims=True)
    xc = vv - mu
    rstd = lax.rsqrt(jnp.mean(xc * xc, axis=-1, keepdims=True) + LN_EPS)
    return xc * rstd, rstd


def _tril_w(w_ref, g):
    r = lax.broadcasted_iota(jnp.int32, (HD, HD), 0)
    c = lax.broadcasted_iota(jnp.int32, (HD, HD), 1)
    return jnp.where(c <= r, w_ref[g], 0.0).astype(bf16)


def _sgu_fwd(name, z, ln_g, ln_b, w_s, bmap):
    T = z.shape[0]

    def body(su_ref, sv_ref, lg_ref, lb_ref, w_ref, bm_ref, o_ref, vn_ref):
        xhat, _ = _ln_stats(_gelu(sv_ref[...]))
        vn_ref[...] = (xhat * lg_ref[...] + lb_ref[...]).astype(bf16)
        for g in range(NH):
            w = _tril_w(w_ref, g)
            cs = slice(g * HD, (g + 1) * HD)
            for c in range(TT // HD):
                rs = slice(c * HD, (c + 1) * HD)
                s = jnp.dot(w, vn_ref[rs, cs], preferred_element_type=f32) + bm_ref[:, cs]
                o_ref[rs, cs] = (_gelu(su_ref[rs, cs]) * s).astype(bf16)

    return pl.pallas_call(
        body, name=name, grid=(T // TT,),
        in_specs=[_row(3), _row(4), _vec(), _vec(), _bs((NH, HD, HD), lambda i: (0, 0, 0)), _vec(HD)],
        out_specs=_row(), out_shape=S((T, D), bf16), scratch_shapes=[pltpu.VMEM((TT, D), bf16)],
        compiler_params=_cp("parallel"))(z, z, ln_g, ln_b, w_s, bmap)


def _sgu_bwd(name, dyb, z, ln_g, ln_b, w_s, bmap):
    T = z.shape[0]

    def body(dyb_ref, su_ref, sv_ref, lg_ref, lb_ref, w_ref, bm_ref, dz_ref, dlg_ref, dlb_ref, dw_ref, db_ref,
             vn_ref, du_ref, dvn_ref):
        i = pl.program_id(0)
        sv = sv_ref[...]
        xhat, rstd = _ln_stats(_gelu(sv))
        vn_ref[...] = (xhat * lg_ref[...] + lb_ref[...]).astype(bf16)
        r = lax.broadcasted_iota(jnp.int32, (HD, HD), 0)
        cc = lax.broadcasted_iota(jnp.int32, (HD, HD), 1)
        for g in range(NH):
            w = _tril_w(w_ref, g)
            cs = slice(g * HD, (g + 1) * HD)
            dw = jnp.zeros((HD, HD), f32)
            db = jnp.zeros((HD, 1), f32)
            for c in range(TT // HD):
                rs = slice(c * HD, (c + 1) * HD)
                vnb = vn_ref[rs, cs]
                s = jnp.dot(w, vnb, preferred_element_type=f32) + bm_ref[:, cs]
                dy = dyb_ref[rs, cs]
                du_ref[rs, cs] = dy * s
                ds = dy * _gelu(su_ref[rs, cs])
                ds16 = ds.astype(bf16)
                dvn_ref[rs, cs] = lax.dot_general(w, ds16, TN, preferred_element_type=f32)
                dw = dw + lax.dot_general(ds16, vnb, NT, preferred_element_type=f32)
                db = db + jnp.sum(ds, axis=1, keepdims=True)
            dw = jnp.where(cc <= r, dw, 0.0)

            @pl.when(i == 0)
            def _():
                dw_ref[g] = dw
                db_ref[:, g:g + 1] = db

            @pl.when(i > 0)
            def _():
                dw_ref[g] += dw
                db_ref[:, g:g + 1] += db

        dvn = dvn_ref[...]
        dxh = dvn * lg_ref[...]
        m1 = jnp.mean(dxh, axis=-1, keepdims=True)
        m2 = jnp.mean(dxh * xhat, axis=-1, keepdims=True)
        dvv = rstd * (dxh - m1 - xhat * m2)
        dz_ref[:, 0:D] = (du_ref[...] * _gelu_grad(su_ref[...])).astype(bf16)
        dz_ref[:, D:2 * D] = (dvv * _gelu_grad(sv)).astype(bf16)
        _acc_store(i, dlg_ref, jnp.sum(dvn * xhat, axis=0, keepdims=True))
        _acc_store(i, dlb_ref, jnp.sum(dvn, axis=0, keepdims=True))

    return pl.pallas_call(
        body, name=name, grid=(T // TT,),
        in_specs=[_row(), _row(3), _row(4), _vec(), _vec(), _bs((NH, HD, HD), lambda i: (0, 0, 0)), _vec(HD)],
        out_specs=[_row(0, 2 * D), _vec(), _vec(), _bs((NH, HD, HD), lambda i: (0, 0, 0)), _bs((HD, NH), lambda i: (0, 0))],
        out_shape=[S((T, 2 * D), bf16), S((1, D), f32), S((1, D), f32), S((NH, HD, HD), f32), S((HD, NH), f32)],
        scratch_shapes=[pltpu.VMEM((TT, D), bf16), pltpu.VMEM((TT, D), f32), pltpu.VMEM((TT, D), f32)],
        compiler_params=_cp("arbitrary"))(dyb, z, z, ln_g, ln_b, w_s, bmap)


def _qk_fwd(name, z, qg, kg, bf):
    T = z.shape[0]

    def body(q_ref, k_ref, v_ref, zf_ref, qg_ref, kg_ref, bf_ref, qn_ref, kn_ref, vb_ref, lf_ref):
        for h in range(NH):
            cs = slice(h * HD, (h + 1) * HD)
            for src, gain, dst in ((q_ref, qg_ref, qn_ref), (k_ref, kg_ref, kn_ref)):
                xv = src[:, cs]
                r = lax.rsqrt(jnp.mean(xv * xv, axis=-1, keepdims=True) + RMS_EPS)
                dst[:, cs] = (xv * r * gain[:, cs]).astype(bf16)
        vb_ref[...] = v_ref[...].astype(bf16)
        xf = zf_ref[...] + bf_ref[...]
        lf_ref[...] = jnp.minimum(xf, 0.0) - jnp.log1p(jnp.exp(-jnp.abs(xf)))

    return pl.pallas_call(
        body, name=name, grid=(T // TT,),
        in_specs=[_row(5), _row(6), _row(7), _bs((TT, 128), lambda i: (i, F_OFF // 128)), _vec(), _vec(), _vec(1, 128)],
        out_specs=[_row(), _row(), _row(), _bs((TT, 128), lambda i: (i, 0))],
        out_shape=[S((T, D), bf16), S((T, D), bf16), S((T, D), bf16), S((T, 128), f32)],
        compiler_params=_cp("parallel"))(z, z, z, z, qg, kg, bf)


def _cum_fwd(name, logf):
    T = logf.shape[0]

    def body(lf_ref, ccol_ref, crow_ref, c_ref):
        c = lf_ref[...]
        row = lax.broadcasted_iota(jnp.int32, c.shape, 0)
        s = 1
        while s < T:
            c = c + jnp.where(row >= s, pltpu.roll(c, s, 0), 0.0)
            s *= 2
        c_ref[...] = c
        crow_ref[...] = c.T[0:NH, :]
        for h in range(NH):
            ccol_ref[h] = jnp.broadcast_to(c_ref[:, h:h + 1], (T, 128))

    return pl.pallas_call(body, name=name, out_shape=[S((NH, T, 128), f32), S((NH, T), f32)],
                          scratch_shapes=[pltpu.VMEM((T, 128), f32)],
                          compiler_params=pltpu.CompilerParams(vmem_limit_bytes=VMEM_LIMIT))(logf)


ATT_TILE = 512


def _fold(x, op=jnp.add):
    acc = x[:, 0:128]
    for t in range(1, x.shape[1] // 128):
        acc = op(acc, x[:, t * 128:(t + 1) * 128])
    return acc


def _to_row(col):
    return jnp.broadcast_to(col, (col.shape[0], 128)).T[0:1, :]


def _causal(t, keys_down=False):
    r = lax.broadcasted_iota(jnp.int32, (t, t), 0)
    c = lax.broadcasted_iota(jnp.int32, (t, t), 1)
    return r <= c if keys_down else c <= r


def _attn_fwd(name, qn, kn, vb, ccol, crow3):
    T = qn.shape[0]
    tq = _tile(T, ATT_TILE)
    nq = T // tq

    def body(q_ref, k_ref, v_ref, cc_ref, cr_ref, o_ref, lse_ref, lser_ref, s_ref):
        qi = pl.program_id(1)
        q = q_ref[...]
        cq = cc_ref[:, 0:1]

        def logits(off):
            s = lax.dot_general(q, k_ref[pl.ds(off, tq), :], NT, preferred_element_type=f32) * ATT_SCALE
            return s + cq - cr_ref[:, pl.ds(off, tq)]

        def below(j, mvec):
            off = pl.multiple_of(j * tq, tq)
            s = logits(off)
            s_ref[:, pl.ds(off, tq)] = s
            return jnp.maximum(mvec, _fold(s, jnp.maximum))

        mvec = lax.fori_loop(0, qi, below, jnp.full((tq, 128), NEG, f32))
        off = pl.multiple_of(qi * tq, tq)
        s = jnp.where(_causal(tq), logits(off), NEG)
        s_ref[:, pl.ds(off, tq)] = s
        m = jnp.max(jnp.maximum(mvec, _fold(s, jnp.maximum)), axis=1, keepdims=True)

        def weigh(j, carry):
            lvec, acc = carry
            off = pl.multiple_of(j * tq, tq)
            p = jnp.exp(s_ref[:, pl.ds(off, tq)] - m)
            acc = acc + jnp.dot(p.astype(bf16), v_ref[pl.ds(off, tq), :], preferred_element_type=f32)
            return lvec + _fold(p), acc

        lvec, acc = lax.fori_loop(0, qi + 1, weigh, (jnp.zeros((tq, 128), f32), jnp.zeros((tq, HD), f32)))
        l = jnp.sum(lvec, axis=1, keepdims=True)
        o_ref[...] = acc / l
        lse = m + jnp.log(l)
        lse_ref[...] = jnp.broadcast_to(lse, (tq, 128))
        lser_ref[...] = _to_row(lse)

    return pl.pallas_call(
        body, name=name, grid=(NH, nq),
        in_specs=[_bs((tq, HD), lambda h, i: (i, h)), _bs((T, HD), lambda h, i: (0, h)), _bs((T, HD), lambda h, i: (0, h)),
                  _bs((None, tq, 128), lambda h, i: (h, i, 0)), _bs((None, 1, T), lambda h, i: (h, 0, 0))],
        out_specs=[_bs((tq, HD), lambda h, i: (i, h)), _bs((None, tq, 128), lambda h, i: (h, i, 0)),
                   _bs((None, 1, tq), lambda h, i: (h, 0, i))],
        out_shape=[S((T, D), f32), S((NH, T, 128), f32), S((NH, 1, T), f32)],
        scratch_shapes=[pltpu.VMEM((tq, T), f32)],
        compiler_params=_cp("parallel", "parallel"))(qn, kn, vb, ccol, crow3)


def _attn_dq(name, qn, kn, vb, do, lse, ccol, crow3):
    T = qn.shape[0]
    tq = _tile(T, ATT_TILE)
    nq = T // tq

    def body(q_ref, k_ref, v_ref, do_ref, lse_ref, cc_ref, cr_ref, dq_ref, dlr_ref, p_ref, dp_ref):
        qi = pl.program_id(1)
        q = q_ref[...]
        do16 = do_ref[...].astype(bf16)
        base = cc_ref[:, 0:1] - lse_ref[:, 0:1]

        def probs(off):
            s = lax.dot_general(q, k_ref[pl.ds(off, tq), :], NT, preferred_element_type=f32) * ATT_SCALE
            return jnp.exp(s + base - cr_ref[:, pl.ds(off, tq)])

        def keep(off, p, dvec):
            dp = lax.dot_general(do16, v_ref[pl.ds(off, tq), :], NT, preferred_element_type=f32)
            p_ref[:, pl.ds(off, tq)] = p
            dp_ref[:, pl.ds(off, tq)] = dp
            return dvec + _fold(p * dp)

        def below(j, dvec):
            off = pl.multiple_of(j * tq, tq)
            return keep(off, probs(off), dvec)

        dvec = lax.fori_loop(0, qi, below, jnp.zeros((tq, 128), f32))
        off = pl.multiple_of(qi * tq, tq)
        dvec = keep(off, jnp.where(_causal(tq), probs(off), 0.0), dvec)
        delta = jnp.sum(dvec, axis=1, keepdims=True)

        def grad(j, acc):
            off = pl.multiple_of(j * tq, tq)
            ds = p_ref[:, pl.ds(off, tq)] * (dp_ref[:, pl.ds(off, tq)] - delta)
            return acc + jnp.dot(ds.astype(bf16), k_ref[pl.ds(off, tq), :], preferred_element_type=f32)

        dq_ref[...] = lax.fori_loop(0, qi + 1, grad, jnp.zeros((tq, HD), f32)) * ATT_SCALE
        dlr_ref[...] = _to_row(delta)

    qb = lambda h, i: (i, h)
    full = lambda h, i: (0, h)
    col = lambda h, i: (h, i, 0)
    return pl.pallas_call(
        body, name=name, grid=(NH, nq),
        in_specs=[_bs((tq, HD), qb), _bs((T, HD), full), _bs((T, HD), full), _bs((tq, HD), qb),
                  _bs((None, tq, 128), col), _bs((None, tq, 128), col), _bs((None, 1, T), lambda h, i: (h, 0, 0))],
        out_specs=[_bs((tq, HD), qb), _bs((None, 1, tq), lambda h, i: (h, 0, i))],
        out_shape=[S((T, D), f32), S((NH, 1, T), f32)],
        scratch_shapes=[pltpu.VMEM((tq, T), f32), pltpu.VMEM((tq, T), f32)],
        compiler_params=_cp("parallel", "parallel"))(qn, kn, vb, do, lse, ccol, crow3)


def _attn_dkv(name, qn, kn, vb, do, lser3, dlr3, ccol, crow3):
    T = qn.shape[0]
    tk = _tile(T, ATT_TILE)
    nk = T // tk

    def body(q_ref, k_ref, v_ref, do_ref, lser_ref, dlr_ref, cc_ref, cr_ref, dk_ref, dv_ref, cs_ref):
        h = pl.program_id(0)
        kj = pl.program_id(1)

        @pl.when((h == 0) & (kj == 0))
        def _():
            cs_ref[...] = jnp.zeros_like(cs_ref)

        kb = k_ref[...]
        vv = v_ref[...]
        ckey = cc_ref[:, 0:1]

        def block(off, diagonal):
            rows = pl.ds(off, tk)
            qb = q_ref[rows, :]
            do16 = do_ref[rows, :].astype(bf16)
            st = lax.dot_general(kb, qb, NT, preferred_element_type=f32) * ATT_SCALE
            pt = jnp.exp(st + (cr_ref[:, rows] - lser_ref[:, rows]) - ckey)
            if diagonal:
                pt = jnp.where(_causal(tk, keys_down=True), pt, 0.0)
            dpt = lax.dot_general(vv, do16, NT, preferred_element_type=f32)
            dst = pt * (dpt - dlr_ref[:, rows])
            ddv = jnp.dot(pt.astype(bf16), do16, preferred_element_type=f32)
            ddk = jnp.dot(dst.astype(bf16), qb, preferred_element_type=f32)
            return ddk, ddv, _fold(dst)

        def above(i, carry):
            ddk, ddv, dcs = block(pl.multiple_of(i * tk, tk), False)
            return carry[0] + ddk, carry[1] + ddv, carry[2] + dcs

        off = pl.multiple_of(kj * tk, tk)
        dk, dv, cs = lax.fori_loop(kj + 1, nk, above, block(off, True))
        dk_ref[...] = dk * ATT_SCALE
        dv_ref[...] = dv
        lane = lax.broadcasted_iota(jnp.int32, (tk, 128), 1)
        cs_ref[pl.ds(off, tk), :] += jnp.where(lane == h, jnp.sum(cs, axis=1, keepdims=True), 0.0)

    full = lambda h, j: (0, h)
    blk = lambda h, j: (j, h)
    row = lambda h, j: (h, 0, 0)
    return pl.pallas_call(
        body, name=name, grid=(NH, nk),
        in_specs=[_bs((T, HD), full), _bs((tk, HD), blk), _bs((tk, HD), blk), _bs((T, HD), full), _bs((None, 1, T), row),
                  _bs((None, 1, T), row), _bs((None, tk, 128), lambda h, j: (h, j, 0)), _bs((None, 1, T), row)],
        out_specs=[_bs((tk, HD), blk), _bs((tk, HD), blk), _bs((T, 128), lambda h, j: (0, 0))],
        out_shape=[S((T, D), f32), S((T, D), f32), S((T, 128), f32)],
        compiler_params=_cp("arbitrary", "arbitrary"))(qn, kn, vb, do, lser3, dlr3, ccol, crow3)


def _forget_bwd(name, cs, z, bf):
    T = cs.shape[0]

    def body(cs_ref, zf_ref, bf_ref, dz_ref, db_ref):
        c = -cs_ref[...]
        row = lax.broadcasted_iota(jnp.int32, c.shape, 0)
        s = 1
        while s < T:
            c = c + jnp.where(row + s < T, pltpu.roll(c, T - s, 0), 0.0)
            s *= 2
        xf = zf_ref[...] + bf_ref[...]
        lane = lax.broadcasted_iota(jnp.int32, c.shape, 1)
        dxf = jnp.where(lane < NH, c / (1.0 + jnp.exp(xf)), 0.0)
        dz_ref[...] = jnp.zeros_like(dz_ref)
        dz_ref[:, 0:128] = dxf.astype(bf16)
        db_ref[...] = jnp.sum(dxf, axis=0, keepdims=True)

    return pl.pallas_call(
        body, name=name, grid=(1,),
        in_specs=[_bs((T, 128), lambda i: (0, 0)), _bs((T, 128), lambda i: (0, F_OFF // 128)), _vec(1, 128)],
        out_specs=[_bs((T, NZ - F_OFF), lambda i: (0, 0)), _vec(1, 128)],
        out_shape=[S((T, NZ - F_OFF), bf16), S((1, 128), f32)], compiler_params=_cp("arbitrary"))(cs, z, bf)


def _qk_bwd(name, dqn, dkn, dv, z, qg, kg):
    T = z.shape[0]

    def body(dq_ref, dk_ref, dv_ref, q_ref, k_ref, qg_ref, kg_ref, dz_ref, dqg_ref, dkg_ref, g_ref):
        i = pl.program_id(0)
        for n, (src, dsrc, gain, dgain) in enumerate(((q_ref, dq_ref, qg_ref, dqg_ref), (k_ref, dk_ref, kg_ref, dkg_ref))):
            for h in range(NH):
                cs = slice(h * HD, (h + 1) * HD)
                xv = src[:, cs]
                r = lax.rsqrt(jnp.mean(xv * xv, axis=-1, keepdims=True) + RMS_EPS)
                xhat = xv * r
                dy = dsrc[:, cs]
                dyg = dy * gain[:, cs]
                m = jnp.mean(dyg * xhat, axis=-1, keepdims=True)
                dz_ref[:, n * D + h * HD:n * D + (h + 1) * HD] = (r * (dyg - xhat * m)).astype(bf16)
                g_ref[:, cs] = jnp.sum(dy * xhat, axis=0, keepdims=True)
            _acc_store(i, dgain, g_ref[...])
        dz_ref[:, 2 * D:3 * D] = dv_ref[...].astype(bf16)

    return pl.pallas_call(
        body, name=name, grid=(T // TT,),
        in_specs=[_row(), _row(), _row(), _row(5), _row(6), _vec(), _vec()],
        out_specs=[_row(0, 3 * D), _vec(), _vec()], out_shape=[S((T, 3 * D), bf16), S((1, D), f32), S((1, D), f32)],
        scratch_shapes=[pltpu.VMEM((1, D), f32)], compiler_params=_cp("arbitrary"))(dqn, dkn, dv, z, z, qg, kg)


GB = GATE_OFF // D


def _merge_fwd(name, ya, yb, yc, z, bg):
    T = z.shape[0]

    def body(ya_ref, yb_ref, yc_ref, g0_ref, g1_ref, g2_ref, bg_ref, o_ref):
        acc = _sigmoid(g0_ref[...] + bg_ref[0:1, :]) * ya_ref[...]
        acc = acc + _sigmoid(g1_ref[...] + bg_ref[1:2, :]) * yb_ref[...]
        acc = acc + _sigmoid(g2_ref[...] + bg_ref[2:3, :]) * yc_ref[...]
        o_ref[...] = acc.astype(bf16)

    return pl.pallas_call(
        body, name=name, grid=(T // TT,),
        in_specs=[_row(), _row(), _row(), _row(GB), _row(GB + 1), _row(GB + 2), _vec(3)],
        out_specs=_row(), out_shape=S((T, D), bf16), compiler_params=_cp("parallel"))(ya, yb, yc, z, z, z, bg)


def _merge_bwd(name, dm, ya, yb, yc, z, bg):
    T = z.shape[0]

    def body(dm_ref, ya_ref, yb_ref, yc_ref, g0_ref, g1_ref, g2_ref, bg_ref, dya_ref, dyb_ref, dyc_ref, dz_ref, db_ref):
        i = pl.program_id(0)
        dm_v = dm_ref[...]
        dbs = []
        for n, (y_ref, g_ref, dy_ref) in enumerate(((ya_ref, g0_ref, dya_ref), (yb_ref, g1_ref, dyb_ref),
                                                    (yc_ref, g2_ref, dyc_ref))):
            gate = _sigmoid(g_ref[...] + bg_ref[n:n + 1, :])
            dy_ref[...] = (dm_v * gate).astype(bf16)
            dl = dm_v * y_ref[...] * gate * (1.0 - gate)
            dz_ref[:, n * D:(n + 1) * D] = dl.astype(bf16)
            dbs.append(jnp.sum(dl, axis=0, keepdims=True))
        _acc_store(i, db_ref, jnp.concatenate(dbs, axis=0))

    return pl.pallas_call(
        body, name=name, grid=(T // TT,),
        in_specs=[_row(), _row(), _row(), _row(), _row(GB), _row(GB + 1), _row(GB + 2), _vec(3)],
        out_specs=[_row(), _row(), _row(), _row(0, 3 * D), _vec(3)],
        out_shape=[S((T, D), bf16), S((T, D), bf16), S((T, D), bf16), S((T, 3 * D), bf16), S((3, D), f32)],
        compiler_params=_cp("arbitrary"))(dm, ya, yb, yc, z, z, z, bg)


SMALL_NAMES = ("ffn1_norm", "mix_norm", "b_forget", "b_gate", "conv_w", "sgu_ln_g", "sgu_ln_b", "sgu_w", "sgu_b",
               "q_norm_g", "k_norm_g", "ffn2_norm")


def _small_params(p):
    out = {n: p[n].reshape(1, D) for n in ("ffn1_norm", "mix_norm", "ffn2_norm", "sgu_ln_g", "sgu_ln_b", "q_norm_g", "k_norm_g")}
    out["b_forget"] = jnp.pad(p["b_forget"].reshape(1, NH), ((0, 0), (0, 128 - NH)))
    out["b_gate"] = p["b_gate"]
    out["conv_w"] = p["conv_w"]
    out["sgu_w"] = p["sgu_w"]
    out["bmap"] = jnp.repeat(p["sgu_b"].T, HD, axis=1)
    return out


def _small_grads_natural(sg):
    out = {n: sg[n].reshape(D) for n in ("ffn1_norm", "mix_norm", "ffn2_norm", "sgu_ln_g", "sgu_ln_b")}
    out["q_norm_g"] = sg["q_norm_g"].reshape(NH, HD)
    out["k_norm_g"] = sg["k_norm_g"].reshape(NH, HD)
    out["b_forget"] = sg["b_forget"][0, :NH]
    out["b_gate"] = sg["b_gate"]
    out["conv_w"] = sg["conv_w"]
    out["sgu_w"] = sg["sgu_w"]
    out["sgu_b"] = sg["sgu_b"]
    return out


SQ_TM = 1024


def _sq_fwd(name, a, wsq, l, n, res=None):
    T = a.shape[0]
    tm = _tile(T, SQ_TM)
    return _mm(name, a, wsq, grid=(T // tm, 1, 1), a_spec=_bs((tm, D), lambda i, j, k: (i, 0)),
               b_spec=_bs((None, None, D, D), lambda i, j, k: (l, n, 0, 0)),
               out_shape=S((T, D), f32), out_spec=_bs((tm, D), lambda i, j, k: (i, 0)), dims=NN, acc_shape=None,
               res=res, res_spec=_bs((tm, D), lambda i, j, k: (i, 0)))


def _sq_bwd_in(name, dy, wsq, l, n):
    T = dy.shape[0]
    tm = _tile(T, SQ_TM)
    return _mm(name, dy, wsq, grid=(T // tm, 1, 1), a_spec=_bs((tm, D), lambda i, j, k: (i, 0)),
               b_spec=_bs((None, None, D, D), lambda i, j, k: (l, n, 0, 0)),
               out_shape=S((T, D), f32), out_spec=_bs((tm, D), lambda i, j, k: (i, 0)), dims=NT, acc_shape=None)


def _sq_bwd_w(name, a, dy, gbuf, l):
    T = a.shape[0]
    return _mm(name, a, dy, grid=(NDEV // 2, 1, 1), a_spec=_bs((T, 256), lambda i, j, k: (0, i)),
               b_spec=_bs((T, D), lambda i, j, k: (0, 0)), out_shape=S(gbuf.shape, bf16),
               out_spec=_bs((2, None, None, 128, D), lambda i, j, k: (0, i, l, 0, 0)),
               dims=TN, acc_shape=None, alias=gbuf, split_rows=128)


def _ffn_fwd(tag, x, g, wgu, wd, l):
    T = x.shape[0]
    h = _rms_fwd(tag + "_rms", x, g)
    gu, a = _swiglu_fwd(tag + "_gu", h, wgu, l)
    tm = _tile(T, 1024)
    xo = _mm(tag + "_down", a, wd, grid=(T // tm, 1, 4), a_spec=_bs((None, tm, GU), lambda i, j, k: (k, i, 0)),
             b_spec=_bs((None, None, GU, D), lambda i, j, k: (l, k, 0, 0)), out_shape=S((T, D), f32),
             out_spec=_bs((tm, D), lambda i, j, k: (i, 0)), dims=NN, acc_shape=(tm, D), res=x,
             res_spec=_bs((tm, D), lambda i, j, k: (i, 0)), alpha=0.5)
    return xo, (h, gu, a)


def _ffn_bwd(tag, dxo, x, g, wgu, wd, l, saved, g_gu, g_d):
    h, gu, a = saved
    T = x.shape[0]
    g_d = _mm(tag + "_dwd", a, dxo, grid=(4, 1, 1), a_spec=_bs((None, T, GU), lambda i, j, k: (i, 0, 0)),
              b_spec=_bs((T, D), lambda i, j, k: (0, 0)), out_shape=S(g_d.shape, bf16),
              out_spec=_bs((2, None, None, GU // 2, D), lambda i, j, k: (0, i, l, 0, 0)), dims=TN, acc_shape=None,
              alpha=0.5, alias=g_d, split_rows=GU // 2)
    dgu = _swiglu_bwd(tag + "_dgu", dxo, wd, gu, l).reshape(NDEV, T, GU)
    dh = _mm(tag + "_dh", dgu, wgu, grid=(1, 1, NDEV), a_spec=_bs((None, T, GU), lambda i, j, k: (k, 0, 0)),
             b_spec=_bs((None, None, D, GU), lambda i, j, k: (l, k, 0, 0)), out_shape=S((T, D), f32),
             out_spec=_bs((T, D), lambda i, j, k: (0, 0)), dims=NT, acc_shape=(T, D))
    g_gu = _mm(tag + "_dwgu", h, dgu, grid=(1, NDEV, 1), a_spec=_bs((T, D), lambda i, j, k: (0, 0)),
               b_spec=_bs((None, T, GU), lambda i, j, k: (j, 0, 0)), out_shape=S(g_gu.shape, bf16),
               out_spec=_bs((None, None, None, D, GU), lambda i, j, k: (j % 2, j // 2, l, 0, 0)), dims=TN,
               acc_shape=None, alias=g_gu)
    dx, dg = _rms_bwd(tag + "_drms", dh, x, g, dxo)
    return dx, dg, g_gu, g_d


def _mixer_fwd(tag, x, p, win, wsq, l):
    T = x.shape[0]
    h = _rms_fwd(tag + "_rms", x, p["mix_norm"])
    tn = 512
    z = _mm(tag + "_in", h, win, grid=(1, NZ // tn, 1), a_spec=_bs((T, D), lambda i, j, k: (0, 0)),
            b_spec=_bs((D, tn), lambda i, j, k: (0, j)), out_shape=S((T, NZ), f32),
            out_spec=_bs((T, tn), lambda i, j, k: (0, j)), dims=NN, acc_shape=None)
    ya_in = _conv_fwd(tag + "_conv", z, p["conv_w"])
    yb_in = _sgu_fwd(tag + "_sgu", z, p["sgu_ln_g"], p["sgu_ln_b"], p["sgu_w"], p["bmap"])
    qn, kn, vb, logf = _qk_fwd(tag + "_qk", z, p["q_norm_g"], p["k_norm_g"], p["b_forget"])
    ccol, crow = _cum_fwd(tag + "_cum", logf)
    crow3 = crow.reshape(NH, 1, T)
    o, lse, lser = _attn_fwd(tag + "_attn", qn, kn, vb, ccol, crow3)
    ya = _sq_fwd(tag + "_oconv", ya_in, wsq, l, 0)
    yb = _sq_fwd(tag + "_osgu", yb_in, wsq, l, 1)
    yc = _sq_fwd(tag + "_oattn", o, wsq, l, 2)
    merged = _merge_fwd(tag + "_merge", ya, yb, yc, z, p["b_gate"])
    xo = _sq_fwd(tag + "_o", merged, wsq, l, 3, res=x)
    return xo, (h, z, ya_in, yb_in, qn, kn, vb, ccol, crow3, o, lse, lser, ya, yb, yc, merged)


def _mixer_bwd(tag, dxo, x, p, win, wsq, l, saved, gsq):
    h, z, ya_in, yb_in, qn, kn, vb, ccol, crow3, o, lse, lser, ya, yb, yc, merged = saved
    T = x.shape[0]
    sg = {}
    dm = _sq_bwd_in(tag + "_dmerged", dxo, wsq, l, 3)
    gsq[3] = _sq_bwd_w(tag + "_dwo", merged, dxo, gsq[3], l)
    dya, dyb, dyc, dz_g, sg["b_gate"] = _merge_bwd(tag + "_dmerge", dm, ya, yb, yc, z, p["b_gate"])
    d_ya_in = _sq_bwd_in(tag + "_dconv_in", dya, wsq, l, 0)
    gsq[0] = _sq_bwd_w(tag + "_dwoc", ya_in, dya, gsq[0], l)
    d_yb_in = _sq_bwd_in(tag + "_dsgu_in", dyb, wsq, l, 1)
    gsq[1] = _sq_bwd_w(tag + "_dwos", yb_in, dyb, gsq[1], l)
    d_o = _sq_bwd_in(tag + "_dattn_in", dyc, wsq, l, 2)
    gsq[2] = _sq_bwd_w(tag + "_dwoa", o, dyc, gsq[2], l)
    dz_c, sg["conv_w"] = _conv_bwd(tag + "_dconv", d_ya_in, z, p["conv_w"])
    dz_s, sg["sgu_ln_g"], sg["sgu_ln_b"], sg["sgu_w"], db_t = _sgu_bwd(
        tag + "_dsgu", d_yb_in, z, p["sgu_ln_g"], p["sgu_ln_b"], p["sgu_w"], p["bmap"])
    sg["sgu_b"] = db_t.T
    dqn, dlr = _attn_dq(tag + "_dattn_q", qn, kn, vb, d_o, lse, ccol, crow3)
    dkn, dv, cs = _attn_dkv(tag + "_dattn_kv", qn, kn, vb, d_o, lser, dlr, ccol, crow3)
    dz_f, sg["b_forget"] = _forget_bwd(tag + "_dforget", cs, z, p["b_forget"])
    dz_q, sg["q_norm_g"], sg["k_norm_g"] = _qk_bwd(tag + "_dqk", dqn, dkn, dv, z, p["q_norm_g"], p["k_norm_g"])
    dz = jnp.concatenate([dz_c, dz_s, dz_q, dz_g, dz_f], axis=1)
    tk = 512
    dh = _mm(tag + "_dh", dz, win, grid=(1, 1, NZ // tk), a_spec=_bs((T, tk), lambda i, j, k: (0, k)),
             b_spec=_bs((D, tk), lambda i, j, k: (0, k)), out_shape=S((T, D), f32),
             out_spec=_bs((T, D), lambda i, j, k: (0, 0)), dims=NT, acc_shape=(T, D))
    tn = 512
    dwin = _mm(tag + "_dwin", h, dz, grid=(1, NZ // tn, 1), a_spec=_bs((T, D), lambda i, j, k: (0, 0)),
               b_spec=_bs((T, tn), lambda i, j, k: (0, j)), out_shape=S((D, NZ), bf16),
               out_spec=_bs((D, tn), lambda i, j, k: (0, j)), dims=TN, acc_shape=None)
    dx, sg["mix_norm"] = _rms_bwd(tag + "_drms", dh, x, p["mix_norm"], dxo)
    return dx, sg, dwin, gsq


ANY = pl.BlockSpec(memory_space=pl.ANY)
HBM = pl.BlockSpec(memory_space=pltpu.HBM)
SEM = pl.BlockSpec(memory_space=pltpu.SEMAPHORE)
EFFECT = pltpu.SideEffectType.DATAFLOW_SIDE_EFFECTING


def _place():
    return lax.axis_index("x"), lax.axis_index("y"), lax.axis_index("c")


NEAR = 4


def _others(x, y, c):
    return [(x, y, 1 - c), (1 - x, y, c), (x, 1 - y, c), (1 - x, 1 - y, c)]


def _gather_start(name, groups):
    sizes = [len(g) for g in groups]
    srcs = [s for g in groups for s, _ in g]
    lands = [b for g in groups for _, b in g]
    n, ng = len(srcs), len(groups)

    def body(*refs):
        src_refs, land_refs = refs[:n], refs[n:2 * n]
        send, recv = refs[2 * n:2 * n + ng], refs[2 * n + ng:2 * n + 2 * ng]
        x, y, c = _place()
        me = 4 * x + 2 * y + c
        u = 0
        for g, size in enumerate(sizes):
            for i in range(size):
                for k, peer in enumerate(_others(x, y, c)):
                    pltpu.make_async_remote_copy(src_ref=src_refs[u], dst_ref=land_refs[u].at[me],
                                                 send_sem=send[g].at[i * NEAR + k], recv_sem=recv[g].at[i * NEAR + k],
                                                 device_id=peer, device_id_type=MESH).start()
                u += 1

    sems = [pltpu.SemaphoreType.DMA((size * NEAR,)) for size in sizes]
    out = pl.pallas_call(
        body, name=name, in_specs=[HBM] * (2 * n), out_specs=[SEM] * (2 * ng) + [HBM] * (2 * n),
        out_shape=sems + sems + [pltpu.HBM(a.shape, a.dtype) for a in srcs + lands],
        input_output_aliases={i: 2 * ng + i for i in range(2 * n)},
        compiler_params=pltpu.CompilerParams(has_side_effects=EFFECT),
    )(*[pltpu.with_memory_space_constraint(a, pltpu.HBM) for a in srcs + lands])
    res, pos = [], 0
    for g, size in enumerate(sizes):
        res.append((out[g], out[ng + g], out[2 * ng + pos:2 * ng + pos + size], out[2 * ng + n + pos:2 * ng + n + pos + size]))
        pos += size
    return res


def _gather_wait(name, started, after=None):
    send, recv, srcs, lands = started
    n = len(srcs)

    def body(*refs):
        src_refs, land_refs = refs[:n], refs[n:2 * n]
        send_ref, recv_ref = refs[2 * n], refs[2 * n + 1]
        x, y, c = _place()
        for i in range(n):
            for k, (px, py, pc) in enumerate(_others(x, y, c)):
                cp = pltpu.make_async_remote_copy(src_ref=src_refs[i], dst_ref=land_refs[i].at[4 * px + 2 * py + pc],
                                                  send_sem=send_ref.at[i * NEAR + k], recv_sem=recv_ref.at[i * NEAR + k],
                                                  device_id=(px, py, pc), device_id_type=MESH)
                cp.wait_send()
                cp.wait_recv()

    extra = [] if after is None else [after]
    out = pl.pallas_call(
        body, name=name, in_specs=[HBM] * (2 * n) + [SEM, SEM] + [ANY] * len(extra), out_specs=[HBM] * (2 * n),
        out_shape=[pltpu.HBM(a.shape, a.dtype) for a in list(srcs) + list(lands)],
        input_output_aliases={i: i for i in range(2 * n)},
        compiler_params=pltpu.CompilerParams(has_side_effects=EFFECT),
    )(*srcs, *lands, send, recv, *extra)
    return out[n:]


def _gather_forward(name, lands):
    n = len(lands)

    def body(*refs):
        have, full = refs[:n], refs[n:2 * n]
        send, recv = refs[2 * n], refs[2 * n + 1]
        x, y, c = _place()
        chips = [(1 - x, y), (x, 1 - y), (1 - x, 1 - y)]

        def copy(i, j, core):
            slot = 4 * chips[j][0] + 2 * chips[j][1] + core
            return pltpu.make_async_remote_copy(src_ref=have[i].at[slot], dst_ref=full[i].at[slot],
                                                send_sem=send.at[i * 3 + j], recv_sem=recv.at[i * 3 + j],
                                                device_id=(x, y, 1 - c), device_id_type=MESH)

        for i in range(n):
            for j in range(3):
                copy(i, j, c).start()
        for i in range(n):
            for j in range(3):
                copy(i, j, c).wait_send()
                copy(i, j, 1 - c).wait_recv()

    return pl.pallas_call(
        body, name=name, in_specs=[ANY] * n, out_specs=[ANY] * n, out_shape=[S(a.shape, a.dtype) for a in lands],
        input_output_aliases={i: i for i in range(n)},
        scratch_shapes=[pltpu.SemaphoreType.DMA((n * 3,)), pltpu.SemaphoreType.DMA((n * 3,))],
    )(*lands)


def _rs_pair(name, gs):
    n = len(gs)

    def body(*refs):
        ins, outs = refs[:n], refs[n:2 * n]
        send, recv = refs[2 * n:]
        x, y, c = _place()
        cps = [pltpu.make_async_remote_copy(src_ref=ins[u].at[1 - c], dst_ref=outs[u], send_sem=send.at[u],
                                            recv_sem=recv.at[u], device_id=(x, y, 1 - c), device_id_type=MESH)
               for u in range(n)]
        for cp in cps:
            cp.start()
        for cp in cps:
            cp.wait()

    return pl.pallas_call(
        body, name=name, in_specs=[ANY] * n, out_specs=[ANY] * n, out_shape=[S(g.shape[1:], g.dtype) for g in gs],
        scratch_shapes=[pltpu.SemaphoreType.DMA((n,)), pltpu.SemaphoreType.DMA((n,))],
    )(*gs)


def _row_tile(r, c):
    return 128 if (r % 128 == 0 and c > D) else (256 if r % 256 == 0 else r)


def _pair_sum(name, core, g, r1):
    _, nq, nl, r, c = g.shape
    tr = _row_tile(r, c)
    g4 = g.reshape(2, nq * nl, r, c)
    r3 = r1.reshape(nq * nl, r, c)

    def body(core_ref, g_ref, r_ref, o_ref):
        o_ref[...] = (g_ref[...].astype(f32) + r_ref[...].astype(f32)).astype(bf16)

    out = pl.pallas_call(
        body, name=name,
        grid_spec=pltpu.PrefetchScalarGridSpec(
            num_scalar_prefetch=1, grid=(nq * nl, r // tr),
            in_specs=[_bs((None, None, tr, c), lambda b, i, cr: (cr[0], b, i, 0)), _bs((None, tr, c), lambda b, i, cr: (b, i, 0))],
            out_specs=_bs((None, tr, c), lambda b, i, cr: (b, i, 0))),
        out_shape=S((nq * nl, r, c), bf16), compiler_params=_cp("parallel", "parallel"))(core, g4, r3)
    return out.reshape(nq, nl, r, c)


def _scatter_copies(ins, outs, send, recv):
    x, y, c = _place()
    chips = [(1 - x, y), (x, 1 - y), (1 - x, 1 - y)]
    return [pltpu.make_async_remote_copy(src_ref=ins[u].at[2 * chip[0] + chip[1]], dst_ref=outs[u].at[k],
                                         send_sem=send.at[u * 3 + k], recv_sem=recv.at[u * 3 + k],
                                         device_id=(*chip, c), device_id_type=MESH)
            for u in range(len(ins)) for k, chip in enumerate(chips)]


def _scatter_start(name, ss, carry):
    n = len(ss)
    lands = [lax.empty((3,) + s.shape[1:], s.dtype) for s in ss]
    held = list(ss) + lands + [carry]

    def body(*refs):
        for cp in _scatter_copies(refs[:n], refs[n:2 * n], refs[2 * n + 1], refs[2 * n + 2]):
            cp.start()

    sems = [pltpu.SemaphoreType.DMA((n * 3,))] * 2
    out = pl.pallas_call(
        body, name=name, in_specs=[HBM] * len(held), out_specs=[SEM, SEM] + [HBM] * len(held),
        out_shape=sems + [pltpu.HBM(a.shape, a.dtype) for a in held],
        input_output_aliases={i: 2 + i for i in range(len(held))},
        compiler_params=pltpu.CompilerParams(has_side_effects=EFFECT),
    )(*[pltpu.with_memory_space_constraint(a, pltpu.HBM) for a in held])
    return (out[0], out[1], out[2:2 + n], out[2 + n:2 + 2 * n]), out[2 + 2 * n]


def _scatter_wait(name, started, after):
    send, recv, srcs, lands = started
    n = len(srcs)

    def body(*refs):
        for cp in _scatter_copies(refs[:n], refs[n:2 * n], refs[2 * n], refs[2 * n + 1]):
            cp.wait_send()
            cp.wait_recv()

    out = pl.pallas_call(
        body, name=name, in_specs=[HBM] * (2 * n) + [SEM, SEM, ANY], out_specs=[HBM] * (2 * n),
        out_shape=[pltpu.HBM(a.shape, a.dtype) for a in list(srcs) + list(lands)],
        input_output_aliases={i: i for i in range(2 * n)},
        compiler_params=pltpu.CompilerParams(has_side_effects=EFFECT),
    )(*srcs, *lands, send, recv, after)
    return out[:n], out[n:]


def _all_reduce_small(name, v):
    r = v.shape[0]

    def body(v_ref, o_ref, buf, send, recv, lsem):
        x, y, c = _place()
        me, sib = (x, y, c), (x, y, 1 - c)
        chips = [(1 - x, y), (x, 1 - y), (1 - x, 1 - y)]

        def slot(px, py, pc):
            return buf.at[4 * px + 2 * py + pc]

        def copy(k, block, to, src=None):
            return pltpu.make_async_remote_copy(src_ref=slot(*block) if src is None else src, dst_ref=slot(*block),
                                                send_sem=send.at[k], recv_sem=recv.at[k], device_id=to,
                                                device_id_type=MESH)

        mine = pltpu.make_async_copy(v_ref, slot(*me), lsem)
        mine.start()
        first = [copy(0, me, sib, v_ref)] + [copy(1 + j, me, (*chip, c), v_ref) for j, chip in enumerate(chips)]
        for cp in first:
            cp.start()
        passed = [copy(4 + j, (*chip, c), sib) for j, chip in enumerate(chips)]
        for j, chip in enumerate(chips):
            copy(1 + j, (*chip, c), me).wait_recv()
            passed[j].start()
        copy(0, sib, me).wait_recv()
        for j, chip in enumerate(chips):
            copy(4 + j, (*chip, 1 - c), me).wait_recv()
        for cp in first + passed:
            cp.wait_send()
        mine.wait()
        acc = buf[0]
        for d in range(1, NDEV):
            acc = acc + buf[d]
        o_ref[...] = acc

    vm = pl.BlockSpec(memory_space=pltpu.VMEM)
    return pl.pallas_call(
        body, name=name, in_specs=[vm], out_specs=vm, out_shape=S((r, 128), f32),
        scratch_shapes=[pltpu.VMEM((NDEV, r, 128), f32), pltpu.SemaphoreType.DMA((7,)), pltpu.SemaphoreType.DMA((7,)),
                        pltpu.SemaphoreType.DMA],
        compiler_params=pltpu.CompilerParams(vmem_limit_bytes=VMEM_LIMIT),
    )(v)


def _adam_math(w, g, m, v):
    m = ADAM_B1 * m + (1.0 - ADAM_B1) * g
    v = ADAM_B2 * v + (1.0 - ADAM_B2) * (g * g)
    m_hat = m / (1.0 - ADAM_B1 ** ADAM_STEP)
    v_hat = v / (1.0 - ADAM_B2 ** ADAM_STEP)
    delta = -ADAM_LR * (m_hat / (jnp.sqrt(v_hat) + ADAM_EPS) + ADAM_WD * w)
    return delta, m, v


def _adamw(name, w, m, v, parts):
    _, r, c = w.shape
    tr = _row_tile(r, c)

    def body(w_ref, m_ref, v_ref, *refs):
        sets, (g_ref, d_ref, mo_ref, vo_ref) = (refs[0:4], refs[4:8]), refs[8:]
        for l in range(2):
            @pl.when(pl.program_id(0) == l)
            def _():
                s_ref, r0_ref, r1_ref, r2_ref = sets[l]
                g = ((s_ref[...].astype(f32) + r0_ref[...].astype(f32)) + r1_ref[...].astype(f32)) + r2_ref[...].astype(f32)
                g_ref[...] = g
                d_ref[...], mo_ref[...], vo_ref[...] = _adam_math(w_ref[...], g, m_ref[...], v_ref[...])

    blk = _bs((None, tr, c), lambda l, i: (l, i, 0))
    operands, specs = [], []
    for n in range(2):
        row = (lambda l, i: i * (1 - l)) if n == 0 else (lambda l, i: i * l)
        s_mine, r2 = parts[n]
        operands += [s_mine, r2, r2, r2]
        specs.append(_bs((tr, c), functools.partial(lambda l, i, row: (row(l, i), 0), row=row)))
        specs += [_bs((None, None, tr, c), functools.partial(lambda l, i, k, row: (k, 0, row(l, i), 0), k=k, row=row))
                  for k in range(3)]
    return pl.pallas_call(
        body, name=name, grid=(2, r // tr), in_specs=[blk, blk, blk] + specs,
        out_specs=[blk] * 4, out_shape=[S(w.shape, f32)] * 4, compiler_params=_cp("arbitrary", "arbitrary"),
    )(w, m, v, *operands)


def _adamw_small(name, w, g, m, v):
    def body(w_ref, g_ref, m_ref, v_ref, d_ref, mo_ref, vo_ref):
        d_ref[...], mo_ref[...], vo_ref[...] = _adam_math(w_ref[...], g_ref[...], m_ref[...], v_ref[...])

    return pl.pallas_call(body, name=name, out_shape=[S(w.shape, f32)] * 3,
                          compiler_params=pltpu.CompilerParams(vmem_limit_bytes=VMEM_LIMIT))(w, g, m, v)


WEIGHT_NAMES = ("ffn1_norm", "ffn1_w_gu", "ffn1_w_down", "mix_norm", "w_in", "b_forget", "b_gate", "conv_w", "sgu_ln_g",
                "sgu_ln_b", "sgu_w", "sgu_b", "q_norm_g", "k_norm_g", "w_out_conv", "w_out_sgu", "w_out_attn", "w_o",
                "ffn2_norm", "ffn2_w_gu", "ffn2_w_down")
BIG = {"ffn1_w_gu": "gu1", "ffn2_w_gu": "gu2", "ffn1_w_down": "d1", "ffn2_w_down": "d2", "w_in": "in",
       "w_out_conv": "oc", "w_out_sgu": "os", "w_out_attn": "oa", "w_o": "o"}
BIG_KEYS = ("gu1", "gu2", "d1", "d2", "in", "oc", "os", "oa", "o")
REPLICATED_SMALL = ("ffn1_norm", "mix_norm", "b_forget", "sgu_ln_g", "sgu_ln_b", "sgu_w", "sgu_b", "q_norm_g",
                    "k_norm_g", "ffn2_norm")
SHARDED_SMALL = ("b_gate", "conv_w")


def _pack(arrays):
    flat = jnp.concatenate([a.reshape(-1).astype(f32) for a in arrays])
    rows = -(-flat.shape[0] // 1024) * 8
    return jnp.pad(flat, (0, rows * 128 - flat.shape[0])).reshape(rows, 128)


def _unpack(packed, shapes):
    flat = packed.reshape(-1)
    out, pos = [], 0
    for shp in shapes:
        size = 1
        for s_ in shp:
            size *= s_
        out.append(flat[pos:pos + size].reshape(shp))
        pos += size
    return out


def _natural_runs(a, b):
    runs = []
    while a < b:
        d = a // INB
        e = min(b, (d + 1) * INB)
        runs.append((d, a - d * INB, e - d * INB))
        a = e
    return runs


def _win_kernel_layout(wg):
    runs = _natural_runs(0, GATE_OFF) + _natural_runs(GATE_OFF + NH, NIN) + _natural_runs(GATE_OFF, GATE_OFF + NH)
    return jnp.concatenate([wg[d, :, a:b] for d, a, b in runs] + [jnp.zeros((D, NZ - NIN), wg.dtype)], axis=1)


def _kernel_column(n):
    return n if n < GATE_OFF else (F_OFF + n - GATE_OFF if n < GATE_OFF + NH else n - NH)


def _win_device_block(dw, d):
    cuts = sorted({d * INB, (d + 1) * INB} | {c for c in (GATE_OFF, GATE_OFF + NH) if d * INB < c < (d + 1) * INB})
    parts = [dw[:, _kernel_column(a):_kernel_column(a) + (b - a)] for a, b in zip(cuts[:-1], cuts[1:])]
    return parts[0] if len(parts) == 1 else jnp.concatenate(parts, axis=1)


def kernel(x, ffn1_norm, ffn1_w_gu, ffn1_w_down, mix_norm, w_in, b_forget, b_gate, conv_w, sgu_ln_g, sgu_ln_b, sgu_w, sgu_b, q_norm_g, k_norm_g, w_out_conv, w_out_sgu, w_out_attn, w_o, ffn2_norm, ffn2_w_gu, ffn2_w_down, loss_target, m_ffn1_norm, m_ffn1_w_gu, m_ffn1_w_down, m_mix_norm, m_w_in, m_b_forget, m_b_gate, m_conv_w, m_sgu_ln_g, m_sgu_ln_b, m_sgu_w, m_sgu_b, m_q_norm_g, m_k_norm_g, m_w_out_conv, m_w_out_sgu, m_w_out_attn, m_w_o, m_ffn2_norm, m_ffn2_w_gu, m_ffn2_w_down, v_ffn1_norm, v_ffn1_w_gu, v_ffn1_w_down, v_mix_norm, v_w_in, v_b_forget, v_b_gate, v_conv_w, v_sgu_ln_g, v_sgu_ln_b, v_sgu_w, v_sgu_b, v_q_norm_g, v_k_norm_g, v_w_out_conv, v_w_out_sgu, v_w_out_attn, v_w_o, v_ffn2_norm, v_ffn2_w_gu, v_ffn2_w_down):
    w = dict(zip(WEIGHT_NAMES, (ffn1_norm, ffn1_w_gu, ffn1_w_down, mix_norm, w_in, b_forget, b_gate, conv_w, sgu_ln_g,
                                sgu_ln_b, sgu_w, sgu_b, q_norm_g, k_norm_g, w_out_conv, w_out_sgu, w_out_attn, w_o,
                                ffn2_norm, ffn2_w_gu, ffn2_w_down)))
    mom = dict(zip(WEIGHT_NAMES, (m_ffn1_norm, m_ffn1_w_gu, m_ffn1_w_down, m_mix_norm, m_w_in, m_b_forget, m_b_gate,
                                  m_conv_w, m_sgu_ln_g, m_sgu_ln_b, m_sgu_w, m_sgu_b, m_q_norm_g, m_k_norm_g,
                                  m_w_out_conv, m_w_out_sgu, m_w_out_attn, m_w_o, m_ffn2_norm, m_ffn2_w_gu,
                                  m_ffn2_w_down)))
    var = dict(zip(WEIGHT_NAMES, (v_ffn1_norm, v_ffn1_w_gu, v_ffn1_w_down, v_mix_norm, v_w_in, v_b_forget, v_b_gate,
                                  v_conv_w, v_sgu_ln_g, v_sgu_ln_b, v_sgu_w, v_sgu_b, v_q_norm_g, v_k_norm_g,
                                  v_w_out_conv, v_w_out_sgu, v_w_out_attn, v_w_o, v_ffn2_norm, v_ffn2_w_gu,
                                  v_ffn2_w_down)))
    px, py, pc = _place()
    dev = 4 * px + 2 * py + pc
    chip = 2 * px + py

    big_names = [n for n in WEIGHT_NAMES if n in BIG]
    key_name = {BIG[n]: n for n in big_names}
    group_keys = (("gu1", "d1"), ("in", "oc", "os", "oa", "o", "small"), ("gu2", "d2"))

    def source(key, l):
        if key == "small":
            return jnp.concatenate([w["b_gate"][l], w["conv_w"][l], jnp.zeros((2, 128), f32)], axis=0)
        return w[key_name[key]][l].astype(bf16)

    def landing(src):
        return lax.dynamic_update_slice(lax.empty((NDEV,) + src.shape, src.dtype), src[None], (dev, 0, 0))

    groups = [[(s, landing(s)) for s in (source(k, l) for k in keys)] for l in range(2) for keys in group_keys]
    started = _gather_start("gather_start", groups)

    def weights(l, part, after):
        got = _gather_wait(f"gather_wait_{l}_{part}", started[3 * l + part], after)
        return dict(zip(group_keys[part], _gather_forward(f"gather_forward_{l}_{part}", got)))

    xl = x[0]
    saved, small, wts = [], [], []
    for l in range(2):
        ga = weights(l, 0, xl if l else None)
        wt = {"gu1": ga["gu1"][None], "d1": ga["d1"].reshape(1, 4, GU, D)}
        x1, s1 = _ffn_fwd("ffn1", xl, w["ffn1_norm"][l].reshape(1, D), wt["gu1"], wt["d1"], 0)
        gb = weights(l, 1, x1)
        p = {n: w[n][l] for n in REPLICATED_SMALL}
        p["b_gate"] = jnp.transpose(gb["small"][:, 0:3, :], (1, 0, 2)).reshape(3, D)
        p["conv_w"] = jnp.transpose(gb["small"][:, 3:6, :], (1, 0, 2)).reshape(3, D)
        p = _small_params(p)
        wt["win"] = _win_kernel_layout(gb["in"])
        wt["sq"] = jnp.stack([gb[k].reshape(D, D) for k in ("oc", "os", "oa", "o")])[None]
        x2, sm = _mixer_fwd("mix", x1, p, wt["win"], wt["sq"], 0)
        gc = weights(l, 2, x2)
        wt.update({"gu2": gc["gu2"][None], "d2": gc["d2"].reshape(1, 4, GU, D)})
        x3, s2 = _ffn_fwd("ffn2", x2, p["ffn2_norm"], wt["gu2"], wt["d2"], 0)
        saved.append((xl, x1, x2, s1, sm, s2))
        small.append(p)
        wts.append(wt)
        xl = x3
    loss_row, dx = _loss("loss", xl, loss_target[0])

    core = pc.reshape(1).astype(jnp.int32)
    buf = lambda r, c: lax.empty((2, 4, 1, r, c), bf16)
    flights = {}

    def scatter(l, part, bufs, dx):
        r1 = _rs_pair(f"rs_pair_{l}_{part}", bufs)
        ss = [_pair_sum(f"pair_sum_{l}_{k}", core, g, r) for k, g, r in zip(group_keys[part], bufs, r1)]
        flights[l, part], dx = _scatter_start(f"scatter_start_{l}_{part}", ss, dx)
        return dx

    sgrads = [None, None]
    for l in (1, 0):
        p, wt = small[l], wts[l]
        x0, x1, x2, s1, sm, s2 = saved[l]
        dx, dn2, g_gu2, g_d2 = _ffn_bwd("ffn2", dx, x2, p["ffn2_norm"], wt["gu2"], wt["d2"], 0, s2, buf(D, GU), buf(GU // 2, D))
        dx = scatter(l, 2, [g_gu2, g_d2], dx)
        dx, sg, dwin, gsq = _mixer_bwd("mix", dx, x1, p, wt["win"], wt["sq"], 0, sm, [buf(128, D) for _ in range(4)])
        g_in = jnp.stack([jnp.stack([_win_device_block(dwin, 2 * q + c)[None] for q in range(4)]) for c in range(2)])
        dx = scatter(l, 1, [g_in] + gsq, dx)
        dx, dn1, g_gu1, g_d1 = _ffn_bwd("ffn1", dx, x0, p["ffn1_norm"], wt["gu1"], wt["d1"], 0, s1, buf(D, GU), buf(GU // 2, D))
        dx = scatter(l, 0, [g_gu1, g_d1], dx)
        sg["ffn1_norm"] = dn1
        sg["ffn2_norm"] = dn2
        sgrads[l] = sg

    grads, delta, new_m, new_v = {}, {}, {}, {}
    after = dx
    for part in (2, 1, 0):
        sets = []
        for l in (1, 0):
            s_all, r2_all = _scatter_wait(f"scatter_wait_{l}_{part}", flights[l, part], after)
            sets.append([(lax.dynamic_index_in_dim(s, chip, 0, keepdims=False)[0], r) for s, r in zip(s_all, r2_all)])
        for i, k in enumerate(k for k in group_keys[part] if k != "small"):
            n = key_name[k]
            grads[n], delta[n], new_m[n], new_v[n] = _adamw("adamw_" + k, w[n], mom[n], var[n], [sets[1][i], sets[0][i]])
            after = delta[n]

    nat = [_small_grads_natural(sgrads[l]) for l in range(2)]
    order = REPLICATED_SMALL + SHARDED_SMALL
    part = _pack([jnp.stack([nat[0][n], nat[1][n]]) for n in order] + [loss_row[0, 0:1]])
    total = _all_reduce_small("reduce_small", part)
    full_shapes = [(2,) + tuple(nat[0][n].shape) for n in order] + [(1,)]
    summed = dict(zip(order + ("loss",), _unpack(total, full_shapes)))
    for n in REPLICATED_SMALL:
        grads[n] = summed[n]
    for n in SHARDED_SMALL:
        grads[n] = lax.dynamic_slice_in_dim(summed[n], dev * 128, 128, axis=2)
    wp = _pack([w[n] for n in order])
    gp = _pack([grads[n] for n in order])
    mp = _pack([mom[n] for n in order])
    vp = _pack([var[n] for n in order])
    dpk, mpk, vpk = _adamw_small("adamw_small", wp, gp, mp, vp)
    local_shapes = [tuple(w[n].shape) for n in order]
    for dst, packed in ((delta, dpk), (new_m, mpk), (new_v, vpk)):
        dst.update(zip(order, _unpack(packed, local_shapes)))

    loss = summed["loss"][0]
    return (loss, dx[None], *[grads[n] for n in WEIGHT_NAMES], *[delta[n] for n in WEIGHT_NAMES],
            *[new_m[n] for n in WEIGHT_NAMES], *[new_v[n] for n in WEIGHT_NAMES])
```

```python
import functools

import jax
import jax.numpy as jnp
from jax import lax
from jax.experimental import pallas as pl
from jax.experimental.pallas import tpu as pltpu

f32 = jnp.float32
bf16 = jnp.bfloat16
S = jax.ShapeDtypeStruct
MESH = pl.DeviceIdType.MESH

D = 1024
NH = 8
HD = 128
NDEV = 8
GU = 704
NIN = 11272
INB = 1409
GATE_OFF = 8192
F_OFF = 11264
NZ = 11776
RMS_EPS = 1e-6
LN_EPS = 1e-5
ATT_SCALE = HD ** -0.5
NEG = -1e30
INV_SQRT2 = 0.7071067811865476
INV_SQRT2PI = 0.3989422804014327

ADAM_LR = 0.001
ADAM_B1 = 0.9
ADAM_B2 = 0.999
ADAM_EPS = 1e-08
ADAM_WD = 0.01
ADAM_STEP = 10

TT = 512
VMEM_LIMIT = 56 * 1024 * 1024


def _cp(*sem):
    return pltpu.CompilerParams(dimension_semantics=sem, vmem_limit_bytes=VMEM_LIMIT)


def _bs(shape, fn):
    return pl.BlockSpec(shape, fn)


NN = (((1,), (0,)), ((), ()))
NT = (((1,), (1,)), ((), ()))
TN = (((0,), (0,)), ((), ()))


def _mm(name, a, b, *, grid, a_spec, b_spec, out_shape, out_spec, dims, acc_shape, res=None, res_spec=None,
        alpha=1.0, alias=None, split_rows=None):
    nk = grid[2]

    def body(*refs):
        a_ref, b_ref = refs[0], refs[1]
        pos = 2
        res_ref = None
        if res is not None:
            res_ref = refs[pos]
            pos += 1
        if alias is not None:
            pos += 1
        o_ref = refs[pos]
        acc_ref = refs[pos + 1] if nk > 1 else None
        part = lax.dot_general(a_ref[...].astype(bf16), b_ref[...].astype(bf16), dims, preferred_element_type=f32)

        def finish(acc):
            if alpha != 1.0:
                acc = alpha * acc
            if res_ref is not None:
                acc = res_ref[...] + acc
            if split_rows is None:
                o_ref[...] = acc.astype(o_ref.dtype)
            else:
                o_ref[0] = acc[:split_rows].astype(o_ref.dtype)
                o_ref[1] = acc[split_rows:].astype(o_ref.dtype)

        if nk == 1:
            finish(part)
        else:
            k = pl.program_id(2)

            @pl.when(k == 0)
            def _():
                acc_ref[...] = part

            @pl.when(k > 0)
            def _():
                acc_ref[...] += part

            @pl.when(k == nk - 1)
            def _():
                finish(acc_ref[...])

    operands = [a, b]
    in_specs = [a_spec, b_spec]
    if res is not None:
        operands.append(res)
        in_specs.append(res_spec)
    aliases = {}
    if alias is not None:
        aliases = {len(operands): 0}
        operands.append(alias)
        in_specs.append(pl.BlockSpec(memory_space=pl.ANY))
    return pl.pallas_call(
        body, name=name, grid=grid, in_specs=in_specs, out_specs=out_spec, out_shape=out_shape,
        scratch_shapes=[pltpu.VMEM(acc_shape, f32)] if nk > 1 else [],
        input_output_aliases=aliases,
        compiler_params=_cp("parallel", "parallel", "arbitrary"),
    )(*operands)


def _tile(n, t):
    return t if n % t == 0 and n >= t else n


DZ_TILE = 512


def _dz_matmul(name, pieces, other, weight_grad):
    T = pieces[0].shape[0]
    counts = [p.shape[1] // DZ_TILE for p in pieces]
    starts = [sum(counts[:i]) for i in range(len(counts))]
    steps = sum(counts)
    npc = len(pieces)

    def body(*refs):
        prefs, o_ref, rest = refs[:npc], refs[npc], refs[npc + 1:]
        k = pl.program_id(0)
        if not weight_grad:
            out_ref, acc_ref = rest

            @pl.when(k == 0)
            def _():
                acc_ref[...] = jnp.zeros_like(acc_ref)

        for p_ref, s, c in zip(prefs, starts, counts):
            @pl.when((k >= s) & (k < s + c))
            def _():
                if weight_grad:
                    rest[0][...] = lax.dot_general(o_ref[...], p_ref[...], TN, preferred_element_type=f32).astype(bf16)
                else:
                    acc_ref[...] += lax.dot_general(p_ref[...], o_ref[...], NT, preferred_element_type=f32)

        if not weight_grad:
            @pl.when(k == steps - 1)
            def _():
                out_ref[...] = acc_ref[...]

    piece_specs = [_bs((T, DZ_TILE), functools.partial(lambda k, s, c: (0, jnp.clip(k - s, 0, c - 1)), s=s, c=c))
                   for s, c in zip(starts, counts)]
    if weight_grad:
        other_spec, out_spec, out_shape, scratch = _bs((T, D), lambda k: (0, 0)), _bs((D, DZ_TILE), lambda k: (0, k)), S((D, NZ), bf16), []
    else:
        other_spec, out_spec, out_shape = _bs((D, DZ_TILE), lambda k: (0, k)), _bs((T, D), lambda k: (0, 0)), S((T, D), f32)
        scratch = [pltpu.VMEM((T, D), f32)]
    return pl.pallas_call(body, name=name, grid=(steps,), in_specs=piece_specs + [other_spec], out_specs=out_spec,
                          out_shape=out_shape, scratch_shapes=scratch, compiler_params=_cp("arbitrary"))(*pieces, other)


def _row(cb=0, w=D):
    return _bs((TT, w), lambda i: (i, cb))


def _vec(rows=1, w=D):
    return _bs((rows, w), lambda i: (0, 0))


def _acc_store(i, ref, val):
    @pl.when(i == 0)
    def _():
        ref[...] = val

    @pl.when(i > 0)
    def _():
        ref[...] += val


def _rms_fwd(name, x, g):
    T = x.shape[0]

    def body(x_ref, g_ref, o_ref):
        xv = x_ref[...]
        r = lax.rsqrt(jnp.mean(xv * xv, axis=-1, keepdims=True) + RMS_EPS)
        o_ref[...] = (xv * r * g_ref[...]).astype(bf16)

    return pl.pallas_call(body, name=name, grid=(T // TT,), in_specs=[_row(), _vec()], out_specs=_row(),
                          out_shape=S((T, D), bf16), compiler_params=_cp("parallel"))(x, g)


def _rms_bwd(name, dh, x, g, dres):
    T = x.shape[0]

    def body(dh_ref, x_ref, g_ref, dres_ref, dx_ref, dg_ref):
        i = pl.program_id(0)
        xv = x_ref[...]
        r = lax.rsqrt(jnp.mean(xv * xv, axis=-1, keepdims=True) + RMS_EPS)
        xhat = xv * r
        dh_v = dh_ref[...]
        dyg = dh_v * g_ref[...]
        m = jnp.mean(dyg * xhat, axis=-1, keepdims=True)
        dx_ref[...] = dres_ref[...] + r * (dyg - xhat * m)
        _acc_store(i, dg_ref, jnp.sum(dh_v * xhat, axis=0, keepdims=True))

    return pl.pallas_call(body, name=name, grid=(T // TT,), in_specs=[_row(), _row(), _vec(), _row()],
                          out_specs=[_row(), _vec()], out_shape=[S((T, D), f32), S((1, D), f32)],
                          compiler_params=_cp("arbitrary"))(dh, x, g, dres)


def _sigmoid(x):
    return 1.0 / (1.0 + jnp.exp(-x))


def _swiglu_fwd(name, h, wgu, l):
    T = h.shape[0]

    def body(h_ref, wg_ref, wu_ref, gu_ref, a_ref):
        hv = h_ref[...]
        g = lax.dot_general(hv, wg_ref[...], NT, preferred_element_type=f32)
        u = lax.dot_general(hv, wu_ref[...], NT, preferred_element_type=f32)
        gu_ref[0] = g
        gu_ref[1] = u
        a_ref[...] = (g * _sigmoid(g) * u).astype(bf16)

    return pl.pallas_call(
        body, name=name, grid=(4,),
        in_specs=[_bs((T, D), lambda j: (0, 0)), _bs((None, None, GU, D), lambda j: (l, j, 0, 0)),
                  _bs((None, None, GU, D), lambda j: (l, j + 4, 0, 0))],
        out_specs=[_bs((2, None, T, GU), lambda j: (0, j, 0, 0)), _bs((None, T, GU), lambda j: (j, 0, 0))],
        out_shape=[S((2, 4, T, GU), f32), S((4, T, GU), bf16)], compiler_params=_cp("parallel"))(h, wgu, wgu)


def _swiglu_bwd(name, dxo, wd, gu, l):
    T = dxo.shape[0]
    tm = _tile(T, 1024)

    def body(dx_ref, wd_ref, g_ref, u_ref, o_ref):
        da = 0.5 * lax.dot_general(dx_ref[...].astype(bf16), wd_ref[...], NT, preferred_element_type=f32)
        g = g_ref[...]
        sg = _sigmoid(g)
        o_ref[0] = (da * u_ref[...] * (sg + g * sg * (1.0 - sg))).astype(bf16)
        o_ref[1] = (da * g * sg).astype(bf16)

    return pl.pallas_call(
        body, name=name, grid=(T // tm, 4),
        in_specs=[_bs((tm, D), lambda i, j: (i, 0)), _bs((None, None, GU, D), lambda i, j: (l, j, 0, 0)),
                  _bs((None, None, tm, GU), lambda i, j: (0, j, i, 0)), _bs((None, None, tm, GU), lambda i, j: (1, j, i, 0))],
        out_specs=_bs((2, None, tm, GU), lambda i, j: (0, j, i, 0)), out_shape=S((2, 4, T, GU), bf16),
        compiler_params=_cp("parallel", "parallel"))(dxo, wd, gu, gu)


def _loss(name, y, tgt):
    T = y.shape[0]

    def body(y_ref, t_ref, l_ref, dy_ref):
        i = pl.program_id(0)
        e = y_ref[...] - t_ref[...]
        dy_ref[...] = e * (1.0 / D)
        s = 0.5 * jnp.sum(jnp.mean(e * e, axis=-1, keepdims=True))
        _acc_store(i, l_ref, jnp.broadcast_to(s, (1, 128)))

    return pl.pallas_call(body, name=name, grid=(T // TT,), in_specs=[_row(), _row()],
                          out_specs=[_vec(1, 128), _row()], out_shape=[S((1, 128), f32), S((T, D), f32)],
                          compiler_params=_cp("arbitrary"))(y, tgt)


def _prev8(T, cb):
    return _bs((8, D), lambda i: (jnp.maximum(i * (TT // 8) - 1, 0), cb))


def _next8(T, cb):
    return _bs((8, D), lambda i: (jnp.minimum((i + 1) * (TT // 8), T // 8 - 1), cb))


def _conv_taps(i, ac_ref, ax_ref, pc_ref, px_ref):
    ca = ac_ref[...] * ax_ref[...]
    keep = (i > 0).astype(f32)
    p1 = pc_ref[7:8, :] * px_ref[7:8, :] * keep
    p2 = pc_ref[6:7, :] * px_ref[6:7, :] * keep
    row = lax.broadcasted_iota(jnp.int32, ca.shape, 0)
    s1 = jnp.where(row == 0, p1, pltpu.roll(ca, 1, 0))
    s2 = jnp.where(row == 0, p2, jnp.where(row == 1, p1, pltpu.roll(ca, 2, 0)))
    return ca, s1, s2


def _conv_fwd(name, z, cw):
    T = z.shape[0]

    def body(ab_ref, ac_ref, ax_ref, pc_ref, px_ref, w_ref, o_ref):
        i = pl.program_id(0)
        ca, s1, s2 = _conv_taps(i, ac_ref, ax_ref, pc_ref, px_ref)
        cv = w_ref[0:1, :] * s2 + w_ref[1:2, :] * s1 + w_ref[2:3, :] * ca
        o_ref[...] = (ab_ref[...] * cv).astype(bf16)

    return pl.pallas_call(
        body, name=name, grid=(T // TT,),
        in_specs=[_row(0), _row(1), _row(2), _prev8(T, 1), _prev8(T, 2), _vec(3)],
        out_specs=_row(), out_shape=S((T, D), bf16), compiler_params=_cp("parallel"))(z, z, z, z, z, cw)


def _conv_bwd(name, dya, z, cw):
    T = z.shape[0]
    n = T // TT

    def body(dya_ref, ab_ref, ac_ref, ax_ref, pc_ref, px_ref, ndya_ref, nab_ref, w_ref, dz_ref, dw_ref):
        i = pl.program_id(0)
        ca, s1, s2 = _conv_taps(i, ac_ref, ax_ref, pc_ref, px_ref)
        w0, w1, w2 = w_ref[0:1, :], w_ref[1:2, :], w_ref[2:3, :]
        cv = w0 * s2 + w1 * s1 + w2 * ca
        dya_v = dya_ref[...]
        ab = ab_ref[...]
        dcv = dya_v * ab
        keep = (i < n - 1).astype(f32)
        n1 = ndya_ref[0:1, :] * nab_ref[0:1, :] * keep
        n2 = ndya_ref[1:2, :] * nab_ref[1:2, :] * keep
        row = lax.broadcasted_iota(jnp.int32, dcv.shape, 0)
        f1 = jnp.where(row == TT - 1, n1, pltpu.roll(dcv, TT - 1, 0))
        f2 = jnp.where(row == TT - 1, n2, jnp.where(row == TT - 2, n1, pltpu.roll(dcv, TT - 2, 0)))
        dca = w2 * dcv + w1 * f1 + w0 * f2
        dz_ref[:, 0:D] = (dya_v * cv).astype(bf16)
        dz_ref[:, D:2 * D] = (dca * ax_ref[...]).astype(bf16)
        dz_ref[:, 2 * D:3 * D] = (dca * ac_ref[...]).astype(bf16)
        dw = jnp.concatenate([jnp.sum(dcv * s2, axis=0, keepdims=True), jnp.sum(dcv * s1, axis=0, keepdims=True),
                              jnp.sum(dcv * ca, axis=0, keepdims=True)], axis=0)
        _acc_store(i, dw_ref, dw)

    return pl.pallas_call(
        body, name=name, grid=(n,),
        in_specs=[_row(), _row(0), _row(1), _row(2), _prev8(T, 1), _prev8(T, 2), _next8(T, 0), _next8(T, 0), _vec(3)],
        out_specs=[_row(0, 3 * D), _vec(3)], out_shape=[S((T, 3 * D), bf16), S((3, D), f32)],
        compiler_params=_cp("arbitrary"))(dya, z, z, z, z, z, dya, z, cw)


def _gelu(x):
    return 0.5 * x * (1.0 + lax.erf(x * INV_SQRT2))


def _gelu_grad(x):
    return 0.5 * (1.0 + lax.erf(x * INV_SQRT2)) + x * jnp.exp(-0.5 * x * x) * INV_SQRT2PI


def _ln_stats(vv):
    mu = jnp.mean(vv, axis=-1, keepdims=True)
    xc = vv - mu
    rstd = lax.rsqrt(jnp.mean(xc * xc, axis=-1, keepdims=True) + LN_EPS)
    return xc * rstd, rstd


def _tril_w(w_ref, g):
    r = lax.broadcasted_iota(jnp.int32, (HD, HD), 0)
    c = lax.broadcasted_iota(jnp.int32, (HD, HD), 1)
    return jnp.where(c <= r, w_ref[g], 0.0).astype(bf16)


def _sgu_fwd(name, z, ln_g, ln_b, w_s, bmap):
    T = z.shape[0]

    def body(su_ref, sv_ref, lg_ref, lb_ref, w_ref, bm_ref, o_ref, vn_ref):
        xhat, _ = _ln_stats(_gelu(sv_ref[...]))
        vn_ref[...] = (xhat * lg_ref[...] + lb_ref[...]).astype(bf16)
        for g in range(NH):
            w = _tril_w(w_ref, g)
            cs = slice(g * HD, (g + 1) * HD)
            for c in range(TT // HD):
                rs = slice(c * HD, (c + 1) * HD)
                s = jnp.dot(w, vn_ref[rs, cs], preferred_element_type=f32) + bm_ref[:, cs]
                o_ref[rs, cs] = (_gelu(su_ref[rs, cs]) * s).astype(bf16)

    return pl.pallas_call(
        body, name=name, grid=(T // TT,),
        in_specs=[_row(3), _row(4), _vec(), _vec(), _bs((NH, HD, HD), lambda i: (0, 0, 0)), _vec(HD)],
        out_specs=_row(), out_shape=S((T, D), bf16), scratch_shapes=[pltpu.VMEM((TT, D), bf16)],
        compiler_params=_cp("parallel"))(z, z, ln_g, ln_b, w_s, bmap)


def _sgu_bwd(name, dyb, z, ln_g, ln_b, w_s, bmap):
    T = z.shape[0]

    def body(dyb_ref, su_ref, sv_ref, lg_ref, lb_ref, w_ref, bm_ref, dz_ref, dlg_ref, dlb_ref, dw_ref, db_ref,
             vn_ref, du_ref, dvn_ref):
        i = pl.program_id(0)
        sv = sv_ref[...]
        xhat, rstd = _ln_stats(_gelu(sv))
        vn_ref[...] = (xhat * lg_ref[...] + lb_ref[...]).astype(bf16)
        r = lax.broadcasted_iota(jnp.int32, (HD, HD), 0)
        cc = lax.broadcasted_iota(jnp.int32, (HD, HD), 1)
        for g in range(NH):
            w = _tril_w(w_ref, g)
            cs = slice(g * HD, (g + 1) * HD)
            dw = jnp.zeros((HD, HD), f32)
            db = jnp.zeros((HD, 1), f32)
            for c in range(TT // HD):
                rs = slice(c * HD, (c + 1) * HD)
                vnb = vn_ref[rs, cs]
                s = jnp.dot(w, vnb, preferred_element_type=f32) + bm_ref[:, cs]
                dy = dyb_ref[rs, cs]
                du_ref[rs, cs] = dy * s
                ds = dy * _gelu(su_ref[rs, cs])
                ds16 = ds.astype(bf16)
                dvn_ref[rs, cs] = lax.dot_general(w, ds16, TN, preferred_element_type=f32)
                dw = dw + lax.dot_general(ds16, vnb, NT, preferred_element_type=f32)
                db = db + jnp.sum(ds, axis=1, keepdims=True)
            dw = jnp.where(cc <= r, dw, 0.0)

            @pl.when(i == 0)
            def _():
                dw_ref[g] = dw
                db_ref[:, g:g + 1] = db

            @pl.when(i > 0)
            def _():
                dw_ref[g] += dw
                db_ref[:, g:g + 1] += db

        dvn = dvn_ref[...]
        dxh = dvn * lg_ref[...]
        m1 = jnp.mean(dxh, axis=-1, keepdims=True)
        m2 = jnp.mean(dxh * xhat, axis=-1, keepdims=True)
        dvv = rstd * (dxh - m1 - xhat * m2)
        dz_ref[:, 0:D] = (du_ref[...] * _gelu_grad(su_ref[...])).astype(bf16)
        dz_ref[:, D:2 * D] = (dvv * _gelu_grad(sv)).astype(bf16)
        _acc_store(i, dlg_ref, jnp.sum(dvn * xhat, axis=0, keepdims=True))
        _acc_store(i, dlb_ref, jnp.sum(dvn, axis=0, keepdims=True))

    return pl.pallas_call(
        body, name=name, grid=(T // TT,),
        in_specs=[_row(), _row(3), _row(4), _vec(), _vec(), _bs((NH, HD, HD), lambda i: (0, 0, 0)), _vec(HD)],
        out_specs=[_row(0, 2 * D), _vec(), _vec(), _bs((NH, HD, HD), lambda i: (0, 0, 0)), _bs((HD, NH), lambda i: (0, 0))],
        out_shape=[S((T, 2 * D), bf16), S((1, D), f32), S((1, D), f32), S((NH, HD, HD), f32), S((HD, NH), f32)],
        scratch_shapes=[pltpu.VMEM((TT, D), bf16), pltpu.VMEM((TT, D), f32), pltpu.VMEM((TT, D), f32)],
        compiler_params=_cp("arbitrary"))(dyb, z, z, ln_g, ln_b, w_s, bmap)


def _qk_fwd(name, z, qg, kg, bf):
    T = z.shape[0]

    def body(q_ref, k_ref, v_ref, zf_ref, qg_ref, kg_ref, bf_ref, qn_ref, kn_ref, vb_ref, lf_ref):
        for h in range(NH):
            cs = slice(h * HD, (h + 1) * HD)
            for src, gain, dst in ((q_ref, qg_ref, qn_ref), (k_ref, kg_ref, kn_ref)):
                xv = src[:, cs]
                r = lax.rsqrt(jnp.mean(xv * xv, axis=-1, keepdims=True) + RMS_EPS)
                dst[:, cs] = (xv * r * gain[:, cs]).astype(bf16)
        vb_ref[...] = v_ref[...].astype(bf16)
        xf = zf_ref[...] + bf_ref[...]
        lf_ref[...] = jnp.minimum(xf, 0.0) - jnp.log1p(jnp.exp(-jnp.abs(xf)))

    return pl.pallas_call(
        body, name=name, grid=(T // TT,),
        in_specs=[_row(5), _row(6), _row(7), _bs((TT, 128), lambda i: (i, F_OFF // 128)), _vec(), _vec(), _vec(1, 128)],
        out_specs=[_row(), _row(), _row(), _bs((TT, 128), lambda i: (i, 0))],
        out_shape=[S((T, D), bf16), S((T, D), bf16), S((T, D), bf16), S((T, 128), f32)],
        compiler_params=_cp("parallel"))(z, z, z, z, qg, kg, bf)


def _cum_fwd(name, logf):
    T = logf.shape[0]

    def body(lf_ref, ccol_ref, crow_ref, c_ref):
        c = lf_ref[...]
        row = lax.broadcasted_iota(jnp.int32, c.shape, 0)
        s = 1
        while s < T:
            c = c + jnp.where(row >= s, pltpu.roll(c, s, 0), 0.0)
            s *= 2
        c_ref[...] = c
        crow_ref[...] = c.T[0:NH, :]
        for h in range(NH):
            ccol_ref[h] = jnp.broadcast_to(c_ref[:, h:h + 1], (T, 128))

    return pl.pallas_call(body, name=name, out_shape=[S((NH, T, 128), f32), S((NH, T), f32)],
                          scratch_shapes=[pltpu.VMEM((T, 128), f32)],
                          compiler_params=pltpu.CompilerParams(vmem_limit_bytes=VMEM_LIMIT))(logf)


ATT_TILE = 512


def _fold(x, op=jnp.add):
    acc = x[:, 0:128]
    for t in range(1, x.shape[1] // 128):
        acc = op(acc, x[:, t * 128:(t + 1) * 128])
    return acc


def _to_row(col):
    return jnp.broadcast_to(col, (col.shape[0], 128)).T[0:1, :]


def _causal(t, keys_down=False):
    r = lax.broadcasted_iota(jnp.int32, (t, t), 0)
    c = lax.broadcasted_iota(jnp.int32, (t, t), 1)
    return r <= c if keys_down else c <= r


def _attn_fwd(name, qn, kn, vb, ccol, crow3):
    T = qn.shape[0]
    tq = _tile(T, ATT_TILE)
    nq = T // tq

    def body(q_ref, k_ref, v_ref, cc_ref, cr_ref, o_ref, lse_ref, lser_ref, s_ref):
        qi = pl.program_id(1)
        q = q_ref[...]
        cq = cc_ref[:, 0:1]

        def logits(off):
            s = lax.dot_general(q, k_ref[pl.ds(off, tq), :], NT, preferred_element_type=f32) * ATT_SCALE
            return s + cq - cr_ref[:, pl.ds(off, tq)]

        def below(j, mvec):
            off = pl.multiple_of(j * tq, tq)
            s = logits(off)
            s_ref[:, pl.ds(off, tq)] = s
            return jnp.maximum(mvec, _fold(s, jnp.maximum))

        mvec = lax.fori_loop(0, qi, below, jnp.full((tq, 128), NEG, f32))
        off = pl.multiple_of(qi * tq, tq)
        s = jnp.where(_causal(tq), logits(off), NEG)
        s_ref[:, pl.ds(off, tq)] = s
        m = jnp.max(jnp.maximum(mvec, _fold(s, jnp.maximum)), axis=1, keepdims=True)

        def weigh(j, carry):
            lvec, acc = carry
            off = pl.multiple_of(j * tq, tq)
            p = jnp.exp(s_ref[:, pl.ds(off, tq)] - m)
            acc = acc + jnp.dot(p.astype(bf16), v_ref[pl.ds(off, tq), :], preferred_element_type=f32)
            return lvec + _fold(p), acc

        lvec, acc = lax.fori_loop(0, qi + 1, weigh, (jnp.zeros((tq, 128), f32), jnp.zeros((tq, HD), f32)))
        l = jnp.sum(lvec, axis=1, keepdims=True)
        o_ref[...] = acc / l
        lse = m + jnp.log(l)
        lse_ref[...] = jnp.broadcast_to(lse, (tq, 128))
        lser_ref[...] = _to_row(lse)

    return pl.pallas_call(
        body, name=name, grid=(NH, nq),
        in_specs=[_bs((tq, HD), lambda h, i: (i, h)), _bs((T, HD), lambda h, i: (0, h)), _bs((T, HD), lambda h, i: (0, h)),
                  _bs((None, tq, 128), lambda h, i: (h, i, 0)), _bs((None, 1, T), lambda h, i: (h, 0, 0))],
        out_specs=[_bs((tq, HD), lambda h, i: (i, h)), _bs((None, tq, 128), lambda h, i: (h, i, 0)),
                   _bs((None, 1, tq), lambda h, i: (h, 0, i))],
        out_shape=[S((T, D), f32), S((NH, T, 128), f32), S((NH, 1, T), f32)],
        scratch_shapes=[pltpu.VMEM((tq, T), f32)],
        compiler_params=_cp("parallel", "parallel"))(qn, kn, vb, ccol, crow3)


def _attn_dq(name, qn, kn, vb, do, lse, ccol, crow3):
    T = qn.shape[0]
    tq = _tile(T, ATT_TILE)
    nq = T // tq

    def body(q_ref, k_ref, v_ref, do_ref, lse_ref, cc_ref, cr_ref, dq_ref, dlr_ref, p_ref, dp_ref):
        qi = pl.program_id(1)
        q = q_ref[...]
        do16 = do_ref[...].astype(bf16)
        base = cc_ref[:, 0:1] - lse_ref[:, 0:1]

        def probs(off):
            s = lax.dot_general(q, k_ref[pl.ds(off, tq), :], NT, preferred_element_type=f32) * ATT_SCALE
            return jnp.exp(s + base - cr_ref[:, pl.ds(off, tq)])

        def keep(off, p, dvec):
            dp = lax.dot_general(do16, v_ref[pl.ds(off, tq), :], NT, preferred_element_type=f32)
            p_ref[:, pl.ds(off, tq)] = p
            dp_ref[:, pl.ds(off, tq)] = dp
            return dvec + _fold(p * dp)

        def below(j, dvec):
            off = pl.multiple_of(j * tq, tq)
            return keep(off, probs(off), dvec)

        dvec = lax.fori_loop(0, qi, below, jnp.zeros((tq, 128), f32))
        off = pl.multiple_of(qi * tq, tq)
        dvec = keep(off, jnp.where(_causal(tq), probs(off), 0.0), dvec)
        delta = jnp.sum(dvec, axis=1, keepdims=True)

        def grad(j, acc):
            off = pl.multiple_of(j * tq, tq)
            ds = p_ref[:, pl.ds(off, tq)] * (dp_ref[:, pl.ds(off, tq)] - delta)
            return acc + jnp.dot(ds.astype(bf16), k_ref[pl.ds(off, tq), :], preferred_element_type=f32)

        dq_ref[...] = lax.fori_loop(0, qi + 1, grad, jnp.zeros((tq, HD), f32)) * ATT_SCALE
        dlr_ref[...] = _to_row(delta)

    qb = lambda h, i: (i, h)
    full = lambda h, i: (0, h)
    col = lambda h, i: (h, i, 0)
    return pl.pallas_call(
        body, name=name, grid=(NH, nq),
        in_specs=[_bs((tq, HD), qb), _bs((T, HD), full), _bs((T, HD), full), _bs((tq, HD), qb),
                  _bs((None, tq, 128), col), _bs((None, tq, 128), col), _bs((None, 1, T), lambda h, i: (h, 0, 0))],
        out_specs=[_bs((tq, HD), qb), _bs((None, 1, tq), lambda h, i: (h, 0, i))],
        out_shape=[S((T, D), f32), S((NH, 1, T), f32)],
        scratch_shapes=[pltpu.VMEM((tq, T), f32), pltpu.VMEM((tq, T), f32)],
        compiler_params=_cp("parallel", "parallel"))(qn, kn, vb, do, lse, ccol, crow3)


def _attn_dkv(name, qn, kn, vb, do, lser3, dlr3, ccol, crow3):
    T = qn.shape[0]
    tk = _tile(T, ATT_TILE)
    nk = T // tk

    def body(q_ref, k_ref, v_ref, do_ref, lser_ref, dlr_ref, cc_ref, cr_ref, dk_ref, dv_ref, cs_ref):
        h = pl.program_id(0)
        kj = pl.program_id(1)

        @pl.when((h == 0) & (kj == 0))
        def _():
            cs_ref[...] = jnp.zeros_like(cs_ref)

        kb = k_ref[...]
        vv = v_ref[...]
        ckey = cc_ref[:, 0:1]

        def block(off, diagonal):
            rows = pl.ds(off, tk)
            qb = q_ref[rows, :]
            do16 = do_ref[rows, :].astype(bf16)
            st = lax.dot_general(kb, qb, NT, preferred_element_type=f32) * ATT_SCALE
            pt = jnp.exp(st + (cr_ref[:, rows] - lser_ref[:, rows]) - ckey)
            if diagonal:
                pt = jnp.where(_causal(tk, keys_down=True), pt, 0.0)
            dpt = lax.dot_general(vv, do16, NT, preferred_element_type=f32)
            dst = pt * (dpt - dlr_ref[:, rows])
            ddv = jnp.dot(pt.astype(bf16), do16, preferred_element_type=f32)
            ddk = jnp.dot(dst.astype(bf16), qb, preferred_element_type=f32)
            return ddk, ddv, _fold(dst)

        def above(i, carry):
            ddk, ddv, dcs = block(pl.multiple_of(i * tk, tk), False)
            return carry[0] + ddk, carry[1] + ddv, carry[2] + dcs

        off = pl.multiple_of(kj * tk, tk)
        dk, dv, cs = lax.fori_loop(kj + 1, nk, above, block(off, True))
        dk_ref[...] = dk * ATT_SCALE
        dv_ref[...] = dv
        lane = lax.broadcasted_iota(jnp.int32, (tk, 128), 1)
        cs_ref[pl.ds(off, tk), :] += jnp.where(lane == h, jnp.sum(cs, axis=1, keepdims=True), 0.0)

    full = lambda h, j: (0, h)
    blk = lambda h, j: (j, h)
    row = lambda h, j: (h, 0, 0)
    return pl.pallas_call(
        body, name=name, grid=(NH, nk),
        in_specs=[_bs((T, HD), full), _bs((tk, HD), blk), _bs((tk, HD), blk), _bs((T, HD), full), _bs((None, 1, T), row),
                  _bs((None, 1, T), row), _bs((None, tk, 128), lambda h, j: (h, j, 0)), _bs((None, 1, T), row)],
        out_specs=[_bs((tk, HD), blk), _bs((tk, HD), blk), _bs((T, 128), lambda h, j: (0, 0))],
        out_shape=[S((T, D), f32), S((T, D), f32), S((T, 128), f32)],
        compiler_params=_cp("arbitrary", "arbitrary"))(qn, kn, vb, do, lser3, dlr3, ccol, crow3)


def _forget_bwd(name, cs, z, bf):
    T = cs.shape[0]

    def body(cs_ref, zf_ref, bf_ref, dz_ref, db_ref):
        c = -cs_ref[...]
        row = lax.broadcasted_iota(jnp.int32, c.shape, 0)
        s = 1
        while s < T:
            c = c + jnp.where(row + s < T, pltpu.roll(c, T - s, 0), 0.0)
            s *= 2
        xf = zf_ref[...] + bf_ref[...]
        lane = lax.broadcasted_iota(jnp.int32, c.shape, 1)
        dxf = jnp.where(lane < NH, c / (1.0 + jnp.exp(xf)), 0.0)
        dz_ref[...] = jnp.zeros_like(dz_ref)
        dz_ref[:, 0:128] = dxf.astype(bf16)
        db_ref[...] = jnp.sum(dxf, axis=0, keepdims=True)

    return pl.pallas_call(
        body, name=name, grid=(1,),
        in_specs=[_bs((T, 128), lambda i: (0, 0)), _bs((T, 128), lambda i: (0, F_OFF // 128)), _vec(1, 128)],
        out_specs=[_bs((T, NZ - F_OFF), lambda i: (0, 0)), _vec(1, 128)],
        out_shape=[S((T, NZ - F_OFF), bf16), S((1, 128), f32)], compiler_params=_cp("arbitrary"))(cs, z, bf)


def _qk_bwd(name, dqn, dkn, dv, z, qg, kg):
    T = z.shape[0]

    def body(dq_ref, dk_ref, dv_ref, q_ref, k_ref, qg_ref, kg_ref, dz_ref, dqg_ref, dkg_ref, g_ref):
        i = pl.program_id(0)
        for n, (src, dsrc, gain, dgain) in enumerate(((q_ref, dq_ref, qg_ref, dqg_ref), (k_ref, dk_ref, kg_ref, dkg_ref))):
            for h in range(NH):
                cs = slice(h * HD, (h + 1) * HD)
                xv = src[:, cs]
                r = lax.rsqrt(jnp.mean(xv * xv, axis=-1, keepdims=True) + RMS_EPS)
                xhat = xv * r
                dy = dsrc[:, cs]
                dyg = dy * gain[:, cs]
                m = jnp.mean(dyg * xhat, axis=-1, keepdims=True)
                dz_ref[:, n * D + h * HD:n * D + (h + 1) * HD] = (r * (dyg - xhat * m)).astype(bf16)
                g_ref[:, cs] = jnp.sum(dy * xhat, axis=0, keepdims=True)
            _acc_store(i, dgain, g_ref[...])
        dz_ref[:, 2 * D:3 * D] = dv_ref[...].astype(bf16)

    return pl.pallas_call(
        body, name=name, grid=(T // TT,),
        in_specs=[_row(), _row(), _row(), _row(5), _row(6), _vec(), _vec()],
        out_specs=[_row(0, 3 * D), _vec(), _vec()], out_shape=[S((T, 3 * D), bf16), S((1, D), f32), S((1, D), f32)],
        scratch_shapes=[pltpu.VMEM((1, D), f32)], compiler_params=_cp("arbitrary"))(dqn, dkn, dv, z, z, qg, kg)


GB = GATE_OFF // D


def _merge_fwd(name, ya, yb, yc, z, bg):
    T = z.shape[0]

    def body(ya_ref, yb_ref, yc_ref, g0_ref, g1_ref, g2_ref, bg_ref, o_ref):
        acc = _sigmoid(g0_ref[...] + bg_ref[0:1, :]) * ya_ref[...]
        acc = acc + _sigmoid(g1_ref[...] + bg_ref[1:2, :]) * yb_ref[...]
        acc = acc + _sigmoid(g2_ref[...] + bg_ref[2:3, :]) * yc_ref[...]
        o_ref[...] = acc.astype(bf16)

    return pl.pallas_call(
        body, name=name, grid=(T // TT,),
        in_specs=[_row(), _row(), _row(), _row(GB), _row(GB + 1), _row(GB + 2), _vec(3)],
        out_specs=_row(), out_shape=S((T, D), bf16), compiler_params=_cp("parallel"))(ya, yb, yc, z, z, z, bg)


def _merge_bwd(name, dm, ya, yb, yc, z, bg):
    T = z.shape[0]

    def body(dm_ref, ya_ref, yb_ref, yc_ref, g0_ref, g1_ref, g2_ref, bg_ref, dya_ref, dyb_ref, dyc_ref, dz_ref, db_ref):
        i = pl.program_id(0)
        dm_v = dm_ref[...]
        dbs = []
        for n, (y_ref, g_ref, dy_ref) in enumerate(((ya_ref, g0_ref, dya_ref), (yb_ref, g1_ref, dyb_ref),
                                                    (yc_ref, g2_ref, dyc_ref))):
            gate = _sigmoid(g_ref[...] + bg_ref[n:n + 1, :])
            dy_ref[...] = (dm_v * gate).astype(bf16)
            dl = dm_v * y_ref[...] * gate * (1.0 - gate)
            dz_ref[:, n * D:(n + 1) * D] = dl.astype(bf16)
            dbs.append(jnp.sum(dl, axis=0, keepdims=True))
        _acc_store(i, db_ref, jnp.concatenate(dbs, axis=0))

    return pl.pallas_call(
        body, name=name, grid=(T // TT,),
        in_specs=[_row(), _row(), _row(), _row(), _row(GB), _row(GB + 1), _row(GB + 2), _vec(3)],
        out_specs=[_row(), _row(), _row(), _row(0, 3 * D), _vec(3)],
        out_shape=[S((T, D), bf16), S((T, D), bf16), S((T, D), bf16), S((T, 3 * D), bf16), S((3, D), f32)],
        compiler_params=_cp("arbitrary"))(dm, ya, yb, yc, z, z, z, bg)


SMALL_NAMES = ("ffn1_norm", "mix_norm", "b_forget", "b_gate", "conv_w", "sgu_ln_g", "sgu_ln_b", "sgu_w", "sgu_b",
               "q_norm_g", "k_norm_g", "ffn2_norm")


def _small_params(p):
    out = {n: p[n].reshape(1, D) for n in ("ffn1_norm", "mix_norm", "ffn2_norm", "sgu_ln_g", "sgu_ln_b", "q_norm_g", "k_norm_g")}
    out["b_forget"] = jnp.pad(p["b_forget"].reshape(1, NH), ((0, 0), (0, 128 - NH)))
    out["b_gate"] = p["b_gate"]
    out["conv_w"] = p["conv_w"]
    out["sgu_w"] = p["sgu_w"]
    out["bmap"] = jnp.repeat(p["sgu_b"].T, HD, axis=1)
    return out


def _small_grads_natural(sg):
    out = {n: sg[n].reshape(D) for n in ("ffn1_norm", "mix_norm", "ffn2_norm", "sgu_ln_g", "sgu_ln_b")}
    out["q_norm_g"] = sg["q_norm_g"].reshape(NH, HD)
    out["k_norm_g"] = sg["k_norm_g"].reshape(NH, HD)
    out["b_forget"] = sg["b_forget"][0, :NH]
    out["b_gate"] = sg["b_gate"]
    out["conv_w"] = sg["conv_w"]
    out["sgu_w"] = sg["sgu_w"]
    out["sgu_b"] = sg["sgu_b"]
    return out


SQ_TM = 1024


def _sq_fwd(name, a, wsq, l, n, res=None):
    T = a.shape[0]
    tm = _tile(T, SQ_TM)
    return _mm(name, a, wsq, grid=(T // tm, 1, 1), a_spec=_bs((tm, D), lambda i, j, k: (i, 0)),
               b_spec=_bs((None, None, D, D), lambda i, j, k: (l, n, 0, 0)),
               out_shape=S((T, D), f32), out_spec=_bs((tm, D), lambda i, j, k: (i, 0)), dims=NN, acc_shape=None,
               res=res, res_spec=_bs((tm, D), lambda i, j, k: (i, 0)))


def _sq_bwd_in(name, dy, wsq, l, n):
    T = dy.shape[0]
    tm = _tile(T, SQ_TM)
    return _mm(name, dy, wsq, grid=(T // tm, 1, 1), a_spec=_bs((tm, D), lambda i, j, k: (i, 0)),
               b_spec=_bs((None, None, D, D), lambda i, j, k: (l, n, 0, 0)),
               out_shape=S((T, D), f32), out_spec=_bs((tm, D), lambda i, j, k: (i, 0)), dims=NT, acc_shape=None)


def _sq_bwd_w(name, a, dy, gbuf, l):
    T = a.shape[0]
    return _mm(name, a, dy, grid=(NDEV // 2, 1, 1), a_spec=_bs((T, 256), lambda i, j, k: (0, i)),
               b_spec=_bs((T, D), lambda i, j, k: (0, 0)), out_shape=S(gbuf.shape, bf16),
               out_spec=_bs((2, None, None, 128, D), lambda i, j, k: (0, i, l, 0, 0)),
               dims=TN, acc_shape=None, alias=gbuf, split_rows=128)


def _ffn_fwd(tag, x, g, wgu, wd, l):
    T = x.shape[0]
    h = _rms_fwd(tag + "_rms", x, g)
    gu, a = _swiglu_fwd(tag + "_gu", h, wgu, l)
    tm = _tile(T, 1024)
    xo = _mm(tag + "_down", a, wd, grid=(T // tm, 1, 4), a_spec=_bs((None, tm, GU), lambda i, j, k: (k, i, 0)),
             b_spec=_bs((None, None, GU, D), lambda i, j, k: (l, k, 0, 0)), out_shape=S((T, D), f32),
             out_spec=_bs((tm, D), lambda i, j, k: (i, 0)), dims=NN, acc_shape=(tm, D), res=x,
             res_spec=_bs((tm, D), lambda i, j, k: (i, 0)), alpha=0.5)
    return xo, (h, gu, a)


def _ffn_bwd(tag, dxo, x, g, wgu, wd, l, saved, g_gu, g_d):
    h, gu, a = saved
    T = x.shape[0]
    g_d = _mm(tag + "_dwd", a, dxo, grid=(4, 1, 1), a_spec=_bs((None, T, GU), lambda i, j, k: (i, 0, 0)),
              b_spec=_bs((T, D), lambda i, j, k: (0, 0)), out_shape=S(g_d.shape, bf16),
              out_spec=_bs((2, None, None, GU // 2, D), lambda i, j, k: (0, i, l, 0, 0)), dims=TN, acc_shape=None,
              alpha=0.5, alias=g_d, split_rows=GU // 2)
    dgu = _swiglu_bwd(tag + "_dgu", dxo, wd, gu, l).reshape(NDEV, T, GU)
    dh = _mm(tag + "_dh", dgu, wgu, grid=(1, 1, NDEV), a_spec=_bs((None, T, GU), lambda i, j, k: (k, 0, 0)),
             b_spec=_bs((None, None, GU, D), lambda i, j, k: (l, k, 0, 0)), out_shape=S((T, D), f32),
             out_spec=_bs((T, D), lambda i, j, k: (0, 0)), dims=NN, acc_shape=(T, D))
    g_gu = _mm(tag + "_dwgu", dgu, h, grid=(NDEV, 1, 1), a_spec=_bs((None, T, GU), lambda i, j, k: (i, 0, 0)),
               b_spec=_bs((T, D), lambda i, j, k: (0, 0)), out_shape=S(g_gu.shape, bf16),
               out_spec=_bs((None, None, None, GU, D), lambda i, j, k: (i % 2, i // 2, l, 0, 0)), dims=TN,
               acc_shape=None, alias=g_gu)
    dx, dg = _rms_bwd(tag + "_drms", dh, x, g, dxo)
    return dx, dg, g_gu, g_d


def _mixer_fwd(tag, x, p, win, wsq, l):
    T = x.shape[0]
    h = _rms_fwd(tag + "_rms", x, p["mix_norm"])
    tn = 512
    z = _mm(tag + "_in", h, win, grid=(1, NZ // tn, 1), a_spec=_bs((T, D), lambda i, j, k: (0, 0)),
            b_spec=_bs((D, tn), lambda i, j, k: (0, j)), out_shape=S((T, NZ), f32),
            out_spec=_bs((T, tn), lambda i, j, k: (0, j)), dims=NN, acc_shape=None)
    ya_in = _conv_fwd(tag + "_conv", z, p["conv_w"])
    yb_in = _sgu_fwd(tag + "_sgu", z, p["sgu_ln_g"], p["sgu_ln_b"], p["sgu_w"], p["bmap"])
    qn, kn, vb, logf = _qk_fwd(tag + "_qk", z, p["q_norm_g"], p["k_norm_g"], p["b_forget"])
    ccol, crow = _cum_fwd(tag + "_cum", logf)
    crow3 = crow.reshape(NH, 1, T)
    o, lse, lser = _attn_fwd(tag + "_attn", qn, kn, vb, ccol, crow3)
    ya = _sq_fwd(tag + "_oconv", ya_in, wsq, l, 0)
    yb = _sq_fwd(tag + "_osgu", yb_in, wsq, l, 1)
    yc = _sq_fwd(tag + "_oattn", o, wsq, l, 2)
    merged = _merge_fwd(tag + "_merge", ya, yb, yc, z, p["b_gate"])
    xo = _sq_fwd(tag + "_o", merged, wsq, l, 3, res=x)
    return xo, (h, z, ya_in, yb_in, qn, kn, vb, ccol, crow3, o, lse, lser, ya, yb, yc, merged)


def _mixer_bwd(tag, dxo, x, p, win, wsq, l, saved, gsq):
    h, z, ya_in, yb_in, qn, kn, vb, ccol, crow3, o, lse, lser, ya, yb, yc, merged = saved
    T = x.shape[0]
    sg = {}
    dm = _sq_bwd_in(tag + "_dmerged", dxo, wsq, l, 3)
    gsq[3] = _sq_bwd_w(tag + "_dwo", merged, dxo, gsq[3], l)
    dya, dyb, dyc, dz_g, sg["b_gate"] = _merge_bwd(tag + "_dmerge", dm, ya, yb, yc, z, p["b_gate"])
    d_ya_in = _sq_bwd_in(tag + "_dconv_in", dya, wsq, l, 0)
    gsq[0] = _sq_bwd_w(tag + "_dwoc", ya_in, dya, gsq[0], l)
    d_yb_in = _sq_bwd_in(tag + "_dsgu_in", dyb, wsq, l, 1)
    gsq[1] = _sq_bwd_w(tag + "_dwos", yb_in, dyb, gsq[1], l)
    d_o = _sq_bwd_in(tag + "_dattn_in", dyc, wsq, l, 2)
    gsq[2] = _sq_bwd_w(tag + "_dwoa", o, dyc, gsq[2], l)
    dz_c, sg["conv_w"] = _conv_bwd(tag + "_dconv", d_ya_in, z, p["conv_w"])
    dz_s, sg["sgu_ln_g"], sg["sgu_ln_b"], sg["sgu_w"], db_t = _sgu_bwd(
        tag + "_dsgu", d_yb_in, z, p["sgu_ln_g"], p["sgu_ln_b"], p["sgu_w"], p["bmap"])
    sg["sgu_b"] = db_t.T
    dqn, dlr = _attn_dq(tag + "_dattn_q", qn, kn, vb, d_o, lse, ccol, crow3)
    dkn, dv, cs = _attn_dkv(tag + "_dattn_kv", qn, kn, vb, d_o, lser, dlr, ccol, crow3)
    dz_f, sg["b_forget"] = _forget_bwd(tag + "_dforget", cs, z, p["b_forget"])
    dz_q, sg["q_norm_g"], sg["k_norm_g"] = _qk_bwd(tag + "_dqk", dqn, dkn, dv, z, p["q_norm_g"], p["k_norm_g"])
    dz = [dz_c, dz_s, dz_q, dz_g, dz_f]
    dh = _dz_matmul(tag + "_dh", dz, win, False)
    dwin = _dz_matmul(tag + "_dwin", dz, h, True)
    dx, sg["mix_norm"] = _rms_bwd(tag + "_drms", dh, x, p["mix_norm"], dxo)
    return dx, sg, dwin, gsq


ANY = pl.BlockSpec(memory_space=pl.ANY)
HBM = pl.BlockSpec(memory_space=pltpu.HBM)
SEM = pl.BlockSpec(memory_space=pltpu.SEMAPHORE)
EFFECT = pltpu.SideEffectType.DATAFLOW_SIDE_EFFECTING


def _place():
    return lax.axis_index("x"), lax.axis_index("y"), lax.axis_index("c")


NEAR = 4


def _others(x, y, c):
    return [(x, y, 1 - c), (1 - x, y, c), (x, 1 - y, c), (1 - x, 1 - y, c)]


def _gather_start(name, groups, carry=None):
    sizes = [len(g) for g in groups]
    srcs = [s for g in groups for s, _ in g]
    lands = [b for g in groups for _, b in g]
    n, ng = len(srcs), len(groups)
    held = srcs + lands + ([] if carry is None else [carry])
    nh = len(held)

    def body(*refs):
        src_refs, land_refs = refs[:n], refs[n:2 * n]
        send, recv = refs[nh:nh + ng], refs[nh + ng:nh + 2 * ng]
        x, y, c = _place()
        me = 4 * x + 2 * y + c
        u = 0
        for g, size in enumerate(sizes):
            for i in range(size):
                for k, peer in enumerate(_others(x, y, c)):
                    pltpu.make_async_remote_copy(src_ref=src_refs[u], dst_ref=land_refs[u].at[me],
                                                 send_sem=send[g].at[i * NEAR + k], recv_sem=recv[g].at[i * NEAR + k],
                                                 device_id=peer, device_id_type=MESH).start()
                u += 1

    sems = [pltpu.SemaphoreType.DMA((size * NEAR,)) for size in sizes]
    out = pl.pallas_call(
        body, name=name, in_specs=[HBM] * nh, out_specs=[SEM] * (2 * ng) + [HBM] * nh,
        out_shape=sems + sems + [pltpu.HBM(a.shape, a.dtype) for a in held],
        input_output_aliases={i: 2 * ng + i for i in range(nh)},
        compiler_params=pltpu.CompilerParams(has_side_effects=EFFECT),
    )(*[pltpu.with_memory_space_constraint(a, pltpu.HBM) for a in held])
    res, pos = [], 0
    for g, size in enumerate(sizes):
        res.append((out[g], out[ng + g], out[2 * ng + pos:2 * ng + pos + size], out[2 * ng + n + pos:2 * ng + n + pos + size]))
        pos += size
    return res if carry is None else (res, out[2 * ng + 2 * n])


def _gather_wait(name, started, after=None):
    send, recv, srcs, lands = started
    n = len(srcs)

    def body(*refs):
        src_refs, land_refs = refs[:n], refs[n:2 * n]
        send_ref, recv_ref = refs[2 * n], refs[2 * n + 1]
        x, y, c = _place()
        for i in range(n):
            for k, (px, py, pc) in enumerate(_others(x, y, c)):
                cp = pltpu.make_async_remote_copy(src_ref=src_refs[i], dst_ref=land_refs[i].at[4 * px + 2 * py + pc],
                                                  send_sem=send_ref.at[i * NEAR + k], recv_sem=recv_ref.at[i * NEAR + k],
                                                  device_id=(px, py, pc), device_id_type=MESH)
                cp.wait_send()
                cp.wait_recv()

    extra = [] if after is None else [after]
    out = pl.pallas_call(
        body, name=name, in_specs=[HBM] * (2 * n) + [SEM, SEM] + [ANY] * len(extra), out_specs=[HBM] * (2 * n),
        out_shape=[pltpu.HBM(a.shape, a.dtype) for a in list(srcs) + list(lands)],
        input_output_aliases={i: i for i in range(2 * n)},
        compiler_params=pltpu.CompilerParams(has_side_effects=EFFECT),
    )(*srcs, *lands, send, recv, *extra)
    return out[n:]


def _gather_forward(name, lands):
    n = len(lands)

    def body(*refs):
        have, full = refs[:n], refs[n:2 * n]
        send, recv = refs[2 * n], refs[2 * n + 1]
        x, y, c = _place()
        chips = [(1 - x, y), (x, 1 - y), (1 - x, 1 - y)]

        def copy(i, j, core):
            slot = 4 * chips[j][0] + 2 * chips[j][1] + core
            return pltpu.make_async_remote_copy(src_ref=have[i].at[slot], dst_ref=full[i].at[slot],
                                                send_sem=send.at[i * 3 + j], recv_sem=recv.at[i * 3 + j],
                                                device_id=(x, y, 1 - c), device_id_type=MESH)

        for i in range(n):
            for j in range(3):
                copy(i, j, c).start()
        for i in range(n):
            for j in range(3):
                copy(i, j, c).wait_send()
                copy(i, j, 1 - c).wait_recv()

    return pl.pallas_call(
        body, name=name, in_specs=[ANY] * n, out_specs=[ANY] * n, out_shape=[S(a.shape, a.dtype) for a in lands],
        input_output_aliases={i: i for i in range(n)},
        scratch_shapes=[pltpu.SemaphoreType.DMA((n * 3,)), pltpu.SemaphoreType.DMA((n * 3,))],
    )(*lands)


def _rs_pair(name, gs):
    n = len(gs)

    def body(*refs):
        ins, outs = refs[:n], refs[n:2 * n]
        send, recv = refs[2 * n:]
        x, y, c = _place()
        cps = [pltpu.make_async_remote_copy(src_ref=ins[u].at[1 - c], dst_ref=outs[u], send_sem=send.at[u],
                                            recv_sem=recv.at[u], device_id=(x, y, 1 - c), device_id_type=MESH)
               for u in range(n)]
        for cp in cps:
            cp.start()
        for cp in cps:
            cp.wait()

    return pl.pallas_call(
        body, name=name, in_specs=[ANY] * n, out_specs=[ANY] * n, out_shape=[S(g.shape[1:], g.dtype) for g in gs],
        scratch_shapes=[pltpu.SemaphoreType.DMA((n,)), pltpu.SemaphoreType.DMA((n,))],
    )(*gs)


def _row_tile(r, c):
    if c > D and r % 128 == 0:
        return 128
    return 256 if r % 256 == 0 else (GU // 2 if r % (GU // 2) == 0 else r)


def _pair_sum(name, core, g, r1):
    _, nq, nl, r, c = g.shape
    tr = _row_tile(r, c)
    g4 = g.reshape(2, nq * nl, r, c)
    r3 = r1.reshape(nq * nl, r, c)

    def body(core_ref, g_ref, r_ref, o_ref):
        o_ref[...] = (g_ref[...].astype(f32) + r_ref[...].astype(f32)).astype(bf16)

    out = pl.pallas_call(
        body, name=name,
        grid_spec=pltpu.PrefetchScalarGridSpec(
            num_scalar_prefetch=1, grid=(nq * nl, r // tr),
            in_specs=[_bs((None, None, tr, c), lambda b, i, cr: (cr[0], b, i, 0)), _bs((None, tr, c), lambda b, i, cr: (b, i, 0))],
            out_specs=_bs((None, tr, c), lambda b, i, cr: (b, i, 0))),
        out_shape=S((nq * nl, r, c), bf16), compiler_params=_cp("parallel", "parallel"))(core, g4, r3)
    return out.reshape(nq, nl, r, c)


def _scatter_copies(ins, outs, send, recv):
    x, y, c = _place()
    chips = [(1 - x, y), (x, 1 - y), (1 - x, 1 - y)]
    return [pltpu.make_async_remote_copy(src_ref=ins[u].at[2 * chip[0] + chip[1]], dst_ref=outs[u].at[k],
                                         send_sem=send.at[u * 3 + k], recv_sem=recv.at[u * 3 + k],
                                         device_id=(*chip, c), device_id_type=MESH)
            for u in range(len(ins)) for k, chip in enumerate(chips)]


def _scatter_start(name, ss, carry):
    n = len(ss)
    lands = [lax.empty((3,) + s.shape[1:], s.dtype) for s in ss]
    held = list(ss) + lands + [carry]

    def body(*refs):
        for cp in _scatter_copies(refs[:n], refs[n:2 * n], refs[2 * n + 1], refs[2 * n + 2]):
            cp.start()

    sems = [pltpu.SemaphoreType.DMA((n * 3,))] * 2
    out = pl.pallas_call(
        body, name=name, in_specs=[HBM] * len(held), out_specs=[SEM, SEM] + [HBM] * len(held),
        out_shape=sems + [pltpu.HBM(a.shape, a.dtype) for a in held],
        input_output_aliases={i: 2 + i for i in range(len(held))},
        compiler_params=pltpu.CompilerParams(has_side_effects=EFFECT),
    )(*[pltpu.with_memory_space_constraint(a, pltpu.HBM) for a in held])
    return (out[0], out[1], out[2:2 + n], out[2 + n:2 + 2 * n]), out[2 + 2 * n]


def _scatter_wait(name, started, after):
    send, recv, srcs, lands = started
    n = len(srcs)

    def body(*refs):
        for cp in _scatter_copies(refs[:n], refs[n:2 * n], refs[2 * n], refs[2 * n + 1]):
            cp.wait_send()
            cp.wait_recv()

    out = pl.pallas_call(
        body, name=name, in_specs=[HBM] * (2 * n) + [SEM, SEM, ANY], out_specs=[HBM] * (2 * n),
        out_shape=[pltpu.HBM(a.shape, a.dtype) for a in list(srcs) + list(lands)],
        input_output_aliases={i: i for i in range(2 * n)},
        compiler_params=pltpu.CompilerParams(has_side_effects=EFFECT),
    )(*srcs, *lands, send, recv, after)
    return out[:n], out[n:]


def _sum_blocks(name, blocks):
    def body(b_ref, o_ref):
        acc = b_ref[0]
        for d in range(1, NDEV):
            acc = acc + b_ref[d]
        o_ref[...] = acc

    return pl.pallas_call(body, name=name, out_shape=S(blocks.shape[1:], f32),
                          compiler_params=pltpu.CompilerParams(vmem_limit_bytes=VMEM_LIMIT))(blocks)


def _adam_math(w, g, m, v):
    m = ADAM_B1 * m + (1.0 - ADAM_B1) * g
    v = ADAM_B2 * v + (1.0 - ADAM_B2) * (g * g)
    m_hat = m / (1.0 - ADAM_B1 ** ADAM_STEP)
    v_hat = v / (1.0 - ADAM_B2 ** ADAM_STEP)
    delta = -ADAM_LR * (m_hat / (jnp.sqrt(v_hat) + ADAM_EPS) + ADAM_WD * w)
    return delta, m, v


def _adamw(name, w, m, v, parts):
    _, r, c = w.shape
    tr = _row_tile(r, c)

    def body(w_ref, m_ref, v_ref, *refs):
        sets, (g_ref, d_ref, mo_ref, vo_ref) = (refs[0:4], refs[4:8]), refs[8:]
        for l in range(2):
            @pl.when(pl.program_id(0) == l)
            def _():
                s_ref, r0_ref, r1_ref, r2_ref = sets[l]
                g = ((s_ref[...].astype(f32) + r0_ref[...].astype(f32)) + r1_ref[...].astype(f32)) + r2_ref[...].astype(f32)
                g_ref[...] = g
                d_ref[...], mo_ref[...], vo_ref[...] = _adam_math(w_ref[...], g, m_ref[...], v_ref[...])

    blk = _bs((None, tr, c), lambda l, i: (l, i, 0))
    operands, specs = [], []
    for n in range(2):
        row = (lambda l, i: i * (1 - l)) if n == 0 else (lambda l, i: i * l)
        s_mine, r2 = parts[n]
        operands += [s_mine, r2, r2, r2]
        specs.append(_bs((tr, c), functools.partial(lambda l, i, row: (row(l, i), 0), row=row)))
        specs += [_bs((None, None, tr, c), functools.partial(lambda l, i, k, row: (k, 0, row(l, i), 0), k=k, row=row))
                  for k in range(3)]
    return pl.pallas_call(
        body, name=name, grid=(2, r // tr), in_specs=[blk, blk, blk] + specs,
        out_specs=[blk] * 4, out_shape=[S(w.shape, f32)] * 4, compiler_params=_cp("arbitrary", "arbitrary"),
    )(w, m, v, *operands)


def _adamw_small(name, w, g, m, v):
    def body(w_ref, g_ref, m_ref, v_ref, d_ref, mo_ref, vo_ref):
        d_ref[...], mo_ref[...], vo_ref[...] = _adam_math(w_ref[...], g_ref[...], m_ref[...], v_ref[...])

    return pl.pallas_call(body, name=name, out_shape=[S(w.shape, f32)] * 3,
                          compiler_params=pltpu.CompilerParams(vmem_limit_bytes=VMEM_LIMIT))(w, g, m, v)


WEIGHT_NAMES = ("ffn1_norm", "ffn1_w_gu", "ffn1_w_down", "mix_norm", "w_in", "b_forget", "b_gate", "conv_w", "sgu_ln_g",
                "sgu_ln_b", "sgu_w", "sgu_b", "q_norm_g", "k_norm_g", "w_out_conv", "w_out_sgu", "w_out_attn", "w_o",
                "ffn2_norm", "ffn2_w_gu", "ffn2_w_down")
BIG = {"ffn1_w_gu": "gu1", "ffn2_w_gu": "gu2", "ffn1_w_down": "d1", "ffn2_w_down": "d2", "w_in": "in",
       "w_out_conv": "oc", "w_out_sgu": "os", "w_out_attn": "oa", "w_o": "o"}
BIG_KEYS = ("gu1", "gu2", "d1", "d2", "in", "oc", "os", "oa", "o")
REPLICATED_SMALL = ("ffn1_norm", "mix_norm", "b_forget", "sgu_ln_g", "sgu_ln_b", "sgu_w", "sgu_b", "q_norm_g",
                    "k_norm_g", "ffn2_norm")
SHARDED_SMALL = ("b_gate", "conv_w")
TRANSPOSED = ("gu1", "gu2")


def _pack(arrays):
    flat = jnp.concatenate([a.reshape(-1).astype(f32) for a in arrays])
    rows = -(-flat.shape[0] // 1024) * 8
    return jnp.pad(flat, (0, rows * 128 - flat.shape[0])).reshape(rows, 128)


def _unpack(packed, shapes):
    flat = packed.reshape(-1)
    out, pos = [], 0
    for shp in shapes:
        size = 1
        for s_ in shp:
            size *= s_
        out.append(flat[pos:pos + size].reshape(shp))
        pos += size
    return out


def _natural_runs(a, b):
    runs = []
    while a < b:
        d = a // INB
        e = min(b, (d + 1) * INB)
        runs.append((d, a - d * INB, e - d * INB))
        a = e
    return runs


def _win_kernel_layout(wg):
    runs = _natural_runs(0, GATE_OFF) + _natural_runs(GATE_OFF + NH, NIN) + _natural_runs(GATE_OFF, GATE_OFF + NH)
    return jnp.concatenate([wg[d, :, a:b] for d, a, b in runs] + [jnp.zeros((D, NZ - NIN), wg.dtype)], axis=1)


def _kernel_column(n):
    return n if n < GATE_OFF else (F_OFF + n - GATE_OFF if n < GATE_OFF + NH else n - NH)


def _win_device_block(dw, d):
    cuts = sorted({d * INB, (d + 1) * INB} | {c for c in (GATE_OFF, GATE_OFF + NH) if d * INB < c < (d + 1) * INB})
    parts = [dw[:, _kernel_column(a):_kernel_column(a) + (b - a)] for a, b in zip(cuts[:-1], cuts[1:])]
    return parts[0] if len(parts) == 1 else jnp.concatenate(parts, axis=1)


def kernel(x, ffn1_norm, ffn1_w_gu, ffn1_w_down, mix_norm, w_in, b_forget, b_gate, conv_w, sgu_ln_g, sgu_ln_b, sgu_w, sgu_b, q_norm_g, k_norm_g, w_out_conv, w_out_sgu, w_out_attn, w_o, ffn2_norm, ffn2_w_gu, ffn2_w_down, loss_target, m_ffn1_norm, m_ffn1_w_gu, m_ffn1_w_down, m_mix_norm, m_w_in, m_b_forget, m_b_gate, m_conv_w, m_sgu_ln_g, m_sgu_ln_b, m_sgu_w, m_sgu_b, m_q_norm_g, m_k_norm_g, m_w_out_conv, m_w_out_sgu, m_w_out_attn, m_w_o, m_ffn2_norm, m_ffn2_w_gu, m_ffn2_w_down, v_ffn1_norm, v_ffn1_w_gu, v_ffn1_w_down, v_mix_norm, v_w_in, v_b_forget, v_b_gate, v_conv_w, v_sgu_ln_g, v_sgu_ln_b, v_sgu_w, v_sgu_b, v_q_norm_g, v_k_norm_g, v_w_out_conv, v_w_out_sgu, v_w_out_attn, v_w_o, v_ffn2_norm, v_ffn2_w_gu, v_ffn2_w_down):
    w = dict(zip(WEIGHT_NAMES, (ffn1_norm, ffn1_w_gu, ffn1_w_down, mix_norm, w_in, b_forget, b_gate, conv_w, sgu_ln_g,
                                sgu_ln_b, sgu_w, sgu_b, q_norm_g, k_norm_g, w_out_conv, w_out_sgu, w_out_attn, w_o,
                                ffn2_norm, ffn2_w_gu, ffn2_w_down)))
    mom = dict(zip(WEIGHT_NAMES, (m_ffn1_norm, m_ffn1_w_gu, m_ffn1_w_down, m_mix_norm, m_w_in, m_b_forget, m_b_gate,
                                  m_conv_w, m_sgu_ln_g, m_sgu_ln_b, m_sgu_w, m_sgu_b, m_q_norm_g, m_k_norm_g,
                                  m_w_out_conv, m_w_out_sgu, m_w_out_attn, m_w_o, m_ffn2_norm, m_ffn2_w_gu,
                                  m_ffn2_w_down)))
    var = dict(zip(WEIGHT_NAMES, (v_ffn1_norm, v_ffn1_w_gu, v_ffn1_w_down, v_mix_norm, v_w_in, v_b_forget, v_b_gate,
                                  v_conv_w, v_sgu_ln_g, v_sgu_ln_b, v_sgu_w, v_sgu_b, v_q_norm_g, v_k_norm_g,
                                  v_w_out_conv, v_w_out_sgu, v_w_out_attn, v_w_o, v_ffn2_norm, v_ffn2_w_gu,
                                  v_ffn2_w_down)))
    px, py, pc = _place()
    dev = 4 * px + 2 * py + pc
    chip = 2 * px + py

    big_names = [n for n in WEIGHT_NAMES if n in BIG]
    key_name = {BIG[n]: n for n in big_names}
    group_keys = (("gu1", "d1"), ("in", "oc", "os", "oa", "o", "small"), ("gu2", "d2"))

    def source(key, l):
        if key == "small":
            return jnp.concatenate([w["b_gate"][l], w["conv_w"][l], jnp.zeros((2, 128), f32)], axis=0)
        block = w[key_name[key]][l]
        return (block.T if key in TRANSPOSED else block).astype(bf16)

    def landing(src):
        return lax.dynamic_update_slice(lax.empty((NDEV,) + src.shape, src.dtype), src[None], (dev, 0, 0))

    groups = [[(s, landing(s)) for s in (source(k, l) for k in keys)] for l in range(2) for keys in group_keys]
    started = _gather_start("gather_start", groups)

    def weights(l, part, after):
        got = _gather_wait(f"gather_wait_{l}_{part}", started[3 * l + part], after)
        return dict(zip(group_keys[part], _gather_forward(f"gather_forward_{l}_{part}", got)))

    xl = x[0]
    saved, small, wts = [], [], []
    for l in range(2):
        ga = weights(l, 0, xl if l else None)
        wt = {"gu1": ga["gu1"][None], "d1": ga["d1"].reshape(1, 4, GU, D)}
        x1, s1 = _ffn_fwd("ffn1", xl, w["ffn1_norm"][l].reshape(1, D), wt["gu1"], wt["d1"], 0)
        gb = weights(l, 1, x1)
        p = {n: w[n][l] for n in REPLICATED_SMALL}
        p["b_gate"] = jnp.transpose(gb["small"][:, 0:3, :], (1, 0, 2)).reshape(3, D)
        p["conv_w"] = jnp.transpose(gb["small"][:, 3:6, :], (1, 0, 2)).reshape(3, D)
        p = _small_params(p)
        wt["win"] = _win_kernel_layout(gb["in"])
        wt["sq"] = jnp.stack([gb[k].reshape(D, D) for k in ("oc", "os", "oa", "o")])[None]
        x2, sm = _mixer_fwd("mix", x1, p, wt["win"], wt["sq"], 0)
        gc = weights(l, 2, x2)
        wt.update({"gu2": gc["gu2"][None], "d2": gc["d2"].reshape(1, 4, GU, D)})
        x3, s2 = _ffn_fwd("ffn2", x2, p["ffn2_norm"], wt["gu2"], wt["d2"], 0)
        saved.append((xl, x1, x2, s1, sm, s2))
        small.append(p)
        wts.append(wt)
        xl = x3
    loss_row, dx = _loss("loss", xl, loss_target[0])

    core = pc.reshape(1).astype(jnp.int32)
    buf = lambda r, c: lax.empty((2, 4, 1, r, c), bf16)
    flights = {}

    def scatter(l, part, bufs, dx):
        r1 = _rs_pair(f"rs_pair_{l}_{part}", bufs)
        ss = [_pair_sum(f"pair_sum_{l}_{k}", core, g, r) for k, g, r in zip(group_keys[part], bufs, r1)]
        flights[l, part], dx = _scatter_start(f"scatter_start_{l}_{part}", ss, dx)
        return dx

    sgrads = [None, None]
    for l in (1, 0):
        p, wt = small[l], wts[l]
        x0, x1, x2, s1, sm, s2 = saved[l]
        dx, dn2, g_gu2, g_d2 = _ffn_bwd("ffn2", dx, x2, p["ffn2_norm"], wt["gu2"], wt["d2"], 0, s2, buf(GU, D), buf(GU // 2, D))
        dx = scatter(l, 2, [g_gu2, g_d2], dx)
        dx, sg, dwin, gsq = _mixer_bwd("mix", dx, x1, p, wt["win"], wt["sq"], 0, sm, [buf(128, D) for _ in range(4)])
        g_in = jnp.stack([jnp.stack([_win_device_block(dwin, 2 * q + c)[None] for q in range(4)]) for c in range(2)])
        dx = scatter(l, 1, [g_in] + gsq, dx)
        dx, dn1, g_gu1, g_d1 = _ffn_bwd("ffn1", dx, x0, p["ffn1_norm"], wt["gu1"], wt["d1"], 0, s1, buf(GU, D), buf(GU // 2, D))
        dx = scatter(l, 0, [g_gu1, g_d1], dx)
        sg["ffn1_norm"] = dn1
        sg["ffn2_norm"] = dn2
        sgrads[l] = sg

    nat = [_small_grads_natural(sgrads[l]) for l in range(2)]
    order = REPLICATED_SMALL + SHARDED_SMALL
    part = _pack([jnp.stack([nat[0][n], nat[1][n]]) for n in order] + [loss_row[0, 0:1]])
    small_flight, dx = _gather_start("small_start", [[(part, landing(part))]], carry=dx)

    grads, delta, new_m, new_v = {}, {}, {}, {}
    after = dx
    for part in (2, 1, 0):
        sets = []
        for l in (1, 0):
            s_all, r2_all = _scatter_wait(f"scatter_wait_{l}_{part}", flights[l, part], after)
            sets.append([(lax.dynamic_index_in_dim(s, chip, 0, keepdims=False)[0], r) for s, r in zip(s_all, r2_all)])
        for i, k in enumerate(k for k in group_keys[part] if k != "small"):
            n = key_name[k]
            view = (lambda a: jnp.swapaxes(a, 1, 2)) if k in TRANSPOSED else (lambda a: a)
            outs = _adamw("adamw_" + k, view(w[n]), view(mom[n]), view(var[n]), [sets[1][i], sets[0][i]])
            grads[n], delta[n], new_m[n], new_v[n] = [view(o) for o in outs]
            after = outs[1]

    blocks = _gather_forward("small_forward", _gather_wait("small_wait", small_flight[0], after))
    total = _sum_blocks("small_sum", blocks[0])
    full_shapes = [(2,) + tuple(nat[0][n].shape) for n in order] + [(1,)]
    summed = dict(zip(order + ("loss",), _unpack(total, full_shapes)))
    for n in REPLICATED_SMALL:
        grads[n] = summed[n]
    for n in SHARDED_SMALL:
        grads[n] = lax.dynamic_slice_in_dim(summed[n], dev * 128, 128, axis=2)
    wp = _pack([w[n] for n in order])
    gp = _pack([grads[n] for n in order])
    mp = _pack([mom[n] for n in order])
    vp = _pack([var[n] for n in order])
    dpk, mpk, vpk = _adamw_small("adamw_small", wp, gp, mp, vp)
    local_shapes = [tuple(w[n].shape) for n in order]
    for dst, packed in ((delta, dpk), (new_m, mpk), (new_v, vpk)):
        dst.update(zip(order, _unpack(packed, local_shapes)))

    loss = summed["loss"][0]
    return (loss, dx[None], *[grads[n] for n in WEIGHT_NAMES], *[delta[n] for n in WEIGHT_NAMES],
            *[new_m[n] for n in WEIGHT_NAMES], *[new_v[n] for n in WEIGHT_NAMES])
```

```python
import functools

import jax
import jax.numpy as jnp
from jax import lax
from jax.experimental import pallas as pl
from jax.experimental.pallas import tpu as pltpu

f32 = jnp.float32
bf16 = jnp.bfloat16
S = jax.ShapeDtypeStruct
MESH = pl.DeviceIdType.MESH

D = 1024
NH = 8
HD = 128
NDEV = 8
GU = 704
NIN = 11272
INB = 1409
GATE_OFF = 8192
F_OFF = 11264
NZ = 11776
RMS_EPS = 1e-6
LN_EPS = 1e-5
ATT_SCALE = HD ** -0.5
NEG = -1e30
INV_SQRT2 = 0.7071067811865476
INV_SQRT2PI = 0.3989422804014327

ADAM_LR = 0.001
ADAM_B1 = 0.9
ADAM_B2 = 0.999
ADAM_EPS = 1e-08
ADAM_WD = 0.01
ADAM_STEP = 10

TT = 512
VMEM_LIMIT = 56 * 1024 * 1024


def _cp(*sem):
    return pltpu.CompilerParams(dimension_semantics=sem, vmem_limit_bytes=VMEM_LIMIT)


def _bs(shape, fn):
    return pl.BlockSpec(shape, fn)


NN = (((1,), (0,)), ((), ()))
NT = (((1,), (1,)), ((), ()))
TN = (((0,), (0,)), ((), ()))


def _mm(name, a, b, *, grid, a_spec, b_spec, out_shape, out_spec, dims, acc_shape, res=None, res_spec=None,
        alpha=1.0, alias=None, split_rows=None):
    nk = grid[2]

    def body(*refs):
        a_ref, b_ref = refs[0], refs[1]
        pos = 2
        res_ref = None
        if res is not None:
            res_ref = refs[pos]
            pos += 1
        if alias is not None:
            pos += 1
        o_ref = refs[pos]
        acc_ref = refs[pos + 1] if nk > 1 else None
        part = lax.dot_general(a_ref[...].astype(bf16), b_ref[...].astype(bf16), dims, preferred_element_type=f32)

        def finish(acc):
            if alpha != 1.0:
                acc = alpha * acc
            if res_ref is not None:
                acc = res_ref[...] + acc
            if split_rows is None:
                o_ref[...] = acc.astype(o_ref.dtype)
            else:
                o_ref[0] = acc[:split_rows].astype(o_ref.dtype)
                o_ref[1] = acc[split_rows:].astype(o_ref.dtype)

        if nk == 1:
            finish(part)
        else:
            k = pl.program_id(2)

            @pl.when(k == 0)
            def _():
                acc_ref[...] = part

            @pl.when(k > 0)
            def _():
                acc_ref[...] += part

            @pl.when(k == nk - 1)
            def _():
                finish(acc_ref[...])

    operands = [a, b]
    in_specs = [a_spec, b_spec]
    if res is not None:
        operands.append(res)
        in_specs.append(res_spec)
    aliases = {}
    if alias is not None:
        aliases = {len(operands): 0}
        operands.append(alias)
        in_specs.append(pl.BlockSpec(memory_space=pl.ANY))
    return pl.pallas_call(
        body, name=name, grid=grid, in_specs=in_specs, out_specs=out_spec, out_shape=out_shape,
        scratch_shapes=[pltpu.VMEM(acc_shape, f32)] if nk > 1 else [],
        input_output_aliases=aliases,
        compiler_params=_cp("parallel", "parallel", "arbitrary"),
    )(*operands)


def _tile(n, t):
    return t if n % t == 0 and n >= t else n


DZ_TILE = 512


def _dz_matmul(name, pieces, other, weight_grad):
    T = pieces[0].shape[0]
    counts = [p.shape[1] // DZ_TILE for p in pieces]
    starts = [sum(counts[:i]) for i in range(len(counts))]
    steps = sum(counts)
    npc = len(pieces)

    def body(*refs):
        prefs, o_ref, rest = refs[:npc], refs[npc], refs[npc + 1:]
        k = pl.program_id(0)
        if not weight_grad:
            out_ref, acc_ref = rest

            @pl.when(k == 0)
            def _():
                acc_ref[...] = jnp.zeros_like(acc_ref)

        for p_ref, s, c in zip(prefs, starts, counts):
            @pl.when((k >= s) & (k < s + c))
            def _():
                if weight_grad:
                    rest[0][...] = lax.dot_general(o_ref[...], p_ref[...], TN, preferred_element_type=f32).astype(bf16)
                else:
                    acc_ref[...] += lax.dot_general(p_ref[...], o_ref[...], NT, preferred_element_type=f32)

        if not weight_grad:
            @pl.when(k == steps - 1)
            def _():
                out_ref[...] = acc_ref[...]

    piece_specs = [_bs((T, DZ_TILE), functools.partial(lambda k, s, c: (0, jnp.clip(k - s, 0, c - 1)), s=s, c=c))
                   for s, c in zip(starts, counts)]
    if weight_grad:
        other_spec, out_spec, out_shape, scratch = _bs((T, D), lambda k: (0, 0)), _bs((D, DZ_TILE), lambda k: (0, k)), S((D, NZ), bf16), []
    else:
        other_spec, out_spec, out_shape = _bs((D, DZ_TILE), lambda k: (0, k)), _bs((T, D), lambda k: (0, 0)), S((T, D), f32)
        scratch = [pltpu.VMEM((T, D), f32)]
    return pl.pallas_call(body, name=name, grid=(steps,), in_specs=piece_specs + [other_spec], out_specs=out_spec,
                          out_shape=out_shape, scratch_shapes=scratch, compiler_params=_cp("arbitrary"))(*pieces, other)


def _row(cb=0, w=D):
    return _bs((TT, w), lambda i: (i, cb))


def _vec(rows=1, w=D):
    return _bs((rows, w), lambda i: (0, 0))


def _acc_store(i, ref, val):
    @pl.when(i == 0)
    def _():
        ref[...] = val

    @pl.when(i > 0)
    def _():
        ref[...] += val


def _rms_fwd(name, x, g):
    T = x.shape[0]

    def body(x_ref, g_ref, o_ref):
        xv = x_ref[...]
        r = lax.rsqrt(jnp.mean(xv * xv, axis=-1, keepdims=True) + RMS_EPS)
        o_ref[...] = (xv * r * g_ref[...]).astype(bf16)

    return pl.pallas_call(body, name=name, grid=(T // TT,), in_specs=[_row(), _vec()], out_specs=_row(),
                          out_shape=S((T, D), bf16), compiler_params=_cp("parallel"))(x, g)


def _rms_bwd(name, dh, x, g, dres):
    T = x.shape[0]

    def body(dh_ref, x_ref, g_ref, dres_ref, dx_ref, dg_ref):
        i = pl.program_id(0)
        xv = x_ref[...]
        r = lax.rsqrt(jnp.mean(xv * xv, axis=-1, keepdims=True) + RMS_EPS)
        xhat = xv * r
        dh_v = dh_ref[...]
        dyg = dh_v * g_ref[...]
        m = jnp.mean(dyg * xhat, axis=-1, keepdims=True)
        dx_ref[...] = dres_ref[...] + r * (dyg - xhat * m)
        _acc_store(i, dg_ref, jnp.sum(dh_v * xhat, axis=0, keepdims=True))

    return pl.pallas_call(body, name=name, grid=(T // TT,), in_specs=[_row(), _row(), _vec(), _row()],
                          out_specs=[_row(), _vec()], out_shape=[S((T, D), f32), S((1, D), f32)],
                          compiler_params=_cp("arbitrary"))(dh, x, g, dres)


def _sigmoid(x):
    return 1.0 / (1.0 + jnp.exp(-x))


def _swiglu_fwd(name, h, wgu, l):
    T = h.shape[0]

    def body(h_ref, wg_ref, wu_ref, gu_ref, a_ref):
        hv = h_ref[...]
        g = lax.dot_general(hv, wg_ref[...], NT, preferred_element_type=f32)
        u = lax.dot_general(hv, wu_ref[...], NT, preferred_element_type=f32)
        gu_ref[0] = g
        gu_ref[1] = u
        a_ref[...] = (g * _sigmoid(g) * u).astype(bf16)

    return pl.pallas_call(
        body, name=name, grid=(4,),
        in_specs=[_bs((T, D), lambda j: (0, 0)), _bs((None, None, GU, D), lambda j: (l, j, 0, 0)),
                  _bs((None, None, GU, D), lambda j: (l, j + 4, 0, 0))],
        out_specs=[_bs((2, None, T, GU), lambda j: (0, j, 0, 0)), _bs((None, T, GU), lambda j: (j, 0, 0))],
        out_shape=[S((2, 4, T, GU), f32), S((4, T, GU), bf16)], compiler_params=_cp("parallel"))(h, wgu, wgu)


def _swiglu_bwd(name, dxo, wd, gu, l):
    T = dxo.shape[0]
    tm = _tile(T, 1024)

    def body(dx_ref, wd_ref, g_ref, u_ref, o_ref):
        da = 0.5 * lax.dot_general(dx_ref[...].astype(bf16), wd_ref[...], NT, preferred_element_type=f32)
        g = g_ref[...]
        sg = _sigmoid(g)
        o_ref[0] = (da * u_ref[...] * (sg + g * sg * (1.0 - sg))).astype(bf16)
        o_ref[1] = (da * g * sg).astype(bf16)

    return pl.pallas_call(
        body, name=name, grid=(T // tm, 4),
        in_specs=[_bs((tm, D), lambda i, j: (i, 0)), _bs((None, None, GU, D), lambda i, j: (l, j, 0, 0)),
                  _bs((None, None, tm, GU), lambda i, j: (0, j, i, 0)), _bs((None, None, tm, GU), lambda i, j: (1, j, i, 0))],
        out_specs=_bs((2, None, tm, GU), lambda i, j: (0, j, i, 0)), out_shape=S((2, 4, T, GU), bf16),
        compiler_params=_cp("parallel", "parallel"))(dxo, wd, gu, gu)


def _loss(name, y, tgt):
    T = y.shape[0]

    def body(y_ref, t_ref, l_ref, dy_ref):
        i = pl.program_id(0)
        e = y_ref[...] - t_ref[...]
        dy_ref[...] = e * (1.0 / D)
        s = 0.5 * jnp.sum(jnp.mean(e * e, axis=-1, keepdims=True))
        _acc_store(i, l_ref, jnp.broadcast_to(s, (1, 128)))

    return pl.pallas_call(body, name=name, grid=(T // TT,), in_specs=[_row(), _row()],
                          out_specs=[_vec(1, 128), _row()], out_shape=[S((1, 128), f32), S((T, D), f32)],
                          compiler_params=_cp("arbitrary"))(y, tgt)


def _prev8(T, cb):
    return _bs((8, D), lambda i: (jnp.maximum(i * (TT // 8) - 1, 0), cb))


def _next8(T, cb):
    return _bs((8, D), lambda i: (jnp.minimum((i + 1) * (TT // 8), T // 8 - 1), cb))


def _conv_taps(i, ac_ref, ax_ref, pc_ref, px_ref):
    ca = ac_ref[...] * ax_ref[...]
    keep = (i > 0).astype(f32)
    p1 = pc_ref[7:8, :] * px_ref[7:8, :] * keep
    p2 = pc_ref[6:7, :] * px_ref[6:7, :] * keep
    row = lax.broadcasted_iota(jnp.int32, ca.shape, 0)
    s1 = jnp.where(row == 0, p1, pltpu.roll(ca, 1, 0))
    s2 = jnp.where(row == 0, p2, jnp.where(row == 1, p1, pltpu.roll(ca, 2, 0)))
    return ca, s1, s2


def _conv_fwd(name, z, cw):
    T = z.shape[0]

    def body(ab_ref, ac_ref, ax_ref, pc_ref, px_ref, w_ref, o_ref):
        i = pl.program_id(0)
        ca, s1, s2 = _conv_taps(i, ac_ref, ax_ref, pc_ref, px_ref)
        cv = w_ref[0:1, :] * s2 + w_ref[1:2, :] * s1 + w_ref[2:3, :] * ca
        o_ref[...] = (ab_ref[...] * cv).astype(bf16)

    return pl.pallas_call(
        body, name=name, grid=(T // TT,),
        in_specs=[_row(0), _row(1), _row(2), _prev8(T, 1), _prev8(T, 2), _vec(3)],
        out_specs=_row(), out_shape=S((T, D), bf16), compiler_params=_cp("parallel"))(z, z, z, z, z, cw)


def _conv_bwd(name, dya, z, cw):
    T = z.shape[0]
    n = T // TT

    def body(dya_ref, ab_ref, ac_ref, ax_ref, pc_ref, px_ref, ndya_ref, nab_ref, w_ref, dz_ref, dw_ref):
        i = pl.program_id(0)
        ca, s1, s2 = _conv_taps(i, ac_ref, ax_ref, pc_ref, px_ref)
        w0, w1, w2 = w_ref[0:1, :], w_ref[1:2, :], w_ref[2:3, :]
        cv = w0 * s2 + w1 * s1 + w2 * ca
        dya_v = dya_ref[...]
        ab = ab_ref[...]
        dcv = dya_v * ab
        keep = (i < n - 1).astype(f32)
        n1 = ndya_ref[0:1, :] * nab_ref[0:1, :] * keep
        n2 = ndya_ref[1:2, :] * nab_ref[1:2, :] * keep
        row = lax.broadcasted_iota(jnp.int32, dcv.shape, 0)
        f1 = jnp.where(row == TT - 1, n1, pltpu.roll(dcv, TT - 1, 0))
        f2 = jnp.where(row == TT - 1, n2, jnp.where(row == TT - 2, n1, pltpu.roll(dcv, TT - 2, 0)))
        dca = w2 * dcv + w1 * f1 + w0 * f2
        dz_ref[:, 0:D] = (dya_v * cv).astype(bf16)
        dz_ref[:, D:2 * D] = (dca * ax_ref[...]).astype(bf16)
        dz_ref[:, 2 * D:3 * D] = (dca * ac_ref[...]).astype(bf16)
        dw = jnp.concatenate([jnp.sum(dcv * s2, axis=0, keepdims=True), jnp.sum(dcv * s1, axis=0, keepdims=True),
                              jnp.sum(dcv * ca, axis=0, keepdims=True)], axis=0)
        _acc_store(i, dw_ref, dw)

    return pl.pallas_call(
        body, name=name, grid=(n,),
        in_specs=[_row(), _row(0), _row(1), _row(2), _prev8(T, 1), _prev8(T, 2), _next8(T, 0), _next8(T, 0), _vec(3)],
        out_specs=[_row(0, 3 * D), _vec(3)], out_shape=[S((T, 3 * D), bf16), S((3, D), f32)],
        compiler_params=_cp("arbitrary"))(dya, z, z, z, z, z, dya, z, cw)


def _gelu(x):
    return 0.5 * x * (1.0 + lax.erf(x * INV_SQRT2))


def _gelu_grad(x):
    return 0.5 * (1.0 + lax.erf(x * INV_SQRT2)) + x * jnp.exp(-0.5 * x * x) * INV_SQRT2PI


def _ln_stats(vv):
    mu = jnp.mean(vv, axis=-1, keepdims=True)
    xc = vv - mu
    rstd = lax.rsqrt(jnp.mean(xc * xc, axis=-1, keepdims=True) + LN_EPS)
    return xc * rstd, rstd


def _tril_w(w_ref, g):
    r = lax.broadcasted_iota(jnp.int32, (HD, HD), 0)
    c = lax.broadcasted_iota(jnp.int32, (HD, HD), 1)
    return jnp.where(c <= r, w_ref[g], 0.0).astype(bf16)


def _sgu_fwd(name, z, ln_g, ln_b, w_s, bmap):
    T = z.shape[0]

    def body(su_ref, sv_ref, lg_ref, lb_ref, w_ref, bm_ref, o_ref, vn_ref):
        xhat, _ = _ln_stats(_gelu(sv_ref[...]))
        vn_ref[...] = (xhat * lg_ref[...] + lb_ref[...]).astype(bf16)
        for g in range(NH):
            w = _tril_w(w_ref, g)
            cs = slice(g * HD, (g + 1) * HD)
            for c in range(TT // HD):
                rs = slice(c * HD, (c + 1) * HD)
                s = jnp.dot(w, vn_ref[rs, cs], preferred_element_type=f32) + bm_ref[:, cs]
                o_ref[rs, cs] = (_gelu(su_ref[rs, cs]) * s).astype(bf16)

    return pl.pallas_call(
        body, name=name, grid=(T // TT,),
        in_specs=[_row(3), _row(4), _vec(), _vec(), _bs((NH, HD, HD), lambda i: (0, 0, 0)), _vec(HD)],
        out_specs=_row(), out_shape=S((T, D), bf16), scratch_shapes=[pltpu.VMEM((TT, D), bf16)],
        compiler_params=_cp("parallel"))(z, z, ln_g, ln_b, w_s, bmap)


def _sgu_bwd(name, dyb, z, ln_g, ln_b, w_s, bmap):
    T = z.shape[0]

    def body(dyb_ref, su_ref, sv_ref, lg_ref, lb_ref, w_ref, bm_ref, dz_ref, dlg_ref, dlb_ref, dw_ref, db_ref,
             vn_ref, du_ref, dvn_ref):
        i = pl.program_id(0)
        sv = sv_ref[...]
        xhat, rstd = _ln_stats(_gelu(sv))
        vn_ref[...] = (xhat * lg_ref[...] + lb_ref[...]).astype(bf16)
        r = lax.broadcasted_iota(jnp.int32, (HD, HD), 0)
        cc = lax.broadcasted_iota(jnp.int32, (HD, HD), 1)
        for g in range(NH):
            w = _tril_w(w_ref, g)
            cs = slice(g * HD, (g + 1) * HD)
            dw = jnp.zeros((HD, HD), f32)
            db = jnp.zeros((HD, 1), f32)
            for c in range(TT // HD):
                rs = slice(c * HD, (c + 1) * HD)
                vnb = vn_ref[rs, cs]
                s = jnp.dot(w, vnb, preferred_element_type=f32) + bm_ref[:, cs]
                dy = dyb_ref[rs, cs]
                du_ref[rs, cs] = dy * s
                ds = dy * _gelu(su_ref[rs, cs])
                ds16 = ds.astype(bf16)
                dvn_ref[rs, cs] = lax.dot_general(w, ds16, TN, preferred_element_type=f32)
                dw = dw + lax.dot_general(ds16, vnb, NT, preferred_element_type=f32)
                db = db + jnp.sum(ds, axis=1, keepdims=True)
            dw = jnp.where(cc <= r, dw, 0.0)

            @pl.when(i == 0)
            def _():
                dw_ref[g] = dw
                db_ref[:, g:g + 1] = db

            @pl.when(i > 0)
            def _():
                dw_ref[g] += dw
                db_ref[:, g:g + 1] += db

        dvn = dvn_ref[...]
        dxh = dvn * lg_ref[...]
        m1 = jnp.mean(dxh, axis=-1, keepdims=True)
        m2 = jnp.mean(dxh * xhat, axis=-1, keepdims=True)
        dvv = rstd * (dxh - m1 - xhat * m2)
        dz_ref[:, 0:D] = (du_ref[...] * _gelu_grad(su_ref[...])).astype(bf16)
        dz_ref[:, D:2 * D] = (dvv * _gelu_grad(sv)).astype(bf16)
        _acc_store(i, dlg_ref, jnp.sum(dvn * xhat, axis=0, keepdims=True))
        _acc_store(i, dlb_ref, jnp.sum(dvn, axis=0, keepdims=True))

    return pl.pallas_call(
        body, name=name, grid=(T // TT,),
        in_specs=[_row(), _row(3), _row(4), _vec(), _vec(), _bs((NH, HD, HD), lambda i: (0, 0, 0)), _vec(HD)],
        out_specs=[_row(0, 2 * D), _vec(), _vec(), _bs((NH, HD, HD), lambda i: (0, 0, 0)), _bs((HD, NH), lambda i: (0, 0))],
        out_shape=[S((T, 2 * D), bf16), S((1, D), f32), S((1, D), f32), S((NH, HD, HD), f32), S((HD, NH), f32)],
        scratch_shapes=[pltpu.VMEM((TT, D), bf16), pltpu.VMEM((TT, D), f32), pltpu.VMEM((TT, D), f32)],
        compiler_params=_cp("arbitrary"))(dyb, z, z, ln_g, ln_b, w_s, bmap)


def _qk_fwd(name, z, qg, kg, bf):
    T = z.shape[0]

    def body(q_ref, k_ref, v_ref, zf_ref, qg_ref, kg_ref, bf_ref, qn_ref, kn_ref, vb_ref, lf_ref):
        for h in range(NH):
            cs = slice(h * HD, (h + 1) * HD)
            for src, gain, dst in ((q_ref, qg_ref, qn_ref), (k_ref, kg_ref, kn_ref)):
                xv = src[:, cs]
                r = lax.rsqrt(jnp.mean(xv * xv, axis=-1, keepdims=True) + RMS_EPS)
                dst[:, cs] = (xv * r * gain[:, cs]).astype(bf16)
        vb_ref[...] = v_ref[...].astype(bf16)
        xf = zf_ref[...] + bf_ref[...]
        lf_ref[...] = jnp.minimum(xf, 0.0) - jnp.log1p(jnp.exp(-jnp.abs(xf)))

    return pl.pallas_call(
        body, name=name, grid=(T // TT,),
        in_specs=[_row(5), _row(6), _row(7), _bs((TT, 128), lambda i: (i, F_OFF // 128)), _vec(), _vec(), _vec(1, 128)],
        out_specs=[_row(), _row(), _row(), _bs((TT, 128), lambda i: (i, 0))],
        out_shape=[S((T, D), bf16), S((T, D), bf16), S((T, D), bf16), S((T, 128), f32)],
        compiler_params=_cp("parallel"))(z, z, z, z, qg, kg, bf)


def _cum_fwd(name, logf):
    T = logf.shape[0]

    def body(lf_ref, ccol_ref, crow_ref, c_ref):
        c = lf_ref[...]
        row = lax.broadcasted_iota(jnp.int32, c.shape, 0)
        s = 1
        while s < T:
            c = c + jnp.where(row >= s, pltpu.roll(c, s, 0), 0.0)
            s *= 2
        c_ref[...] = c
        crow_ref[...] = c.T[0:NH, :]
        for h in range(NH):
            ccol_ref[h] = jnp.broadcast_to(c_ref[:, h:h + 1], (T, 128))

    return pl.pallas_call(body, name=name, out_shape=[S((NH, T, 128), f32), S((NH, T), f32)],
                          scratch_shapes=[pltpu.VMEM((T, 128), f32)],
                          compiler_params=pltpu.CompilerParams(vmem_limit_bytes=VMEM_LIMIT))(logf)


ATT_TILE = 1024


def _fold(x, op=jnp.add):
    acc = x[:, 0:128]
    for t in range(1, x.shape[1] // 128):
        acc = op(acc, x[:, t * 128:(t + 1) * 128])
    return acc


def _to_row(col):
    return jnp.broadcast_to(col, (col.shape[0], 128)).T[0:1, :]


def _causal(t, keys_down=False):
    r = lax.broadcasted_iota(jnp.int32, (t, t), 0)
    c = lax.broadcasted_iota(jnp.int32, (t, t), 1)
    return r <= c if keys_down else c <= r


def _attn_fwd(name, qn, kn, vb, ccol, crow3):
    T = qn.shape[0]
    tq = _tile(T, ATT_TILE)
    nq = T // tq

    def body(q_ref, k_ref, v_ref, cc_ref, cr_ref, o_ref, lse_ref, lser_ref, s_ref):
        qi = pl.program_id(1)
        q = q_ref[...]
        cq = cc_ref[:, 0:1]

        def logits(off):
            s = lax.dot_general(q, k_ref[pl.ds(off, tq), :], NT, preferred_element_type=f32) * ATT_SCALE
            return s + cq - cr_ref[:, pl.ds(off, tq)]

        def below(j, mvec):
            off = pl.multiple_of(j * tq, tq)
            s = logits(off)
            s_ref[:, pl.ds(off, tq)] = s
            return jnp.maximum(mvec, _fold(s, jnp.maximum))

        mvec = lax.fori_loop(0, qi, below, jnp.full((tq, 128), NEG, f32))
        off = pl.multiple_of(qi * tq, tq)
        s = jnp.where(_causal(tq), logits(off), NEG)
        s_ref[:, pl.ds(off, tq)] = s
        m = jnp.max(jnp.maximum(mvec, _fold(s, jnp.maximum)), axis=1, keepdims=True)

        def weigh(j, carry):
            lvec, acc = carry
            off = pl.multiple_of(j * tq, tq)
            p = jnp.exp(s_ref[:, pl.ds(off, tq)] - m)
            acc = acc + jnp.dot(p.astype(bf16), v_ref[pl.ds(off, tq), :], preferred_element_type=f32)
            return lvec + _fold(p), acc

        lvec, acc = lax.fori_loop(0, qi + 1, weigh, (jnp.zeros((tq, 128), f32), jnp.zeros((tq, HD), f32)))
        l = jnp.sum(lvec, axis=1, keepdims=True)
        o_ref[...] = acc / l
        lse = m + jnp.log(l)
        lse_ref[...] = jnp.broadcast_to(lse, (tq, 128))
        lser_ref[...] = _to_row(lse)

    return pl.pallas_call(
        body, name=name, grid=(NH, nq),
        in_specs=[_bs((tq, HD), lambda h, i: (i, h)), _bs((T, HD), lambda h, i: (0, h)), _bs((T, HD), lambda h, i: (0, h)),
                  _bs((None, tq, 128), lambda h, i: (h, i, 0)), _bs((None, 1, T), lambda h, i: (h, 0, 0))],
        out_specs=[_bs((tq, HD), lambda h, i: (i, h)), _bs((None, tq, 128), lambda h, i: (h, i, 0)),
                   _bs((None, 1, tq), lambda h, i: (h, 0, i))],
        out_shape=[S((T, D), f32), S((NH, T, 128), f32), S((NH, 1, T), f32)],
        scratch_shapes=[pltpu.VMEM((tq, T), f32)],
        compiler_params=_cp("parallel", "parallel"))(qn, kn, vb, ccol, crow3)


def _attn_dq(name, qn, kn, vb, do, lse, ccol, crow3):
    T = qn.shape[0]
    tq = _tile(T, ATT_TILE)
    nq = T // tq

    def body(q_ref, k_ref, v_ref, do_ref, lse_ref, cc_ref, cr_ref, dq_ref, dlr_ref, p_ref, dp_ref):
        qi = pl.program_id(1)
        q = q_ref[...]
        do16 = do_ref[...].astype(bf16)
        base = cc_ref[:, 0:1] - lse_ref[:, 0:1]

        def probs(off):
            s = lax.dot_general(q, k_ref[pl.ds(off, tq), :], NT, preferred_element_type=f32) * ATT_SCALE
            return jnp.exp(s + base - cr_ref[:, pl.ds(off, tq)])

        def keep(off, p, dvec):
            dp = lax.dot_general(do16, v_ref[pl.ds(off, tq), :], NT, preferred_element_type=f32)
            p_ref[:, pl.ds(off, tq)] = p
            dp_ref[:, pl.ds(off, tq)] = dp
            return dvec + _fold(p * dp)

        def below(j, dvec):
            off = pl.multiple_of(j * tq, tq)
            return keep(off, probs(off), dvec)

        dvec = lax.fori_loop(0, qi, below, jnp.zeros((tq, 128), f32))
        off = pl.multiple_of(qi * tq, tq)
        dvec = keep(off, jnp.where(_causal(tq), probs(off), 0.0), dvec)
        delta = jnp.sum(dvec, axis=1, keepdims=True)

        def grad(j, acc):
            off = pl.multiple_of(j * tq, tq)
            ds = p_ref[:, pl.ds(off, tq)] * (dp_ref[:, pl.ds(off, tq)] - delta)
            return acc + jnp.dot(ds.astype(bf16), k_ref[pl.ds(off, tq), :], preferred_element_type=f32)

        dq_ref[...] = lax.fori_loop(0, qi + 1, grad, jnp.zeros((tq, HD), f32)) * ATT_SCALE
        dlr_ref[...] = _to_row(delta)

    qb = lambda h, i: (i, h)
    full = lambda h, i: (0, h)
    col = lambda h, i: (h, i, 0)
    return pl.pallas_call(
        body, name=name, grid=(NH, nq),
        in_specs=[_bs((tq, HD), qb), _bs((T, HD), full), _bs((T, HD), full), _bs((tq, HD), qb),
                  _bs((None, tq, 128), col), _bs((None, tq, 128), col), _bs((None, 1, T), lambda h, i: (h, 0, 0))],
        out_specs=[_bs((tq, HD), qb), _bs((None, 1, tq), lambda h, i: (h, 0, i))],
        out_shape=[S((T, D), f32), S((NH, 1, T), f32)],
        scratch_shapes=[pltpu.VMEM((tq, T), f32), pltpu.VMEM((tq, T), f32)],
        compiler_params=_cp("parallel", "parallel"))(qn, kn, vb, do, lse, ccol, crow3)


def _attn_dkv(name, qn, kn, vb, do, lser3, dlr3, ccol, crow3):
    T = qn.shape[0]
    tk = _tile(T, ATT_TILE)
    nk = T // tk

    def body(q_ref, k_ref, v_ref, do_ref, lser_ref, dlr_ref, cc_ref, cr_ref, dk_ref, dv_ref, cs_ref):
        h = pl.program_id(0)
        kj = pl.program_id(1)

        @pl.when((h == 0) & (kj == 0))
        def _():
            cs_ref[...] = jnp.zeros_like(cs_ref)

        kb = k_ref[...]
        vv = v_ref[...]
        ckey = cc_ref[:, 0:1]

        def block(off, diagonal):
            rows = pl.ds(off, tk)
            qb = q_ref[rows, :]
            do16 = do_ref[rows, :].astype(bf16)
            st = lax.dot_general(kb, qb, NT, preferred_element_type=f32) * ATT_SCALE
            pt = jnp.exp(st + (cr_ref[:, rows] - lser_ref[:, rows]) - ckey)
            if diagonal:
                pt = jnp.where(_causal(tk, keys_down=True), pt, 0.0)
            dpt = lax.dot_general(vv, do16, NT, preferred_element_type=f32)
            dst = pt * (dpt - dlr_ref[:, rows])
            ddv = jnp.dot(pt.astype(bf16), do16, preferred_element_type=f32)
            ddk = jnp.dot(dst.astype(bf16), qb, preferred_element_type=f32)
            return ddk, ddv, _fold(dst)

        def above(i, carry):
            ddk, ddv, dcs = block(pl.multiple_of(i * tk, tk), False)
            return carry[0] + ddk, carry[1] + ddv, carry[2] + dcs

        off = pl.multiple_of(kj * tk, tk)
        dk, dv, cs = lax.fori_loop(kj + 1, nk, above, block(off, True))
        dk_ref[...] = dk * ATT_SCALE
        dv_ref[...] = dv
        lane = lax.broadcasted_iota(jnp.int32, (tk, 128), 1)
        cs_ref[pl.ds(off, tk), :] += jnp.where(lane == h, jnp.sum(cs, axis=1, keepdims=True), 0.0)

    full = lambda h, j: (0, h)
    blk = lambda h, j: (j, h)
    row = lambda h, j: (h, 0, 0)
    return pl.pallas_call(
        body, name=name, grid=(NH, nk),
        in_specs=[_bs((T, HD), full), _bs((tk, HD), blk), _bs((tk, HD), blk), _bs((T, HD), full), _bs((None, 1, T), row),
                  _bs((None, 1, T), row), _bs((None, tk, 128), lambda h, j: (h, j, 0)), _bs((None, 1, T), row)],
        out_specs=[_bs((tk, HD), blk), _bs((tk, HD), blk), _bs((T, 128), lambda h, j: (0, 0))],
        out_shape=[S((T, D), f32), S((T, D), f32), S((T, 128), f32)],
        compiler_params=_cp("arbitrary", "arbitrary"))(qn, kn, vb, do, lser3, dlr3, ccol, crow3)


def _forget_bwd(name, cs, z, bf):
    T = cs.shape[0]

    def body(cs_ref, zf_ref, bf_ref, dz_ref, db_ref):
        c = -cs_ref[...]
        row = lax.broadcasted_iota(jnp.int32, c.shape, 0)
        s = 1
        while s < T:
            c = c + jnp.where(row + s < T, pltpu.roll(c, T - s, 0), 0.0)
            s *= 2
        xf = zf_ref[...] + bf_ref[...]
        lane = lax.broadcasted_iota(jnp.int32, c.shape, 1)
        dxf = jnp.where(lane < NH, c / (1.0 + jnp.exp(xf)), 0.0)
        dz_ref[...] = jnp.zeros_like(dz_ref)
        dz_ref[:, 0:128] = dxf.astype(bf16)
        db_ref[...] = jnp.sum(dxf, axis=0, keepdims=True)

    return pl.pallas_call(
        body, name=name, grid=(1,),
        in_specs=[_bs((T, 128), lambda i: (0, 0)), _bs((T, 128), lambda i: (0, F_OFF // 128)), _vec(1, 128)],
        out_specs=[_bs((T, NZ - F_OFF), lambda i: (0, 0)), _vec(1, 128)],
        out_shape=[S((T, NZ - F_OFF), bf16), S((1, 128), f32)], compiler_params=_cp("arbitrary"))(cs, z, bf)


def _qk_bwd(name, dqn, dkn, dv, z, qg, kg):
    T = z.shape[0]

    def body(dq_ref, dk_ref, dv_ref, q_ref, k_ref, qg_ref, kg_ref, dz_ref, dqg_ref, dkg_ref, g_ref):
        i = pl.program_id(0)
        for n, (src, dsrc, gain, dgain) in enumerate(((q_ref, dq_ref, qg_ref, dqg_ref), (k_ref, dk_ref, kg_ref, dkg_ref))):
            for h in range(NH):
                cs = slice(h * HD, (h + 1) * HD)
                xv = src[:, cs]
                r = lax.rsqrt(jnp.mean(xv * xv, axis=-1, keepdims=True) + RMS_EPS)
                xhat = xv * r
                dy = dsrc[:, cs]
                dyg = dy * gain[:, cs]
                m = jnp.mean(dyg * xhat, axis=-1, keepdims=True)
                dz_ref[:, n * D + h * HD:n * D + (h + 1) * HD] = (r * (dyg - xhat * m)).astype(bf16)
                g_ref[:, cs] = jnp.sum(dy * xhat, axis=0, keepdims=True)
            _acc_store(i, dgain, g_ref[...])
        dz_ref[:, 2 * D:3 * D] = dv_ref[...].astype(bf16)

    return pl.pallas_call(
        body, name=name, grid=(T // TT,),
        in_specs=[_row(), _row(), _row(), _row(5), _row(6), _vec(), _vec()],
        out_specs=[_row(0, 3 * D), _vec(), _vec()], out_shape=[S((T, 3 * D), bf16), S((1, D), f32), S((1, D), f32)],
        scratch_shapes=[pltpu.VMEM((1, D), f32)], compiler_params=_cp("arbitrary"))(dqn, dkn, dv, z, z, qg, kg)


GB = GATE_OFF // D


def _merge_fwd(name, ya, yb, yc, z, bg):
    T = z.shape[0]

    def body(ya_ref, yb_ref, yc_ref, g0_ref, g1_ref, g2_ref, bg_ref, o_ref):
        acc = _sigmoid(g0_ref[...] + bg_ref[0:1, :]) * ya_ref[...]
        acc = acc + _sigmoid(g1_ref[...] + bg_ref[1:2, :]) * yb_ref[...]
        acc = acc + _sigmoid(g2_ref[...] + bg_ref[2:3, :]) * yc_ref[...]
        o_ref[...] = acc.astype(bf16)

    return pl.pallas_call(
        body, name=name, grid=(T // TT,),
        in_specs=[_row(), _row(), _row(), _row(GB), _row(GB + 1), _row(GB + 2), _vec(3)],
        out_specs=_row(), out_shape=S((T, D), bf16), compiler_params=_cp("parallel"))(ya, yb, yc, z, z, z, bg)


def _merge_bwd(name, dm, ya, yb, yc, z, bg):
    T = z.shape[0]

    def body(dm_ref, ya_ref, yb_ref, yc_ref, g0_ref, g1_ref, g2_ref, bg_ref, dya_ref, dyb_ref, dyc_ref, dz_ref, db_ref):
        i = pl.program_id(0)
        dm_v = dm_ref[...]
        dbs = []
        for n, (y_ref, g_ref, dy_ref) in enumerate(((ya_ref, g0_ref, dya_ref), (yb_ref, g1_ref, dyb_ref),
                                                    (yc_ref, g2_ref, dyc_ref))):
            gate = _sigmoid(g_ref[...] + bg_ref[n:n + 1, :])
            dy_ref[...] = (dm_v * gate).astype(bf16)
            dl = dm_v * y_ref[...] * gate * (1.0 - gate)
            dz_ref[:, n * D:(n + 1) * D] = dl.astype(bf16)
            dbs.append(jnp.sum(dl, axis=0, keepdims=True))
        _acc_store(i, db_ref, jnp.concatenate(dbs, axis=0))

    return pl.pallas_call(
        body, name=name, grid=(T // TT,),
        in_specs=[_row(), _row(), _row(), _row(), _row(GB), _row(GB + 1), _row(GB + 2), _vec(3)],
        out_specs=[_row(), _row(), _row(), _row(0, 3 * D), _vec(3)],
        out_shape=[S((T, D), bf16), S((T, D), bf16), S((T, D), bf16), S((T, 3 * D), bf16), S((3, D), f32)],
        compiler_params=_cp("arbitrary"))(dm, ya, yb, yc, z, z, z, bg)


SMALL_NAMES = ("ffn1_norm", "mix_norm", "b_forget", "b_gate", "conv_w", "sgu_ln_g", "sgu_ln_b", "sgu_w", "sgu_b",
               "q_norm_g", "k_norm_g", "ffn2_norm")


def _small_params(p):
    out = {n: p[n].reshape(1, D) for n in ("ffn1_norm", "mix_norm", "ffn2_norm", "sgu_ln_g", "sgu_ln_b", "q_norm_g", "k_norm_g")}
    out["b_forget"] = jnp.pad(p["b_forget"].reshape(1, NH), ((0, 0), (0, 128 - NH)))
    out["b_gate"] = p["b_gate"]
    out["conv_w"] = p["conv_w"]
    out["sgu_w"] = p["sgu_w"]
    out["bmap"] = jnp.repeat(p["sgu_b"].T, HD, axis=1)
    return out


def _small_grads_natural(sg):
    out = {n: sg[n].reshape(D) for n in ("ffn1_norm", "mix_norm", "ffn2_norm", "sgu_ln_g", "sgu_ln_b")}
    out["q_norm_g"] = sg["q_norm_g"].reshape(NH, HD)
    out["k_norm_g"] = sg["k_norm_g"].reshape(NH, HD)
    out["b_forget"] = sg["b_forget"][0, :NH]
    out["b_gate"] = sg["b_gate"]
    out["conv_w"] = sg["conv_w"]
    out["sgu_w"] = sg["sgu_w"]
    out["sgu_b"] = sg["sgu_b"]
    return out


SQ_TM = 1024


def _sq_fwd(name, a, wsq, l, n, res=None):
    T = a.shape[0]
    tm = _tile(T, SQ_TM)
    return _mm(name, a, wsq, grid=(T // tm, 1, 1), a_spec=_bs((tm, D), lambda i, j, k: (i, 0)),
               b_spec=_bs((None, None, D, D), lambda i, j, k: (l, n, 0, 0)),
               out_shape=S((T, D), f32), out_spec=_bs((tm, D), lambda i, j, k: (i, 0)), dims=NN, acc_shape=None,
               res=res, res_spec=_bs((tm, D), lambda i, j, k: (i, 0)))


def _sq_bwd_in(name, dy, wsq, l, n):
    T = dy.shape[0]
    tm = _tile(T, SQ_TM)
    return _mm(name, dy, wsq, grid=(T // tm, 1, 1), a_spec=_bs((tm, D), lambda i, j, k: (i, 0)),
               b_spec=_bs((None, None, D, D), lambda i, j, k: (l, n, 0, 0)),
               out_shape=S((T, D), f32), out_spec=_bs((tm, D), lambda i, j, k: (i, 0)), dims=NT, acc_shape=None)


def _sq_bwd_w(name, a, dy, gbuf, l):
    T = a.shape[0]
    return _mm(name, a, dy, grid=(NDEV // 2, 1, 1), a_spec=_bs((T, 256), lambda i, j, k: (0, i)),
               b_spec=_bs((T, D), lambda i, j, k: (0, 0)), out_shape=S(gbuf.shape, bf16),
               out_spec=_bs((2, None, None, 128, D), lambda i, j, k: (0, i, l, 0, 0)),
               dims=TN, acc_shape=None, alias=gbuf, split_rows=128)


def _ffn_fwd(tag, x, g, wgu, wd, l):
    T = x.shape[0]
    h = _rms_fwd(tag + "_rms", x, g)
    gu, a = _swiglu_fwd(tag + "_gu", h, wgu, l)
    tm = _tile(T, 1024)
    xo = _mm(tag + "_down", a, wd, grid=(T // tm, 1, 4), a_spec=_bs((None, tm, GU), lambda i, j, k: (k, i, 0)),
             b_spec=_bs((None, None, GU, D), lambda i, j, k: (l, k, 0, 0)), out_shape=S((T, D), f32),
             out_spec=_bs((tm, D), lambda i, j, k: (i, 0)), dims=NN, acc_shape=(tm, D), res=x,
             res_spec=_bs((tm, D), lambda i, j, k: (i, 0)), alpha=0.5)
    return xo, (h, gu, a)


def _ffn_bwd(tag, dxo, x, g, wgu, wd, l, saved, g_gu, g_d):
    h, gu, a = saved
    T = x.shape[0]
    g_d = _mm(tag + "_dwd", a, dxo, grid=(4, 1, 1), a_spec=_bs((None, T, GU), lambda i, j, k: (i, 0, 0)),
              b_spec=_bs((T, D), lambda i, j, k: (0, 0)), out_shape=S(g_d.shape, bf16),
              out_spec=_bs((2, None, None, GU // 2, D), lambda i, j, k: (0, i, l, 0, 0)), dims=TN, acc_shape=None,
              alpha=0.5, alias=g_d, split_rows=GU // 2)
    dgu = _swiglu_bwd(tag + "_dgu", dxo, wd, gu, l).reshape(NDEV, T, GU)
    dh = _mm(tag + "_dh", dgu, wgu, grid=(1, 1, NDEV), a_spec=_bs((None, T, GU), lambda i, j, k: (k, 0, 0)),
             b_spec=_bs((None, None, GU, D), lambda i, j, k: (l, k, 0, 0)), out_shape=S((T, D), f32),
             out_spec=_bs((T, D), lambda i, j, k: (0, 0)), dims=NN, acc_shape=(T, D))
    g_gu = _mm(tag + "_dwgu", dgu, h, grid=(NDEV, 1, 1), a_spec=_bs((None, T, GU), lambda i, j, k: (i, 0, 0)),
               b_spec=_bs((T, D), lambda i, j, k: (0, 0)), out_shape=S(g_gu.shape, bf16),
               out_spec=_bs((None, None, None, GU, D), lambda i, j, k: (i % 2, i // 2, l, 0, 0)), dims=TN,
               acc_shape=None, alias=g_gu)
    dx, dg = _rms_bwd(tag + "_drms", dh, x, g, dxo)
    return dx, dg, g_gu, g_d


def _mixer_fwd(tag, x, p, win, wsq, l):
    T = x.shape[0]
    h = _rms_fwd(tag + "_rms", x, p["mix_norm"])
    tn = 512
    z = _mm(tag + "_in", h, win, grid=(1, NZ // tn, 1), a_spec=_bs((T, D), lambda i, j, k: (0, 0)),
            b_spec=_bs((D, tn), lambda i, j, k: (0, j)), out_shape=S((T, NZ), f32),
            out_spec=_bs((T, tn), lambda i, j, k: (0, j)), dims=NN, acc_shape=None)
    ya_in = _conv_fwd(tag + "_conv", z, p["conv_w"])
    yb_in = _sgu_fwd(tag + "_sgu", z, p["sgu_ln_g"], p["sgu_ln_b"], p["sgu_w"], p["bmap"])
    qn, kn, vb, logf = _qk_fwd(tag + "_qk", z, p["q_norm_g"], p["k_norm_g"], p["b_forget"])
    ccol, crow = _cum_fwd(tag + "_cum", logf)
    crow3 = crow.reshape(NH, 1, T)
    o, lse, lser = _attn_fwd(tag + "_attn", qn, kn, vb, ccol, crow3)
    ya = _sq_fwd(tag + "_oconv", ya_in, wsq, l, 0)
    yb = _sq_fwd(tag + "_osgu", yb_in, wsq, l, 1)
    yc = _sq_fwd(tag + "_oattn", o, wsq, l, 2)
    merged = _merge_fwd(tag + "_merge", ya, yb, yc, z, p["b_gate"])
    xo = _sq_fwd(tag + "_o", merged, wsq, l, 3, res=x)
    return xo, (h, z, ya_in, yb_in, qn, kn, vb, ccol, crow3, o, lse, lser, ya, yb, yc, merged)


def _mixer_bwd(tag, dxo, x, p, win, wsq, l, saved, gsq):
    h, z, ya_in, yb_in, qn, kn, vb, ccol, crow3, o, lse, lser, ya, yb, yc, merged = saved
    T = x.shape[0]
    sg = {}
    dm = _sq_bwd_in(tag + "_dmerged", dxo, wsq, l, 3)
    gsq[3] = _sq_bwd_w(tag + "_dwo", merged, dxo, gsq[3], l)
    dya, dyb, dyc, dz_g, sg["b_gate"] = _merge_bwd(tag + "_dmerge", dm, ya, yb, yc, z, p["b_gate"])
    d_ya_in = _sq_bwd_in(tag + "_dconv_in", dya, wsq, l, 0)
    gsq[0] = _sq_bwd_w(tag + "_dwoc", ya_in, dya, gsq[0], l)
    d_yb_in = _sq_bwd_in(tag + "_dsgu_in", dyb, wsq, l, 1)
    gsq[1] = _sq_bwd_w(tag + "_dwos", yb_in, dyb, gsq[1], l)
    d_o = _sq_bwd_in(tag + "_dattn_in", dyc, wsq, l, 2)
    gsq[2] = _sq_bwd_w(tag + "_dwoa", o, dyc, gsq[2], l)
    dz_c, sg["conv_w"] = _conv_bwd(tag + "_dconv", d_ya_in, z, p["conv_w"])
    dz_s, sg["sgu_ln_g"], sg["sgu_ln_b"], sg["sgu_w"], db_t = _sgu_bwd(
        tag + "_dsgu", d_yb_in, z, p["sgu_ln_g"], p["sgu_ln_b"], p["sgu_w"], p["bmap"])
    sg["sgu_b"] = db_t.T
    dqn, dlr = _attn_dq(tag + "_dattn_q", qn, kn, vb, d_o, lse, ccol, crow3)
    dkn, dv, cs = _attn_dkv(tag + "_dattn_kv", qn, kn, vb, d_o, lser, dlr, ccol, crow3)
    dz_f, sg["b_forget"] = _forget_bwd(tag + "_dforget", cs, z, p["b_forget"])
    dz_q, sg["q_norm_g"], sg["k_norm_g"] = _qk_bwd(tag + "_dqk", dqn, dkn, dv, z, p["q_norm_g"], p["k_norm_g"])
    dz = [dz_c, dz_s, dz_q, dz_g, dz_f]
    dh = _dz_matmul(tag + "_dh", dz, win, False)
    dwin = _dz_matmul(tag + "_dwin", dz, h, True)
    dx, sg["mix_norm"] = _rms_bwd(tag + "_drms", dh, x, p["mix_norm"], dxo)
    return dx, sg, dwin, gsq


ANY = pl.BlockSpec(memory_space=pl.ANY)
HBM = pl.BlockSpec(memory_space=pltpu.HBM)
SEM = pl.BlockSpec(memory_space=pltpu.SEMAPHORE)
EFFECT = pltpu.SideEffectType.DATAFLOW_SIDE_EFFECTING


def _place():
    return lax.axis_index("x"), lax.axis_index("y"), lax.axis_index("c")


NEAR = 4


def _others(x, y, c):
    return [(x, y, 1 - c), (1 - x, y, c), (x, 1 - y, c), (1 - x, 1 - y, c)]


def _gather_start(name, groups, carry=None):
    sizes = [len(g) for g in groups]
    srcs = [s for g in groups for s, _ in g]
    lands = [b for g in groups for _, b in g]
    n, ng = len(srcs), len(groups)
    held = srcs + lands + ([] if carry is None else [carry])
    nh = len(held)

    def body(*refs):
        src_refs, land_refs = refs[:n], refs[n:2 * n]
        send, recv = refs[nh:nh + ng], refs[nh + ng:nh + 2 * ng]
        x, y, c = _place()
        me = 4 * x + 2 * y + c
        u = 0
        for g, size in enumerate(sizes):
            for i in range(size):
                for k, peer in enumerate(_others(x, y, c)):
                    pltpu.make_async_remote_copy(src_ref=src_refs[u], dst_ref=land_refs[u].at[me],
                                                 send_sem=send[g].at[i * NEAR + k], recv_sem=recv[g].at[i * NEAR + k],
                                                 device_id=peer, device_id_type=MESH).start()
                u += 1

    sems = [pltpu.SemaphoreType.DMA((size * NEAR,)) for size in sizes]
    out = pl.pallas_call(
        body, name=name, in_specs=[HBM] * nh, out_specs=[SEM] * (2 * ng) + [HBM] * nh,
        out_shape=sems + sems + [pltpu.HBM(a.shape, a.dtype) for a in held],
        input_output_aliases={i: 2 * ng + i for i in range(nh)},
        compiler_params=pltpu.CompilerParams(has_side_effects=EFFECT),
    )(*[pltpu.with_memory_space_constraint(a, pltpu.HBM) for a in held])
    res, pos = [], 0
    for g, size in enumerate(sizes):
        res.append((out[g], out[ng + g], out[2 * ng + pos:2 * ng + pos + size], out[2 * ng + n + pos:2 * ng + n + pos + size]))
        pos += size
    return res if carry is None else (res, out[2 * ng + 2 * n])


def _gather_wait(name, started, after=None):
    send, recv, srcs, lands = started
    n = len(srcs)

    def body(*refs):
        src_refs, land_refs = refs[:n], refs[n:2 * n]
        send_ref, recv_ref = refs[2 * n], refs[2 * n + 1]
        x, y, c = _place()
        for i in range(n):
            for k, (px, py, pc) in enumerate(_others(x, y, c)):
                cp = pltpu.make_async_remote_copy(src_ref=src_refs[i], dst_ref=land_refs[i].at[4 * px + 2 * py + pc],
                                                  send_sem=send_ref.at[i * NEAR + k], recv_sem=recv_ref.at[i * NEAR + k],
                                                  device_id=(px, py, pc), device_id_type=MESH)
                cp.wait_send()
                cp.wait_recv()

    extra = [] if after is None else [after]
    out = pl.pallas_call(
        body, name=name, in_specs=[HBM] * (2 * n) + [SEM, SEM] + [ANY] * len(extra), out_specs=[HBM] * (2 * n),
        out_shape=[pltpu.HBM(a.shape, a.dtype) for a in list(srcs) + list(lands)],
        input_output_aliases={i: i for i in range(2 * n)},
        compiler_params=pltpu.CompilerParams(has_side_effects=EFFECT),
    )(*srcs, *lands, send, recv, *extra)
    return out[n:]


def _gather_forward(name, lands):
    n = len(lands)

    def body(*refs):
        have, full = refs[:n], refs[n:2 * n]
        send, recv = refs[2 * n], refs[2 * n + 1]
        x, y, c = _place()
        chips = [(1 - x, y), (x, 1 - y), (1 - x, 1 - y)]

        def copy(i, j, core):
            slot = 4 * chips[j][0] + 2 * chips[j][1] + core
            return pltpu.make_async_remote_copy(src_ref=have[i].at[slot], dst_ref=full[i].at[slot],
                                                send_sem=send.at[i * 3 + j], recv_sem=recv.at[i * 3 + j],
                                                device_id=(x, y, 1 - c), device_id_type=MESH)

        for i in range(n):
            for j in range(3):
                copy(i, j, c).start()
        for i in range(n):
            for j in range(3):
                copy(i, j, c).wait_send()
                copy(i, j, 1 - c).wait_recv()

    return pl.pallas_call(
        body, name=name, in_specs=[ANY] * n, out_specs=[ANY] * n, out_shape=[S(a.shape, a.dtype) for a in lands],
        input_output_aliases={i: i for i in range(n)},
        scratch_shapes=[pltpu.SemaphoreType.DMA((n * 3,)), pltpu.SemaphoreType.DMA((n * 3,))],
    )(*lands)


def _rs_pair(name, gs):
    n = len(gs)

    def body(*refs):
        ins, outs = refs[:n], refs[n:2 * n]
        send, recv = refs[2 * n:]
        x, y, c = _place()
        cps = [pltpu.make_async_remote_copy(src_ref=ins[u].at[1 - c], dst_ref=outs[u], send_sem=send.at[u],
                                            recv_sem=recv.at[u], device_id=(x, y, 1 - c), device_id_type=MESH)
               for u in range(n)]
        for cp in cps:
            cp.start()
        for cp in cps:
            cp.wait()

    return pl.pallas_call(
        body, name=name, in_specs=[ANY] * n, out_specs=[ANY] * n, out_shape=[S(g.shape[1:], g.dtype) for g in gs],
        scratch_shapes=[pltpu.SemaphoreType.DMA((n,)), pltpu.SemaphoreType.DMA((n,))],
    )(*gs)


def _row_tile(r, c):
    if c > D and r % 128 == 0:
        return 128
    return 256 if r % 256 == 0 else (GU // 2 if r % (GU // 2) == 0 else r)


def _pair_sum(name, core, g, r1):
    _, nq, nl, r, c = g.shape
    tr = _row_tile(r, c)
    g4 = g.reshape(2, nq * nl, r, c)
    r3 = r1.reshape(nq * nl, r, c)

    def body(core_ref, g_ref, r_ref, o_ref):
        o_ref[...] = (g_ref[...].astype(f32) + r_ref[...].astype(f32)).astype(bf16)

    out = pl.pallas_call(
        body, name=name,
        grid_spec=pltpu.PrefetchScalarGridSpec(
            num_scalar_prefetch=1, grid=(nq * nl, r // tr),
            in_specs=[_bs((None, None, tr, c), lambda b, i, cr: (cr[0], b, i, 0)), _bs((None, tr, c), lambda b, i, cr: (b, i, 0))],
            out_specs=_bs((None, tr, c), lambda b, i, cr: (b, i, 0))),
        out_shape=S((nq * nl, r, c), bf16), compiler_params=_cp("parallel", "parallel"))(core, g4, r3)
    return out.reshape(nq, nl, r, c)


def _scatter_copies(ins, outs, send, recv):
    x, y, c = _place()
    chips = [(1 - x, y), (x, 1 - y), (1 - x, 1 - y)]
    return [pltpu.make_async_remote_copy(src_ref=ins[u].at[2 * chip[0] + chip[1]], dst_ref=outs[u].at[k],
                                         send_sem=send.at[u * 3 + k], recv_sem=recv.at[u * 3 + k],
                                         device_id=(*chip, c), device_id_type=MESH)
            for u in range(len(ins)) for k, chip in enumerate(chips)]


def _scatter_start(name, ss, carry):
    n = len(ss)
    lands = [lax.empty((3,) + s.shape[1:], s.dtype) for s in ss]
    held = list(ss) + lands + [carry]

    def body(*refs):
        for cp in _scatter_copies(refs[:n], refs[n:2 * n], refs[2 * n + 1], refs[2 * n + 2]):
            cp.start()

    sems = [pltpu.SemaphoreType.DMA((n * 3,))] * 2
    out = pl.pallas_call(
        body, name=name, in_specs=[HBM] * len(held), out_specs=[SEM, SEM] + [HBM] * len(held),
        out_shape=sems + [pltpu.HBM(a.shape, a.dtype) for a in held],
        input_output_aliases={i: 2 + i for i in range(len(held))},
        compiler_params=pltpu.CompilerParams(has_side_effects=EFFECT),
    )(*[pltpu.with_memory_space_constraint(a, pltpu.HBM) for a in held])
    return (out[0], out[1], out[2:2 + n], out[2 + n:2 + 2 * n]), out[2 + 2 * n]


def _scatter_wait(name, started, after):
    send, recv, srcs, lands = started
    n = len(srcs)

    def body(*refs):
        for cp in _scatter_copies(refs[:n], refs[n:2 * n], refs[2 * n], refs[2 * n + 1]):
            cp.wait_send()
            cp.wait_recv()

    out = pl.pallas_call(
        body, name=name, in_specs=[HBM] * (2 * n) + [SEM, SEM, ANY], out_specs=[HBM] * (2 * n),
        out_shape=[pltpu.HBM(a.shape, a.dtype) for a in list(srcs) + list(lands)],
        input_output_aliases={i: i for i in range(2 * n)},
        compiler_params=pltpu.CompilerParams(has_side_effects=EFFECT),
    )(*srcs, *lands, send, recv, after)
    return out[:n], out[n:]


def _sum_blocks(name, blocks):
    def body(b_ref, o_ref):
        acc = b_ref[0]
        for d in range(1, NDEV):
            acc = acc + b_ref[d]
        o_ref[...] = acc

    return pl.pallas_call(body, name=name, out_shape=S(blocks.shape[1:], f32),
                          compiler_params=pltpu.CompilerParams(vmem_limit_bytes=VMEM_LIMIT))(blocks)


def _adam_math(w, g, m, v):
    m = ADAM_B1 * m + (1.0 - ADAM_B1) * g
    v = ADAM_B2 * v + (1.0 - ADAM_B2) * (g * g)
    m_hat = m / (1.0 - ADAM_B1 ** ADAM_STEP)
    v_hat = v / (1.0 - ADAM_B2 ** ADAM_STEP)
    delta = -ADAM_LR * (m_hat / (jnp.sqrt(v_hat) + ADAM_EPS) + ADAM_WD * w)
    return delta, m, v


def _adamw(name, w, m, v, parts):
    _, r, c = w.shape
    tr = _row_tile(r, c)

    def body(w_ref, m_ref, v_ref, *refs):
        sets, (g_ref, d_ref, mo_ref, vo_ref) = (refs[0:4], refs[4:8]), refs[8:]
        for l in range(2):
            @pl.when(pl.program_id(0) == l)
            def _():
                s_ref, r0_ref, r1_ref, r2_ref = sets[l]
                g = ((s_ref[...].astype(f32) + r0_ref[...].astype(f32)) + r1_ref[...].astype(f32)) + r2_ref[...].astype(f32)
                g_ref[...] = g
                d_ref[...], mo_ref[...], vo_ref[...] = _adam_math(w_ref[...], g, m_ref[...], v_ref[...])

    blk = _bs((None, tr, c), lambda l, i: (l, i, 0))
    operands, specs = [], []
    for n in range(2):
        row = (lambda l, i: i * (1 - l)) if n == 0 else (lambda l, i: i * l)
        s_mine, r2 = parts[n]
        operands += [s_mine, r2, r2, r2]
        specs.append(_bs((tr, c), functools.partial(lambda l, i, row: (row(l, i), 0), row=row)))
        specs += [_bs((None, None, tr, c), functools.partial(lambda l, i, k, row: (k, 0, row(l, i), 0), k=k, row=row))
                  for k in range(3)]
    return pl.pallas_call(
        body, name=name, grid=(2, r // tr), in_specs=[blk, blk, blk] + specs,
        out_specs=[blk] * 4, out_shape=[S(w.shape, f32)] * 4, compiler_params=_cp("arbitrary", "arbitrary"),
    )(w, m, v, *operands)


def _adamw_small(name, w, g, m, v):
    def body(w_ref, g_ref, m_ref, v_ref, d_ref, mo_ref, vo_ref):
        d_ref[...], mo_ref[...], vo_ref[...] = _adam_math(w_ref[...], g_ref[...], m_ref[...], v_ref[...])

    return pl.pallas_call(body, name=name, out_shape=[S(w.shape, f32)] * 3,
                          compiler_params=pltpu.CompilerParams(vmem_limit_bytes=VMEM_LIMIT))(w, g, m, v)


WEIGHT_NAMES = ("ffn1_norm", "ffn1_w_gu", "ffn1_w_down", "mix_norm", "w_in", "b_forget", "b_gate", "conv_w", "sgu_ln_g",
                "sgu_ln_b", "sgu_w", "sgu_b", "q_norm_g", "k_norm_g", "w_out_conv", "w_out_sgu", "w_out_attn", "w_o",
                "ffn2_norm", "ffn2_w_gu", "ffn2_w_down")
BIG = {"ffn1_w_gu": "gu1", "ffn2_w_gu": "gu2", "ffn1_w_down": "d1", "ffn2_w_down": "d2", "w_in": "in",
       "w_out_conv": "oc", "w_out_sgu": "os", "w_out_attn": "oa", "w_o": "o"}
BIG_KEYS = ("gu1", "gu2", "d1", "d2", "in", "oc", "os", "oa", "o")
REPLICATED_SMALL = ("ffn1_norm", "mix_norm", "b_forget", "sgu_ln_g", "sgu_ln_b", "sgu_w", "sgu_b", "q_norm_g",
                    "k_norm_g", "ffn2_norm")
SHARDED_SMALL = ("b_gate", "conv_w")
TRANSPOSED = ("gu1", "gu2")


def _packed_rows(shape):
    size = 1
    for s_ in shape:
        size *= s_
    return size, -(-size // 1024) * 8


def _pack(arrays):
    pieces = []
    for a in arrays:
        size, rows = _packed_rows(a.shape)
        pieces.append(jnp.pad(a.reshape(-1).astype(f32), (0, rows * 128 - size)).reshape(rows, 128))
    return jnp.concatenate(pieces, axis=0)


def _unpack(packed, shapes):
    out, pos = [], 0
    for shp in shapes:
        size, rows = _packed_rows(shp)
        out.append(packed[pos:pos + rows].reshape(-1)[:size].reshape(shp))
        pos += rows
    return out


def _natural_runs(a, b):
    runs = []
    while a < b:
        d = a // INB
        e = min(b, (d + 1) * INB)
        runs.append((d, a - d * INB, e - d * INB))
        a = e
    return runs


def _win_kernel_layout(wg):
    runs = _natural_runs(0, GATE_OFF) + _natural_runs(GATE_OFF + NH, NIN) + _natural_runs(GATE_OFF, GATE_OFF + NH)
    return jnp.concatenate([wg[d, :, a:b] for d, a, b in runs] + [jnp.zeros((D, NZ - NIN), wg.dtype)], axis=1)


def _kernel_column(n):
    return n if n < GATE_OFF else (F_OFF + n - GATE_OFF if n < GATE_OFF + NH else n - NH)


def _win_device_block(dw, d):
    cuts = sorted({d * INB, (d + 1) * INB} | {c for c in (GATE_OFF, GATE_OFF + NH) if d * INB < c < (d + 1) * INB})
    parts = [dw[:, _kernel_column(a):_kernel_column(a) + (b - a)] for a, b in zip(cuts[:-1], cuts[1:])]
    return parts[0] if len(parts) == 1 else jnp.concatenate(parts, axis=1)


def kernel(x, ffn1_norm, ffn1_w_gu, ffn1_w_down, mix_norm, w_in, b_forget, b_gate, conv_w, sgu_ln_g, sgu_ln_b, sgu_w, sgu_b, q_norm_g, k_norm_g, w_out_conv, w_out_sgu, w_out_attn, w_o, ffn2_norm, ffn2_w_gu, ffn2_w_down, loss_target, m_ffn1_norm, m_ffn1_w_gu, m_ffn1_w_down, m_mix_norm, m_w_in, m_b_forget, m_b_gate, m_conv_w, m_sgu_ln_g, m_sgu_ln_b, m_sgu_w, m_sgu_b, m_q_norm_g, m_k_norm_g, m_w_out_conv, m_w_out_sgu, m_w_out_attn, m_w_o, m_ffn2_norm, m_ffn2_w_gu, m_ffn2_w_down, v_ffn1_norm, v_ffn1_w_gu, v_ffn1_w_down, v_mix_norm, v_w_in, v_b_forget, v_b_gate, v_conv_w, v_sgu_ln_g, v_sgu_ln_b, v_sgu_w, v_sgu_b, v_q_norm_g, v_k_norm_g, v_w_out_conv, v_w_out_sgu, v_w_out_attn, v_w_o, v_ffn2_norm, v_ffn2_w_gu, v_ffn2_w_down):
    w = dict(zip(WEIGHT_NAMES, (ffn1_norm, ffn1_w_gu, ffn1_w_down, mix_norm, w_in, b_forget, b_gate, conv_w, sgu_ln_g,
                                sgu_ln_b, sgu_w, sgu_b, q_norm_g, k_norm_g, w_out_conv, w_out_sgu, w_out_attn, w_o,
                                ffn2_norm, ffn2_w_gu, ffn2_w_down)))
    mom = dict(zip(WEIGHT_NAMES, (m_ffn1_norm, m_ffn1_w_gu, m_ffn1_w_down, m_mix_norm, m_w_in, m_b_forget, m_b_gate,
                                  m_conv_w, m_sgu_ln_g, m_sgu_ln_b, m_sgu_w, m_sgu_b, m_q_norm_g, m_k_norm_g,
                                  m_w_out_conv, m_w_out_sgu, m_w_out_attn, m_w_o, m_ffn2_norm, m_ffn2_w_gu,
                                  m_ffn2_w_down)))
    var = dict(zip(WEIGHT_NAMES, (v_ffn1_norm, v_ffn1_w_gu, v_ffn1_w_down, v_mix_norm, v_w_in, v_b_forget, v_b_gate,
                                  v_conv_w, v_sgu_ln_g, v_sgu_ln_b, v_sgu_w, v_sgu_b, v_q_norm_g, v_k_norm_g,
                                  v_w_out_conv, v_w_out_sgu, v_w_out_attn, v_w_o, v_ffn2_norm, v_ffn2_w_gu,
                                  v_ffn2_w_down)))
    px, py, pc = _place()
    dev = 4 * px + 2 * py + pc
    chip = 2 * px + py

    big_names = [n for n in WEIGHT_NAMES if n in BIG]
    key_name = {BIG[n]: n for n in big_names}
    group_keys = (("gu1", "d1"), ("in", "oc", "os", "oa", "o", "small"), ("gu2", "d2"))

    def source(key, l):
        if key == "small":
            return jnp.concatenate([w["b_gate"][l], w["conv_w"][l], jnp.zeros((2, 128), f32)], axis=0)
        block = w[key_name[key]][l]
        return (block.T if key in TRANSPOSED else block).astype(bf16)

    def landing(src):
        return lax.dynamic_update_slice(lax.empty((NDEV,) + src.shape, src.dtype), src[None], (dev, 0, 0))

    groups = [[(s, landing(s)) for s in (source(k, l) for k in keys)] for l in range(2) for keys in group_keys]
    started = _gather_start("gather_start", groups)

    def weights(l, part, after):
        got = _gather_wait(f"gather_wait_{l}_{part}", started[3 * l + part], after)
        return dict(zip(group_keys[part], _gather_forward(f"gather_forward_{l}_{part}", got)))

    xl = x[0]
    saved, small, wts = [], [], []
    for l in range(2):
        ga = weights(l, 0, xl if l else None)
        wt = {"gu1": ga["gu1"][None], "d1": ga["d1"].reshape(1, 4, GU, D)}
        x1, s1 = _ffn_fwd("ffn1", xl, w["ffn1_norm"][l].reshape(1, D), wt["gu1"], wt["d1"], 0)
        gb = weights(l, 1, x1)
        p = {n: w[n][l] for n in REPLICATED_SMALL}
        p["b_gate"] = jnp.transpose(gb["small"][:, 0:3, :], (1, 0, 2)).reshape(3, D)
        p["conv_w"] = jnp.transpose(gb["small"][:, 3:6, :], (1, 0, 2)).reshape(3, D)
        p = _small_params(p)
        wt["win"] = _win_kernel_layout(gb["in"])
        wt["sq"] = jnp.stack([gb[k].reshape(D, D) for k in ("oc", "os", "oa", "o")])[None]
        x2, sm = _mixer_fwd("mix", x1, p, wt["win"], wt["sq"], 0)
        gc = weights(l, 2, x2)
        wt.update({"gu2": gc["gu2"][None], "d2": gc["d2"].reshape(1, 4, GU, D)})
        x3, s2 = _ffn_fwd("ffn2", x2, p["ffn2_norm"], wt["gu2"], wt["d2"], 0)
        saved.append((xl, x1, x2, s1, sm, s2))
        small.append(p)
        wts.append(wt)
        xl = x3
    loss_row, dx = _loss("loss", xl, loss_target[0])

    core = pc.reshape(1).astype(jnp.int32)
    buf = lambda r, c: lax.empty((2, 4, 1, r, c), bf16)
    flights = {}

    def scatter(l, part, bufs, dx):
        r1 = _rs_pair(f"rs_pair_{l}_{part}", bufs)
        ss = [_pair_sum(f"pair_sum_{l}_{k}", core, g, r) for k, g, r in zip(group_keys[part], bufs, r1)]
        flights[l, part], dx = _scatter_start(f"scatter_start_{l}_{part}", ss, dx)
        return dx

    sgrads = [None, None]
    for l in (1, 0):
        p, wt = small[l], wts[l]
        x0, x1, x2, s1, sm, s2 = saved[l]
        dx, dn2, g_gu2, g_d2 = _ffn_bwd("ffn2", dx, x2, p["ffn2_norm"], wt["gu2"], wt["d2"], 0, s2, buf(GU, D), buf(GU // 2, D))
        dx = scatter(l, 2, [g_gu2, g_d2], dx)
        dx, sg, dwin, gsq = _mixer_bwd("mix", dx, x1, p, wt["win"], wt["sq"], 0, sm, [buf(128, D) for _ in range(4)])
        g_in = jnp.stack([jnp.stack([_win_device_block(dwin, 2 * q + c)[None] for q in range(4)]) for c in range(2)])
        dx = scatter(l, 1, [g_in] + gsq, dx)
        dx, dn1, g_gu1, g_d1 = _ffn_bwd("ffn1", dx, x0, p["ffn1_norm"], wt["gu1"], wt["d1"], 0, s1, buf(GU, D), buf(GU // 2, D))
        dx = scatter(l, 0, [g_gu1, g_d1], dx)
        sg["ffn1_norm"] = dn1
        sg["ffn2_norm"] = dn2
        sgrads[l] = sg

    nat = [_small_grads_natural(sgrads[l]) for l in range(2)]
    order = REPLICATED_SMALL + SHARDED_SMALL
    part = _pack([jnp.stack([nat[0][n], nat[1][n]]) for n in order] + [loss_row[0, 0:1]])
    small_flight, dx = _gather_start("small_start", [[(part, landing(part))]], carry=dx)

    grads, delta, new_m, new_v = {}, {}, {}, {}
    after = dx
    for part in (2, 1, 0):
        sets = []
        for l in (1, 0):
            s_all, r2_all = _scatter_wait(f"scatter_wait_{l}_{part}", flights[l, part], after)
            sets.append([(lax.dynamic_index_in_dim(s, chip, 0, keepdims=False)[0], r) for s, r in zip(s_all, r2_all)])
        for i, k in enumerate(k for k in group_keys[part] if k != "small"):
            n = key_name[k]
            view = (lambda a: jnp.swapaxes(a, 1, 2)) if k in TRANSPOSED else (lambda a: a)
            outs = _adamw("adamw_" + k, view(w[n]), view(mom[n]), view(var[n]), [sets[1][i], sets[0][i]])
            grads[n], delta[n], new_m[n], new_v[n] = [view(o) for o in outs]
            after = outs[1]

    blocks = _gather_forward("small_forward", _gather_wait("small_wait", small_flight[0], after))
    total = _sum_blocks("small_sum", blocks[0])
    full_shapes = [(2,) + tuple(nat[0][n].shape) for n in order] + [(1,)]
    summed = dict(zip(order + ("loss",), _unpack(total, full_shapes)))
    for n in REPLICATED_SMALL:
        grads[n] = summed[n]
    for n in SHARDED_SMALL:
        grads[n] = lax.dynamic_slice_in_dim(summed[n], dev * 128, 128, axis=2)
    wp = _pack([w[n] for n in order])
    gp = _pack([grads[n] for n in order])
    mp = _pack([mom[n] for n in order])
    vp = _pack([var[n] for n in order])
    dpk, mpk, vpk = _adamw_small("adamw_small", wp, gp, mp, vp)
    local_shapes = [tuple(w[n].shape) for n in order]
    for dst, packed in ((delta, dpk), (new_m, mpk), (new_v, vpk)):
        dst.update(zip(order, _unpack(packed, local_shapes)))

    loss = summed["loss"][0]
    return (loss, dx[None], *[grads[n] for n in WEIGHT_NAMES], *[delta[n] for n in WEIGHT_NAMES],
            *[new_m[n] for n in WEIGHT_NAMES], *[new_v[n] for n in WEIGHT_NAMES])
```

```python
import functools

import jax
import jax.numpy as jnp
from jax import lax
from jax.experimental import pallas as pl
from jax.experimental.pallas import tpu as pltpu

f32 = jnp.float32
bf16 = jnp.bfloat16
S = jax.ShapeDtypeStruct
MESH = pl.DeviceIdType.MESH

D = 1024
NH = 8
HD = 128
NDEV = 8
GU = 704
NIN = 11272
INB = 1409
GATE_OFF = 8192
F_OFF = 11264
NZ = 11776
RMS_EPS = 1e-6
LN_EPS = 1e-5
ATT_SCALE = HD ** -0.5
NEG = -1e30
INV_SQRT2 = 0.7071067811865476
INV_SQRT2PI = 0.3989422804014327

ADAM_LR = 0.001
ADAM_B1 = 0.9
ADAM_B2 = 0.999
ADAM_EPS = 1e-08
ADAM_WD = 0.01
ADAM_STEP = 10

TT = 512
VMEM_LIMIT = 56 * 1024 * 1024


def _cp(*sem):
    return pltpu.CompilerParams(dimension_semantics=sem, vmem_limit_bytes=VMEM_LIMIT)


def _bs(shape, fn):
    return pl.BlockSpec(shape, fn)


NN = (((1,), (0,)), ((), ()))
NT = (((1,), (1,)), ((), ()))
TN = (((0,), (0,)), ((), ()))


def _mm(name, a, b, *, grid, a_spec, b_spec, out_shape, out_spec, dims, acc_shape, res=None, res_spec=None,
        alpha=1.0, alias=None, split_rows=None):
    nk = grid[2]

    def body(*refs):
        a_ref, b_ref = refs[0], refs[1]
        pos = 2
        res_ref = None
        if res is not None:
            res_ref = refs[pos]
            pos += 1
        if alias is not None:
            pos += 1
        o_ref = refs[pos]
        acc_ref = refs[pos + 1] if nk > 1 else None
        part = lax.dot_general(a_ref[...].astype(bf16), b_ref[...].astype(bf16), dims, preferred_element_type=f32)

        def finish(acc):
            if alpha != 1.0:
                acc = alpha * acc
            if res_ref is not None:
                acc = res_ref[...] + acc
            if split_rows is None:
                o_ref[...] = acc.astype(o_ref.dtype)
            else:
                o_ref[0] = acc[:split_rows].astype(o_ref.dtype)
                o_ref[1] = acc[split_rows:].astype(o_ref.dtype)

        if nk == 1:
            finish(part)
        else:
            k = pl.program_id(2)

            @pl.when(k == 0)
            def _():
                acc_ref[...] = part

            @pl.when(k > 0)
            def _():
                acc_ref[...] += part

            @pl.when(k == nk - 1)
            def _():
                finish(acc_ref[...])

    operands = [a, b]
    in_specs = [a_spec, b_spec]
    if res is not None:
        operands.append(res)
        in_specs.append(res_spec)
    aliases = {}
    if alias is not None:
        aliases = {len(operands): 0}
        operands.append(alias)
        in_specs.append(pl.BlockSpec(memory_space=pl.ANY))
    return pl.pallas_call(
        body, name=name, grid=grid, in_specs=in_specs, out_specs=out_spec, out_shape=out_shape,
        scratch_shapes=[pltpu.VMEM(acc_shape, f32)] if nk > 1 else [],
        input_output_aliases=aliases,
        compiler_params=_cp("parallel", "parallel", "arbitrary"),
    )(*operands)


def _tile(n, t):
    return t if n % t == 0 and n >= t else n


DZ_TILE = 512


def _dz_matmul(name, pieces, other, weight_grad):
    T = pieces[0].shape[0]
    counts = [p.shape[1] // DZ_TILE for p in pieces]
    starts = [sum(counts[:i]) for i in range(len(counts))]
    steps = sum(counts)
    npc = len(pieces)

    def body(*refs):
        prefs, o_ref, rest = refs[:npc], refs[npc], refs[npc + 1:]
        k = pl.program_id(0)
        if not weight_grad:
            out_ref, acc_ref = rest

            @pl.when(k == 0)
            def _():
                acc_ref[...] = jnp.zeros_like(acc_ref)

        for p_ref, s, c in zip(prefs, starts, counts):
            @pl.when((k >= s) & (k < s + c))
            def _():
                if weight_grad:
                    rest[0][...] = lax.dot_general(o_ref[...], p_ref[...], TN, preferred_element_type=f32).astype(bf16)
                else:
                    acc_ref[...] += lax.dot_general(p_ref[...], o_ref[...], NT, preferred_element_type=f32)

        if not weight_grad:
            @pl.when(k == steps - 1)
            def _():
                out_ref[...] = acc_ref[...]

    piece_specs = [_bs((T, DZ_TILE), functools.partial(lambda k, s, c: (0, jnp.clip(k - s, 0, c - 1)), s=s, c=c))
                   for s, c in zip(starts, counts)]
    if weight_grad:
        other_spec, out_spec, out_shape, scratch = _bs((T, D), lambda k: (0, 0)), _bs((D, DZ_TILE), lambda k: (0, k)), S((D, NZ), bf16), []
    else:
        other_spec, out_spec, out_shape = _bs((D, DZ_TILE), lambda k: (0, k)), _bs((T, D), lambda k: (0, 0)), S((T, D), f32)
        scratch = [pltpu.VMEM((T, D), f32)]
    return pl.pallas_call(body, name=name, grid=(steps,), in_specs=piece_specs + [other_spec], out_specs=out_spec,
                          out_shape=out_shape, scratch_shapes=scratch, compiler_params=_cp("arbitrary"))(*pieces, other)


def _row(cb=0, w=D):
    return _bs((TT, w), lambda i: (i, cb))


def _vec(rows=1, w=D):
    return _bs((rows, w), lambda i: (0, 0))


def _acc_store(i, ref, val):
    @pl.when(i == 0)
    def _():
        ref[...] = val

    @pl.when(i > 0)
    def _():
        ref[...] += val


def _rms_fwd(name, x, g):
    T = x.shape[0]

    def body(x_ref, g_ref, o_ref):
        xv = x_ref[...]
        r = lax.rsqrt(jnp.mean(xv * xv, axis=-1, keepdims=True) + RMS_EPS)
        o_ref[...] = (xv * r * g_ref[...]).astype(bf16)

    return pl.pallas_call(body, name=name, grid=(T // TT,), in_specs=[_row(), _vec()], out_specs=_row(),
                          out_shape=S((T, D), bf16), compiler_params=_cp("parallel"))(x, g)


def _rms_bwd(name, dh, x, g, dres):
    T = x.shape[0]

    def body(dh_ref, x_ref, g_ref, dres_ref, dx_ref, dg_ref):
        i = pl.program_id(0)
        xv = x_ref[...]
        r = lax.rsqrt(jnp.mean(xv * xv, axis=-1, keepdims=True) + RMS_EPS)
        xhat = xv * r
        dh_v = dh_ref[...]
        dyg = dh_v * g_ref[...]
        m = jnp.mean(dyg * xhat, axis=-1, keepdims=True)
        dx_ref[...] = dres_ref[...] + r * (dyg - xhat * m)
        _acc_store(i, dg_ref, jnp.sum(dh_v * xhat, axis=0, keepdims=True))

    return pl.pallas_call(body, name=name, grid=(T // TT,), in_specs=[_row(), _row(), _vec(), _row()],
                          out_specs=[_row(), _vec()], out_shape=[S((T, D), f32), S((1, D), f32)],
                          compiler_params=_cp("arbitrary"))(dh, x, g, dres)


def _sigmoid(x):
    return 1.0 / (1.0 + jnp.exp(-x))


def _swiglu_fwd(name, h, wgu, l):
    T = h.shape[0]

    def body(h_ref, wg_ref, wu_ref, gu_ref, a_ref):
        hv = h_ref[...]
        g = lax.dot_general(hv, wg_ref[...], NT, preferred_element_type=f32)
        u = lax.dot_general(hv, wu_ref[...], NT, preferred_element_type=f32)
        gu_ref[0] = g
        gu_ref[1] = u
        a_ref[...] = (g * _sigmoid(g) * u).astype(bf16)

    return pl.pallas_call(
        body, name=name, grid=(4,),
        in_specs=[_bs((T, D), lambda j: (0, 0)), _bs((None, None, GU, D), lambda j: (l, j, 0, 0)),
                  _bs((None, None, GU, D), lambda j: (l, j + 4, 0, 0))],
        out_specs=[_bs((2, None, T, GU), lambda j: (0, j, 0, 0)), _bs((None, T, GU), lambda j: (j, 0, 0))],
        out_shape=[S((2, 4, T, GU), f32), S((4, T, GU), bf16)], compiler_params=_cp("parallel"))(h, wgu, wgu)


def _swiglu_bwd(name, dxo, wd, gu, l):
    T = dxo.shape[0]
    tm = _tile(T, 1024)

    def body(dx_ref, wd_ref, g_ref, u_ref, o_ref):
        da = 0.5 * lax.dot_general(dx_ref[...].astype(bf16), wd_ref[...], NT, preferred_element_type=f32)
        g = g_ref[...]
        sg = _sigmoid(g)
        o_ref[0] = (da * u_ref[...] * (sg + g * sg * (1.0 - sg))).astype(bf16)
        o_ref[1] = (da * g * sg).astype(bf16)

    return pl.pallas_call(
        body, name=name, grid=(T // tm, 4),
        in_specs=[_bs((tm, D), lambda i, j: (i, 0)), _bs((None, None, GU, D), lambda i, j: (l, j, 0, 0)),
                  _bs((None, None, tm, GU), lambda i, j: (0, j, i, 0)), _bs((None, None, tm, GU), lambda i, j: (1, j, i, 0))],
        out_specs=_bs((2, None, tm, GU), lambda i, j: (0, j, i, 0)), out_shape=S((2, 4, T, GU), bf16),
        compiler_params=_cp("parallel", "parallel"))(dxo, wd, gu, gu)


def _loss(name, y, tgt):
    T = y.shape[0]

    def body(y_ref, t_ref, l_ref, dy_ref):
        i = pl.program_id(0)
        e = y_ref[...] - t_ref[...]
        dy_ref[...] = e * (1.0 / D)
        s = 0.5 * jnp.sum(jnp.mean(e * e, axis=-1, keepdims=True))
        _acc_store(i, l_ref, jnp.broadcast_to(s, (1, 128)))

    return pl.pallas_call(body, name=name, grid=(T // TT,), in_specs=[_row(), _row()],
                          out_specs=[_vec(1, 128), _row()], out_shape=[S((1, 128), f32), S((T, D), f32)],
                          compiler_params=_cp("arbitrary"))(y, tgt)


def _prev8(T, cb):
    return _bs((8, D), lambda i: (jnp.maximum(i * (TT // 8) - 1, 0), cb))


def _next8(T, cb):
    return _bs((8, D), lambda i: (jnp.minimum((i + 1) * (TT // 8), T // 8 - 1), cb))


def _conv_taps(i, ac_ref, ax_ref, pc_ref, px_ref):
    ca = ac_ref[...] * ax_ref[...]
    keep = (i > 0).astype(f32)
    p1 = pc_ref[7:8, :] * px_ref[7:8, :] * keep
    p2 = pc_ref[6:7, :] * px_ref[6:7, :] * keep
    row = lax.broadcasted_iota(jnp.int32, ca.shape, 0)
    s1 = jnp.where(row == 0, p1, pltpu.roll(ca, 1, 0))
    s2 = jnp.where(row == 0, p2, jnp.where(row == 1, p1, pltpu.roll(ca, 2, 0)))
    return ca, s1, s2


def _conv_fwd(name, z, cw):
    T = z.shape[0]

    def body(ab_ref, ac_ref, ax_ref, pc_ref, px_ref, w_ref, o_ref):
        i = pl.program_id(0)
        ca, s1, s2 = _conv_taps(i, ac_ref, ax_ref, pc_ref, px_ref)
        cv = w_ref[0:1, :] * s2 + w_ref[1:2, :] * s1 + w_ref[2:3, :] * ca
        o_ref[...] = (ab_ref[...] * cv).astype(bf16)

    return pl.pallas_call(
        body, name=name, grid=(T // TT,),
        in_specs=[_row(0), _row(1), _row(2), _prev8(T, 1), _prev8(T, 2), _vec(3)],
        out_specs=_row(), out_shape=S((T, D), bf16), compiler_params=_cp("parallel"))(z, z, z, z, z, cw)


def _conv_bwd(name, dya, z, cw):
    T = z.shape[0]
    n = T // TT

    def body(dya_ref, ab_ref, ac_ref, ax_ref, pc_ref, px_ref, ndya_ref, nab_ref, w_ref, dz_ref, dw_ref):
        i = pl.program_id(0)
        ca, s1, s2 = _conv_taps(i, ac_ref, ax_ref, pc_ref, px_ref)
        w0, w1, w2 = w_ref[0:1, :], w_ref[1:2, :], w_ref[2:3, :]
        cv = w0 * s2 + w1 * s1 + w2 * ca
        dya_v = dya_ref[...]
        ab = ab_ref[...]
        dcv = dya_v * ab
        keep = (i < n - 1).astype(f32)
        n1 = ndya_ref[0:1, :] * nab_ref[0:1, :] * keep
        n2 = ndya_ref[1:2, :] * nab_ref[1:2, :] * keep
        row = lax.broadcasted_iota(jnp.int32, dcv.shape, 0)
        f1 = jnp.where(row == TT - 1, n1, pltpu.roll(dcv, TT - 1, 0))
        f2 = jnp.where(row == TT - 1, n2, jnp.where(row == TT - 2, n1, pltpu.roll(dcv, TT - 2, 0)))
        dca = w2 * dcv + w1 * f1 + w0 * f2
        dz_ref[:, 0:D] = (dya_v * cv).astype(bf16)
        dz_ref[:, D:2 * D] = (dca * ax_ref[...]).astype(bf16)
        dz_ref[:, 2 * D:3 * D] = (dca * ac_ref[...]).astype(bf16)
        dw = jnp.concatenate([jnp.sum(dcv * s2, axis=0, keepdims=True), jnp.sum(dcv * s1, axis=0, keepdims=True),
                              jnp.sum(dcv * ca, axis=0, keepdims=True)], axis=0)
        _acc_store(i, dw_ref, dw)

    return pl.pallas_call(
        body, name=name, grid=(n,),
        in_specs=[_row(), _row(0), _row(1), _row(2), _prev8(T, 1), _prev8(T, 2), _next8(T, 0), _next8(T, 0), _vec(3)],
        out_specs=[_row(0, 3 * D), _vec(3)], out_shape=[S((T, 3 * D), bf16), S((3, D), f32)],
        compiler_params=_cp("arbitrary"))(dya, z, z, z, z, z, dya, z, cw)


def _gelu(x):
    return 0.5 * x * (1.0 + lax.erf(x * INV_SQRT2))


def _gelu_grad(x):
    return 0.5 * (1.0 + lax.erf(x * INV_SQRT2)) + x * jnp.exp(-0.5 * x * x) * INV_SQRT2PI


def _ln_stats(vv):
    mu = jnp.mean(vv, axis=-1, keepdims=True)
    xc = vv - mu
    rstd = lax.rsqrt(jnp.mean(xc * xc, axis=-1, keepdims=True) + LN_EPS)
    return xc * rstd, rstd


def _tril_w(w_ref, g):
    r = lax.broadcasted_iota(jnp.int32, (HD, HD), 0)
    c = lax.broadcasted_iota(jnp.int32, (HD, HD), 1)
    return jnp.where(c <= r, w_ref[g], 0.0).astype(bf16)


def _sgu_fwd(name, z, ln_g, ln_b, w_s, bmap):
    T = z.shape[0]

    def body(su_ref, sv_ref, lg_ref, lb_ref, w_ref, bm_ref, o_ref, vn_ref):
        xhat, _ = _ln_stats(_gelu(sv_ref[...]))
        vn_ref[...] = (xhat * lg_ref[...] + lb_ref[...]).astype(bf16)
        for g in range(NH):
            w = _tril_w(w_ref, g)
            cs = slice(g * HD, (g + 1) * HD)
            for c in range(TT // HD):
                rs = slice(c * HD, (c + 1) * HD)
                s = jnp.dot(w, vn_ref[rs, cs], preferred_element_type=f32) + bm_ref[:, cs]
                o_ref[rs, cs] = (_gelu(su_ref[rs, cs]) * s).astype(bf16)

    return pl.pallas_call(
        body, name=name, grid=(T // TT,),
        in_specs=[_row(3), _row(4), _vec(), _vec(), _bs((NH, HD, HD), lambda i: (0, 0, 0)), _vec(HD)],
        out_specs=_row(), out_shape=S((T, D), bf16), scratch_shapes=[pltpu.VMEM((TT, D), bf16)],
        compiler_params=_cp("parallel"))(z, z, ln_g, ln_b, w_s, bmap)


def _sgu_bwd(name, dyb, z, ln_g, ln_b, w_s, bmap):
    T = z.shape[0]

    def body(dyb_ref, su_ref, sv_ref, lg_ref, lb_ref, w_ref, bm_ref, dz_ref, dlg_ref, dlb_ref, dw_ref, db_ref,
             vn_ref, du_ref, dvn_ref):
        i = pl.program_id(0)
        sv = sv_ref[...]
        xhat, rstd = _ln_stats(_gelu(sv))
        vn_ref[...] = (xhat * lg_ref[...] + lb_ref[...]).astype(bf16)
        r = lax.broadcasted_iota(jnp.int32, (HD, HD), 0)
        cc = lax.broadcasted_iota(jnp.int32, (HD, HD), 1)
        for g in range(NH):
            w = _tril_w(w_ref, g)
            cs = slice(g * HD, (g + 1) * HD)
            dw = jnp.zeros((HD, HD), f32)
            db = jnp.zeros((HD, 1), f32)
            for c in range(TT // HD):
                rs = slice(c * HD, (c + 1) * HD)
                vnb = vn_ref[rs, cs]
                s = jnp.dot(w, vnb, preferred_element_type=f32) + bm_ref[:, cs]
                dy = dyb_ref[rs, cs]
                du_ref[rs, cs] = dy * s
                ds = dy * _gelu(su_ref[rs, cs])
                ds16 = ds.astype(bf16)
                dvn_ref[rs, cs] = lax.dot_general(w, ds16, TN, preferred_element_type=f32)
                dw = dw + lax.dot_general(ds16, vnb, NT, preferred_element_type=f32)
                db = db + jnp.sum(ds, axis=1, keepdims=True)
            dw = jnp.where(cc <= r, dw, 0.0)

            @pl.when(i == 0)
            def _():
                dw_ref[g] = dw
                db_ref[:, g:g + 1] = db

            @pl.when(i > 0)
            def _():
                dw_ref[g] += dw
                db_ref[:, g:g + 1] += db

        dvn = dvn_ref[...]
        dxh = dvn * lg_ref[...]
        m1 = jnp.mean(dxh, axis=-1, keepdims=True)
        m2 = jnp.mean(dxh * xhat, axis=-1, keepdims=True)
        dvv = rstd * (dxh - m1 - xhat * m2)
        dz_ref[:, 0:D] = (du_ref[...] * _gelu_grad(su_ref[...])).astype(bf16)
        dz_ref[:, D:2 * D] = (dvv * _gelu_grad(sv)).astype(bf16)
        _acc_store(i, dlg_ref, jnp.sum(dvn * xhat, axis=0, keepdims=True))
        _acc_store(i, dlb_ref, jnp.sum(dvn, axis=0, keepdims=True))

    return pl.pallas_call(
        body, name=name, grid=(T // TT,),
        in_specs=[_row(), _row(3), _row(4), _vec(), _vec(), _bs((NH, HD, HD), lambda i: (0, 0, 0)), _vec(HD)],
        out_specs=[_row(0, 2 * D), _vec(), _vec(), _bs((NH, HD, HD), lambda i: (0, 0, 0)), _bs((HD, NH), lambda i: (0, 0))],
        out_shape=[S((T, 2 * D), bf16), S((1, D), f32), S((1, D), f32), S((NH, HD, HD), f32), S((HD, NH), f32)],
        scratch_shapes=[pltpu.VMEM((TT, D), bf16), pltpu.VMEM((TT, D), f32), pltpu.VMEM((TT, D), f32)],
        compiler_params=_cp("arbitrary"))(dyb, z, z, ln_g, ln_b, w_s, bmap)


def _qk_fwd(name, z, qg, kg, bf):
    T = z.shape[0]

    def body(q_ref, k_ref, v_ref, zf_ref, qg_ref, kg_ref, bf_ref, qn_ref, kn_ref, vb_ref, lf_ref):
        for h in range(NH):
            cs = slice(h * HD, (h + 1) * HD)
            for src, gain, dst in ((q_ref, qg_ref, qn_ref), (k_ref, kg_ref, kn_ref)):
                xv = src[:, cs]
                r = lax.rsqrt(jnp.mean(xv * xv, axis=-1, keepdims=True) + RMS_EPS)
                dst[:, cs] = (xv * r * gain[:, cs]).astype(bf16)
        vb_ref[...] = v_ref[...].astype(bf16)
        xf = zf_ref[...] + bf_ref[...]
        lf_ref[...] = jnp.minimum(xf, 0.0) - jnp.log1p(jnp.exp(-jnp.abs(xf)))

    return pl.pallas_call(
        body, name=name, grid=(T // TT,),
        in_specs=[_row(5), _row(6), _row(7), _bs((TT, 128), lambda i: (i, F_OFF // 128)), _vec(), _vec(), _vec(1, 128)],
        out_specs=[_row(), _row(), _row(), _bs((TT, 128), lambda i: (i, 0))],
        out_shape=[S((T, D), bf16), S((T, D), bf16), S((T, D), bf16), S((T, 128), f32)],
        compiler_params=_cp("parallel"))(z, z, z, z, qg, kg, bf)


def _cum_fwd(name, logf):
    T = logf.shape[0]

    def body(lf_ref, ccol_ref, crow_ref, c_ref):
        c = lf_ref[...]
        row = lax.broadcasted_iota(jnp.int32, c.shape, 0)
        s = 1
        while s < T:
            c = c + jnp.where(row >= s, pltpu.roll(c, s, 0), 0.0)
            s *= 2
        c_ref[...] = c
        crow_ref[...] = c.T[0:NH, :]
        for h in range(NH):
            ccol_ref[h] = jnp.broadcast_to(c_ref[:, h:h + 1], (T, 128))

    return pl.pallas_call(body, name=name, out_shape=[S((NH, T, 128), f32), S((NH, T), f32)],
                          scratch_shapes=[pltpu.VMEM((T, 128), f32)],
                          compiler_params=pltpu.CompilerParams(vmem_limit_bytes=VMEM_LIMIT))(logf)


ATT_TILE = 1024


def _fold(x, op=jnp.add):
    acc = x[:, 0:128]
    for t in range(1, x.shape[1] // 128):
        acc = op(acc, x[:, t * 128:(t + 1) * 128])
    return acc


def _to_row(col):
    return jnp.broadcast_to(col, (col.shape[0], 128)).T[0:1, :]


def _causal(t, keys_down=False):
    r = lax.broadcasted_iota(jnp.int32, (t, t), 0)
    c = lax.broadcasted_iota(jnp.int32, (t, t), 1)
    return r <= c if keys_down else c <= r


def _attn_fwd(name, qn, kn, vb, ccol, crow3):
    T = qn.shape[0]
    tq = _tile(T, ATT_TILE)
    nq = T // tq

    def body(q_ref, k_ref, v_ref, cc_ref, cr_ref, o_ref, lse_ref, lser_ref, s_ref):
        qi = pl.program_id(1)
        q = q_ref[...]
        cq = cc_ref[:, 0:1]

        def logits(off):
            s = lax.dot_general(q, k_ref[pl.ds(off, tq), :], NT, preferred_element_type=f32) * ATT_SCALE
            return s + cq - cr_ref[:, pl.ds(off, tq)]

        def below(j, mvec):
            off = pl.multiple_of(j * tq, tq)
            s = logits(off)
            s_ref[:, pl.ds(off, tq)] = s
            return jnp.maximum(mvec, _fold(s, jnp.maximum))

        mvec = lax.fori_loop(0, qi, below, jnp.full((tq, 128), NEG, f32))
        off = pl.multiple_of(qi * tq, tq)
        s = jnp.where(_causal(tq), logits(off), NEG)
        s_ref[:, pl.ds(off, tq)] = s
        m = jnp.max(jnp.maximum(mvec, _fold(s, jnp.maximum)), axis=1, keepdims=True)

        def weigh(j, carry):
            lvec, acc = carry
            off = pl.multiple_of(j * tq, tq)
            p = jnp.exp(s_ref[:, pl.ds(off, tq)] - m)
            acc = acc + jnp.dot(p.astype(bf16), v_ref[pl.ds(off, tq), :], preferred_element_type=f32)
            return lvec + _fold(p), acc

        lvec, acc = lax.fori_loop(0, qi + 1, weigh, (jnp.zeros((tq, 128), f32), jnp.zeros((tq, HD), f32)))
        l = jnp.sum(lvec, axis=1, keepdims=True)
        o_ref[...] = acc / l
        lse = m + jnp.log(l)
        lse_ref[...] = jnp.broadcast_to(lse, (tq, 128))
        lser_ref[...] = _to_row(lse)

    return pl.pallas_call(
        body, name=name, grid=(NH, nq),
        in_specs=[_bs((tq, HD), lambda h, i: (i, h)), _bs((T, HD), lambda h, i: (0, h)), _bs((T, HD), lambda h, i: (0, h)),
                  _bs((None, tq, 128), lambda h, i: (h, i, 0)), _bs((None, 1, T), lambda h, i: (h, 0, 0))],
        out_specs=[_bs((tq, HD), lambda h, i: (i, h)), _bs((None, tq, 128), lambda h, i: (h, i, 0)),
                   _bs((None, 1, tq), lambda h, i: (h, 0, i))],
        out_shape=[S((T, D), f32), S((NH, T, 128), f32), S((NH, 1, T), f32)],
        scratch_shapes=[pltpu.VMEM((tq, T), f32)],
        compiler_params=_cp("parallel", "parallel"))(qn, kn, vb, ccol, crow3)


def _attn_dq(name, qn, kn, vb, do, lse, ccol, crow3):
    T = qn.shape[0]
    tq = _tile(T, ATT_TILE)
    nq = T // tq

    def body(q_ref, k_ref, v_ref, do_ref, lse_ref, cc_ref, cr_ref, dq_ref, dlr_ref, p_ref, dp_ref):
        qi = pl.program_id(1)
        q = q_ref[...]
        do16 = do_ref[...].astype(bf16)
        base = cc_ref[:, 0:1] - lse_ref[:, 0:1]

        def probs(off):
            s = lax.dot_general(q, k_ref[pl.ds(off, tq), :], NT, preferred_element_type=f32) * ATT_SCALE
            return jnp.exp(s + base - cr_ref[:, pl.ds(off, tq)])

        def keep(off, p, dvec):
            dp = lax.dot_general(do16, v_ref[pl.ds(off, tq), :], NT, preferred_element_type=f32)
            p_ref[:, pl.ds(off, tq)] = p
            dp_ref[:, pl.ds(off, tq)] = dp
            return dvec + _fold(p * dp)

        def below(j, dvec):
            off = pl.multiple_of(j * tq, tq)
            return keep(off, probs(off), dvec)

        dvec = lax.fori_loop(0, qi, below, jnp.zeros((tq, 128), f32))
        off = pl.multiple_of(qi * tq, tq)
        dvec = keep(off, jnp.where(_causal(tq), probs(off), 0.0), dvec)
        delta = jnp.sum(dvec, axis=1, keepdims=True)

        def grad(j, acc):
            off = pl.multiple_of(j * tq, tq)
            ds = p_ref[:, pl.ds(off, tq)] * (dp_ref[:, pl.ds(off, tq)] - delta)
            return acc + jnp.dot(ds.astype(bf16), k_ref[pl.ds(off, tq), :], preferred_element_type=f32)

        dq_ref[...] = lax.fori_loop(0, qi + 1, grad, jnp.zeros((tq, HD), f32)) * ATT_SCALE
        dlr_ref[...] = _to_row(delta)

    qb = lambda h, i: (i, h)
    full = lambda h, i: (0, h)
    col = lambda h, i: (h, i, 0)
    return pl.pallas_call(
        body, name=name, grid=(NH, nq),
        in_specs=[_bs((tq, HD), qb), _bs((T, HD), full), _bs((T, HD), full), _bs((tq, HD), qb),
                  _bs((None, tq, 128), col), _bs((None, tq, 128), col), _bs((None, 1, T), lambda h, i: (h, 0, 0))],
        out_specs=[_bs((tq, HD), qb), _bs((None, 1, tq), lambda h, i: (h, 0, i))],
        out_shape=[S((T, D), f32), S((NH, 1, T), f32)],
        scratch_shapes=[pltpu.VMEM((tq, T), f32), pltpu.VMEM((tq, T), f32)],
        compiler_params=_cp("parallel", "parallel"))(qn, kn, vb, do, lse, ccol, crow3)


def _attn_dkv(name, qn, kn, vb, do, lser3, dlr3, ccol, crow3):
    T = qn.shape[0]
    tk = _tile(T, ATT_TILE)
    nk = T // tk

    def body(q_ref, k_ref, v_ref, do_ref, lser_ref, dlr_ref, cc_ref, cr_ref, dk_ref, dv_ref, cs_ref):
        h = pl.program_id(0)
        kj = pl.program_id(1)

        @pl.when((h == 0) & (kj == 0))
        def _():
            cs_ref[...] = jnp.zeros_like(cs_ref)

        kb = k_ref[...]
        vv = v_ref[...]
        ckey = cc_ref[:, 0:1]

        def block(off, diagonal):
            rows = pl.ds(off, tk)
            qb = q_ref[rows, :]
            do16 = do_ref[rows, :].astype(bf16)
            st = lax.dot_general(kb, qb, NT, preferred_element_type=f32) * ATT_SCALE
            pt = jnp.exp(st + (cr_ref[:, rows] - lser_ref[:, rows]) - ckey)
            if diagonal:
                pt = jnp.where(_causal(tk, keys_down=True), pt, 0.0)
            dpt = lax.dot_general(vv, do16, NT, preferred_element_type=f32)
            dst = pt * (dpt - dlr_ref[:, rows])
            ddv = jnp.dot(pt.astype(bf16), do16, preferred_element_type=f32)
            ddk = jnp.dot(dst.astype(bf16), qb, preferred_element_type=f32)
            return ddk, ddv, _fold(dst)

        def above(i, carry):
            ddk, ddv, dcs = block(pl.multiple_of(i * tk, tk), False)
            return carry[0] + ddk, carry[1] + ddv, carry[2] + dcs

        off = pl.multiple_of(kj * tk, tk)
        dk, dv, cs = lax.fori_loop(kj + 1, nk, above, block(off, True))
        dk_ref[...] = dk * ATT_SCALE
        dv_ref[...] = dv
        lane = lax.broadcasted_iota(jnp.int32, (tk, 128), 1)
        cs_ref[pl.ds(off, tk), :] += jnp.where(lane == h, jnp.sum(cs, axis=1, keepdims=True), 0.0)

    full = lambda h, j: (0, h)
    blk = lambda h, j: (j, h)
    row = lambda h, j: (h, 0, 0)
    return pl.pallas_call(
        body, name=name, grid=(NH, nk),
        in_specs=[_bs((T, HD), full), _bs((tk, HD), blk), _bs((tk, HD), blk), _bs((T, HD), full), _bs((None, 1, T), row),
                  _bs((None, 1, T), row), _bs((None, tk, 128), lambda h, j: (h, j, 0)), _bs((None, 1, T), row)],
        out_specs=[_bs((tk, HD), blk), _bs((tk, HD), blk), _bs((T, 128), lambda h, j: (0, 0))],
        out_shape=[S((T, D), f32), S((T, D), f32), S((T, 128), f32)],
        compiler_params=_cp("arbitrary", "arbitrary"))(qn, kn, vb, do, lser3, dlr3, ccol, crow3)


def _forget_bwd(name, cs, z, bf):
    T = cs.shape[0]

    def body(cs_ref, zf_ref, bf_ref, dz_ref, db_ref):
        c = -cs_ref[...]
        row = lax.broadcasted_iota(jnp.int32, c.shape, 0)
        s = 1
        while s < T:
            c = c + jnp.where(row + s < T, pltpu.roll(c, T - s, 0), 0.0)
            s *= 2
        xf = zf_ref[...] + bf_ref[...]
        lane = lax.broadcasted_iota(jnp.int32, c.shape, 1)
        dxf = jnp.where(lane < NH, c / (1.0 + jnp.exp(xf)), 0.0)
        dz_ref[...] = jnp.zeros_like(dz_ref)
        dz_ref[:, 0:128] = dxf.astype(bf16)
        db_ref[...] = jnp.sum(dxf, axis=0, keepdims=True)

    return pl.pallas_call(
        body, name=name, grid=(1,),
        in_specs=[_bs((T, 128), lambda i: (0, 0)), _bs((T, 128), lambda i: (0, F_OFF // 128)), _vec(1, 128)],
        out_specs=[_bs((T, NZ - F_OFF), lambda i: (0, 0)), _vec(1, 128)],
        out_shape=[S((T, NZ - F_OFF), bf16), S((1, 128), f32)], compiler_params=_cp("arbitrary"))(cs, z, bf)


def _qk_bwd(name, dqn, dkn, dv, z, qg, kg):
    T = z.shape[0]

    def body(dq_ref, dk_ref, dv_ref, q_ref, k_ref, qg_ref, kg_ref, dz_ref, dqg_ref, dkg_ref, g_ref):
        i = pl.program_id(0)
        for n, (src, dsrc, gain, dgain) in enumerate(((q_ref, dq_ref, qg_ref, dqg_ref), (k_ref, dk_ref, kg_ref, dkg_ref))):
            for h in range(NH):
                cs = slice(h * HD, (h + 1) * HD)
                xv = src[:, cs]
                r = lax.rsqrt(jnp.mean(xv * xv, axis=-1, keepdims=True) + RMS_EPS)
                xhat = xv * r
                dy = dsrc[:, cs]
                dyg = dy * gain[:, cs]
                m = jnp.mean(dyg * xhat, axis=-1, keepdims=True)
                dz_ref[:, n * D + h * HD:n * D + (h + 1) * HD] = (r * (dyg - xhat * m)).astype(bf16)
                g_ref[:, cs] = jnp.sum(dy * xhat, axis=0, keepdims=True)
            _acc_store(i, dgain, g_ref[...])
        dz_ref[:, 2 * D:3 * D] = dv_ref[...].astype(bf16)

    return pl.pallas_call(
        body, name=name, grid=(T // TT,),
        in_specs=[_row(), _row(), _row(), _row(5), _row(6), _vec(), _vec()],
        out_specs=[_row(0, 3 * D), _vec(), _vec()], out_shape=[S((T, 3 * D), bf16), S((1, D), f32), S((1, D), f32)],
        scratch_shapes=[pltpu.VMEM((1, D), f32)], compiler_params=_cp("arbitrary"))(dqn, dkn, dv, z, z, qg, kg)


GB = GATE_OFF // D


def _merge_fwd(name, ya, yb, yc, z, bg):
    T = z.shape[0]

    def body(ya_ref, yb_ref, yc_ref, g0_ref, g1_ref, g2_ref, bg_ref, o_ref):
        acc = _sigmoid(g0_ref[...] + bg_ref[0:1, :]) * ya_ref[...]
        acc = acc + _sigmoid(g1_ref[...] + bg_ref[1:2, :]) * yb_ref[...]
        acc = acc + _sigmoid(g2_ref[...] + bg_ref[2:3, :]) * yc_ref[...]
        o_ref[...] = acc.astype(bf16)

    return pl.pallas_call(
        body, name=name, grid=(T // TT,),
        in_specs=[_row(), _row(), _row(), _row(GB), _row(GB + 1), _row(GB + 2), _vec(3)],
        out_specs=_row(), out_shape=S((T, D), bf16), compiler_params=_cp("parallel"))(ya, yb, yc, z, z, z, bg)


def _merge_bwd(name, dm, ya, yb, yc, z, bg):
    T = z.shape[0]

    def body(dm_ref, ya_ref, yb_ref, yc_ref, g0_ref, g1_ref, g2_ref, bg_ref, dya_ref, dyb_ref, dyc_ref, dz_ref, db_ref):
        i = pl.program_id(0)
        dm_v = dm_ref[...]
        dbs = []
        for n, (y_ref, g_ref, dy_ref) in enumerate(((ya_ref, g0_ref, dya_ref), (yb_ref, g1_ref, dyb_ref),
                                                    (yc_ref, g2_ref, dyc_ref))):
            gate = _sigmoid(g_ref[...] + bg_ref[n:n + 1, :])
            dy_ref[...] = (dm_v * gate).astype(bf16)
            dl = dm_v * y_ref[...] * gate * (1.0 - gate)
            dz_ref[:, n * D:(n + 1) * D] = dl.astype(bf16)
            dbs.append(jnp.sum(dl, axis=0, keepdims=True))
        _acc_store(i, db_ref, jnp.concatenate(dbs, axis=0))

    return pl.pallas_call(
        body, name=name, grid=(T // TT,),
        in_specs=[_row(), _row(), _row(), _row(), _row(GB), _row(GB + 1), _row(GB + 2), _vec(3)],
        out_specs=[_row(), _row(), _row(), _row(0, 3 * D), _vec(3)],
        out_shape=[S((T, D), bf16), S((T, D), bf16), S((T, D), bf16), S((T, 3 * D), bf16), S((3, D), f32)],
        compiler_params=_cp("arbitrary"))(dm, ya, yb, yc, z, z, z, bg)


SMALL_NAMES = ("ffn1_norm", "mix_norm", "b_forget", "b_gate", "conv_w", "sgu_ln_g", "sgu_ln_b", "sgu_w", "sgu_b",
               "q_norm_g", "k_norm_g", "ffn2_norm")


def _small_params(p):
    out = {n: p[n].reshape(1, D) for n in ("ffn1_norm", "mix_norm", "ffn2_norm", "sgu_ln_g", "sgu_ln_b", "q_norm_g", "k_norm_g")}
    out["b_forget"] = jnp.pad(p["b_forget"].reshape(1, NH), ((0, 0), (0, 128 - NH)))
    out["b_gate"] = p["b_gate"]
    out["conv_w"] = p["conv_w"]
    out["sgu_w"] = p["sgu_w"]
    out["bmap"] = jnp.repeat(p["sgu_b"].T, HD, axis=1)
    return out


def _small_grads_natural(sg):
    out = {n: sg[n].reshape(D) for n in ("ffn1_norm", "mix_norm", "ffn2_norm", "sgu_ln_g", "sgu_ln_b")}
    out["q_norm_g"] = sg["q_norm_g"].reshape(NH, HD)
    out["k_norm_g"] = sg["k_norm_g"].reshape(NH, HD)
    out["b_forget"] = sg["b_forget"][0, :NH]
    out["b_gate"] = sg["b_gate"]
    out["conv_w"] = sg["conv_w"]
    out["sgu_w"] = sg["sgu_w"]
    out["sgu_b"] = sg["sgu_b"]
    return out


SQ_TM = 1024


def _sq_fwd(name, a, wsq, l, n, res=None):
    T = a.shape[0]
    tm = _tile(T, SQ_TM)
    return _mm(name, a, wsq, grid=(T // tm, 1, 1), a_spec=_bs((tm, D), lambda i, j, k: (i, 0)),
               b_spec=_bs((None, None, D, D), lambda i, j, k: (l, n, 0, 0)),
               out_shape=S((T, D), f32), out_spec=_bs((tm, D), lambda i, j, k: (i, 0)), dims=NN, acc_shape=None,
               res=res, res_spec=_bs((tm, D), lambda i, j, k: (i, 0)))


def _sq_bwd_in(name, dy, wsq, l, n):
    T = dy.shape[0]
    tm = _tile(T, SQ_TM)
    return _mm(name, dy, wsq, grid=(T // tm, 1, 1), a_spec=_bs((tm, D), lambda i, j, k: (i, 0)),
               b_spec=_bs((None, None, D, D), lambda i, j, k: (l, n, 0, 0)),
               out_shape=S((T, D), f32), out_spec=_bs((tm, D), lambda i, j, k: (i, 0)), dims=NT, acc_shape=None)


def _sq_bwd_w(name, a, dy, gbuf, l):
    T = a.shape[0]
    return _mm(name, a, dy, grid=(NDEV // 2, 1, 1), a_spec=_bs((T, 256), lambda i, j, k: (0, i)),
               b_spec=_bs((T, D), lambda i, j, k: (0, 0)), out_shape=S(gbuf.shape, bf16),
               out_spec=_bs((2, None, None, 128, D), lambda i, j, k: (0, i, l, 0, 0)),
               dims=TN, acc_shape=None, alias=gbuf, split_rows=128)


def _ffn_fwd(tag, x, g, wgu, wd, l):
    T = x.shape[0]
    h = _rms_fwd(tag + "_rms", x, g)
    gu, a = _swiglu_fwd(tag + "_gu", h, wgu, l)
    tm = _tile(T, 1024)
    xo = _mm(tag + "_down", a, wd, grid=(T // tm, 1, 4), a_spec=_bs((None, tm, GU), lambda i, j, k: (k, i, 0)),
             b_spec=_bs((None, None, GU, D), lambda i, j, k: (l, k, 0, 0)), out_shape=S((T, D), f32),
             out_spec=_bs((tm, D), lambda i, j, k: (i, 0)), dims=NN, acc_shape=(tm, D), res=x,
             res_spec=_bs((tm, D), lambda i, j, k: (i, 0)), alpha=0.5)
    return xo, (h, gu, a)


def _ffn_bwd(tag, dxo, x, g, wgu, wd, l, saved, g_gu, g_d, hook):
    h, gu, a = saved
    T = x.shape[0]
    g_d = _mm(tag + "_dwd", a, dxo, grid=(4, 1, 1), a_spec=_bs((None, T, GU), lambda i, j, k: (i, 0, 0)),
              b_spec=_bs((T, D), lambda i, j, k: (0, 0)), out_shape=S(g_d.shape, bf16),
              out_spec=_bs((2, None, None, GU // 2, D), lambda i, j, k: (0, i, l, 0, 0)), dims=TN, acc_shape=None,
              alpha=0.5, alias=g_d, split_rows=GU // 2)
    dxo = hook(dxo, g_d)
    dgu = _swiglu_bwd(tag + "_dgu", dxo, wd, gu, l).reshape(NDEV, T, GU)
    dh = _mm(tag + "_dh", dgu, wgu, grid=(1, 1, NDEV), a_spec=_bs((None, T, GU), lambda i, j, k: (k, 0, 0)),
             b_spec=_bs((None, None, GU, D), lambda i, j, k: (l, k, 0, 0)), out_shape=S((T, D), f32),
             out_spec=_bs((T, D), lambda i, j, k: (0, 0)), dims=NN, acc_shape=(T, D))
    g_gu = _mm(tag + "_dwgu", dgu, h, grid=(NDEV, 1, 1), a_spec=_bs((None, T, GU), lambda i, j, k: (i, 0, 0)),
               b_spec=_bs((T, D), lambda i, j, k: (0, 0)), out_shape=S(g_gu.shape, bf16),
               out_spec=_bs((None, None, None, GU, D), lambda i, j, k: (i % 2, i // 2, l, 0, 0)), dims=TN,
               acc_shape=None, alias=g_gu)
    dx, dg = _rms_bwd(tag + "_drms", dh, x, g, dxo)
    return dx, dg, g_gu, g_d


def _mixer_fwd(tag, x, p, win, wsq, l):
    T = x.shape[0]
    h = _rms_fwd(tag + "_rms", x, p["mix_norm"])
    tn = 512
    z = _mm(tag + "_in", h, win, grid=(1, NZ // tn, 1), a_spec=_bs((T, D), lambda i, j, k: (0, 0)),
            b_spec=_bs((D, tn), lambda i, j, k: (0, j)), out_shape=S((T, NZ), f32),
            out_spec=_bs((T, tn), lambda i, j, k: (0, j)), dims=NN, acc_shape=None)
    ya_in = _conv_fwd(tag + "_conv", z, p["conv_w"])
    yb_in = _sgu_fwd(tag + "_sgu", z, p["sgu_ln_g"], p["sgu_ln_b"], p["sgu_w"], p["bmap"])
    qn, kn, vb, logf = _qk_fwd(tag + "_qk", z, p["q_norm_g"], p["k_norm_g"], p["b_forget"])
    ccol, crow = _cum_fwd(tag + "_cum", logf)
    crow3 = crow.reshape(NH, 1, T)
    o, lse, lser = _attn_fwd(tag + "_attn", qn, kn, vb, ccol, crow3)
    ya = _sq_fwd(tag + "_oconv", ya_in, wsq, l, 0)
    yb = _sq_fwd(tag + "_osgu", yb_in, wsq, l, 1)
    yc = _sq_fwd(tag + "_oattn", o, wsq, l, 2)
    merged = _merge_fwd(tag + "_merge", ya, yb, yc, z, p["b_gate"])
    xo = _sq_fwd(tag + "_o", merged, wsq, l, 3, res=x)
    return xo, (h, z, ya_in, yb_in, qn, kn, vb, ccol, crow3, o, lse, lser, ya, yb, yc, merged)


def _mixer_bwd(tag, dxo, x, p, win, wsq, l, saved, gsq, hook):
    h, z, ya_in, yb_in, qn, kn, vb, ccol, crow3, o, lse, lser, ya, yb, yc, merged = saved
    T = x.shape[0]
    sg = {}
    dm = _sq_bwd_in(tag + "_dmerged", dxo, wsq, l, 3)
    dxo = hook(dxo, dm)
    gsq[3] = _sq_bwd_w(tag + "_dwo", merged, dxo, gsq[3], l)
    dya, dyb, dyc, dz_g, sg["b_gate"] = _merge_bwd(tag + "_dmerge", dm, ya, yb, yc, z, p["b_gate"])
    d_ya_in = _sq_bwd_in(tag + "_dconv_in", dya, wsq, l, 0)
    gsq[0] = _sq_bwd_w(tag + "_dwoc", ya_in, dya, gsq[0], l)
    d_yb_in = _sq_bwd_in(tag + "_dsgu_in", dyb, wsq, l, 1)
    gsq[1] = _sq_bwd_w(tag + "_dwos", yb_in, dyb, gsq[1], l)
    d_o = _sq_bwd_in(tag + "_dattn_in", dyc, wsq, l, 2)
    gsq[2] = _sq_bwd_w(tag + "_dwoa", o, dyc, gsq[2], l)
    dz_c, sg["conv_w"] = _conv_bwd(tag + "_dconv", d_ya_in, z, p["conv_w"])
    dz_s, sg["sgu_ln_g"], sg["sgu_ln_b"], sg["sgu_w"], db_t = _sgu_bwd(
        tag + "_dsgu", d_yb_in, z, p["sgu_ln_g"], p["sgu_ln_b"], p["sgu_w"], p["bmap"])
    sg["sgu_b"] = db_t.T
    dqn, dlr = _attn_dq(tag + "_dattn_q", qn, kn, vb, d_o, lse, ccol, crow3)
    dkn, dv, cs = _attn_dkv(tag + "_dattn_kv", qn, kn, vb, d_o, lser, dlr, ccol, crow3)
    dz_f, sg["b_forget"] = _forget_bwd(tag + "_dforget", cs, z, p["b_forget"])
    dz_q, sg["q_norm_g"], sg["k_norm_g"] = _qk_bwd(tag + "_dqk", dqn, dkn, dv, z, p["q_norm_g"], p["k_norm_g"])
    dz = [dz_c, dz_s, dz_q, dz_g, dz_f]
    dh = _dz_matmul(tag + "_dh", dz, win, False)
    dwin = _dz_matmul(tag + "_dwin", dz, h, True)
    dx, sg["mix_norm"] = _rms_bwd(tag + "_drms", dh, x, p["mix_norm"], dxo)
    return dx, sg, dwin, gsq


ANY = pl.BlockSpec(memory_space=pl.ANY)
HBM = pl.BlockSpec(memory_space=pltpu.HBM)
SEM = pl.BlockSpec(memory_space=pltpu.SEMAPHORE)
EFFECT = pltpu.SideEffectType.DATAFLOW_SIDE_EFFECTING


def _place():
    return lax.axis_index("x"), lax.axis_index("y"), lax.axis_index("c")


NEAR = 4


def _others(x, y, c):
    return [(x, y, 1 - c), (1 - x, y, c), (x, 1 - y, c), (1 - x, 1 - y, c)]


def _gather_start(name, groups, carry=None):
    sizes = [len(g) for g in groups]
    srcs = [s for g in groups for s, _ in g]
    lands = [b for g in groups for _, b in g]
    n, ng = len(srcs), len(groups)
    held = srcs + lands + ([] if carry is None else [carry])
    nh = len(held)

    def body(*refs):
        src_refs, land_refs = refs[:n], refs[n:2 * n]
        send, recv = refs[nh:nh + ng], refs[nh + ng:nh + 2 * ng]
        x, y, c = _place()
        me = 4 * x + 2 * y + c
        u = 0
        for g, size in enumerate(sizes):
            for i in range(size):
                for k, peer in enumerate(_others(x, y, c)):
                    pltpu.make_async_remote_copy(src_ref=src_refs[u], dst_ref=land_refs[u].at[me],
                                                 send_sem=send[g].at[i * NEAR + k], recv_sem=recv[g].at[i * NEAR + k],
                                                 device_id=peer, device_id_type=MESH).start()
                u += 1

    sems = [pltpu.SemaphoreType.DMA((size * NEAR,)) for size in sizes]
    out = pl.pallas_call(
        body, name=name, in_specs=[HBM] * nh, out_specs=[SEM] * (2 * ng) + [HBM] * nh,
        out_shape=sems + sems + [pltpu.HBM(a.shape, a.dtype) for a in held],
        input_output_aliases={i: 2 * ng + i for i in range(nh)},
        compiler_params=pltpu.CompilerParams(has_side_effects=EFFECT),
    )(*[pltpu.with_memory_space_constraint(a, pltpu.HBM) for a in held])
    res, pos = [], 0
    for g, size in enumerate(sizes):
        res.append((out[g], out[ng + g], out[2 * ng + pos:2 * ng + pos + size], out[2 * ng + n + pos:2 * ng + n + pos + size]))
        pos += size
    return res if carry is None else (res, out[2 * ng + 2 * n])


def _gather_wait(name, started, after=None):
    send, recv, srcs, lands = started
    n = len(srcs)

    def body(*refs):
        src_refs, land_refs = refs[:n], refs[n:2 * n]
        send_ref, recv_ref = refs[2 * n], refs[2 * n + 1]
        x, y, c = _place()
        for i in range(n):
            for k, (px, py, pc) in enumerate(_others(x, y, c)):
                cp = pltpu.make_async_remote_copy(src_ref=src_refs[i], dst_ref=land_refs[i].at[4 * px + 2 * py + pc],
                                                  send_sem=send_ref.at[i * NEAR + k], recv_sem=recv_ref.at[i * NEAR + k],
                                                  device_id=(px, py, pc), device_id_type=MESH)
                cp.wait_send()
                cp.wait_recv()

    extra = [] if after is None else [after]
    out = pl.pallas_call(
        body, name=name, in_specs=[HBM] * (2 * n) + [SEM, SEM] + [ANY] * len(extra), out_specs=[HBM] * (2 * n),
        out_shape=[pltpu.HBM(a.shape, a.dtype) for a in list(srcs) + list(lands)],
        input_output_aliases={i: i for i in range(2 * n)},
        compiler_params=pltpu.CompilerParams(has_side_effects=EFFECT),
    )(*srcs, *lands, send, recv, *extra)
    return out[n:]


def _gather_forward(name, lands):
    n = len(lands)

    def body(*refs):
        have, full = refs[:n], refs[n:2 * n]
        send, recv = refs[2 * n], refs[2 * n + 1]
        x, y, c = _place()
        chips = [(1 - x, y), (x, 1 - y), (1 - x, 1 - y)]

        def copy(i, j, core):
            slot = 4 * chips[j][0] + 2 * chips[j][1] + core
            return pltpu.make_async_remote_copy(src_ref=have[i].at[slot], dst_ref=full[i].at[slot],
                                                send_sem=send.at[i * 3 + j], recv_sem=recv.at[i * 3 + j],
                                                device_id=(x, y, 1 - c), device_id_type=MESH)

        for i in range(n):
            for j in range(3):
                copy(i, j, c).start()
        for i in range(n):
            for j in range(3):
                copy(i, j, c).wait_send()
                copy(i, j, 1 - c).wait_recv()

    return pl.pallas_call(
        body, name=name, in_specs=[ANY] * n, out_specs=[ANY] * n, out_shape=[S(a.shape, a.dtype) for a in lands],
        input_output_aliases={i: i for i in range(n)},
        scratch_shapes=[pltpu.SemaphoreType.DMA((n * 3,)), pltpu.SemaphoreType.DMA((n * 3,))],
    )(*lands)


def _pair_copies(ins, outs, send, recv):
    x, y, c = _place()
    return [pltpu.make_async_remote_copy(src_ref=ins[u].at[1 - c], dst_ref=outs[u], send_sem=send.at[u],
                                         recv_sem=recv.at[u], device_id=(x, y, 1 - c), device_id_type=MESH)
            for u in range(len(ins))]


def _pair_start(name, gs, carry):
    n = len(gs)
    held = list(gs) + [lax.empty(g.shape[1:], g.dtype) for g in gs] + [carry]

    def body(*refs):
        for cp in _pair_copies(refs[:n], refs[n:2 * n], refs[2 * n + 1], refs[2 * n + 2]):
            cp.start()

    sems = [pltpu.SemaphoreType.DMA((n,))] * 2
    out = pl.pallas_call(
        body, name=name, in_specs=[HBM] * len(held), out_specs=[SEM, SEM] + [HBM] * len(held),
        out_shape=sems + [pltpu.HBM(a.shape, a.dtype) for a in held],
        input_output_aliases={i: 2 + i for i in range(len(held))},
        compiler_params=pltpu.CompilerParams(has_side_effects=EFFECT),
    )(*[pltpu.with_memory_space_constraint(a, pltpu.HBM) for a in held])
    return (out[0], out[1], out[2:2 + n], out[2 + n:2 + 2 * n]), out[2 + 2 * n]


def _pair_wait(name, started, after):
    send, recv, gs, lands = started
    n = len(gs)

    def body(*refs):
        for cp in _pair_copies(refs[:n], refs[n:2 * n], refs[2 * n], refs[2 * n + 1]):
            cp.wait_send()
            cp.wait_recv()

    out = pl.pallas_call(
        body, name=name, in_specs=[HBM] * (2 * n) + [SEM, SEM, ANY], out_specs=[HBM] * (2 * n),
        out_shape=[pltpu.HBM(a.shape, a.dtype) for a in list(gs) + list(lands)],
        input_output_aliases={i: i for i in range(2 * n)},
        compiler_params=pltpu.CompilerParams(has_side_effects=EFFECT),
    )(*gs, *lands, send, recv, after)
    return out[:n], out[n:]


def _row_tile(r, c):
    if c > D and r % 128 == 0:
        return 128
    return 256 if r % 256 == 0 else (GU // 2 if r % (GU // 2) == 0 else r)


def _pair_sum(name, core, g, r1):
    _, nq, nl, r, c = g.shape
    tr = _row_tile(r, c)
    g4 = g.reshape(2, nq * nl, r, c)
    r3 = r1.reshape(nq * nl, r, c)

    def body(core_ref, g_ref, r_ref, o_ref):
        o_ref[...] = (g_ref[...].astype(f32) + r_ref[...].astype(f32)).astype(bf16)

    out = pl.pallas_call(
        body, name=name,
        grid_spec=pltpu.PrefetchScalarGridSpec(
            num_scalar_prefetch=1, grid=(nq * nl, r // tr),
            in_specs=[_bs((None, None, tr, c), lambda b, i, cr: (cr[0], b, i, 0)), _bs((None, tr, c), lambda b, i, cr: (b, i, 0))],
            out_specs=_bs((None, tr, c), lambda b, i, cr: (b, i, 0))),
        out_shape=S((nq * nl, r, c), bf16), compiler_params=_cp("parallel", "parallel"))(core, g4, r3)
    return out.reshape(nq, nl, r, c)


def _scatter_copies(ins, outs, send, recv):
    x, y, c = _place()
    chips = [(1 - x, y), (x, 1 - y), (1 - x, 1 - y)]
    return [pltpu.make_async_remote_copy(src_ref=ins[u].at[2 * chip[0] + chip[1]], dst_ref=outs[u].at[k],
                                         send_sem=send.at[u * 3 + k], recv_sem=recv.at[u * 3 + k],
                                         device_id=(*chip, c), device_id_type=MESH)
            for u in range(len(ins)) for k, chip in enumerate(chips)]


def _scatter_start(name, ss, carry):
    n = len(ss)
    lands = [lax.empty((3,) + s.shape[1:], s.dtype) for s in ss]
    held = list(ss) + lands + [carry]

    def body(*refs):
        for cp in _scatter_copies(refs[:n], refs[n:2 * n], refs[2 * n + 1], refs[2 * n + 2]):
            cp.start()

    sems = [pltpu.SemaphoreType.DMA((n * 3,))] * 2
    out = pl.pallas_call(
        body, name=name, in_specs=[HBM] * len(held), out_specs=[SEM, SEM] + [HBM] * len(held),
        out_shape=sems + [pltpu.HBM(a.shape, a.dtype) for a in held],
        input_output_aliases={i: 2 + i for i in range(len(held))},
        compiler_params=pltpu.CompilerParams(has_side_effects=EFFECT),
    )(*[pltpu.with_memory_space_constraint(a, pltpu.HBM) for a in held])
    return (out[0], out[1], out[2:2 + n], out[2 + n:2 + 2 * n]), out[2 + 2 * n]


def _scatter_wait(name, started, after):
    send, recv, srcs, lands = started
    n = len(srcs)

    def body(*refs):
        for cp in _scatter_copies(refs[:n], refs[n:2 * n], refs[2 * n], refs[2 * n + 1]):
            cp.wait_send()
            cp.wait_recv()

    out = pl.pallas_call(
        body, name=name, in_specs=[HBM] * (2 * n) + [SEM, SEM, ANY], out_specs=[HBM] * (2 * n),
        out_shape=[pltpu.HBM(a.shape, a.dtype) for a in list(srcs) + list(lands)],
        input_output_aliases={i: i for i in range(2 * n)},
        compiler_params=pltpu.CompilerParams(has_side_effects=EFFECT),
    )(*srcs, *lands, send, recv, after)
    return out[:n], out[n:]


def _sum_blocks(name, blocks):
    def body(b_ref, o_ref):
        acc = b_ref[0]
        for d in range(1, NDEV):
            acc = acc + b_ref[d]
        o_ref[...] = acc

    return pl.pallas_call(body, name=name, out_shape=S(blocks.shape[1:], f32),
                          compiler_params=pltpu.CompilerParams(vmem_limit_bytes=VMEM_LIMIT))(blocks)


def _adam_math(w, g, m, v):
    m = ADAM_B1 * m + (1.0 - ADAM_B1) * g
    v = ADAM_B2 * v + (1.0 - ADAM_B2) * (g * g)
    m_hat = m / (1.0 - ADAM_B1 ** ADAM_STEP)
    v_hat = v / (1.0 - ADAM_B2 ** ADAM_STEP)
    delta = -ADAM_LR * (m_hat / (jnp.sqrt(v_hat) + ADAM_EPS) + ADAM_WD * w)
    return delta, m, v


def _adamw(name, w, m, v, parts):
    _, r, c = w.shape
    tr = _row_tile(r, c)

    def body(w_ref, m_ref, v_ref, *refs):
        sets, (g_ref, d_ref, mo_ref, vo_ref) = (refs[0:4], refs[4:8]), refs[8:]
        for l in range(2):
            @pl.when(pl.program_id(0) == l)
            def _():
                s_ref, r0_ref, r1_ref, r2_ref = sets[l]
                g = ((s_ref[...].astype(f32) + r0_ref[...].astype(f32)) + r1_ref[...].astype(f32)) + r2_ref[...].astype(f32)
                g_ref[...] = g
                d_ref[...], mo_ref[...], vo_ref[...] = _adam_math(w_ref[...], g, m_ref[...], v_ref[...])

    blk = _bs((None, tr, c), lambda l, i: (l, i, 0))
    operands, specs = [], []
    for n in range(2):
        row = (lambda l, i: i * (1 - l)) if n == 0 else (lambda l, i: i * l)
        s_mine, r2 = parts[n]
        operands += [s_mine, r2, r2, r2]
        specs.append(_bs((tr, c), functools.partial(lambda l, i, row: (row(l, i), 0), row=row)))
        specs += [_bs((None, None, tr, c), functools.partial(lambda l, i, k, row: (k, 0, row(l, i), 0), k=k, row=row))
                  for k in range(3)]
    return pl.pallas_call(
        body, name=name, grid=(2, r // tr), in_specs=[blk, blk, blk] + specs,
        out_specs=[blk] * 4, out_shape=[S(w.shape, f32)] * 4, compiler_params=_cp("arbitrary", "arbitrary"),
    )(w, m, v, *operands)


def _adamw_small(name, w, g, m, v):
    def body(w_ref, g_ref, m_ref, v_ref, d_ref, mo_ref, vo_ref):
        d_ref[...], mo_ref[...], vo_ref[...] = _adam_math(w_ref[...], g_ref[...], m_ref[...], v_ref[...])

    return pl.pallas_call(body, name=name, out_shape=[S(w.shape, f32)] * 3,
                          compiler_params=pltpu.CompilerParams(vmem_limit_bytes=VMEM_LIMIT))(w, g, m, v)


WEIGHT_NAMES = ("ffn1_norm", "ffn1_w_gu", "ffn1_w_down", "mix_norm", "w_in", "b_forget", "b_gate", "conv_w", "sgu_ln_g",
                "sgu_ln_b", "sgu_w", "sgu_b", "q_norm_g", "k_norm_g", "w_out_conv", "w_out_sgu", "w_out_attn", "w_o",
                "ffn2_norm", "ffn2_w_gu", "ffn2_w_down")
BIG = {"ffn1_w_gu": "gu1", "ffn2_w_gu": "gu2", "ffn1_w_down": "d1", "ffn2_w_down": "d2", "w_in": "in",
       "w_out_conv": "oc", "w_out_sgu": "os", "w_out_attn": "oa", "w_o": "o"}
BIG_KEYS = ("gu1", "gu2", "d1", "d2", "in", "oc", "os", "oa", "o")
REPLICATED_SMALL = ("ffn1_norm", "mix_norm", "b_forget", "sgu_ln_g", "sgu_ln_b", "sgu_w", "sgu_b", "q_norm_g",
                    "k_norm_g", "ffn2_norm")
SHARDED_SMALL = ("b_gate", "conv_w")
TRANSPOSED = ("gu1", "gu2")


def _packed_rows(shape):
    size = 1
    for s_ in shape:
        size *= s_
    return size, -(-size // 1024) * 8


def _pack(arrays):
    pieces = []
    for a in arrays:
        size, rows = _packed_rows(a.shape)
        pieces.append(jnp.pad(a.reshape(-1).astype(f32), (0, rows * 128 - size)).reshape(rows, 128))
    return jnp.concatenate(pieces, axis=0)


def _unpack(packed, shapes):
    out, pos = [], 0
    for shp in shapes:
        size, rows = _packed_rows(shp)
        out.append(packed[pos:pos + rows].reshape(-1)[:size].reshape(shp))
        pos += rows
    return out


def _natural_runs(a, b):
    runs = []
    while a < b:
        d = a // INB
        e = min(b, (d + 1) * INB)
        runs.append((d, a - d * INB, e - d * INB))
        a = e
    return runs


def _win_kernel_layout(wg):
    runs = _natural_runs(0, GATE_OFF) + _natural_runs(GATE_OFF + NH, NIN) + _natural_runs(GATE_OFF, GATE_OFF + NH)
    return jnp.concatenate([wg[d, :, a:b] for d, a, b in runs] + [jnp.zeros((D, NZ - NIN), wg.dtype)], axis=1)


def _kernel_column(n):
    return n if n < GATE_OFF else (F_OFF + n - GATE_OFF if n < GATE_OFF + NH else n - NH)


def _win_device_block(dw, d):
    cuts = sorted({d * INB, (d + 1) * INB} | {c for c in (GATE_OFF, GATE_OFF + NH) if d * INB < c < (d + 1) * INB})
    parts = [dw[:, _kernel_column(a):_kernel_column(a) + (b - a)] for a, b in zip(cuts[:-1], cuts[1:])]
    return parts[0] if len(parts) == 1 else jnp.concatenate(parts, axis=1)


def kernel(x, ffn1_norm, ffn1_w_gu, ffn1_w_down, mix_norm, w_in, b_forget, b_gate, conv_w, sgu_ln_g, sgu_ln_b, sgu_w, sgu_b, q_norm_g, k_norm_g, w_out_conv, w_out_sgu, w_out_attn, w_o, ffn2_norm, ffn2_w_gu, ffn2_w_down, loss_target, m_ffn1_norm, m_ffn1_w_gu, m_ffn1_w_down, m_mix_norm, m_w_in, m_b_forget, m_b_gate, m_conv_w, m_sgu_ln_g, m_sgu_ln_b, m_sgu_w, m_sgu_b, m_q_norm_g, m_k_norm_g, m_w_out_conv, m_w_out_sgu, m_w_out_attn, m_w_o, m_ffn2_norm, m_ffn2_w_gu, m_ffn2_w_down, v_ffn1_norm, v_ffn1_w_gu, v_ffn1_w_down, v_mix_norm, v_w_in, v_b_forget, v_b_gate, v_conv_w, v_sgu_ln_g, v_sgu_ln_b, v_sgu_w, v_sgu_b, v_q_norm_g, v_k_norm_g, v_w_out_conv, v_w_out_sgu, v_w_out_attn, v_w_o, v_ffn2_norm, v_ffn2_w_gu, v_ffn2_w_down):
    w = dict(zip(WEIGHT_NAMES, (ffn1_norm, ffn1_w_gu, ffn1_w_down, mix_norm, w_in, b_forget, b_gate, conv_w, sgu_ln_g,
                                sgu_ln_b, sgu_w, sgu_b, q_norm_g, k_norm_g, w_out_conv, w_out_sgu, w_out_attn, w_o,
                                ffn2_norm, ffn2_w_gu, ffn2_w_down)))
    mom = dict(zip(WEIGHT_NAMES, (m_ffn1_norm, m_ffn1_w_gu, m_ffn1_w_down, m_mix_norm, m_w_in, m_b_forget, m_b_gate,
                                  m_conv_w, m_sgu_ln_g, m_sgu_ln_b, m_sgu_w, m_sgu_b, m_q_norm_g, m_k_norm_g,
                                  m_w_out_conv, m_w_out_sgu, m_w_out_attn, m_w_o, m_ffn2_norm, m_ffn2_w_gu,
                                  m_ffn2_w_down)))
    var = dict(zip(WEIGHT_NAMES, (v_ffn1_norm, v_ffn1_w_gu, v_ffn1_w_down, v_mix_norm, v_w_in, v_b_forget, v_b_gate,
                                  v_conv_w, v_sgu_ln_g, v_sgu_ln_b, v_sgu_w, v_sgu_b, v_q_norm_g, v_k_norm_g,
                                  v_w_out_conv, v_w_out_sgu, v_w_out_attn, v_w_o, v_ffn2_norm, v_ffn2_w_gu,
                                  v_ffn2_w_down)))
    px, py, pc = _place()
    dev = 4 * px + 2 * py + pc
    chip = 2 * px + py

    big_names = [n for n in WEIGHT_NAMES if n in BIG]
    key_name = {BIG[n]: n for n in big_names}
    group_keys = (("gu1", "d1"), ("in", "oc", "os", "oa", "o", "small"), ("gu2", "d2"))

    def source(key, l):
        if key == "small":
            return jnp.concatenate([w["b_gate"][l], w["conv_w"][l], jnp.zeros((2, 128), f32)], axis=0)
        block = w[key_name[key]][l]
        return (block.T if key in TRANSPOSED else block).astype(bf16)

    def landing(src):
        return lax.dynamic_update_slice(lax.empty((NDEV,) + src.shape, src.dtype), src[None], (dev, 0, 0))

    groups = [[(s, landing(s)) for s in (source(k, l) for k in keys)] for l in range(2) for keys in group_keys]
    started = _gather_start("gather_start", groups)

    def weights(l, part, after):
        got = _gather_wait(f"gather_wait_{l}_{part}", started[3 * l + part], after)
        return dict(zip(group_keys[part], _gather_forward(f"gather_forward_{l}_{part}", got)))

    xl = x[0]
    saved, small, wts = [], [], []
    for l in range(2):
        ga = weights(l, 0, xl if l else None)
        wt = {"gu1": ga["gu1"][None], "d1": ga["d1"].reshape(1, 4, GU, D)}
        x1, s1 = _ffn_fwd("ffn1", xl, w["ffn1_norm"][l].reshape(1, D), wt["gu1"], wt["d1"], 0)
        gb = weights(l, 1, x1)
        p = {n: w[n][l] for n in REPLICATED_SMALL}
        p["b_gate"] = jnp.transpose(gb["small"][:, 0:3, :], (1, 0, 2)).reshape(3, D)
        p["conv_w"] = jnp.transpose(gb["small"][:, 3:6, :], (1, 0, 2)).reshape(3, D)
        p = _small_params(p)
        wt["win"] = _win_kernel_layout(gb["in"])
        wt["sq"] = jnp.stack([gb[k].reshape(D, D) for k in ("oc", "os", "oa", "o")])[None]
        x2, sm = _mixer_fwd("mix", x1, p, wt["win"], wt["sq"], 0)
        gc = weights(l, 2, x2)
        wt.update({"gu2": gc["gu2"][None], "d2": gc["d2"].reshape(1, 4, GU, D)})
        x3, s2 = _ffn_fwd("ffn2", x2, p["ffn2_norm"], wt["gu2"], wt["d2"], 0)
        saved.append((xl, x1, x2, s1, sm, s2))
        small.append(p)
        wts.append(wt)
        xl = x3
    loss_row, dx = _loss("loss", xl, loss_target[0])

    core = pc.reshape(1).astype(jnp.int32)
    buf = lambda r, c: lax.empty((2, 4, 1, r, c), bf16)
    flights = {}

    pairs = []

    def scatter(l, part, bufs, dx):
        started, dx = _pair_start(f"pair_start_{l}_{part}", bufs, dx)
        pairs.append((l, part, started))
        return dx

    def finish(dx, after):
        if pairs:
            l, part, started = pairs.pop()
            bufs, r1 = _pair_wait(f"pair_wait_{l}_{part}", started, after)
            ss = [_pair_sum(f"pair_sum_{l}_{k}", core, g, r) for k, g, r in zip(group_keys[part], bufs, r1)]
            flights[l, part], dx = _scatter_start(f"scatter_start_{l}_{part}", ss, dx)
        return dx

    sgrads = [None, None]
    for l in (1, 0):
        p, wt = small[l], wts[l]
        x0, x1, x2, s1, sm, s2 = saved[l]
        dx, dn2, g_gu2, g_d2 = _ffn_bwd("ffn2", dx, x2, p["ffn2_norm"], wt["gu2"], wt["d2"], 0, s2, buf(GU, D),
                                        buf(GU // 2, D), finish)
        dx = scatter(l, 2, [g_gu2, g_d2], dx)
        dx, sg, dwin, gsq = _mixer_bwd("mix", dx, x1, p, wt["win"], wt["sq"], 0, sm, [buf(128, D) for _ in range(4)], finish)
        g_in = jnp.stack([_win_device_block(dwin, 2 * q + c) for c in range(2) for q in range(4)]).reshape(2, 4, 1, D, INB)
        dx = scatter(l, 1, [g_in] + gsq, dx)
        dx, dn1, g_gu1, g_d1 = _ffn_bwd("ffn1", dx, x0, p["ffn1_norm"], wt["gu1"], wt["d1"], 0, s1, buf(GU, D),
                                        buf(GU // 2, D), finish)
        dx = scatter(l, 0, [g_gu1, g_d1], dx)
        sg["ffn1_norm"] = dn1
        sg["ffn2_norm"] = dn2
        sgrads[l] = sg
    dx = finish(dx, dx)

    nat = [_small_grads_natural(sgrads[l]) for l in range(2)]
    order = REPLICATED_SMALL + SHARDED_SMALL
    part = _pack([jnp.stack([nat[0][n], nat[1][n]]) for n in order] + [loss_row[0, 0:1]])
    small_flight, dx = _gather_start("small_start", [[(part, landing(part))]], carry=dx)

    grads, delta, new_m, new_v = {}, {}, {}, {}
    after = dx
    for part in (2, 1, 0):
        sets = []
        for l in (1, 0):
            s_all, r2_all = _scatter_wait(f"scatter_wait_{l}_{part}", flights[l, part], after)
            sets.append([(lax.dynamic_index_in_dim(s, chip, 0, keepdims=False)[0], r) for s, r in zip(s_all, r2_all)])
        for i, k in enumerate(k for k in group_keys[part] if k != "small"):
            n = key_name[k]
            view = (lambda a: jnp.swapaxes(a, 1, 2)) if k in TRANSPOSED else (lambda a: a)
            outs = _adamw("adamw_" + k, view(w[n]), view(mom[n]), view(var[n]), [sets[1][i], sets[0][i]])
            grads[n], delta[n], new_m[n], new_v[n] = [view(o) for o in outs]
            after = outs[1]

    blocks = _gather_forward("small_forward", _gather_wait("small_wait", small_flight[0], after))
    total = _sum_blocks("small_sum", blocks[0])
    full_shapes = [(2,) + tuple(nat[0][n].shape) for n in order] + [(1,)]
    summed = dict(zip(order + ("loss",), _unpack(total, full_shapes)))
    for n in REPLICATED_SMALL:
        grads[n] = summed[n]
    for n in SHARDED_SMALL:
        grads[n] = lax.dynamic_slice_in_dim(summed[n], dev * 128, 128, axis=2)
    wp = _pack([w[n] for n in order])
    gp = _pack([grads[n] for n in order])
    mp = _pack([mom[n] for n in order])
    vp = _pack([var[n] for n in order])
    dpk, mpk, vpk = _adamw_small("adamw_small", wp, gp, mp, vp)
    local_shapes = [tuple(w[n].shape) for n in order]
    for dst, packed in ((delta, dpk), (new_m, mpk), (new_v, vpk)):
        dst.update(zip(order, _unpack(packed, local_shapes)))

    loss = summed["loss"][0]
    return (loss, dx[None], *[grads[n] for n in WEIGHT_NAMES], *[delta[n] for n in WEIGHT_NAMES],
            *[new_m[n] for n in WEIGHT_NAMES], *[new_v[n] for n in WEIGHT_NAMES])
```

```python
import functools

import jax
import jax.numpy as jnp
from jax import lax
from jax.experimental import pallas as pl
from jax.experimental.pallas import tpu as pltpu

f32 = jnp.float32
bf16 = jnp.bfloat16
S = jax.ShapeDtypeStruct
MESH = pl.DeviceIdType.MESH

D = 1024
NH = 8
HD = 128
NDEV = 8
GU = 704
NIN = 11272
INB = 1409
GATE_OFF = 8192
F_OFF = 11264
NZ = 11776
RMS_EPS = 1e-6
LN_EPS = 1e-5
ATT_SCALE = HD ** -0.5
NEG = -1e30
INV_SQRT2 = 0.7071067811865476
INV_SQRT2PI = 0.3989422804014327

ADAM_LR = 0.001
ADAM_B1 = 0.9
ADAM_B2 = 0.999
ADAM_EPS = 1e-08
ADAM_WD = 0.01
ADAM_STEP = 10

TT = 512
VMEM_LIMIT = 56 * 1024 * 1024


def _cp(*sem):
    return pltpu.CompilerParams(dimension_semantics=sem, vmem_limit_bytes=VMEM_LIMIT)


def _bs(shape, fn):
    return pl.BlockSpec(shape, fn)


NN = (((1,), (0,)), ((), ()))
NT = (((1,), (1,)), ((), ()))
TN = (((0,), (0,)), ((), ()))


def _mm(name, a, b, *, grid, a_spec, b_spec, out_shape, out_spec, dims, acc_shape, res=None, res_spec=None,
        alpha=1.0, alias=None, split_rows=None):
    nk = grid[2]

    def body(*refs):
        a_ref, b_ref = refs[0], refs[1]
        pos = 2
        res_ref = None
        if res is not None:
            res_ref = refs[pos]
            pos += 1
        if alias is not None:
            pos += 1
        o_ref = refs[pos]
        acc_ref = refs[pos + 1] if nk > 1 else None
        part = lax.dot_general(a_ref[...].astype(bf16), b_ref[...].astype(bf16), dims, preferred_element_type=f32)

        def finish(acc):
            if alpha != 1.0:
                acc = alpha * acc
            if res_ref is not None:
                acc = res_ref[...] + acc
            if split_rows is None:
                o_ref[...] = acc.astype(o_ref.dtype)
            else:
                o_ref[0] = acc[:split_rows].astype(o_ref.dtype)
                o_ref[1] = acc[split_rows:].astype(o_ref.dtype)

        if nk == 1:
            finish(part)
        else:
            k = pl.program_id(2)

            @pl.when(k == 0)
            def _():
                acc_ref[...] = part

            @pl.when(k > 0)
            def _():
                acc_ref[...] += part

            @pl.when(k == nk - 1)
            def _():
                finish(acc_ref[...])

    operands = [a, b]
    in_specs = [a_spec, b_spec]
    if res is not None:
        operands.append(res)
        in_specs.append(res_spec)
    aliases = {}
    if alias is not None:
        aliases = {len(operands): 0}
        operands.append(alias)
        in_specs.append(pl.BlockSpec(memory_space=pl.ANY))
    return pl.pallas_call(
        body, name=name, grid=grid, in_specs=in_specs, out_specs=out_spec, out_shape=out_shape,
        scratch_shapes=[pltpu.VMEM(acc_shape, f32)] if nk > 1 else [],
        input_output_aliases=aliases,
        compiler_params=_cp("parallel", "parallel", "arbitrary"),
    )(*operands)


def _tile(n, t):
    return t if n % t == 0 and n >= t else n


DZ_TILE = 512


def _dz_matmul(name, pieces, other, weight_grad):
    T = pieces[0].shape[0]
    counts = [p.shape[1] // DZ_TILE for p in pieces]
    starts = [sum(counts[:i]) for i in range(len(counts))]
    steps = sum(counts)
    npc = len(pieces)

    def body(*refs):
        prefs, o_ref, rest = refs[:npc], refs[npc], refs[npc + 1:]
        k = pl.program_id(0)
        if not weight_grad:
            out_ref, acc_ref = rest

            @pl.when(k == 0)
            def _():
                acc_ref[...] = jnp.zeros_like(acc_ref)

        for p_ref, s, c in zip(prefs, starts, counts):
            @pl.when((k >= s) & (k < s + c))
            def _():
                if weight_grad:
                    rest[0][...] = lax.dot_general(o_ref[...], p_ref[...], TN, preferred_element_type=f32).astype(bf16)
                else:
                    acc_ref[...] += lax.dot_general(p_ref[...], o_ref[...], NT, preferred_element_type=f32)

        if not weight_grad:
            @pl.when(k == steps - 1)
            def _():
                out_ref[...] = acc_ref[...]

    piece_specs = [_bs((T, DZ_TILE), functools.partial(lambda k, s, c: (0, jnp.clip(k - s, 0, c - 1)), s=s, c=c))
                   for s, c in zip(starts, counts)]
    if weight_grad:
        other_spec, out_spec, out_shape, scratch = _bs((T, D), lambda k: (0, 0)), _bs((D, DZ_TILE), lambda k: (0, k)), S((D, NZ), bf16), []
    else:
        other_spec, out_spec, out_shape = _bs((D, DZ_TILE), lambda k: (0, k)), _bs((T, D), lambda k: (0, 0)), S((T, D), f32)
        scratch = [pltpu.VMEM((T, D), f32)]
    return pl.pallas_call(body, name=name, grid=(steps,), in_specs=piece_specs + [other_spec], out_specs=out_spec,
                          out_shape=out_shape, scratch_shapes=scratch, compiler_params=_cp("arbitrary"))(*pieces, other)


def _row(cb=0, w=D):
    return _bs((TT, w), lambda i: (i, cb))


def _vec(rows=1, w=D):
    return _bs((rows, w), lambda i: (0, 0))


def _acc_store(i, ref, val):
    @pl.when(i == 0)
    def _():
        ref[...] = val

    @pl.when(i > 0)
    def _():
        ref[...] += val


def _rms_fwd(name, x, g):
    T = x.shape[0]

    def body(x_ref, g_ref, o_ref):
        xv = x_ref[...]
        r = lax.rsqrt(jnp.mean(xv * xv, axis=-1, keepdims=True) + RMS_EPS)
        o_ref[...] = (xv * r * g_ref[...]).astype(bf16)

    return pl.pallas_call(body, name=name, grid=(T // TT,), in_specs=[_row(), _vec()], out_specs=_row(),
                          out_shape=S((T, D), bf16), compiler_params=_cp("parallel"))(x, g)


def _rms_bwd(name, dh, x, g, dres):
    T = x.shape[0]

    def body(dh_ref, x_ref, g_ref, dres_ref, dx_ref, dg_ref):
        i = pl.program_id(0)
        xv = x_ref[...]
        r = lax.rsqrt(jnp.mean(xv * xv, axis=-1, keepdims=True) + RMS_EPS)
        xhat = xv * r
        dh_v = dh_ref[...]
        dyg = dh_v * g_ref[...]
        m = jnp.mean(dyg * xhat, axis=-1, keepdims=True)
        dx_ref[...] = dres_ref[...] + r * (dyg - xhat * m)
        _acc_store(i, dg_ref, jnp.sum(dh_v * xhat, axis=0, keepdims=True))

    return pl.pallas_call(body, name=name, grid=(T // TT,), in_specs=[_row(), _row(), _vec(), _row()],
                          out_specs=[_row(), _vec()], out_shape=[S((T, D), f32), S((1, D), f32)],
                          compiler_params=_cp("arbitrary"))(dh, x, g, dres)


def _sigmoid(x):
    return 1.0 / (1.0 + jnp.exp(-x))


def _swiglu_fwd(name, h, wgu, l):
    T = h.shape[0]

    def body(h_ref, wg_ref, wu_ref, gu_ref, a_ref):
        hv = h_ref[...]
        g = lax.dot_general(hv, wg_ref[...], NT, preferred_element_type=f32)
        u = lax.dot_general(hv, wu_ref[...], NT, preferred_element_type=f32)
        gu_ref[0] = g
        gu_ref[1] = u
        a_ref[...] = (g * _sigmoid(g) * u).astype(bf16)

    return pl.pallas_call(
        body, name=name, grid=(4,),
        in_specs=[_bs((T, D), lambda j: (0, 0)), _bs((None, None, GU, D), lambda j: (l, j, 0, 0)),
                  _bs((None, None, GU, D), lambda j: (l, j + 4, 0, 0))],
        out_specs=[_bs((2, None, T, GU), lambda j: (0, j, 0, 0)), _bs((None, T, GU), lambda j: (j, 0, 0))],
        out_shape=[S((2, 4, T, GU), f32), S((4, T, GU), bf16)], compiler_params=_cp("parallel"))(h, wgu, wgu)


def _swiglu_bwd(name, dxo, wd, gu, l):
    T = dxo.shape[0]
    tm = _tile(T, 1024)

    def body(dx_ref, wd_ref, g_ref, u_ref, o_ref):
        da = 0.5 * lax.dot_general(dx_ref[...].astype(bf16), wd_ref[...], NT, preferred_element_type=f32)
        g = g_ref[...]
        sg = _sigmoid(g)
        o_ref[0] = (da * u_ref[...] * (sg + g * sg * (1.0 - sg))).astype(bf16)
        o_ref[1] = (da * g * sg).astype(bf16)

    return pl.pallas_call(
        body, name=name, grid=(T // tm, 4),
        in_specs=[_bs((tm, D), lambda i, j: (i, 0)), _bs((None, None, GU, D), lambda i, j: (l, j, 0, 0)),
                  _bs((None, None, tm, GU), lambda i, j: (0, j, i, 0)), _bs((None, None, tm, GU), lambda i, j: (1, j, i, 0))],
        out_specs=_bs((2, None, tm, GU), lambda i, j: (0, j, i, 0)), out_shape=S((2, 4, T, GU), bf16),
        compiler_params=_cp("parallel", "parallel"))(dxo, wd, gu, gu)


def _loss(name, y, tgt):
    T = y.shape[0]

    def body(y_ref, t_ref, l_ref, dy_ref):
        i = pl.program_id(0)
        e = y_ref[...] - t_ref[...]
        dy_ref[...] = e * (1.0 / D)
        s = 0.5 * jnp.sum(jnp.mean(e * e, axis=-1, keepdims=True))
        _acc_store(i, l_ref, jnp.broadcast_to(s, (1, 128)))

    return pl.pallas_call(body, name=name, grid=(T // TT,), in_specs=[_row(), _row()],
                          out_specs=[_vec(1, 128), _row()], out_shape=[S((1, 128), f32), S((T, D), f32)],
                          compiler_params=_cp("arbitrary"))(y, tgt)


def _prev8(T, cb):
    return _bs((8, D), lambda i: (jnp.maximum(i * (TT // 8) - 1, 0), cb))


def _next8(T, cb):
    return _bs((8, D), lambda i: (jnp.minimum((i + 1) * (TT // 8), T // 8 - 1), cb))


def _conv_taps(i, ac_ref, ax_ref, pc_ref, px_ref):
    ca = ac_ref[...] * ax_ref[...]
    keep = (i > 0).astype(f32)
    p1 = pc_ref[7:8, :] * px_ref[7:8, :] * keep
    p2 = pc_ref[6:7, :] * px_ref[6:7, :] * keep
    row = lax.broadcasted_iota(jnp.int32, ca.shape, 0)
    s1 = jnp.where(row == 0, p1, pltpu.roll(ca, 1, 0))
    s2 = jnp.where(row == 0, p2, jnp.where(row == 1, p1, pltpu.roll(ca, 2, 0)))
    return ca, s1, s2


def _conv_fwd(name, z, cw):
    T = z.shape[0]

    def body(ab_ref, ac_ref, ax_ref, pc_ref, px_ref, w_ref, o_ref):
        i = pl.program_id(0)
        ca, s1, s2 = _conv_taps(i, ac_ref, ax_ref, pc_ref, px_ref)
        cv = w_ref[0:1, :] * s2 + w_ref[1:2, :] * s1 + w_ref[2:3, :] * ca
        o_ref[...] = (ab_ref[...] * cv).astype(bf16)

    return pl.pallas_call(
        body, name=name, grid=(T // TT,),
        in_specs=[_row(0), _row(1), _row(2), _prev8(T, 1), _prev8(T, 2), _vec(3)],
        out_specs=_row(), out_shape=S((T, D), bf16), compiler_params=_cp("parallel"))(z, z, z, z, z, cw)


def _conv_bwd(name, dya, z, cw):
    T = z.shape[0]
    n = T // TT

    def body(dya_ref, ab_ref, ac_ref, ax_ref, pc_ref, px_ref, ndya_ref, nab_ref, w_ref, dz_ref, dw_ref):
        i = pl.program_id(0)
        ca, s1, s2 = _conv_taps(i, ac_ref, ax_ref, pc_ref, px_ref)
        w0, w1, w2 = w_ref[0:1, :], w_ref[1:2, :], w_ref[2:3, :]
        cv = w0 * s2 + w1 * s1 + w2 * ca
        dya_v = dya_ref[...]
        ab = ab_ref[...]
        dcv = dya_v * ab
        keep = (i < n - 1).astype(f32)
        n1 = ndya_ref[0:1, :] * nab_ref[0:1, :] * keep
        n2 = ndya_ref[1:2, :] * nab_ref[1:2, :] * keep
        row = lax.broadcasted_iota(jnp.int32, dcv.shape, 0)
        f1 = jnp.where(row == TT - 1, n1, pltpu.roll(dcv, TT - 1, 0))
        f2 = jnp.where(row == TT - 1, n2, jnp.where(row == TT - 2, n1, pltpu.roll(dcv, TT - 2, 0)))
        dca = w2 * dcv + w1 * f1 + w0 * f2
        dz_ref[:, 0:D] = (dya_v * cv).astype(bf16)
        dz_ref[:, D:2 * D] = (dca * ax_ref[...]).astype(bf16)
        dz_ref[:, 2 * D:3 * D] = (dca * ac_ref[...]).astype(bf16)
        dw = jnp.concatenate([jnp.sum(dcv * s2, axis=0, keepdims=True), jnp.sum(dcv * s1, axis=0, keepdims=True),
                              jnp.sum(dcv * ca, axis=0, keepdims=True)], axis=0)
        _acc_store(i, dw_ref, dw)

    return pl.pallas_call(
        body, name=name, grid=(n,),
        in_specs=[_row(), _row(0), _row(1), _row(2), _prev8(T, 1), _prev8(T, 2), _next8(T, 0), _next8(T, 0), _vec(3)],
        out_specs=[_row(0, 3 * D), _vec(3)], out_shape=[S((T, 3 * D), bf16), S((3, D), f32)],
        compiler_params=_cp("arbitrary"))(dya, z, z, z, z, z, dya, z, cw)


def _gelu(x):
    return 0.5 * x * (1.0 + lax.erf(x * INV_SQRT2))


def _gelu_grad(x):
    return 0.5 * (1.0 + lax.erf(x * INV_SQRT2)) + x * jnp.exp(-0.5 * x * x) * INV_SQRT2PI


def _ln_stats(vv):
    mu = jnp.mean(vv, axis=-1, keepdims=True)
    xc = vv - mu
    rstd = lax.rsqrt(jnp.mean(xc * xc, axis=-1, keepdims=True) + LN_EPS)
    return xc * rstd, rstd


def _tril_w(w_ref, g):
    r = lax.broadcasted_iota(jnp.int32, (HD, HD), 0)
    c = lax.broadcasted_iota(jnp.int32, (HD, HD), 1)
    return jnp.where(c <= r, w_ref[g], 0.0).astype(bf16)


def _sgu_fwd(name, z, ln_g, ln_b, w_s, bmap):
    T = z.shape[0]

    def body(su_ref, sv_ref, lg_ref, lb_ref, w_ref, bm_ref, o_ref, vn_ref):
        xhat, _ = _ln_stats(_gelu(sv_ref[...]))
        vn_ref[...] = (xhat * lg_ref[...] + lb_ref[...]).astype(bf16)
        for g in range(NH):
            w = _tril_w(w_ref, g)
            cs = slice(g * HD, (g + 1) * HD)
            for c in range(TT // HD):
                rs = slice(c * HD, (c + 1) * HD)
                s = jnp.dot(w, vn_ref[rs, cs], preferred_element_type=f32) + bm_ref[:, cs]
                o_ref[rs, cs] = (_gelu(su_ref[rs, cs]) * s).astype(bf16)

    return pl.pallas_call(
        body, name=name, grid=(T // TT,),
        in_specs=[_row(3), _row(4), _vec(), _vec(), _bs((NH, HD, HD), lambda i: (0, 0, 0)), _vec(HD)],
        out_specs=_row(), out_shape=S((T, D), bf16), scratch_shapes=[pltpu.VMEM((TT, D), bf16)],
        compiler_params=_cp("parallel"))(z, z, ln_g, ln_b, w_s, bmap)


def _sgu_bwd(name, dyb, z, ln_g, ln_b, w_s, bmap):
    T = z.shape[0]

    def body(dyb_ref, su_ref, sv_ref, lg_ref, lb_ref, w_ref, bm_ref, dz_ref, dlg_ref, dlb_ref, dw_ref, db_ref,
             vn_ref, du_ref, dvn_ref):
        i = pl.program_id(0)
        sv = sv_ref[...]
        xhat, rstd = _ln_stats(_gelu(sv))
        vn_ref[...] = (xhat * lg_ref[...] + lb_ref[...]).astype(bf16)
        r = lax.broadcasted_iota(jnp.int32, (HD, HD), 0)
        cc = lax.broadcasted_iota(jnp.int32, (HD, HD), 1)
        for g in range(NH):
            w = _tril_w(w_ref, g)
            cs = slice(g * HD, (g + 1) * HD)
            dw = jnp.zeros((HD, HD), f32)
            db = jnp.zeros((HD, 1), f32)
            for c in range(TT // HD):
                rs = slice(c * HD, (c + 1) * HD)
                vnb = vn_ref[rs, cs]
                s = jnp.dot(w, vnb, preferred_element_type=f32) + bm_ref[:, cs]
                dy = dyb_ref[rs, cs]
                du_ref[rs, cs] = dy * s
                ds = dy * _gelu(su_ref[rs, cs])
                ds16 = ds.astype(bf16)
                dvn_ref[rs, cs] = lax.dot_general(w, ds16, TN, preferred_element_type=f32)
                dw = dw + lax.dot_general(ds16, vnb, NT, preferred_element_type=f32)
                db = db + jnp.sum(ds, axis=1, keepdims=True)
            dw = jnp.where(cc <= r, dw, 0.0)

            @pl.when(i == 0)
            def _():
                dw_ref[g] = dw
                db_ref[:, g:g + 1] = db

            @pl.when(i > 0)
            def _():
                dw_ref[g] += dw
                db_ref[:, g:g + 1] += db

        dvn = dvn_ref[...]
        dxh = dvn * lg_ref[...]
        m1 = jnp.mean(dxh, axis=-1, keepdims=True)
        m2 = jnp.mean(dxh * xhat, axis=-1, keepdims=True)
        dvv = rstd * (dxh - m1 - xhat * m2)
        dz_ref[:, 0:D] = (du_ref[...] * _gelu_grad(su_ref[...])).astype(bf16)
        dz_ref[:, D:2 * D] = (dvv * _gelu_grad(sv)).astype(bf16)
        _acc_store(i, dlg_ref, jnp.sum(dvn * xhat, axis=0, keepdims=True))
        _acc_store(i, dlb_ref, jnp.sum(dvn, axis=0, keepdims=True))

    return pl.pallas_call(
        body, name=name, grid=(T // TT,),
        in_specs=[_row(), _row(3), _row(4), _vec(), _vec(), _bs((NH, HD, HD), lambda i: (0, 0, 0)), _vec(HD)],
        out_specs=[_row(0, 2 * D), _vec(), _vec(), _bs((NH, HD, HD), lambda i: (0, 0, 0)), _bs((HD, NH), lambda i: (0, 0))],
        out_shape=[S((T, 2 * D), bf16), S((1, D), f32), S((1, D), f32), S((NH, HD, HD), f32), S((HD, NH), f32)],
        scratch_shapes=[pltpu.VMEM((TT, D), bf16), pltpu.VMEM((TT, D), f32), pltpu.VMEM((TT, D), f32)],
        compiler_params=_cp("arbitrary"))(dyb, z, z, ln_g, ln_b, w_s, bmap)


def _qk_fwd(name, z, qg, kg, bf):
    T = z.shape[0]

    def body(q_ref, k_ref, v_ref, zf_ref, qg_ref, kg_ref, bf_ref, qn_ref, kn_ref, vb_ref, lf_ref):
        for h in range(NH):
            cs = slice(h * HD, (h + 1) * HD)
            for src, gain, dst in ((q_ref, qg_ref, qn_ref), (k_ref, kg_ref, kn_ref)):
                xv = src[:, cs]
                r = lax.rsqrt(jnp.mean(xv * xv, axis=-1, keepdims=True) + RMS_EPS)
                dst[:, cs] = (xv * r * gain[:, cs]).astype(bf16)
        vb_ref[...] = v_ref[...].astype(bf16)
        xf = zf_ref[...] + bf_ref[...]
        lf_ref[...] = jnp.minimum(xf, 0.0) - jnp.log1p(jnp.exp(-jnp.abs(xf)))

    return pl.pallas_call(
        body, name=name, grid=(T // TT,),
        in_specs=[_row(5), _row(6), _row(7), _bs((TT, 128), lambda i: (i, F_OFF // 128)), _vec(), _vec(), _vec(1, 128)],
        out_specs=[_row(), _row(), _row(), _bs((TT, 128), lambda i: (i, 0))],
        out_shape=[S((T, D), bf16), S((T, D), bf16), S((T, D), bf16), S((T, 128), f32)],
        compiler_params=_cp("parallel"))(z, z, z, z, qg, kg, bf)


def _cum_fwd(name, logf):
    T = logf.shape[0]

    def body(lf_ref, ccol_ref, crow_ref, c_ref):
        c = lf_ref[...]
        row = lax.broadcasted_iota(jnp.int32, c.shape, 0)
        s = 1
        while s < T:
            c = c + jnp.where(row >= s, pltpu.roll(c, s, 0), 0.0)
            s *= 2
        c_ref[...] = c
        crow_ref[...] = c.T[0:NH, :]
        for h in range(NH):
            ccol_ref[h] = jnp.broadcast_to(c_ref[:, h:h + 1], (T, 128))

    return pl.pallas_call(body, name=name, out_shape=[S((NH, T, 128), f32), S((NH, T), f32)],
                          scratch_shapes=[pltpu.VMEM((T, 128), f32)],
                          compiler_params=pltpu.CompilerParams(vmem_limit_bytes=VMEM_LIMIT))(logf)


ATT_TILE = 1024


def _fold(x, op=jnp.add):
    acc = x[:, 0:128]
    for t in range(1, x.shape[1] // 128):
        acc = op(acc, x[:, t * 128:(t + 1) * 128])
    return acc


def _to_row(col):
    return jnp.broadcast_to(col, (col.shape[0], 128)).T[0:1, :]


def _causal(t, keys_down=False):
    r = lax.broadcasted_iota(jnp.int32, (t, t), 0)
    c = lax.broadcasted_iota(jnp.int32, (t, t), 1)
    return r <= c if keys_down else c <= r


def _attn_fwd(name, qn, kn, vb, ccol, crow3):
    T = qn.shape[0]
    tq = _tile(T, ATT_TILE)
    nq = T // tq

    def body(q_ref, k_ref, v_ref, cc_ref, cr_ref, o_ref, lse_ref, lser_ref, s_ref):
        qi = pl.program_id(1)
        q = q_ref[...]
        cq = cc_ref[:, 0:1]

        def logits(off):
            s = lax.dot_general(q, k_ref[pl.ds(off, tq), :], NT, preferred_element_type=f32) * ATT_SCALE
            return s + cq - cr_ref[:, pl.ds(off, tq)]

        def below(j, mvec):
            off = pl.multiple_of(j * tq, tq)
            s = logits(off)
            s_ref[:, pl.ds(off, tq)] = s
            return jnp.maximum(mvec, _fold(s, jnp.maximum))

        mvec = lax.fori_loop(0, qi, below, jnp.full((tq, 128), NEG, f32))
        off = pl.multiple_of(qi * tq, tq)
        s = jnp.where(_causal(tq), logits(off), NEG)
        s_ref[:, pl.ds(off, tq)] = s
        m = jnp.max(jnp.maximum(mvec, _fold(s, jnp.maximum)), axis=1, keepdims=True)

        def weigh(j, carry):
            lvec, acc = carry
            off = pl.multiple_of(j * tq, tq)
            p = jnp.exp(s_ref[:, pl.ds(off, tq)] - m)
            acc = acc + jnp.dot(p.astype(bf16), v_ref[pl.ds(off, tq), :], preferred_element_type=f32)
            return lvec + _fold(p), acc

        lvec, acc = lax.fori_loop(0, qi + 1, weigh, (jnp.zeros((tq, 128), f32), jnp.zeros((tq, HD), f32)))
        l = jnp.sum(lvec, axis=1, keepdims=True)
        o_ref[...] = acc / l
        lse = m + jnp.log(l)
        lse_ref[...] = jnp.broadcast_to(lse, (tq, 128))
        lser_ref[...] = _to_row(lse)

    return pl.pallas_call(
        body, name=name, grid=(NH, nq),
        in_specs=[_bs((tq, HD), lambda h, i: (i, h)), _bs((T, HD), lambda h, i: (0, h)), _bs((T, HD), lambda h, i: (0, h)),
                  _bs((None, tq, 128), lambda h, i: (h, i, 0)), _bs((None, 1, T), lambda h, i: (h, 0, 0))],
        out_specs=[_bs((tq, HD), lambda h, i: (i, h)), _bs((None, tq, 128), lambda h, i: (h, i, 0)),
                   _bs((None, 1, tq), lambda h, i: (h, 0, i))],
        out_shape=[S((T, D), f32), S((NH, T, 128), f32), S((NH, 1, T), f32)],
        scratch_shapes=[pltpu.VMEM((tq, T), f32)],
        compiler_params=_cp("parallel", "parallel"))(qn, kn, vb, ccol, crow3)


def _attn_dq(name, qn, kn, vb, do, lse, ccol, crow3):
    T = qn.shape[0]
    tq = _tile(T, ATT_TILE)
    nq = T // tq

    def body(q_ref, k_ref, v_ref, do_ref, lse_ref, cc_ref, cr_ref, dq_ref, dlr_ref, p_ref, dp_ref):
        qi = pl.program_id(1)
        q = q_ref[...]
        do16 = do_ref[...].astype(bf16)
        base = cc_ref[:, 0:1] - lse_ref[:, 0:1]

        def probs(off):
            s = lax.dot_general(q, k_ref[pl.ds(off, tq), :], NT, preferred_element_type=f32) * ATT_SCALE
            return jnp.exp(s + base - cr_ref[:, pl.ds(off, tq)])

        def keep(off, p, dvec):
            dp = lax.dot_general(do16, v_ref[pl.ds(off, tq), :], NT, preferred_element_type=f32)
            p_ref[:, pl.ds(off, tq)] = p
            dp_ref[:, pl.ds(off, tq)] = dp
            return dvec + _fold(p * dp)

        def below(j, dvec):
            off = pl.multiple_of(j * tq, tq)
            return keep(off, probs(off), dvec)

        dvec = lax.fori_loop(0, qi, below, jnp.zeros((tq, 128), f32))
        off = pl.multiple_of(qi * tq, tq)
        dvec = keep(off, jnp.where(_causal(tq), probs(off), 0.0), dvec)
        delta = jnp.sum(dvec, axis=1, keepdims=True)

        def grad(j, acc):
            off = pl.multiple_of(j * tq, tq)
            ds = p_ref[:, pl.ds(off, tq)] * (dp_ref[:, pl.ds(off, tq)] - delta)
            return acc + jnp.dot(ds.astype(bf16), k_ref[pl.ds(off, tq), :], preferred_element_type=f32)

        dq_ref[...] = lax.fori_loop(0, qi + 1, grad, jnp.zeros((tq, HD), f32)) * ATT_SCALE
        dlr_ref[...] = _to_row(delta)

    qb = lambda h, i: (i, h)
    full = lambda h, i: (0, h)
    col = lambda h, i: (h, i, 0)
    return pl.pallas_call(
        body, name=name, grid=(NH, nq),
        in_specs=[_bs((tq, HD), qb), _bs((T, HD), full), _bs((T, HD), full), _bs((tq, HD), qb),
                  _bs((None, tq, 128), col), _bs((None, tq, 128), col), _bs((None, 1, T), lambda h, i: (h, 0, 0))],
        out_specs=[_bs((tq, HD), qb), _bs((None, 1, tq), lambda h, i: (h, 0, i))],
        out_shape=[S((T, D), f32), S((NH, 1, T), f32)],
        scratch_shapes=[pltpu.VMEM((tq, T), f32), pltpu.VMEM((tq, T), f32)],
        compiler_params=_cp("parallel", "parallel"))(qn, kn, vb, do, lse, ccol, crow3)


def _attn_dkv(name, qn, kn, vb, do, lser3, dlr3, ccol, crow3):
    T = qn.shape[0]
    tk = _tile(T, ATT_TILE)
    nk = T // tk

    def body(q_ref, k_ref, v_ref, do_ref, lser_ref, dlr_ref, cc_ref, cr_ref, dk_ref, dv_ref, cs_ref):
        h = pl.program_id(0)
        kj = pl.program_id(1)

        @pl.when((h == 0) & (kj == 0))
        def _():
            cs_ref[...] = jnp.zeros_like(cs_ref)

        kb = k_ref[...]
        vv = v_ref[...]
        ckey = cc_ref[:, 0:1]

        def block(off, diagonal):
            rows = pl.ds(off, tk)
            qb = q_ref[rows, :]
            do16 = do_ref[rows, :].astype(bf16)
            st = lax.dot_general(kb, qb, NT, preferred_element_type=f32) * ATT_SCALE
            pt = jnp.exp(st + (cr_ref[:, rows] - lser_ref[:, rows]) - ckey)
            if diagonal:
                pt = jnp.where(_causal(tk, keys_down=True), pt, 0.0)
            dpt = lax.dot_general(vv, do16, NT, preferred_element_type=f32)
            dst = pt * (dpt - dlr_ref[:, rows])
            ddv = jnp.dot(pt.astype(bf16), do16, preferred_element_type=f32)
            ddk = jnp.dot(dst.astype(bf16), qb, preferred_element_type=f32)
            return ddk, ddv, _fold(dst)

        def above(i, carry):
            ddk, ddv, dcs = block(pl.multiple_of(i * tk, tk), False)
            return carry[0] + ddk, carry[1] + ddv, carry[2] + dcs

        off = pl.multiple_of(kj * tk, tk)
        dk, dv, cs = lax.fori_loop(kj + 1, nk, above, block(off, True))
        dk_ref[...] = dk * ATT_SCALE
        dv_ref[...] = dv
        lane = lax.broadcasted_iota(jnp.int32, (tk, 128), 1)
        cs_ref[pl.ds(off, tk), :] += jnp.where(lane == h, jnp.sum(cs, axis=1, keepdims=True), 0.0)

    full = lambda h, j: (0, h)
    blk = lambda h, j: (j, h)
    row = lambda h, j: (h, 0, 0)
    return pl.pallas_call(
        body, name=name, grid=(NH, nk),
        in_specs=[_bs((T, HD), full), _bs((tk, HD), blk), _bs((tk, HD), blk), _bs((T, HD), full), _bs((None, 1, T), row),
                  _bs((None, 1, T), row), _bs((None, tk, 128), lambda h, j: (h, j, 0)), _bs((None, 1, T), row)],
        out_specs=[_bs((tk, HD), blk), _bs((tk, HD), blk), _bs((T, 128), lambda h, j: (0, 0))],
        out_shape=[S((T, D), f32), S((T, D), f32), S((T, 128), f32)],
        compiler_params=_cp("arbitrary", "arbitrary"))(qn, kn, vb, do, lser3, dlr3, ccol, crow3)


def _forget_bwd(name, cs, z, bf):
    T = cs.shape[0]

    def body(cs_ref, zf_ref, bf_ref, dz_ref, db_ref):
        c = -cs_ref[...]
        row = lax.broadcasted_iota(jnp.int32, c.shape, 0)
        s = 1
        while s < T:
            c = c + jnp.where(row + s < T, pltpu.roll(c, T - s, 0), 0.0)
            s *= 2
        xf = zf_ref[...] + bf_ref[...]
        lane = lax.broadcasted_iota(jnp.int32, c.shape, 1)
        dxf = jnp.where(lane < NH, c / (1.0 + jnp.exp(xf)), 0.0)
        dz_ref[...] = jnp.zeros_like(dz_ref)
        dz_ref[:, 0:128] = dxf.astype(bf16)
        db_ref[...] = jnp.sum(dxf, axis=0, keepdims=True)

    return pl.pallas_call(
        body, name=name, grid=(1,),
        in_specs=[_bs((T, 128), lambda i: (0, 0)), _bs((T, 128), lambda i: (0, F_OFF // 128)), _vec(1, 128)],
        out_specs=[_bs((T, NZ - F_OFF), lambda i: (0, 0)), _vec(1, 128)],
        out_shape=[S((T, NZ - F_OFF), bf16), S((1, 128), f32)], compiler_params=_cp("arbitrary"))(cs, z, bf)


def _qk_bwd(name, dqn, dkn, dv, z, qg, kg):
    T = z.shape[0]

    def body(dq_ref, dk_ref, dv_ref, q_ref, k_ref, qg_ref, kg_ref, dz_ref, dqg_ref, dkg_ref, g_ref):
        i = pl.program_id(0)
        for n, (src, dsrc, gain, dgain) in enumerate(((q_ref, dq_ref, qg_ref, dqg_ref), (k_ref, dk_ref, kg_ref, dkg_ref))):
            for h in range(NH):
                cs = slice(h * HD, (h + 1) * HD)
                xv = src[:, cs]
                r = lax.rsqrt(jnp.mean(xv * xv, axis=-1, keepdims=True) + RMS_EPS)
                xhat = xv * r
                dy = dsrc[:, cs]
                dyg = dy * gain[:, cs]
                m = jnp.mean(dyg * xhat, axis=-1, keepdims=True)
                dz_ref[:, n * D + h * HD:n * D + (h + 1) * HD] = (r * (dyg - xhat * m)).astype(bf16)
                g_ref[:, cs] = jnp.sum(dy * xhat, axis=0, keepdims=True)
            _acc_store(i, dgain, g_ref[...])
        dz_ref[:, 2 * D:3 * D] = dv_ref[...].astype(bf16)

    return pl.pallas_call(
        body, name=name, grid=(T // TT,),
        in_specs=[_row(), _row(), _row(), _row(5), _row(6), _vec(), _vec()],
        out_specs=[_row(0, 3 * D), _vec(), _vec()], out_shape=[S((T, 3 * D), bf16), S((1, D), f32), S((1, D), f32)],
        scratch_shapes=[pltpu.VMEM((1, D), f32)], compiler_params=_cp("arbitrary"))(dqn, dkn, dv, z, z, qg, kg)


GB = GATE_OFF // D


def _merge_fwd(name, ya, yb, yc, z, bg):
    T = z.shape[0]

    def body(ya_ref, yb_ref, yc_ref, g0_ref, g1_ref, g2_ref, bg_ref, o_ref):
        acc = _sigmoid(g0_ref[...] + bg_ref[0:1, :]) * ya_ref[...]
        acc = acc + _sigmoid(g1_ref[...] + bg_ref[1:2, :]) * yb_ref[...]
        acc = acc + _sigmoid(g2_ref[...] + bg_ref[2:3, :]) * yc_ref[...]
        o_ref[...] = acc.astype(bf16)

    return pl.pallas_call(
        body, name=name, grid=(T // TT,),
        in_specs=[_row(), _row(), _row(), _row(GB), _row(GB + 1), _row(GB + 2), _vec(3)],
        out_specs=_row(), out_shape=S((T, D), bf16), compiler_params=_cp("parallel"))(ya, yb, yc, z, z, z, bg)


def _merge_bwd(name, dm, ya, yb, yc, z, bg):
    T = z.shape[0]

    def body(dm_ref, ya_ref, yb_ref, yc_ref, g0_ref, g1_ref, g2_ref, bg_ref, dya_ref, dyb_ref, dyc_ref, dz_ref, db_ref):
        i = pl.program_id(0)
        dm_v = dm_ref[...]
        dbs = []
        for n, (y_ref, g_ref, dy_ref) in enumerate(((ya_ref, g0_ref, dya_ref), (yb_ref, g1_ref, dyb_ref),
                                                    (yc_ref, g2_ref, dyc_ref))):
            gate = _sigmoid(g_ref[...] + bg_ref[n:n + 1, :])
            dy_ref[...] = (dm_v * gate).astype(bf16)
            dl = dm_v * y_ref[...] * gate * (1.0 - gate)
            dz_ref[:, n * D:(n + 1) * D] = dl.astype(bf16)
            dbs.append(jnp.sum(dl, axis=0, keepdims=True))
        _acc_store(i, db_ref, jnp.concatenate(dbs, axis=0))

    return pl.pallas_call(
        body, name=name, grid=(T // TT,),
        in_specs=[_row(), _row(), _row(), _row(), _row(GB), _row(GB + 1), _row(GB + 2), _vec(3)],
        out_specs=[_row(), _row(), _row(), _row(0, 3 * D), _vec(3)],
        out_shape=[S((T, D), bf16), S((T, D), bf16), S((T, D), bf16), S((T, 3 * D), bf16), S((3, D), f32)],
        compiler_params=_cp("arbitrary"))(dm, ya, yb, yc, z, z, z, bg)


SMALL_NAMES = ("ffn1_norm", "mix_norm", "b_forget", "b_gate", "conv_w", "sgu_ln_g", "sgu_ln_b", "sgu_w", "sgu_b",
               "q_norm_g", "k_norm_g", "ffn2_norm")


def _small_params(p):
    out = {n: p[n].reshape(1, D) for n in ("ffn1_norm", "mix_norm", "ffn2_norm", "sgu_ln_g", "sgu_ln_b", "q_norm_g", "k_norm_g")}
    out["b_forget"] = jnp.pad(p["b_forget"].reshape(1, NH), ((0, 0), (0, 128 - NH)))
    out["b_gate"] = p["b_gate"]
    out["conv_w"] = p["conv_w"]
    out["sgu_w"] = p["sgu_w"]
    out["bmap"] = jnp.repeat(p["sgu_b"].T, HD, axis=1)
    return out


def _small_grads_natural(sg):
    out = {n: sg[n].reshape(D) for n in ("ffn1_norm", "mix_norm", "ffn2_norm", "sgu_ln_g", "sgu_ln_b")}
    out["q_norm_g"] = sg["q_norm_g"].reshape(NH, HD)
    out["k_norm_g"] = sg["k_norm_g"].reshape(NH, HD)
    out["b_forget"] = sg["b_forget"][0, :NH]
    out["b_gate"] = sg["b_gate"]
    out["conv_w"] = sg["conv_w"]
    out["sgu_w"] = sg["sgu_w"]
    out["sgu_b"] = sg["sgu_b"]
    return out


SQ_TM = 1024


def _sq_fwd(name, a, wsq, l, n, res=None):
    T = a.shape[0]
    tm = _tile(T, SQ_TM)
    return _mm(name, a, wsq, grid=(T // tm, 1, 1), a_spec=_bs((tm, D), lambda i, j, k: (i, 0)),
               b_spec=_bs((None, None, D, D), lambda i, j, k: (l, n, 0, 0)),
               out_shape=S((T, D), f32), out_spec=_bs((tm, D), lambda i, j, k: (i, 0)), dims=NN, acc_shape=None,
               res=res, res_spec=_bs((tm, D), lambda i, j, k: (i, 0)))


def _sq_bwd_in(name, dy, wsq, l, n):
    T = dy.shape[0]
    tm = _tile(T, SQ_TM)
    return _mm(name, dy, wsq, grid=(T // tm, 1, 1), a_spec=_bs((tm, D), lambda i, j, k: (i, 0)),
               b_spec=_bs((None, None, D, D), lambda i, j, k: (l, n, 0, 0)),
               out_shape=S((T, D), f32), out_spec=_bs((tm, D), lambda i, j, k: (i, 0)), dims=NT, acc_shape=None)


def _sq_bwd_w(name, a, dy, gbuf, l):
    T = a.shape[0]
    return _mm(name, a, dy, grid=(NDEV // 2, 1, 1), a_spec=_bs((T, 256), lambda i, j, k: (0, i)),
               b_spec=_bs((T, D), lambda i, j, k: (0, 0)), out_shape=S(gbuf.shape, bf16),
               out_spec=_bs((2, None, None, 128, D), lambda i, j, k: (0, i, l, 0, 0)),
               dims=TN, acc_shape=None, alias=gbuf, split_rows=128)


def _ffn_fwd(tag, x, g, wgu, wd, l):
    T = x.shape[0]
    h = _rms_fwd(tag + "_rms", x, g)
    gu, a = _swiglu_fwd(tag + "_gu", h, wgu, l)
    tm = _tile(T, 1024)
    xo = _mm(tag + "_down", a, wd, grid=(T // tm, 1, 4), a_spec=_bs((None, tm, GU), lambda i, j, k: (k, i, 0)),
             b_spec=_bs((None, None, GU, D), lambda i, j, k: (l, k, 0, 0)), out_shape=S((T, D), f32),
             out_spec=_bs((tm, D), lambda i, j, k: (i, 0)), dims=NN, acc_shape=(tm, D), res=x,
             res_spec=_bs((tm, D), lambda i, j, k: (i, 0)), alpha=0.5)
    return xo, (h, gu, a)


def _ffn_bwd(tag, dxo, x, g, wgu, wd, l, saved, g_gu, g_d, hook):
    h, gu, a = saved
    T = x.shape[0]
    g_d = _mm(tag + "_dwd", a, dxo, grid=(4, 1, 1), a_spec=_bs((None, T, GU), lambda i, j, k: (i, 0, 0)),
              b_spec=_bs((T, D), lambda i, j, k: (0, 0)), out_shape=S(g_d.shape, bf16),
              out_spec=_bs((2, None, None, GU // 2, D), lambda i, j, k: (0, i, l, 0, 0)), dims=TN, acc_shape=None,
              alpha=0.5, alias=g_d, split_rows=GU // 2)
    dxo = hook(dxo, g_d)
    dgu = _swiglu_bwd(tag + "_dgu", dxo, wd, gu, l).reshape(NDEV, T, GU)
    dh = _mm(tag + "_dh", dgu, wgu, grid=(1, 1, NDEV), a_spec=_bs((None, T, GU), lambda i, j, k: (k, 0, 0)),
             b_spec=_bs((None, None, GU, D), lambda i, j, k: (l, k, 0, 0)), out_shape=S((T, D), f32),
             out_spec=_bs((T, D), lambda i, j, k: (0, 0)), dims=NN, acc_shape=(T, D))
    g_gu = _mm(tag + "_dwgu", dgu, h, grid=(NDEV, 1, 1), a_spec=_bs((None, T, GU), lambda i, j, k: (i, 0, 0)),
               b_spec=_bs((T, D), lambda i, j, k: (0, 0)), out_shape=S(g_gu.shape, bf16),
               out_spec=_bs((None, None, None, GU, D), lambda i, j, k: (i % 2, i // 2, l, 0, 0)), dims=TN,
               acc_shape=None, alias=g_gu)
    dx, dg = _rms_bwd(tag + "_drms", dh, x, g, dxo)
    return dx, dg, g_gu, g_d


def _mixer_fwd(tag, x, p, win, wsq, l):
    T = x.shape[0]
    h = _rms_fwd(tag + "_rms", x, p["mix_norm"])
    tn = 512
    z = _mm(tag + "_in", h, win, grid=(1, NZ // tn, 1), a_spec=_bs((T, D), lambda i, j, k: (0, 0)),
            b_spec=_bs((D, tn), lambda i, j, k: (0, j)), out_shape=S((T, NZ), f32),
            out_spec=_bs((T, tn), lambda i, j, k: (0, j)), dims=NN, acc_shape=None)
    ya_in = _conv_fwd(tag + "_conv", z, p["conv_w"])
    yb_in = _sgu_fwd(tag + "_sgu", z, p["sgu_ln_g"], p["sgu_ln_b"], p["sgu_w"], p["bmap"])
    qn, kn, vb, logf = _qk_fwd(tag + "_qk", z, p["q_norm_g"], p["k_norm_g"], p["b_forget"])
    ccol, crow = _cum_fwd(tag + "_cum", logf)
    crow3 = crow.reshape(NH, 1, T)
    o, lse, lser = _attn_fwd(tag + "_attn", qn, kn, vb, ccol, crow3)
    ya = _sq_fwd(tag + "_oconv", ya_in, wsq, l, 0)
    yb = _sq_fwd(tag + "_osgu", yb_in, wsq, l, 1)
    yc = _sq_fwd(tag + "_oattn", o, wsq, l, 2)
    merged = _merge_fwd(tag + "_merge", ya, yb, yc, z, p["b_gate"])
    xo = _sq_fwd(tag + "_o", merged, wsq, l, 3, res=x)
    return xo, (h, z, ya_in, yb_in, qn, kn, vb, ccol, crow3, o, lse, lser, ya, yb, yc, merged)


def _mixer_bwd(tag, dxo, x, p, win, wsq, l, saved, gsq, hook):
    h, z, ya_in, yb_in, qn, kn, vb, ccol, crow3, o, lse, lser, ya, yb, yc, merged = saved
    T = x.shape[0]
    sg = {}
    dm = _sq_bwd_in(tag + "_dmerged", dxo, wsq, l, 3)
    dxo = hook(dxo, dm)
    gsq[3] = _sq_bwd_w(tag + "_dwo", merged, dxo, gsq[3], l)
    dya, dyb, dyc, dz_g, sg["b_gate"] = _merge_bwd(tag + "_dmerge", dm, ya, yb, yc, z, p["b_gate"])
    d_ya_in = _sq_bwd_in(tag + "_dconv_in", dya, wsq, l, 0)
    gsq[0] = _sq_bwd_w(tag + "_dwoc", ya_in, dya, gsq[0], l)
    d_yb_in = _sq_bwd_in(tag + "_dsgu_in", dyb, wsq, l, 1)
    gsq[1] = _sq_bwd_w(tag + "_dwos", yb_in, dyb, gsq[1], l)
    d_o = _sq_bwd_in(tag + "_dattn_in", dyc, wsq, l, 2)
    gsq[2] = _sq_bwd_w(tag + "_dwoa", o, dyc, gsq[2], l)
    dz_c, sg["conv_w"] = _conv_bwd(tag + "_dconv", d_ya_in, z, p["conv_w"])
    dz_s, sg["sgu_ln_g"], sg["sgu_ln_b"], sg["sgu_w"], db_t = _sgu_bwd(
        tag + "_dsgu", d_yb_in, z, p["sgu_ln_g"], p["sgu_ln_b"], p["sgu_w"], p["bmap"])
    sg["sgu_b"] = db_t.T
    dqn, dlr = _attn_dq(tag + "_dattn_q", qn, kn, vb, d_o, lse, ccol, crow3)
    dkn, dv, cs = _attn_dkv(tag + "_dattn_kv", qn, kn, vb, d_o, lser, dlr, ccol, crow3)
    dz_f, sg["b_forget"] = _forget_bwd(tag + "_dforget", cs, z, p["b_forget"])
    dz_q, sg["q_norm_g"], sg["k_norm_g"] = _qk_bwd(tag + "_dqk", dqn, dkn, dv, z, p["q_norm_g"], p["k_norm_g"])
    dz = [dz_c, dz_s, dz_q, dz_g, dz_f]
    dh = _dz_matmul(tag + "_dh", dz, win, False)
    dwin = _dz_matmul(tag + "_dwin", dz, h, True)
    dx, sg["mix_norm"] = _rms_bwd(tag + "_drms", dh, x, p["mix_norm"], dxo)
    return dx, sg, dwin, gsq


ANY = pl.BlockSpec(memory_space=pl.ANY)
HBM = pl.BlockSpec(memory_space=pltpu.HBM)
SEM = pl.BlockSpec(memory_space=pltpu.SEMAPHORE)
EFFECT = pltpu.SideEffectType.DATAFLOW_SIDE_EFFECTING


def _place():
    return lax.axis_index("x"), lax.axis_index("y"), lax.axis_index("c")


NEAR = 4


def _others(x, y, c):
    return [(x, y, 1 - c), (1 - x, y, c), (x, 1 - y, c), (1 - x, 1 - y, c)]


def _gather_start(name, groups, carry=None):
    sizes = [len(g) for g in groups]
    srcs = [s for g in groups for s, _ in g]
    lands = [b for g in groups for _, b in g]
    n, ng = len(srcs), len(groups)
    held = srcs + lands + ([] if carry is None else [carry])
    nh = len(held)

    def body(*refs):
        src_refs, land_refs = refs[:n], refs[n:2 * n]
        send, recv = refs[nh:nh + ng], refs[nh + ng:nh + 2 * ng]
        x, y, c = _place()
        me = 4 * x + 2 * y + c
        u = 0
        for g, size in enumerate(sizes):
            for i in range(size):
                for k, peer in enumerate(_others(x, y, c)):
                    pltpu.make_async_remote_copy(src_ref=src_refs[u], dst_ref=land_refs[u].at[me],
                                                 send_sem=send[g].at[i * NEAR + k], recv_sem=recv[g].at[i * NEAR + k],
                                                 device_id=peer, device_id_type=MESH).start()
                u += 1

    sems = [pltpu.SemaphoreType.DMA((size * NEAR,)) for size in sizes]
    out = pl.pallas_call(
        body, name=name, in_specs=[HBM] * nh, out_specs=[SEM] * (2 * ng) + [HBM] * nh,
        out_shape=sems + sems + [pltpu.HBM(a.shape, a.dtype) for a in held],
        input_output_aliases={i: 2 * ng + i for i in range(nh)},
        compiler_params=pltpu.CompilerParams(has_side_effects=EFFECT),
    )(*[pltpu.with_memory_space_constraint(a, pltpu.HBM) for a in held])
    res, pos = [], 0
    for g, size in enumerate(sizes):
        res.append((out[g], out[ng + g], out[2 * ng + pos:2 * ng + pos + size], out[2 * ng + n + pos:2 * ng + n + pos + size]))
        pos += size
    return res if carry is None else (res, out[2 * ng + 2 * n])


def _gather_wait(name, started, after=None):
    send, recv, srcs, lands = started
    n = len(srcs)

    def body(*refs):
        src_refs, land_refs = refs[:n], refs[n:2 * n]
        send_ref, recv_ref = refs[2 * n], refs[2 * n + 1]
        x, y, c = _place()
        for i in range(n):
            for k, (px, py, pc) in enumerate(_others(x, y, c)):
                cp = pltpu.make_async_remote_copy(src_ref=src_refs[i], dst_ref=land_refs[i].at[4 * px + 2 * py + pc],
                                                  send_sem=send_ref.at[i * NEAR + k], recv_sem=recv_ref.at[i * NEAR + k],
                                                  device_id=(px, py, pc), device_id_type=MESH)
                cp.wait_send()
                cp.wait_recv()

    extra = [] if after is None else [after]
    out = pl.pallas_call(
        body, name=name, in_specs=[HBM] * (2 * n) + [SEM, SEM] + [ANY] * len(extra), out_specs=[HBM] * (2 * n),
        out_shape=[pltpu.HBM(a.shape, a.dtype) for a in list(srcs) + list(lands)],
        input_output_aliases={i: i for i in range(2 * n)},
        compiler_params=pltpu.CompilerParams(has_side_effects=EFFECT),
    )(*srcs, *lands, send, recv, *extra)
    return out[n:]


def _gather_forward(name, lands):
    n = len(lands)

    def body(*refs):
        have, full = refs[:n], refs[n:2 * n]
        send, recv = refs[2 * n], refs[2 * n + 1]
        x, y, c = _place()
        chips = [(1 - x, y), (x, 1 - y), (1 - x, 1 - y)]

        def copy(i, j, core):
            slot = 4 * chips[j][0] + 2 * chips[j][1] + core
            return pltpu.make_async_remote_copy(src_ref=have[i].at[slot], dst_ref=full[i].at[slot],
                                                send_sem=send.at[i * 3 + j], recv_sem=recv.at[i * 3 + j],
                                                device_id=(x, y, 1 - c), device_id_type=MESH)

        for i in range(n):
            for j in range(3):
                copy(i, j, c).start()
        for i in range(n):
            for j in range(3):
                copy(i, j, c).wait_send()
                copy(i, j, 1 - c).wait_recv()

    return pl.pallas_call(
        body, name=name, in_specs=[ANY] * n, out_specs=[ANY] * n, out_shape=[S(a.shape, a.dtype) for a in lands],
        input_output_aliases={i: i for i in range(n)},
        scratch_shapes=[pltpu.SemaphoreType.DMA((n * 3,)), pltpu.SemaphoreType.DMA((n * 3,))],
    )(*lands)


def _pair_copies(ins, outs, send, recv):
    x, y, c = _place()
    return [pltpu.make_async_remote_copy(src_ref=ins[u].at[1 - c], dst_ref=outs[u], send_sem=send.at[u],
                                         recv_sem=recv.at[u], device_id=(x, y, 1 - c), device_id_type=MESH)
            for u in range(len(ins))]


def _pair_start(name, gs, carry):
    n = len(gs)
    held = list(gs) + [lax.empty(g.shape[1:], g.dtype) for g in gs] + [carry]

    def body(*refs):
        for cp in _pair_copies(refs[:n], refs[n:2 * n], refs[2 * n + 1], refs[2 * n + 2]):
            cp.start()

    sems = [pltpu.SemaphoreType.DMA((n,))] * 2
    out = pl.pallas_call(
        body, name=name, in_specs=[HBM] * len(held), out_specs=[SEM, SEM] + [HBM] * len(held),
        out_shape=sems + [pltpu.HBM(a.shape, a.dtype) for a in held],
        input_output_aliases={i: 2 + i for i in range(len(held))},
        compiler_params=pltpu.CompilerParams(has_side_effects=EFFECT),
    )(*[pltpu.with_memory_space_constraint(a, pltpu.HBM) for a in held])
    return (out[0], out[1], out[2:2 + n], out[2 + n:2 + 2 * n]), out[2 + 2 * n]


def _pair_wait(name, started, after):
    send, recv, gs, lands = started
    n = len(gs)

    def body(*refs):
        for cp in _pair_copies(refs[:n], refs[n:2 * n], refs[2 * n], refs[2 * n + 1]):
            cp.wait_send()
            cp.wait_recv()

    out = pl.pallas_call(
        body, name=name, in_specs=[HBM] * (2 * n) + [SEM, SEM, ANY], out_specs=[HBM] * (2 * n),
        out_shape=[pltpu.HBM(a.shape, a.dtype) for a in list(gs) + list(lands)],
        input_output_aliases={i: i for i in range(2 * n)},
        compiler_params=pltpu.CompilerParams(has_side_effects=EFFECT),
    )(*gs, *lands, send, recv, after)
    return out[:n], out[n:]


def _row_tile(r, c):
    if c > D and r % 128 == 0:
        return 128
    return 256 if r % 256 == 0 else (GU // 2 if r % (GU // 2) == 0 else r)


def _pair_sum(name, core, g, r1):
    _, nq, nl, r, c = g.shape
    tr = r
    g4 = g.reshape(2, nq * nl, r, c)
    r3 = r1.reshape(nq * nl, r, c)

    def body(core_ref, g_ref, r_ref, o_ref):
        o_ref[...] = (g_ref[...].astype(f32) + r_ref[...].astype(f32)).astype(bf16)

    out = pl.pallas_call(
        body, name=name,
        grid_spec=pltpu.PrefetchScalarGridSpec(
            num_scalar_prefetch=1, grid=(nq * nl, r // tr),
            in_specs=[_bs((None, None, tr, c), lambda b, i, cr: (cr[0], b, i, 0)), _bs((None, tr, c), lambda b, i, cr: (b, i, 0))],
            out_specs=_bs((None, tr, c), lambda b, i, cr: (b, i, 0))),
        out_shape=S((nq * nl, r, c), bf16), compiler_params=_cp("parallel", "parallel"))(core, g4, r3)
    return out.reshape(nq, nl, r, c)


def _scatter_copies(ins, outs, send, recv):
    x, y, c = _place()
    chips = [(1 - x, y), (x, 1 - y), (1 - x, 1 - y)]
    return [pltpu.make_async_remote_copy(src_ref=ins[u].at[2 * chip[0] + chip[1]], dst_ref=outs[u].at[k],
                                         send_sem=send.at[u * 3 + k], recv_sem=recv.at[u * 3 + k],
                                         device_id=(*chip, c), device_id_type=MESH)
            for u in range(len(ins)) for k, chip in enumerate(chips)]


def _scatter_start(name, ss, carry):
    n = len(ss)
    lands = [lax.empty((3,) + s.shape[1:], s.dtype) for s in ss]
    held = list(ss) + lands + [carry]

    def body(*refs):
        for cp in _scatter_copies(refs[:n], refs[n:2 * n], refs[2 * n + 1], refs[2 * n + 2]):
            cp.start()

    sems = [pltpu.SemaphoreType.DMA((n * 3,))] * 2
    out = pl.pallas_call(
        body, name=name, in_specs=[HBM] * len(held), out_specs=[SEM, SEM] + [HBM] * len(held),
        out_shape=sems + [pltpu.HBM(a.shape, a.dtype) for a in held],
        input_output_aliases={i: 2 + i for i in range(len(held))},
        compiler_params=pltpu.CompilerParams(has_side_effects=EFFECT),
    )(*[pltpu.with_memory_space_constraint(a, pltpu.HBM) for a in held])
    return (out[0], out[1], out[2:2 + n], out[2 + n:2 + 2 * n]), out[2 + 2 * n]


def _scatter_wait(name, started, after):
    send, recv, srcs, lands = started
    n = len(srcs)

    def body(*refs):
        for cp in _scatter_copies(refs[:n], refs[n:2 * n], refs[2 * n], refs[2 * n + 1]):
            cp.wait_send()
            cp.wait_recv()

    out = pl.pallas_call(
        body, name=name, in_specs=[HBM] * (2 * n) + [SEM, SEM, ANY], out_specs=[HBM] * (2 * n),
        out_shape=[pltpu.HBM(a.shape, a.dtype) for a in list(srcs) + list(lands)],
        input_output_aliases={i: i for i in range(2 * n)},
        compiler_params=pltpu.CompilerParams(has_side_effects=EFFECT),
    )(*srcs, *lands, send, recv, after)
    return out[:n], out[n:]


def _sum_blocks(name, blocks):
    def body(b_ref, o_ref):
        acc = b_ref[0]
        for d in range(1, NDEV):
            acc = acc + b_ref[d]
        o_ref[...] = acc

    return pl.pallas_call(body, name=name, out_shape=S(blocks.shape[1:], f32),
                          compiler_params=pltpu.CompilerParams(vmem_limit_bytes=VMEM_LIMIT))(blocks)


def _adam_math(w, g, m, v):
    m = ADAM_B1 * m + (1.0 - ADAM_B1) * g
    v = ADAM_B2 * v + (1.0 - ADAM_B2) * (g * g)
    m_hat = m / (1.0 - ADAM_B1 ** ADAM_STEP)
    v_hat = v / (1.0 - ADAM_B2 ** ADAM_STEP)
    delta = -ADAM_LR * (m_hat / (jnp.sqrt(v_hat) + ADAM_EPS) + ADAM_WD * w)
    return delta, m, v


def _adamw(name, chip, w, m, v, parts):
    _, r, c = w.shape
    tr = _row_tile(r, c)

    def body(chip_ref, w_ref, m_ref, v_ref, *refs):
        sets, (g_ref, d_ref, mo_ref, vo_ref) = (refs[0:4], refs[4:8]), refs[8:]
        for l in range(2):
            @pl.when(pl.program_id(0) == l)
            def _():
                s_ref, r0_ref, r1_ref, r2_ref = sets[l]
                g = ((s_ref[...].astype(f32) + r0_ref[...].astype(f32)) + r1_ref[...].astype(f32)) + r2_ref[...].astype(f32)
                g_ref[...] = g
                d_ref[...], mo_ref[...], vo_ref[...] = _adam_math(w_ref[...], g, m_ref[...], v_ref[...])

    blk = _bs((None, tr, c), lambda l, i, cr: (l, i, 0))
    operands, specs = [], []
    for n in range(2):
        row = (lambda l, i: i * (1 - l)) if n == 0 else (lambda l, i: i * l)
        sums, r2 = parts[n]
        operands += [sums, r2, r2, r2]
        specs.append(_bs((None, None, tr, c), functools.partial(lambda l, i, cr, row: (cr[0], 0, row(l, i), 0), row=row)))
        specs += [_bs((None, None, tr, c), functools.partial(lambda l, i, cr, k, row: (k, 0, row(l, i), 0), k=k, row=row))
                  for k in range(3)]
    return pl.pallas_call(
        body, name=name,
        grid_spec=pltpu.PrefetchScalarGridSpec(num_scalar_prefetch=1, grid=(2, r // tr), in_specs=[blk, blk, blk] + specs,
                                               out_specs=[blk] * 4),
        out_shape=[S(w.shape, f32)] * 4, compiler_params=_cp("arbitrary", "arbitrary"),
    )(chip, w, m, v, *operands)


def _adamw_small(name, w, g, m, v):
    def body(w_ref, g_ref, m_ref, v_ref, d_ref, mo_ref, vo_ref):
        d_ref[...], mo_ref[...], vo_ref[...] = _adam_math(w_ref[...], g_ref[...], m_ref[...], v_ref[...])

    return pl.pallas_call(body, name=name, out_shape=[S(w.shape, f32)] * 3,
                          compiler_params=pltpu.CompilerParams(vmem_limit_bytes=VMEM_LIMIT))(w, g, m, v)


WEIGHT_NAMES = ("ffn1_norm", "ffn1_w_gu", "ffn1_w_down", "mix_norm", "w_in", "b_forget", "b_gate", "conv_w", "sgu_ln_g",
                "sgu_ln_b", "sgu_w", "sgu_b", "q_norm_g", "k_norm_g", "w_out_conv", "w_out_sgu", "w_out_attn", "w_o",
                "ffn2_norm", "ffn2_w_gu", "ffn2_w_down")
BIG = {"ffn1_w_gu": "gu1", "ffn2_w_gu": "gu2", "ffn1_w_down": "d1", "ffn2_w_down": "d2", "w_in": "in",
       "w_out_conv": "oc", "w_out_sgu": "os", "w_out_attn": "oa", "w_o": "o"}
BIG_KEYS = ("gu1", "gu2", "d1", "d2", "in", "oc", "os", "oa", "o")
REPLICATED_SMALL = ("ffn1_norm", "mix_norm", "b_forget", "sgu_ln_g", "sgu_ln_b", "sgu_w", "sgu_b", "q_norm_g",
                    "k_norm_g", "ffn2_norm")
SHARDED_SMALL = ("b_gate", "conv_w")
TRANSPOSED = ("gu1", "gu2")


def _packed_rows(shape):
    size = 1
    for s_ in shape:
        size *= s_
    return size, -(-size // 1024) * 8


def _pack(arrays):
    pieces = []
    for a in arrays:
        size, rows = _packed_rows(a.shape)
        pieces.append(jnp.pad(a.reshape(-1).astype(f32), (0, rows * 128 - size)).reshape(rows, 128))
    return jnp.concatenate(pieces, axis=0)


def _unpack(packed, shapes):
    out, pos = [], 0
    for shp in shapes:
        size, rows = _packed_rows(shp)
        out.append(packed[pos:pos + rows].reshape(-1)[:size].reshape(shp))
        pos += rows
    return out


def _natural_runs(a, b):
    runs = []
    while a < b:
        d = a // INB
        e = min(b, (d + 1) * INB)
        runs.append((d, a - d * INB, e - d * INB))
        a = e
    return runs


def _win_kernel_layout(wg):
    runs = _natural_runs(0, GATE_OFF) + _natural_runs(GATE_OFF + NH, NIN) + _natural_runs(GATE_OFF, GATE_OFF + NH)
    return jnp.concatenate([wg[d, :, a:b] for d, a, b in runs] + [jnp.zeros((D, NZ - NIN), wg.dtype)], axis=1)


def _kernel_column(n):
    return n if n < GATE_OFF else (F_OFF + n - GATE_OFF if n < GATE_OFF + NH else n - NH)


def _win_device_block(dw, d):
    cuts = sorted({d * INB, (d + 1) * INB} | {c for c in (GATE_OFF, GATE_OFF + NH) if d * INB < c < (d + 1) * INB})
    parts = [dw[:, _kernel_column(a):_kernel_column(a) + (b - a)] for a, b in zip(cuts[:-1], cuts[1:])]
    return parts[0] if len(parts) == 1 else jnp.concatenate(parts, axis=1)


def kernel(x, ffn1_norm, ffn1_w_gu, ffn1_w_down, mix_norm, w_in, b_forget, b_gate, conv_w, sgu_ln_g, sgu_ln_b, sgu_w, sgu_b, q_norm_g, k_norm_g, w_out_conv, w_out_sgu, w_out_attn, w_o, ffn2_norm, ffn2_w_gu, ffn2_w_down, loss_target, m_ffn1_norm, m_ffn1_w_gu, m_ffn1_w_down, m_mix_norm, m_w_in, m_b_forget, m_b_gate, m_conv_w, m_sgu_ln_g, m_sgu_ln_b, m_sgu_w, m_sgu_b, m_q_norm_g, m_k_norm_g, m_w_out_conv, m_w_out_sgu, m_w_out_attn, m_w_o, m_ffn2_norm, m_ffn2_w_gu, m_ffn2_w_down, v_ffn1_norm, v_ffn1_w_gu, v_ffn1_w_down, v_mix_norm, v_w_in, v_b_forget, v_b_gate, v_conv_w, v_sgu_ln_g, v_sgu_ln_b, v_sgu_w, v_sgu_b, v_q_norm_g, v_k_norm_g, v_w_out_conv, v_w_out_sgu, v_w_out_attn, v_w_o, v_ffn2_norm, v_ffn2_w_gu, v_ffn2_w_down):
    w = dict(zip(WEIGHT_NAMES, (ffn1_norm, ffn1_w_gu, ffn1_w_down, mix_norm, w_in, b_forget, b_gate, conv_w, sgu_ln_g,
                                sgu_ln_b, sgu_w, sgu_b, q_norm_g, k_norm_g, w_out_conv, w_out_sgu, w_out_attn, w_o,
                                ffn2_norm, ffn2_w_gu, ffn2_w_down)))
    mom = dict(zip(WEIGHT_NAMES, (m_ffn1_norm, m_ffn1_w_gu, m_ffn1_w_down, m_mix_norm, m_w_in, m_b_forget, m_b_gate,
                                  m_conv_w, m_sgu_ln_g, m_sgu_ln_b, m_sgu_w, m_sgu_b, m_q_norm_g, m_k_norm_g,
                                  m_w_out_conv, m_w_out_sgu, m_w_out_attn, m_w_o, m_ffn2_norm, m_ffn2_w_gu,
                                  m_ffn2_w_down)))
    var = dict(zip(WEIGHT_NAMES, (v_ffn1_norm, v_ffn1_w_gu, v_ffn1_w_down, v_mix_norm, v_w_in, v_b_forget, v_b_gate,
                                  v_conv_w, v_sgu_ln_g, v_sgu_ln_b, v_sgu_w, v_sgu_b, v_q_norm_g, v_k_norm_g,
                                  v_w_out_conv, v_w_out_sgu, v_w_out_attn, v_w_o, v_ffn2_norm, v_ffn2_w_gu,
                                  v_ffn2_w_down)))
    px, py, pc = _place()
    dev = 4 * px + 2 * py + pc
    chip = 2 * px + py

    big_names = [n for n in WEIGHT_NAMES if n in BIG]
    key_name = {BIG[n]: n for n in big_names}
    group_keys = (("gu1", "d1"), ("in", "oc", "os", "oa", "o", "small"), ("gu2", "d2"))

    def source(key, l):
        if key == "small":
            return jnp.concatenate([w["b_gate"][l], w["conv_w"][l], jnp.zeros((2, 128), f32)], axis=0)
        block = w[key_name[key]][l]
        return (block.T if key in TRANSPOSED else block).astype(bf16)

    def landing(src):
        return lax.dynamic_update_slice(lax.empty((NDEV,) + src.shape, src.dtype), src[None], (dev, 0, 0))

    groups = [[(s, landing(s)) for s in (source(k, l) for k in keys)] for l in range(2) for keys in group_keys]
    started = _gather_start("gather_start", groups)

    def weights(l, part, after):
        got = _gather_wait(f"gather_wait_{l}_{part}", started[3 * l + part], after)
        return dict(zip(group_keys[part], _gather_forward(f"gather_forward_{l}_{part}", got)))

    xl = x[0]
    saved, small, wts = [], [], []
    for l in range(2):
        ga = weights(l, 0, xl if l else None)
        wt = {"gu1": ga["gu1"][None], "d1": ga["d1"].reshape(1, 4, GU, D)}
        x1, s1 = _ffn_fwd("ffn1", xl, w["ffn1_norm"][l].reshape(1, D), wt["gu1"], wt["d1"], 0)
        gb = weights(l, 1, x1)
        p = {n: w[n][l] for n in REPLICATED_SMALL}
        p["b_gate"] = jnp.transpose(gb["small"][:, 0:3, :], (1, 0, 2)).reshape(3, D)
        p["conv_w"] = jnp.transpose(gb["small"][:, 3:6, :], (1, 0, 2)).reshape(3, D)
        p = _small_params(p)
        wt["win"] = _win_kernel_layout(gb["in"])
        wt["sq"] = jnp.stack([gb[k].reshape(D, D) for k in ("oc", "os", "oa", "o")])[None]
        x2, sm = _mixer_fwd("mix", x1, p, wt["win"], wt["sq"], 0)
        gc = weights(l, 2, x2)
        wt.update({"gu2": gc["gu2"][None], "d2": gc["d2"].reshape(1, 4, GU, D)})
        x3, s2 = _ffn_fwd("ffn2", x2, p["ffn2_norm"], wt["gu2"], wt["d2"], 0)
        saved.append((xl, x1, x2, s1, sm, s2))
        small.append(p)
        wts.append(wt)
        xl = x3
    loss_row, dx = _loss("loss", xl, loss_target[0])

    core = pc.reshape(1).astype(jnp.int32)
    chip_op = chip.reshape(1).astype(jnp.int32)
    buf = lambda r, c: lax.empty((2, 4, 1, r, c), bf16)
    flights = {}

    pairs = []

    def scatter(l, part, bufs, dx):
        started, dx = _pair_start(f"pair_start_{l}_{part}", bufs, dx)
        pairs.append((l, part, started))
        return dx

    def finish(dx, after):
        if pairs:
            l, part, started = pairs.pop()
            bufs, r1 = _pair_wait(f"pair_wait_{l}_{part}", started, after)
            ss = [_pair_sum(f"pair_sum_{l}_{k}", core, g, r) for k, g, r in zip(group_keys[part], bufs, r1)]
            flights[l, part], dx = _scatter_start(f"scatter_start_{l}_{part}", ss, dx)
        return dx

    sgrads = [None, None]
    for l in (1, 0):
        p, wt = small[l], wts[l]
        x0, x1, x2, s1, sm, s2 = saved[l]
        dx, dn2, g_gu2, g_d2 = _ffn_bwd("ffn2", dx, x2, p["ffn2_norm"], wt["gu2"], wt["d2"], 0, s2, buf(GU, D),
                                        buf(GU // 2, D), finish)
        dx = scatter(l, 2, [g_gu2, g_d2], dx)
        dx, sg, dwin, gsq = _mixer_bwd("mix", dx, x1, p, wt["win"], wt["sq"], 0, sm, [buf(128, D) for _ in range(4)], finish)
        g_in = jnp.stack([_win_device_block(dwin, 2 * q + c) for c in range(2) for q in range(4)]).reshape(2, 4, 1, D, INB)
        dx = scatter(l, 1, [g_in] + gsq, dx)
        dx, dn1, g_gu1, g_d1 = _ffn_bwd("ffn1", dx, x0, p["ffn1_norm"], wt["gu1"], wt["d1"], 0, s1, buf(GU, D),
                                        buf(GU // 2, D), finish)
        dx = scatter(l, 0, [g_gu1, g_d1], dx)
        sg["ffn1_norm"] = dn1
        sg["ffn2_norm"] = dn2
        sgrads[l] = sg
    dx = finish(dx, dx)

    nat = [_small_grads_natural(sgrads[l]) for l in range(2)]
    order = REPLICATED_SMALL + SHARDED_SMALL
    part = _pack([jnp.stack([nat[0][n], nat[1][n]]) for n in order] + [loss_row[0, 0:1]])
    small_flight, dx = _gather_start("small_start", [[(part, landing(part))]], carry=dx)

    grads, delta, new_m, new_v = {}, {}, {}, {}
    after = dx
    for part in (2, 1, 0):
        sets = []
        for l in (1, 0):
            s_all, r2_all = _scatter_wait(f"scatter_wait_{l}_{part}", flights[l, part], after)
            sets.append(list(zip(s_all, r2_all)))
        for i, k in enumerate(k for k in group_keys[part] if k != "small"):
            n = key_name[k]
            view = (lambda a: jnp.swapaxes(a, 1, 2)) if k in TRANSPOSED else (lambda a: a)
            outs = _adamw("adamw_" + k, chip_op, view(w[n]), view(mom[n]), view(var[n]), [sets[1][i], sets[0][i]])
            grads[n], delta[n], new_m[n], new_v[n] = [view(o) for o in outs]
            after = outs[1]

    blocks = _gather_forward("small_forward", _gather_wait("small_wait", small_flight[0], after))
    total = _sum_blocks("small_sum", blocks[0])
    full_shapes = [(2,) + tuple(nat[0][n].shape) for n in order] + [(1,)]
    summed = dict(zip(order + ("loss",), _unpack(total, full_shapes)))
    for n in REPLICATED_SMALL:
        grads[n] = summed[n]
    for n in SHARDED_SMALL:
        grads[n] = lax.dynamic_slice_in_dim(summed[n], dev * 128, 128, axis=2)
    wp = _pack([w[n] for n in order])
    gp = _pack([grads[n] for n in order])
    mp = _pack([mom[n] for n in order])
    vp = _pack([var[n] for n in order])
    dpk, mpk, vpk = _adamw_small("adamw_small", wp, gp, mp, vp)
    local_shapes = [tuple(w[n].shape) for n in order]
    for dst, packed in ((delta, dpk), (new_m, mpk), (new_v, vpk)):
        dst.update(zip(order, _unpack(packed, local_shapes)))

    loss = summed["loss"][0]
    return (loss, dx[None], *[grads[n] for n in WEIGHT_NAMES], *[delta[n] for n in WEIGHT_NAMES],
            *[new_m[n] for n in WEIGHT_NAMES], *[new_v[n] for n in WEIGHT_NAMES])
```

```python
import functools

import jax
import jax.numpy as jnp
from jax import lax
from jax.experimental import pallas as pl
from jax.experimental.pallas import tpu as pltpu

f32 = jnp.float32
bf16 = jnp.bfloat16
S = jax.ShapeDtypeStruct
MESH = pl.DeviceIdType.MESH

D = 1024
NH = 8
HD = 128
NDEV = 8
GU = 704
NIN = 11272
INB = 1409
GATE_OFF = 8192
F_OFF = 11264
NZ = 11776
RMS_EPS = 1e-6
LN_EPS = 1e-5
ATT_SCALE = HD ** -0.5
NEG = -1e30
INV_SQRT2 = 0.7071067811865476
INV_SQRT2PI = 0.3989422804014327

ADAM_LR = 0.001
ADAM_B1 = 0.9
ADAM_B2 = 0.999
ADAM_EPS = 1e-08
ADAM_WD = 0.01
ADAM_STEP = 10

TT = 512
VMEM_LIMIT = 56 * 1024 * 1024


def _cp(*sem):
    return pltpu.CompilerParams(dimension_semantics=sem, vmem_limit_bytes=VMEM_LIMIT)


def _bs(shape, fn):
    return pl.BlockSpec(shape, fn)


NN = (((1,), (0,)), ((), ()))
NT = (((1,), (1,)), ((), ()))
TN = (((0,), (0,)), ((), ()))


def _mm(name, a, b, *, grid, a_spec, b_spec, out_shape, out_spec, dims, acc_shape, res=None, res_spec=None,
        alpha=1.0, alias=None, split_rows=None):
    nk = grid[2]

    def body(*refs):
        a_ref, b_ref = refs[0], refs[1]
        pos = 2
        res_ref = None
        if res is not None:
            res_ref = refs[pos]
            pos += 1
        if alias is not None:
            pos += 1
        o_ref = refs[pos]
        acc_ref = refs[pos + 1] if nk > 1 else None
        part = lax.dot_general(a_ref[...].astype(bf16), b_ref[...].astype(bf16), dims, preferred_element_type=f32)

        def finish(acc):
            if alpha != 1.0:
                acc = alpha * acc
            if res_ref is not None:
                acc = res_ref[...] + acc
            if split_rows is None:
                o_ref[...] = acc.astype(o_ref.dtype)
            else:
                o_ref[0] = acc[:split_rows].astype(o_ref.dtype)
                o_ref[1] = acc[split_rows:].astype(o_ref.dtype)

        if nk == 1:
            finish(part)
        else:
            k = pl.program_id(2)

            @pl.when(k == 0)
            def _():
                acc_ref[...] = part

            @pl.when(k > 0)
            def _():
                acc_ref[...] += part

            @pl.when(k == nk - 1)
            def _():
                finish(acc_ref[...])

    operands = [a, b]
    in_specs = [a_spec, b_spec]
    if res is not None:
        operands.append(res)
        in_specs.append(res_spec)
    aliases = {}
    if alias is not None:
        aliases = {len(operands): 0}
        operands.append(alias)
        in_specs.append(pl.BlockSpec(memory_space=pl.ANY))
    return pl.pallas_call(
        body, name=name, grid=grid, in_specs=in_specs, out_specs=out_spec, out_shape=out_shape,
        scratch_shapes=[pltpu.VMEM(acc_shape, f32)] if nk > 1 else [],
        input_output_aliases=aliases,
        compiler_params=_cp("parallel", "parallel", "arbitrary"),
    )(*operands)


def _tile(n, t):
    return t if n % t == 0 and n >= t else n


DZ_TILE = 512


def _dz_matmul(name, pieces, other, weight_grad):
    T = pieces[0].shape[0]
    counts = [p.shape[1] // DZ_TILE for p in pieces]
    starts = [sum(counts[:i]) for i in range(len(counts))]
    steps = sum(counts)
    npc = len(pieces)

    def body(*refs):
        prefs, o_ref, rest = refs[:npc], refs[npc], refs[npc + 1:]
        k = pl.program_id(0)
        if not weight_grad:
            out_ref, acc_ref = rest

            @pl.when(k == 0)
            def _():
                acc_ref[...] = jnp.zeros_like(acc_ref)

        for p_ref, s, c in zip(prefs, starts, counts):
            @pl.when((k >= s) & (k < s + c))
            def _():
                if weight_grad:
                    rest[0][...] = lax.dot_general(o_ref[...], p_ref[...], TN, preferred_element_type=f32).astype(bf16)
                else:
                    acc_ref[...] += lax.dot_general(p_ref[...], o_ref[...], NT, preferred_element_type=f32)

        if not weight_grad:
            @pl.when(k == steps - 1)
            def _():
                out_ref[...] = acc_ref[...]

    piece_specs = [_bs((T, DZ_TILE), functools.partial(lambda k, s, c: (0, jnp.clip(k - s, 0, c - 1)), s=s, c=c))
                   for s, c in zip(starts, counts)]
    if weight_grad:
        other_spec, out_spec, out_shape, scratch = _bs((T, D), lambda k: (0, 0)), _bs((D, DZ_TILE), lambda k: (0, k)), S((D, NZ), bf16), []
    else:
        other_spec, out_spec, out_shape = _bs((D, DZ_TILE), lambda k: (0, k)), _bs((T, D), lambda k: (0, 0)), S((T, D), f32)
        scratch = [pltpu.VMEM((T, D), f32)]
    return pl.pallas_call(body, name=name, grid=(steps,), in_specs=piece_specs + [other_spec], out_specs=out_spec,
                          out_shape=out_shape, scratch_shapes=scratch, compiler_params=_cp("arbitrary"))(*pieces, other)


def _row(cb=0, w=D):
    return _bs((TT, w), lambda i: (i, cb))


def _vec(rows=1, w=D):
    return _bs((rows, w), lambda i: (0, 0))


def _acc_store(i, ref, val):
    @pl.when(i == 0)
    def _():
        ref[...] = val

    @pl.when(i > 0)
    def _():
        ref[...] += val


def _rms_fwd(name, x, g):
    T = x.shape[0]

    def body(x_ref, g_ref, o_ref):
        xv = x_ref[...]
        r = lax.rsqrt(jnp.mean(xv * xv, axis=-1, keepdims=True) + RMS_EPS)
        o_ref[...] = (xv * r * g_ref[...]).astype(bf16)

    return pl.pallas_call(body, name=name, grid=(T // TT,), in_specs=[_row(), _vec()], out_specs=_row(),
                          out_shape=S((T, D), bf16), compiler_params=_cp("parallel"))(x, g)


def _rms_bwd(name, dh, x, g, dres):
    T = x.shape[0]

    def body(dh_ref, x_ref, g_ref, dres_ref, dx_ref, dg_ref):
        i = pl.program_id(0)
        xv = x_ref[...]
        r = lax.rsqrt(jnp.mean(xv * xv, axis=-1, keepdims=True) + RMS_EPS)
        xhat = xv * r
        dh_v = dh_ref[...]
        dyg = dh_v * g_ref[...]
        m = jnp.mean(dyg * xhat, axis=-1, keepdims=True)
        dx_ref[...] = dres_ref[...] + r * (dyg - xhat * m)
        _acc_store(i, dg_ref, jnp.sum(dh_v * xhat, axis=0, keepdims=True))

    return pl.pallas_call(body, name=name, grid=(T // TT,), in_specs=[_row(), _row(), _vec(), _row()],
                          out_specs=[_row(), _vec()], out_shape=[S((T, D), f32), S((1, D), f32)],
                          compiler_params=_cp("arbitrary"))(dh, x, g, dres)


def _sigmoid(x):
    return 1.0 / (1.0 + jnp.exp(-x))


def _swiglu_fwd(name, h, wgu, l):
    T = h.shape[0]

    def body(h_ref, wg_ref, wu_ref, gu_ref, a_ref):
        hv = h_ref[...]
        g = lax.dot_general(hv, wg_ref[...], NT, preferred_element_type=f32)
        u = lax.dot_general(hv, wu_ref[...], NT, preferred_element_type=f32)
        gu_ref[0] = g
        gu_ref[1] = u
        a_ref[...] = (g * _sigmoid(g) * u).astype(bf16)

    return pl.pallas_call(
        body, name=name, grid=(4,),
        in_specs=[_bs((T, D), lambda j: (0, 0)), _bs((None, None, GU, D), lambda j: (l, j, 0, 0)),
                  _bs((None, None, GU, D), lambda j: (l, j + 4, 0, 0))],
        out_specs=[_bs((2, None, T, GU), lambda j: (0, j, 0, 0)), _bs((None, T, GU), lambda j: (j, 0, 0))],
        out_shape=[S((2, 4, T, GU), f32), S((4, T, GU), bf16)], compiler_params=_cp("parallel"))(h, wgu, wgu)


def _swiglu_bwd(name, dxo, wd, gu, l):
    T = dxo.shape[0]
    tm = _tile(T, 1024)

    def body(dx_ref, wd_ref, g_ref, u_ref, o_ref):
        da = 0.5 * lax.dot_general(dx_ref[...].astype(bf16), wd_ref[...], NT, preferred_element_type=f32)
        g = g_ref[...]
        sg = _sigmoid(g)
        o_ref[0] = (da * u_ref[...] * (sg + g * sg * (1.0 - sg))).astype(bf16)
        o_ref[1] = (da * g * sg).astype(bf16)

    return pl.pallas_call(
        body, name=name, grid=(T // tm, 4),
        in_specs=[_bs((tm, D), lambda i, j: (i, 0)), _bs((None, None, GU, D), lambda i, j: (l, j, 0, 0)),
                  _bs((None, None, tm, GU), lambda i, j: (0, j, i, 0)), _bs((None, None, tm, GU), lambda i, j: (1, j, i, 0))],
        out_specs=_bs((2, None, tm, GU), lambda i, j: (0, j, i, 0)), out_shape=S((2, 4, T, GU), bf16),
        compiler_params=_cp("parallel", "parallel"))(dxo, wd, gu, gu)


def _loss(name, y, tgt):
    T = y.shape[0]

    def body(y_ref, t_ref, l_ref, dy_ref):
        i = pl.program_id(0)
        e = y_ref[...] - t_ref[...]
        dy_ref[...] = e * (1.0 / D)
        s = 0.5 * jnp.sum(jnp.mean(e * e, axis=-1, keepdims=True))
        _acc_store(i, l_ref, jnp.broadcast_to(s, (1, 128)))

    return pl.pallas_call(body, name=name, grid=(T // TT,), in_specs=[_row(), _row()],
                          out_specs=[_vec(1, 128), _row()], out_shape=[S((1, 128), f32), S((T, D), f32)],
                          compiler_params=_cp("arbitrary"))(y, tgt)


def _prev8(T, cb):
    return _bs((8, D), lambda i: (jnp.maximum(i * (TT // 8) - 1, 0), cb))


def _next8(T, cb):
    return _bs((8, D), lambda i: (jnp.minimum((i + 1) * (TT // 8), T // 8 - 1), cb))


def _conv_taps(i, ac_ref, ax_ref, pc_ref, px_ref):
    ca = ac_ref[...] * ax_ref[...]
    keep = (i > 0).astype(f32)
    p1 = pc_ref[7:8, :] * px_ref[7:8, :] * keep
    p2 = pc_ref[6:7, :] * px_ref[6:7, :] * keep
    row = lax.broadcasted_iota(jnp.int32, ca.shape, 0)
    s1 = jnp.where(row == 0, p1, pltpu.roll(ca, 1, 0))
    s2 = jnp.where(row == 0, p2, jnp.where(row == 1, p1, pltpu.roll(ca, 2, 0)))
    return ca, s1, s2


def _conv_fwd(name, z, cw):
    T = z.shape[0]

    def body(ab_ref, ac_ref, ax_ref, pc_ref, px_ref, w_ref, o_ref):
        i = pl.program_id(0)
        ca, s1, s2 = _conv_taps(i, ac_ref, ax_ref, pc_ref, px_ref)
        cv = w_ref[0:1, :] * s2 + w_ref[1:2, :] * s1 + w_ref[2:3, :] * ca
        o_ref[...] = (ab_ref[...] * cv).astype(bf16)

    return pl.pallas_call(
        body, name=name, grid=(T // TT,),
        in_specs=[_row(0), _row(1), _row(2), _prev8(T, 1), _prev8(T, 2), _vec(3)],
        out_specs=_row(), out_shape=S((T, D), bf16), compiler_params=_cp("parallel"))(z, z, z, z, z, cw)


def _conv_bwd(name, dya, z, cw):
    T = z.shape[0]
    n = T // TT

    def body(dya_ref, ab_ref, ac_ref, ax_ref, pc_ref, px_ref, ndya_ref, nab_ref, w_ref, dz_ref, dw_ref):
        i = pl.program_id(0)
        ca, s1, s2 = _conv_taps(i, ac_ref, ax_ref, pc_ref, px_ref)
        w0, w1, w2 = w_ref[0:1, :], w_ref[1:2, :], w_ref[2:3, :]
        cv = w0 * s2 + w1 * s1 + w2 * ca
        dya_v = dya_ref[...]
        ab = ab_ref[...]
        dcv = dya_v * ab
        keep = (i < n - 1).astype(f32)
        n1 = ndya_ref[0:1, :] * nab_ref[0:1, :] * keep
        n2 = ndya_ref[1:2, :] * nab_ref[1:2, :] * keep
        row = lax.broadcasted_iota(jnp.int32, dcv.shape, 0)
        f1 = jnp.where(row == TT - 1, n1, pltpu.roll(dcv, TT - 1, 0))
        f2 = jnp.where(row == TT - 1, n2, jnp.where(row == TT - 2, n1, pltpu.roll(dcv, TT - 2, 0)))
        dca = w2 * dcv + w1 * f1 + w0 * f2
        dz_ref[:, 0:D] = (dya_v * cv).astype(bf16)
        dz_ref[:, D:2 * D] = (dca * ax_ref[...]).astype(bf16)
        dz_ref[:, 2 * D:3 * D] = (dca * ac_ref[...]).astype(bf16)
        dw = jnp.concatenate([jnp.sum(dcv * s2, axis=0, keepdims=True), jnp.sum(dcv * s1, axis=0, keepdims=True),
                              jnp.sum(dcv * ca, axis=0, keepdims=True)], axis=0)
        _acc_store(i, dw_ref, dw)

    return pl.pallas_call(
        body, name=name, grid=(n,),
        in_specs=[_row(), _row(0), _row(1), _row(2), _prev8(T, 1), _prev8(T, 2), _next8(T, 0), _next8(T, 0), _vec(3)],
        out_specs=[_row(0, 3 * D), _vec(3)], out_shape=[S((T, 3 * D), bf16), S((3, D), f32)],
        compiler_params=_cp("arbitrary"))(dya, z, z, z, z, z, dya, z, cw)


def _gelu(x):
    return 0.5 * x * (1.0 + lax.erf(x * INV_SQRT2))


def _gelu_cdf(x):
    return 0.5 * (1.0 + lax.erf(x * INV_SQRT2))


def _gelu_grad(x, cdf):
    return cdf + x * jnp.exp(-0.5 * x * x) * INV_SQRT2PI


def _ln_stats(vv):
    mu = jnp.mean(vv, axis=-1, keepdims=True)
    xc = vv - mu
    rstd = lax.rsqrt(jnp.mean(xc * xc, axis=-1, keepdims=True) + LN_EPS)
    return xc * rstd, rstd


def _tril_w(w_ref, g):
    r = lax.broadcasted_iota(jnp.int32, (HD, HD), 0)
    c = lax.broadcasted_iota(jnp.int32, (HD, HD), 1)
    return jnp.where(c <= r, w_ref[g], 0.0).astype(bf16)


def _sgu_fwd(name, z, ln_g, ln_b, w_s, bmap):
    T = z.shape[0]

    def body(su_ref, sv_ref, lg_ref, lb_ref, w_ref, bm_ref, o_ref, vn_ref):
        xhat, _ = _ln_stats(_gelu(sv_ref[...]))
        vn_ref[...] = (xhat * lg_ref[...] + lb_ref[...]).astype(bf16)
        for g in range(NH):
            w = _tril_w(w_ref, g)
            cs = slice(g * HD, (g + 1) * HD)
            for c in range(TT // HD):
                rs = slice(c * HD, (c + 1) * HD)
                s = jnp.dot(w, vn_ref[rs, cs], preferred_element_type=f32) + bm_ref[:, cs]
                o_ref[rs, cs] = (_gelu(su_ref[rs, cs]) * s).astype(bf16)

    return pl.pallas_call(
        body, name=name, grid=(T // TT,),
        in_specs=[_row(3), _row(4), _vec(), _vec(), _bs((NH, HD, HD), lambda i: (0, 0, 0)), _vec(HD)],
        out_specs=_row(), out_shape=S((T, D), bf16), scratch_shapes=[pltpu.VMEM((TT, D), bf16)],
        compiler_params=_cp("parallel"))(z, z, ln_g, ln_b, w_s, bmap)


def _sgu_bwd(name, dyb, z, ln_g, ln_b, w_s, bmap):
    T = z.shape[0]

    def body(dyb_ref, su_ref, sv_ref, lg_ref, lb_ref, w_ref, bm_ref, dz_ref, dlg_ref, dlb_ref, dw_ref, db_ref,
             vn_ref, du_ref, dvn_ref, cu_ref, cv_ref):
        i = pl.program_id(0)
        sv = sv_ref[...]
        cv_ref[...] = _gelu_cdf(sv)
        cu_ref[...] = _gelu_cdf(su_ref[...])
        xhat, rstd = _ln_stats(sv * cv_ref[...])
        vn_ref[...] = (xhat * lg_ref[...] + lb_ref[...]).astype(bf16)
        r = lax.broadcasted_iota(jnp.int32, (HD, HD), 0)
        cc = lax.broadcasted_iota(jnp.int32, (HD, HD), 1)
        for g in range(NH):
            w = _tril_w(w_ref, g)
            cs = slice(g * HD, (g + 1) * HD)
            dw = jnp.zeros((HD, HD), f32)
            db = jnp.zeros((HD, 1), f32)
            for c in range(TT // HD):
                rs = slice(c * HD, (c + 1) * HD)
                vnb = vn_ref[rs, cs]
                s = jnp.dot(w, vnb, preferred_element_type=f32) + bm_ref[:, cs]
                dy = dyb_ref[rs, cs]
                du_ref[rs, cs] = dy * s
                ds = dy * (su_ref[rs, cs] * cu_ref[rs, cs])
                ds16 = ds.astype(bf16)
                dvn_ref[rs, cs] = lax.dot_general(w, ds16, TN, preferred_element_type=f32)
                dw = dw + lax.dot_general(ds16, vnb, NT, preferred_element_type=f32)
                db = db + jnp.sum(ds, axis=1, keepdims=True)
            dw = jnp.where(cc <= r, dw, 0.0)

            @pl.when(i == 0)
            def _():
                dw_ref[g] = dw
                db_ref[:, g:g + 1] = db

            @pl.when(i > 0)
            def _():
                dw_ref[g] += dw
                db_ref[:, g:g + 1] += db

        dvn = dvn_ref[...]
        dxh = dvn * lg_ref[...]
        m1 = jnp.mean(dxh, axis=-1, keepdims=True)
        m2 = jnp.mean(dxh * xhat, axis=-1, keepdims=True)
        dvv = rstd * (dxh - m1 - xhat * m2)
        dz_ref[:, 0:D] = (du_ref[...] * _gelu_grad(su_ref[...], cu_ref[...])).astype(bf16)
        dz_ref[:, D:2 * D] = (dvv * _gelu_grad(sv, cv_ref[...])).astype(bf16)
        _acc_store(i, dlg_ref, jnp.sum(dvn * xhat, axis=0, keepdims=True))
        _acc_store(i, dlb_ref, jnp.sum(dvn, axis=0, keepdims=True))

    return pl.pallas_call(
        body, name=name, grid=(T // TT,),
        in_specs=[_row(), _row(3), _row(4), _vec(), _vec(), _bs((NH, HD, HD), lambda i: (0, 0, 0)), _vec(HD)],
        out_specs=[_row(0, 2 * D), _vec(), _vec(), _bs((NH, HD, HD), lambda i: (0, 0, 0)), _bs((HD, NH), lambda i: (0, 0))],
        out_shape=[S((T, 2 * D), bf16), S((1, D), f32), S((1, D), f32), S((NH, HD, HD), f32), S((HD, NH), f32)],
        scratch_shapes=[pltpu.VMEM((TT, D), bf16)] + [pltpu.VMEM((TT, D), f32)] * 4,
        compiler_params=_cp("arbitrary"))(dyb, z, z, ln_g, ln_b, w_s, bmap)


def _qk_fwd(name, z, qg, kg, bf):
    T = z.shape[0]

    def body(q_ref, k_ref, v_ref, zf_ref, qg_ref, kg_ref, bf_ref, qn_ref, kn_ref, vb_ref, lf_ref):
        for h in range(NH):
            cs = slice(h * HD, (h + 1) * HD)
            for src, gain, dst in ((q_ref, qg_ref, qn_ref), (k_ref, kg_ref, kn_ref)):
                xv = src[:, cs]
                r = lax.rsqrt(jnp.mean(xv * xv, axis=-1, keepdims=True) + RMS_EPS)
                dst[:, cs] = (xv * r * gain[:, cs]).astype(bf16)
        vb_ref[...] = v_ref[...].astype(bf16)
        xf = zf_ref[...] + bf_ref[...]
        lf_ref[...] = jnp.minimum(xf, 0.0) - jnp.log1p(jnp.exp(-jnp.abs(xf)))

    return pl.pallas_call(
        body, name=name, grid=(T // TT,),
        in_specs=[_row(5), _row(6), _row(7), _bs((TT, 128), lambda i: (i, F_OFF // 128)), _vec(), _vec(), _vec(1, 128)],
        out_specs=[_row(), _row(), _row(), _bs((TT, 128), lambda i: (i, 0))],
        out_shape=[S((T, D), bf16), S((T, D), bf16), S((T, D), bf16), S((T, 128), f32)],
        compiler_params=_cp("parallel"))(z, z, z, z, qg, kg, bf)


def _cum_fwd(name, logf):
    T = logf.shape[0]

    def body(lf_ref, ccol_ref, crow_ref, c_ref):
        c = lf_ref[...]
        row = lax.broadcasted_iota(jnp.int32, c.shape, 0)
        s = 1
        while s < T:
            c = c + jnp.where(row >= s, pltpu.roll(c, s, 0), 0.0)
            s *= 2
        c_ref[...] = c
        crow_ref[...] = c.T[0:NH, :]
        for h in range(NH):
            ccol_ref[h] = jnp.broadcast_to(c_ref[:, h:h + 1], (T, 128))

    return pl.pallas_call(body, name=name, out_shape=[S((NH, T, 128), f32), S((NH, T), f32)],
                          scratch_shapes=[pltpu.VMEM((T, 128), f32)],
                          compiler_params=pltpu.CompilerParams(vmem_limit_bytes=VMEM_LIMIT))(logf)


ATT_TILE = 1024


def _fold(x, op=jnp.add):
    acc = x[:, 0:128]
    for t in range(1, x.shape[1] // 128):
        acc = op(acc, x[:, t * 128:(t + 1) * 128])
    return acc


def _to_row(col):
    return jnp.broadcast_to(col, (col.shape[0], 128)).T[0:1, :]


def _causal(t, keys_down=False):
    r = lax.broadcasted_iota(jnp.int32, (t, t), 0)
    c = lax.broadcasted_iota(jnp.int32, (t, t), 1)
    return r <= c if keys_down else c <= r


def _attn_fwd(name, qn, kn, vb, ccol, crow3):
    T = qn.shape[0]
    tq = _tile(T, ATT_TILE)
    nq = T // tq

    def body(q_ref, k_ref, v_ref, cc_ref, cr_ref, o_ref, lse_ref, lser_ref, s_ref):
        qi = pl.program_id(1)
        q = q_ref[...]
        cq = cc_ref[:, 0:1]

        def logits(off):
            s = lax.dot_general(q, k_ref[pl.ds(off, tq), :], NT, preferred_element_type=f32) * ATT_SCALE
            return s + cq - cr_ref[:, pl.ds(off, tq)]

        def below(j, mvec):
            off = pl.multiple_of(j * tq, tq)
            s = logits(off)
            s_ref[:, pl.ds(off, tq)] = s
            return jnp.maximum(mvec, _fold(s, jnp.maximum))

        mvec = lax.fori_loop(0, qi, below, jnp.full((tq, 128), NEG, f32))
        off = pl.multiple_of(qi * tq, tq)
        s = jnp.where(_causal(tq), logits(off), NEG)
        s_ref[:, pl.ds(off, tq)] = s
        m = jnp.max(jnp.maximum(mvec, _fold(s, jnp.maximum)), axis=1, keepdims=True)

        def weigh(j, carry):
            lvec, acc = carry
            off = pl.multiple_of(j * tq, tq)
            p = jnp.exp(s_ref[:, pl.ds(off, tq)] - m)
            acc = acc + jnp.dot(p.astype(bf16), v_ref[pl.ds(off, tq), :], preferred_element_type=f32)
            return lvec + _fold(p), acc

        lvec, acc = lax.fori_loop(0, qi + 1, weigh, (jnp.zeros((tq, 128), f32), jnp.zeros((tq, HD), f32)))
        l = jnp.sum(lvec, axis=1, keepdims=True)
        o_ref[...] = acc / l
        lse = m + jnp.log(l)
        lse_ref[...] = jnp.broadcast_to(lse, (tq, 128))
        lser_ref[...] = _to_row(lse)

    return pl.pallas_call(
        body, name=name, grid=(NH, nq),
        in_specs=[_bs((tq, HD), lambda h, i: (i, h)), _bs((T, HD), lambda h, i: (0, h)), _bs((T, HD), lambda h, i: (0, h)),
                  _bs((None, tq, 128), lambda h, i: (h, i, 0)), _bs((None, 1, T), lambda h, i: (h, 0, 0))],
        out_specs=[_bs((tq, HD), lambda h, i: (i, h)), _bs((None, tq, 128), lambda h, i: (h, i, 0)),
                   _bs((None, 1, tq), lambda h, i: (h, 0, i))],
        out_shape=[S((T, D), f32), S((NH, T, 128), f32), S((NH, 1, T), f32)],
        scratch_shapes=[pltpu.VMEM((tq, T), f32)],
        compiler_params=_cp("parallel", "parallel"))(qn, kn, vb, ccol, crow3)


def _attn_dq(name, qn, kn, vb, do, lse, ccol, crow3):
    T = qn.shape[0]
    tq = _tile(T, ATT_TILE)
    nq = T // tq

    def body(q_ref, k_ref, v_ref, do_ref, lse_ref, cc_ref, cr_ref, dq_ref, dlr_ref, p_ref, dp_ref):
        qi = pl.program_id(1)
        q = q_ref[...]
        do16 = do_ref[...].astype(bf16)
        base = cc_ref[:, 0:1] - lse_ref[:, 0:1]

        def probs(off):
            s = lax.dot_general(q, k_ref[pl.ds(off, tq), :], NT, preferred_element_type=f32) * ATT_SCALE
            return jnp.exp(s + base - cr_ref[:, pl.ds(off, tq)])

        def keep(off, p, dvec):
            dp = lax.dot_general(do16, v_ref[pl.ds(off, tq), :], NT, preferred_element_type=f32)
            p_ref[:, pl.ds(off, tq)] = p
            dp_ref[:, pl.ds(off, tq)] = dp
            return dvec + _fold(p * dp)

        def below(j, dvec):
            off = pl.multiple_of(j * tq, tq)
            return keep(off, probs(off), dvec)

        dvec = lax.fori_loop(0, qi, below, jnp.zeros((tq, 128), f32))
        off = pl.multiple_of(qi * tq, tq)
        dvec = keep(off, jnp.where(_causal(tq), probs(off), 0.0), dvec)
        delta = jnp.sum(dvec, axis=1, keepdims=True)

        def grad(j, acc):
            off = pl.multiple_of(j * tq, tq)
            ds = p_ref[:, pl.ds(off, tq)] * (dp_ref[:, pl.ds(off, tq)] - delta)
            return acc + jnp.dot(ds.astype(bf16), k_ref[pl.ds(off, tq), :], preferred_element_type=f32)

        dq_ref[...] = lax.fori_loop(0, qi + 1, grad, jnp.zeros((tq, HD), f32)) * ATT_SCALE
        dlr_ref[...] = _to_row(delta)

    qb = lambda h, i: (i, h)
    full = lambda h, i: (0, h)
    col = lambda h, i: (h, i, 0)
    return pl.pallas_call(
        body, name=name, grid=(NH, nq),
        in_specs=[_bs((tq, HD), qb), _bs((T, HD), full), _bs((T, HD), full), _bs((tq, HD), qb),
                  _bs((None, tq, 128), col), _bs((None, tq, 128), col), _bs((None, 1, T), lambda h, i: (h, 0, 0))],
        out_specs=[_bs((tq, HD), qb), _bs((None, 1, tq), lambda h, i: (h, 0, i))],
        out_shape=[S((T, D), f32), S((NH, 1, T), f32)],
        scratch_shapes=[pltpu.VMEM((tq, T), f32), pltpu.VMEM((tq, T), f32)],
        compiler_params=_cp("parallel", "parallel"))(qn, kn, vb, do, lse, ccol, crow3)


def _attn_dkv(name, qn, kn, vb, do, lser3, dlr3, ccol, crow3):
    T = qn.shape[0]
    tk = _tile(T, ATT_TILE)
    nk = T // tk

    def body(q_ref, k_ref, v_ref, do_ref, lser_ref, dlr_ref, cc_ref, cr_ref, dk_ref, dv_ref, cs_ref):
        h = pl.program_id(0)
        kj = pl.program_id(1)

        @pl.when((h == 0) & (kj == 0))
        def _():
            cs_ref[...] = jnp.zeros_like(cs_ref)

        kb = k_ref[...]
        vv = v_ref[...]
        ckey = cc_ref[:, 0:1]

        def block(off, diagonal):
            rows = pl.ds(off, tk)
            qb = q_ref[rows, :]
            do16 = do_ref[rows, :].astype(bf16)
            st = lax.dot_general(kb, qb, NT, preferred_element_type=f32) * ATT_SCALE
            pt = jnp.exp(st + (cr_ref[:, rows] - lser_ref[:, rows]) - ckey)
            if diagonal:
                pt = jnp.where(_causal(tk, keys_down=True), pt, 0.0)
            dpt = lax.dot_general(vv, do16, NT, preferred_element_type=f32)
            dst = pt * (dpt - dlr_ref[:, rows])
            ddv = jnp.dot(pt.astype(bf16), do16, preferred_element_type=f32)
            ddk = jnp.dot(dst.astype(bf16), qb, preferred_element_type=f32)
            return ddk, ddv, _fold(dst)

        def above(i, carry):
            ddk, ddv, dcs = block(pl.multiple_of(i * tk, tk), False)
            return carry[0] + ddk, carry[1] + ddv, carry[2] + dcs

        off = pl.multiple_of(kj * tk, tk)
        dk, dv, cs = lax.fori_loop(kj + 1, nk, above, block(off, True))
        dk_ref[...] = dk * ATT_SCALE
        dv_ref[...] = dv
        lane = lax.broadcasted_iota(jnp.int32, (tk, 128), 1)
        cs_ref[pl.ds(off, tk), :] += jnp.where(lane == h, jnp.sum(cs, axis=1, keepdims=True), 0.0)

    full = lambda h, j: (0, h)
    blk = lambda h, j: (j, h)
    row = lambda h, j: (h, 0, 0)
    return pl.pallas_call(
        body, name=name, grid=(NH, nk),
        in_specs=[_bs((T, HD), full), _bs((tk, HD), blk), _bs((tk, HD), blk), _bs((T, HD), full), _bs((None, 1, T), row),
                  _bs((None, 1, T), row), _bs((None, tk, 128), lambda h, j: (h, j, 0)), _bs((None, 1, T), row)],
        out_specs=[_bs((tk, HD), blk), _bs((tk, HD), blk), _bs((T, 128), lambda h, j: (0, 0))],
        out_shape=[S((T, D), f32), S((T, D), f32), S((T, 128), f32)],
        compiler_params=_cp("arbitrary", "arbitrary"))(qn, kn, vb, do, lser3, dlr3, ccol, crow3)


def _forget_bwd(name, cs, z, bf):
    T = cs.shape[0]

    def body(cs_ref, zf_ref, bf_ref, dz_ref, db_ref):
        c = -cs_ref[...]
        row = lax.broadcasted_iota(jnp.int32, c.shape, 0)
        s = 1
        while s < T:
            c = c + jnp.where(row + s < T, pltpu.roll(c, T - s, 0), 0.0)
            s *= 2
        xf = zf_ref[...] + bf_ref[...]
        lane = lax.broadcasted_iota(jnp.int32, c.shape, 1)
        dxf = jnp.where(lane < NH, c / (1.0 + jnp.exp(xf)), 0.0)
        dz_ref[...] = jnp.zeros_like(dz_ref)
        dz_ref[:, 0:128] = dxf.astype(bf16)
        db_ref[...] = jnp.sum(dxf, axis=0, keepdims=True)

    return pl.pallas_call(
        body, name=name, grid=(1,),
        in_specs=[_bs((T, 128), lambda i: (0, 0)), _bs((T, 128), lambda i: (0, F_OFF // 128)), _vec(1, 128)],
        out_specs=[_bs((T, NZ - F_OFF), lambda i: (0, 0)), _vec(1, 128)],
        out_shape=[S((T, NZ - F_OFF), bf16), S((1, 128), f32)], compiler_params=_cp("arbitrary"))(cs, z, bf)


def _qk_bwd(name, dqn, dkn, dv, z, qg, kg):
    T = z.shape[0]

    def body(dq_ref, dk_ref, dv_ref, q_ref, k_ref, qg_ref, kg_ref, dz_ref, dqg_ref, dkg_ref, g_ref):
        i = pl.program_id(0)
        for n, (src, dsrc, gain, dgain) in enumerate(((q_ref, dq_ref, qg_ref, dqg_ref), (k_ref, dk_ref, kg_ref, dkg_ref))):
            for h in range(NH):
                cs = slice(h * HD, (h + 1) * HD)
                xv = src[:, cs]
                r = lax.rsqrt(jnp.mean(xv * xv, axis=-1, keepdims=True) + RMS_EPS)
                xhat = xv * r
                dy = dsrc[:, cs]
                dyg = dy * gain[:, cs]
                m = jnp.mean(dyg * xhat, axis=-1, keepdims=True)
                dz_ref[:, n * D + h * HD:n * D + (h + 1) * HD] = (r * (dyg - xhat * m)).astype(bf16)
                g_ref[:, cs] = jnp.sum(dy * xhat, axis=0, keepdims=True)
            _acc_store(i, dgain, g_ref[...])
        dz_ref[:, 2 * D:3 * D] = dv_ref[...].astype(bf16)

    return pl.pallas_call(
        body, name=name, grid=(T // TT,),
        in_specs=[_row(), _row(), _row(), _row(5), _row(6), _vec(), _vec()],
        out_specs=[_row(0, 3 * D), _vec(), _vec()], out_shape=[S((T, 3 * D), bf16), S((1, D), f32), S((1, D), f32)],
        scratch_shapes=[pltpu.VMEM((1, D), f32)], compiler_params=_cp("arbitrary"))(dqn, dkn, dv, z, z, qg, kg)


GB = GATE_OFF // D


def _merge_fwd(name, ya, yb, yc, z, bg):
    T = z.shape[0]

    def body(ya_ref, yb_ref, yc_ref, g0_ref, g1_ref, g2_ref, bg_ref, o_ref):
        acc = _sigmoid(g0_ref[...] + bg_ref[0:1, :]) * ya_ref[...]
        acc = acc + _sigmoid(g1_ref[...] + bg_ref[1:2, :]) * yb_ref[...]
        acc = acc + _sigmoid(g2_ref[...] + bg_ref[2:3, :]) * yc_ref[...]
        o_ref[...] = acc.astype(bf16)

    return pl.pallas_call(
        body, name=name, grid=(T // TT,),
        in_specs=[_row(), _row(), _row(), _row(GB), _row(GB + 1), _row(GB + 2), _vec(3)],
        out_specs=_row(), out_shape=S((T, D), bf16), compiler_params=_cp("parallel"))(ya, yb, yc, z, z, z, bg)


def _merge_bwd(name, dm, ya, yb, yc, z, bg):
    T = z.shape[0]

    def body(dm_ref, ya_ref, yb_ref, yc_ref, g0_ref, g1_ref, g2_ref, bg_ref, dya_ref, dyb_ref, dyc_ref, dz_ref, db_ref):
        i = pl.program_id(0)
        dm_v = dm_ref[...]
        dbs = []
        for n, (y_ref, g_ref, dy_ref) in enumerate(((ya_ref, g0_ref, dya_ref), (yb_ref, g1_ref, dyb_ref),
                                                    (yc_ref, g2_ref, dyc_ref))):
            gate = _sigmoid(g_ref[...] + bg_ref[n:n + 1, :])
            dy_ref[...] = (dm_v * gate).astype(bf16)
            dl = dm_v * y_ref[...] * gate * (1.0 - gate)
            dz_ref[:, n * D:(n + 1) * D] = dl.astype(bf16)
            dbs.append(jnp.sum(dl, axis=0, keepdims=True))
        _acc_store(i, db_ref, jnp.concatenate(dbs, axis=0))

    return pl.pallas_call(
        body, name=name, grid=(T // TT,),
        in_specs=[_row(), _row(), _row(), _row(), _row(GB), _row(GB + 1), _row(GB + 2), _vec(3)],
        out_specs=[_row(), _row(), _row(), _row(0, 3 * D), _vec(3)],
        out_shape=[S((T, D), bf16), S((T, D), bf16), S((T, D), bf16), S((T, 3 * D), bf16), S((3, D), f32)],
        compiler_params=_cp("arbitrary"))(dm, ya, yb, yc, z, z, z, bg)


SMALL_NAMES = ("ffn1_norm", "mix_norm", "b_forget", "b_gate", "conv_w", "sgu_ln_g", "sgu_ln_b", "sgu_w", "sgu_b",
               "q_norm_g", "k_norm_g", "ffn2_norm")


def _small_params(p):
    out = {n: p[n].reshape(1, D) for n in ("ffn1_norm", "mix_norm", "ffn2_norm", "sgu_ln_g", "sgu_ln_b", "q_norm_g", "k_norm_g")}
    out["b_forget"] = jnp.pad(p["b_forget"].reshape(1, NH), ((0, 0), (0, 128 - NH)))
    out["b_gate"] = p["b_gate"]
    out["conv_w"] = p["conv_w"]
    out["sgu_w"] = p["sgu_w"]
    out["bmap"] = jnp.repeat(p["sgu_b"].T, HD, axis=1)
    return out


def _small_grads_natural(sg):
    out = {n: sg[n].reshape(D) for n in ("ffn1_norm", "mix_norm", "ffn2_norm", "sgu_ln_g", "sgu_ln_b")}
    out["q_norm_g"] = sg["q_norm_g"].reshape(NH, HD)
    out["k_norm_g"] = sg["k_norm_g"].reshape(NH, HD)
    out["b_forget"] = sg["b_forget"][0, :NH]
    out["b_gate"] = sg["b_gate"]
    out["conv_w"] = sg["conv_w"]
    out["sgu_w"] = sg["sgu_w"]
    out["sgu_b"] = sg["sgu_b"]
    return out


SQ_TM = 1024


def _sq_fwd(name, a, wsq, l, n, res=None):
    T = a.shape[0]
    tm = _tile(T, SQ_TM)
    return _mm(name, a, wsq, grid=(T // tm, 1, 1), a_spec=_bs((tm, D), lambda i, j, k: (i, 0)),
               b_spec=_bs((None, None, D, D), lambda i, j, k: (l, n, 0, 0)),
               out_shape=S((T, D), f32), out_spec=_bs((tm, D), lambda i, j, k: (i, 0)), dims=NN, acc_shape=None,
               res=res, res_spec=_bs((tm, D), lambda i, j, k: (i, 0)))


def _sq_bwd_in(name, dy, wsq, l, n):
    T = dy.shape[0]
    tm = _tile(T, SQ_TM)
    return _mm(name, dy, wsq, grid=(T // tm, 1, 1), a_spec=_bs((tm, D), lambda i, j, k: (i, 0)),
               b_spec=_bs((None, None, D, D), lambda i, j, k: (l, n, 0, 0)),
               out_shape=S((T, D), f32), out_spec=_bs((tm, D), lambda i, j, k: (i, 0)), dims=NT, acc_shape=None)


def _sq_bwd_w(name, a, dy, gbuf, l):
    T = a.shape[0]
    return _mm(name, a, dy, grid=(NDEV // 2, 1, 1), a_spec=_bs((T, 256), lambda i, j, k: (0, i)),
               b_spec=_bs((T, D), lambda i, j, k: (0, 0)), out_shape=S(gbuf.shape, bf16),
               out_spec=_bs((2, None, None, 128, D), lambda i, j, k: (0, i, l, 0, 0)),
               dims=TN, acc_shape=None, alias=gbuf, split_rows=128)


def _ffn_fwd(tag, x, g, wgu, wd, l):
    T = x.shape[0]
    h = _rms_fwd(tag + "_rms", x, g)
    gu, a = _swiglu_fwd(tag + "_gu", h, wgu, l)
    tm = _tile(T, 1024)
    xo = _mm(tag + "_down", a, wd, grid=(T // tm, 1, 4), a_spec=_bs((None, tm, GU), lambda i, j, k: (k, i, 0)),
             b_spec=_bs((None, None, GU, D), lambda i, j, k: (l, k, 0, 0)), out_shape=S((T, D), f32),
             out_spec=_bs((tm, D), lambda i, j, k: (i, 0)), dims=NN, acc_shape=(tm, D), res=x,
             res_spec=_bs((tm, D), lambda i, j, k: (i, 0)), alpha=0.5)
    return xo, (h, gu, a)


def _ffn_bwd(tag, dxo, x, g, wgu, wd, l, saved, g_gu, g_d, ship, hook):
    h, gu, a = saved
    T = x.shape[0]
    g_d = _mm(tag + "_dwd", a, dxo, grid=(4, 1, 1), a_spec=_bs((None, T, GU), lambda i, j, k: (i, 0, 0)),
              b_spec=_bs((T, D), lambda i, j, k: (0, 0)), out_shape=S(g_d.shape, bf16),
              out_spec=_bs((2, None, None, GU // 2, D), lambda i, j, k: (0, i, l, 0, 0)), dims=TN, acc_shape=None,
              alpha=0.5, alias=g_d, split_rows=GU // 2)
    dgu = _swiglu_bwd(tag + "_dgu", dxo, wd, gu, l).reshape(NDEV, T, GU)
    g_gu = _mm(tag + "_dwgu", dgu, h, grid=(NDEV, 1, 1), a_spec=_bs((None, T, GU), lambda i, j, k: (i, 0, 0)),
               b_spec=_bs((T, D), lambda i, j, k: (0, 0)), out_shape=S(g_gu.shape, bf16),
               out_spec=_bs((None, None, None, GU, D), lambda i, j, k: (i % 2, i // 2, l, 0, 0)), dims=TN,
               acc_shape=None, alias=g_gu)
    dxo = ship([g_gu, g_d], dxo)
    dh = _mm(tag + "_dh", dgu, wgu, grid=(1, 1, NDEV), a_spec=_bs((None, T, GU), lambda i, j, k: (k, 0, 0)),
             b_spec=_bs((None, None, GU, D), lambda i, j, k: (l, k, 0, 0)), out_shape=S((T, D), f32),
             out_spec=_bs((T, D), lambda i, j, k: (0, 0)), dims=NN, acc_shape=(T, D))
    dxo = hook(dxo, dh)
    return _rms_bwd(tag + "_drms", dh, x, g, dxo)


def _mixer_fwd(tag, x, p, win, wsq, l):
    T = x.shape[0]
    h = _rms_fwd(tag + "_rms", x, p["mix_norm"])
    tn = 512
    z = _mm(tag + "_in", h, win, grid=(1, NZ // tn, 1), a_spec=_bs((T, D), lambda i, j, k: (0, 0)),
            b_spec=_bs((D, tn), lambda i, j, k: (0, j)), out_shape=S((T, NZ), f32),
            out_spec=_bs((T, tn), lambda i, j, k: (0, j)), dims=NN, acc_shape=None)
    ya_in = _conv_fwd(tag + "_conv", z, p["conv_w"])
    yb_in = _sgu_fwd(tag + "_sgu", z, p["sgu_ln_g"], p["sgu_ln_b"], p["sgu_w"], p["bmap"])
    qn, kn, vb, logf = _qk_fwd(tag + "_qk", z, p["q_norm_g"], p["k_norm_g"], p["b_forget"])
    ccol, crow = _cum_fwd(tag + "_cum", logf)
    crow3 = crow.reshape(NH, 1, T)
    o, lse, lser = _attn_fwd(tag + "_attn", qn, kn, vb, ccol, crow3)
    ya = _sq_fwd(tag + "_oconv", ya_in, wsq, l, 0)
    yb = _sq_fwd(tag + "_osgu", yb_in, wsq, l, 1)
    yc = _sq_fwd(tag + "_oattn", o, wsq, l, 2)
    merged = _merge_fwd(tag + "_merge", ya, yb, yc, z, p["b_gate"])
    xo = _sq_fwd(tag + "_o", merged, wsq, l, 3, res=x)
    return xo, (h, z, ya_in, yb_in, qn, kn, vb, ccol, crow3, o, lse, lser, ya, yb, yc, merged)


def _mixer_bwd(tag, dxo, x, p, win, wsq, l, saved, gsq, ship, hook):
    h, z, ya_in, yb_in, qn, kn, vb, ccol, crow3, o, lse, lser, ya, yb, yc, merged = saved
    T = x.shape[0]
    sg = {}
    dm = _sq_bwd_in(tag + "_dmerged", dxo, wsq, l, 3)
    gsq[3] = _sq_bwd_w(tag + "_dwo", merged, dxo, gsq[3], l)
    dya, dyb, dyc, dz_g, sg["b_gate"] = _merge_bwd(tag + "_dmerge", dm, ya, yb, yc, z, p["b_gate"])
    d_ya_in = _sq_bwd_in(tag + "_dconv_in", dya, wsq, l, 0)
    gsq[0] = _sq_bwd_w(tag + "_dwoc", ya_in, dya, gsq[0], l)
    d_yb_in = _sq_bwd_in(tag + "_dsgu_in", dyb, wsq, l, 1)
    gsq[1] = _sq_bwd_w(tag + "_dwos", yb_in, dyb, gsq[1], l)
    d_o = _sq_bwd_in(tag + "_dattn_in", dyc, wsq, l, 2)
    gsq[2] = _sq_bwd_w(tag + "_dwoa", o, dyc, gsq[2], l)
    dz_c, sg["conv_w"] = _conv_bwd(tag + "_dconv", d_ya_in, z, p["conv_w"])
    dz_s, sg["sgu_ln_g"], sg["sgu_ln_b"], sg["sgu_w"], db_t = _sgu_bwd(
        tag + "_dsgu", d_yb_in, z, p["sgu_ln_g"], p["sgu_ln_b"], p["sgu_w"], p["bmap"])
    sg["sgu_b"] = db_t.T
    dqn, dlr = _attn_dq(tag + "_dattn_q", qn, kn, vb, d_o, lse, ccol, crow3)
    dkn, dv, cs = _attn_dkv(tag + "_dattn_kv", qn, kn, vb, d_o, lser, dlr, ccol, crow3)
    dz_f, sg["b_forget"] = _forget_bwd(tag + "_dforget", cs, z, p["b_forget"])
    dz_q, sg["q_norm_g"], sg["k_norm_g"] = _qk_bwd(tag + "_dqk", dqn, dkn, dv, z, p["q_norm_g"], p["k_norm_g"])
    dz = [dz_c, dz_s, dz_q, dz_g, dz_f]
    dwin = _dz_matmul(tag + "_dwin", dz, h, True)
    dxo = ship(dwin, gsq, dxo)
    dh = _dz_matmul(tag + "_dh", dz, win, False)
    dxo = hook(dxo, dh)
    dx, sg["mix_norm"] = _rms_bwd(tag + "_drms", dh, x, p["mix_norm"], dxo)
    return dx, sg


ANY = pl.BlockSpec(memory_space=pl.ANY)
HBM = pl.BlockSpec(memory_space=pltpu.HBM)
SEM = pl.BlockSpec(memory_space=pltpu.SEMAPHORE)
EFFECT = pltpu.SideEffectType.DATAFLOW_SIDE_EFFECTING


def _place():
    return lax.axis_index("x"), lax.axis_index("y"), lax.axis_index("c")


NEAR = 4


def _others(x, y, c):
    return [(x, y, 1 - c), (1 - x, y, c), (x, 1 - y, c), (1 - x, 1 - y, c)]


def _gather_start(name, groups, carry=None):
    sizes = [len(g) for g in groups]
    srcs = [s for g in groups for s, _ in g]
    lands = [b for g in groups for _, b in g]
    n, ng = len(srcs), len(groups)
    held = srcs + lands + ([] if carry is None else [carry])
    nh = len(held)

    def body(*refs):
        src_refs, land_refs = refs[:n], refs[n:2 * n]
        send, recv = refs[nh:nh + ng], refs[nh + ng:nh + 2 * ng]
        x, y, c = _place()
        me = 4 * x + 2 * y + c
        u = 0
        for g, size in enumerate(sizes):
            for i in range(size):
                for k, peer in enumerate(_others(x, y, c)):
                    pltpu.make_async_remote_copy(src_ref=src_refs[u], dst_ref=land_refs[u].at[me],
                                                 send_sem=send[g].at[i * NEAR + k], recv_sem=recv[g].at[i * NEAR + k],
                                                 device_id=peer, device_id_type=MESH).start()
                u += 1

    sems = [pltpu.SemaphoreType.DMA((size * NEAR,)) for size in sizes]
    out = pl.pallas_call(
        body, name=name, in_specs=[HBM] * nh, out_specs=[SEM] * (2 * ng) + [HBM] * nh,
        out_shape=sems + sems + [pltpu.HBM(a.shape, a.dtype) for a in held],
        input_output_aliases={i: 2 * ng + i for i in range(nh)},
        compiler_params=pltpu.CompilerParams(has_side_effects=EFFECT),
    )(*[pltpu.with_memory_space_constraint(a, pltpu.HBM) for a in held])
    res, pos = [], 0
    for g, size in enumerate(sizes):
        res.append((out[g], out[ng + g], out[2 * ng + pos:2 * ng + pos + size], out[2 * ng + n + pos:2 * ng + n + pos + size]))
        pos += size
    return res if carry is None else (res, out[2 * ng + 2 * n])


def _gather_wait(name, started, after=None):
    send, recv, srcs, lands = started
    n = len(srcs)

    def body(*refs):
        src_refs, land_refs = refs[:n], refs[n:2 * n]
        send_ref, recv_ref = refs[2 * n], refs[2 * n + 1]
        x, y, c = _place()
        for i in range(n):
            for k, (px, py, pc) in enumerate(_others(x, y, c)):
                cp = pltpu.make_async_remote_copy(src_ref=src_refs[i], dst_ref=land_refs[i].at[4 * px + 2 * py + pc],
                                                  send_sem=send_ref.at[i * NEAR + k], recv_sem=recv_ref.at[i * NEAR + k],
                                                  device_id=(px, py, pc), device_id_type=MESH)
                cp.wait_send()
                cp.wait_recv()

    extra = [] if after is None else [after]
    out = pl.pallas_call(
        body, name=name, in_specs=[HBM] * (2 * n) + [SEM, SEM] + [ANY] * len(extra), out_specs=[HBM] * (2 * n),
        out_shape=[pltpu.HBM(a.shape, a.dtype) for a in list(srcs) + list(lands)],
        input_output_aliases={i: i for i in range(2 * n)},
        compiler_params=pltpu.CompilerParams(has_side_effects=EFFECT),
    )(*srcs, *lands, send, recv, *extra)
    return out[n:]


def _gather_forward(name, lands):
    n = len(lands)

    def body(*refs):
        have, full = refs[:n], refs[n:2 * n]
        send, recv = refs[2 * n], refs[2 * n + 1]
        x, y, c = _place()
        chips = [(1 - x, y), (x, 1 - y), (1 - x, 1 - y)]

        def copy(i, j, core):
            slot = 4 * chips[j][0] + 2 * chips[j][1] + core
            return pltpu.make_async_remote_copy(src_ref=have[i].at[slot], dst_ref=full[i].at[slot],
                                                send_sem=send.at[i * 3 + j], recv_sem=recv.at[i * 3 + j],
                                                device_id=(x, y, 1 - c), device_id_type=MESH)

        for i in range(n):
            for j in range(3):
                copy(i, j, c).start()
        for i in range(n):
            for j in range(3):
                copy(i, j, c).wait_send()
                copy(i, j, 1 - c).wait_recv()

    return pl.pallas_call(
        body, name=name, in_specs=[ANY] * n, out_specs=[ANY] * n, out_shape=[S(a.shape, a.dtype) for a in lands],
        input_output_aliases={i: i for i in range(n)},
        scratch_shapes=[pltpu.SemaphoreType.DMA((n * 3,)), pltpu.SemaphoreType.DMA((n * 3,))],
    )(*lands)


def _pair_copies(ins, outs, send, recv):
    x, y, c = _place()
    return [pltpu.make_async_remote_copy(src_ref=ins[u].at[1 - c], dst_ref=outs[u], send_sem=send.at[u],
                                         recv_sem=recv.at[u], device_id=(x, y, 1 - c), device_id_type=MESH)
            for u in range(len(ins))]


def _pair_start(name, gs, carry):
    n = len(gs)
    held = list(gs) + [lax.empty(g.shape[1:], g.dtype) for g in gs] + [carry]

    def body(*refs):
        for cp in _pair_copies(refs[:n], refs[n:2 * n], refs[2 * n + 1], refs[2 * n + 2]):
            cp.start()

    sems = [pltpu.SemaphoreType.DMA((n,))] * 2
    out = pl.pallas_call(
        body, name=name, in_specs=[HBM] * len(held), out_specs=[SEM, SEM] + [HBM] * len(held),
        out_shape=sems + [pltpu.HBM(a.shape, a.dtype) for a in held],
        input_output_aliases={i: 2 + i for i in range(len(held))},
        compiler_params=pltpu.CompilerParams(has_side_effects=EFFECT),
    )(*[pltpu.with_memory_space_constraint(a, pltpu.HBM) for a in held])
    return (out[0], out[1], out[2:2 + n], out[2 + n:2 + 2 * n]), out[2 + 2 * n]


def _pair_wait(name, started, after):
    send, recv, gs, lands = started
    n = len(gs)

    def body(*refs):
        for cp in _pair_copies(refs[:n], refs[n:2 * n], refs[2 * n], refs[2 * n + 1]):
            cp.wait_send()
            cp.wait_recv()

    out = pl.pallas_call(
        body, name=name, in_specs=[HBM] * (2 * n) + [SEM, SEM, ANY], out_specs=[HBM] * (2 * n),
        out_shape=[pltpu.HBM(a.shape, a.dtype) for a in list(gs) + list(lands)],
        input_output_aliases={i: i for i in range(2 * n)},
        compiler_params=pltpu.CompilerParams(has_side_effects=EFFECT),
    )(*gs, *lands, send, recv, after)
    return out[:n], out[n:]


def _row_tile(r, c):
    if c > D and r % 128 == 0:
        return 128
    return 256 if r % 256 == 0 else (GU // 2 if r % (GU // 2) == 0 else r)


def _pair_sum(name, core, g, r1):
    _, nq, nl, r, c = g.shape
    tr = r
    g4 = g.reshape(2, nq * nl, r, c)
    r3 = r1.reshape(nq * nl, r, c)

    def body(core_ref, g_ref, r_ref, o_ref):
        o_ref[...] = (g_ref[...].astype(f32) + r_ref[...].astype(f32)).astype(bf16)

    out = pl.pallas_call(
        body, name=name,
        grid_spec=pltpu.PrefetchScalarGridSpec(
            num_scalar_prefetch=1, grid=(nq * nl, r // tr),
            in_specs=[_bs((None, None, tr, c), lambda b, i, cr: (cr[0], b, i, 0)), _bs((None, tr, c), lambda b, i, cr: (b, i, 0))],
            out_specs=_bs((None, tr, c), lambda b, i, cr: (b, i, 0))),
        out_shape=S((nq * nl, r, c), bf16), compiler_params=_cp("parallel", "parallel"))(core, g4, r3)
    return out.reshape(nq, nl, r, c)


def _scatter_copies(ins, outs, send, recv):
    x, y, c = _place()
    chips = [(1 - x, y), (x, 1 - y), (1 - x, 1 - y)]
    return [pltpu.make_async_remote_copy(src_ref=ins[u].at[2 * chip[0] + chip[1]], dst_ref=outs[u].at[k],
                                         send_sem=send.at[u * 3 + k], recv_sem=recv.at[u * 3 + k],
                                         device_id=(*chip, c), device_id_type=MESH)
            for u in range(len(ins)) for k, chip in enumerate(chips)]


def _scatter_start(name, ss, carry):
    n = len(ss)
    lands = [lax.empty((3,) + s.shape[1:], s.dtype) for s in ss]
    held = list(ss) + lands + [carry]

    def body(*refs):
        for cp in _scatter_copies(refs[:n], refs[n:2 * n], refs[2 * n + 1], refs[2 * n + 2]):
            cp.start()

    sems = [pltpu.SemaphoreType.DMA((n * 3,))] * 2
    out = pl.pallas_call(
        body, name=name, in_specs=[HBM] * len(held), out_specs=[SEM, SEM] + [HBM] * len(held),
        out_shape=sems + [pltpu.HBM(a.shape, a.dtype) for a in held],
        input_output_aliases={i: 2 + i for i in range(len(held))},
        compiler_params=pltpu.CompilerParams(has_side_effects=EFFECT),
    )(*[pltpu.with_memory_space_constraint(a, pltpu.HBM) for a in held])
    return (out[0], out[1], out[2:2 + n], out[2 + n:2 + 2 * n]), out[2 + 2 * n]


def _scatter_wait(name, started, after):
    send, recv, srcs, lands = started
    n = len(srcs)

    def body(*refs):
        for cp in _scatter_copies(refs[:n], refs[n:2 * n], refs[2 * n], refs[2 * n + 1]):
            cp.wait_send()
            cp.wait_recv()

    out = pl.pallas_call(
        body, name=name, in_specs=[HBM] * (2 * n) + [SEM, SEM, ANY], out_specs=[HBM] * (2 * n),
        out_shape=[pltpu.HBM(a.shape, a.dtype) for a in list(srcs) + list(lands)],
        input_output_aliases={i: i for i in range(2 * n)},
        compiler_params=pltpu.CompilerParams(has_side_effects=EFFECT),
    )(*srcs, *lands, send, recv, after)
    return out[:n], out[n:]


def _sum_blocks(name, blocks):
    def body(b_ref, o_ref):
        acc = b_ref[0]
        for d in range(1, NDEV):
            acc = acc + b_ref[d]
        o_ref[...] = acc

    return pl.pallas_call(body, name=name, out_shape=S(blocks.shape[1:], f32),
                          compiler_params=pltpu.CompilerParams(vmem_limit_bytes=VMEM_LIMIT))(blocks)


def _adam_math(w, g, m, v):
    m = ADAM_B1 * m + (1.0 - ADAM_B1) * g
    v = ADAM_B2 * v + (1.0 - ADAM_B2) * (g * g)
    m_hat = m / (1.0 - ADAM_B1 ** ADAM_STEP)
    v_hat = v / (1.0 - ADAM_B2 ** ADAM_STEP)
    delta = -ADAM_LR * (m_hat / (jnp.sqrt(v_hat) + ADAM_EPS) + ADAM_WD * w)
    return delta, m, v


def _adamw(name, chip, w, m, v, parts):
    _, r, c = w.shape
    tr = _row_tile(r, c)

    def body(chip_ref, w_ref, m_ref, v_ref, *refs):
        sets, (g_ref, d_ref, mo_ref, vo_ref) = (refs[0:4], refs[4:8]), refs[8:]
        for l in range(2):
            @pl.when(pl.program_id(0) == l)
            def _():
                s_ref, r0_ref, r1_ref, r2_ref = sets[l]
                g = ((s_ref[...].astype(f32) + r0_ref[...].astype(f32)) + r1_ref[...].astype(f32)) + r2_ref[...].astype(f32)
                g_ref[...] = g
                d_ref[...], mo_ref[...], vo_ref[...] = _adam_math(w_ref[...], g, m_ref[...], v_ref[...])

    blk = _bs((None, tr, c), lambda l, i, cr: (l, i, 0))
    operands, specs = [], []
    for n in range(2):
        row = (lambda l, i: i * (1 - l)) if n == 0 else (lambda l, i: i * l)
        sums, r2 = parts[n]
        operands += [sums, r2, r2, r2]
        specs.append(_bs((None, None, tr, c), functools.partial(lambda l, i, cr, row: (cr[0], 0, row(l, i), 0), row=row)))
        specs += [_bs((None, None, tr, c), functools.partial(lambda l, i, cr, k, row: (k, 0, row(l, i), 0), k=k, row=row))
                  for k in range(3)]
    return pl.pallas_call(
        body, name=name,
        grid_spec=pltpu.PrefetchScalarGridSpec(num_scalar_prefetch=1, grid=(2, r // tr), in_specs=[blk, blk, blk] + specs,
                                               out_specs=[blk] * 4),
        out_shape=[S(w.shape, f32)] * 4, compiler_params=_cp("arbitrary", "arbitrary"),
    )(chip, w, m, v, *operands)


def _adamw_small(name, w, g, m, v):
    def body(w_ref, g_ref, m_ref, v_ref, d_ref, mo_ref, vo_ref):
        d_ref[...], mo_ref[...], vo_ref[...] = _adam_math(w_ref[...], g_ref[...], m_ref[...], v_ref[...])

    return pl.pallas_call(body, name=name, out_shape=[S(w.shape, f32)] * 3,
                          compiler_params=pltpu.CompilerParams(vmem_limit_bytes=VMEM_LIMIT))(w, g, m, v)


WEIGHT_NAMES = ("ffn1_norm", "ffn1_w_gu", "ffn1_w_down", "mix_norm", "w_in", "b_forget", "b_gate", "conv_w", "sgu_ln_g",
                "sgu_ln_b", "sgu_w", "sgu_b", "q_norm_g", "k_norm_g", "w_out_conv", "w_out_sgu", "w_out_attn", "w_o",
                "ffn2_norm", "ffn2_w_gu", "ffn2_w_down")
BIG = {"ffn1_w_gu": "gu1", "ffn2_w_gu": "gu2", "ffn1_w_down": "d1", "ffn2_w_down": "d2", "w_in": "in",
       "w_out_conv": "oc", "w_out_sgu": "os", "w_out_attn": "oa", "w_o": "o"}
BIG_KEYS = ("gu1", "gu2", "d1", "d2", "in", "oc", "os", "oa", "o")
REPLICATED_SMALL = ("ffn1_norm", "mix_norm", "b_forget", "sgu_ln_g", "sgu_ln_b", "sgu_w", "sgu_b", "q_norm_g",
                    "k_norm_g", "ffn2_norm")
SHARDED_SMALL = ("b_gate", "conv_w")
TRANSPOSED = ("gu1", "gu2")


def _packed_rows(shape):
    size = 1
    for s_ in shape:
        size *= s_
    return size, -(-size // 1024) * 8


def _pack(arrays):
    pieces = []
    for a in arrays:
        size, rows = _packed_rows(a.shape)
        pieces.append(jnp.pad(a.reshape(-1).astype(f32), (0, rows * 128 - size)).reshape(rows, 128))
    return jnp.concatenate(pieces, axis=0)


def _unpack(packed, shapes):
    out, pos = [], 0
    for shp in shapes:
        size, rows = _packed_rows(shp)
        out.append(packed[pos:pos + rows].reshape(-1)[:size].reshape(shp))
        pos += rows
    return out


def _natural_runs(a, b):
    runs = []
    while a < b:
        d = a // INB
        e = min(b, (d + 1) * INB)
        runs.append((d, a - d * INB, e - d * INB))
        a = e
    return runs


def _win_kernel_layout(wg):
    runs = _natural_runs(0, GATE_OFF) + _natural_runs(GATE_OFF + NH, NIN) + _natural_runs(GATE_OFF, GATE_OFF + NH)
    return jnp.concatenate([wg[d, :, a:b] for d, a, b in runs] + [jnp.zeros((D, NZ - NIN), wg.dtype)], axis=1)


def _kernel_column(n):
    return n if n < GATE_OFF else (F_OFF + n - GATE_OFF if n < GATE_OFF + NH else n - NH)


def _win_device_block(dw, d):
    cuts = sorted({d * INB, (d + 1) * INB} | {c for c in (GATE_OFF, GATE_OFF + NH) if d * INB < c < (d + 1) * INB})
    parts = [dw[:, _kernel_column(a):_kernel_column(a) + (b - a)] for a, b in zip(cuts[:-1], cuts[1:])]
    return parts[0] if len(parts) == 1 else jnp.concatenate(parts, axis=1)


def kernel(x, ffn1_norm, ffn1_w_gu, ffn1_w_down, mix_norm, w_in, b_forget, b_gate, conv_w, sgu_ln_g, sgu_ln_b, sgu_w, sgu_b, q_norm_g, k_norm_g, w_out_conv, w_out_sgu, w_out_attn, w_o, ffn2_norm, ffn2_w_gu, ffn2_w_down, loss_target, m_ffn1_norm, m_ffn1_w_gu, m_ffn1_w_down, m_mix_norm, m_w_in, m_b_forget, m_b_gate, m_conv_w, m_sgu_ln_g, m_sgu_ln_b, m_sgu_w, m_sgu_b, m_q_norm_g, m_k_norm_g, m_w_out_conv, m_w_out_sgu, m_w_out_attn, m_w_o, m_ffn2_norm, m_ffn2_w_gu, m_ffn2_w_down, v_ffn1_norm, v_ffn1_w_gu, v_ffn1_w_down, v_mix_norm, v_w_in, v_b_forget, v_b_gate, v_conv_w, v_sgu_ln_g, v_sgu_ln_b, v_sgu_w, v_sgu_b, v_q_norm_g, v_k_norm_g, v_w_out_conv, v_w_out_sgu, v_w_out_attn, v_w_o, v_ffn2_norm, v_ffn2_w_gu, v_ffn2_w_down):
    w = dict(zip(WEIGHT_NAMES, (ffn1_norm, ffn1_w_gu, ffn1_w_down, mix_norm, w_in, b_forget, b_gate, conv_w, sgu_ln_g,
                                sgu_ln_b, sgu_w, sgu_b, q_norm_g, k_norm_g, w_out_conv, w_out_sgu, w_out_attn, w_o,
                                ffn2_norm, ffn2_w_gu, ffn2_w_down)))
    mom = dict(zip(WEIGHT_NAMES, (m_ffn1_norm, m_ffn1_w_gu, m_ffn1_w_down, m_mix_norm, m_w_in, m_b_forget, m_b_gate,
                                  m_conv_w, m_sgu_ln_g, m_sgu_ln_b, m_sgu_w, m_sgu_b, m_q_norm_g, m_k_norm_g,
                                  m_w_out_conv, m_w_out_sgu, m_w_out_attn, m_w_o, m_ffn2_norm, m_ffn2_w_gu,
                                  m_ffn2_w_down)))
    var = dict(zip(WEIGHT_NAMES, (v_ffn1_norm, v_ffn1_w_gu, v_ffn1_w_down, v_mix_norm, v_w_in, v_b_forget, v_b_gate,
                                  v_conv_w, v_sgu_ln_g, v_sgu_ln_b, v_sgu_w, v_sgu_b, v_q_norm_g, v_k_norm_g,
                                  v_w_out_conv, v_w_out_sgu, v_w_out_attn, v_w_o, v_ffn2_norm, v_ffn2_w_gu,
                                  v_ffn2_w_down)))
    px, py, pc = _place()
    dev = 4 * px + 2 * py + pc
    chip = 2 * px + py

    big_names = [n for n in WEIGHT_NAMES if n in BIG]
    key_name = {BIG[n]: n for n in big_names}
    group_keys = (("gu1", "d1"), ("in", "oc", "os", "oa", "o", "small"), ("gu2", "d2"))

    def source(key, l):
        if key == "small":
            return jnp.concatenate([w["b_gate"][l], w["conv_w"][l], jnp.zeros((2, 128), f32)], axis=0)
        block = w[key_name[key]][l]
        return (block.T if key in TRANSPOSED else block).astype(bf16)

    def landing(src):
        return lax.dynamic_update_slice(lax.empty((NDEV,) + src.shape, src.dtype), src[None], (dev, 0, 0))

    groups = [[(s, landing(s)) for s in (source(k, l) for k in keys)] for l in range(2) for keys in group_keys]
    (send0, recv0, srcs0, lands0), = _gather_start("gather_start_first", groups[:1])
    rest, src0 = _gather_start("gather_start_rest", groups[1:], carry=srcs0[0])
    started = [(send0, recv0, [src0] + list(srcs0[1:]), lands0)] + rest

    def weights(l, part, after):
        got = _gather_wait(f"gather_wait_{l}_{part}", started[3 * l + part], after)
        return dict(zip(group_keys[part], _gather_forward(f"gather_forward_{l}_{part}", got)))

    xl = x[0]
    saved, small, wts = [], [], []
    for l in range(2):
        ga = weights(l, 0, xl if l else None)
        wt = {"gu1": ga["gu1"][None], "d1": ga["d1"].reshape(1, 4, GU, D)}
        x1, s1 = _ffn_fwd("ffn1", xl, w["ffn1_norm"][l].reshape(1, D), wt["gu1"], wt["d1"], 0)
        gb = weights(l, 1, x1)
        p = {n: w[n][l] for n in REPLICATED_SMALL}
        p["b_gate"] = jnp.transpose(gb["small"][:, 0:3, :], (1, 0, 2)).reshape(3, D)
        p["conv_w"] = jnp.transpose(gb["small"][:, 3:6, :], (1, 0, 2)).reshape(3, D)
        p = _small_params(p)
        wt["win"] = _win_kernel_layout(gb["in"])
        wt["sq"] = jnp.stack([gb[k].reshape(D, D) for k in ("oc", "os", "oa", "o")])[None]
        x2, sm = _mixer_fwd("mix", x1, p, wt["win"], wt["sq"], 0)
        gc = weights(l, 2, x2)
        wt.update({"gu2": gc["gu2"][None], "d2": gc["d2"].reshape(1, 4, GU, D)})
        x3, s2 = _ffn_fwd("ffn2", x2, p["ffn2_norm"], wt["gu2"], wt["d2"], 0)
        saved.append((xl, x1, x2, s1, sm, s2))
        small.append(p)
        wts.append(wt)
        xl = x3
    loss_row, dx = _loss("loss", xl, loss_target[0])

    core = pc.reshape(1).astype(jnp.int32)
    chip_op = chip.reshape(1).astype(jnp.int32)
    buf = lambda r, c: lax.empty((2, 4, 1, r, c), bf16)
    flights = {}

    pairs = []

    def ship(l, part, bufs, dx):
        started, dx = _pair_start(f"pair_start_{l}_{part}", bufs, dx)
        pairs.append((l, part, started))
        return dx

    def finish(dx, after):
        l, part, started = pairs.pop()
        bufs, r1 = _pair_wait(f"pair_wait_{l}_{part}", started, after)
        ss = [_pair_sum(f"pair_sum_{l}_{k}", core, g, r) for k, g, r in zip(group_keys[part], bufs, r1)]
        flights[l, part], dx = _scatter_start(f"scatter_start_{l}_{part}", ss, dx)
        return dx

    def ship_mixer(l, dwin, gsq, dx):
        g_in = jnp.stack([_win_device_block(dwin, 2 * q + c) for c in range(2) for q in range(4)]).reshape(2, 4, 1, D, INB)
        return ship(l, 1, [g_in] + gsq, dx)

    sgrads = [None, None]
    for l in (1, 0):
        p, wt = small[l], wts[l]
        x0, x1, x2, s1, sm, s2 = saved[l]
        dx, dn2 = _ffn_bwd("ffn2", dx, x2, p["ffn2_norm"], wt["gu2"], wt["d2"], 0, s2, buf(GU, D), buf(GU // 2, D),
                           functools.partial(ship, l, 2), finish)
        dx, sg = _mixer_bwd("mix", dx, x1, p, wt["win"], wt["sq"], 0, sm, [buf(128, D) for _ in range(4)],
                            functools.partial(ship_mixer, l), finish)
        dx, dn1 = _ffn_bwd("ffn1", dx, x0, p["ffn1_norm"], wt["gu1"], wt["d1"], 0, s1, buf(GU, D), buf(GU // 2, D),
                           functools.partial(ship, l, 0), finish)
        sg["ffn1_norm"] = dn1
        sg["ffn2_norm"] = dn2
        sgrads[l] = sg

    nat = [_small_grads_natural(sgrads[l]) for l in range(2)]
    order = REPLICATED_SMALL + SHARDED_SMALL
    part = _pack([jnp.stack([nat[0][n], nat[1][n]]) for n in order] + [loss_row[0, 0:1]])
    small_flight, dx = _gather_start("small_start", [[(part, landing(part))]], carry=dx)

    grads, delta, new_m, new_v = {}, {}, {}, {}
    after = dx
    for part in (2, 1, 0):
        sets = []
        for l in (1, 0):
            s_all, r2_all = _scatter_wait(f"scatter_wait_{l}_{part}", flights[l, part], after)
            sets.append(list(zip(s_all, r2_all)))
        for i, k in enumerate(k for k in group_keys[part] if k != "small"):
            n = key_name[k]
            view = (lambda a: jnp.swapaxes(a, 1, 2)) if k in TRANSPOSED else (lambda a: a)
            outs = _adamw("adamw_" + k, chip_op, view(w[n]), view(mom[n]), view(var[n]), [sets[1][i], sets[0][i]])
            grads[n], delta[n], new_m[n], new_v[n] = [view(o) for o in outs]
            after = outs[1]

    blocks = _gather_forward("small_forward", _gather_wait("small_wait", small_flight[0], after))
    total = _sum_blocks("small_sum", blocks[0])
    full_shapes = [(2,) + tuple(nat[0][n].shape) for n in order] + [(1,)]
    summed = dict(zip(order + ("loss",), _unpack(total, full_shapes)))
    for n in REPLICATED_SMALL:
        grads[n] = summed[n]
    for n in SHARDED_SMALL:
        grads[n] = lax.dynamic_slice_in_dim(summed[n], dev * 128, 128, axis=2)
    wp = _pack([w[n] for n in order])
    gp = _pack([grads[n] for n in order])
    mp = _pack([mom[n] for n in order])
    vp = _pack([var[n] for n in order])
    dpk, mpk, vpk = _adamw_small("adamw_small", wp, gp, mp, vp)
    local_shapes = [tuple(w[n].shape) for n in order]
    for dst, packed in ((delta, dpk), (new_m, mpk), (new_v, vpk)):
        dst.update(zip(order, _unpack(packed, local_shapes)))

    loss = summed["loss"][0]
    return (loss, dx[None], *[grads[n] for n in WEIGHT_NAMES], *[delta[n] for n in WEIGHT_NAMES],
            *[new_m[n] for n in WEIGHT_NAMES], *[new_v[n] for n in WEIGHT_NAMES])
```

```python
import functools

import jax
import jax.numpy as jnp
from jax import lax
from jax.experimental import pallas as pl
from jax.experimental.pallas import tpu as pltpu

f32 = jnp.float32
bf16 = jnp.bfloat16
S = jax.ShapeDtypeStruct
MESH = pl.DeviceIdType.MESH

D = 1024
NH = 8
HD = 128
NDEV = 8
GU = 704
NIN = 11272
INB = 1409
GATE_OFF = 8192
F_OFF = 11264
NZ = 11776
RMS_EPS = 1e-6
LN_EPS = 1e-5
ATT_SCALE = HD ** -0.5
NEG = -1e30
INV_SQRT2 = 0.7071067811865476
INV_SQRT2PI = 0.3989422804014327

ADAM_LR = 0.001
ADAM_B1 = 0.9
ADAM_B2 = 0.999
ADAM_EPS = 1e-08
ADAM_WD = 0.01
ADAM_STEP = 10

TT = 512
VMEM_LIMIT = 56 * 1024 * 1024


def _cp(*sem):
    return pltpu.CompilerParams(dimension_semantics=sem, vmem_limit_bytes=VMEM_LIMIT)


def _bs(shape, fn):
    return pl.BlockSpec(shape, fn)


NN = (((1,), (0,)), ((), ()))
NT = (((1,), (1,)), ((), ()))
TN = (((0,), (0,)), ((), ()))


def _mm(name, a, b, *, grid, a_spec, b_spec, out_shape, out_spec, dims, acc_shape, res=None, res_spec=None,
        alpha=1.0, alias=None, split_rows=None):
    nk = grid[2]

    def body(*refs):
        a_ref, b_ref = refs[0], refs[1]
        pos = 2
        res_ref = None
        if res is not None:
            res_ref = refs[pos]
            pos += 1
        if alias is not None:
            pos += 1
        o_ref = refs[pos]
        acc_ref = refs[pos + 1] if nk > 1 else None
        part = lax.dot_general(a_ref[...].astype(bf16), b_ref[...].astype(bf16), dims, preferred_element_type=f32)

        def finish(acc):
            if alpha != 1.0:
                acc = alpha * acc
            if res_ref is not None:
                acc = res_ref[...] + acc
            if split_rows is None:
                o_ref[...] = acc.astype(o_ref.dtype)
            else:
                o_ref[0] = acc[:split_rows].astype(o_ref.dtype)
                o_ref[1] = acc[split_rows:].astype(o_ref.dtype)

        if nk == 1:
            finish(part)
        else:
            k = pl.program_id(2)

            @pl.when(k == 0)
            def _():
                acc_ref[...] = part

            @pl.when(k > 0)
            def _():
                acc_ref[...] += part

            @pl.when(k == nk - 1)
            def _():
                finish(acc_ref[...])

    operands = [a, b]
    in_specs = [a_spec, b_spec]
    if res is not None:
        operands.append(res)
        in_specs.append(res_spec)
    aliases = {}
    if alias is not None:
        aliases = {len(operands): 0}
        operands.append(alias)
        in_specs.append(pl.BlockSpec(memory_space=pl.ANY))
    return pl.pallas_call(
        body, name=name, grid=grid, in_specs=in_specs, out_specs=out_spec, out_shape=out_shape,
        scratch_shapes=[pltpu.VMEM(acc_shape, f32)] if nk > 1 else [],
        input_output_aliases=aliases,
        compiler_params=_cp("parallel", "parallel", "arbitrary"),
    )(*operands)


def _tile(n, t):
    return t if n % t == 0 and n >= t else n


DZ_TILE = 512


def _dz_matmul(name, pieces, other, weight_grad):
    T = pieces[0].shape[0]
    counts = [p.shape[1] // DZ_TILE for p in pieces]
    starts = [sum(counts[:i]) for i in range(len(counts))]
    steps = sum(counts)
    npc = len(pieces)

    def body(*refs):
        prefs, o_ref, rest = refs[:npc], refs[npc], refs[npc + 1:]
        k = pl.program_id(0)
        if not weight_grad:
            out_ref, acc_ref = rest

            @pl.when(k == 0)
            def _():
                acc_ref[...] = jnp.zeros_like(acc_ref)

        for p_ref, s, c in zip(prefs, starts, counts):
            @pl.when((k >= s) & (k < s + c))
            def _():
                if weight_grad:
                    rest[0][...] = lax.dot_general(o_ref[...], p_ref[...], TN, preferred_element_type=f32).astype(bf16)
                else:
                    acc_ref[...] += lax.dot_general(p_ref[...], o_ref[...], NT, preferred_element_type=f32)

        if not weight_grad:
            @pl.when(k == steps - 1)
            def _():
                out_ref[...] = acc_ref[...]

    piece_specs = [_bs((T, DZ_TILE), functools.partial(lambda k, s, c: (0, jnp.clip(k - s, 0, c - 1)), s=s, c=c))
                   for s, c in zip(starts, counts)]
    if weight_grad:
        other_spec, out_spec, out_shape, scratch = _bs((T, D), lambda k: (0, 0)), _bs((D, DZ_TILE), lambda k: (0, k)), S((D, NZ), bf16), []
    else:
        other_spec, out_spec, out_shape = _bs((D, DZ_TILE), lambda k: (0, k)), _bs((T, D), lambda k: (0, 0)), S((T, D), f32)
        scratch = [pltpu.VMEM((T, D), f32)]
    return pl.pallas_call(body, name=name, grid=(steps,), in_specs=piece_specs + [other_spec], out_specs=out_spec,
                          out_shape=out_shape, scratch_shapes=scratch, compiler_params=_cp("arbitrary"))(*pieces, other)


def _row(cb=0, w=D):
    return _bs((TT, w), lambda i: (i, cb))


def _vec(rows=1, w=D):
    return _bs((rows, w), lambda i: (0, 0))


def _acc_store(i, ref, val):
    @pl.when(i == 0)
    def _():
        ref[...] = val

    @pl.when(i > 0)
    def _():
        ref[...] += val


def _rms_fwd(name, x, g):
    T = x.shape[0]

    def body(x_ref, g_ref, o_ref):
        xv = x_ref[...]
        r = lax.rsqrt(jnp.mean(xv * xv, axis=-1, keepdims=True) + RMS_EPS)
        o_ref[...] = (xv * r * g_ref[...]).astype(bf16)

    return pl.pallas_call(body, name=name, grid=(T // TT,), in_specs=[_row(), _vec()], out_specs=_row(),
                          out_shape=S((T, D), bf16), compiler_params=_cp("parallel"))(x, g)


def _rms_bwd(name, dh, x, g, dres):
    T = x.shape[0]

    def body(dh_ref, x_ref, g_ref, dres_ref, dx_ref, dg_ref):
        i = pl.program_id(0)
        xv = x_ref[...]
        r = lax.rsqrt(jnp.mean(xv * xv, axis=-1, keepdims=True) + RMS_EPS)
        xhat = xv * r
        dh_v = dh_ref[...]
        dyg = dh_v * g_ref[...]
        m = jnp.mean(dyg * xhat, axis=-1, keepdims=True)
        dx_ref[...] = dres_ref[...] + r * (dyg - xhat * m)
        _acc_store(i, dg_ref, jnp.sum(dh_v * xhat, axis=0, keepdims=True))

    return pl.pallas_call(body, name=name, grid=(T // TT,), in_specs=[_row(), _row(), _vec(), _row()],
                          out_specs=[_row(), _vec()], out_shape=[S((T, D), f32), S((1, D), f32)],
                          compiler_params=_cp("arbitrary"))(dh, x, g, dres)


def _sigmoid(x):
    return 1.0 / (1.0 + jnp.exp(-x))


def _swiglu_fwd(name, h, wgu, l):
    T = h.shape[0]

    def body(h_ref, wg_ref, wu_ref, gu_ref, a_ref):
        hv = h_ref[...]
        g = lax.dot_general(hv, wg_ref[...], NT, preferred_element_type=f32)
        u = lax.dot_general(hv, wu_ref[...], NT, preferred_element_type=f32)
        gu_ref[0] = g
        gu_ref[1] = u
        a_ref[...] = (g * _sigmoid(g) * u).astype(bf16)

    return pl.pallas_call(
        body, name=name, grid=(4,),
        in_specs=[_bs((T, D), lambda j: (0, 0)), _bs((None, None, GU, D), lambda j: (l, j, 0, 0)),
                  _bs((None, None, GU, D), lambda j: (l, j + 4, 0, 0))],
        out_specs=[_bs((2, None, T, GU), lambda j: (0, j, 0, 0)), _bs((None, T, GU), lambda j: (j, 0, 0))],
        out_shape=[S((2, 4, T, GU), f32), S((4, T, GU), bf16)], compiler_params=_cp("parallel"))(h, wgu, wgu)


def _swiglu_bwd(name, dxo, wd, gu, l):
    T = dxo.shape[0]
    tm = _tile(T, 1024)

    def body(dx_ref, wd_ref, g_ref, u_ref, o_ref):
        da = 0.5 * lax.dot_general(dx_ref[...].astype(bf16), wd_ref[...], NT, preferred_element_type=f32)
        g = g_ref[...]
        sg = _sigmoid(g)
        o_ref[0] = (da * u_ref[...] * (sg + g * sg * (1.0 - sg))).astype(bf16)
        o_ref[1] = (da * g * sg).astype(bf16)

    return pl.pallas_call(
        body, name=name, grid=(T // tm, 4),
        in_specs=[_bs((tm, D), lambda i, j: (i, 0)), _bs((None, None, GU, D), lambda i, j: (l, j, 0, 0)),
                  _bs((None, None, tm, GU), lambda i, j: (0, j, i, 0)), _bs((None, None, tm, GU), lambda i, j: (1, j, i, 0))],
        out_specs=_bs((2, None, tm, GU), lambda i, j: (0, j, i, 0)), out_shape=S((2, 4, T, GU), bf16),
        compiler_params=_cp("parallel", "parallel"))(dxo, wd, gu, gu)


def _loss(name, y, tgt):
    T = y.shape[0]

    def body(y_ref, t_ref, l_ref, dy_ref):
        i = pl.program_id(0)
        e = y_ref[...] - t_ref[...]
        dy_ref[...] = e * (1.0 / D)
        s = 0.5 * jnp.sum(jnp.mean(e * e, axis=-1, keepdims=True))
        _acc_store(i, l_ref, jnp.broadcast_to(s, (1, 128)))

    return pl.pallas_call(body, name=name, grid=(T // TT,), in_specs=[_row(), _row()],
                          out_specs=[_vec(1, 128), _row()], out_shape=[S((1, 128), f32), S((T, D), f32)],
                          compiler_params=_cp("arbitrary"))(y, tgt)


def _prev8(T, cb):
    return _bs((8, D), lambda i: (jnp.maximum(i * (TT // 8) - 1, 0), cb))


def _next8(T, cb):
    return _bs((8, D), lambda i: (jnp.minimum((i + 1) * (TT // 8), T // 8 - 1), cb))


def _conv_taps(i, ac_ref, ax_ref, pc_ref, px_ref):
    ca = ac_ref[...] * ax_ref[...]
    keep = (i > 0).astype(f32)
    p1 = pc_ref[7:8, :] * px_ref[7:8, :] * keep
    p2 = pc_ref[6:7, :] * px_ref[6:7, :] * keep
    row = lax.broadcasted_iota(jnp.int32, ca.shape, 0)
    s1 = jnp.where(row == 0, p1, pltpu.roll(ca, 1, 0))
    s2 = jnp.where(row == 0, p2, jnp.where(row == 1, p1, pltpu.roll(ca, 2, 0)))
    return ca, s1, s2


def _conv_fwd(name, z, cw):
    T = z.shape[0]

    def body(ab_ref, ac_ref, ax_ref, pc_ref, px_ref, w_ref, o_ref):
        i = pl.program_id(0)
        ca, s1, s2 = _conv_taps(i, ac_ref, ax_ref, pc_ref, px_ref)
        cv = w_ref[0:1, :] * s2 + w_ref[1:2, :] * s1 + w_ref[2:3, :] * ca
        o_ref[...] = (ab_ref[...] * cv).astype(bf16)

    return pl.pallas_call(
        body, name=name, grid=(T // TT,),
        in_specs=[_row(0), _row(1), _row(2), _prev8(T, 1), _prev8(T, 2), _vec(3)],
        out_specs=_row(), out_shape=S((T, D), bf16), compiler_params=_cp("parallel"))(z, z, z, z, z, cw)


def _conv_bwd(name, dya, z, cw):
    T = z.shape[0]
    n = T // TT

    def body(dya_ref, ab_ref, ac_ref, ax_ref, pc_ref, px_ref, ndya_ref, nab_ref, w_ref, dz_ref, dw_ref):
        i = pl.program_id(0)
        ca, s1, s2 = _conv_taps(i, ac_ref, ax_ref, pc_ref, px_ref)
        w0, w1, w2 = w_ref[0:1, :], w_ref[1:2, :], w_ref[2:3, :]
        cv = w0 * s2 + w1 * s1 + w2 * ca
        dya_v = dya_ref[...]
        ab = ab_ref[...]
        dcv = dya_v * ab
        keep = (i < n - 1).astype(f32)
        n1 = ndya_ref[0:1, :] * nab_ref[0:1, :] * keep
        n2 = ndya_ref[1:2, :] * nab_ref[1:2, :] * keep
        row = lax.broadcasted_iota(jnp.int32, dcv.shape, 0)
        f1 = jnp.where(row == TT - 1, n1, pltpu.roll(dcv, TT - 1, 0))
        f2 = jnp.where(row == TT - 1, n2, jnp.where(row == TT - 2, n1, pltpu.roll(dcv, TT - 2, 0)))
        dca = w2 * dcv + w1 * f1 + w0 * f2
        dz_ref[:, 0:D] = (dya_v * cv).astype(bf16)
        dz_ref[:, D:2 * D] = (dca * ax_ref[...]).astype(bf16)
        dz_ref[:, 2 * D:3 * D] = (dca * ac_ref[...]).astype(bf16)
        dw = jnp.concatenate([jnp.sum(dcv * s2, axis=0, keepdims=True), jnp.sum(dcv * s1, axis=0, keepdims=True),
                              jnp.sum(dcv * ca, axis=0, keepdims=True)], axis=0)
        _acc_store(i, dw_ref, dw)

    return pl.pallas_call(
        body, name=name, grid=(n,),
        in_specs=[_row(), _row(0), _row(1), _row(2), _prev8(T, 1), _prev8(T, 2), _next8(T, 0), _next8(T, 0), _vec(3)],
        out_specs=[_row(0, 3 * D), _vec(3)], out_shape=[S((T, 3 * D), bf16), S((3, D), f32)],
        compiler_params=_cp("arbitrary"))(dya, z, z, z, z, z, dya, z, cw)


def _gelu(x):
    return 0.5 * x * (1.0 + lax.erf(x * INV_SQRT2))


def _gelu_cdf(x):
    return 0.5 * (1.0 + lax.erf(x * INV_SQRT2))


def _gelu_grad(x, cdf):
    return cdf + x * jnp.exp(-0.5 * x * x) * INV_SQRT2PI


def _ln_stats(vv):
    mu = jnp.mean(vv, axis=-1, keepdims=True)
    xc = vv - mu
    rstd = lax.rsqrt(jnp.mean(xc * xc, axis=-1, keepdims=True) + LN_EPS)
    return xc * rstd, rstd


def _tril_w(w_ref, g):
    r = lax.broadcasted_iota(jnp.int32, (HD, HD), 0)
    c = lax.broadcasted_iota(jnp.int32, (HD, HD), 1)
    return jnp.where(c <= r, w_ref[g], 0.0).astype(bf16)


def _sgu_fwd(name, z, ln_g, ln_b, w_s, bmap):
    T = z.shape[0]

    def body(su_ref, sv_ref, lg_ref, lb_ref, w_ref, bm_ref, o_ref, vn_ref):
        xhat, _ = _ln_stats(_gelu(sv_ref[...]))
        vn_ref[...] = (xhat * lg_ref[...] + lb_ref[...]).astype(bf16)
        for g in range(NH):
            w = _tril_w(w_ref, g)
            cs = slice(g * HD, (g + 1) * HD)
            for c in range(TT // HD):
                rs = slice(c * HD, (c + 1) * HD)
                s = jnp.dot(w, vn_ref[rs, cs], preferred_element_type=f32) + bm_ref[:, cs]
                o_ref[rs, cs] = (_gelu(su_ref[rs, cs]) * s).astype(bf16)

    return pl.pallas_call(
        body, name=name, grid=(T // TT,),
        in_specs=[_row(3), _row(4), _vec(), _vec(), _bs((NH, HD, HD), lambda i: (0, 0, 0)), _vec(HD)],
        out_specs=_row(), out_shape=S((T, D), bf16), scratch_shapes=[pltpu.VMEM((TT, D), bf16)],
        compiler_params=_cp("parallel"))(z, z, ln_g, ln_b, w_s, bmap)


def _sgu_bwd(name, dyb, z, ln_g, ln_b, w_s, bmap):
    T = z.shape[0]

    def body(dyb_ref, su_ref, sv_ref, lg_ref, lb_ref, w_ref, bm_ref, dz_ref, dlg_ref, dlb_ref, dw_ref, db_ref,
             vn_ref, du_ref, dvn_ref, cu_ref, cv_ref):
        i = pl.program_id(0)
        sv = sv_ref[...]
        cv_ref[...] = _gelu_cdf(sv)
        cu_ref[...] = _gelu_cdf(su_ref[...])
        xhat, rstd = _ln_stats(sv * cv_ref[...])
        vn_ref[...] = (xhat * lg_ref[...] + lb_ref[...]).astype(bf16)
        r = lax.broadcasted_iota(jnp.int32, (HD, HD), 0)
        cc = lax.broadcasted_iota(jnp.int32, (HD, HD), 1)
        for g in range(NH):
            w = _tril_w(w_ref, g)
            cs = slice(g * HD, (g + 1) * HD)
            dw = jnp.zeros((HD, HD), f32)
            db = jnp.zeros((HD, 1), f32)
            for c in range(TT // HD):
                rs = slice(c * HD, (c + 1) * HD)
                vnb = vn_ref[rs, cs]
                s = jnp.dot(w, vnb, preferred_element_type=f32) + bm_ref[:, cs]
                dy = dyb_ref[rs, cs]
                du_ref[rs, cs] = dy * s
                ds = dy * (su_ref[rs, cs] * cu_ref[rs, cs])
                ds16 = ds.astype(bf16)
                dvn_ref[rs, cs] = lax.dot_general(w, ds16, TN, preferred_element_type=f32)
                dw = dw + lax.dot_general(ds16, vnb, NT, preferred_element_type=f32)
                db = db + jnp.sum(ds, axis=1, keepdims=True)
            dw = jnp.where(cc <= r, dw, 0.0)

            @pl.when(i == 0)
            def _():
                dw_ref[g] = dw
                db_ref[:, g:g + 1] = db

            @pl.when(i > 0)
            def _():
                dw_ref[g] += dw
                db_ref[:, g:g + 1] += db

        dvn = dvn_ref[...]
        dxh = dvn * lg_ref[...]
        m1 = jnp.mean(dxh, axis=-1, keepdims=True)
        m2 = jnp.mean(dxh * xhat, axis=-1, keepdims=True)
        dvv = rstd * (dxh - m1 - xhat * m2)
        dz_ref[:, 0:D] = (du_ref[...] * _gelu_grad(su_ref[...], cu_ref[...])).astype(bf16)
        dz_ref[:, D:2 * D] = (dvv * _gelu_grad(sv, cv_ref[...])).astype(bf16)
        _acc_store(i, dlg_ref, jnp.sum(dvn * xhat, axis=0, keepdims=True))
        _acc_store(i, dlb_ref, jnp.sum(dvn, axis=0, keepdims=True))

    return pl.pallas_call(
        body, name=name, grid=(T // TT,),
        in_specs=[_row(), _row(3), _row(4), _vec(), _vec(), _bs((NH, HD, HD), lambda i: (0, 0, 0)), _vec(HD)],
        out_specs=[_row(0, 2 * D), _vec(), _vec(), _bs((NH, HD, HD), lambda i: (0, 0, 0)), _bs((HD, NH), lambda i: (0, 0))],
        out_shape=[S((T, 2 * D), bf16), S((1, D), f32), S((1, D), f32), S((NH, HD, HD), f32), S((HD, NH), f32)],
        scratch_shapes=[pltpu.VMEM((TT, D), bf16)] + [pltpu.VMEM((TT, D), f32)] * 4,
        compiler_params=_cp("arbitrary"))(dyb, z, z, ln_g, ln_b, w_s, bmap)


def _qk_fwd(name, z, qg, kg, bf):
    T = z.shape[0]

    def body(q_ref, k_ref, v_ref, zf_ref, qg_ref, kg_ref, bf_ref, qn_ref, kn_ref, vb_ref, lf_ref):
        for h in range(NH):
            cs = slice(h * HD, (h + 1) * HD)
            for src, gain, dst in ((q_ref, qg_ref, qn_ref), (k_ref, kg_ref, kn_ref)):
                xv = src[:, cs]
                r = lax.rsqrt(jnp.mean(xv * xv, axis=-1, keepdims=True) + RMS_EPS)
                dst[:, cs] = (xv * r * gain[:, cs]).astype(bf16)
        vb_ref[...] = v_ref[...].astype(bf16)
        xf = zf_ref[...] + bf_ref[...]
        lf_ref[...] = jnp.minimum(xf, 0.0) - jnp.log1p(jnp.exp(-jnp.abs(xf)))

    return pl.pallas_call(
        body, name=name, grid=(T // TT,),
        in_specs=[_row(5), _row(6), _row(7), _bs((TT, 128), lambda i: (i, F_OFF // 128)), _vec(), _vec(), _vec(1, 128)],
        out_specs=[_row(), _row(), _row(), _bs((TT, 128), lambda i: (i, 0))],
        out_shape=[S((T, D), bf16), S((T, D), bf16), S((T, D), bf16), S((T, 128), f32)],
        compiler_params=_cp("parallel"))(z, z, z, z, qg, kg, bf)


def _cum_fwd(name, logf):
    T = logf.shape[0]

    def body(lf_ref, ccol_ref, crow_ref, c_ref):
        c = lf_ref[...]
        row = lax.broadcasted_iota(jnp.int32, c.shape, 0)
        s = 1
        while s < T:
            c = c + jnp.where(row >= s, pltpu.roll(c, s, 0), 0.0)
            s *= 2
        c_ref[...] = c
        crow_ref[...] = c.T[0:NH, :]
        for h in range(NH):
            ccol_ref[h] = jnp.broadcast_to(c_ref[:, h:h + 1], (T, 128))

    return pl.pallas_call(body, name=name, out_shape=[S((NH, T, 128), f32), S((NH, T), f32)],
                          scratch_shapes=[pltpu.VMEM((T, 128), f32)],
                          compiler_params=pltpu.CompilerParams(vmem_limit_bytes=VMEM_LIMIT))(logf)


ATT_TILE = 1024


def _fold(x, op=jnp.add):
    acc = x[:, 0:128]
    for t in range(1, x.shape[1] // 128):
        acc = op(acc, x[:, t * 128:(t + 1) * 128])
    return acc


def _to_row(col):
    return jnp.broadcast_to(col, (col.shape[0], 128)).T[0:1, :]


def _causal(t, keys_down=False):
    r = lax.broadcasted_iota(jnp.int32, (t, t), 0)
    c = lax.broadcasted_iota(jnp.int32, (t, t), 1)
    return r <= c if keys_down else c <= r


def _attn_fwd(name, qn, kn, vb, ccol, crow3):
    T = qn.shape[0]
    tq = _tile(T, ATT_TILE)
    nq = T // tq

    def body(q_ref, k_ref, v_ref, cc_ref, cr_ref, o_ref, lse_ref, lser_ref, s_ref):
        qi = pl.program_id(1)
        q = q_ref[...]
        cq = cc_ref[:, 0:1]

        def logits(off):
            s = lax.dot_general(q, k_ref[pl.ds(off, tq), :], NT, preferred_element_type=f32) * ATT_SCALE
            return s + cq - cr_ref[:, pl.ds(off, tq)]

        def below(j, mvec):
            off = pl.multiple_of(j * tq, tq)
            s = logits(off)
            s_ref[:, pl.ds(off, tq)] = s
            return jnp.maximum(mvec, _fold(s, jnp.maximum))

        mvec = lax.fori_loop(0, qi, below, jnp.full((tq, 128), NEG, f32))
        off = pl.multiple_of(qi * tq, tq)
        s = jnp.where(_causal(tq), logits(off), NEG)
        s_ref[:, pl.ds(off, tq)] = s
        m = jnp.max(jnp.maximum(mvec, _fold(s, jnp.maximum)), axis=1, keepdims=True)

        def weigh(j, carry):
            lvec, acc = carry
            off = pl.multiple_of(j * tq, tq)
            p = jnp.exp(s_ref[:, pl.ds(off, tq)] - m)
            acc = acc + jnp.dot(p.astype(bf16), v_ref[pl.ds(off, tq), :], preferred_element_type=f32)
            return lvec + _fold(p), acc

        lvec, acc = lax.fori_loop(0, qi + 1, weigh, (jnp.zeros((tq, 128), f32), jnp.zeros((tq, HD), f32)))
        l = jnp.sum(lvec, axis=1, keepdims=True)
        o_ref[...] = acc / l
        lse = m + jnp.log(l)
        lse_ref[...] = jnp.broadcast_to(lse, (tq, 128))
        lser_ref[...] = _to_row(lse)

    return pl.pallas_call(
        body, name=name, grid=(NH, nq),
        in_specs=[_bs((tq, HD), lambda h, i: (i, h)), _bs((T, HD), lambda h, i: (0, h)), _bs((T, HD), lambda h, i: (0, h)),
                  _bs((None, tq, 128), lambda h, i: (h, i, 0)), _bs((None, 1, T), lambda h, i: (h, 0, 0))],
        out_specs=[_bs((tq, HD), lambda h, i: (i, h)), _bs((None, tq, 128), lambda h, i: (h, i, 0)),
                   _bs((None, 1, tq), lambda h, i: (h, 0, i))],
        out_shape=[S((T, D), f32), S((NH, T, 128), f32), S((NH, 1, T), f32)],
        scratch_shapes=[pltpu.VMEM((tq, T), f32)],
        compiler_params=_cp("parallel", "parallel"))(qn, kn, vb, ccol, crow3)


def _attn_dq(name, qn, kn, vb, do, lse, ccol, crow3):
    T = qn.shape[0]
    tq = _tile(T, ATT_TILE)
    nq = T // tq

    def body(q_ref, k_ref, v_ref, do_ref, lse_ref, cc_ref, cr_ref, dq_ref, dlr_ref, p_ref, dp_ref):
        qi = pl.program_id(1)
        q = q_ref[...]
        do16 = do_ref[...].astype(bf16)
        base = cc_ref[:, 0:1] - lse_ref[:, 0:1]

        def probs(off):
            s = lax.dot_general(q, k_ref[pl.ds(off, tq), :], NT, preferred_element_type=f32) * ATT_SCALE
            return jnp.exp(s + base - cr_ref[:, pl.ds(off, tq)])

        def keep(off, p, dvec):
            dp = lax.dot_general(do16, v_ref[pl.ds(off, tq), :], NT, preferred_element_type=f32)
            p_ref[:, pl.ds(off, tq)] = p
            dp_ref[:, pl.ds(off, tq)] = dp
            return dvec + _fold(p * dp)

        def below(j, dvec):
            off = pl.multiple_of(j * tq, tq)
            return keep(off, probs(off), dvec)

        dvec = lax.fori_loop(0, qi, below, jnp.zeros((tq, 128), f32))
        off = pl.multiple_of(qi * tq, tq)
        dvec = keep(off, jnp.where(_causal(tq), probs(off), 0.0), dvec)
        delta = jnp.sum(dvec, axis=1, keepdims=True)

        def grad(j, acc):
            off = pl.multiple_of(j * tq, tq)
            ds = p_ref[:, pl.ds(off, tq)] * (dp_ref[:, pl.ds(off, tq)] - delta)
            return acc + jnp.dot(ds.astype(bf16), k_ref[pl.ds(off, tq), :], preferred_element_type=f32)

        dq_ref[...] = lax.fori_loop(0, qi + 1, grad, jnp.zeros((tq, HD), f32)) * ATT_SCALE
        dlr_ref[...] = _to_row(delta)

    qb = lambda h, i: (i, h)
    full = lambda h, i: (0, h)
    col = lambda h, i: (h, i, 0)
    return pl.pallas_call(
        body, name=name, grid=(NH, nq),
        in_specs=[_bs((tq, HD), qb), _bs((T, HD), full), _bs((T, HD), full), _bs((tq, HD), qb),
                  _bs((None, tq, 128), col), _bs((None, tq, 128), col), _bs((None, 1, T), lambda h, i: (h, 0, 0))],
        out_specs=[_bs((tq, HD), qb), _bs((None, 1, tq), lambda h, i: (h, 0, i))],
        out_shape=[S((T, D), f32), S((NH, 1, T), f32)],
        scratch_shapes=[pltpu.VMEM((tq, T), f32), pltpu.VMEM((tq, T), f32)],
        compiler_params=_cp("parallel", "parallel"))(qn, kn, vb, do, lse, ccol, crow3)


def _attn_dkv(name, qn, kn, vb, do, lser3, dlr3, ccol, crow3):
    T = qn.shape[0]
    tk = _tile(T, ATT_TILE)
    nk = T // tk

    def body(q_ref, k_ref, v_ref, do_ref, lser_ref, dlr_ref, cc_ref, cr_ref, dk_ref, dv_ref, cs_ref):
        h = pl.program_id(0)
        kj = pl.program_id(1)

        @pl.when((h == 0) & (kj == 0))
        def _():
            cs_ref[...] = jnp.zeros_like(cs_ref)

        kb = k_ref[...]
        vv = v_ref[...]
        ckey = cc_ref[:, 0:1]

        def block(off, diagonal):
            rows = pl.ds(off, tk)
            qb = q_ref[rows, :]
            do16 = do_ref[rows, :].astype(bf16)
            st = lax.dot_general(kb, qb, NT, preferred_element_type=f32) * ATT_SCALE
            pt = jnp.exp(st + (cr_ref[:, rows] - lser_ref[:, rows]) - ckey)
            if diagonal:
                pt = jnp.where(_causal(tk, keys_down=True), pt, 0.0)
            dpt = lax.dot_general(vv, do16, NT, preferred_element_type=f32)
            dst = pt * (dpt - dlr_ref[:, rows])
            ddv = jnp.dot(pt.astype(bf16), do16, preferred_element_type=f32)
            ddk = jnp.dot(dst.astype(bf16), qb, preferred_element_type=f32)
            return ddk, ddv, _fold(dst)

        def above(i, carry):
            ddk, ddv, dcs = block(pl.multiple_of(i * tk, tk), False)
            return carry[0] + ddk, carry[1] + ddv, carry[2] + dcs

        off = pl.multiple_of(kj * tk, tk)
        dk, dv, cs = lax.fori_loop(kj + 1, nk, above, block(off, True))
        dk_ref[...] = dk * ATT_SCALE
        dv_ref[...] = dv
        lane = lax.broadcasted_iota(jnp.int32, (tk, 128), 1)
        cs_ref[pl.ds(off, tk), :] += jnp.where(lane == h, jnp.sum(cs, axis=1, keepdims=True), 0.0)

    full = lambda h, j: (0, h)
    blk = lambda h, j: (j, h)
    row = lambda h, j: (h, 0, 0)
    return pl.pallas_call(
        body, name=name, grid=(NH, nk),
        in_specs=[_bs((T, HD), full), _bs((tk, HD), blk), _bs((tk, HD), blk), _bs((T, HD), full), _bs((None, 1, T), row),
                  _bs((None, 1, T), row), _bs((None, tk, 128), lambda h, j: (h, j, 0)), _bs((None, 1, T), row)],
        out_specs=[_bs((tk, HD), blk), _bs((tk, HD), blk), _bs((T, 128), lambda h, j: (0, 0))],
        out_shape=[S((T, D), f32), S((T, D), f32), S((T, 128), f32)],
        compiler_params=_cp("arbitrary", "arbitrary"))(qn, kn, vb, do, lser3, dlr3, ccol, crow3)


def _forget_bwd(name, cs, z, bf):
    T = cs.shape[0]

    def body(cs_ref, zf_ref, bf_ref, dz_ref, db_ref):
        c = -cs_ref[...]
        row = lax.broadcasted_iota(jnp.int32, c.shape, 0)
        s = 1
        while s < T:
            c = c + jnp.where(row + s < T, pltpu.roll(c, T - s, 0), 0.0)
            s *= 2
        xf = zf_ref[...] + bf_ref[...]
        lane = lax.broadcasted_iota(jnp.int32, c.shape, 1)
        dxf = jnp.where(lane < NH, c / (1.0 + jnp.exp(xf)), 0.0)
        dz_ref[...] = jnp.zeros_like(dz_ref)
        dz_ref[:, 0:128] = dxf.astype(bf16)
        db_ref[...] = jnp.sum(dxf, axis=0, keepdims=True)

    return pl.pallas_call(
        body, name=name, grid=(1,),
        in_specs=[_bs((T, 128), lambda i: (0, 0)), _bs((T, 128), lambda i: (0, F_OFF // 128)), _vec(1, 128)],
        out_specs=[_bs((T, NZ - F_OFF), lambda i: (0, 0)), _vec(1, 128)],
        out_shape=[S((T, NZ - F_OFF), bf16), S((1, 128), f32)], compiler_params=_cp("arbitrary"))(cs, z, bf)


def _qk_bwd(name, dqn, dkn, dv, z, qg, kg):
    T = z.shape[0]

    def body(dq_ref, dk_ref, dv_ref, q_ref, k_ref, qg_ref, kg_ref, dz_ref, dqg_ref, dkg_ref, g_ref):
        i = pl.program_id(0)
        for n, (src, dsrc, gain, dgain) in enumerate(((q_ref, dq_ref, qg_ref, dqg_ref), (k_ref, dk_ref, kg_ref, dkg_ref))):
            for h in range(NH):
                cs = slice(h * HD, (h + 1) * HD)
                xv = src[:, cs]
                r = lax.rsqrt(jnp.mean(xv * xv, axis=-1, keepdims=True) + RMS_EPS)
                xhat = xv * r
                dy = dsrc[:, cs]
                dyg = dy * gain[:, cs]
                m = jnp.mean(dyg * xhat, axis=-1, keepdims=True)
                dz_ref[:, n * D + h * HD:n * D + (h + 1) * HD] = (r * (dyg - xhat * m)).astype(bf16)
                g_ref[:, cs] = jnp.sum(dy * xhat, axis=0, keepdims=True)
            _acc_store(i, dgain, g_ref[...])
        dz_ref[:, 2 * D:3 * D] = dv_ref[...].astype(bf16)

    return pl.pallas_call(
        body, name=name, grid=(T // TT,),
        in_specs=[_row(), _row(), _row(), _row(5), _row(6), _vec(), _vec()],
        out_specs=[_row(0, 3 * D), _vec(), _vec()], out_shape=[S((T, 3 * D), bf16), S((1, D), f32), S((1, D), f32)],
        scratch_shapes=[pltpu.VMEM((1, D), f32)], compiler_params=_cp("arbitrary"))(dqn, dkn, dv, z, z, qg, kg)


GB = GATE_OFF // D


def _merge_fwd(name, ya, yb, yc, z, bg):
    T = z.shape[0]

    def body(ya_ref, yb_ref, yc_ref, g0_ref, g1_ref, g2_ref, bg_ref, o_ref):
        acc = _sigmoid(g0_ref[...] + bg_ref[0:1, :]) * ya_ref[...]
        acc = acc + _sigmoid(g1_ref[...] + bg_ref[1:2, :]) * yb_ref[...]
        acc = acc + _sigmoid(g2_ref[...] + bg_ref[2:3, :]) * yc_ref[...]
        o_ref[...] = acc.astype(bf16)

    return pl.pallas_call(
        body, name=name, grid=(T // TT,),
        in_specs=[_row(), _row(), _row(), _row(GB), _row(GB + 1), _row(GB + 2), _vec(3)],
        out_specs=_row(), out_shape=S((T, D), bf16), compiler_params=_cp("parallel"))(ya, yb, yc, z, z, z, bg)


def _merge_bwd(name, dm, ya, yb, yc, z, bg):
    T = z.shape[0]

    def body(dm_ref, ya_ref, yb_ref, yc_ref, g0_ref, g1_ref, g2_ref, bg_ref, dya_ref, dyb_ref, dyc_ref, dz_ref, db_ref):
        i = pl.program_id(0)
        dm_v = dm_ref[...]
        dbs = []
        for n, (y_ref, g_ref, dy_ref) in enumerate(((ya_ref, g0_ref, dya_ref), (yb_ref, g1_ref, dyb_ref),
                                                    (yc_ref, g2_ref, dyc_ref))):
            gate = _sigmoid(g_ref[...] + bg_ref[n:n + 1, :])
            dy_ref[...] = (dm_v * gate).astype(bf16)
            dl = dm_v * y_ref[...] * gate * (1.0 - gate)
            dz_ref[:, n * D:(n + 1) * D] = dl.astype(bf16)
            dbs.append(jnp.sum(dl, axis=0, keepdims=True))
        _acc_store(i, db_ref, jnp.concatenate(dbs, axis=0))

    return pl.pallas_call(
        body, name=name, grid=(T // TT,),
        in_specs=[_row(), _row(), _row(), _row(), _row(GB), _row(GB + 1), _row(GB + 2), _vec(3)],
        out_specs=[_row(), _row(), _row(), _row(0, 3 * D), _vec(3)],
        out_shape=[S((T, D), bf16), S((T, D), bf16), S((T, D), bf16), S((T, 3 * D), bf16), S((3, D), f32)],
        compiler_params=_cp("arbitrary"))(dm, ya, yb, yc, z, z, z, bg)


SMALL_NAMES = ("ffn1_norm", "mix_norm", "b_forget", "b_gate", "conv_w", "sgu_ln_g", "sgu_ln_b", "sgu_w", "sgu_b",
               "q_norm_g", "k_norm_g", "ffn2_norm")


def _small_params(p):
    out = {n: p[n].reshape(1, D) for n in ("ffn1_norm", "mix_norm", "ffn2_norm", "sgu_ln_g", "sgu_ln_b", "q_norm_g", "k_norm_g")}
    out["b_forget"] = jnp.pad(p["b_forget"].reshape(1, NH), ((0, 0), (0, 128 - NH)))
    out["b_gate"] = p["b_gate"]
    out["conv_w"] = p["conv_w"]
    out["sgu_w"] = p["sgu_w"]
    out["bmap"] = jnp.repeat(p["sgu_b"].T, HD, axis=1)
    return out


def _small_grads_natural(sg):
    out = {n: sg[n].reshape(D) for n in ("ffn1_norm", "mix_norm", "ffn2_norm", "sgu_ln_g", "sgu_ln_b")}
    out["q_norm_g"] = sg["q_norm_g"].reshape(NH, HD)
    out["k_norm_g"] = sg["k_norm_g"].reshape(NH, HD)
    out["b_forget"] = sg["b_forget"][0, :NH]
    out["b_gate"] = sg["b_gate"]
    out["conv_w"] = sg["conv_w"]
    out["sgu_w"] = sg["sgu_w"]
    out["sgu_b"] = sg["sgu_b"]
    return out


SQ_TM = 1024


def _sq_fwd(name, a, wsq, l, n, res=None):
    T = a.shape[0]
    tm = _tile(T, SQ_TM)
    return _mm(name, a, wsq, grid=(T // tm, 1, 1), a_spec=_bs((tm, D), lambda i, j, k: (i, 0)),
               b_spec=_bs((None, None, D, D), lambda i, j, k: (l, n, 0, 0)),
               out_shape=S((T, D), f32), out_spec=_bs((tm, D), lambda i, j, k: (i, 0)), dims=NN, acc_shape=None,
               res=res, res_spec=_bs((tm, D), lambda i, j, k: (i, 0)))


def _sq_bwd_in(name, dy, wsq, l, n):
    T = dy.shape[0]
    tm = _tile(T, SQ_TM)
    return _mm(name, dy, wsq, grid=(T // tm, 1, 1), a_spec=_bs((tm, D), lambda i, j, k: (i, 0)),
               b_spec=_bs((None, None, D, D), lambda i, j, k: (l, n, 0, 0)),
               out_shape=S((T, D), f32), out_spec=_bs((tm, D), lambda i, j, k: (i, 0)), dims=NT, acc_shape=None)


def _sq_bwd_w(name, a, dy, gbuf, l):
    T = a.shape[0]
    return _mm(name, a, dy, grid=(NDEV // 2, 1, 1), a_spec=_bs((T, 256), lambda i, j, k: (0, i)),
               b_spec=_bs((T, D), lambda i, j, k: (0, 0)), out_shape=S(gbuf.shape, bf16),
               out_spec=_bs((2, None, None, 128, D), lambda i, j, k: (0, i, l, 0, 0)),
               dims=TN, acc_shape=None, alias=gbuf, split_rows=128)


def _ffn_fwd(tag, x, g, wgu, wd, l):
    T = x.shape[0]
    h = _rms_fwd(tag + "_rms", x, g)
    gu, a = _swiglu_fwd(tag + "_gu", h, wgu, l)
    tm = _tile(T, 1024)
    xo = _mm(tag + "_down", a, wd, grid=(T // tm, 1, 4), a_spec=_bs((None, tm, GU), lambda i, j, k: (k, i, 0)),
             b_spec=_bs((None, None, GU, D), lambda i, j, k: (l, k, 0, 0)), out_shape=S((T, D), f32),
             out_spec=_bs((tm, D), lambda i, j, k: (i, 0)), dims=NN, acc_shape=(tm, D), res=x,
             res_spec=_bs((tm, D), lambda i, j, k: (i, 0)), alpha=0.5)
    return xo, (h, gu, a)


def _ffn_bwd(tag, dxo, x, g, wgu, wd, l, saved, g_gu, g_d, ship, hook):
    h, gu, a = saved
    T = x.shape[0]
    g_d = _mm(tag + "_dwd", a, dxo, grid=(4, 1, 1), a_spec=_bs((None, T, GU), lambda i, j, k: (i, 0, 0)),
              b_spec=_bs((T, D), lambda i, j, k: (0, 0)), out_shape=S(g_d.shape, bf16),
              out_spec=_bs((2, None, None, GU // 2, D), lambda i, j, k: (0, i, l, 0, 0)), dims=TN, acc_shape=None,
              alpha=0.5, alias=g_d, split_rows=GU // 2)
    dgu = _swiglu_bwd(tag + "_dgu", dxo, wd, gu, l).reshape(NDEV, T, GU)
    g_gu = _mm(tag + "_dwgu", dgu, h, grid=(NDEV, 1, 1), a_spec=_bs((None, T, GU), lambda i, j, k: (i, 0, 0)),
               b_spec=_bs((T, D), lambda i, j, k: (0, 0)), out_shape=S(g_gu.shape, bf16),
               out_spec=_bs((None, None, None, GU, D), lambda i, j, k: (i % 2, i // 2, l, 0, 0)), dims=TN,
               acc_shape=None, alias=g_gu)
    dxo = ship([g_gu, g_d], dxo)
    dh = _mm(tag + "_dh", dgu, wgu, grid=(1, 1, NDEV), a_spec=_bs((None, T, GU), lambda i, j, k: (k, 0, 0)),
             b_spec=_bs((None, None, GU, D), lambda i, j, k: (l, k, 0, 0)), out_shape=S((T, D), f32),
             out_spec=_bs((T, D), lambda i, j, k: (0, 0)), dims=NN, acc_shape=(T, D))
    dxo = hook(dxo, dh)
    return _rms_bwd(tag + "_drms", dh, x, g, dxo)


def _mixer_fwd(tag, x, p, win, wsq, l):
    T = x.shape[0]
    h = _rms_fwd(tag + "_rms", x, p["mix_norm"])
    tn = 512
    z = _mm(tag + "_in", h, win, grid=(1, NZ // tn, 1), a_spec=_bs((T, D), lambda i, j, k: (0, 0)),
            b_spec=_bs((D, tn), lambda i, j, k: (0, j)), out_shape=S((T, NZ), f32),
            out_spec=_bs((T, tn), lambda i, j, k: (0, j)), dims=NN, acc_shape=None)
    ya_in = _conv_fwd(tag + "_conv", z, p["conv_w"])
    yb_in = _sgu_fwd(tag + "_sgu", z, p["sgu_ln_g"], p["sgu_ln_b"], p["sgu_w"], p["bmap"])
    qn, kn, vb, logf = _qk_fwd(tag + "_qk", z, p["q_norm_g"], p["k_norm_g"], p["b_forget"])
    ccol, crow = _cum_fwd(tag + "_cum", logf)
    crow3 = crow.reshape(NH, 1, T)
    o, lse, lser = _attn_fwd(tag + "_attn", qn, kn, vb, ccol, crow3)
    ya = _sq_fwd(tag + "_oconv", ya_in, wsq, l, 0)
    yb = _sq_fwd(tag + "_osgu", yb_in, wsq, l, 1)
    yc = _sq_fwd(tag + "_oattn", o, wsq, l, 2)
    merged = _merge_fwd(tag + "_merge", ya, yb, yc, z, p["b_gate"])
    xo = _sq_fwd(tag + "_o", merged, wsq, l, 3, res=x)
    return xo, (h, z, ya_in, yb_in, qn, kn, vb, ccol, crow3, o, lse, lser, ya, yb, yc, merged)


def _mixer_bwd(tag, dxo, x, p, win, wsq, l, saved, gsq, ship, hook):
    h, z, ya_in, yb_in, qn, kn, vb, ccol, crow3, o, lse, lser, ya, yb, yc, merged = saved
    T = x.shape[0]
    sg = {}
    dm = _sq_bwd_in(tag + "_dmerged", dxo, wsq, l, 3)
    gsq[3] = _sq_bwd_w(tag + "_dwo", merged, dxo, gsq[3], l)
    dya, dyb, dyc, dz_g, sg["b_gate"] = _merge_bwd(tag + "_dmerge", dm, ya, yb, yc, z, p["b_gate"])
    d_ya_in = _sq_bwd_in(tag + "_dconv_in", dya, wsq, l, 0)
    gsq[0] = _sq_bwd_w(tag + "_dwoc", ya_in, dya, gsq[0], l)
    d_yb_in = _sq_bwd_in(tag + "_dsgu_in", dyb, wsq, l, 1)
    gsq[1] = _sq_bwd_w(tag + "_dwos", yb_in, dyb, gsq[1], l)
    d_o = _sq_bwd_in(tag + "_dattn_in", dyc, wsq, l, 2)
    gsq[2] = _sq_bwd_w(tag + "_dwoa", o, dyc, gsq[2], l)
    dz_c, sg["conv_w"] = _conv_bwd(tag + "_dconv", d_ya_in, z, p["conv_w"])
    dz_s, sg["sgu_ln_g"], sg["sgu_ln_b"], sg["sgu_w"], db_t = _sgu_bwd(
        tag + "_dsgu", d_yb_in, z, p["sgu_ln_g"], p["sgu_ln_b"], p["sgu_w"], p["bmap"])
    sg["sgu_b"] = db_t.T
    dqn, dlr = _attn_dq(tag + "_dattn_q", qn, kn, vb, d_o, lse, ccol, crow3)
    dkn, dv, cs = _attn_dkv(tag + "_dattn_kv", qn, kn, vb, d_o, lser, dlr, ccol, crow3)
    dz_f, sg["b_forget"] = _forget_bwd(tag + "_dforget", cs, z, p["b_forget"])
    dz_q, sg["q_norm_g"], sg["k_norm_g"] = _qk_bwd(tag + "_dqk", dqn, dkn, dv, z, p["q_norm_g"], p["k_norm_g"])
    dz = [dz_c, dz_s, dz_q, dz_g, dz_f]
    dwin = _dz_matmul(tag + "_dwin", dz, h, True)
    dxo = ship(dwin, gsq, dxo)
    dh = _dz_matmul(tag + "_dh", dz, win, False)
    dxo = hook(dxo, dh)
    dx, sg["mix_norm"] = _rms_bwd(tag + "_drms", dh, x, p["mix_norm"], dxo)
    return dx, sg


ANY = pl.BlockSpec(memory_space=pl.ANY)
HBM = pl.BlockSpec(memory_space=pltpu.HBM)
SEM = pl.BlockSpec(memory_space=pltpu.SEMAPHORE)
EFFECT = pltpu.SideEffectType.DATAFLOW_SIDE_EFFECTING


def _place():
    return lax.axis_index("x"), lax.axis_index("y"), lax.axis_index("c")


NEAR = 4


def _others(x, y, c):
    return [(x, y, 1 - c), (1 - x, y, c), (x, 1 - y, c), (1 - x, 1 - y, c)]


def _gather_start(name, groups, carry=None):
    sizes = [len(g) for g in groups]
    srcs = [s for g in groups for s, _ in g]
    lands = [b for g in groups for _, b in g]
    n, ng = len(srcs), len(groups)
    held = srcs + lands + ([] if carry is None else [carry])
    nh = len(held)

    def body(*refs):
        src_refs, land_refs = refs[:n], refs[n:2 * n]
        send, recv = refs[nh:nh + ng], refs[nh + ng:nh + 2 * ng]
        x, y, c = _place()
        me = 4 * x + 2 * y + c
        u = 0
        for g, size in enumerate(sizes):
            for i in range(size):
                for k, peer in enumerate(_others(x, y, c)):
                    pltpu.make_async_remote_copy(src_ref=src_refs[u], dst_ref=land_refs[u].at[me],
                                                 send_sem=send[g].at[i * NEAR + k], recv_sem=recv[g].at[i * NEAR + k],
                                                 device_id=peer, device_id_type=MESH).start()
                u += 1

    sems = [pltpu.SemaphoreType.DMA((size * NEAR,)) for size in sizes]
    out = pl.pallas_call(
        body, name=name, in_specs=[HBM] * nh, out_specs=[SEM] * (2 * ng) + [HBM] * nh,
        out_shape=sems + sems + [pltpu.HBM(a.shape, a.dtype) for a in held],
        input_output_aliases={i: 2 * ng + i for i in range(nh)},
        compiler_params=pltpu.CompilerParams(has_side_effects=EFFECT),
    )(*[pltpu.with_memory_space_constraint(a, pltpu.HBM) for a in held])
    res, pos = [], 0
    for g, size in enumerate(sizes):
        res.append((out[g], out[ng + g], out[2 * ng + pos:2 * ng + pos + size], out[2 * ng + n + pos:2 * ng + n + pos + size]))
        pos += size
    return res if carry is None else (res, out[2 * ng + 2 * n])


def _gather_wait(name, started, after=None):
    send, recv, srcs, lands = started
    n = len(srcs)

    def body(*refs):
        src_refs, land_refs = refs[:n], refs[n:2 * n]
        send_ref, recv_ref = refs[2 * n], refs[2 * n + 1]
        x, y, c = _place()
        for i in range(n):
            for k, (px, py, pc) in enumerate(_others(x, y, c)):
                cp = pltpu.make_async_remote_copy(src_ref=src_refs[i], dst_ref=land_refs[i].at[4 * px + 2 * py + pc],
                                                  send_sem=send_ref.at[i * NEAR + k], recv_sem=recv_ref.at[i * NEAR + k],
                                                  device_id=(px, py, pc), device_id_type=MESH)
                cp.wait_send()
                cp.wait_recv()

    extra = [] if after is None else [after]
    out = pl.pallas_call(
        body, name=name, in_specs=[HBM] * (2 * n) + [SEM, SEM] + [ANY] * len(extra), out_specs=[HBM] * (2 * n),
        out_shape=[pltpu.HBM(a.shape, a.dtype) for a in list(srcs) + list(lands)],
        input_output_aliases={i: i for i in range(2 * n)},
        compiler_params=pltpu.CompilerParams(has_side_effects=EFFECT),
    )(*srcs, *lands, send, recv, *extra)
    return out[n:]


def _forward_copy(have, full, send, recv, i, j, core):
    x, y, c = _place()
    chip = [(1 - x, y), (x, 1 - y), (1 - x, 1 - y)][j]
    slot = 4 * chip[0] + 2 * chip[1] + core
    return pltpu.make_async_remote_copy(src_ref=have[i].at[slot], dst_ref=full[i].at[slot], send_sem=send.at[i * 3 + j],
                                        recv_sem=recv.at[i * 3 + j], device_id=(x, y, 1 - c), device_id_type=MESH)


def _gather_forward(name, lands):
    n = len(lands)

    def body(*refs):
        have, full = refs[:n], refs[n:2 * n]
        send, recv = refs[2 * n], refs[2 * n + 1]
        c = lax.axis_index("c")
        for i in range(n):
            for j in range(3):
                _forward_copy(have, full, send, recv, i, j, c).start()
        for i in range(n):
            for j in range(3):
                _forward_copy(have, full, send, recv, i, j, c).wait_send()
                _forward_copy(have, full, send, recv, i, j, 1 - c).wait_recv()

    return pl.pallas_call(
        body, name=name, in_specs=[ANY] * n, out_specs=[ANY] * n, out_shape=[S(a.shape, a.dtype) for a in lands],
        input_output_aliases={i: i for i in range(n)},
        scratch_shapes=[pltpu.SemaphoreType.DMA((n * 3,)), pltpu.SemaphoreType.DMA((n * 3,))],
    )(*lands)


def _forward_start(name, lands, carry):
    n = len(lands)
    held = list(lands) + [carry]

    def body(*refs):
        c = lax.axis_index("c")
        for i in range(n):
            for j in range(3):
                _forward_copy(refs[:n], refs[:n], refs[n + 1], refs[n + 2], i, j, c).start()

    sems = [pltpu.SemaphoreType.DMA((n * 3,))] * 2
    out = pl.pallas_call(
        body, name=name, in_specs=[HBM] * (n + 1), out_specs=[SEM, SEM] + [HBM] * (n + 1),
        out_shape=sems + [pltpu.HBM(a.shape, a.dtype) for a in held],
        input_output_aliases={i: 2 + i for i in range(n + 1)},
        compiler_params=pltpu.CompilerParams(has_side_effects=EFFECT),
    )(*[pltpu.with_memory_space_constraint(a, pltpu.HBM) for a in held])
    return (out[0], out[1], out[2:2 + n]), out[2 + n]


def _forward_wait(name, started, after):
    send, recv, lands = started
    n = len(lands)

    def body(*refs):
        c = lax.axis_index("c")
        for i in range(n):
            for j in range(3):
                _forward_copy(refs[:n], refs[:n], refs[n], refs[n + 1], i, j, c).wait_send()
                _forward_copy(refs[:n], refs[:n], refs[n], refs[n + 1], i, j, 1 - c).wait_recv()

    return pl.pallas_call(
        body, name=name, in_specs=[HBM] * n + [SEM, SEM, ANY], out_specs=[HBM] * n,
        out_shape=[pltpu.HBM(a.shape, a.dtype) for a in lands],
        input_output_aliases={i: i for i in range(n)},
        compiler_params=pltpu.CompilerParams(has_side_effects=EFFECT),
    )(*lands, send, recv, after)


def _pair_copies(ins, outs, send, recv):
    x, y, c = _place()
    return [pltpu.make_async_remote_copy(src_ref=ins[u].at[1 - c], dst_ref=outs[u], send_sem=send.at[u],
                                         recv_sem=recv.at[u], device_id=(x, y, 1 - c), device_id_type=MESH)
            for u in range(len(ins))]


def _pair_start(name, gs, carry):
    n = len(gs)
    held = list(gs) + [lax.empty(g.shape[1:], g.dtype) for g in gs] + [carry]

    def body(*refs):
        for cp in _pair_copies(refs[:n], refs[n:2 * n], refs[2 * n + 1], refs[2 * n + 2]):
            cp.start()

    sems = [pltpu.SemaphoreType.DMA((n,))] * 2
    out = pl.pallas_call(
        body, name=name, in_specs=[HBM] * len(held), out_specs=[SEM, SEM] + [HBM] * len(held),
        out_shape=sems + [pltpu.HBM(a.shape, a.dtype) for a in held],
        input_output_aliases={i: 2 + i for i in range(len(held))},
        compiler_params=pltpu.CompilerParams(has_side_effects=EFFECT),
    )(*[pltpu.with_memory_space_constraint(a, pltpu.HBM) for a in held])
    return (out[0], out[1], out[2:2 + n], out[2 + n:2 + 2 * n]), out[2 + 2 * n]


def _pair_wait(name, started, after):
    send, recv, gs, lands = started
    n = len(gs)

    def body(*refs):
        for cp in _pair_copies(refs[:n], refs[n:2 * n], refs[2 * n], refs[2 * n + 1]):
            cp.wait_send()
            cp.wait_recv()

    out = pl.pallas_call(
        body, name=name, in_specs=[HBM] * (2 * n) + [SEM, SEM, ANY], out_specs=[HBM] * (2 * n),
        out_shape=[pltpu.HBM(a.shape, a.dtype) for a in list(gs) + list(lands)],
        input_output_aliases={i: i for i in range(2 * n)},
        compiler_params=pltpu.CompilerParams(has_side_effects=EFFECT),
    )(*gs, *lands, send, recv, after)
    return out[:n], out[n:]


def _row_tile(r, c):
    if c > D and r % 128 == 0:
        return 128
    return 256 if r % 256 == 0 else (GU // 2 if r % (GU // 2) == 0 else r)


def _pair_sum(name, core, g, r1):
    _, nq, nl, r, c = g.shape
    tr = r
    g4 = g.reshape(2, nq * nl, r, c)
    r3 = r1.reshape(nq * nl, r, c)

    def body(core_ref, g_ref, r_ref, o_ref):
        o_ref[...] = (g_ref[...].astype(f32) + r_ref[...].astype(f32)).astype(bf16)

    out = pl.pallas_call(
        body, name=name,
        grid_spec=pltpu.PrefetchScalarGridSpec(
            num_scalar_prefetch=1, grid=(nq * nl, r // tr),
            in_specs=[_bs((None, None, tr, c), lambda b, i, cr: (cr[0], b, i, 0)), _bs((None, tr, c), lambda b, i, cr: (b, i, 0))],
            out_specs=_bs((None, tr, c), lambda b, i, cr: (b, i, 0))),
        out_shape=S((nq * nl, r, c), bf16), compiler_params=_cp("parallel", "parallel"))(core, g4, r3)
    return out.reshape(nq, nl, r, c)


def _scatter_copies(ins, outs, send, recv):
    x, y, c = _place()
    chips = [(1 - x, y), (x, 1 - y), (1 - x, 1 - y)]
    return [pltpu.make_async_remote_copy(src_ref=ins[u].at[2 * chip[0] + chip[1]], dst_ref=outs[u].at[k],
                                         send_sem=send.at[u * 3 + k], recv_sem=recv.at[u * 3 + k],
                                         device_id=(*chip, c), device_id_type=MESH)
            for u in range(len(ins)) for k, chip in enumerate(chips)]


def _scatter_start(name, ss, carry):
    n = len(ss)
    lands = [lax.empty((3,) + s.shape[1:], s.dtype) for s in ss]
    held = list(ss) + lands + [carry]

    def body(*refs):
        for cp in _scatter_copies(refs[:n], refs[n:2 * n], refs[2 * n + 1], refs[2 * n + 2]):
            cp.start()

    sems = [pltpu.SemaphoreType.DMA((n * 3,))] * 2
    out = pl.pallas_call(
        body, name=name, in_specs=[HBM] * len(held), out_specs=[SEM, SEM] + [HBM] * len(held),
        out_shape=sems + [pltpu.HBM(a.shape, a.dtype) for a in held],
        input_output_aliases={i: 2 + i for i in range(len(held))},
        compiler_params=pltpu.CompilerParams(has_side_effects=EFFECT),
    )(*[pltpu.with_memory_space_constraint(a, pltpu.HBM) for a in held])
    return (out[0], out[1], out[2:2 + n], out[2 + n:2 + 2 * n]), out[2 + 2 * n]


def _scatter_wait(name, started, after):
    send, recv, srcs, lands = started
    n = len(srcs)

    def body(*refs):
        for cp in _scatter_copies(refs[:n], refs[n:2 * n], refs[2 * n], refs[2 * n + 1]):
            cp.wait_send()
            cp.wait_recv()

    out = pl.pallas_call(
        body, name=name, in_specs=[HBM] * (2 * n) + [SEM, SEM, ANY], out_specs=[HBM] * (2 * n),
        out_shape=[pltpu.HBM(a.shape, a.dtype) for a in list(srcs) + list(lands)],
        input_output_aliases={i: i for i in range(2 * n)},
        compiler_params=pltpu.CompilerParams(has_side_effects=EFFECT),
    )(*srcs, *lands, send, recv, after)
    return out[:n], out[n:]


def _sum_blocks(name, blocks):
    def body(b_ref, o_ref):
        acc = b_ref[0]
        for d in range(1, NDEV):
            acc = acc + b_ref[d]
        o_ref[...] = acc

    return pl.pallas_call(body, name=name, out_shape=S(blocks.shape[1:], f32),
                          compiler_params=pltpu.CompilerParams(vmem_limit_bytes=VMEM_LIMIT))(blocks)


def _adam_math(w, g, m, v):
    m = ADAM_B1 * m + (1.0 - ADAM_B1) * g
    v = ADAM_B2 * v + (1.0 - ADAM_B2) * (g * g)
    m_hat = m / (1.0 - ADAM_B1 ** ADAM_STEP)
    v_hat = v / (1.0 - ADAM_B2 ** ADAM_STEP)
    delta = -ADAM_LR * (m_hat / (jnp.sqrt(v_hat) + ADAM_EPS) + ADAM_WD * w)
    return delta, m, v


def _adamw(name, chip, w, m, v, parts):
    _, r, c = w.shape
    tr = _row_tile(r, c)

    def body(chip_ref, w_ref, m_ref, v_ref, *refs):
        sets, (g_ref, d_ref, mo_ref, vo_ref) = (refs[0:4], refs[4:8]), refs[8:]
        for l in range(2):
            @pl.when(pl.program_id(0) == l)
            def _():
                s_ref, r0_ref, r1_ref, r2_ref = sets[l]
                g = ((s_ref[...].astype(f32) + r0_ref[...].astype(f32)) + r1_ref[...].astype(f32)) + r2_ref[...].astype(f32)
                g_ref[...] = g
                d_ref[...], mo_ref[...], vo_ref[...] = _adam_math(w_ref[...], g, m_ref[...], v_ref[...])

    blk = _bs((None, tr, c), lambda l, i, cr: (l, i, 0))
    operands, specs = [], []
    for n in range(2):
        row = (lambda l, i: i * (1 - l)) if n == 0 else (lambda l, i: i * l)
        sums, r2 = parts[n]
        operands += [sums, r2, r2, r2]
        specs.append(_bs((None, None, tr, c), functools.partial(lambda l, i, cr, row: (cr[0], 0, row(l, i), 0), row=row)))
        specs += [_bs((None, None, tr, c), functools.partial(lambda l, i, cr, k, row: (k, 0, row(l, i), 0), k=k, row=row))
                  for k in range(3)]
    return pl.pallas_call(
        body, name=name,
        grid_spec=pltpu.PrefetchScalarGridSpec(num_scalar_prefetch=1, grid=(2, r // tr), in_specs=[blk, blk, blk] + specs,
                                               out_specs=[blk] * 4),
        out_shape=[S(w.shape, f32)] * 4, compiler_params=_cp("arbitrary", "arbitrary"),
    )(chip, w, m, v, *operands)


def _adamw_small(name, w, g, m, v):
    def body(w_ref, g_ref, m_ref, v_ref, d_ref, mo_ref, vo_ref):
        d_ref[...], mo_ref[...], vo_ref[...] = _adam_math(w_ref[...], g_ref[...], m_ref[...], v_ref[...])

    return pl.pallas_call(body, name=name, out_shape=[S(w.shape, f32)] * 3,
                          compiler_params=pltpu.CompilerParams(vmem_limit_bytes=VMEM_LIMIT))(w, g, m, v)


WEIGHT_NAMES = ("ffn1_norm", "ffn1_w_gu", "ffn1_w_down", "mix_norm", "w_in", "b_forget", "b_gate", "conv_w", "sgu_ln_g",
                "sgu_ln_b", "sgu_w", "sgu_b", "q_norm_g", "k_norm_g", "w_out_conv", "w_out_sgu", "w_out_attn", "w_o",
                "ffn2_norm", "ffn2_w_gu", "ffn2_w_down")
BIG = {"ffn1_w_gu": "gu1", "ffn2_w_gu": "gu2", "ffn1_w_down": "d1", "ffn2_w_down": "d2", "w_in": "in",
       "w_out_conv": "oc", "w_out_sgu": "os", "w_out_attn": "oa", "w_o": "o"}
BIG_KEYS = ("gu1", "gu2", "d1", "d2", "in", "oc", "os", "oa", "o")
REPLICATED_SMALL = ("ffn1_norm", "mix_norm", "b_forget", "sgu_ln_g", "sgu_ln_b", "sgu_w", "sgu_b", "q_norm_g",
                    "k_norm_g", "ffn2_norm")
SHARDED_SMALL = ("b_gate", "conv_w")
TRANSPOSED = ("gu1", "gu2")


def _packed_rows(shape):
    size = 1
    for s_ in shape:
        size *= s_
    return size, -(-size // 1024) * 8


def _pack(arrays):
    pieces = []
    for a in arrays:
        size, rows = _packed_rows(a.shape)
        pieces.append(jnp.pad(a.reshape(-1).astype(f32), (0, rows * 128 - size)).reshape(rows, 128))
    return jnp.concatenate(pieces, axis=0)


def _unpack(packed, shapes):
    out, pos = [], 0
    for shp in shapes:
        size, rows = _packed_rows(shp)
        out.append(packed[pos:pos + rows].reshape(-1)[:size].reshape(shp))
        pos += rows
    return out


def _natural_runs(a, b):
    runs = []
    while a < b:
        d = a // INB
        e = min(b, (d + 1) * INB)
        runs.append((d, a - d * INB, e - d * INB))
        a = e
    return runs


def _win_kernel_layout(wg):
    runs = _natural_runs(0, GATE_OFF) + _natural_runs(GATE_OFF + NH, NIN) + _natural_runs(GATE_OFF, GATE_OFF + NH)
    return jnp.concatenate([wg[d, :, a:b] for d, a, b in runs] + [jnp.zeros((D, NZ - NIN), wg.dtype)], axis=1)


def _kernel_column(n):
    return n if n < GATE_OFF else (F_OFF + n - GATE_OFF if n < GATE_OFF + NH else n - NH)


def _win_device_block(dw, d):
    cuts = sorted({d * INB, (d + 1) * INB} | {c for c in (GATE_OFF, GATE_OFF + NH) if d * INB < c < (d + 1) * INB})
    parts = [dw[:, _kernel_column(a):_kernel_column(a) + (b - a)] for a, b in zip(cuts[:-1], cuts[1:])]
    return parts[0] if len(parts) == 1 else jnp.concatenate(parts, axis=1)


def kernel(x, ffn1_norm, ffn1_w_gu, ffn1_w_down, mix_norm, w_in, b_forget, b_gate, conv_w, sgu_ln_g, sgu_ln_b, sgu_w, sgu_b, q_norm_g, k_norm_g, w_out_conv, w_out_sgu, w_out_attn, w_o, ffn2_norm, ffn2_w_gu, ffn2_w_down, loss_target, m_ffn1_norm, m_ffn1_w_gu, m_ffn1_w_down, m_mix_norm, m_w_in, m_b_forget, m_b_gate, m_conv_w, m_sgu_ln_g, m_sgu_ln_b, m_sgu_w, m_sgu_b, m_q_norm_g, m_k_norm_g, m_w_out_conv, m_w_out_sgu, m_w_out_attn, m_w_o, m_ffn2_norm, m_ffn2_w_gu, m_ffn2_w_down, v_ffn1_norm, v_ffn1_w_gu, v_ffn1_w_down, v_mix_norm, v_w_in, v_b_forget, v_b_gate, v_conv_w, v_sgu_ln_g, v_sgu_ln_b, v_sgu_w, v_sgu_b, v_q_norm_g, v_k_norm_g, v_w_out_conv, v_w_out_sgu, v_w_out_attn, v_w_o, v_ffn2_norm, v_ffn2_w_gu, v_ffn2_w_down):
    w = dict(zip(WEIGHT_NAMES, (ffn1_norm, ffn1_w_gu, ffn1_w_down, mix_norm, w_in, b_forget, b_gate, conv_w, sgu_ln_g,
                                sgu_ln_b, sgu_w, sgu_b, q_norm_g, k_norm_g, w_out_conv, w_out_sgu, w_out_attn, w_o,
                                ffn2_norm, ffn2_w_gu, ffn2_w_down)))
    mom = dict(zip(WEIGHT_NAMES, (m_ffn1_norm, m_ffn1_w_gu, m_ffn1_w_down, m_mix_norm, m_w_in, m_b_forget, m_b_gate,
                                  m_conv_w, m_sgu_ln_g, m_sgu_ln_b, m_sgu_w, m_sgu_b, m_q_norm_g, m_k_norm_g,
                                  m_w_out_conv, m_w_out_sgu, m_w_out_attn, m_w_o, m_ffn2_norm, m_ffn2_w_gu,
                                  m_ffn2_w_down)))
    var = dict(zip(WEIGHT_NAMES, (v_ffn1_norm, v_ffn1_w_gu, v_ffn1_w_down, v_mix_norm, v_w_in, v_b_forget, v_b_gate,
                                  v_conv_w, v_sgu_ln_g, v_sgu_ln_b, v_sgu_w, v_sgu_b, v_q_norm_g, v_k_norm_g,
                                  v_w_out_conv, v_w_out_sgu, v_w_out_attn, v_w_o, v_ffn2_norm, v_ffn2_w_gu,
                                  v_ffn2_w_down)))
    px, py, pc = _place()
    dev = 4 * px + 2 * py + pc
    chip = 2 * px + py

    big_names = [n for n in WEIGHT_NAMES if n in BIG]
    key_name = {BIG[n]: n for n in big_names}
    group_keys = (("gu1", "d1"), ("in", "oc", "os", "oa", "o", "small"), ("gu2", "d2"))

    def source(key, l):
        if key == "small":
            return jnp.concatenate([w["b_gate"][l], w["conv_w"][l], jnp.zeros((2, 128), f32)], axis=0)
        block = w[key_name[key]][l]
        return (block.T if key in TRANSPOSED else block).astype(bf16)

    def landing(src):
        return lax.dynamic_update_slice(lax.empty((NDEV,) + src.shape, src.dtype), src[None], (dev, 0, 0))

    groups = [[(s, landing(s)) for s in (source(k, l) for k in keys)] for l in range(2) for keys in group_keys]
    (send0, recv0, srcs0, lands0), = _gather_start("gather_start_first", groups[:1])
    rest, src0 = _gather_start("gather_start_rest", groups[1:], carry=srcs0[0])
    started = [(send0, recv0, [src0] + list(srcs0[1:]), lands0)] + rest

    early = {}

    def prefetch(l, part, act):
        got = _gather_wait(f"gather_wait_{l}_{part}", started[3 * l + part], act)
        early[l, part], act = _forward_start(f"forward_start_{l}_{part}", got, act)
        return act

    def weights(l, part, after):
        if (l, part) in early:
            lands = _forward_wait(f"forward_wait_{l}_{part}", early.pop((l, part)), after)
        else:
            got = _gather_wait(f"gather_wait_{l}_{part}", started[3 * l + part], after)
            lands = _gather_forward(f"gather_forward_{l}_{part}", got)
        return dict(zip(group_keys[part], lands))

    xl = x[0]
    saved, small, wts = [], [], []
    for l in range(2):
        ga = weights(l, 0, xl if l else None)
        if l:
            xl = prefetch(l, 1, xl)
        wt = {"gu1": ga["gu1"][None], "d1": ga["d1"].reshape(1, 4, GU, D)}
        x1, s1 = _ffn_fwd("ffn1", xl, w["ffn1_norm"][l].reshape(1, D), wt["gu1"], wt["d1"], 0)
        gb = weights(l, 1, x1)
        x1 = prefetch(l, 2, x1)
        p = {n: w[n][l] for n in REPLICATED_SMALL}
        p["b_gate"] = jnp.transpose(gb["small"][:, 0:3, :], (1, 0, 2)).reshape(3, D)
        p["conv_w"] = jnp.transpose(gb["small"][:, 3:6, :], (1, 0, 2)).reshape(3, D)
        p = _small_params(p)
        wt["win"] = _win_kernel_layout(gb["in"])
        wt["sq"] = jnp.stack([gb[k].reshape(D, D) for k in ("oc", "os", "oa", "o")])[None]
        x2, sm = _mixer_fwd("mix", x1, p, wt["win"], wt["sq"], 0)
        gc = weights(l, 2, x2)
        if l == 0:
            x2 = prefetch(1, 0, x2)
        wt.update({"gu2": gc["gu2"][None], "d2": gc["d2"].reshape(1, 4, GU, D)})
        x3, s2 = _ffn_fwd("ffn2", x2, p["ffn2_norm"], wt["gu2"], wt["d2"], 0)
        saved.append((xl, x1, x2, s1, sm, s2))
        small.append(p)
        wts.append(wt)
        xl = x3
    loss_row, dx = _loss("loss", xl, loss_target[0])

    core = pc.reshape(1).astype(jnp.int32)
    chip_op = chip.reshape(1).astype(jnp.int32)
    buf = lambda r, c: lax.empty((2, 4, 1, r, c), bf16)
    flights = {}

    pairs = []

    def ship(l, part, bufs, dx):
        started, dx = _pair_start(f"pair_start_{l}_{part}", bufs, dx)
        pairs.append((l, part, started))
        return dx

    def finish(dx, after):
        l, part, started = pairs.pop()
        bufs, r1 = _pair_wait(f"pair_wait_{l}_{part}", started, after)
        ss = [_pair_sum(f"pair_sum_{l}_{k}", core, g, r) for k, g, r in zip(group_keys[part], bufs, r1)]
        flights[l, part], dx = _scatter_start(f"scatter_start_{l}_{part}", ss, dx)
        return dx

    def ship_mixer(l, dwin, gsq, dx):
        g_in = jnp.stack([_win_device_block(dwin, 2 * q + c) for c in range(2) for q in range(4)]).reshape(2, 4, 1, D, INB)
        return ship(l, 1, [g_in] + gsq, dx)

    sgrads = [None, None]
    for l in (1, 0):
        p, wt = small[l], wts[l]
        x0, x1, x2, s1, sm, s2 = saved[l]
        dx, dn2 = _ffn_bwd("ffn2", dx, x2, p["ffn2_norm"], wt["gu2"], wt["d2"], 0, s2, buf(GU, D), buf(GU // 2, D),
                           functools.partial(ship, l, 2), finish)
        dx, sg = _mixer_bwd("mix", dx, x1, p, wt["win"], wt["sq"], 0, sm, [buf(128, D) for _ in range(4)],
                            functools.partial(ship_mixer, l), finish)
        dx, dn1 = _ffn_bwd("ffn1", dx, x0, p["ffn1_norm"], wt["gu1"], wt["d1"], 0, s1, buf(GU, D), buf(GU // 2, D),
                           functools.partial(ship, l, 0), finish)
        sg["ffn1_norm"] = dn1
        sg["ffn2_norm"] = dn2
        sgrads[l] = sg

    nat = [_small_grads_natural(sgrads[l]) for l in range(2)]
    order = REPLICATED_SMALL + SHARDED_SMALL
    part = _pack([jnp.stack([nat[0][n], nat[1][n]]) for n in order] + [loss_row[0, 0:1]])
    small_flight, dx = _gather_start("small_start", [[(part, landing(part))]], carry=dx)

    grads, delta, new_m, new_v = {}, {}, {}, {}
    after = dx
    for part in (2, 1, 0):
        sets = []
        for l in (1, 0):
            s_all, r2_all = _scatter_wait(f"scatter_wait_{l}_{part}", flights[l, part], after)
            sets.append(list(zip(s_all, r2_all)))
        for i, k in enumerate(k for k in group_keys[part] if k != "small"):
            n = key_name[k]
            view = (lambda a: jnp.swapaxes(a, 1, 2)) if k in TRANSPOSED else (lambda a: a)
            outs = _adamw("adamw_" + k, chip_op, view(w[n]), view(mom[n]), view(var[n]), [sets[1][i], sets[0][i]])
            grads[n], delta[n], new_m[n], new_v[n] = [view(o) for o in outs]
            after = outs[1]

    blocks = _gather_forward("small_forward", _gather_wait("small_wait", small_flight[0], after))
    total = _sum_blocks("small_sum", blocks[0])
    full_shapes = [(2,) + tuple(nat[0][n].shape) for n in order] + [(1,)]
    summed = dict(zip(order + ("loss",), _unpack(total, full_shapes)))
    for n in REPLICATED_SMALL:
        grads[n] = summed[n]
    for n in SHARDED_SMALL:
        grads[n] = lax.dynamic_slice_in_dim(summed[n], dev * 128, 128, axis=2)
    wp = _pack([w[n] for n in order])
    gp = _pack([grads[n] for n in order])
    mp = _pack([mom[n] for n in order])
    vp = _pack([var[n] for n in order])
    dpk, mpk, vpk = _adamw_small("adamw_small", wp, gp, mp, vp)
    local_shapes = [tuple(w[n].shape) for n in order]
    for dst, packed in ((delta, dpk), (new_m, mpk), (new_v, vpk)):
        dst.update(zip(order, _unpack(packed, local_shapes)))

    loss = summed["loss"][0]
    return (loss, dx[None], *[grads[n] for n in WEIGHT_NAMES], *[delta[n] for n in WEIGHT_NAMES],
            *[new_m[n] for n in WEIGHT_NAMES], *[new_v[n] for n in WEIGHT_NAMES])
```

```python
import functools

import jax
import jax.numpy as jnp
from jax import lax
from jax.experimental import pallas as pl
from jax.experimental.pallas import tpu as pltpu

f32 = jnp.float32
bf16 = jnp.bfloat16
S = jax.ShapeDtypeStruct
MESH = pl.DeviceIdType.MESH

D = 1024
NH = 8
HD = 128
NDEV = 8
GU = 704
NIN = 11272
INB = 1409
GATE_OFF = 8192
F_OFF = 11264
NZ = 11776
RMS_EPS = 1e-6
LN_EPS = 1e-5
ATT_SCALE = HD ** -0.5
NEG = -1e30
INV_SQRT2 = 0.7071067811865476
INV_SQRT2PI = 0.3989422804014327

ADAM_LR = 0.001
ADAM_B1 = 0.9
ADAM_B2 = 0.999
ADAM_EPS = 1e-08
ADAM_WD = 0.01
ADAM_STEP = 10

TT = 512
VMEM_LIMIT = 56 * 1024 * 1024


def _cp(*sem):
    return pltpu.CompilerParams(dimension_semantics=sem, vmem_limit_bytes=VMEM_LIMIT)


def _bs(shape, fn):
    return pl.BlockSpec(shape, fn)


NN = (((1,), (0,)), ((), ()))
NT = (((1,), (1,)), ((), ()))
TN = (((0,), (0,)), ((), ()))


def _mm(name, a, b, *, grid, a_spec, b_spec, out_shape, out_spec, dims, acc_shape, res=None, res_spec=None,
        alpha=1.0, alias=None, split_rows=None):
    nk = grid[2]

    def body(*refs):
        a_ref, b_ref = refs[0], refs[1]
        pos = 2
        res_ref = None
        if res is not None:
            res_ref = refs[pos]
            pos += 1
        if alias is not None:
            pos += 1
        o_ref = refs[pos]
        acc_ref = refs[pos + 1] if nk > 1 else None
        part = lax.dot_general(a_ref[...].astype(bf16), b_ref[...].astype(bf16), dims, preferred_element_type=f32)

        def finish(acc):
            if alpha != 1.0:
                acc = alpha * acc
            if res_ref is not None:
                acc = res_ref[...] + acc
            if split_rows is None:
                o_ref[...] = acc.astype(o_ref.dtype)
            else:
                o_ref[0] = acc[:split_rows].astype(o_ref.dtype)
                o_ref[1] = acc[split_rows:].astype(o_ref.dtype)

        if nk == 1:
            finish(part)
        else:
            k = pl.program_id(2)

            @pl.when(k == 0)
            def _():
                acc_ref[...] = part

            @pl.when(k > 0)
            def _():
                acc_ref[...] += part

            @pl.when(k == nk - 1)
            def _():
                finish(acc_ref[...])

    operands = [a, b]
    in_specs = [a_spec, b_spec]
    if res is not None:
        operands.append(res)
        in_specs.append(res_spec)
    aliases = {}
    if alias is not None:
        aliases = {len(operands): 0}
        operands.append(alias)
        in_specs.append(pl.BlockSpec(memory_space=pl.ANY))
    return pl.pallas_call(
        body, name=name, grid=grid, in_specs=in_specs, out_specs=out_spec, out_shape=out_shape,
        scratch_shapes=[pltpu.VMEM(acc_shape, f32)] if nk > 1 else [],
        input_output_aliases=aliases,
        compiler_params=_cp("parallel", "parallel", "arbitrary"),
    )(*operands)


def _tile(n, t):
    return t if n % t == 0 and n >= t else n


DZ_TILE = 512


def _dz_matmul(name, pieces, other, weight_grad):
    T = pieces[0].shape[0]
    counts = [p.shape[1] // DZ_TILE for p in pieces]
    starts = [sum(counts[:i]) for i in range(len(counts))]
    steps = sum(counts)
    npc = len(pieces)

    def body(*refs):
        prefs, o_ref, rest = refs[:npc], refs[npc], refs[npc + 1:]
        k = pl.program_id(0)
        if not weight_grad:
            out_ref, acc_ref = rest

            @pl.when(k == 0)
            def _():
                acc_ref[...] = jnp.zeros_like(acc_ref)

        for p_ref, s, c in zip(prefs, starts, counts):
            @pl.when((k >= s) & (k < s + c))
            def _():
                if weight_grad:
                    rest[0][...] = lax.dot_general(o_ref[...], p_ref[...], TN, preferred_element_type=f32).astype(bf16)
                else:
                    acc_ref[...] += lax.dot_general(p_ref[...], o_ref[...], NT, preferred_element_type=f32)

        if not weight_grad:
            @pl.when(k == steps - 1)
            def _():
                out_ref[...] = acc_ref[...]

    piece_specs = [_bs((T, DZ_TILE), functools.partial(lambda k, s, c: (0, jnp.clip(k - s, 0, c - 1)), s=s, c=c))
                   for s, c in zip(starts, counts)]
    if weight_grad:
        other_spec, out_spec, out_shape, scratch = _bs((T, D), lambda k: (0, 0)), _bs((D, DZ_TILE), lambda k: (0, k)), S((D, NZ), bf16), []
    else:
        other_spec, out_spec, out_shape = _bs((D, DZ_TILE), lambda k: (0, k)), _bs((T, D), lambda k: (0, 0)), S((T, D), f32)
        scratch = [pltpu.VMEM((T, D), f32)]
    return pl.pallas_call(body, name=name, grid=(steps,), in_specs=piece_specs + [other_spec], out_specs=out_spec,
                          out_shape=out_shape, scratch_shapes=scratch, compiler_params=_cp("arbitrary"))(*pieces, other)


def _row(cb=0, w=D):
    return _bs((TT, w), lambda i: (i, cb))


def _vec(rows=1, w=D):
    return _bs((rows, w), lambda i: (0, 0))


def _acc_store(i, ref, val):
    @pl.when(i == 0)
    def _():
        ref[...] = val

    @pl.when(i > 0)
    def _():
        ref[...] += val


def _rms_fwd(name, x, g):
    T = x.shape[0]

    def body(x_ref, g_ref, o_ref):
        xv = x_ref[...]
        r = lax.rsqrt(jnp.mean(xv * xv, axis=-1, keepdims=True) + RMS_EPS)
        o_ref[...] = (xv * r * g_ref[...]).astype(bf16)

    return pl.pallas_call(body, name=name, grid=(T // TT,), in_specs=[_row(), _vec()], out_specs=_row(),
                          out_shape=S((T, D), bf16), compiler_params=_cp("parallel"))(x, g)


def _rms_bwd(name, dh, x, g, dres):
    T = x.shape[0]

    def body(dh_ref, x_ref, g_ref, dres_ref, dx_ref, dg_ref):
        i = pl.program_id(0)
        xv = x_ref[...]
        r = lax.rsqrt(jnp.mean(xv * xv, axis=-1, keepdims=True) + RMS_EPS)
        xhat = xv * r
        dh_v = dh_ref[...]
        dyg = dh_v * g_ref[...]
        m = jnp.mean(dyg * xhat, axis=-1, keepdims=True)
        dx_ref[...] = dres_ref[...] + r * (dyg - xhat * m)
        _acc_store(i, dg_ref, jnp.sum(dh_v * xhat, axis=0, keepdims=True))

    return pl.pallas_call(body, name=name, grid=(T // TT,), in_specs=[_row(), _row(), _vec(), _row()],
                          out_specs=[_row(), _vec()], out_shape=[S((T, D), f32), S((1, D), f32)],
                          compiler_params=_cp("arbitrary"))(dh, x, g, dres)


def _sigmoid(x):
    return 1.0 / (1.0 + jnp.exp(-x))


def _swiglu_fwd(name, h, wgu, l):
    T = h.shape[0]

    def body(h_ref, wg_ref, wu_ref, gu_ref, a_ref):
        hv = h_ref[...]
        g = lax.dot_general(hv, wg_ref[...], NT, preferred_element_type=f32)
        u = lax.dot_general(hv, wu_ref[...], NT, preferred_element_type=f32)
        gu_ref[0] = g
        gu_ref[1] = u
        a_ref[...] = (g * _sigmoid(g) * u).astype(bf16)

    return pl.pallas_call(
        body, name=name, grid=(4,),
        in_specs=[_bs((T, D), lambda j: (0, 0)), _bs((None, None, GU, D), lambda j: (l, j, 0, 0)),
                  _bs((None, None, GU, D), lambda j: (l, j + 4, 0, 0))],
        out_specs=[_bs((2, None, T, GU), lambda j: (0, j, 0, 0)), _bs((None, T, GU), lambda j: (j, 0, 0))],
        out_shape=[S((2, 4, T, GU), f32), S((4, T, GU), bf16)], compiler_params=_cp("parallel"))(h, wgu, wgu)


def _swiglu_bwd(name, dxo, wd, gu, l):
    T = dxo.shape[0]
    tm = _tile(T, 1024)

    def body(dx_ref, wd_ref, g_ref, u_ref, o_ref):
        da = 0.5 * lax.dot_general(dx_ref[...].astype(bf16), wd_ref[...], NT, preferred_element_type=f32)
        g = g_ref[...]
        sg = _sigmoid(g)
        o_ref[0] = (da * u_ref[...] * (sg + g * sg * (1.0 - sg))).astype(bf16)
        o_ref[1] = (da * g * sg).astype(bf16)

    return pl.pallas_call(
        body, name=name, grid=(T // tm, 4),
        in_specs=[_bs((tm, D), lambda i, j: (i, 0)), _bs((None, None, GU, D), lambda i, j: (l, j, 0, 0)),
                  _bs((None, None, tm, GU), lambda i, j: (0, j, i, 0)), _bs((None, None, tm, GU), lambda i, j: (1, j, i, 0))],
        out_specs=_bs((2, None, tm, GU), lambda i, j: (0, j, i, 0)), out_shape=S((2, 4, T, GU), bf16),
        compiler_params=_cp("parallel", "parallel"))(dxo, wd, gu, gu)


def _loss(name, y, tgt):
    T = y.shape[0]

    def body(y_ref, t_ref, l_ref, dy_ref):
        i = pl.program_id(0)
        e = y_ref[...] - t_ref[...]
        dy_ref[...] = e * (1.0 / D)
        s = 0.5 * jnp.sum(jnp.mean(e * e, axis=-1, keepdims=True))
        _acc_store(i, l_ref, jnp.broadcast_to(s, (1, 128)))

    return pl.pallas_call(body, name=name, grid=(T // TT,), in_specs=[_row(), _row()],
                          out_specs=[_vec(1, 128), _row()], out_shape=[S((1, 128), f32), S((T, D), f32)],
                          compiler_params=_cp("arbitrary"))(y, tgt)


def _prev8(T, cb):
    return _bs((8, D), lambda i: (jnp.maximum(i * (TT // 8) - 1, 0), cb))


def _next8(T, cb):
    return _bs((8, D), lambda i: (jnp.minimum((i + 1) * (TT // 8), T // 8 - 1), cb))


def _conv_taps(i, ac_ref, ax_ref, pc_ref, px_ref):
    ca = ac_ref[...] * ax_ref[...]
    keep = (i > 0).astype(f32)
    p1 = pc_ref[7:8, :] * px_ref[7:8, :] * keep
    p2 = pc_ref[6:7, :] * px_ref[6:7, :] * keep
    row = lax.broadcasted_iota(jnp.int32, ca.shape, 0)
    s1 = jnp.where(row == 0, p1, pltpu.roll(ca, 1, 0))
    s2 = jnp.where(row == 0, p2, jnp.where(row == 1, p1, pltpu.roll(ca, 2, 0)))
    return ca, s1, s2


def _conv_fwd(name, z, cw):
    T = z.shape[0]

    def body(ab_ref, ac_ref, ax_ref, pc_ref, px_ref, w_ref, o_ref):
        i = pl.program_id(0)
        ca, s1, s2 = _conv_taps(i, ac_ref, ax_ref, pc_ref, px_ref)
        cv = w_ref[0:1, :] * s2 + w_ref[1:2, :] * s1 + w_ref[2:3, :] * ca
        o_ref[...] = (ab_ref[...] * cv).astype(bf16)

    return pl.pallas_call(
        body, name=name, grid=(T // TT,),
        in_specs=[_row(0), _row(1), _row(2), _prev8(T, 1), _prev8(T, 2), _vec(3)],
        out_specs=_row(), out_shape=S((T, D), bf16), compiler_params=_cp("parallel"))(z, z, z, z, z, cw)


def _conv_bwd(name, dya, z, cw):
    T = z.shape[0]
    n = T // TT

    def body(dya_ref, ab_ref, ac_ref, ax_ref, pc_ref, px_ref, ndya_ref, nab_ref, w_ref, dz_ref, dw_ref):
        i = pl.program_id(0)
        ca, s1, s2 = _conv_taps(i, ac_ref, ax_ref, pc_ref, px_ref)
        w0, w1, w2 = w_ref[0:1, :], w_ref[1:2, :], w_ref[2:3, :]
        cv = w0 * s2 + w1 * s1 + w2 * ca
        dya_v = dya_ref[...]
        ab = ab_ref[...]
        dcv = dya_v * ab
        keep = (i < n - 1).astype(f32)
        n1 = ndya_ref[0:1, :] * nab_ref[0:1, :] * keep
        n2 = ndya_ref[1:2, :] * nab_ref[1:2, :] * keep
        row = lax.broadcasted_iota(jnp.int32, dcv.shape, 0)
        f1 = jnp.where(row == TT - 1, n1, pltpu.roll(dcv, TT - 1, 0))
        f2 = jnp.where(row == TT - 1, n2, jnp.where(row == TT - 2, n1, pltpu.roll(dcv, TT - 2, 0)))
        dca = w2 * dcv + w1 * f1 + w0 * f2
        dz_ref[:, 0:D] = (dya_v * cv).astype(bf16)
        dz_ref[:, D:2 * D] = (dca * ax_ref[...]).astype(bf16)
        dz_ref[:, 2 * D:3 * D] = (dca * ac_ref[...]).astype(bf16)
        dw = jnp.concatenate([jnp.sum(dcv * s2, axis=0, keepdims=True), jnp.sum(dcv * s1, axis=0, keepdims=True),
                              jnp.sum(dcv * ca, axis=0, keepdims=True)], axis=0)
        _acc_store(i, dw_ref, dw)

    return pl.pallas_call(
        body, name=name, grid=(n,),
        in_specs=[_row(), _row(0), _row(1), _row(2), _prev8(T, 1), _prev8(T, 2), _next8(T, 0), _next8(T, 0), _vec(3)],
        out_specs=[_row(0, 3 * D), _vec(3)], out_shape=[S((T, 3 * D), bf16), S((3, D), f32)],
        compiler_params=_cp("arbitrary"))(dya, z, z, z, z, z, dya, z, cw)


def _gelu(x):
    return 0.5 * x * (1.0 + lax.erf(x * INV_SQRT2))


def _gelu_cdf(x):
    return 0.5 * (1.0 + lax.erf(x * INV_SQRT2))


def _gelu_grad(x, cdf):
    return cdf + x * jnp.exp(-0.5 * x * x) * INV_SQRT2PI


def _ln_stats(vv):
    mu = jnp.mean(vv, axis=-1, keepdims=True)
    xc = vv - mu
    rstd = lax.rsqrt(jnp.mean(xc * xc, axis=-1, keepdims=True) + LN_EPS)
    return xc * rstd, rstd


def _tril_w(w_ref, g):
    r = lax.broadcasted_iota(jnp.int32, (HD, HD), 0)
    c = lax.broadcasted_iota(jnp.int32, (HD, HD), 1)
    return jnp.where(c <= r, w_ref[g], 0.0).astype(bf16)


def _sgu_fwd(name, z, ln_g, ln_b, w_s, bmap):
    T = z.shape[0]

    def body(su_ref, sv_ref, lg_ref, lb_ref, w_ref, bm_ref, o_ref, vn_ref):
        xhat, _ = _ln_stats(_gelu(sv_ref[...]))
        vn_ref[...] = (xhat * lg_ref[...] + lb_ref[...]).astype(bf16)
        for g in range(NH):
            w = _tril_w(w_ref, g)
            cs = slice(g * HD, (g + 1) * HD)
            for c in range(TT // HD):
                rs = slice(c * HD, (c + 1) * HD)
                s = jnp.dot(w, vn_ref[rs, cs], preferred_element_type=f32) + bm_ref[:, cs]
                o_ref[rs, cs] = (_gelu(su_ref[rs, cs]) * s).astype(bf16)

    return pl.pallas_call(
        body, name=name, grid=(T // TT,),
        in_specs=[_row(3), _row(4), _vec(), _vec(), _bs((NH, HD, HD), lambda i: (0, 0, 0)), _vec(HD)],
        out_specs=_row(), out_shape=S((T, D), bf16), scratch_shapes=[pltpu.VMEM((TT, D), bf16)],
        compiler_params=_cp("parallel"))(z, z, ln_g, ln_b, w_s, bmap)


def _sgu_bwd(name, dyb, z, ln_g, ln_b, w_s, bmap):
    T = z.shape[0]

    def body(dyb_ref, su_ref, sv_ref, lg_ref, lb_ref, w_ref, bm_ref, dz_ref, dlg_ref, dlb_ref, dw_ref, db_ref,
             vn_ref, du_ref, dvn_ref, cu_ref, cv_ref):
        i = pl.program_id(0)
        sv = sv_ref[...]
        cv_ref[...] = _gelu_cdf(sv)
        cu_ref[...] = _gelu_cdf(su_ref[...])
        xhat, rstd = _ln_stats(sv * cv_ref[...])
        vn_ref[...] = (xhat * lg_ref[...] + lb_ref[...]).astype(bf16)
        r = lax.broadcasted_iota(jnp.int32, (HD, HD), 0)
        cc = lax.broadcasted_iota(jnp.int32, (HD, HD), 1)
        for g in range(NH):
            w = _tril_w(w_ref, g)
            cs = slice(g * HD, (g + 1) * HD)
            dw = jnp.zeros((HD, HD), f32)
            db = jnp.zeros((HD, 1), f32)
            for c in range(TT // HD):
                rs = slice(c * HD, (c + 1) * HD)
                vnb = vn_ref[rs, cs]
                s = jnp.dot(w, vnb, preferred_element_type=f32) + bm_ref[:, cs]
                dy = dyb_ref[rs, cs]
                du_ref[rs, cs] = dy * s
                ds = dy * (su_ref[rs, cs] * cu_ref[rs, cs])
                ds16 = ds.astype(bf16)
                dvn_ref[rs, cs] = lax.dot_general(w, ds16, TN, preferred_element_type=f32)
                dw = dw + lax.dot_general(ds16, vnb, NT, preferred_element_type=f32)
                db = db + jnp.sum(ds, axis=1, keepdims=True)
            dw = jnp.where(cc <= r, dw, 0.0)

            @pl.when(i == 0)
            def _():
                dw_ref[g] = dw
                db_ref[:, g:g + 1] = db

            @pl.when(i > 0)
            def _():
                dw_ref[g] += dw
                db_ref[:, g:g + 1] += db

        dvn = dvn_ref[...]
        dxh = dvn * lg_ref[...]
        m1 = jnp.mean(dxh, axis=-1, keepdims=True)
        m2 = jnp.mean(dxh * xhat, axis=-1, keepdims=True)
        dvv = rstd * (dxh - m1 - xhat * m2)
        dz_ref[:, 0:D] = (du_ref[...] * _gelu_grad(su_ref[...], cu_ref[...])).astype(bf16)
        dz_ref[:, D:2 * D] = (dvv * _gelu_grad(sv, cv_ref[...])).astype(bf16)
        _acc_store(i, dlg_ref, jnp.sum(dvn * xhat, axis=0, keepdims=True))
        _acc_store(i, dlb_ref, jnp.sum(dvn, axis=0, keepdims=True))

    return pl.pallas_call(
        body, name=name, grid=(T // TT,),
        in_specs=[_row(), _row(3), _row(4), _vec(), _vec(), _bs((NH, HD, HD), lambda i: (0, 0, 0)), _vec(HD)],
        out_specs=[_row(0, 2 * D), _vec(), _vec(), _bs((NH, HD, HD), lambda i: (0, 0, 0)), _bs((HD, NH), lambda i: (0, 0))],
        out_shape=[S((T, 2 * D), bf16), S((1, D), f32), S((1, D), f32), S((NH, HD, HD), f32), S((HD, NH), f32)],
        scratch_shapes=[pltpu.VMEM((TT, D), bf16)] + [pltpu.VMEM((TT, D), f32)] * 4,
        compiler_params=_cp("arbitrary"))(dyb, z, z, ln_g, ln_b, w_s, bmap)


def _qk_fwd(name, z, qg, kg, bf):
    T = z.shape[0]

    def body(q_ref, k_ref, v_ref, zf_ref, qg_ref, kg_ref, bf_ref, qn_ref, kn_ref, vb_ref, lf_ref):
        for h in range(NH):
            cs = slice(h * HD, (h + 1) * HD)
            for src, gain, dst in ((q_ref, qg_ref, qn_ref), (k_ref, kg_ref, kn_ref)):
                xv = src[:, cs]
                r = lax.rsqrt(jnp.mean(xv * xv, axis=-1, keepdims=True) + RMS_EPS)
                dst[:, cs] = (xv * r * gain[:, cs]).astype(bf16)
        vb_ref[...] = v_ref[...].astype(bf16)
        xf = zf_ref[...] + bf_ref[...]
        lf_ref[...] = jnp.minimum(xf, 0.0) - jnp.log1p(jnp.exp(-jnp.abs(xf)))

    return pl.pallas_call(
        body, name=name, grid=(T // TT,),
        in_specs=[_row(5), _row(6), _row(7), _bs((TT, 128), lambda i: (i, F_OFF // 128)), _vec(), _vec(), _vec(1, 128)],
        out_specs=[_row(), _row(), _row(), _bs((TT, 128), lambda i: (i, 0))],
        out_shape=[S((T, D), bf16), S((T, D), bf16), S((T, D), bf16), S((T, 128), f32)],
        compiler_params=_cp("parallel"))(z, z, z, z, qg, kg, bf)


def _cum_fwd(name, logf):
    T = logf.shape[0]

    def body(lf_ref, ccol_ref, crow_ref, c_ref):
        c = lf_ref[...]
        row = lax.broadcasted_iota(jnp.int32, c.shape, 0)
        s = 1
        while s < T:
            c = c + jnp.where(row >= s, pltpu.roll(c, s, 0), 0.0)
            s *= 2
        c_ref[...] = c
        crow_ref[...] = c.T[0:NH, :]
        for h in range(NH):
            ccol_ref[h] = jnp.broadcast_to(c_ref[:, h:h + 1], (T, 128))

    return pl.pallas_call(body, name=name, out_shape=[S((NH, T, 128), f32), S((NH, T), f32)],
                          scratch_shapes=[pltpu.VMEM((T, 128), f32)],
                          compiler_params=pltpu.CompilerParams(vmem_limit_bytes=VMEM_LIMIT))(logf)


ATT_TILE = 1024


def _fold(x, op=jnp.add):
    acc = x[:, 0:128]
    for t in range(1, x.shape[1] // 128):
        acc = op(acc, x[:, t * 128:(t + 1) * 128])
    return acc


def _to_row(col):
    return jnp.broadcast_to(col, (col.shape[0], 128)).T[0:1, :]


def _causal(t, keys_down=False):
    r = lax.broadcasted_iota(jnp.int32, (t, t), 0)
    c = lax.broadcasted_iota(jnp.int32, (t, t), 1)
    return r <= c if keys_down else c <= r


def _attn_fwd(name, qn, kn, vb, ccol, crow3):
    T = qn.shape[0]
    tq = _tile(T, ATT_TILE)
    nq = T // tq

    def body(q_ref, k_ref, v_ref, cc_ref, cr_ref, o_ref, lse_ref, lser_ref, s_ref):
        qi = pl.program_id(1)
        q = q_ref[...]
        cq = cc_ref[:, 0:1]

        def logits(off):
            s = lax.dot_general(q, k_ref[pl.ds(off, tq), :], NT, preferred_element_type=f32) * ATT_SCALE
            return s + cq - cr_ref[:, pl.ds(off, tq)]

        def below(j, mvec):
            off = pl.multiple_of(j * tq, tq)
            s = logits(off)
            s_ref[:, pl.ds(off, tq)] = s
            return jnp.maximum(mvec, _fold(s, jnp.maximum))

        mvec = lax.fori_loop(0, qi, below, jnp.full((tq, 128), NEG, f32))
        off = pl.multiple_of(qi * tq, tq)
        s = jnp.where(_causal(tq), logits(off), NEG)
        s_ref[:, pl.ds(off, tq)] = s
        m = jnp.max(jnp.maximum(mvec, _fold(s, jnp.maximum)), axis=1, keepdims=True)

        def weigh(j, carry):
            lvec, acc = carry
            off = pl.multiple_of(j * tq, tq)
            p = jnp.exp(s_ref[:, pl.ds(off, tq)] - m)
            acc = acc + jnp.dot(p.astype(bf16), v_ref[pl.ds(off, tq), :], preferred_element_type=f32)
            return lvec + _fold(p), acc

        lvec, acc = lax.fori_loop(0, qi + 1, weigh, (jnp.zeros((tq, 128), f32), jnp.zeros((tq, HD), f32)))
        l = jnp.sum(lvec, axis=1, keepdims=True)
        o_ref[...] = acc / l
        lse = m + jnp.log(l)
        lse_ref[...] = jnp.broadcast_to(lse, (tq, 128))
        lser_ref[...] = _to_row(lse)

    return pl.pallas_call(
        body, name=name, grid=(NH, nq),
        in_specs=[_bs((tq, HD), lambda h, i: (i, h)), _bs((T, HD), lambda h, i: (0, h)), _bs((T, HD), lambda h, i: (0, h)),
                  _bs((None, tq, 128), lambda h, i: (h, i, 0)), _bs((None, 1, T), lambda h, i: (h, 0, 0))],
        out_specs=[_bs((tq, HD), lambda h, i: (i, h)), _bs((None, tq, 128), lambda h, i: (h, i, 0)),
                   _bs((None, 1, tq), lambda h, i: (h, 0, i))],
        out_shape=[S((T, D), f32), S((NH, T, 128), f32), S((NH, 1, T), f32)],
        scratch_shapes=[pltpu.VMEM((tq, T), f32)],
        compiler_params=_cp("parallel", "parallel"))(qn, kn, vb, ccol, crow3)


def _attn_dq(name, qn, kn, vb, do, lse, ccol, crow3):
    T = qn.shape[0]
    tq = _tile(T, ATT_TILE)
    nq = T // tq

    def body(q_ref, k_ref, v_ref, do_ref, lse_ref, cc_ref, cr_ref, dq_ref, dlr_ref, p_ref, dp_ref):
        qi = pl.program_id(1)
        q = q_ref[...]
        do16 = do_ref[...].astype(bf16)
        base = cc_ref[:, 0:1] - lse_ref[:, 0:1]

        def probs(off):
            s = lax.dot_general(q, k_ref[pl.ds(off, tq), :], NT, preferred_element_type=f32) * ATT_SCALE
            return jnp.exp(s + base - cr_ref[:, pl.ds(off, tq)])

        def keep(off, p, dvec):
            dp = lax.dot_general(do16, v_ref[pl.ds(off, tq), :], NT, preferred_element_type=f32)
            p_ref[:, pl.ds(off, tq)] = p
            dp_ref[:, pl.ds(off, tq)] = dp
            return dvec + _fold(p * dp)

        def below(j, dvec):
            off = pl.multiple_of(j * tq, tq)
            return keep(off, probs(off), dvec)

        dvec = lax.fori_loop(0, qi, below, jnp.zeros((tq, 128), f32))
        off = pl.multiple_of(qi * tq, tq)
        dvec = keep(off, jnp.where(_causal(tq), probs(off), 0.0), dvec)
        delta = jnp.sum(dvec, axis=1, keepdims=True)

        def grad(j, acc):
            off = pl.multiple_of(j * tq, tq)
            ds = p_ref[:, pl.ds(off, tq)] * (dp_ref[:, pl.ds(off, tq)] - delta)
            return acc + jnp.dot(ds.astype(bf16), k_ref[pl.ds(off, tq), :], preferred_element_type=f32)

        dq_ref[...] = lax.fori_loop(0, qi + 1, grad, jnp.zeros((tq, HD), f32)) * ATT_SCALE
        dlr_ref[...] = _to_row(delta)

    qb = lambda h, i: (i, h)
    full = lambda h, i: (0, h)
    col = lambda h, i: (h, i, 0)
    return pl.pallas_call(
        body, name=name, grid=(NH, nq),
        in_specs=[_bs((tq, HD), qb), _bs((T, HD), full), _bs((T, HD), full), _bs((tq, HD), qb),
                  _bs((None, tq, 128), col), _bs((None, tq, 128), col), _bs((None, 1, T), lambda h, i: (h, 0, 0))],
        out_specs=[_bs((tq, HD), qb), _bs((None, 1, tq), lambda h, i: (h, 0, i))],
        out_shape=[S((T, D), f32), S((NH, 1, T), f32)],
        scratch_shapes=[pltpu.VMEM((tq, T), f32), pltpu.VMEM((tq, T), f32)],
        compiler_params=_cp("parallel", "parallel"))(qn, kn, vb, do, lse, ccol, crow3)


def _attn_dkv(name, qn, kn, vb, do, lser3, dlr3, ccol, crow3):
    T = qn.shape[0]
    tk = _tile(T, ATT_TILE)
    nk = T // tk

    def body(q_ref, k_ref, v_ref, do_ref, lser_ref, dlr_ref, cc_ref, cr_ref, dk_ref, dv_ref, cs_ref):
        h = pl.program_id(0)
        kj = pl.program_id(1)

        @pl.when((h == 0) & (kj == 0))
        def _():
            cs_ref[...] = jnp.zeros_like(cs_ref)

        kb = k_ref[...]
        vv = v_ref[...]
        ckey = cc_ref[:, 0:1]

        def block(off, diagonal):
            rows = pl.ds(off, tk)
            qb = q_ref[rows, :]
            do16 = do_ref[rows, :].astype(bf16)
            st = lax.dot_general(kb, qb, NT, preferred_element_type=f32) * ATT_SCALE
            pt = jnp.exp(st + (cr_ref[:, rows] - lser_ref[:, rows]) - ckey)
            if diagonal:
                pt = jnp.where(_causal(tk, keys_down=True), pt, 0.0)
            dpt = lax.dot_general(vv, do16, NT, preferred_element_type=f32)
            dst = pt * (dpt - dlr_ref[:, rows])
            ddv = jnp.dot(pt.astype(bf16), do16, preferred_element_type=f32)
            ddk = jnp.dot(dst.astype(bf16), qb, preferred_element_type=f32)
            return ddk, ddv, _fold(dst)

        def above(i, carry):
            ddk, ddv, dcs = block(pl.multiple_of(i * tk, tk), False)
            return carry[0] + ddk, carry[1] + ddv, carry[2] + dcs

        off = pl.multiple_of(kj * tk, tk)
        dk, dv, cs = lax.fori_loop(kj + 1, nk, above, block(off, True))
        dk_ref[...] = dk * ATT_SCALE
        dv_ref[...] = dv
        lane = lax.broadcasted_iota(jnp.int32, (tk, 128), 1)
        cs_ref[pl.ds(off, tk), :] += jnp.where(lane == h, jnp.sum(cs, axis=1, keepdims=True), 0.0)

    full = lambda h, j: (0, h)
    blk = lambda h, j: (j, h)
    row = lambda h, j: (h, 0, 0)
    return pl.pallas_call(
        body, name=name, grid=(NH, nk),
        in_specs=[_bs((T, HD), full), _bs((tk, HD), blk), _bs((tk, HD), blk), _bs((T, HD), full), _bs((None, 1, T), row),
                  _bs((None, 1, T), row), _bs((None, tk, 128), lambda h, j: (h, j, 0)), _bs((None, 1, T), row)],
        out_specs=[_bs((tk, HD), blk), _bs((tk, HD), blk), _bs((T, 128), lambda h, j: (0, 0))],
        out_shape=[S((T, D), f32), S((T, D), f32), S((T, 128), f32)],
        compiler_params=_cp("arbitrary", "arbitrary"))(qn, kn, vb, do, lser3, dlr3, ccol, crow3)


def _forget_bwd(name, cs, z, bf):
    T = cs.shape[0]

    def body(cs_ref, zf_ref, bf_ref, dz_ref, db_ref):
        c = -cs_ref[...]
        row = lax.broadcasted_iota(jnp.int32, c.shape, 0)
        s = 1
        while s < T:
            c = c + jnp.where(row + s < T, pltpu.roll(c, T - s, 0), 0.0)
            s *= 2
        xf = zf_ref[...] + bf_ref[...]
        lane = lax.broadcasted_iota(jnp.int32, c.shape, 1)
        dxf = jnp.where(lane < NH, c / (1.0 + jnp.exp(xf)), 0.0)
        dz_ref[...] = jnp.zeros_like(dz_ref)
        dz_ref[:, 0:128] = dxf.astype(bf16)
        db_ref[...] = jnp.sum(dxf, axis=0, keepdims=True)

    return pl.pallas_call(
        body, name=name, grid=(1,),
        in_specs=[_bs((T, 128), lambda i: (0, 0)), _bs((T, 128), lambda i: (0, F_OFF // 128)), _vec(1, 128)],
        out_specs=[_bs((T, NZ - F_OFF), lambda i: (0, 0)), _vec(1, 128)],
        out_shape=[S((T, NZ - F_OFF), bf16), S((1, 128), f32)], compiler_params=_cp("arbitrary"))(cs, z, bf)


def _qk_bwd(name, dqn, dkn, dv, z, qg, kg):
    T = z.shape[0]

    def body(dq_ref, dk_ref, dv_ref, q_ref, k_ref, qg_ref, kg_ref, dz_ref, dqg_ref, dkg_ref, g_ref):
        i = pl.program_id(0)
        for n, (src, dsrc, gain, dgain) in enumerate(((q_ref, dq_ref, qg_ref, dqg_ref), (k_ref, dk_ref, kg_ref, dkg_ref))):
            for h in range(NH):
                cs = slice(h * HD, (h + 1) * HD)
                xv = src[:, cs]
                r = lax.rsqrt(jnp.mean(xv * xv, axis=-1, keepdims=True) + RMS_EPS)
                xhat = xv * r
                dy = dsrc[:, cs]
                dyg = dy * gain[:, cs]
                m = jnp.mean(dyg * xhat, axis=-1, keepdims=True)
                dz_ref[:, n * D + h * HD:n * D + (h + 1) * HD] = (r * (dyg - xhat * m)).astype(bf16)
                g_ref[:, cs] = jnp.sum(dy * xhat, axis=0, keepdims=True)
            _acc_store(i, dgain, g_ref[...])
        dz_ref[:, 2 * D:3 * D] = dv_ref[...].astype(bf16)

    return pl.pallas_call(
        body, name=name, grid=(T // TT,),
        in_specs=[_row(), _row(), _row(), _row(5), _row(6), _vec(), _vec()],
        out_specs=[_row(0, 3 * D), _vec(), _vec()], out_shape=[S((T, 3 * D), bf16), S((1, D), f32), S((1, D), f32)],
        scratch_shapes=[pltpu.VMEM((1, D), f32)], compiler_params=_cp("arbitrary"))(dqn, dkn, dv, z, z, qg, kg)


GB = GATE_OFF // D


def _merge_fwd(name, ya, yb, yc, z, bg):
    T = z.shape[0]

    def body(ya_ref, yb_ref, yc_ref, g0_ref, g1_ref, g2_ref, bg_ref, o_ref):
        acc = _sigmoid(g0_ref[...] + bg_ref[0:1, :]) * ya_ref[...]
        acc = acc + _sigmoid(g1_ref[...] + bg_ref[1:2, :]) * yb_ref[...]
        acc = acc + _sigmoid(g2_ref[...] + bg_ref[2:3, :]) * yc_ref[...]
        o_ref[...] = acc.astype(bf16)

    return pl.pallas_call(
        body, name=name, grid=(T // TT,),
        in_specs=[_row(), _row(), _row(), _row(GB), _row(GB + 1), _row(GB + 2), _vec(3)],
        out_specs=_row(), out_shape=S((T, D), bf16), compiler_params=_cp("parallel"))(ya, yb, yc, z, z, z, bg)


def _merge_bwd(name, dm, ya, yb, yc, z, bg):
    T = z.shape[0]

    def body(dm_ref, ya_ref, yb_ref, yc_ref, g0_ref, g1_ref, g2_ref, bg_ref, dya_ref, dyb_ref, dyc_ref, dz_ref, db_ref):
        i = pl.program_id(0)
        dm_v = dm_ref[...]
        dbs = []
        for n, (y_ref, g_ref, dy_ref) in enumerate(((ya_ref, g0_ref, dya_ref), (yb_ref, g1_ref, dyb_ref),
                                                    (yc_ref, g2_ref, dyc_ref))):
            gate = _sigmoid(g_ref[...] + bg_ref[n:n + 1, :])
            dy_ref[...] = (dm_v * gate).astype(bf16)
            dl = dm_v * y_ref[...] * gate * (1.0 - gate)
            dz_ref[:, n * D:(n + 1) * D] = dl.astype(bf16)
            dbs.append(jnp.sum(dl, axis=0, keepdims=True))
        _acc_store(i, db_ref, jnp.concatenate(dbs, axis=0))

    return pl.pallas_call(
        body, name=name, grid=(T // TT,),
        in_specs=[_row(), _row(), _row(), _row(), _row(GB), _row(GB + 1), _row(GB + 2), _vec(3)],
        out_specs=[_row(), _row(), _row(), _row(0, 3 * D), _vec(3)],
        out_shape=[S((T, D), bf16), S((T, D), bf16), S((T, D), bf16), S((T, 3 * D), bf16), S((3, D), f32)],
        compiler_params=_cp("arbitrary"))(dm, ya, yb, yc, z, z, z, bg)


SMALL_NAMES = ("ffn1_norm", "mix_norm", "b_forget", "b_gate", "conv_w", "sgu_ln_g", "sgu_ln_b", "sgu_w", "sgu_b",
               "q_norm_g", "k_norm_g", "ffn2_norm")


def _small_params(p):
    out = {n: p[n].reshape(1, D) for n in ("ffn1_norm", "mix_norm", "ffn2_norm", "sgu_ln_g", "sgu_ln_b", "q_norm_g", "k_norm_g")}
    out["b_forget"] = jnp.pad(p["b_forget"].reshape(1, NH), ((0, 0), (0, 128 - NH)))
    out["b_gate"] = p["b_gate"]
    out["conv_w"] = p["conv_w"]
    out["sgu_w"] = p["sgu_w"]
    out["bmap"] = jnp.repeat(p["sgu_b"].T, HD, axis=1)
    return out


def _small_grads_natural(sg):
    out = {n: sg[n].reshape(D) for n in ("ffn1_norm", "mix_norm", "ffn2_norm", "sgu_ln_g", "sgu_ln_b")}
    out["q_norm_g"] = sg["q_norm_g"].reshape(NH, HD)
    out["k_norm_g"] = sg["k_norm_g"].reshape(NH, HD)
    out["b_forget"] = sg["b_forget"][0, :NH]
    out["b_gate"] = sg["b_gate"]
    out["conv_w"] = sg["conv_w"]
    out["sgu_w"] = sg["sgu_w"]
    out["sgu_b"] = sg["sgu_b"]
    return out


SQ_TM = 1024


def _sq_fwd(name, a, wsq, l, n, res=None):
    T = a.shape[0]
    tm = _tile(T, SQ_TM)
    return _mm(name, a, wsq, grid=(T // tm, 1, 1), a_spec=_bs((tm, D), lambda i, j, k: (i, 0)),
               b_spec=_bs((None, None, D, D), lambda i, j, k: (l, n, 0, 0)),
               out_shape=S((T, D), f32), out_spec=_bs((tm, D), lambda i, j, k: (i, 0)), dims=NN, acc_shape=None,
               res=res, res_spec=_bs((tm, D), lambda i, j, k: (i, 0)))


def _sq_bwd_in(name, dy, wsq, l, n):
    T = dy.shape[0]
    tm = _tile(T, SQ_TM)
    return _mm(name, dy, wsq, grid=(T // tm, 1, 1), a_spec=_bs((tm, D), lambda i, j, k: (i, 0)),
               b_spec=_bs((None, None, D, D), lambda i, j, k: (l, n, 0, 0)),
               out_shape=S((T, D), f32), out_spec=_bs((tm, D), lambda i, j, k: (i, 0)), dims=NT, acc_shape=None)


def _sq_bwd_w(name, a, dy, gbuf, l):
    T = a.shape[0]
    return _mm(name, a, dy, grid=(NDEV // 2, 1, 1), a_spec=_bs((T, 256), lambda i, j, k: (0, i)),
               b_spec=_bs((T, D), lambda i, j, k: (0, 0)), out_shape=S(gbuf.shape, bf16),
               out_spec=_bs((2, None, None, 128, D), lambda i, j, k: (0, i, l, 0, 0)),
               dims=TN, acc_shape=None, alias=gbuf, split_rows=128)


def _ffn_fwd(tag, x, g, wgu, wd, l):
    T = x.shape[0]
    h = _rms_fwd(tag + "_rms", x, g)
    gu, a = _swiglu_fwd(tag + "_gu", h, wgu, l)
    tm = _tile(T, 1024)
    xo = _mm(tag + "_down", a, wd, grid=(T // tm, 1, 4), a_spec=_bs((None, tm, GU), lambda i, j, k: (k, i, 0)),
             b_spec=_bs((None, None, GU, D), lambda i, j, k: (l, k, 0, 0)), out_shape=S((T, D), f32),
             out_spec=_bs((tm, D), lambda i, j, k: (i, 0)), dims=NN, acc_shape=(tm, D), res=x,
             res_spec=_bs((tm, D), lambda i, j, k: (i, 0)), alpha=0.5)
    return xo, (h, gu, a)


def _ffn_bwd(tag, dxo, x, g, wgu, wd, l, saved, g_gu, g_d, ship, hook, hook_at_once=False):
    h, gu, a = saved
    T = x.shape[0]
    g_d = _mm(tag + "_dwd", a, dxo, grid=(4, 1, 1), a_spec=_bs((None, T, GU), lambda i, j, k: (i, 0, 0)),
              b_spec=_bs((T, D), lambda i, j, k: (0, 0)), out_shape=S(g_d.shape, bf16),
              out_spec=_bs((2, None, None, GU // 2, D), lambda i, j, k: (0, i, l, 0, 0)), dims=TN, acc_shape=None,
              alpha=0.5, alias=g_d, split_rows=GU // 2)
    dgu = _swiglu_bwd(tag + "_dgu", dxo, wd, gu, l).reshape(NDEV, T, GU)
    g_gu = _mm(tag + "_dwgu", dgu, h, grid=(NDEV, 1, 1), a_spec=_bs((None, T, GU), lambda i, j, k: (i, 0, 0)),
               b_spec=_bs((T, D), lambda i, j, k: (0, 0)), out_shape=S(g_gu.shape, bf16),
               out_spec=_bs((None, None, None, GU, D), lambda i, j, k: (i % 2, i // 2, l, 0, 0)), dims=TN,
               acc_shape=None, alias=g_gu)
    dxo = ship([g_gu, g_d], dxo)
    if hook_at_once:
        dxo = hook(dxo, g_gu)
    dh = _mm(tag + "_dh", dgu, wgu, grid=(1, 1, NDEV), a_spec=_bs((None, T, GU), lambda i, j, k: (k, 0, 0)),
             b_spec=_bs((None, None, GU, D), lambda i, j, k: (l, k, 0, 0)), out_shape=S((T, D), f32),
             out_spec=_bs((T, D), lambda i, j, k: (0, 0)), dims=NN, acc_shape=(T, D))
    if not hook_at_once:
        dxo = hook(dxo, dh)
    return _rms_bwd(tag + "_drms", dh, x, g, dxo)


def _mixer_fwd(tag, x, p, win, wsq, l):
    T = x.shape[0]
    h = _rms_fwd(tag + "_rms", x, p["mix_norm"])
    tn = 512
    z = _mm(tag + "_in", h, win, grid=(1, NZ // tn, 1), a_spec=_bs((T, D), lambda i, j, k: (0, 0)),
            b_spec=_bs((D, tn), lambda i, j, k: (0, j)), out_shape=S((T, NZ), f32),
            out_spec=_bs((T, tn), lambda i, j, k: (0, j)), dims=NN, acc_shape=None)
    ya_in = _conv_fwd(tag + "_conv", z, p["conv_w"])
    yb_in = _sgu_fwd(tag + "_sgu", z, p["sgu_ln_g"], p["sgu_ln_b"], p["sgu_w"], p["bmap"])
    qn, kn, vb, logf = _qk_fwd(tag + "_qk", z, p["q_norm_g"], p["k_norm_g"], p["b_forget"])
    ccol, crow = _cum_fwd(tag + "_cum", logf)
    crow3 = crow.reshape(NH, 1, T)
    o, lse, lser = _attn_fwd(tag + "_attn", qn, kn, vb, ccol, crow3)
    ya = _sq_fwd(tag + "_oconv", ya_in, wsq, l, 0)
    yb = _sq_fwd(tag + "_osgu", yb_in, wsq, l, 1)
    yc = _sq_fwd(tag + "_oattn", o, wsq, l, 2)
    merged = _merge_fwd(tag + "_merge", ya, yb, yc, z, p["b_gate"])
    xo = _sq_fwd(tag + "_o", merged, wsq, l, 3, res=x)
    return xo, (h, z, ya_in, yb_in, qn, kn, vb, ccol, crow3, o, lse, lser, ya, yb, yc, merged)


def _mixer_bwd(tag, dxo, x, p, win, wsq, l, saved, gsq, ship_sq, ship_in, hook):
    h, z, ya_in, yb_in, qn, kn, vb, ccol, crow3, o, lse, lser, ya, yb, yc, merged = saved
    T = x.shape[0]
    sg = {}
    dm = _sq_bwd_in(tag + "_dmerged", dxo, wsq, l, 3)
    gsq[3] = _sq_bwd_w(tag + "_dwo", merged, dxo, gsq[3], l)
    dya, dyb, dyc, dz_g, sg["b_gate"] = _merge_bwd(tag + "_dmerge", dm, ya, yb, yc, z, p["b_gate"])
    d_ya_in = _sq_bwd_in(tag + "_dconv_in", dya, wsq, l, 0)
    gsq[0] = _sq_bwd_w(tag + "_dwoc", ya_in, dya, gsq[0], l)
    d_yb_in = _sq_bwd_in(tag + "_dsgu_in", dyb, wsq, l, 1)
    gsq[1] = _sq_bwd_w(tag + "_dwos", yb_in, dyb, gsq[1], l)
    d_o = _sq_bwd_in(tag + "_dattn_in", dyc, wsq, l, 2)
    gsq[2] = _sq_bwd_w(tag + "_dwoa", o, dyc, gsq[2], l)
    dxo = ship_sq(gsq, dxo)
    dz_c, sg["conv_w"] = _conv_bwd(tag + "_dconv", d_ya_in, z, p["conv_w"])
    dxo = hook(dxo, dz_c)
    dz_s, sg["sgu_ln_g"], sg["sgu_ln_b"], sg["sgu_w"], db_t = _sgu_bwd(
        tag + "_dsgu", d_yb_in, z, p["sgu_ln_g"], p["sgu_ln_b"], p["sgu_w"], p["bmap"])
    sg["sgu_b"] = db_t.T
    dqn, dlr = _attn_dq(tag + "_dattn_q", qn, kn, vb, d_o, lse, ccol, crow3)
    dkn, dv, cs = _attn_dkv(tag + "_dattn_kv", qn, kn, vb, d_o, lser, dlr, ccol, crow3)
    dz_f, sg["b_forget"] = _forget_bwd(tag + "_dforget", cs, z, p["b_forget"])
    dz_q, sg["q_norm_g"], sg["k_norm_g"] = _qk_bwd(tag + "_dqk", dqn, dkn, dv, z, p["q_norm_g"], p["k_norm_g"])
    dz = [dz_c, dz_s, dz_q, dz_g, dz_f]
    dwin = _dz_matmul(tag + "_dwin", dz, h, True)
    dxo = ship_in(dwin, dxo)
    dh = _dz_matmul(tag + "_dh", dz, win, False)
    dxo = hook(dxo, dh)
    dx, sg["mix_norm"] = _rms_bwd(tag + "_drms", dh, x, p["mix_norm"], dxo)
    return dx, sg


ANY = pl.BlockSpec(memory_space=pl.ANY)
HBM = pl.BlockSpec(memory_space=pltpu.HBM)
SEM = pl.BlockSpec(memory_space=pltpu.SEMAPHORE)
EFFECT = pltpu.SideEffectType.DATAFLOW_SIDE_EFFECTING


def _place():
    return lax.axis_index("x"), lax.axis_index("y"), lax.axis_index("c")


NEAR = 4


def _others(x, y, c):
    return [(x, y, 1 - c), (1 - x, y, c), (x, 1 - y, c), (1 - x, 1 - y, c)]


def _gather_start(name, groups, carry=None):
    sizes = [len(g) for g in groups]
    srcs = [s for g in groups for s, _ in g]
    lands = [b for g in groups for _, b in g]
    n, ng = len(srcs), len(groups)
    held = srcs + lands + ([] if carry is None else [carry])
    nh = len(held)

    def body(*refs):
        src_refs, land_refs = refs[:n], refs[n:2 * n]
        send, recv = refs[nh:nh + ng], refs[nh + ng:nh + 2 * ng]
        x, y, c = _place()
        me = 4 * x + 2 * y + c
        u = 0
        for g, size in enumerate(sizes):
            for i in range(size):
                for k, peer in enumerate(_others(x, y, c)):
                    pltpu.make_async_remote_copy(src_ref=src_refs[u], dst_ref=land_refs[u].at[me],
                                                 send_sem=send[g].at[i * NEAR + k], recv_sem=recv[g].at[i * NEAR + k],
                                                 device_id=peer, device_id_type=MESH).start()
                u += 1

    sems = [pltpu.SemaphoreType.DMA((size * NEAR,)) for size in sizes]
    out = pl.pallas_call(
        body, name=name, in_specs=[HBM] * nh, out_specs=[SEM] * (2 * ng) + [HBM] * nh,
        out_shape=sems + sems + [pltpu.HBM(a.shape, a.dtype) for a in held],
        input_output_aliases={i: 2 * ng + i for i in range(nh)},
        compiler_params=pltpu.CompilerParams(has_side_effects=EFFECT),
    )(*[pltpu.with_memory_space_constraint(a, pltpu.HBM) for a in held])
    res, pos = [], 0
    for g, size in enumerate(sizes):
        res.append((out[g], out[ng + g], out[2 * ng + pos:2 * ng + pos + size], out[2 * ng + n + pos:2 * ng + n + pos + size]))
        pos += size
    return res if carry is None else (res, out[2 * ng + 2 * n])


def _gather_wait(name, started, after=None):
    send, recv, srcs, lands = started
    n = len(srcs)

    def body(*refs):
        src_refs, land_refs = refs[:n], refs[n:2 * n]
        send_ref, recv_ref = refs[2 * n], refs[2 * n + 1]
        x, y, c = _place()
        for i in range(n):
            for k, (px, py, pc) in enumerate(_others(x, y, c)):
                cp = pltpu.make_async_remote_copy(src_ref=src_refs[i], dst_ref=land_refs[i].at[4 * px + 2 * py + pc],
                                                  send_sem=send_ref.at[i * NEAR + k], recv_sem=recv_ref.at[i * NEAR + k],
                                                  device_id=(px, py, pc), device_id_type=MESH)
                cp.wait_send()
                cp.wait_recv()

    extra = [] if after is None else [after]
    out = pl.pallas_call(
        body, name=name, in_specs=[HBM] * (2 * n) + [SEM, SEM] + [ANY] * len(extra), out_specs=[HBM] * (2 * n),
        out_shape=[pltpu.HBM(a.shape, a.dtype) for a in list(srcs) + list(lands)],
        input_output_aliases={i: i for i in range(2 * n)},
        compiler_params=pltpu.CompilerParams(has_side_effects=EFFECT),
    )(*srcs, *lands, send, recv, *extra)
    return out[n:]


def _forward_copy(have, full, send, recv, i, j, core):
    x, y, c = _place()
    chip = [(1 - x, y), (x, 1 - y), (1 - x, 1 - y)][j]
    slot = 4 * chip[0] + 2 * chip[1] + core
    return pltpu.make_async_remote_copy(src_ref=have[i].at[slot], dst_ref=full[i].at[slot], send_sem=send.at[i * 3 + j],
                                        recv_sem=recv.at[i * 3 + j], device_id=(x, y, 1 - c), device_id_type=MESH)


def _gather_forward(name, lands):
    n = len(lands)

    def body(*refs):
        have, full = refs[:n], refs[n:2 * n]
        send, recv = refs[2 * n], refs[2 * n + 1]
        c = lax.axis_index("c")
        for i in range(n):
            for j in range(3):
                _forward_copy(have, full, send, recv, i, j, c).start()
        for i in range(n):
            for j in range(3):
                _forward_copy(have, full, send, recv, i, j, c).wait_send()
                _forward_copy(have, full, send, recv, i, j, 1 - c).wait_recv()

    return pl.pallas_call(
        body, name=name, in_specs=[ANY] * n, out_specs=[ANY] * n, out_shape=[S(a.shape, a.dtype) for a in lands],
        input_output_aliases={i: i for i in range(n)},
        scratch_shapes=[pltpu.SemaphoreType.DMA((n * 3,)), pltpu.SemaphoreType.DMA((n * 3,))],
    )(*lands)


def _forward_start(name, lands, carry):
    n = len(lands)
    held = list(lands) + [carry]

    def body(*refs):
        c = lax.axis_index("c")
        for i in range(n):
            for j in range(3):
                _forward_copy(refs[:n], refs[:n], refs[n + 1], refs[n + 2], i, j, c).start()

    sems = [pltpu.SemaphoreType.DMA((n * 3,))] * 2
    out = pl.pallas_call(
        body, name=name, in_specs=[HBM] * (n + 1), out_specs=[SEM, SEM] + [HBM] * (n + 1),
        out_shape=sems + [pltpu.HBM(a.shape, a.dtype) for a in held],
        input_output_aliases={i: 2 + i for i in range(n + 1)},
        compiler_params=pltpu.CompilerParams(has_side_effects=EFFECT),
    )(*[pltpu.with_memory_space_constraint(a, pltpu.HBM) for a in held])
    return (out[0], out[1], out[2:2 + n]), out[2 + n]


def _forward_wait(name, started, after):
    send, recv, lands = started
    n = len(lands)

    def body(*refs):
        c = lax.axis_index("c")
        for i in range(n):
            for j in range(3):
                _forward_copy(refs[:n], refs[:n], refs[n], refs[n + 1], i, j, c).wait_send()
                _forward_copy(refs[:n], refs[:n], refs[n], refs[n + 1], i, j, 1 - c).wait_recv()

    return pl.pallas_call(
        body, name=name, in_specs=[HBM] * n + [SEM, SEM, ANY], out_specs=[HBM] * n,
        out_shape=[pltpu.HBM(a.shape, a.dtype) for a in lands],
        input_output_aliases={i: i for i in range(n)},
        compiler_params=pltpu.CompilerParams(has_side_effects=EFFECT),
    )(*lands, send, recv, after)


def _pair_copies(ins, outs, send, recv):
    x, y, c = _place()
    return [pltpu.make_async_remote_copy(src_ref=ins[u].at[1 - c], dst_ref=outs[u], send_sem=send.at[u],
                                         recv_sem=recv.at[u], device_id=(x, y, 1 - c), device_id_type=MESH)
            for u in range(len(ins))]


def _pair_start(name, gs, carry):
    n = len(gs)
    held = list(gs) + [lax.empty(g.shape[1:], g.dtype) for g in gs] + [carry]

    def body(*refs):
        for cp in _pair_copies(refs[:n], refs[n:2 * n], refs[2 * n + 1], refs[2 * n + 2]):
            cp.start()

    sems = [pltpu.SemaphoreType.DMA((n,))] * 2
    out = pl.pallas_call(
        body, name=name, in_specs=[HBM] * len(held), out_specs=[SEM, SEM] + [HBM] * len(held),
        out_shape=sems + [pltpu.HBM(a.shape, a.dtype) for a in held],
        input_output_aliases={i: 2 + i for i in range(len(held))},
        compiler_params=pltpu.CompilerParams(has_side_effects=EFFECT),
    )(*[pltpu.with_memory_space_constraint(a, pltpu.HBM) for a in held])
    return (out[0], out[1], out[2:2 + n], out[2 + n:2 + 2 * n]), out[2 + 2 * n]


def _pair_wait(name, started, after):
    send, recv, gs, lands = started
    n = len(gs)

    def body(*refs):
        for cp in _pair_copies(refs[:n], refs[n:2 * n], refs[2 * n], refs[2 * n + 1]):
            cp.wait_send()
            cp.wait_recv()

    out = pl.pallas_call(
        body, name=name, in_specs=[HBM] * (2 * n) + [SEM, SEM, ANY], out_specs=[HBM] * (2 * n),
        out_shape=[pltpu.HBM(a.shape, a.dtype) for a in list(gs) + list(lands)],
        input_output_aliases={i: i for i in range(2 * n)},
        compiler_params=pltpu.CompilerParams(has_side_effects=EFFECT),
    )(*gs, *lands, send, recv, after)
    return out[:n], out[n:]


def _row_tile(r, c):
    if c > D and r % 128 == 0:
        return 128
    return 256 if r % 256 == 0 else (GU // 2 if r % (GU // 2) == 0 else r)


def _pair_sum(name, core, g, r1):
    _, nq, nl, r, c = g.shape
    tr = r
    g4 = g.reshape(2, nq * nl, r, c)
    r3 = r1.reshape(nq * nl, r, c)

    def body(core_ref, g_ref, r_ref, o_ref):
        o_ref[...] = (g_ref[...].astype(f32) + r_ref[...].astype(f32)).astype(bf16)

    out = pl.pallas_call(
        body, name=name,
        grid_spec=pltpu.PrefetchScalarGridSpec(
            num_scalar_prefetch=1, grid=(nq * nl, r // tr),
            in_specs=[_bs((None, None, tr, c), lambda b, i, cr: (cr[0], b, i, 0)), _bs((None, tr, c), lambda b, i, cr: (b, i, 0))],
            out_specs=_bs((None, tr, c), lambda b, i, cr: (b, i, 0))),
        out_shape=S((nq * nl, r, c), bf16), compiler_params=_cp("parallel", "parallel"))(core, g4, r3)
    return out.reshape(nq, nl, r, c)


def _scatter_copies(ins, outs, send, recv):
    x, y, c = _place()
    chips = [(1 - x, y), (x, 1 - y), (1 - x, 1 - y)]
    return [pltpu.make_async_remote_copy(src_ref=ins[u].at[2 * chip[0] + chip[1]], dst_ref=outs[u].at[k],
                                         send_sem=send.at[u * 3 + k], recv_sem=recv.at[u * 3 + k],
                                         device_id=(*chip, c), device_id_type=MESH)
            for u in range(len(ins)) for k, chip in enumerate(chips)]


def _scatter_start(name, ss, carry):
    n = len(ss)
    lands = [lax.empty((3,) + s.shape[1:], s.dtype) for s in ss]
    held = list(ss) + lands + [carry]

    def body(*refs):
        for cp in _scatter_copies(refs[:n], refs[n:2 * n], refs[2 * n + 1], refs[2 * n + 2]):
            cp.start()

    sems = [pltpu.SemaphoreType.DMA((n * 3,))] * 2
    out = pl.pallas_call(
        body, name=name, in_specs=[HBM] * len(held), out_specs=[SEM, SEM] + [HBM] * len(held),
        out_shape=sems + [pltpu.HBM(a.shape, a.dtype) for a in held],
        input_output_aliases={i: 2 + i for i in range(len(held))},
        compiler_params=pltpu.CompilerParams(has_side_effects=EFFECT),
    )(*[pltpu.with_memory_space_constraint(a, pltpu.HBM) for a in held])
    return (out[0], out[1], out[2:2 + n], out[2 + n:2 + 2 * n]), out[2 + 2 * n]


def _scatter_wait(name, started, after):
    send, recv, srcs, lands = started
    n = len(srcs)

    def body(*refs):
        for cp in _scatter_copies(refs[:n], refs[n:2 * n], refs[2 * n], refs[2 * n + 1]):
            cp.wait_send()
            cp.wait_recv()

    out = pl.pallas_call(
        body, name=name, in_specs=[HBM] * (2 * n) + [SEM, SEM, ANY], out_specs=[HBM] * (2 * n),
        out_shape=[pltpu.HBM(a.shape, a.dtype) for a in list(srcs) + list(lands)],
        input_output_aliases={i: i for i in range(2 * n)},
        compiler_params=pltpu.CompilerParams(has_side_effects=EFFECT),
    )(*srcs, *lands, send, recv, after)
    return out[:n], out[n:]


def _sum_blocks(name, blocks):
    def body(b_ref, o_ref):
        acc = b_ref[0]
        for d in range(1, NDEV):
            acc = acc + b_ref[d]
        o_ref[...] = acc

    return pl.pallas_call(body, name=name, out_shape=S(blocks.shape[1:], f32),
                          compiler_params=pltpu.CompilerParams(vmem_limit_bytes=VMEM_LIMIT))(blocks)


def _adam_math(w, g, m, v):
    m = ADAM_B1 * m + (1.0 - ADAM_B1) * g
    v = ADAM_B2 * v + (1.0 - ADAM_B2) * (g * g)
    m_hat = m / (1.0 - ADAM_B1 ** ADAM_STEP)
    v_hat = v / (1.0 - ADAM_B2 ** ADAM_STEP)
    delta = -ADAM_LR * (m_hat / (jnp.sqrt(v_hat) + ADAM_EPS) + ADAM_WD * w)
    return delta, m, v


def _adamw(name, chip, w, m, v, parts):
    _, r, c = w.shape
    tr = _row_tile(r, c)

    def body(chip_ref, w_ref, m_ref, v_ref, *refs):
        sets, (g_ref, d_ref, mo_ref, vo_ref) = (refs[0:4], refs[4:8]), refs[8:]
        for l in range(2):
            @pl.when(pl.program_id(0) == l)
            def _():
                s_ref, r0_ref, r1_ref, r2_ref = sets[l]
                g = ((s_ref[...].astype(f32) + r0_ref[...].astype(f32)) + r1_ref[...].astype(f32)) + r2_ref[...].astype(f32)
                g_ref[...] = g
                d_ref[...], mo_ref[...], vo_ref[...] = _adam_math(w_ref[...], g, m_ref[...], v_ref[...])

    blk = _bs((None, tr, c), lambda l, i, cr: (l, i, 0))
    operands, specs = [], []
    for n in range(2):
        row = (lambda l, i: i * (1 - l)) if n == 0 else (lambda l, i: i * l)
        sums, r2 = parts[n]
        operands += [sums, r2, r2, r2]
        specs.append(_bs((None, None, tr, c), functools.partial(lambda l, i, cr, row: (cr[0], 0, row(l, i), 0), row=row)))
        specs += [_bs((None, None, tr, c), functools.partial(lambda l, i, cr, k, row: (k, 0, row(l, i), 0), k=k, row=row))
                  for k in range(3)]
    return pl.pallas_call(
        body, name=name,
        grid_spec=pltpu.PrefetchScalarGridSpec(num_scalar_prefetch=1, grid=(2, r // tr), in_specs=[blk, blk, blk] + specs,
                                               out_specs=[blk] * 4),
        out_shape=[S(w.shape, f32)] * 4, compiler_params=_cp("arbitrary", "arbitrary"),
    )(chip, w, m, v, *operands)


def _adamw_small(name, w, g, m, v):
    def body(w_ref, g_ref, m_ref, v_ref, d_ref, mo_ref, vo_ref):
        d_ref[...], mo_ref[...], vo_ref[...] = _adam_math(w_ref[...], g_ref[...], m_ref[...], v_ref[...])

    return pl.pallas_call(body, name=name, out_shape=[S(w.shape, f32)] * 3,
                          compiler_params=pltpu.CompilerParams(vmem_limit_bytes=VMEM_LIMIT))(w, g, m, v)


WEIGHT_NAMES = ("ffn1_norm", "ffn1_w_gu", "ffn1_w_down", "mix_norm", "w_in", "b_forget", "b_gate", "conv_w", "sgu_ln_g",
                "sgu_ln_b", "sgu_w", "sgu_b", "q_norm_g", "k_norm_g", "w_out_conv", "w_out_sgu", "w_out_attn", "w_o",
                "ffn2_norm", "ffn2_w_gu", "ffn2_w_down")
BIG = {"ffn1_w_gu": "gu1", "ffn2_w_gu": "gu2", "ffn1_w_down": "d1", "ffn2_w_down": "d2", "w_in": "in",
       "w_out_conv": "oc", "w_out_sgu": "os", "w_out_attn": "oa", "w_o": "o"}
BIG_KEYS = ("gu1", "gu2", "d1", "d2", "in", "oc", "os", "oa", "o")
REPLICATED_SMALL = ("ffn1_norm", "mix_norm", "b_forget", "sgu_ln_g", "sgu_ln_b", "sgu_w", "sgu_b", "q_norm_g",
                    "k_norm_g", "ffn2_norm")
SHARDED_SMALL = ("b_gate", "conv_w")
TRANSPOSED = ("gu1", "gu2")


def _packed_rows(shape):
    size = 1
    for s_ in shape:
        size *= s_
    return size, -(-size // 1024) * 8


def _pack(arrays):
    pieces = []
    for a in arrays:
        size, rows = _packed_rows(a.shape)
        pieces.append(jnp.pad(a.reshape(-1).astype(f32), (0, rows * 128 - size)).reshape(rows, 128))
    return jnp.concatenate(pieces, axis=0)


def _unpack(packed, shapes):
    out, pos = [], 0
    for shp in shapes:
        size, rows = _packed_rows(shp)
        out.append(packed[pos:pos + rows].reshape(-1)[:size].reshape(shp))
        pos += rows
    return out


def _natural_runs(a, b):
    runs = []
    while a < b:
        d = a // INB
        e = min(b, (d + 1) * INB)
        runs.append((d, a - d * INB, e - d * INB))
        a = e
    return runs


def _win_kernel_layout(wg):
    runs = _natural_runs(0, GATE_OFF) + _natural_runs(GATE_OFF + NH, NIN) + _natural_runs(GATE_OFF, GATE_OFF + NH)
    return jnp.concatenate([wg[d, :, a:b] for d, a, b in runs] + [jnp.zeros((D, NZ - NIN), wg.dtype)], axis=1)


def _kernel_column(n):
    return n if n < GATE_OFF else (F_OFF + n - GATE_OFF if n < GATE_OFF + NH else n - NH)


def _win_device_block(dw, d):
    cuts = sorted({d * INB, (d + 1) * INB} | {c for c in (GATE_OFF, GATE_OFF + NH) if d * INB < c < (d + 1) * INB})
    parts = [dw[:, _kernel_column(a):_kernel_column(a) + (b - a)] for a, b in zip(cuts[:-1], cuts[1:])]
    return parts[0] if len(parts) == 1 else jnp.concatenate(parts, axis=1)


def kernel(x, ffn1_norm, ffn1_w_gu, ffn1_w_down, mix_norm, w_in, b_forget, b_gate, conv_w, sgu_ln_g, sgu_ln_b, sgu_w, sgu_b, q_norm_g, k_norm_g, w_out_conv, w_out_sgu, w_out_attn, w_o, ffn2_norm, ffn2_w_gu, ffn2_w_down, loss_target, m_ffn1_norm, m_ffn1_w_gu, m_ffn1_w_down, m_mix_norm, m_w_in, m_b_forget, m_b_gate, m_conv_w, m_sgu_ln_g, m_sgu_ln_b, m_sgu_w, m_sgu_b, m_q_norm_g, m_k_norm_g, m_w_out_conv, m_w_out_sgu, m_w_out_attn, m_w_o, m_ffn2_norm, m_ffn2_w_gu, m_ffn2_w_down, v_ffn1_norm, v_ffn1_w_gu, v_ffn1_w_down, v_mix_norm, v_w_in, v_b_forget, v_b_gate, v_conv_w, v_sgu_ln_g, v_sgu_ln_b, v_sgu_w, v_sgu_b, v_q_norm_g, v_k_norm_g, v_w_out_conv, v_w_out_sgu, v_w_out_attn, v_w_o, v_ffn2_norm, v_ffn2_w_gu, v_ffn2_w_down):
    w = dict(zip(WEIGHT_NAMES, (ffn1_norm, ffn1_w_gu, ffn1_w_down, mix_norm, w_in, b_forget, b_gate, conv_w, sgu_ln_g,
                                sgu_ln_b, sgu_w, sgu_b, q_norm_g, k_norm_g, w_out_conv, w_out_sgu, w_out_attn, w_o,
                                ffn2_norm, ffn2_w_gu, ffn2_w_down)))
    mom = dict(zip(WEIGHT_NAMES, (m_ffn1_norm, m_ffn1_w_gu, m_ffn1_w_down, m_mix_norm, m_w_in, m_b_forget, m_b_gate,
                                  m_conv_w, m_sgu_ln_g, m_sgu_ln_b, m_sgu_w, m_sgu_b, m_q_norm_g, m_k_norm_g,
                                  m_w_out_conv, m_w_out_sgu, m_w_out_attn, m_w_o, m_ffn2_norm, m_ffn2_w_gu,
                                  m_ffn2_w_down)))
    var = dict(zip(WEIGHT_NAMES, (v_ffn1_norm, v_ffn1_w_gu, v_ffn1_w_down, v_mix_norm, v_w_in, v_b_forget, v_b_gate,
                                  v_conv_w, v_sgu_ln_g, v_sgu_ln_b, v_sgu_w, v_sgu_b, v_q_norm_g, v_k_norm_g,
                                  v_w_out_conv, v_w_out_sgu, v_w_out_attn, v_w_o, v_ffn2_norm, v_ffn2_w_gu,
                                  v_ffn2_w_down)))
    px, py, pc = _place()
    dev = 4 * px + 2 * py + pc
    chip = 2 * px + py

    big_names = [n for n in WEIGHT_NAMES if n in BIG]
    key_name = {BIG[n]: n for n in big_names}
    group_keys = (("gu1", "d1"), ("in", "oc", "os", "oa", "o", "small"), ("gu2", "d2"))

    def source(key, l):
        if key == "small":
            return jnp.concatenate([w["b_gate"][l], w["conv_w"][l], jnp.zeros((2, 128), f32)], axis=0)
        block = w[key_name[key]][l]
        return (block.T if key in TRANSPOSED else block).astype(bf16)

    def landing(src):
        return lax.dynamic_update_slice(lax.empty((NDEV,) + src.shape, src.dtype), src[None], (dev, 0, 0))

    groups = [[(s, landing(s)) for s in (source(k, l) for k in keys)] for l in range(2) for keys in group_keys]
    (send0, recv0, srcs0, lands0), = _gather_start("gather_start_first", groups[:1])
    rest, src0 = _gather_start("gather_start_rest", groups[1:], carry=srcs0[0])
    started = [(send0, recv0, [src0] + list(srcs0[1:]), lands0)] + rest

    early = {}

    def prefetch(l, part, act):
        got = _gather_wait(f"gather_wait_{l}_{part}", started[3 * l + part], act)
        early[l, part], act = _forward_start(f"forward_start_{l}_{part}", got, act)
        return act

    def weights(l, part, after):
        if (l, part) in early:
            lands = _forward_wait(f"forward_wait_{l}_{part}", early.pop((l, part)), after)
        else:
            got = _gather_wait(f"gather_wait_{l}_{part}", started[3 * l + part], after)
            lands = _gather_forward(f"gather_forward_{l}_{part}", got)
        return dict(zip(group_keys[part], lands))

    xl = x[0]
    saved, small, wts = [], [], []
    for l in range(2):
        ga = weights(l, 0, xl if l else None)
        if l:
            xl = prefetch(l, 1, xl)
        wt = {"gu1": ga["gu1"][None], "d1": ga["d1"].reshape(1, 4, GU, D)}
        x1, s1 = _ffn_fwd("ffn1", xl, w["ffn1_norm"][l].reshape(1, D), wt["gu1"], wt["d1"], 0)
        gb = weights(l, 1, x1)
        x1 = prefetch(l, 2, x1)
        p = {n: w[n][l] for n in REPLICATED_SMALL}
        p["b_gate"] = jnp.transpose(gb["small"][:, 0:3, :], (1, 0, 2)).reshape(3, D)
        p["conv_w"] = jnp.transpose(gb["small"][:, 3:6, :], (1, 0, 2)).reshape(3, D)
        p = _small_params(p)
        wt["win"] = _win_kernel_layout(gb["in"])
        wt["sq"] = jnp.stack([gb[k].reshape(D, D) for k in ("oc", "os", "oa", "o")])[None]
        x2, sm = _mixer_fwd("mix", x1, p, wt["win"], wt["sq"], 0)
        gc = weights(l, 2, x2)
        if l == 0:
            x2 = prefetch(1, 0, x2)
        wt.update({"gu2": gc["gu2"][None], "d2": gc["d2"].reshape(1, 4, GU, D)})
        x3, s2 = _ffn_fwd("ffn2", x2, p["ffn2_norm"], wt["gu2"], wt["d2"], 0)
        saved.append((xl, x1, x2, s1, sm, s2))
        small.append(p)
        wts.append(wt)
        xl = x3
    loss_row, dx = _loss("loss", xl, loss_target[0])

    core = pc.reshape(1).astype(jnp.int32)
    chip_op = chip.reshape(1).astype(jnp.int32)
    buf = lambda r, c: lax.empty((2, 4, 1, r, c), bf16)
    flights = {}

    pairs = []
    scatter_keys = {"ffn2": ("gu2", "d2"), "sq": ("oc", "os", "oa", "o"), "in": ("in",), "ffn1": ("gu1", "d1")}

    def ship(l, part, bufs, dx):
        started, dx = _pair_start(f"pair_start_{l}_{part}", bufs, dx)
        pairs.append((l, part, started))
        return dx

    def finish(dx, after):
        l, part, started = pairs.pop()
        bufs, r1 = _pair_wait(f"pair_wait_{l}_{part}", started, after)
        ss = [_pair_sum(f"pair_sum_{l}_{k}", core, g, r) for k, g, r in zip(scatter_keys[part], bufs, r1)]
        flights[l, part], dx = _scatter_start(f"scatter_start_{l}_{part}", ss, dx)
        return dx

    def ship_in(l, dwin, dx):
        g_in = jnp.stack([_win_device_block(dwin, 2 * q + c) for c in range(2) for q in range(4)]).reshape(2, 4, 1, D, INB)
        return ship(l, "in", [g_in], dx)

    sgrads = [None, None]
    for l in (1, 0):
        p, wt = small[l], wts[l]
        x0, x1, x2, s1, sm, s2 = saved[l]
        dx, dn2 = _ffn_bwd("ffn2", dx, x2, p["ffn2_norm"], wt["gu2"], wt["d2"], 0, s2, buf(GU, D), buf(GU // 2, D),
                           functools.partial(ship, l, "ffn2"), finish)
        dx, sg = _mixer_bwd("mix", dx, x1, p, wt["win"], wt["sq"], 0, sm, [buf(128, D) for _ in range(4)],
                            functools.partial(ship, l, "sq"), functools.partial(ship_in, l), finish)
        dx, dn1 = _ffn_bwd("ffn1", dx, x0, p["ffn1_norm"], wt["gu1"], wt["d1"], 0, s1, buf(GU, D), buf(GU // 2, D),
                           functools.partial(ship, l, "ffn1"), finish, hook_at_once=(l == 0))
        sg["ffn1_norm"] = dn1
        sg["ffn2_norm"] = dn2
        sgrads[l] = sg

    nat = [_small_grads_natural(sgrads[l]) for l in range(2)]
    order = REPLICATED_SMALL + SHARDED_SMALL
    part = _pack([jnp.stack([nat[0][n], nat[1][n]]) for n in order] + [loss_row[0, 0:1]])
    small_flight, dx = _gather_start("small_start", [[(part, landing(part))]], carry=dx)

    grads, delta, new_m, new_v = {}, {}, {}, {}
    after = dx
    for part in ("ffn2", "sq", "in", "ffn1"):
        sets = []
        for l in (1, 0):
            s_all, r2_all = _scatter_wait(f"scatter_wait_{l}_{part}", flights[l, part], after)
            sets.append(list(zip(s_all, r2_all)))
        for i, k in enumerate(scatter_keys[part]):
            n = key_name[k]
            view = (lambda a: jnp.swapaxes(a, 1, 2)) if k in TRANSPOSED else (lambda a: a)
            outs = _adamw("adamw_" + k, chip_op, view(w[n]), view(mom[n]), view(var[n]), [sets[1][i], sets[0][i]])
            grads[n], delta[n], new_m[n], new_v[n] = [view(o) for o in outs]
            after = outs[1]

    blocks = _gather_forward("small_forward", _gather_wait("small_wait", small_flight[0], after))
    total = _sum_blocks("small_sum", blocks[0])
    full_shapes = [(2,) + tuple(nat[0][n].shape) for n in order] + [(1,)]
    summed = dict(zip(order + ("loss",), _unpack(total, full_shapes)))
    for n in REPLICATED_SMALL:
        grads[n] = summed[n]
    for n in SHARDED_SMALL:
        grads[n] = lax.dynamic_slice_in_dim(summed[n], dev * 128, 128, axis=2)
    wp = _pack([w[n] for n in order])
    gp = _pack([grads[n] for n in order])
    mp = _pack([mom[n] for n in order])
    vp = _pack([var[n] for n in order])
    dpk, mpk, vpk = _adamw_small("adamw_small", wp, gp, mp, vp)
    local_shapes = [tuple(w[n].shape) for n in order]
    for dst, packed in ((delta, dpk), (new_m, mpk), (new_v, vpk)):
        dst.update(zip(order, _unpack(packed, local_shapes)))

    loss = summed["loss"][0]
    return (loss, dx[None], *[grads[n] for n in WEIGHT_NAMES], *[delta[n] for n in WEIGHT_NAMES],
            *[new_m[n] for n in WEIGHT_NAMES], *[new_v[n] for n in WEIGHT_NAMES])
```

```python
import functools

import jax
import jax.numpy as jnp
from jax import lax
from jax.experimental import pallas as pl
from jax.experimental.pallas import tpu as pltpu

f32 = jnp.float32
bf16 = jnp.bfloat16
S = jax.ShapeDtypeStruct
MESH = pl.DeviceIdType.MESH

D = 1024
NH = 8
HD = 128
NDEV = 8
GU = 704
NIN = 11272
INB = 1409
GATE_OFF = 8192
F_OFF = 11264
NZ = 11776
RMS_EPS = 1e-6
LN_EPS = 1e-5
ATT_SCALE = HD ** -0.5
NEG = -1e30
INV_SQRT2 = 0.7071067811865476
INV_SQRT2PI = 0.3989422804014327

ADAM_LR = 0.001
ADAM_B1 = 0.9
ADAM_B2 = 0.999
ADAM_EPS = 1e-08
ADAM_WD = 0.01
ADAM_STEP = 10

TT = 512
VMEM_LIMIT = 56 * 1024 * 1024


def _cp(*sem):
    return pltpu.CompilerParams(dimension_semantics=sem, vmem_limit_bytes=VMEM_LIMIT)


def _bs(shape, fn):
    return pl.BlockSpec(shape, fn)


NN = (((1,), (0,)), ((), ()))
NT = (((1,), (1,)), ((), ()))
TN = (((0,), (0,)), ((), ()))


def _mm(name, a, b, *, grid, a_spec, b_spec, out_shape, out_spec, dims, acc_shape, res=None, res_spec=None,
        alpha=1.0, alias=None, split_rows=None):
    nk = grid[2]

    def body(*refs):
        a_ref, b_ref = refs[0], refs[1]
        pos = 2
        res_ref = None
        if res is not None:
            res_ref = refs[pos]
            pos += 1
        if alias is not None:
            pos += 1
        o_ref = refs[pos]
        acc_ref = refs[pos + 1] if nk > 1 else None
        part = lax.dot_general(a_ref[...].astype(bf16), b_ref[...].astype(bf16), dims, preferred_element_type=f32)

        def finish(acc):
            if alpha != 1.0:
                acc = alpha * acc
            if res_ref is not None:
                acc = res_ref[...] + acc
            if split_rows is None:
                o_ref[...] = acc.astype(o_ref.dtype)
            else:
                o_ref[0] = acc[:split_rows].astype(o_ref.dtype)
                o_ref[1] = acc[split_rows:].astype(o_ref.dtype)

        if nk == 1:
            finish(part)
        else:
            k = pl.program_id(2)

            @pl.when(k == 0)
            def _():
                acc_ref[...] = part

            @pl.when(k > 0)
            def _():
                acc_ref[...] += part

            @pl.when(k == nk - 1)
            def _():
                finish(acc_ref[...])

    operands = [a, b]
    in_specs = [a_spec, b_spec]
    if res is not None:
        operands.append(res)
        in_specs.append(res_spec)
    aliases = {}
    if alias is not None:
        aliases = {len(operands): 0}
        operands.append(alias)
        in_specs.append(pl.BlockSpec(memory_space=pl.ANY))
    return pl.pallas_call(
        body, name=name, grid=grid, in_specs=in_specs, out_specs=out_spec, out_shape=out_shape,
        scratch_shapes=[pltpu.VMEM(acc_shape, f32)] if nk > 1 else [],
        input_output_aliases=aliases,
        compiler_params=_cp("parallel", "parallel", "arbitrary"),
    )(*operands)


def _tile(n, t):
    return t if n % t == 0 and n >= t else n


DZ_TILE = 512


def _dz_matmul(name, pieces, other, weight_grad):
    T = pieces[0].shape[0]
    counts = [p.shape[1] // DZ_TILE for p in pieces]
    starts = [sum(counts[:i]) for i in range(len(counts))]
    steps = sum(counts)
    npc = len(pieces)

    def body(*refs):
        prefs, o_ref, rest = refs[:npc], refs[npc], refs[npc + 1:]
        k = pl.program_id(0)
        if not weight_grad:
            out_ref, acc_ref = rest

            @pl.when(k == 0)
            def _():
                acc_ref[...] = jnp.zeros_like(acc_ref)

        for p_ref, s, c in zip(prefs, starts, counts):
            @pl.when((k >= s) & (k < s + c))
            def _():
                if weight_grad:
                    rest[0][...] = lax.dot_general(o_ref[...], p_ref[...], TN, preferred_element_type=f32).astype(bf16)
                else:
                    acc_ref[...] += lax.dot_general(p_ref[...], o_ref[...], NT, preferred_element_type=f32)

        if not weight_grad:
            @pl.when(k == steps - 1)
            def _():
                out_ref[...] = acc_ref[...]

    piece_specs = [_bs((T, DZ_TILE), functools.partial(lambda k, s, c: (0, jnp.clip(k - s, 0, c - 1)), s=s, c=c))
                   for s, c in zip(starts, counts)]
    if weight_grad:
        other_spec, out_spec, out_shape, scratch = _bs((T, D), lambda k: (0, 0)), _bs((D, DZ_TILE), lambda k: (0, k)), S((D, NZ), bf16), []
    else:
        other_spec, out_spec, out_shape = _bs((D, DZ_TILE), lambda k: (0, k)), _bs((T, D), lambda k: (0, 0)), S((T, D), f32)
        scratch = [pltpu.VMEM((T, D), f32)]
    return pl.pallas_call(body, name=name, grid=(steps,), in_specs=piece_specs + [other_spec], out_specs=out_spec,
                          out_shape=out_shape, scratch_shapes=scratch, compiler_params=_cp("arbitrary"))(*pieces, other)


def _row(cb=0, w=D):
    return _bs((TT, w), lambda i: (i, cb))


def _vec(rows=1, w=D):
    return _bs((rows, w), lambda i: (0, 0))


def _acc_store(i, ref, val):
    @pl.when(i == 0)
    def _():
        ref[...] = val

    @pl.when(i > 0)
    def _():
        ref[...] += val


def _rms_fwd(name, x, g):
    T = x.shape[0]

    def body(x_ref, g_ref, o_ref):
        xv = x_ref[...]
        r = lax.rsqrt(jnp.mean(xv * xv, axis=-1, keepdims=True) + RMS_EPS)
        o_ref[...] = (xv * r * g_ref[...]).astype(bf16)

    return pl.pallas_call(body, name=name, grid=(T // TT,), in_specs=[_row(), _vec()], out_specs=_row(),
                          out_shape=S((T, D), bf16), compiler_params=_cp("parallel"))(x, g)


def _rms_bwd(name, dh, x, g, dres):
    T = x.shape[0]

    def body(dh_ref, x_ref, g_ref, dres_ref, dx_ref, dg_ref):
        i = pl.program_id(0)
        xv = x_ref[...]
        r = lax.rsqrt(jnp.mean(xv * xv, axis=-1, keepdims=True) + RMS_EPS)
        xhat = xv * r
        dh_v = dh_ref[...]
        dyg = dh_v * g_ref[...]
        m = jnp.mean(dyg * xhat, axis=-1, keepdims=True)
        dx_ref[...] = dres_ref[...] + r * (dyg - xhat * m)
        _acc_store(i, dg_ref, jnp.sum(dh_v * xhat, axis=0, keepdims=True))

    return pl.pallas_call(body, name=name, grid=(T // TT,), in_specs=[_row(), _row(), _vec(), _row()],
                          out_specs=[_row(), _vec()], out_shape=[S((T, D), f32), S((1, D), f32)],
                          compiler_params=_cp("arbitrary"))(dh, x, g, dres)


def _sigmoid(x):
    return 1.0 / (1.0 + jnp.exp(-x))


def _swiglu_fwd(name, h, wgu, l):
    T = h.shape[0]

    def body(h_ref, wg_ref, wu_ref, gu_ref, a_ref):
        hv = h_ref[...]
        g = lax.dot_general(hv, wg_ref[...], NT, preferred_element_type=f32)
        u = lax.dot_general(hv, wu_ref[...], NT, preferred_element_type=f32)
        gu_ref[0] = g
        gu_ref[1] = u
        a_ref[...] = (g * _sigmoid(g) * u).astype(bf16)

    return pl.pallas_call(
        body, name=name, grid=(4,),
        in_specs=[_bs((T, D), lambda j: (0, 0)), _bs((None, None, GU, D), lambda j: (l, j, 0, 0)),
                  _bs((None, None, GU, D), lambda j: (l, j + 4, 0, 0))],
        out_specs=[_bs((2, None, T, GU), lambda j: (0, j, 0, 0)), _bs((None, T, GU), lambda j: (j, 0, 0))],
        out_shape=[S((2, 4, T, GU), f32), S((4, T, GU), bf16)], compiler_params=_cp("parallel"))(h, wgu, wgu)


def _swiglu_bwd(name, dxo, wd, gu, l):
    T = dxo.shape[0]
    tm = _tile(T, 1024)

    def body(dx_ref, wd_ref, g_ref, u_ref, o_ref):
        da = 0.5 * lax.dot_general(dx_ref[...].astype(bf16), wd_ref[...], NT, preferred_element_type=f32)
        g = g_ref[...]
        sg = _sigmoid(g)
        o_ref[0] = (da * u_ref[...] * (sg + g * sg * (1.0 - sg))).astype(bf16)
        o_ref[1] = (da * g * sg).astype(bf16)

    return pl.pallas_call(
        body, name=name, grid=(T // tm, 4),
        in_specs=[_bs((tm, D), lambda i, j: (i, 0)), _bs((None, None, GU, D), lambda i, j: (l, j, 0, 0)),
                  _bs((None, None, tm, GU), lambda i, j: (0, j, i, 0)), _bs((None, None, tm, GU), lambda i, j: (1, j, i, 0))],
        out_specs=_bs((2, None, tm, GU), lambda i, j: (0, j, i, 0)), out_shape=S((2, 4, T, GU), bf16),
        compiler_params=_cp("parallel", "parallel"))(dxo, wd, gu, gu)


def _loss(name, y, tgt):
    T = y.shape[0]

    def body(y_ref, t_ref, l_ref, dy_ref):
        i = pl.program_id(0)
        e = y_ref[...] - t_ref[...]
        dy_ref[...] = e * (1.0 / D)
        s = 0.5 * jnp.sum(jnp.mean(e * e, axis=-1, keepdims=True))
        _acc_store(i, l_ref, jnp.broadcast_to(s, (1, 128)))

    return pl.pallas_call(body, name=name, grid=(T // TT,), in_specs=[_row(), _row()],
                          out_specs=[_vec(1, 128), _row()], out_shape=[S((1, 128), f32), S((T, D), f32)],
                          compiler_params=_cp("arbitrary"))(y, tgt)


def _prev8(T, cb):
    return _bs((8, D), lambda i: (jnp.maximum(i * (TT // 8) - 1, 0), cb))


def _next8(T, cb):
    return _bs((8, D), lambda i: (jnp.minimum((i + 1) * (TT // 8), T // 8 - 1), cb))


def _conv_taps(i, ac_ref, ax_ref, pc_ref, px_ref):
    ca = ac_ref[...] * ax_ref[...]
    keep = (i > 0).astype(f32)
    p1 = pc_ref[7:8, :] * px_ref[7:8, :] * keep
    p2 = pc_ref[6:7, :] * px_ref[6:7, :] * keep
    row = lax.broadcasted_iota(jnp.int32, ca.shape, 0)
    s1 = jnp.where(row == 0, p1, pltpu.roll(ca, 1, 0))
    s2 = jnp.where(row == 0, p2, jnp.where(row == 1, p1, pltpu.roll(ca, 2, 0)))
    return ca, s1, s2


def _conv_fwd(name, z, cw):
    T = z.shape[0]

    def body(ab_ref, ac_ref, ax_ref, pc_ref, px_ref, w_ref, o_ref):
        i = pl.program_id(0)
        ca, s1, s2 = _conv_taps(i, ac_ref, ax_ref, pc_ref, px_ref)
        cv = w_ref[0:1, :] * s2 + w_ref[1:2, :] * s1 + w_ref[2:3, :] * ca
        o_ref[...] = (ab_ref[...] * cv).astype(bf16)

    return pl.pallas_call(
        body, name=name, grid=(T // TT,),
        in_specs=[_row(0), _row(1), _row(2), _prev8(T, 1), _prev8(T, 2), _vec(3)],
        out_specs=_row(), out_shape=S((T, D), bf16), compiler_params=_cp("parallel"))(z, z, z, z, z, cw)


def _conv_bwd(name, dya, z, cw):
    T = z.shape[0]
    n = T // TT

    def body(dya_ref, ab_ref, ac_ref, ax_ref, pc_ref, px_ref, ndya_ref, nab_ref, w_ref, dz_ref, dw_ref):
        i = pl.program_id(0)
        ca, s1, s2 = _conv_taps(i, ac_ref, ax_ref, pc_ref, px_ref)
        w0, w1, w2 = w_ref[0:1, :], w_ref[1:2, :], w_ref[2:3, :]
        cv = w0 * s2 + w1 * s1 + w2 * ca
        dya_v = dya_ref[...]
        ab = ab_ref[...]
        dcv = dya_v * ab
        keep = (i < n - 1).astype(f32)
        n1 = ndya_ref[0:1, :] * nab_ref[0:1, :] * keep
        n2 = ndya_ref[1:2, :] * nab_ref[1:2, :] * keep
        row = lax.broadcasted_iota(jnp.int32, dcv.shape, 0)
        f1 = jnp.where(row == TT - 1, n1, pltpu.roll(dcv, TT - 1, 0))
        f2 = jnp.where(row == TT - 1, n2, jnp.where(row == TT - 2, n1, pltpu.roll(dcv, TT - 2, 0)))
        dca = w2 * dcv + w1 * f1 + w0 * f2
        dz_ref[:, 0:D] = (dya_v * cv).astype(bf16)
        dz_ref[:, D:2 * D] = (dca * ax_ref[...]).astype(bf16)
        dz_ref[:, 2 * D:3 * D] = (dca * ac_ref[...]).astype(bf16)
        dw = jnp.concatenate([jnp.sum(dcv * s2, axis=0, keepdims=True), jnp.sum(dcv * s1, axis=0, keepdims=True),
                              jnp.sum(dcv * ca, axis=0, keepdims=True)], axis=0)
        _acc_store(i, dw_ref, dw)

    return pl.pallas_call(
        body, name=name, grid=(n,),
        in_specs=[_row(), _row(0), _row(1), _row(2), _prev8(T, 1), _prev8(T, 2), _next8(T, 0), _next8(T, 0), _vec(3)],
        out_specs=[_row(0, 3 * D), _vec(3)], out_shape=[S((T, 3 * D), bf16), S((3, D), f32)],
        compiler_params=_cp("arbitrary"))(dya, z, z, z, z, z, dya, z, cw)


def _gelu(x):
    return 0.5 * x * (1.0 + lax.erf(x * INV_SQRT2))


def _gelu_cdf(x):
    return 0.5 * (1.0 + lax.erf(x * INV_SQRT2))


def _gelu_grad(x, cdf):
    return cdf + x * jnp.exp(-0.5 * x * x) * INV_SQRT2PI


def _ln_stats(vv):
    mu = jnp.mean(vv, axis=-1, keepdims=True)
    xc = vv - mu
    rstd = lax.rsqrt(jnp.mean(xc * xc, axis=-1, keepdims=True) + LN_EPS)
    return xc * rstd, rstd


def _tril_w(w_ref, g):
    r = lax.broadcasted_iota(jnp.int32, (HD, HD), 0)
    c = lax.broadcasted_iota(jnp.int32, (HD, HD), 1)
    return jnp.where(c <= r, w_ref[g], 0.0).astype(bf16)


def _sgu_fwd(name, z, ln_g, ln_b, w_s, bmap):
    T = z.shape[0]

    def body(su_ref, sv_ref, lg_ref, lb_ref, w_ref, bm_ref, o_ref, vn_ref):
        xhat, _ = _ln_stats(_gelu(sv_ref[...]))
        vn_ref[...] = (xhat * lg_ref[...] + lb_ref[...]).astype(bf16)
        for g in range(NH):
            w = _tril_w(w_ref, g)
            cs = slice(g * HD, (g + 1) * HD)
            for c in range(TT // HD):
                rs = slice(c * HD, (c + 1) * HD)
                s = jnp.dot(w, vn_ref[rs, cs], preferred_element_type=f32) + bm_ref[:, cs]
                o_ref[rs, cs] = (_gelu(su_ref[rs, cs]) * s).astype(bf16)

    return pl.pallas_call(
        body, name=name, grid=(T // TT,),
        in_specs=[_row(3), _row(4), _vec(), _vec(), _bs((NH, HD, HD), lambda i: (0, 0, 0)), _vec(HD)],
        out_specs=_row(), out_shape=S((T, D), bf16), scratch_shapes=[pltpu.VMEM((TT, D), bf16)],
        compiler_params=_cp("parallel"))(z, z, ln_g, ln_b, w_s, bmap)


def _sgu_bwd(name, dyb, z, ln_g, ln_b, w_s, bmap):
    T = z.shape[0]

    def body(dyb_ref, su_ref, sv_ref, lg_ref, lb_ref, w_ref, bm_ref, dz_ref, dlg_ref, dlb_ref, dw_ref, db_ref,
             vn_ref, du_ref, dvn_ref, cu_ref, cv_ref):
        i = pl.program_id(0)
        sv = sv_ref[...]
        cv_ref[...] = _gelu_cdf(sv)
        cu_ref[...] = _gelu_cdf(su_ref[...])
        xhat, rstd = _ln_stats(sv * cv_ref[...])
        vn_ref[...] = (xhat * lg_ref[...] + lb_ref[...]).astype(bf16)
        r = lax.broadcasted_iota(jnp.int32, (HD, HD), 0)
        cc = lax.broadcasted_iota(jnp.int32, (HD, HD), 1)
        for g in range(NH):
            w = _tril_w(w_ref, g)
            cs = slice(g * HD, (g + 1) * HD)
            dw = jnp.zeros((HD, HD), f32)
            db = jnp.zeros((HD, 1), f32)
            for c in range(TT // HD):
                rs = slice(c * HD, (c + 1) * HD)
                vnb = vn_ref[rs, cs]
                s = jnp.dot(w, vnb, preferred_element_type=f32) + bm_ref[:, cs]
                dy = dyb_ref[rs, cs]
                du_ref[rs, cs] = dy * s
                ds = dy * (su_ref[rs, cs] * cu_ref[rs, cs])
                ds16 = ds.astype(bf16)
                dvn_ref[rs, cs] = lax.dot_general(w, ds16, TN, preferred_element_type=f32)
                dw = dw + lax.dot_general(ds16, vnb, NT, preferred_element_type=f32)
                db = db + jnp.sum(ds, axis=1, keepdims=True)
            dw = jnp.where(cc <= r, dw, 0.0)

            @pl.when(i == 0)
            def _():
                dw_ref[g] = dw
                db_ref[:, g:g + 1] = db

            @pl.when(i > 0)
            def _():
                dw_ref[g] += dw
                db_ref[:, g:g + 1] += db

        dvn = dvn_ref[...]
        dxh = dvn * lg_ref[...]
        m1 = jnp.mean(dxh, axis=-1, keepdims=True)
        m2 = jnp.mean(dxh * xhat, axis=-1, keepdims=True)
        dvv = rstd * (dxh - m1 - xhat * m2)
        dz_ref[:, 0:D] = (du_ref[...] * _gelu_grad(su_ref[...], cu_ref[...])).astype(bf16)
        dz_ref[:, D:2 * D] = (dvv * _gelu_grad(sv, cv_ref[...])).astype(bf16)
        _acc_store(i, dlg_ref, jnp.sum(dvn * xhat, axis=0, keepdims=True))
        _acc_store(i, dlb_ref, jnp.sum(dvn, axis=0, keepdims=True))

    return pl.pallas_call(
        body, name=name, grid=(T // TT,),
        in_specs=[_row(), _row(3), _row(4), _vec(), _vec(), _bs((NH, HD, HD), lambda i: (0, 0, 0)), _vec(HD)],
        out_specs=[_row(0, 2 * D), _vec(), _vec(), _bs((NH, HD, HD), lambda i: (0, 0, 0)), _bs((HD, NH), lambda i: (0, 0))],
        out_shape=[S((T, 2 * D), bf16), S((1, D), f32), S((1, D), f32), S((NH, HD, HD), f32), S((HD, NH), f32)],
        scratch_shapes=[pltpu.VMEM((TT, D), bf16)] + [pltpu.VMEM((TT, D), f32)] * 4,
        compiler_params=_cp("arbitrary"))(dyb, z, z, ln_g, ln_b, w_s, bmap)


def _qk_fwd(name, z, qg, kg, bf):
    T = z.shape[0]

    def body(q_ref, k_ref, v_ref, zf_ref, qg_ref, kg_ref, bf_ref, qn_ref, kn_ref, vb_ref, lf_ref):
        for h in range(NH):
            cs = slice(h * HD, (h + 1) * HD)
            for src, gain, dst in ((q_ref, qg_ref, qn_ref), (k_ref, kg_ref, kn_ref)):
                xv = src[:, cs]
                r = lax.rsqrt(jnp.mean(xv * xv, axis=-1, keepdims=True) + RMS_EPS)
                dst[:, cs] = (xv * r * gain[:, cs]).astype(bf16)
        vb_ref[...] = v_ref[...].astype(bf16)
        xf = zf_ref[...] + bf_ref[...]
        lf_ref[...] = jnp.minimum(xf, 0.0) - jnp.log1p(jnp.exp(-jnp.abs(xf)))

    return pl.pallas_call(
        body, name=name, grid=(T // TT,),
        in_specs=[_row(5), _row(6), _row(7), _bs((TT, 128), lambda i: (i, F_OFF // 128)), _vec(), _vec(), _vec(1, 128)],
        out_specs=[_row(), _row(), _row(), _bs((TT, 128), lambda i: (i, 0))],
        out_shape=[S((T, D), bf16), S((T, D), bf16), S((T, D), bf16), S((T, 128), f32)],
        compiler_params=_cp("parallel"))(z, z, z, z, qg, kg, bf)


def _cum_fwd(name, logf):
    T = logf.shape[0]

    def body(lf_ref, ccol_ref, crow_ref, c_ref):
        c = lf_ref[...]
        row = lax.broadcasted_iota(jnp.int32, c.shape, 0)
        s = 1
        while s < T:
            c = c + jnp.where(row >= s, pltpu.roll(c, s, 0), 0.0)
            s *= 2
        c_ref[...] = c
        crow_ref[...] = c.T[0:NH, :]
        for h in range(NH):
            ccol_ref[h] = jnp.broadcast_to(c_ref[:, h:h + 1], (T, 128))

    return pl.pallas_call(body, name=name, out_shape=[S((NH, T, 128), f32), S((NH, T), f32)],
                          scratch_shapes=[pltpu.VMEM((T, 128), f32)],
                          compiler_params=pltpu.CompilerParams(vmem_limit_bytes=VMEM_LIMIT))(logf)


ATT_TILE = 1024


def _fold(x, op=jnp.add):
    acc = x[:, 0:128]
    for t in range(1, x.shape[1] // 128):
        acc = op(acc, x[:, t * 128:(t + 1) * 128])
    return acc


def _to_row(col):
    return jnp.broadcast_to(col, (col.shape[0], 128)).T[0:1, :]


def _causal(t, keys_down=False):
    r = lax.broadcasted_iota(jnp.int32, (t, t), 0)
    c = lax.broadcasted_iota(jnp.int32, (t, t), 1)
    return r <= c if keys_down else c <= r


def _attn_fwd(name, qn, kn, vb, ccol, crow3):
    T = qn.shape[0]
    tq = _tile(T, ATT_TILE)
    nq = T // tq

    def body(q_ref, k_ref, v_ref, cc_ref, cr_ref, o_ref, lse_ref, lser_ref, s_ref):
        qi = pl.program_id(1)
        q = q_ref[...]
        cq = cc_ref[:, 0:1]

        def logits(off):
            s = lax.dot_general(q, k_ref[pl.ds(off, tq), :], NT, preferred_element_type=f32) * ATT_SCALE
            return s + cq - cr_ref[:, pl.ds(off, tq)]

        def below(j, mvec):
            off = pl.multiple_of(j * tq, tq)
            s = logits(off)
            s_ref[:, pl.ds(off, tq)] = s
            return jnp.maximum(mvec, _fold(s, jnp.maximum))

        mvec = lax.fori_loop(0, qi, below, jnp.full((tq, 128), NEG, f32))
        off = pl.multiple_of(qi * tq, tq)
        s = jnp.where(_causal(tq), logits(off), NEG)
        s_ref[:, pl.ds(off, tq)] = s
        m = jnp.max(jnp.maximum(mvec, _fold(s, jnp.maximum)), axis=1, keepdims=True)

        def weigh(j, carry):
            lvec, acc = carry
            off = pl.multiple_of(j * tq, tq)
            p = jnp.exp(s_ref[:, pl.ds(off, tq)] - m)
            acc = acc + jnp.dot(p.astype(bf16), v_ref[pl.ds(off, tq), :], preferred_element_type=f32)
            return lvec + _fold(p), acc

        lvec, acc = lax.fori_loop(0, qi + 1, weigh, (jnp.zeros((tq, 128), f32), jnp.zeros((tq, HD), f32)))
        l = jnp.sum(lvec, axis=1, keepdims=True)
        o_ref[...] = acc / l
        lse = m + jnp.log(l)
        lse_ref[...] = jnp.broadcast_to(lse, (tq, 128))
        lser_ref[...] = _to_row(lse)

    return pl.pallas_call(
        body, name=name, grid=(NH, nq),
        in_specs=[_bs((tq, HD), lambda h, i: (i, h)), _bs((T, HD), lambda h, i: (0, h)), _bs((T, HD), lambda h, i: (0, h)),
                  _bs((None, tq, 128), lambda h, i: (h, i, 0)), _bs((None, 1, T), lambda h, i: (h, 0, 0))],
        out_specs=[_bs((tq, HD), lambda h, i: (i, h)), _bs((None, tq, 128), lambda h, i: (h, i, 0)),
                   _bs((None, 1, tq), lambda h, i: (h, 0, i))],
        out_shape=[S((T, D), f32), S((NH, T, 128), f32), S((NH, 1, T), f32)],
        scratch_shapes=[pltpu.VMEM((tq, T), f32)],
        compiler_params=_cp("parallel", "parallel"))(qn, kn, vb, ccol, crow3)


def _attn_dq(name, qn, kn, vb, do, lse, ccol, crow3):
    T = qn.shape[0]
    tq = _tile(T, ATT_TILE)
    nq = T // tq

    def body(q_ref, k_ref, v_ref, do_ref, lse_ref, cc_ref, cr_ref, dq_ref, dlr_ref, p_ref, dp_ref):
        qi = pl.program_id(1)
        q = q_ref[...]
        do16 = do_ref[...].astype(bf16)
        base = cc_ref[:, 0:1] - lse_ref[:, 0:1]

        def probs(off):
            s = lax.dot_general(q, k_ref[pl.ds(off, tq), :], NT, preferred_element_type=f32) * ATT_SCALE
            return jnp.exp(s + base - cr_ref[:, pl.ds(off, tq)])

        def keep(off, p, dvec):
            dp = lax.dot_general(do16, v_ref[pl.ds(off, tq), :], NT, preferred_element_type=f32)
            p_ref[:, pl.ds(off, tq)] = p
            dp_ref[:, pl.ds(off, tq)] = dp
            return dvec + _fold(p * dp)

        def below(j, dvec):
            off = pl.multiple_of(j * tq, tq)
            return keep(off, probs(off), dvec)

        dvec = lax.fori_loop(0, qi, below, jnp.zeros((tq, 128), f32))
        off = pl.multiple_of(qi * tq, tq)
        dvec = keep(off, jnp.where(_causal(tq), probs(off), 0.0), dvec)
        delta = jnp.sum(dvec, axis=1, keepdims=True)

        def grad(j, acc):
            off = pl.multiple_of(j * tq, tq)
            ds = p_ref[:, pl.ds(off, tq)] * (dp_ref[:, pl.ds(off, tq)] - delta)
            return acc + jnp.dot(ds.astype(bf16), k_ref[pl.ds(off, tq), :], preferred_element_type=f32)

        dq_ref[...] = lax.fori_loop(0, qi + 1, grad, jnp.zeros((tq, HD), f32)) * ATT_SCALE
        dlr_ref[...] = _to_row(delta)

    qb = lambda h, i: (i, h)
    full = lambda h, i: (0, h)
    col = lambda h, i: (h, i, 0)
    return pl.pallas_call(
        body, name=name, grid=(NH, nq),
        in_specs=[_bs((tq, HD), qb), _bs((T, HD), full), _bs((T, HD), full), _bs((tq, HD), qb),
                  _bs((None, tq, 128), col), _bs((None, tq, 128), col), _bs((None, 1, T), lambda h, i: (h, 0, 0))],
        out_specs=[_bs((tq, HD), qb), _bs((None, 1, tq), lambda h, i: (h, 0, i))],
        out_shape=[S((T, D), f32), S((NH, 1, T), f32)],
        scratch_shapes=[pltpu.VMEM((tq, T), f32), pltpu.VMEM((tq, T), f32)],
        compiler_params=_cp("parallel", "parallel"))(qn, kn, vb, do, lse, ccol, crow3)


def _attn_dkv(name, qn, kn, vb, do, lser3, dlr3, ccol, crow3):
    T = qn.shape[0]
    tk = _tile(T, ATT_TILE)
    nk = T // tk

    def body(q_ref, k_ref, v_ref, do_ref, lser_ref, dlr_ref, cc_ref, cr_ref, dk_ref, dv_ref, cs_ref):
        h = pl.program_id(0)
        kj = pl.program_id(1)

        @pl.when((h == 0) & (kj == 0))
        def _():
            cs_ref[...] = jnp.zeros_like(cs_ref)

        kb = k_ref[...]
        vv = v_ref[...]
        ckey = cc_ref[:, 0:1]

        def block(off, diagonal):
            rows = pl.ds(off, tk)
            qb = q_ref[rows, :]
            do16 = do_ref[rows, :].astype(bf16)
            st = lax.dot_general(kb, qb, NT, preferred_element_type=f32) * ATT_SCALE
            pt = jnp.exp(st + (cr_ref[:, rows] - lser_ref[:, rows]) - ckey)
            if diagonal:
                pt = jnp.where(_causal(tk, keys_down=True), pt, 0.0)
            dpt = lax.dot_general(vv, do16, NT, preferred_element_type=f32)
            dst = pt * (dpt - dlr_ref[:, rows])
            ddv = jnp.dot(pt.astype(bf16), do16, preferred_element_type=f32)
            ddk = jnp.dot(dst.astype(bf16), qb, preferred_element_type=f32)
            return ddk, ddv, _fold(dst)

        def above(i, carry):
            ddk, ddv, dcs = block(pl.multiple_of(i * tk, tk), False)
            return carry[0] + ddk, carry[1] + ddv, carry[2] + dcs

        off = pl.multiple_of(kj * tk, tk)
        dk, dv, cs = lax.fori_loop(kj + 1, nk, above, block(off, True))
        dk_ref[...] = dk * ATT_SCALE
        dv_ref[...] = dv
        lane = lax.broadcasted_iota(jnp.int32, (tk, 128), 1)
        cs_ref[pl.ds(off, tk), :] += jnp.where(lane == h, jnp.sum(cs, axis=1, keepdims=True), 0.0)

    full = lambda h, j: (0, h)
    blk = lambda h, j: (j, h)
    row = lambda h, j: (h, 0, 0)
    return pl.pallas_call(
        body, name=name, grid=(NH, nk),
        in_specs=[_bs((T, HD), full), _bs((tk, HD), blk), _bs((tk, HD), blk), _bs((T, HD), full), _bs((None, 1, T), row),
                  _bs((None, 1, T), row), _bs((None, tk, 128), lambda h, j: (h, j, 0)), _bs((None, 1, T), row)],
        out_specs=[_bs((tk, HD), blk), _bs((tk, HD), blk), _bs((T, 128), lambda h, j: (0, 0))],
        out_shape=[S((T, D), f32), S((T, D), f32), S((T, 128), f32)],
        compiler_params=_cp("arbitrary", "arbitrary"))(qn, kn, vb, do, lser3, dlr3, ccol, crow3)


def _forget_bwd(name, cs, z, bf):
    T = cs.shape[0]

    def body(cs_ref, zf_ref, bf_ref, dz_ref, db_ref):
        c = -cs_ref[...]
        row = lax.broadcasted_iota(jnp.int32, c.shape, 0)
        s = 1
        while s < T:
            c = c + jnp.where(row + s < T, pltpu.roll(c, T - s, 0), 0.0)
            s *= 2
        xf = zf_ref[...] + bf_ref[...]
        lane = lax.broadcasted_iota(jnp.int32, c.shape, 1)
        dxf = jnp.where(lane < NH, c / (1.0 + jnp.exp(xf)), 0.0)
        dz_ref[...] = jnp.zeros_like(dz_ref)
        dz_ref[:, 0:128] = dxf.astype(bf16)
        db_ref[...] = jnp.sum(dxf, axis=0, keepdims=True)

    return pl.pallas_call(
        body, name=name, grid=(1,),
        in_specs=[_bs((T, 128), lambda i: (0, 0)), _bs((T, 128), lambda i: (0, F_OFF // 128)), _vec(1, 128)],
        out_specs=[_bs((T, NZ - F_OFF), lambda i: (0, 0)), _vec(1, 128)],
        out_shape=[S((T, NZ - F_OFF), bf16), S((1, 128), f32)], compiler_params=_cp("arbitrary"))(cs, z, bf)


def _qk_bwd(name, dqn, dkn, dv, z, qg, kg):
    T = z.shape[0]

    def body(dq_ref, dk_ref, dv_ref, q_ref, k_ref, qg_ref, kg_ref, dz_ref, dqg_ref, dkg_ref, g_ref):
        i = pl.program_id(0)
        for n, (src, dsrc, gain, dgain) in enumerate(((q_ref, dq_ref, qg_ref, dqg_ref), (k_ref, dk_ref, kg_ref, dkg_ref))):
            for h in range(NH):
                cs = slice(h * HD, (h + 1) * HD)
                xv = src[:, cs]
                r = lax.rsqrt(jnp.mean(xv * xv, axis=-1, keepdims=True) + RMS_EPS)
                xhat = xv * r
                dy = dsrc[:, cs]
                dyg = dy * gain[:, cs]
                m = jnp.mean(dyg * xhat, axis=-1, keepdims=True)
                dz_ref[:, n * D + h * HD:n * D + (h + 1) * HD] = (r * (dyg - xhat * m)).astype(bf16)
                g_ref[:, cs] = jnp.sum(dy * xhat, axis=0, keepdims=True)
            _acc_store(i, dgain, g_ref[...])
        dz_ref[:, 2 * D:3 * D] = dv_ref[...].astype(bf16)

    return pl.pallas_call(
        body, name=name, grid=(T // TT,),
        in_specs=[_row(), _row(), _row(), _row(5), _row(6), _vec(), _vec()],
        out_specs=[_row(0, 3 * D), _vec(), _vec()], out_shape=[S((T, 3 * D), bf16), S((1, D), f32), S((1, D), f32)],
        scratch_shapes=[pltpu.VMEM((1, D), f32)], compiler_params=_cp("arbitrary"))(dqn, dkn, dv, z, z, qg, kg)


GB = GATE_OFF // D


def _merge_fwd(name, ya, yb, yc, z, bg):
    T = z.shape[0]

    def body(ya_ref, yb_ref, yc_ref, g0_ref, g1_ref, g2_ref, bg_ref, o_ref):
        acc = _sigmoid(g0_ref[...] + bg_ref[0:1, :]) * ya_ref[...]
        acc = acc + _sigmoid(g1_ref[...] + bg_ref[1:2, :]) * yb_ref[...]
        acc = acc + _sigmoid(g2_ref[...] + bg_ref[2:3, :]) * yc_ref[...]
        o_ref[...] = acc.astype(bf16)

    return pl.pallas_call(
        body, name=name, grid=(T // TT,),
        in_specs=[_row(), _row(), _row(), _row(GB), _row(GB + 1), _row(GB + 2), _vec(3)],
        out_specs=_row(), out_shape=S((T, D), bf16), compiler_params=_cp("parallel"))(ya, yb, yc, z, z, z, bg)


def _merge_bwd(name, dm, ya, yb, yc, z, bg):
    T = z.shape[0]

    def body(dm_ref, ya_ref, yb_ref, yc_ref, g0_ref, g1_ref, g2_ref, bg_ref, dya_ref, dyb_ref, dyc_ref, dz_ref, db_ref):
        i = pl.program_id(0)
        dm_v = dm_ref[...]
        dbs = []
        for n, (y_ref, g_ref, dy_ref) in enumerate(((ya_ref, g0_ref, dya_ref), (yb_ref, g1_ref, dyb_ref),
                                                    (yc_ref, g2_ref, dyc_ref))):
            gate = _sigmoid(g_ref[...] + bg_ref[n:n + 1, :])
            dy_ref[...] = (dm_v * gate).astype(bf16)
            dl = dm_v * y_ref[...] * gate * (1.0 - gate)
            dz_ref[:, n * D:(n + 1) * D] = dl.astype(bf16)
            dbs.append(jnp.sum(dl, axis=0, keepdims=True))
        _acc_store(i, db_ref, jnp.concatenate(dbs, axis=0))

    return pl.pallas_call(
        body, name=name, grid=(T // TT,),
        in_specs=[_row(), _row(), _row(), _row(), _row(GB), _row(GB + 1), _row(GB + 2), _vec(3)],
        out_specs=[_row(), _row(), _row(), _row(0, 3 * D), _vec(3)],
        out_shape=[S((T, D), bf16), S((T, D), bf16), S((T, D), bf16), S((T, 3 * D), bf16), S((3, D), f32)],
        compiler_params=_cp("arbitrary"))(dm, ya, yb, yc, z, z, z, bg)


SMALL_NAMES = ("ffn1_norm", "mix_norm", "b_forget", "b_gate", "conv_w", "sgu_ln_g", "sgu_ln_b", "sgu_w", "sgu_b",
               "q_norm_g", "k_norm_g", "ffn2_norm")


def _small_params(p):
    out = {n: p[n].reshape(1, D) for n in ("ffn1_norm", "mix_norm", "ffn2_norm", "sgu_ln_g", "sgu_ln_b", "q_norm_g", "k_norm_g")}
    out["b_forget"] = jnp.pad(p["b_forget"].reshape(1, NH), ((0, 0), (0, 128 - NH)))
    out["b_gate"] = p["b_gate"]
    out["conv_w"] = p["conv_w"]
    out["sgu_w"] = p["sgu_w"]
    out["bmap"] = jnp.repeat(p["sgu_b"].T, HD, axis=1)
    return out


def _small_grads_natural(sg):
    out = {n: sg[n].reshape(D) for n in ("ffn1_norm", "mix_norm", "ffn2_norm", "sgu_ln_g", "sgu_ln_b")}
    out["q_norm_g"] = sg["q_norm_g"].reshape(NH, HD)
    out["k_norm_g"] = sg["k_norm_g"].reshape(NH, HD)
    out["b_forget"] = sg["b_forget"][0, :NH]
    out["b_gate"] = sg["b_gate"]
    out["conv_w"] = sg["conv_w"]
    out["sgu_w"] = sg["sgu_w"]
    out["sgu_b"] = sg["sgu_b"]
    return out


SQ_TM = 1024


def _sq_fwd(name, a, wsq, l, n, res=None):
    T = a.shape[0]
    tm = _tile(T, SQ_TM)
    return _mm(name, a, wsq, grid=(T // tm, 1, 1), a_spec=_bs((tm, D), lambda i, j, k: (i, 0)),
               b_spec=_bs((None, None, D, D), lambda i, j, k: (l, n, 0, 0)),
               out_shape=S((T, D), f32), out_spec=_bs((tm, D), lambda i, j, k: (i, 0)), dims=NN, acc_shape=None,
               res=res, res_spec=_bs((tm, D), lambda i, j, k: (i, 0)))


def _sq_bwd_in(name, dy, wsq, l, n):
    T = dy.shape[0]
    tm = _tile(T, SQ_TM)
    return _mm(name, dy, wsq, grid=(T // tm, 1, 1), a_spec=_bs((tm, D), lambda i, j, k: (i, 0)),
               b_spec=_bs((None, None, D, D), lambda i, j, k: (l, n, 0, 0)),
               out_shape=S((T, D), f32), out_spec=_bs((tm, D), lambda i, j, k: (i, 0)), dims=NT, acc_shape=None)


def _sq_bwd_w(name, a, dy, gbuf, l):
    T = a.shape[0]
    return _mm(name, a, dy, grid=(NDEV // 2, 1, 1), a_spec=_bs((T, 256), lambda i, j, k: (0, i)),
               b_spec=_bs((T, D), lambda i, j, k: (0, 0)), out_shape=S(gbuf.shape, bf16),
               out_spec=_bs((2, None, None, 128, D), lambda i, j, k: (0, i, l, 0, 0)),
               dims=TN, acc_shape=None, alias=gbuf, split_rows=128)


def _ffn_fwd(tag, x, g, wgu, wd, l):
    T = x.shape[0]
    h = _rms_fwd(tag + "_rms", x, g)
    gu, a = _swiglu_fwd(tag + "_gu", h, wgu, l)
    tm = _tile(T, 1024)
    xo = _mm(tag + "_down", a, wd, grid=(T // tm, 1, 4), a_spec=_bs((None, tm, GU), lambda i, j, k: (k, i, 0)),
             b_spec=_bs((None, None, GU, D), lambda i, j, k: (l, k, 0, 0)), out_shape=S((T, D), f32),
             out_spec=_bs((tm, D), lambda i, j, k: (i, 0)), dims=NN, acc_shape=(tm, D), res=x,
             res_spec=_bs((tm, D), lambda i, j, k: (i, 0)), alpha=0.5)
    return xo, (h, gu, a)


def _ffn_bwd(tag, dxo, x, g, wgu, wd, l, saved, g_gu, g_d, ship, hook, hook_at_once=False):
    h, gu, a = saved
    T = x.shape[0]
    g_d = _mm(tag + "_dwd", a, dxo, grid=(4, 1, 1), a_spec=_bs((None, T, GU), lambda i, j, k: (i, 0, 0)),
              b_spec=_bs((T, D), lambda i, j, k: (0, 0)), out_shape=S(g_d.shape, bf16),
              out_spec=_bs((2, None, None, GU // 2, D), lambda i, j, k: (0, i, l, 0, 0)), dims=TN, acc_shape=None,
              alpha=0.5, alias=g_d, split_rows=GU // 2)
    dgu = _swiglu_bwd(tag + "_dgu", dxo, wd, gu, l).reshape(NDEV, T, GU)
    g_gu = _mm(tag + "_dwgu", dgu, h, grid=(NDEV, 1, 1), a_spec=_bs((None, T, GU), lambda i, j, k: (i, 0, 0)),
               b_spec=_bs((T, D), lambda i, j, k: (0, 0)), out_shape=S(g_gu.shape, bf16),
               out_spec=_bs((None, None, None, GU, D), lambda i, j, k: (i % 2, i // 2, l, 0, 0)), dims=TN,
               acc_shape=None, alias=g_gu)
    dxo = ship([g_gu, g_d], dxo)
    if hook_at_once:
        dxo = hook(dxo, g_gu)
    dh = _mm(tag + "_dh", dgu, wgu, grid=(1, 1, NDEV), a_spec=_bs((None, T, GU), lambda i, j, k: (k, 0, 0)),
             b_spec=_bs((None, None, GU, D), lambda i, j, k: (l, k, 0, 0)), out_shape=S((T, D), f32),
             out_spec=_bs((T, D), lambda i, j, k: (0, 0)), dims=NN, acc_shape=(T, D))
    if not hook_at_once:
        dxo = hook(dxo, dh)
    return _rms_bwd(tag + "_drms", dh, x, g, dxo)


def _mixer_fwd(tag, x, p, win, wsq, l):
    T = x.shape[0]
    h = _rms_fwd(tag + "_rms", x, p["mix_norm"])
    tn = 512
    z = _mm(tag + "_in", h, win, grid=(1, NZ // tn, 1), a_spec=_bs((T, D), lambda i, j, k: (0, 0)),
            b_spec=_bs((D, tn), lambda i, j, k: (0, j)), out_shape=S((T, NZ), f32),
            out_spec=_bs((T, tn), lambda i, j, k: (0, j)), dims=NN, acc_shape=None)
    ya_in = _conv_fwd(tag + "_conv", z, p["conv_w"])
    yb_in = _sgu_fwd(tag + "_sgu", z, p["sgu_ln_g"], p["sgu_ln_b"], p["sgu_w"], p["bmap"])
    qn, kn, vb, logf = _qk_fwd(tag + "_qk", z, p["q_norm_g"], p["k_norm_g"], p["b_forget"])
    ccol, crow = _cum_fwd(tag + "_cum", logf)
    crow3 = crow.reshape(NH, 1, T)
    o, lse, lser = _attn_fwd(tag + "_attn", qn, kn, vb, ccol, crow3)
    ya = _sq_fwd(tag + "_oconv", ya_in, wsq, l, 0)
    yb = _sq_fwd(tag + "_osgu", yb_in, wsq, l, 1)
    yc = _sq_fwd(tag + "_oattn", o, wsq, l, 2)
    merged = _merge_fwd(tag + "_merge", ya, yb, yc, z, p["b_gate"])
    xo = _sq_fwd(tag + "_o", merged, wsq, l, 3, res=x)
    return xo, (h, z, ya_in, yb_in, qn, kn, vb, ccol, crow3, o, lse, lser, ya, yb, yc, merged)


def _mixer_bwd(tag, dxo, x, p, win, wsq, l, saved, gsq, ship_sq, ship_in, hook):
    h, z, ya_in, yb_in, qn, kn, vb, ccol, crow3, o, lse, lser, ya, yb, yc, merged = saved
    T = x.shape[0]
    sg = {}
    dm = _sq_bwd_in(tag + "_dmerged", dxo, wsq, l, 3)
    gsq[3] = _sq_bwd_w(tag + "_dwo", merged, dxo, gsq[3], l)
    dya, dyb, dyc, dz_g, sg["b_gate"] = _merge_bwd(tag + "_dmerge", dm, ya, yb, yc, z, p["b_gate"])
    d_ya_in = _sq_bwd_in(tag + "_dconv_in", dya, wsq, l, 0)
    gsq[0] = _sq_bwd_w(tag + "_dwoc", ya_in, dya, gsq[0], l)
    d_yb_in = _sq_bwd_in(tag + "_dsgu_in", dyb, wsq, l, 1)
    gsq[1] = _sq_bwd_w(tag + "_dwos", yb_in, dyb, gsq[1], l)
    d_o = _sq_bwd_in(tag + "_dattn_in", dyc, wsq, l, 2)
    gsq[2] = _sq_bwd_w(tag + "_dwoa", o, dyc, gsq[2], l)
    dxo = ship_sq(gsq, dxo)
    dz_c, sg["conv_w"] = _conv_bwd(tag + "_dconv", d_ya_in, z, p["conv_w"])
    dxo = hook(dxo, dz_c)
    dz_s, sg["sgu_ln_g"], sg["sgu_ln_b"], sg["sgu_w"], db_t = _sgu_bwd(
        tag + "_dsgu", d_yb_in, z, p["sgu_ln_g"], p["sgu_ln_b"], p["sgu_w"], p["bmap"])
    sg["sgu_b"] = db_t.T
    dqn, dlr = _attn_dq(tag + "_dattn_q", qn, kn, vb, d_o, lse, ccol, crow3)
    dkn, dv, cs = _attn_dkv(tag + "_dattn_kv", qn, kn, vb, d_o, lser, dlr, ccol, crow3)
    dz_f, sg["b_forget"] = _forget_bwd(tag + "_dforget", cs, z, p["b_forget"])
    dz_q, sg["q_norm_g"], sg["k_norm_g"] = _qk_bwd(tag + "_dqk", dqn, dkn, dv, z, p["q_norm_g"], p["k_norm_g"])
    dz = [dz_c, dz_s, dz_q, dz_g, dz_f]
    dwin = _dz_matmul(tag + "_dwin", dz, h, True)
    dxo = ship_in(dwin, dxo)
    dh = _dz_matmul(tag + "_dh", dz, win, False)
    dxo = hook(dxo, dh)
    dx, sg["mix_norm"] = _rms_bwd(tag + "_drms", dh, x, p["mix_norm"], dxo)
    return dx, sg


ANY = pl.BlockSpec(memory_space=pl.ANY)
HBM = pl.BlockSpec(memory_space=pltpu.HBM)
SEM = pl.BlockSpec(memory_space=pltpu.SEMAPHORE)
EFFECT = pltpu.SideEffectType.DATAFLOW_SIDE_EFFECTING


def _place():
    return lax.axis_index("x"), lax.axis_index("y"), lax.axis_index("c")


NEAR = 4


def _others(x, y, c):
    return [(x, y, 1 - c), (1 - x, y, c), (x, 1 - y, c), (1 - x, 1 - y, c)]


def _gather_start(name, groups, carry=None):
    sizes = [len(g) for g in groups]
    srcs = [s for g in groups for s, _ in g]
    lands = [b for g in groups for _, b in g]
    n, ng = len(srcs), len(groups)
    held = srcs + lands + ([] if carry is None else [carry])
    nh = len(held)

    def body(*refs):
        src_refs, land_refs = refs[:n], refs[n:2 * n]
        send, recv = refs[nh:nh + ng], refs[nh + ng:nh + 2 * ng]
        x, y, c = _place()
        me = 4 * x + 2 * y + c
        u = 0
        for g, size in enumerate(sizes):
            for i in range(size):
                for k, peer in enumerate(_others(x, y, c)):
                    pltpu.make_async_remote_copy(src_ref=src_refs[u], dst_ref=land_refs[u].at[me],
                                                 send_sem=send[g].at[i * NEAR + k], recv_sem=recv[g].at[i * NEAR + k],
                                                 device_id=peer, device_id_type=MESH).start()
                u += 1

    sems = [pltpu.SemaphoreType.DMA((size * NEAR,)) for size in sizes]
    out = pl.pallas_call(
        body, name=name, in_specs=[HBM] * nh, out_specs=[SEM] * (2 * ng) + [HBM] * nh,
        out_shape=sems + sems + [pltpu.HBM(a.shape, a.dtype) for a in held],
        input_output_aliases={i: 2 * ng + i for i in range(nh)},
        compiler_params=pltpu.CompilerParams(has_side_effects=EFFECT),
    )(*[pltpu.with_memory_space_constraint(a, pltpu.HBM) for a in held])
    res, pos = [], 0
    for g, size in enumerate(sizes):
        res.append((out[g], out[ng + g], out[2 * ng + pos:2 * ng + pos + size], out[2 * ng + n + pos:2 * ng + n + pos + size]))
        pos += size
    return res if carry is None else (res, out[2 * ng + 2 * n])


def _gather_wait(name, started, after=None):
    send, recv, srcs, lands = started
    n = len(srcs)

    def body(*refs):
        src_refs, land_refs = refs[:n], refs[n:2 * n]
        send_ref, recv_ref = refs[2 * n], refs[2 * n + 1]
        x, y, c = _place()
        for i in range(n):
            for k, (px, py, pc) in enumerate(_others(x, y, c)):
                cp = pltpu.make_async_remote_copy(src_ref=src_refs[i], dst_ref=land_refs[i].at[4 * px + 2 * py + pc],
                                                  send_sem=send_ref.at[i * NEAR + k], recv_sem=recv_ref.at[i * NEAR + k],
                                                  device_id=(px, py, pc), device_id_type=MESH)
                cp.wait_send()
                cp.wait_recv()

    extra = [] if after is None else [after]
    out = pl.pallas_call(
        body, name=name, in_specs=[HBM] * (2 * n) + [SEM, SEM] + [ANY] * len(extra), out_specs=[HBM] * (2 * n),
        out_shape=[pltpu.HBM(a.shape, a.dtype) for a in list(srcs) + list(lands)],
        input_output_aliases={i: i for i in range(2 * n)},
        compiler_params=pltpu.CompilerParams(has_side_effects=EFFECT),
    )(*srcs, *lands, send, recv, *extra)
    return out[n:]


def _forward_copy(have, full, send, recv, i, j, core):
    x, y, c = _place()
    chip = [(1 - x, y), (x, 1 - y), (1 - x, 1 - y)][j]
    slot = 4 * chip[0] + 2 * chip[1] + core
    return pltpu.make_async_remote_copy(src_ref=have[i].at[slot], dst_ref=full[i].at[slot], send_sem=send.at[i * 3 + j],
                                        recv_sem=recv.at[i * 3 + j], device_id=(x, y, 1 - c), device_id_type=MESH)


def _gather_forward(name, lands):
    n = len(lands)

    def body(*refs):
        have, full = refs[:n], refs[n:2 * n]
        send, recv = refs[2 * n], refs[2 * n + 1]
        c = lax.axis_index("c")
        for i in range(n):
            for j in range(3):
                _forward_copy(have, full, send, recv, i, j, c).start()
        for i in range(n):
            for j in range(3):
                _forward_copy(have, full, send, recv, i, j, c).wait_send()
                _forward_copy(have, full, send, recv, i, j, 1 - c).wait_recv()

    return pl.pallas_call(
        body, name=name, in_specs=[ANY] * n, out_specs=[ANY] * n, out_shape=[S(a.shape, a.dtype) for a in lands],
        input_output_aliases={i: i for i in range(n)},
        scratch_shapes=[pltpu.SemaphoreType.DMA((n * 3,)), pltpu.SemaphoreType.DMA((n * 3,))],
    )(*lands)


def _forward_start(name, lands, carry):
    n = len(lands)
    held = list(lands) + [carry]

    def body(*refs):
        c = lax.axis_index("c")
        for i in range(n):
            for j in range(3):
                _forward_copy(refs[:n], refs[:n], refs[n + 1], refs[n + 2], i, j, c).start()

    sems = [pltpu.SemaphoreType.DMA((n * 3,))] * 2
    out = pl.pallas_call(
        body, name=name, in_specs=[HBM] * (n + 1), out_specs=[SEM, SEM] + [HBM] * (n + 1),
        out_shape=sems + [pltpu.HBM(a.shape, a.dtype) for a in held],
        input_output_aliases={i: 2 + i for i in range(n + 1)},
        compiler_params=pltpu.CompilerParams(has_side_effects=EFFECT),
    )(*[pltpu.with_memory_space_constraint(a, pltpu.HBM) for a in held])
    return (out[0], out[1], out[2:2 + n]), out[2 + n]


def _forward_wait(name, started, after):
    send, recv, lands = started
    n = len(lands)

    def body(*refs):
        c = lax.axis_index("c")
        for i in range(n):
            for j in range(3):
                _forward_copy(refs[:n], refs[:n], refs[n], refs[n + 1], i, j, c).wait_send()
                _forward_copy(refs[:n], refs[:n], refs[n], refs[n + 1], i, j, 1 - c).wait_recv()

    return pl.pallas_call(
        body, name=name, in_specs=[HBM] * n + [SEM, SEM, ANY], out_specs=[HBM] * n,
        out_shape=[pltpu.HBM(a.shape, a.dtype) for a in lands],
        input_output_aliases={i: i for i in range(n)},
        compiler_params=pltpu.CompilerParams(has_side_effects=EFFECT),
    )(*lands, send, recv, after)


def _pair_copies(ins, outs, send, recv):
    x, y, c = _place()
    return [pltpu.make_async_remote_copy(src_ref=ins[u].at[1 - c], dst_ref=outs[u], send_sem=send.at[u],
                                         recv_sem=recv.at[u], device_id=(x, y, 1 - c), device_id_type=MESH)
            for u in range(len(ins))]


def _pair_start(name, gs, carry):
    n = len(gs)
    held = list(gs) + [lax.empty(g.shape[1:], g.dtype) for g in gs] + [carry]

    def body(*refs):
        for cp in _pair_copies(refs[:n], refs[n:2 * n], refs[2 * n + 1], refs[2 * n + 2]):
            cp.start()

    sems = [pltpu.SemaphoreType.DMA((n,))] * 2
    out = pl.pallas_call(
        body, name=name, in_specs=[HBM] * len(held), out_specs=[SEM, SEM] + [HBM] * len(held),
        out_shape=sems + [pltpu.HBM(a.shape, a.dtype) for a in held],
        input_output_aliases={i: 2 + i for i in range(len(held))},
        compiler_params=pltpu.CompilerParams(has_side_effects=EFFECT),
    )(*[pltpu.with_memory_space_constraint(a, pltpu.HBM) for a in held])
    return (out[0], out[1], out[2:2 + n], out[2 + n:2 + 2 * n]), out[2 + 2 * n]


def _pair_wait(name, started, after):
    send, recv, gs, lands = started
    n = len(gs)

    def body(*refs):
        for cp in _pair_copies(refs[:n], refs[n:2 * n], refs[2 * n], refs[2 * n + 1]):
            cp.wait_send()
            cp.wait_recv()

    out = pl.pallas_call(
        body, name=name, in_specs=[HBM] * (2 * n) + [SEM, SEM, ANY], out_specs=[HBM] * (2 * n),
        out_shape=[pltpu.HBM(a.shape, a.dtype) for a in list(gs) + list(lands)],
        input_output_aliases={i: i for i in range(2 * n)},
        compiler_params=pltpu.CompilerParams(has_side_effects=EFFECT),
    )(*gs, *lands, send, recv, after)
    return out[:n], out[n:]


def _row_tile(r, c):
    if c > D and r % 128 == 0:
        return 128
    return 256 if r % 256 == 0 else (GU // 2 if r % (GU // 2) == 0 else r)


def _pair_sum(name, core, g, r1):
    _, nq, nl, r, c = g.shape
    tr = r
    g4 = g.reshape(2, nq * nl, r, c)
    r3 = r1.reshape(nq * nl, r, c)

    def body(core_ref, g_ref, r_ref, o_ref):
        o_ref[...] = (g_ref[...].astype(f32) + r_ref[...].astype(f32)).astype(bf16)

    out = pl.pallas_call(
        body, name=name,
        grid_spec=pltpu.PrefetchScalarGridSpec(
            num_scalar_prefetch=1, grid=(nq * nl, r // tr),
            in_specs=[_bs((None, None, tr, c), lambda b, i, cr: (cr[0], b, i, 0)), _bs((None, tr, c), lambda b, i, cr: (b, i, 0))],
            out_specs=_bs((None, tr, c), lambda b, i, cr: (b, i, 0))),
        out_shape=S((nq * nl, r, c), bf16), compiler_params=_cp("parallel", "parallel"))(core, g4, r3)
    return out.reshape(nq, nl, r, c)


def _scatter_copies(ins, outs, send, recv):
    x, y, c = _place()
    chips = [(1 - x, y), (x, 1 - y), (1 - x, 1 - y)]
    return [pltpu.make_async_remote_copy(src_ref=ins[u].at[2 * chip[0] + chip[1]], dst_ref=outs[u].at[k],
                                         send_sem=send.at[u * 3 + k], recv_sem=recv.at[u * 3 + k],
                                         device_id=(*chip, c), device_id_type=MESH)
            for u in range(len(ins)) for k, chip in enumerate(chips)]


def _scatter_start(name, ss, carry):
    n = len(ss)
    lands = [lax.empty((3,) + s.shape[1:], s.dtype) for s in ss]
    held = list(ss) + lands + [carry]

    def body(*refs):
        for cp in _scatter_copies(refs[:n], refs[n:2 * n], refs[2 * n + 1], refs[2 * n + 2]):
            cp.start()

    sems = [pltpu.SemaphoreType.DMA((n * 3,))] * 2
    out = pl.pallas_call(
        body, name=name, in_specs=[HBM] * len(held), out_specs=[SEM, SEM] + [HBM] * len(held),
        out_shape=sems + [pltpu.HBM(a.shape, a.dtype) for a in held],
        input_output_aliases={i: 2 + i for i in range(len(held))},
        compiler_params=pltpu.CompilerParams(has_side_effects=EFFECT),
    )(*[pltpu.with_memory_space_constraint(a, pltpu.HBM) for a in held])
    return (out[0], out[1], out[2:2 + n], out[2 + n:2 + 2 * n]), out[2 + 2 * n]


def _scatter_wait(name, started, after):
    send, recv, srcs, lands = started
    n = len(srcs)

    def body(*refs):
        for cp in _scatter_copies(refs[:n], refs[n:2 * n], refs[2 * n], refs[2 * n + 1]):
            cp.wait_send()
            cp.wait_recv()

    out = pl.pallas_call(
        body, name=name, in_specs=[HBM] * (2 * n) + [SEM, SEM, ANY], out_specs=[HBM] * (2 * n),
        out_shape=[pltpu.HBM(a.shape, a.dtype) for a in list(srcs) + list(lands)],
        input_output_aliases={i: i for i in range(2 * n)},
        compiler_params=pltpu.CompilerParams(has_side_effects=EFFECT),
    )(*srcs, *lands, send, recv, after)
    return out[:n], out[n:]


def _sum_blocks(name, blocks):
    def body(b_ref, o_ref):
        acc = b_ref[0]
        for d in range(1, NDEV):
            acc = acc + b_ref[d]
        o_ref[...] = acc

    return pl.pallas_call(body, name=name, out_shape=S(blocks.shape[1:], f32),
                          compiler_params=pltpu.CompilerParams(vmem_limit_bytes=VMEM_LIMIT))(blocks)


def _adam_math(w, g, m, v):
    m = ADAM_B1 * m + (1.0 - ADAM_B1) * g
    v = ADAM_B2 * v + (1.0 - ADAM_B2) * (g * g)
    m_hat = m / (1.0 - ADAM_B1 ** ADAM_STEP)
    v_hat = v / (1.0 - ADAM_B2 ** ADAM_STEP)
    delta = -ADAM_LR * (m_hat / (jnp.sqrt(v_hat) + ADAM_EPS) + ADAM_WD * w)
    return delta, m, v


def _adamw(name, chip, w, m, v, parts):
    _, r, c = w.shape
    tr = _row_tile(r, c)

    def body(chip_ref, w_ref, m_ref, v_ref, *refs):
        sets, (g_ref, d_ref, mo_ref, vo_ref) = (refs[0:4], refs[4:8]), refs[8:]
        for l in range(2):
            @pl.when(pl.program_id(0) == l)
            def _():
                s_ref, r0_ref, r1_ref, r2_ref = sets[l]
                g = ((s_ref[...].astype(f32) + r0_ref[...].astype(f32)) + r1_ref[...].astype(f32)) + r2_ref[...].astype(f32)
                g_ref[...] = g
                d_ref[...], mo_ref[...], vo_ref[...] = _adam_math(w_ref[...], g, m_ref[...], v_ref[...])

    blk = _bs((None, tr, c), lambda l, i, cr: (l, i, 0))
    operands, specs = [], []
    for n in range(2):
        row = (lambda l, i: i * (1 - l)) if n == 0 else (lambda l, i: i * l)
        sums, r2 = parts[n]
        operands += [sums, r2, r2, r2]
        specs.append(_bs((None, None, tr, c), functools.partial(lambda l, i, cr, row: (cr[0], 0, row(l, i), 0), row=row)))
        specs += [_bs((None, None, tr, c), functools.partial(lambda l, i, cr, k, row: (k, 0, row(l, i), 0), k=k, row=row))
                  for k in range(3)]
    return pl.pallas_call(
        body, name=name,
        grid_spec=pltpu.PrefetchScalarGridSpec(num_scalar_prefetch=1, grid=(2, r // tr), in_specs=[blk, blk, blk] + specs,
                                               out_specs=[blk] * 4),
        out_shape=[S(w.shape, f32)] * 4, compiler_params=_cp("arbitrary", "arbitrary"),
    )(chip, w, m, v, *operands)


def _adamw_small(name, w, g, m, v):
    def body(w_ref, g_ref, m_ref, v_ref, d_ref, mo_ref, vo_ref):
        d_ref[...], mo_ref[...], vo_ref[...] = _adam_math(w_ref[...], g_ref[...], m_ref[...], v_ref[...])

    return pl.pallas_call(body, name=name, out_shape=[S(w.shape, f32)] * 3,
                          compiler_params=pltpu.CompilerParams(vmem_limit_bytes=VMEM_LIMIT))(w, g, m, v)


WEIGHT_NAMES = ("ffn1_norm", "ffn1_w_gu", "ffn1_w_down", "mix_norm", "w_in", "b_forget", "b_gate", "conv_w", "sgu_ln_g",
                "sgu_ln_b", "sgu_w", "sgu_b", "q_norm_g", "k_norm_g", "w_out_conv", "w_out_sgu", "w_out_attn", "w_o",
                "ffn2_norm", "ffn2_w_gu", "ffn2_w_down")
BIG = {"ffn1_w_gu": "gu1", "ffn2_w_gu": "gu2", "ffn1_w_down": "d1", "ffn2_w_down": "d2", "w_in": "in",
       "w_out_conv": "oc", "w_out_sgu": "os", "w_out_attn": "oa", "w_o": "o"}
BIG_KEYS = ("gu1", "gu2", "d1", "d2", "in", "oc", "os", "oa", "o")
REPLICATED_SMALL = ("ffn1_norm", "mix_norm", "b_forget", "sgu_ln_g", "sgu_ln_b", "sgu_w", "sgu_b", "q_norm_g",
                    "k_norm_g", "ffn2_norm")
SHARDED_SMALL = ("b_gate", "conv_w")
TRANSPOSED = ("gu1", "gu2")


def _packed_rows(shape):
    size = 1
    for s_ in shape:
        size *= s_
    return size, -(-size // 1024) * 8


def _pack(arrays):
    pieces = []
    for a in arrays:
        size, rows = _packed_rows(a.shape)
        pieces.append(jnp.pad(a.reshape(-1).astype(f32), (0, rows * 128 - size)).reshape(rows, 128))
    return jnp.concatenate(pieces, axis=0)


def _unpack(packed, shapes):
    out, pos = [], 0
    for shp in shapes:
        size, rows = _packed_rows(shp)
        out.append(packed[pos:pos + rows].reshape(-1)[:size].reshape(shp))
        pos += rows
    return out


def _natural_runs(a, b):
    runs = []
    while a < b:
        d = a // INB
        e = min(b, (d + 1) * INB)
        runs.append((d, a - d * INB, e - d * INB))
        a = e
    return runs


def _win_runs():
    runs = _natural_runs(0, GATE_OFF) + _natural_runs(GATE_OFF + NH, NIN) + _natural_runs(GATE_OFF, GATE_OFF + NH)
    out, pos = [], 0
    for d, a, b in runs:
        out.append((d, a, b, pos))
        pos += b - a
    return out


RELAYOUT_ROWS = 128


def _win_kernel_layout(name, wg):
    def body(w_ref, o_ref):
        for d, a, b, pos in _win_runs():
            o_ref[:, pos:pos + (b - a)] = w_ref[d, :, a:b]
        o_ref[:, NIN:NZ] = jnp.zeros((RELAYOUT_ROWS, NZ - NIN), wg.dtype)

    return pl.pallas_call(
        body, name=name, grid=(D // RELAYOUT_ROWS,), in_specs=[_bs((NDEV, RELAYOUT_ROWS, INB), lambda i: (0, i, 0))],
        out_specs=_bs((RELAYOUT_ROWS, NZ), lambda i: (i, 0)), out_shape=S((D, NZ), wg.dtype),
        compiler_params=_cp("parallel"))(wg)


def _win_device_blocks(name, dw):
    def body(dw_ref, o_ref):
        for d, a, b, pos in _win_runs():
            o_ref[d % 2, d // 2, :, a:b] = dw_ref[:, pos:pos + (b - a)]

    out = pl.pallas_call(
        body, name=name, grid=(D // RELAYOUT_ROWS,), in_specs=[_bs((RELAYOUT_ROWS, NZ), lambda i: (i, 0))],
        out_specs=_bs((2, 4, RELAYOUT_ROWS, INB), lambda i: (0, 0, i, 0)), out_shape=S((2, 4, D, INB), dw.dtype),
        compiler_params=_cp("parallel"))(dw)
    return out.reshape(2, 4, 1, D, INB)


def kernel(x, ffn1_norm, ffn1_w_gu, ffn1_w_down, mix_norm, w_in, b_forget, b_gate, conv_w, sgu_ln_g, sgu_ln_b, sgu_w, sgu_b, q_norm_g, k_norm_g, w_out_conv, w_out_sgu, w_out_attn, w_o, ffn2_norm, ffn2_w_gu, ffn2_w_down, loss_target, m_ffn1_norm, m_ffn1_w_gu, m_ffn1_w_down, m_mix_norm, m_w_in, m_b_forget, m_b_gate, m_conv_w, m_sgu_ln_g, m_sgu_ln_b, m_sgu_w, m_sgu_b, m_q_norm_g, m_k_norm_g, m_w_out_conv, m_w_out_sgu, m_w_out_attn, m_w_o, m_ffn2_norm, m_ffn2_w_gu, m_ffn2_w_down, v_ffn1_norm, v_ffn1_w_gu, v_ffn1_w_down, v_mix_norm, v_w_in, v_b_forget, v_b_gate, v_conv_w, v_sgu_ln_g, v_sgu_ln_b, v_sgu_w, v_sgu_b, v_q_norm_g, v_k_norm_g, v_w_out_conv, v_w_out_sgu, v_w_out_attn, v_w_o, v_ffn2_norm, v_ffn2_w_gu, v_ffn2_w_down):
    w = dict(zip(WEIGHT_NAMES, (ffn1_norm, ffn1_w_gu, ffn1_w_down, mix_norm, w_in, b_forget, b_gate, conv_w, sgu_ln_g,
                                sgu_ln_b, sgu_w, sgu_b, q_norm_g, k_norm_g, w_out_conv, w_out_sgu, w_out_attn, w_o,
                                ffn2_norm, ffn2_w_gu, ffn2_w_down)))
    mom = dict(zip(WEIGHT_NAMES, (m_ffn1_norm, m_ffn1_w_gu, m_ffn1_w_down, m_mix_norm, m_w_in, m_b_forget, m_b_gate,
                                  m_conv_w, m_sgu_ln_g, m_sgu_ln_b, m_sgu_w, m_sgu_b, m_q_norm_g, m_k_norm_g,
                                  m_w_out_conv, m_w_out_sgu, m_w_out_attn, m_w_o, m_ffn2_norm, m_ffn2_w_gu,
                                  m_ffn2_w_down)))
    var = dict(zip(WEIGHT_NAMES, (v_ffn1_norm, v_ffn1_w_gu, v_ffn1_w_down, v_mix_norm, v_w_in, v_b_forget, v_b_gate,
                                  v_conv_w, v_sgu_ln_g, v_sgu_ln_b, v_sgu_w, v_sgu_b, v_q_norm_g, v_k_norm_g,
                                  v_w_out_conv, v_w_out_sgu, v_w_out_attn, v_w_o, v_ffn2_norm, v_ffn2_w_gu,
                                  v_ffn2_w_down)))
    px, py, pc = _place()
    dev = 4 * px + 2 * py + pc
    chip = 2 * px + py

    big_names = [n for n in WEIGHT_NAMES if n in BIG]
    key_name = {BIG[n]: n for n in big_names}
    group_keys = (("gu1", "d1"), ("in", "oc", "os", "oa", "o", "small"), ("gu2", "d2"))

    def source(key, l):
        if key == "small":
            return jnp.concatenate([w["b_gate"][l], w["conv_w"][l], jnp.zeros((2, 128), f32)], axis=0)
        block = w[key_name[key]][l]
        return (block.T if key in TRANSPOSED else block).astype(bf16)

    def landing(src):
        return lax.dynamic_update_slice(lax.empty((NDEV,) + src.shape, src.dtype), src[None], (dev, 0, 0))

    groups = [[(s, landing(s)) for s in (source(k, l) for k in keys)] for l in range(2) for keys in group_keys]
    (send0, recv0, srcs0, lands0), = _gather_start("gather_start_first", groups[:1])
    rest, src0 = _gather_start("gather_start_rest", groups[1:], carry=srcs0[0])
    started = [(send0, recv0, [src0] + list(srcs0[1:]), lands0)] + rest

    early = {}

    def prefetch(l, part, act):
        got = _gather_wait(f"gather_wait_{l}_{part}", started[3 * l + part], act)
        early[l, part], act = _forward_start(f"forward_start_{l}_{part}", got, act)
        return act

    def weights(l, part, after):
        if (l, part) in early:
            lands = _forward_wait(f"forward_wait_{l}_{part}", early.pop((l, part)), after)
        else:
            got = _gather_wait(f"gather_wait_{l}_{part}", started[3 * l + part], after)
            lands = _gather_forward(f"gather_forward_{l}_{part}", got)
        return dict(zip(group_keys[part], lands))

    xl = x[0]
    saved, small, wts = [], [], []
    for l in range(2):
        ga = weights(l, 0, xl if l else None)
        if l:
            xl = prefetch(l, 1, xl)
        wt = {"gu1": ga["gu1"][None], "d1": ga["d1"].reshape(1, 4, GU, D)}
        x1, s1 = _ffn_fwd("ffn1", xl, w["ffn1_norm"][l].reshape(1, D), wt["gu1"], wt["d1"], 0)
        gb = weights(l, 1, x1)
        x1 = prefetch(l, 2, x1)
        p = {n: w[n][l] for n in REPLICATED_SMALL}
        p["b_gate"] = jnp.transpose(gb["small"][:, 0:3, :], (1, 0, 2)).reshape(3, D)
        p["conv_w"] = jnp.transpose(gb["small"][:, 3:6, :], (1, 0, 2)).reshape(3, D)
        p = _small_params(p)
        wt["win"] = _win_kernel_layout("win_layout", gb["in"])
        wt["sq"] = jnp.stack([gb[k].reshape(D, D) for k in ("oc", "os", "oa", "o")])[None]
        x2, sm = _mixer_fwd("mix", x1, p, wt["win"], wt["sq"], 0)
        gc = weights(l, 2, x2)
        if l == 0:
            x2 = prefetch(1, 0, x2)
        wt.update({"gu2": gc["gu2"][None], "d2": gc["d2"].reshape(1, 4, GU, D)})
        x3, s2 = _ffn_fwd("ffn2", x2, p["ffn2_norm"], wt["gu2"], wt["d2"], 0)
        saved.append((xl, x1, x2, s1, sm, s2))
        small.append(p)
        wts.append(wt)
        xl = x3
    loss_row, dx = _loss("loss", xl, loss_target[0])

    core = pc.reshape(1).astype(jnp.int32)
    chip_op = chip.reshape(1).astype(jnp.int32)
    buf = lambda r, c: lax.empty((2, 4, 1, r, c), bf16)
    flights = {}

    pairs = []
    scatter_keys = {"ffn2": ("gu2", "d2"), "sq": ("oc", "os", "oa", "o"), "in": ("in",), "ffn1": ("gu1", "d1")}

    def ship(l, part, bufs, dx):
        started, dx = _pair_start(f"pair_start_{l}_{part}", bufs, dx)
        pairs.append((l, part, started))
        return dx

    def finish(dx, after):
        l, part, started = pairs.pop()
        bufs, r1 = _pair_wait(f"pair_wait_{l}_{part}", started, after)
        ss = [_pair_sum(f"pair_sum_{l}_{k}", core, g, r) for k, g, r in zip(scatter_keys[part], bufs, r1)]
        flights[l, part], dx = _scatter_start(f"scatter_start_{l}_{part}", ss, dx)
        return dx

    def ship_in(l, dwin, dx):
        return ship(l, "in", [_win_device_blocks("dwin_blocks", dwin)], dx)

    sgrads = [None, None]
    for l in (1, 0):
        p, wt = small[l], wts[l]
        x0, x1, x2, s1, sm, s2 = saved[l]
        dx, dn2 = _ffn_bwd("ffn2", dx, x2, p["ffn2_norm"], wt["gu2"], wt["d2"], 0, s2, buf(GU, D), buf(GU // 2, D),
                           functools.partial(ship, l, "ffn2"), finish)
        dx, sg = _mixer_bwd("mix", dx, x1, p, wt["win"], wt["sq"], 0, sm, [buf(128, D) for _ in range(4)],
                            functools.partial(ship, l, "sq"), functools.partial(ship_in, l), finish)
        dx, dn1 = _ffn_bwd("ffn1", dx, x0, p["ffn1_norm"], wt["gu1"], wt["d1"], 0, s1, buf(GU, D), buf(GU // 2, D),
                           functools.partial(ship, l, "ffn1"), finish, hook_at_once=(l == 0))
        sg["ffn1_norm"] = dn1
        sg["ffn2_norm"] = dn2
        sgrads[l] = sg

    nat = [_small_grads_natural(sgrads[l]) for l in range(2)]
    order = REPLICATED_SMALL + SHARDED_SMALL
    part = _pack([jnp.stack([nat[0][n], nat[1][n]]) for n in order] + [loss_row[0, 0:1]])
    small_flight, dx = _gather_start("small_start", [[(part, landing(part))]], carry=dx)

    grads, delta, new_m, new_v = {}, {}, {}, {}
    after = dx
    for part in ("ffn2", "sq", "in", "ffn1"):
        sets = []
        for l in (1, 0):
            s_all, r2_all = _scatter_wait(f"scatter_wait_{l}_{part}", flights[l, part], after)
            sets.append(list(zip(s_all, r2_all)))
        for i, k in enumerate(scatter_keys[part]):
            n = key_name[k]
            view = (lambda a: jnp.swapaxes(a, 1, 2)) if k in TRANSPOSED else (lambda a: a)
            outs = _adamw("adamw_" + k, chip_op, view(w[n]), view(mom[n]), view(var[n]), [sets[1][i], sets[0][i]])
            grads[n], delta[n], new_m[n], new_v[n] = [view(o) for o in outs]
            after = outs[1]

    blocks = _gather_forward("small_forward", _gather_wait("small_wait", small_flight[0], after))
    total = _sum_blocks("small_sum", blocks[0])
    full_shapes = [(2,) + tuple(nat[0][n].shape) for n in order] + [(1,)]
    summed = dict(zip(order + ("loss",), _unpack(total, full_shapes)))
    for n in REPLICATED_SMALL:
        grads[n] = summed[n]
    for n in SHARDED_SMALL:
        grads[n] = lax.dynamic_slice_in_dim(summed[n], dev * 128, 128, axis=2)
    wp = _pack([w[n] for n in order])
    gp = _pack([grads[n] for n in order])
    mp = _pack([mom[n] for n in order])
    vp = _pack([var[n] for n in order])
    dpk, mpk, vpk = _adamw_small("adamw_small", wp, gp, mp, vp)
    local_shapes = [tuple(w[n].shape) for n in order]
    for dst, packed in ((delta, dpk), (new_m, mpk), (new_v, vpk)):
        dst.update(zip(order, _unpack(packed, local_shapes)))

    loss = summed["loss"][0]
    return (loss, dx[None], *[grads[n] for n in WEIGHT_NAMES], *[delta[n] for n in WEIGHT_NAMES],
            *[new_m[n] for n in WEIGHT_NAMES], *[new_v[n] for n in WEIGHT_NAMES])
```

```python
import functools

import jax
import jax.numpy as jnp
from jax import lax
from jax.experimental import pallas as pl
from jax.experimental.pallas import tpu as pltpu

f32 = jnp.float32
bf16 = jnp.bfloat16
S = jax.ShapeDtypeStruct
MESH = pl.DeviceIdType.MESH

D = 1024
NH = 8
HD = 128
NDEV = 8
GU = 704
NIN = 11272
INB = 1409
GATE_OFF = 8192
F_OFF = 11264
NZ = 11776
RMS_EPS = 1e-6
LN_EPS = 1e-5
ATT_SCALE = HD ** -0.5
NEG = -1e30
INV_SQRT2 = 0.7071067811865476
INV_SQRT2PI = 0.3989422804014327

ADAM_LR = 0.001
ADAM_B1 = 0.9
ADAM_B2 = 0.999
ADAM_EPS = 1e-08
ADAM_WD = 0.01
ADAM_STEP = 10

TT = 512
VMEM_LIMIT = 56 * 1024 * 1024


def _cp(*sem):
    return pltpu.CompilerParams(dimension_semantics=sem, vmem_limit_bytes=VMEM_LIMIT)


def _bs(shape, fn):
    return pl.BlockSpec(shape, fn)


NN = (((1,), (0,)), ((), ()))
NT = (((1,), (1,)), ((), ()))
TN = (((0,), (0,)), ((), ()))


def _mm(name, a, b, *, grid, a_spec, b_spec, out_shape, out_spec, dims, acc_shape, res=None, res_spec=None,
        alpha=1.0, alias=None, split_rows=None):
    nk = grid[2]

    def body(*refs):
        a_ref, b_ref = refs[0], refs[1]
        pos = 2
        res_ref = None
        if res is not None:
            res_ref = refs[pos]
            pos += 1
        if alias is not None:
            pos += 1
        o_ref = refs[pos]
        acc_ref = refs[pos + 1] if nk > 1 else None
        part = lax.dot_general(a_ref[...].astype(bf16), b_ref[...].astype(bf16), dims, preferred_element_type=f32)

        def finish(acc):
            if alpha != 1.0:
                acc = alpha * acc
            if res_ref is not None:
                acc = res_ref[...] + acc
            if split_rows is None:
                o_ref[...] = acc.astype(o_ref.dtype)
            else:
                o_ref[0] = acc[:split_rows].astype(o_ref.dtype)
                o_ref[1] = acc[split_rows:].astype(o_ref.dtype)

        if nk == 1:
            finish(part)
        else:
            k = pl.program_id(2)

            @pl.when(k == 0)
            def _():
                acc_ref[...] = part

            @pl.when(k > 0)
            def _():
                acc_ref[...] += part

            @pl.when(k == nk - 1)
            def _():
                finish(acc_ref[...])

    operands = [a, b]
    in_specs = [a_spec, b_spec]
    if res is not None:
        operands.append(res)
        in_specs.append(res_spec)
    aliases = {}
    if alias is not None:
        aliases = {len(operands): 0}
        operands.append(alias)
        in_specs.append(pl.BlockSpec(memory_space=pl.ANY))
    return pl.pallas_call(
        body, name=name, grid=grid, in_specs=in_specs, out_specs=out_spec, out_shape=out_shape,
        scratch_shapes=[pltpu.VMEM(acc_shape, f32)] if nk > 1 else [],
        input_output_aliases=aliases,
        compiler_params=_cp("parallel", "parallel", "arbitrary"),
    )(*operands)


def _tile(n, t):
    return t if n % t == 0 and n >= t else n


DZ_TILE = 512


def _dz_matmul(name, pieces, other, weight_grad):
    T = pieces[0].shape[0]
    counts = [p.shape[1] // DZ_TILE for p in pieces]
    starts = [sum(counts[:i]) for i in range(len(counts))]
    steps = sum(counts)
    npc = len(pieces)

    def body(*refs):
        prefs, o_ref, rest = refs[:npc], refs[npc], refs[npc + 1:]
        k = pl.program_id(0)
        if not weight_grad:
            out_ref, acc_ref = rest

            @pl.when(k == 0)
            def _():
                acc_ref[...] = jnp.zeros_like(acc_ref)

        for p_ref, s, c in zip(prefs, starts, counts):
            @pl.when((k >= s) & (k < s + c))
            def _():
                if weight_grad:
                    rest[0][...] = lax.dot_general(o_ref[...], p_ref[...], TN, preferred_element_type=f32).astype(bf16)
                else:
                    acc_ref[...] += lax.dot_general(p_ref[...], o_ref[...], NT, preferred_element_type=f32)

        if not weight_grad:
            @pl.when(k == steps - 1)
            def _():
                out_ref[...] = acc_ref[...]

    piece_specs = [_bs((T, DZ_TILE), functools.partial(lambda k, s, c: (0, jnp.clip(k - s, 0, c - 1)), s=s, c=c))
                   for s, c in zip(starts, counts)]
    if weight_grad:
        other_spec, out_spec, out_shape, scratch = _bs((T, D), lambda k: (0, 0)), _bs((D, DZ_TILE), lambda k: (0, k)), S((D, NZ), bf16), []
    else:
        other_spec, out_spec, out_shape = _bs((D, DZ_TILE), lambda k: (0, k)), _bs((T, D), lambda k: (0, 0)), S((T, D), f32)
        scratch = [pltpu.VMEM((T, D), f32)]
    return pl.pallas_call(body, name=name, grid=(steps,), in_specs=piece_specs + [other_spec], out_specs=out_spec,
                          out_shape=out_shape, scratch_shapes=scratch, compiler_params=_cp("arbitrary"))(*pieces, other)


def _row(cb=0, w=D):
    return _bs((TT, w), lambda i: (i, cb))


def _vec(rows=1, w=D):
    return _bs((rows, w), lambda i: (0, 0))


def _acc_store(i, ref, val):
    @pl.when(i == 0)
    def _():
        ref[...] = val

    @pl.when(i > 0)
    def _():
        ref[...] += val


def _rms_fwd(name, x, g):
    T = x.shape[0]

    def body(x_ref, g_ref, o_ref):
        xv = x_ref[...]
        r = lax.rsqrt(jnp.mean(xv * xv, axis=-1, keepdims=True) + RMS_EPS)
        o_ref[...] = (xv * r * g_ref[...]).astype(bf16)

    return pl.pallas_call(body, name=name, grid=(T // TT,), in_specs=[_row(), _vec()], out_specs=_row(),
                          out_shape=S((T, D), bf16), compiler_params=_cp("parallel"))(x, g)


def _rms_bwd(name, dh, x, g, dres):
    T = x.shape[0]

    def body(dh_ref, x_ref, g_ref, dres_ref, dx_ref, dg_ref):
        i = pl.program_id(0)
        xv = x_ref[...]
        r = lax.rsqrt(jnp.mean(xv * xv, axis=-1, keepdims=True) + RMS_EPS)
        xhat = xv * r
        dh_v = dh_ref[...]
        dyg = dh_v * g_ref[...]
        m = jnp.mean(dyg * xhat, axis=-1, keepdims=True)
        dx_ref[...] = dres_ref[...] + r * (dyg - xhat * m)
        _acc_store(i, dg_ref, jnp.sum(dh_v * xhat, axis=0, keepdims=True))

    return pl.pallas_call(body, name=name, grid=(T // TT,), in_specs=[_row(), _row(), _vec(), _row()],
                          out_specs=[_row(), _vec()], out_shape=[S((T, D), f32), S((1, D), f32)],
                          compiler_params=_cp("arbitrary"))(dh, x, g, dres)


def _sigmoid(x):
    return 1.0 / (1.0 + jnp.exp(-x))


def _swiglu_fwd(name, h, wgu, l):
    T = h.shape[0]

    def body(h_ref, wg_ref, wu_ref, gu_ref, a_ref):
        hv = h_ref[...]
        g = lax.dot_general(hv, wg_ref[...], NT, preferred_element_type=f32)
        u = lax.dot_general(hv, wu_ref[...], NT, preferred_element_type=f32)
        gu_ref[0] = g
        gu_ref[1] = u
        a_ref[...] = (g * _sigmoid(g) * u).astype(bf16)

    return pl.pallas_call(
        body, name=name, grid=(4,),
        in_specs=[_bs((T, D), lambda j: (0, 0)), _bs((None, None, GU, D), lambda j: (l, j, 0, 0)),
                  _bs((None, None, GU, D), lambda j: (l, j + 4, 0, 0))],
        out_specs=[_bs((2, None, T, GU), lambda j: (0, j, 0, 0)), _bs((None, T, GU), lambda j: (j, 0, 0))],
        out_shape=[S((2, 4, T, GU), f32), S((4, T, GU), bf16)], compiler_params=_cp("parallel"))(h, wgu, wgu)


def _swiglu_bwd(name, dxo, wd, gu, l):
    T = dxo.shape[0]
    tm = _tile(T, 1024)

    def body(dx_ref, wd_ref, g_ref, u_ref, o_ref):
        da = 0.5 * lax.dot_general(dx_ref[...].astype(bf16), wd_ref[...], NT, preferred_element_type=f32)
        g = g_ref[...]
        sg = _sigmoid(g)
        o_ref[0] = (da * u_ref[...] * (sg + g * sg * (1.0 - sg))).astype(bf16)
        o_ref[1] = (da * g * sg).astype(bf16)

    return pl.pallas_call(
        body, name=name, grid=(T // tm, 4),
        in_specs=[_bs((tm, D), lambda i, j: (i, 0)), _bs((None, None, GU, D), lambda i, j: (l, j, 0, 0)),
                  _bs((None, None, tm, GU), lambda i, j: (0, j, i, 0)), _bs((None, None, tm, GU), lambda i, j: (1, j, i, 0))],
        out_specs=_bs((2, None, tm, GU), lambda i, j: (0, j, i, 0)), out_shape=S((2, 4, T, GU), bf16),
        compiler_params=_cp("parallel", "parallel"))(dxo, wd, gu, gu)


def _loss(name, y, tgt):
    T = y.shape[0]

    def body(y_ref, t_ref, l_ref, dy_ref):
        i = pl.program_id(0)
        e = y_ref[...] - t_ref[...]
        dy_ref[...] = e * (1.0 / D)
        s = 0.5 * jnp.sum(jnp.mean(e * e, axis=-1, keepdims=True))
        _acc_store(i, l_ref, jnp.broadcast_to(s, (1, 128)))

    return pl.pallas_call(body, name=name, grid=(T // TT,), in_specs=[_row(), _row()],
                          out_specs=[_vec(1, 128), _row()], out_shape=[S((1, 128), f32), S((T, D), f32)],
                          compiler_params=_cp("arbitrary"))(y, tgt)


def _prev8(T, cb):
    return _bs((8, D), lambda i: (jnp.maximum(i * (TT // 8) - 1, 0), cb))


def _next8(T, cb):
    return _bs((8, D), lambda i: (jnp.minimum((i + 1) * (TT // 8), T // 8 - 1), cb))


def _conv_taps(i, ac_ref, ax_ref, pc_ref, px_ref):
    ca = ac_ref[...] * ax_ref[...]
    keep = (i > 0).astype(f32)
    p1 = pc_ref[7:8, :] * px_ref[7:8, :] * keep
    p2 = pc_ref[6:7, :] * px_ref[6:7, :] * keep
    row = lax.broadcasted_iota(jnp.int32, ca.shape, 0)
    s1 = jnp.where(row == 0, p1, pltpu.roll(ca, 1, 0))
    s2 = jnp.where(row == 0, p2, jnp.where(row == 1, p1, pltpu.roll(ca, 2, 0)))
    return ca, s1, s2


def _conv_fwd(name, z, cw):
    T = z.shape[0]

    def body(ab_ref, ac_ref, ax_ref, pc_ref, px_ref, w_ref, o_ref):
        i = pl.program_id(0)
        ca, s1, s2 = _conv_taps(i, ac_ref, ax_ref, pc_ref, px_ref)
        cv = w_ref[0:1, :] * s2 + w_ref[1:2, :] * s1 + w_ref[2:3, :] * ca
        o_ref[...] = (ab_ref[...] * cv).astype(bf16)

    return pl.pallas_call(
        body, name=name, grid=(T // TT,),
        in_specs=[_row(0), _row(1), _row(2), _prev8(T, 1), _prev8(T, 2), _vec(3)],
        out_specs=_row(), out_shape=S((T, D), bf16), compiler_params=_cp("parallel"))(z, z, z, z, z, cw)


def _conv_bwd(name, dya, z, cw):
    T = z.shape[0]
    n = T // TT

    def body(dya_ref, ab_ref, ac_ref, ax_ref, pc_ref, px_ref, ndya_ref, nab_ref, w_ref, dz_ref, dw_ref):
        i = pl.program_id(0)
        ca, s1, s2 = _conv_taps(i, ac_ref, ax_ref, pc_ref, px_ref)
        w0, w1, w2 = w_ref[0:1, :], w_ref[1:2, :], w_ref[2:3, :]
        cv = w0 * s2 + w1 * s1 + w2 * ca
        dya_v = dya_ref[...]
        ab = ab_ref[...]
        dcv = dya_v * ab
        keep = (i < n - 1).astype(f32)
        n1 = ndya_ref[0:1, :] * nab_ref[0:1, :] * keep
        n2 = ndya_ref[1:2, :] * nab_ref[1:2, :] * keep
        row = lax.broadcasted_iota(jnp.int32, dcv.shape, 0)
        f1 = jnp.where(row == TT - 1, n1, pltpu.roll(dcv, TT - 1, 0))
        f2 = jnp.where(row == TT - 1, n2, jnp.where(row == TT - 2, n1, pltpu.roll(dcv, TT - 2, 0)))
        dca = w2 * dcv + w1 * f1 + w0 * f2
        dz_ref[:, 0:D] = (dya_v * cv).astype(bf16)
        dz_ref[:, D:2 * D] = (dca * ax_ref[...]).astype(bf16)
        dz_ref[:, 2 * D:3 * D] = (dca * ac_ref[...]).astype(bf16)
        dw = jnp.concatenate([jnp.sum(dcv * s2, axis=0, keepdims=True), jnp.sum(dcv * s1, axis=0, keepdims=True),
                              jnp.sum(dcv * ca, axis=0, keepdims=True)], axis=0)
        _acc_store(i, dw_ref, dw)

    return pl.pallas_call(
        body, name=name, grid=(n,),
        in_specs=[_row(), _row(0), _row(1), _row(2), _prev8(T, 1), _prev8(T, 2), _next8(T, 0), _next8(T, 0), _vec(3)],
        out_specs=[_row(0, 3 * D), _vec(3)], out_shape=[S((T, 3 * D), bf16), S((3, D), f32)],
        compiler_params=_cp("arbitrary"))(dya, z, z, z, z, z, dya, z, cw)


def _gelu(x):
    return 0.5 * x * (1.0 + lax.erf(x * INV_SQRT2))


def _gelu_cdf(x):
    return 0.5 * (1.0 + lax.erf(x * INV_SQRT2))


def _gelu_grad(x, cdf):
    return cdf + x * jnp.exp(-0.5 * x * x) * INV_SQRT2PI


def _ln_stats(vv):
    mu = jnp.mean(vv, axis=-1, keepdims=True)
    xc = vv - mu
    rstd = lax.rsqrt(jnp.mean(xc * xc, axis=-1, keepdims=True) + LN_EPS)
    return xc * rstd, rstd


def _tril_w(w_ref, g):
    r = lax.broadcasted_iota(jnp.int32, (HD, HD), 0)
    c = lax.broadcasted_iota(jnp.int32, (HD, HD), 1)
    return jnp.where(c <= r, w_ref[g], 0.0).astype(bf16)


def _sgu_fwd(name, z, ln_g, ln_b, w_s, bmap):
    T = z.shape[0]

    def body(su_ref, sv_ref, lg_ref, lb_ref, w_ref, bm_ref, o_ref, vn_ref):
        xhat, _ = _ln_stats(_gelu(sv_ref[...]))
        vn_ref[...] = (xhat * lg_ref[...] + lb_ref[...]).astype(bf16)
        for g in range(NH):
            w = _tril_w(w_ref, g)
            cs = slice(g * HD, (g + 1) * HD)
            for c in range(TT // HD):
                rs = slice(c * HD, (c + 1) * HD)
                s = jnp.dot(w, vn_ref[rs, cs], preferred_element_type=f32) + bm_ref[:, cs]
                o_ref[rs, cs] = (_gelu(su_ref[rs, cs]) * s).astype(bf16)

    return pl.pallas_call(
        body, name=name, grid=(T // TT,),
        in_specs=[_row(3), _row(4), _vec(), _vec(), _bs((NH, HD, HD), lambda i: (0, 0, 0)), _vec(HD)],
        out_specs=_row(), out_shape=S((T, D), bf16), scratch_shapes=[pltpu.VMEM((TT, D), bf16)],
        compiler_params=_cp("parallel"))(z, z, ln_g, ln_b, w_s, bmap)


def _sgu_bwd(name, dyb, z, ln_g, ln_b, w_s, bmap):
    T = z.shape[0]

    def body(dyb_ref, su_ref, sv_ref, lg_ref, lb_ref, w_ref, bm_ref, dz_ref, dlg_ref, dlb_ref, dw_ref, db_ref,
             vn_ref, du_ref, dvn_ref, cu_ref, cv_ref):
        i = pl.program_id(0)
        sv = sv_ref[...]
        cv_ref[...] = _gelu_cdf(sv)
        cu_ref[...] = _gelu_cdf(su_ref[...])
        xhat, rstd = _ln_stats(sv * cv_ref[...])
        vn_ref[...] = (xhat * lg_ref[...] + lb_ref[...]).astype(bf16)
        r = lax.broadcasted_iota(jnp.int32, (HD, HD), 0)
        cc = lax.broadcasted_iota(jnp.int32, (HD, HD), 1)
        for g in range(NH):
            w = _tril_w(w_ref, g)
            cs = slice(g * HD, (g + 1) * HD)
            dw = jnp.zeros((HD, HD), f32)
            db = jnp.zeros((HD, 1), f32)
            for c in range(TT // HD):
                rs = slice(c * HD, (c + 1) * HD)
                vnb = vn_ref[rs, cs]
                s = jnp.dot(w, vnb, preferred_element_type=f32) + bm_ref[:, cs]
                dy = dyb_ref[rs, cs]
                du_ref[rs, cs] = dy * s
                ds = dy * (su_ref[rs, cs] * cu_ref[rs, cs])
                ds16 = ds.astype(bf16)
                dvn_ref[rs, cs] = lax.dot_general(w, ds16, TN, preferred_element_type=f32)
                dw = dw + lax.dot_general(ds16, vnb, NT, preferred_element_type=f32)
                db = db + jnp.sum(ds, axis=1, keepdims=True)
            dw = jnp.where(cc <= r, dw, 0.0)

            @pl.when(i == 0)
            def _():
                dw_ref[g] = dw
                db_ref[:, g:g + 1] = db

            @pl.when(i > 0)
            def _():
                dw_ref[g] += dw
                db_ref[:, g:g + 1] += db

        dvn = dvn_ref[...]
        dxh = dvn * lg_ref[...]
        m1 = jnp.mean(dxh, axis=-1, keepdims=True)
        m2 = jnp.mean(dxh * xhat, axis=-1, keepdims=True)
        dvv = rstd * (dxh - m1 - xhat * m2)
        dz_ref[:, 0:D] = (du_ref[...] * _gelu_grad(su_ref[...], cu_ref[...])).astype(bf16)
        dz_ref[:, D:2 * D] = (dvv * _gelu_grad(sv, cv_ref[...])).astype(bf16)
        _acc_store(i, dlg_ref, jnp.sum(dvn * xhat, axis=0, keepdims=True))
        _acc_store(i, dlb_ref, jnp.sum(dvn, axis=0, keepdims=True))

    return pl.pallas_call(
        body, name=name, grid=(T // TT,),
        in_specs=[_row(), _row(3), _row(4), _vec(), _vec(), _bs((NH, HD, HD), lambda i: (0, 0, 0)), _vec(HD)],
        out_specs=[_row(0, 2 * D), _vec(), _vec(), _bs((NH, HD, HD), lambda i: (0, 0, 0)), _bs((HD, NH), lambda i: (0, 0))],
        out_shape=[S((T, 2 * D), bf16), S((1, D), f32), S((1, D), f32), S((NH, HD, HD), f32), S((HD, NH), f32)],
        scratch_shapes=[pltpu.VMEM((TT, D), bf16)] + [pltpu.VMEM((TT, D), f32)] * 4,
        compiler_params=_cp("arbitrary"))(dyb, z, z, ln_g, ln_b, w_s, bmap)


def _qk_fwd(name, z, qg, kg, bf):
    T = z.shape[0]

    def body(q_ref, k_ref, v_ref, zf_ref, qg_ref, kg_ref, bf_ref, qn_ref, kn_ref, vb_ref, lf_ref):
        for h in range(NH):
            cs = slice(h * HD, (h + 1) * HD)
            for src, gain, dst in ((q_ref, qg_ref, qn_ref), (k_ref, kg_ref, kn_ref)):
                xv = src[:, cs]
                r = lax.rsqrt(jnp.mean(xv * xv, axis=-1, keepdims=True) + RMS_EPS)
                dst[:, cs] = (xv * r * gain[:, cs]).astype(bf16)
        vb_ref[...] = v_ref[...].astype(bf16)
        xf = zf_ref[...] + bf_ref[...]
        lf_ref[...] = jnp.minimum(xf, 0.0) - jnp.log1p(jnp.exp(-jnp.abs(xf)))

    return pl.pallas_call(
        body, name=name, grid=(T // TT,),
        in_specs=[_row(5), _row(6), _row(7), _bs((TT, 128), lambda i: (i, F_OFF // 128)), _vec(), _vec(), _vec(1, 128)],
        out_specs=[_row(), _row(), _row(), _bs((TT, 128), lambda i: (i, 0))],
        out_shape=[S((T, D), bf16), S((T, D), bf16), S((T, D), bf16), S((T, 128), f32)],
        compiler_params=_cp("parallel"))(z, z, z, z, qg, kg, bf)


def _cum_fwd(name, logf):
    T = logf.shape[0]

    def body(lf_ref, ccol_ref, crow_ref, c_ref):
        c = lf_ref[...]
        row = lax.broadcasted_iota(jnp.int32, c.shape, 0)
        s = 1
        while s < T:
            c = c + jnp.where(row >= s, pltpu.roll(c, s, 0), 0.0)
            s *= 2
        c_ref[...] = c
        crow_ref[...] = c.T[0:NH, :]
        for h in range(NH):
            ccol_ref[h] = jnp.broadcast_to(c_ref[:, h:h + 1], (T, 128))

    return pl.pallas_call(body, name=name, out_shape=[S((NH, T, 128), f32), S((NH, T), f32)],
                          scratch_shapes=[pltpu.VMEM((T, 128), f32)],
                          compiler_params=pltpu.CompilerParams(vmem_limit_bytes=VMEM_LIMIT))(logf)


ATT_TILE = 1024


def _fold(x, op=jnp.add):
    acc = x[:, 0:128]
    for t in range(1, x.shape[1] // 128):
        acc = op(acc, x[:, t * 128:(t + 1) * 128])
    return acc


def _to_row(col):
    return jnp.broadcast_to(col, (col.shape[0], 128)).T[0:1, :]


def _causal(t, keys_down=False):
    r = lax.broadcasted_iota(jnp.int32, (t, t), 0)
    c = lax.broadcasted_iota(jnp.int32, (t, t), 1)
    return r <= c if keys_down else c <= r


def _attn_fwd(name, qn, kn, vb, ccol, crow3):
    T = qn.shape[0]
    tq = _tile(T, ATT_TILE)
    nq = T // tq

    def body(q_ref, k_ref, v_ref, cc_ref, cr_ref, o_ref, lse_ref, lser_ref, s_ref):
        qi = pl.program_id(1)
        q = q_ref[...]
        cq = cc_ref[:, 0:1]

        def logits(off):
            s = lax.dot_general(q, k_ref[pl.ds(off, tq), :], NT, preferred_element_type=f32) * ATT_SCALE
            return s + cq - cr_ref[:, pl.ds(off, tq)]

        def below(j, mvec):
            off = pl.multiple_of(j * tq, tq)
            s = logits(off)
            s_ref[:, pl.ds(off, tq)] = s
            return jnp.maximum(mvec, _fold(s, jnp.maximum))

        mvec = lax.fori_loop(0, qi, below, jnp.full((tq, 128), NEG, f32))
        off = pl.multiple_of(qi * tq, tq)
        s = jnp.where(_causal(tq), logits(off), NEG)
        s_ref[:, pl.ds(off, tq)] = s
        m = jnp.max(jnp.maximum(mvec, _fold(s, jnp.maximum)), axis=1, keepdims=True)

        def weigh(j, carry):
            lvec, acc = carry
            off = pl.multiple_of(j * tq, tq)
            p = jnp.exp(s_ref[:, pl.ds(off, tq)] - m)
            acc = acc + jnp.dot(p.astype(bf16), v_ref[pl.ds(off, tq), :], preferred_element_type=f32)
            return lvec + _fold(p), acc

        lvec, acc = lax.fori_loop(0, qi + 1, weigh, (jnp.zeros((tq, 128), f32), jnp.zeros((tq, HD), f32)))
        l = jnp.sum(lvec, axis=1, keepdims=True)
        o_ref[...] = acc / l
        lse = m + jnp.log(l)
        lse_ref[...] = jnp.broadcast_to(lse, (tq, 128))
        lser_ref[...] = _to_row(lse)

    return pl.pallas_call(
        body, name=name, grid=(NH, nq),
        in_specs=[_bs((tq, HD), lambda h, i: (i, h)), _bs((T, HD), lambda h, i: (0, h)), _bs((T, HD), lambda h, i: (0, h)),
                  _bs((None, tq, 128), lambda h, i: (h, i, 0)), _bs((None, 1, T), lambda h, i: (h, 0, 0))],
        out_specs=[_bs((tq, HD), lambda h, i: (i, h)), _bs((None, tq, 128), lambda h, i: (h, i, 0)),
                   _bs((None, 1, tq), lambda h, i: (h, 0, i))],
        out_shape=[S((T, D), f32), S((NH, T, 128), f32), S((NH, 1, T), f32)],
        scratch_shapes=[pltpu.VMEM((tq, T), f32)],
        compiler_params=_cp("parallel", "parallel"))(qn, kn, vb, ccol, crow3)


def _attn_dq(name, qn, kn, vb, do, lse, ccol, crow3):
    T = qn.shape[0]
    tq = _tile(T, ATT_TILE)
    nq = T // tq

    def body(q_ref, k_ref, v_ref, do_ref, lse_ref, cc_ref, cr_ref, dq_ref, dlr_ref, p_ref, dp_ref):
        qi = pl.program_id(1)
        q = q_ref[...]
        do16 = do_ref[...].astype(bf16)
        base = cc_ref[:, 0:1] - lse_ref[:, 0:1]

        def probs(off):
            s = lax.dot_general(q, k_ref[pl.ds(off, tq), :], NT, preferred_element_type=f32) * ATT_SCALE
            return jnp.exp(s + base - cr_ref[:, pl.ds(off, tq)])

        def keep(off, p, dvec):
            dp = lax.dot_general(do16, v_ref[pl.ds(off, tq), :], NT, preferred_element_type=f32)
            p_ref[:, pl.ds(off, tq)] = p
            dp_ref[:, pl.ds(off, tq)] = dp
            return dvec + _fold(p * dp)

        def below(j, dvec):
            off = pl.multiple_of(j * tq, tq)
            return keep(off, probs(off), dvec)

        dvec = lax.fori_loop(0, qi, below, jnp.zeros((tq, 128), f32))
        off = pl.multiple_of(qi * tq, tq)
        dvec = keep(off, jnp.where(_causal(tq), probs(off), 0.0), dvec)
        delta = jnp.sum(dvec, axis=1, keepdims=True)

        def grad(j, acc):
            off = pl.multiple_of(j * tq, tq)
            ds = p_ref[:, pl.ds(off, tq)] * (dp_ref[:, pl.ds(off, tq)] - delta)
            return acc + jnp.dot(ds.astype(bf16), k_ref[pl.ds(off, tq), :], preferred_element_type=f32)

        dq_ref[...] = lax.fori_loop(0, qi + 1, grad, jnp.zeros((tq, HD), f32)) * ATT_SCALE
        dlr_ref[...] = _to_row(delta)

    qb = lambda h, i: (i, h)
    full = lambda h, i: (0, h)
    col = lambda h, i: (h, i, 0)
    return pl.pallas_call(
        body, name=name, grid=(NH, nq),
        in_specs=[_bs((tq, HD), qb), _bs((T, HD), full), _bs((T, HD), full), _bs((tq, HD), qb),
                  _bs((None, tq, 128), col), _bs((None, tq, 128), col), _bs((None, 1, T), lambda h, i: (h, 0, 0))],
        out_specs=[_bs((tq, HD), qb), _bs((None, 1, tq), lambda h, i: (h, 0, i))],
        out_shape=[S((T, D), f32), S((NH, 1, T), f32)],
        scratch_shapes=[pltpu.VMEM((tq, T), f32), pltpu.VMEM((tq, T), f32)],
        compiler_params=_cp("parallel", "parallel"))(qn, kn, vb, do, lse, ccol, crow3)


def _attn_dkv(name, qn, kn, vb, do, lser3, dlr3, ccol, crow3):
    T = qn.shape[0]
    tk = _tile(T, ATT_TILE)
    nk = T // tk

    def body(q_ref, k_ref, v_ref, do_ref, lser_ref, dlr_ref, cc_ref, cr_ref, dk_ref, dv_ref, cs_ref):
        h = pl.program_id(0)
        kj = pl.program_id(1)

        @pl.when((h == 0) & (kj == 0))
        def _():
            cs_ref[...] = jnp.zeros_like(cs_ref)

        kb = k_ref[...]
        vv = v_ref[...]
        ckey = cc_ref[:, 0:1]

        def block(off, diagonal):
            rows = pl.ds(off, tk)
            qb = q_ref[rows, :]
            do16 = do_ref[rows, :].astype(bf16)
            st = lax.dot_general(kb, qb, NT, preferred_element_type=f32) * ATT_SCALE
            pt = jnp.exp(st + (cr_ref[:, rows] - lser_ref[:, rows]) - ckey)
            if diagonal:
                pt = jnp.where(_causal(tk, keys_down=True), pt, 0.0)
            dpt = lax.dot_general(vv, do16, NT, preferred_element_type=f32)
            dst = pt * (dpt - dlr_ref[:, rows])
            ddv = jnp.dot(pt.astype(bf16), do16, preferred_element_type=f32)
            ddk = jnp.dot(dst.astype(bf16), qb, preferred_element_type=f32)
            return ddk, ddv, _fold(dst)

        def above(i, carry):
            ddk, ddv, dcs = block(pl.multiple_of(i * tk, tk), False)
            return carry[0] + ddk, carry[1] + ddv, carry[2] + dcs

        off = pl.multiple_of(kj * tk, tk)
        dk, dv, cs = lax.fori_loop(kj + 1, nk, above, block(off, True))
        dk_ref[...] = dk * ATT_SCALE
        dv_ref[...] = dv
        lane = lax.broadcasted_iota(jnp.int32, (tk, 128), 1)
        cs_ref[pl.ds(off, tk), :] += jnp.where(lane == h, jnp.sum(cs, axis=1, keepdims=True), 0.0)

    full = lambda h, j: (0, h)
    blk = lambda h, j: (j, h)
    row = lambda h, j: (h, 0, 0)
    return pl.pallas_call(
        body, name=name, grid=(NH, nk),
        in_specs=[_bs((T, HD), full), _bs((tk, HD), blk), _bs((tk, HD), blk), _bs((T, HD), full), _bs((None, 1, T), row),
                  _bs((None, 1, T), row), _bs((None, tk, 128), lambda h, j: (h, j, 0)), _bs((None, 1, T), row)],
        out_specs=[_bs((tk, HD), blk), _bs((tk, HD), blk), _bs((T, 128), lambda h, j: (0, 0))],
        out_shape=[S((T, D), f32), S((T, D), f32), S((T, 128), f32)],
        compiler_params=_cp("arbitrary", "arbitrary"))(qn, kn, vb, do, lser3, dlr3, ccol, crow3)


def _forget_bwd(name, cs, z, bf):
    T = cs.shape[0]

    def body(cs_ref, zf_ref, bf_ref, dz_ref, db_ref):
        c = -cs_ref[...]
        row = lax.broadcasted_iota(jnp.int32, c.shape, 0)
        s = 1
        while s < T:
            c = c + jnp.where(row + s < T, pltpu.roll(c, T - s, 0), 0.0)
            s *= 2
        xf = zf_ref[...] + bf_ref[...]
        lane = lax.broadcasted_iota(jnp.int32, c.shape, 1)
        dxf = jnp.where(lane < NH, c / (1.0 + jnp.exp(xf)), 0.0)
        dz_ref[...] = jnp.zeros_like(dz_ref)
        dz_ref[:, 0:128] = dxf.astype(bf16)
        db_ref[...] = jnp.sum(dxf, axis=0, keepdims=True)

    return pl.pallas_call(
        body, name=name, grid=(1,),
        in_specs=[_bs((T, 128), lambda i: (0, 0)), _bs((T, 128), lambda i: (0, F_OFF // 128)), _vec(1, 128)],
        out_specs=[_bs((T, NZ - F_OFF), lambda i: (0, 0)), _vec(1, 128)],
        out_shape=[S((T, NZ - F_OFF), bf16), S((1, 128), f32)], compiler_params=_cp("arbitrary"))(cs, z, bf)


def _qk_bwd(name, dqn, dkn, dv, z, qg, kg):
    T = z.shape[0]

    def body(dq_ref, dk_ref, dv_ref, q_ref, k_ref, qg_ref, kg_ref, dz_ref, dqg_ref, dkg_ref, g_ref):
        i = pl.program_id(0)
        for n, (src, dsrc, gain, dgain) in enumerate(((q_ref, dq_ref, qg_ref, dqg_ref), (k_ref, dk_ref, kg_ref, dkg_ref))):
            for h in range(NH):
                cs = slice(h * HD, (h + 1) * HD)
                xv = src[:, cs]
                r = lax.rsqrt(jnp.mean(xv * xv, axis=-1, keepdims=True) + RMS_EPS)
                xhat = xv * r
                dy = dsrc[:, cs]
                dyg = dy * gain[:, cs]
                m = jnp.mean(dyg * xhat, axis=-1, keepdims=True)
                dz_ref[:, n * D + h * HD:n * D + (h + 1) * HD] = (r * (dyg - xhat * m)).astype(bf16)
                g_ref[:, cs] = jnp.sum(dy * xhat, axis=0, keepdims=True)
            _acc_store(i, dgain, g_ref[...])
        dz_ref[:, 2 * D:3 * D] = dv_ref[...].astype(bf16)

    return pl.pallas_call(
        body, name=name, grid=(T // TT,),
        in_specs=[_row(), _row(), _row(), _row(5), _row(6), _vec(), _vec()],
        out_specs=[_row(0, 3 * D), _vec(), _vec()], out_shape=[S((T, 3 * D), bf16), S((1, D), f32), S((1, D), f32)],
        scratch_shapes=[pltpu.VMEM((1, D), f32)], compiler_params=_cp("arbitrary"))(dqn, dkn, dv, z, z, qg, kg)


GB = GATE_OFF // D


def _merge_fwd(name, ya, yb, yc, z, bg):
    T = z.shape[0]

    def body(ya_ref, yb_ref, yc_ref, g0_ref, g1_ref, g2_ref, bg_ref, o_ref):
        acc = _sigmoid(g0_ref[...] + bg_ref[0:1, :]) * ya_ref[...]
        acc = acc + _sigmoid(g1_ref[...] + bg_ref[1:2, :]) * yb_ref[...]
        acc = acc + _sigmoid(g2_ref[...] + bg_ref[2:3, :]) * yc_ref[...]
        o_ref[...] = acc.astype(bf16)

    return pl.pallas_call(
        body, name=name, grid=(T // TT,),
        in_specs=[_row(), _row(), _row(), _row(GB), _row(GB + 1), _row(GB + 2), _vec(3)],
        out_specs=_row(), out_shape=S((T, D), bf16), compiler_params=_cp("parallel"))(ya, yb, yc, z, z, z, bg)


def _merge_bwd(name, dm, ya, yb, yc, z, bg):
    T = z.shape[0]

    def body(dm_ref, ya_ref, yb_ref, yc_ref, g0_ref, g1_ref, g2_ref, bg_ref, dya_ref, dyb_ref, dyc_ref, dz_ref, db_ref):
        i = pl.program_id(0)
        dm_v = dm_ref[...]
        dbs = []
        for n, (y_ref, g_ref, dy_ref) in enumerate(((ya_ref, g0_ref, dya_ref), (yb_ref, g1_ref, dyb_ref),
                                                    (yc_ref, g2_ref, dyc_ref))):
            gate = _sigmoid(g_ref[...] + bg_ref[n:n + 1, :])
            dy_ref[...] = (dm_v * gate).astype(bf16)
            dl = dm_v * y_ref[...] * gate * (1.0 - gate)
            dz_ref[:, n * D:(n + 1) * D] = dl.astype(bf16)
            dbs.append(jnp.sum(dl, axis=0, keepdims=True))
        _acc_store(i, db_ref, jnp.concatenate(dbs, axis=0))

    return pl.pallas_call(
        body, name=name, grid=(T // TT,),
        in_specs=[_row(), _row(), _row(), _row(), _row(GB), _row(GB + 1), _row(GB + 2), _vec(3)],
        out_specs=[_row(), _row(), _row(), _row(0, 3 * D), _vec(3)],
        out_shape=[S((T, D), bf16), S((T, D), bf16), S((T, D), bf16), S((T, 3 * D), bf16), S((3, D), f32)],
        compiler_params=_cp("arbitrary"))(dm, ya, yb, yc, z, z, z, bg)


SMALL_NAMES = ("ffn1_norm", "mix_norm", "b_forget", "b_gate", "conv_w", "sgu_ln_g", "sgu_ln_b", "sgu_w", "sgu_b",
               "q_norm_g", "k_norm_g", "ffn2_norm")


def _small_params(p):
    out = {n: p[n].reshape(1, D) for n in ("ffn1_norm", "mix_norm", "ffn2_norm", "sgu_ln_g", "sgu_ln_b", "q_norm_g", "k_norm_g")}
    out["b_forget"] = jnp.pad(p["b_forget"].reshape(1, NH), ((0, 0), (0, 128 - NH)))
    out["b_gate"] = p["b_gate"]
    out["conv_w"] = p["conv_w"]
    out["sgu_w"] = p["sgu_w"]
    out["bmap"] = jnp.repeat(p["sgu_b"].T, HD, axis=1)
    return out


def _small_grads_natural(sg):
    out = {n: sg[n].reshape(D) for n in ("ffn1_norm", "mix_norm", "ffn2_norm", "sgu_ln_g", "sgu_ln_b")}
    out["q_norm_g"] = sg["q_norm_g"].reshape(NH, HD)
    out["k_norm_g"] = sg["k_norm_g"].reshape(NH, HD)
    out["b_forget"] = sg["b_forget"][0, :NH]
    out["b_gate"] = sg["b_gate"]
    out["conv_w"] = sg["conv_w"]
    out["sgu_w"] = sg["sgu_w"]
    out["sgu_b"] = sg["sgu_b"]
    return out


SQ_TM = 1024


def _sq_fwd(name, a, wsq, l, n, res=None):
    T = a.shape[0]
    tm = _tile(T, SQ_TM)
    return _mm(name, a, wsq, grid=(T // tm, 1, 1), a_spec=_bs((tm, D), lambda i, j, k: (i, 0)),
               b_spec=_bs((None, None, D, D), lambda i, j, k: (l, n, 0, 0)),
               out_shape=S((T, D), f32), out_spec=_bs((tm, D), lambda i, j, k: (i, 0)), dims=NN, acc_shape=None,
               res=res, res_spec=_bs((tm, D), lambda i, j, k: (i, 0)))


def _sq_bwd_in(name, dy, wsq, l, n):
    T = dy.shape[0]
    tm = _tile(T, SQ_TM)
    return _mm(name, dy, wsq, grid=(T // tm, 1, 1), a_spec=_bs((tm, D), lambda i, j, k: (i, 0)),
               b_spec=_bs((None, None, D, D), lambda i, j, k: (l, n, 0, 0)),
               out_shape=S((T, D), f32), out_spec=_bs((tm, D), lambda i, j, k: (i, 0)), dims=NT, acc_shape=None)


def _sq_bwd_w(name, a, dy, gbuf, l):
    T = a.shape[0]
    return _mm(name, a, dy, grid=(NDEV // 2, 1, 1), a_spec=_bs((T, 256), lambda i, j, k: (0, i)),
               b_spec=_bs((T, D), lambda i, j, k: (0, 0)), out_shape=S(gbuf.shape, bf16),
               out_spec=_bs((2, None, None, 128, D), lambda i, j, k: (0, i, l, 0, 0)),
               dims=TN, acc_shape=None, alias=gbuf, split_rows=128)


def _ffn_fwd(tag, x, g, wgu, wd, l):
    T = x.shape[0]
    h = _rms_fwd(tag + "_rms", x, g)
    gu, a = _swiglu_fwd(tag + "_gu", h, wgu, l)
    tm = _tile(T, 1024)
    xo = _mm(tag + "_down", a, wd, grid=(T // tm, 1, 4), a_spec=_bs((None, tm, GU), lambda i, j, k: (k, i, 0)),
             b_spec=_bs((None, None, GU, D), lambda i, j, k: (l, k, 0, 0)), out_shape=S((T, D), f32),
             out_spec=_bs((tm, D), lambda i, j, k: (i, 0)), dims=NN, acc_shape=(tm, D), res=x,
             res_spec=_bs((tm, D), lambda i, j, k: (i, 0)), alpha=0.5)
    return xo, (h, gu, a)


def _ffn_bwd(tag, dxo, x, g, wgu, wd, l, saved, g_gu, g_d, ship, hook, hook_at_once=False):
    h, gu, a = saved
    T = x.shape[0]
    g_d = _mm(tag + "_dwd", a, dxo, grid=(4, 1, 1), a_spec=_bs((None, T, GU), lambda i, j, k: (i, 0, 0)),
              b_spec=_bs((T, D), lambda i, j, k: (0, 0)), out_shape=S(g_d.shape, bf16),
              out_spec=_bs((2, None, None, GU // 2, D), lambda i, j, k: (0, i, l, 0, 0)), dims=TN, acc_shape=None,
              alpha=0.5, alias=g_d, split_rows=GU // 2)
    dgu = _swiglu_bwd(tag + "_dgu", dxo, wd, gu, l).reshape(NDEV, T, GU)
    g_gu = _mm(tag + "_dwgu", dgu, h, grid=(NDEV, 1, 1), a_spec=_bs((None, T, GU), lambda i, j, k: (i, 0, 0)),
               b_spec=_bs((T, D), lambda i, j, k: (0, 0)), out_shape=S(g_gu.shape, bf16),
               out_spec=_bs((None, None, None, GU, D), lambda i, j, k: (i % 2, i // 2, l, 0, 0)), dims=TN,
               acc_shape=None, alias=g_gu)
    dxo = ship([g_gu, g_d], dxo)
    if hook_at_once:
        dxo = hook(dxo, g_gu)
    dh = _mm(tag + "_dh", dgu, wgu, grid=(1, 1, NDEV), a_spec=_bs((None, T, GU), lambda i, j, k: (k, 0, 0)),
             b_spec=_bs((None, None, GU, D), lambda i, j, k: (l, k, 0, 0)), out_shape=S((T, D), f32),
             out_spec=_bs((T, D), lambda i, j, k: (0, 0)), dims=NN, acc_shape=(T, D))
    if not hook_at_once:
        dxo = hook(dxo, dh)
    return _rms_bwd(tag + "_drms", dh, x, g, dxo)


def _mixer_fwd(tag, x, p, win, wsq, l, after_in=None):
    T = x.shape[0]
    h = _rms_fwd(tag + "_rms", x, p["mix_norm"])
    tn = 512
    z = _mm(tag + "_in", h, win, grid=(1, NZ // tn, 1), a_spec=_bs((T, D), lambda i, j, k: (0, 0)),
            b_spec=_bs((D, tn), lambda i, j, k: (0, j)), out_shape=S((T, NZ), f32),
            out_spec=_bs((T, tn), lambda i, j, k: (0, j)), dims=NN, acc_shape=None)
    if after_in is not None:
        z = after_in(z)
    ya_in = _conv_fwd(tag + "_conv", z, p["conv_w"])
    yb_in = _sgu_fwd(tag + "_sgu", z, p["sgu_ln_g"], p["sgu_ln_b"], p["sgu_w"], p["bmap"])
    qn, kn, vb, logf = _qk_fwd(tag + "_qk", z, p["q_norm_g"], p["k_norm_g"], p["b_forget"])
    ccol, crow = _cum_fwd(tag + "_cum", logf)
    crow3 = crow.reshape(NH, 1, T)
    o, lse, lser = _attn_fwd(tag + "_attn", qn, kn, vb, ccol, crow3)
    ya = _sq_fwd(tag + "_oconv", ya_in, wsq, l, 0)
    yb = _sq_fwd(tag + "_osgu", yb_in, wsq, l, 1)
    yc = _sq_fwd(tag + "_oattn", o, wsq, l, 2)
    merged = _merge_fwd(tag + "_merge", ya, yb, yc, z, p["b_gate"])
    xo = _sq_fwd(tag + "_o", merged, wsq, l, 3, res=x)
    return xo, (h, z, ya_in, yb_in, qn, kn, vb, ccol, crow3, o, lse, lser, ya, yb, yc, merged)


def _mixer_bwd(tag, dxo, x, p, win, wsq, l, saved, gsq, ship_sq, ship_in, hook):
    h, z, ya_in, yb_in, qn, kn, vb, ccol, crow3, o, lse, lser, ya, yb, yc, merged = saved
    T = x.shape[0]
    sg = {}
    dm = _sq_bwd_in(tag + "_dmerged", dxo, wsq, l, 3)
    gsq[3] = _sq_bwd_w(tag + "_dwo", merged, dxo, gsq[3], l)
    dya, dyb, dyc, dz_g, sg["b_gate"] = _merge_bwd(tag + "_dmerge", dm, ya, yb, yc, z, p["b_gate"])
    d_ya_in = _sq_bwd_in(tag + "_dconv_in", dya, wsq, l, 0)
    gsq[0] = _sq_bwd_w(tag + "_dwoc", ya_in, dya, gsq[0], l)
    d_yb_in = _sq_bwd_in(tag + "_dsgu_in", dyb, wsq, l, 1)
    gsq[1] = _sq_bwd_w(tag + "_dwos", yb_in, dyb, gsq[1], l)
    d_o = _sq_bwd_in(tag + "_dattn_in", dyc, wsq, l, 2)
    gsq[2] = _sq_bwd_w(tag + "_dwoa", o, dyc, gsq[2], l)
    dxo = ship_sq(gsq, dxo)
    dz_c, sg["conv_w"] = _conv_bwd(tag + "_dconv", d_ya_in, z, p["conv_w"])
    dxo = hook(dxo, dz_c)
    dz_s, sg["sgu_ln_g"], sg["sgu_ln_b"], sg["sgu_w"], db_t = _sgu_bwd(
        tag + "_dsgu", d_yb_in, z, p["sgu_ln_g"], p["sgu_ln_b"], p["sgu_w"], p["bmap"])
    sg["sgu_b"] = db_t.T
    dqn, dlr = _attn_dq(tag + "_dattn_q", qn, kn, vb, d_o, lse, ccol, crow3)
    dkn, dv, cs = _attn_dkv(tag + "_dattn_kv", qn, kn, vb, d_o, lser, dlr, ccol, crow3)
    dz_f, sg["b_forget"] = _forget_bwd(tag + "_dforget", cs, z, p["b_forget"])
    dz_q, sg["q_norm_g"], sg["k_norm_g"] = _qk_bwd(tag + "_dqk", dqn, dkn, dv, z, p["q_norm_g"], p["k_norm_g"])
    dz = [dz_c, dz_s, dz_q, dz_g, dz_f]
    dwin = _dz_matmul(tag + "_dwin", dz, h, True)
    dxo = ship_in(dwin, dxo)
    dh = _dz_matmul(tag + "_dh", dz, win, False)
    dxo = hook(dxo, dh)
    dx, sg["mix_norm"] = _rms_bwd(tag + "_drms", dh, x, p["mix_norm"], dxo)
    return dx, sg


ANY = pl.BlockSpec(memory_space=pl.ANY)
HBM = pl.BlockSpec(memory_space=pltpu.HBM)
SEM = pl.BlockSpec(memory_space=pltpu.SEMAPHORE)
EFFECT = pltpu.SideEffectType.DATAFLOW_SIDE_EFFECTING


def _place():
    return lax.axis_index("x"), lax.axis_index("y"), lax.axis_index("c")


NEAR = 4


def _others(x, y, c):
    return [(x, y, 1 - c), (1 - x, y, c), (x, 1 - y, c), (1 - x, 1 - y, c)]


def _gather_start(name, groups, carry=None):
    sizes = [len(g) for g in groups]
    srcs = [s for g in groups for s, _ in g]
    lands = [b for g in groups for _, b in g]
    n, ng = len(srcs), len(groups)
    held = srcs + lands + ([] if carry is None else [carry])
    nh = len(held)

    def body(*refs):
        src_refs, land_refs = refs[:n], refs[n:2 * n]
        send, recv = refs[nh:nh + ng], refs[nh + ng:nh + 2 * ng]
        x, y, c = _place()
        me = 4 * x + 2 * y + c
        u = 0
        for g, size in enumerate(sizes):
            for i in range(size):
                for k, peer in enumerate(_others(x, y, c)):
                    pltpu.make_async_remote_copy(src_ref=src_refs[u], dst_ref=land_refs[u].at[me],
                                                 send_sem=send[g].at[i * NEAR + k], recv_sem=recv[g].at[i * NEAR + k],
                                                 device_id=peer, device_id_type=MESH).start()
                u += 1

    sems = [pltpu.SemaphoreType.DMA((size * NEAR,)) for size in sizes]
    out = pl.pallas_call(
        body, name=name, in_specs=[HBM] * nh, out_specs=[SEM] * (2 * ng) + [HBM] * nh,
        out_shape=sems + sems + [pltpu.HBM(a.shape, a.dtype) for a in held],
        input_output_aliases={i: 2 * ng + i for i in range(nh)},
        compiler_params=pltpu.CompilerParams(has_side_effects=EFFECT),
    )(*[pltpu.with_memory_space_constraint(a, pltpu.HBM) for a in held])
    res, pos = [], 0
    for g, size in enumerate(sizes):
        res.append((out[g], out[ng + g], out[2 * ng + pos:2 * ng + pos + size], out[2 * ng + n + pos:2 * ng + n + pos + size]))
        pos += size
    return res if carry is None else (res, out[2 * ng + 2 * n])


def _gather_wait(name, started, after=None):
    send, recv, srcs, lands = started
    n = len(srcs)

    def body(*refs):
        src_refs, land_refs = refs[:n], refs[n:2 * n]
        send_ref, recv_ref = refs[2 * n], refs[2 * n + 1]
        x, y, c = _place()
        for i in range(n):
            for k, (px, py, pc) in enumerate(_others(x, y, c)):
                cp = pltpu.make_async_remote_copy(src_ref=src_refs[i], dst_ref=land_refs[i].at[4 * px + 2 * py + pc],
                                                  send_sem=send_ref.at[i * NEAR + k], recv_sem=recv_ref.at[i * NEAR + k],
                                                  device_id=(px, py, pc), device_id_type=MESH)
                cp.wait_send()
                cp.wait_recv()

    extra = [] if after is None else (list(after) if isinstance(after, (list, tuple)) else [after])
    out = pl.pallas_call(
        body, name=name, in_specs=[HBM] * (2 * n) + [SEM, SEM] + [ANY] * len(extra), out_specs=[HBM] * (2 * n),
        out_shape=[pltpu.HBM(a.shape, a.dtype) for a in list(srcs) + list(lands)],
        input_output_aliases={i: i for i in range(2 * n)},
        compiler_params=pltpu.CompilerParams(has_side_effects=EFFECT),
    )(*srcs, *lands, send, recv, *extra)
    return out[n:]


def _forward_copy(have, full, send, recv, i, j, core):
    x, y, c = _place()
    chip = [(1 - x, y), (x, 1 - y), (1 - x, 1 - y)][j]
    slot = 4 * chip[0] + 2 * chip[1] + core
    return pltpu.make_async_remote_copy(src_ref=have[i].at[slot], dst_ref=full[i].at[slot], send_sem=send.at[i * 3 + j],
                                        recv_sem=recv.at[i * 3 + j], device_id=(x, y, 1 - c), device_id_type=MESH)


def _gather_forward(name, lands):
    n = len(lands)

    def body(*refs):
        have, full = refs[:n], refs[n:2 * n]
        send, recv = refs[2 * n], refs[2 * n + 1]
        c = lax.axis_index("c")
        for i in range(n):
            for j in range(3):
                _forward_copy(have, full, send, recv, i, j, c).start()
        for i in range(n):
            for j in range(3):
                _forward_copy(have, full, send, recv, i, j, c).wait_send()
                _forward_copy(have, full, send, recv, i, j, 1 - c).wait_recv()

    return pl.pallas_call(
        body, name=name, in_specs=[ANY] * n, out_specs=[ANY] * n, out_shape=[S(a.shape, a.dtype) for a in lands],
        input_output_aliases={i: i for i in range(n)},
        scratch_shapes=[pltpu.SemaphoreType.DMA((n * 3,)), pltpu.SemaphoreType.DMA((n * 3,))],
    )(*lands)


def _forward_start(name, lands, carry):
    n = len(lands)
    held = list(lands) + [carry]

    def body(*refs):
        c = lax.axis_index("c")
        for i in range(n):
            for j in range(3):
                _forward_copy(refs[:n], refs[:n], refs[n + 1], refs[n + 2], i, j, c).start()

    sems = [pltpu.SemaphoreType.DMA((n * 3,))] * 2
    out = pl.pallas_call(
        body, name=name, in_specs=[HBM] * (n + 1), out_specs=[SEM, SEM] + [HBM] * (n + 1),
        out_shape=sems + [pltpu.HBM(a.shape, a.dtype) for a in held],
        input_output_aliases={i: 2 + i for i in range(n + 1)},
        compiler_params=pltpu.CompilerParams(has_side_effects=EFFECT),
    )(*[pltpu.with_memory_space_constraint(a, pltpu.HBM) for a in held])
    return (out[0], out[1], out[2:2 + n]), out[2 + n]


def _forward_wait(name, started, after):
    send, recv, lands = started
    n = len(lands)

    def body(*refs):
        c = lax.axis_index("c")
        for i in range(n):
            for j in range(3):
                _forward_copy(refs[:n], refs[:n], refs[n], refs[n + 1], i, j, c).wait_send()
                _forward_copy(refs[:n], refs[:n], refs[n], refs[n + 1], i, j, 1 - c).wait_recv()

    return pl.pallas_call(
        body, name=name, in_specs=[HBM] * n + [SEM, SEM, ANY], out_specs=[HBM] * n,
        out_shape=[pltpu.HBM(a.shape, a.dtype) for a in lands],
        input_output_aliases={i: i for i in range(n)},
        compiler_params=pltpu.CompilerParams(has_side_effects=EFFECT),
    )(*lands, send, recv, after)


def _pair_copies(ins, outs, send, recv):
    x, y, c = _place()
    return [pltpu.make_async_remote_copy(src_ref=ins[u].at[1 - c], dst_ref=outs[u], send_sem=send.at[u],
                                         recv_sem=recv.at[u], device_id=(x, y, 1 - c), device_id_type=MESH)
            for u in range(len(ins))]


def _pair_start(name, gs, carry):
    n = len(gs)
    held = list(gs) + [lax.empty(g.shape[1:], g.dtype) for g in gs] + [carry]

    def body(*refs):
        for cp in _pair_copies(refs[:n], refs[n:2 * n], refs[2 * n + 1], refs[2 * n + 2]):
            cp.start()

    sems = [pltpu.SemaphoreType.DMA((n,))] * 2
    out = pl.pallas_call(
        body, name=name, in_specs=[HBM] * len(held), out_specs=[SEM, SEM] + [HBM] * len(held),
        out_shape=sems + [pltpu.HBM(a.shape, a.dtype) for a in held],
        input_output_aliases={i: 2 + i for i in range(len(held))},
        compiler_params=pltpu.CompilerParams(has_side_effects=EFFECT),
    )(*[pltpu.with_memory_space_constraint(a, pltpu.HBM) for a in held])
    return (out[0], out[1], out[2:2 + n], out[2 + n:2 + 2 * n]), out[2 + 2 * n]


def _pair_wait(name, started, after):
    send, recv, gs, lands = started
    n = len(gs)

    def body(*refs):
        for cp in _pair_copies(refs[:n], refs[n:2 * n], refs[2 * n], refs[2 * n + 1]):
            cp.wait_send()
            cp.wait_recv()

    out = pl.pallas_call(
        body, name=name, in_specs=[HBM] * (2 * n) + [SEM, SEM, ANY], out_specs=[HBM] * (2 * n),
        out_shape=[pltpu.HBM(a.shape, a.dtype) for a in list(gs) + list(lands)],
        input_output_aliases={i: i for i in range(2 * n)},
        compiler_params=pltpu.CompilerParams(has_side_effects=EFFECT),
    )(*gs, *lands, send, recv, after)
    return out[:n], out[n:]


def _row_tile(r, c):
    if c > D and r % 128 == 0:
        return 128
    return 256 if r % 256 == 0 else (GU // 2 if r % (GU // 2) == 0 else r)


def _pair_sum(name, core, g, r1):
    _, nq, nl, r, c = g.shape
    tr = r
    g4 = g.reshape(2, nq * nl, r, c)
    r3 = r1.reshape(nq * nl, r, c)

    def body(core_ref, g_ref, r_ref, o_ref):
        o_ref[...] = (g_ref[...].astype(f32) + r_ref[...].astype(f32)).astype(bf16)

    out = pl.pallas_call(
        body, name=name,
        grid_spec=pltpu.PrefetchScalarGridSpec(
            num_scalar_prefetch=1, grid=(nq * nl, r // tr),
            in_specs=[_bs((None, None, tr, c), lambda b, i, cr: (cr[0], b, i, 0)), _bs((None, tr, c), lambda b, i, cr: (b, i, 0))],
            out_specs=_bs((None, tr, c), lambda b, i, cr: (b, i, 0))),
        out_shape=S((nq * nl, r, c), bf16), compiler_params=_cp("parallel", "parallel"))(core, g4, r3)
    return out.reshape(nq, nl, r, c)


def _scatter_copies(ins, outs, send, recv):
    x, y, c = _place()
    chips = [(1 - x, y), (x, 1 - y), (1 - x, 1 - y)]
    return [pltpu.make_async_remote_copy(src_ref=ins[u].at[2 * chip[0] + chip[1]], dst_ref=outs[u].at[k],
                                         send_sem=send.at[u * 3 + k], recv_sem=recv.at[u * 3 + k],
                                         device_id=(*chip, c), device_id_type=MESH)
            for u in range(len(ins)) for k, chip in enumerate(chips)]


def _scatter_start(name, ss, carry):
    n = len(ss)
    lands = [lax.empty((3,) + s.shape[1:], s.dtype) for s in ss]
    held = list(ss) + lands + [carry]

    def body(*refs):
        for cp in _scatter_copies(refs[:n], refs[n:2 * n], refs[2 * n + 1], refs[2 * n + 2]):
            cp.start()

    sems = [pltpu.SemaphoreType.DMA((n * 3,))] * 2
    out = pl.pallas_call(
        body, name=name, in_specs=[HBM] * len(held), out_specs=[SEM, SEM] + [HBM] * len(held),
        out_shape=sems + [pltpu.HBM(a.shape, a.dtype) for a in held],
        input_output_aliases={i: 2 + i for i in range(len(held))},
        compiler_params=pltpu.CompilerParams(has_side_effects=EFFECT),
    )(*[pltpu.with_memory_space_constraint(a, pltpu.HBM) for a in held])
    return (out[0], out[1], out[2:2 + n], out[2 + n:2 + 2 * n]), out[2 + 2 * n]


def _scatter_wait(name, started, after):
    send, recv, srcs, lands = started
    n = len(srcs)

    def body(*refs):
        for cp in _scatter_copies(refs[:n], refs[n:2 * n], refs[2 * n], refs[2 * n + 1]):
            cp.wait_send()
            cp.wait_recv()

    out = pl.pallas_call(
        body, name=name, in_specs=[HBM] * (2 * n) + [SEM, SEM, ANY], out_specs=[HBM] * (2 * n),
        out_shape=[pltpu.HBM(a.shape, a.dtype) for a in list(srcs) + list(lands)],
        input_output_aliases={i: i for i in range(2 * n)},
        compiler_params=pltpu.CompilerParams(has_side_effects=EFFECT),
    )(*srcs, *lands, send, recv, after)
    return out[:n], out[n:]


def _sum_blocks(name, blocks):
    def body(b_ref, o_ref):
        acc = b_ref[0]
        for d in range(1, NDEV):
            acc = acc + b_ref[d]
        o_ref[...] = acc

    return pl.pallas_call(body, name=name, out_shape=S(blocks.shape[1:], f32),
                          compiler_params=pltpu.CompilerParams(vmem_limit_bytes=VMEM_LIMIT))(blocks)


def _adam_math(w, g, m, v):
    m = ADAM_B1 * m + (1.0 - ADAM_B1) * g
    v = ADAM_B2 * v + (1.0 - ADAM_B2) * (g * g)
    m_hat = m / (1.0 - ADAM_B1 ** ADAM_STEP)
    v_hat = v / (1.0 - ADAM_B2 ** ADAM_STEP)
    delta = -ADAM_LR * (m_hat / (jnp.sqrt(v_hat) + ADAM_EPS) + ADAM_WD * w)
    return delta, m, v


def _adamw(name, chip, w, m, v, parts):
    _, r, c = w.shape
    tr = _row_tile(r, c)

    def body(chip_ref, w_ref, m_ref, v_ref, *refs):
        sets, (g_ref, d_ref, mo_ref, vo_ref) = (refs[0:4], refs[4:8]), refs[8:]
        for l in range(2):
            @pl.when(pl.program_id(0) == l)
            def _():
                s_ref, r0_ref, r1_ref, r2_ref = sets[l]
                g = ((s_ref[...].astype(f32) + r0_ref[...].astype(f32)) + r1_ref[...].astype(f32)) + r2_ref[...].astype(f32)
                g_ref[...] = g
                d_ref[...], mo_ref[...], vo_ref[...] = _adam_math(w_ref[...], g, m_ref[...], v_ref[...])

    blk = _bs((None, tr, c), lambda l, i, cr: (l, i, 0))
    operands, specs = [], []
    for n in range(2):
        row = (lambda l, i: i * (1 - l)) if n == 0 else (lambda l, i: i * l)
        sums, r2 = parts[n]
        operands += [sums, r2, r2, r2]
        specs.append(_bs((None, None, tr, c), functools.partial(lambda l, i, cr, row: (cr[0], 0, row(l, i), 0), row=row)))
        specs += [_bs((None, None, tr, c), functools.partial(lambda l, i, cr, k, row: (k, 0, row(l, i), 0), k=k, row=row))
                  for k in range(3)]
    return pl.pallas_call(
        body, name=name,
        grid_spec=pltpu.PrefetchScalarGridSpec(num_scalar_prefetch=1, grid=(2, r // tr), in_specs=[blk, blk, blk] + specs,
                                               out_specs=[blk] * 4),
        out_shape=[S(w.shape, f32)] * 4, compiler_params=_cp("arbitrary", "arbitrary"),
    )(chip, w, m, v, *operands)


def _adamw_small(name, w, g, m, v):
    def body(w_ref, g_ref, m_ref, v_ref, d_ref, mo_ref, vo_ref):
        d_ref[...], mo_ref[...], vo_ref[...] = _adam_math(w_ref[...], g_ref[...], m_ref[...], v_ref[...])

    return pl.pallas_call(body, name=name, out_shape=[S(w.shape, f32)] * 3,
                          compiler_params=pltpu.CompilerParams(vmem_limit_bytes=VMEM_LIMIT))(w, g, m, v)


WEIGHT_NAMES = ("ffn1_norm", "ffn1_w_gu", "ffn1_w_down", "mix_norm", "w_in", "b_forget", "b_gate", "conv_w", "sgu_ln_g",
                "sgu_ln_b", "sgu_w", "sgu_b", "q_norm_g", "k_norm_g", "w_out_conv", "w_out_sgu", "w_out_attn", "w_o",
                "ffn2_norm", "ffn2_w_gu", "ffn2_w_down")
BIG = {"ffn1_w_gu": "gu1", "ffn2_w_gu": "gu2", "ffn1_w_down": "d1", "ffn2_w_down": "d2", "w_in": "in",
       "w_out_conv": "oc", "w_out_sgu": "os", "w_out_attn": "oa", "w_o": "o"}
BIG_KEYS = ("gu1", "gu2", "d1", "d2", "in", "oc", "os", "oa", "o")
REPLICATED_SMALL = ("ffn1_norm", "mix_norm", "b_forget", "sgu_ln_g", "sgu_ln_b", "sgu_w", "sgu_b", "q_norm_g",
                    "k_norm_g", "ffn2_norm")
SHARDED_SMALL = ("b_gate", "conv_w")
TRANSPOSED = ("gu1", "gu2")


def _packed_rows(shape):
    size = 1
    for s_ in shape:
        size *= s_
    return size, -(-size // 1024) * 8


def _pack(arrays):
    pieces = []
    for a in arrays:
        size, rows = _packed_rows(a.shape)
        pieces.append(jnp.pad(a.reshape(-1).astype(f32), (0, rows * 128 - size)).reshape(rows, 128))
    return jnp.concatenate(pieces, axis=0)


def _unpack(packed, shapes):
    out, pos = [], 0
    for shp in shapes:
        size, rows = _packed_rows(shp)
        out.append(packed[pos:pos + rows].reshape(-1)[:size].reshape(shp))
        pos += rows
    return out


def _natural_runs(a, b):
    runs = []
    while a < b:
        d = a // INB
        e = min(b, (d + 1) * INB)
        runs.append((d, a - d * INB, e - d * INB))
        a = e
    return runs


def _win_runs():
    runs = _natural_runs(0, GATE_OFF) + _natural_runs(GATE_OFF + NH, NIN) + _natural_runs(GATE_OFF, GATE_OFF + NH)
    out, pos = [], 0
    for d, a, b in runs:
        out.append((d, a, b, pos))
        pos += b - a
    return out


RELAYOUT_ROWS = 128


def _win_kernel_layout(name, wg):
    def body(w_ref, o_ref):
        for d, a, b, pos in _win_runs():
            o_ref[:, pos:pos + (b - a)] = w_ref[d, :, a:b]
        o_ref[:, NIN:NZ] = jnp.zeros((RELAYOUT_ROWS, NZ - NIN), wg.dtype)

    return pl.pallas_call(
        body, name=name, grid=(D // RELAYOUT_ROWS,), in_specs=[_bs((NDEV, RELAYOUT_ROWS, INB), lambda i: (0, i, 0))],
        out_specs=_bs((RELAYOUT_ROWS, NZ), lambda i: (i, 0)), out_shape=S((D, NZ), wg.dtype),
        compiler_params=_cp("parallel"))(wg)


def _win_device_blocks(name, dw):
    def body(dw_ref, o_ref):
        for d, a, b, pos in _win_runs():
            o_ref[d % 2, d // 2, :, a:b] = dw_ref[:, pos:pos + (b - a)]

    out = pl.pallas_call(
        body, name=name, grid=(D // RELAYOUT_ROWS,), in_specs=[_bs((RELAYOUT_ROWS, NZ), lambda i: (i, 0))],
        out_specs=_bs((2, 4, RELAYOUT_ROWS, INB), lambda i: (0, 0, i, 0)), out_shape=S((2, 4, D, INB), dw.dtype),
        compiler_params=_cp("parallel"))(dw)
    return out.reshape(2, 4, 1, D, INB)


def kernel(x, ffn1_norm, ffn1_w_gu, ffn1_w_down, mix_norm, w_in, b_forget, b_gate, conv_w, sgu_ln_g, sgu_ln_b, sgu_w, sgu_b, q_norm_g, k_norm_g, w_out_conv, w_out_sgu, w_out_attn, w_o, ffn2_norm, ffn2_w_gu, ffn2_w_down, loss_target, m_ffn1_norm, m_ffn1_w_gu, m_ffn1_w_down, m_mix_norm, m_w_in, m_b_forget, m_b_gate, m_conv_w, m_sgu_ln_g, m_sgu_ln_b, m_sgu_w, m_sgu_b, m_q_norm_g, m_k_norm_g, m_w_out_conv, m_w_out_sgu, m_w_out_attn, m_w_o, m_ffn2_norm, m_ffn2_w_gu, m_ffn2_w_down, v_ffn1_norm, v_ffn1_w_gu, v_ffn1_w_down, v_mix_norm, v_w_in, v_b_forget, v_b_gate, v_conv_w, v_sgu_ln_g, v_sgu_ln_b, v_sgu_w, v_sgu_b, v_q_norm_g, v_k_norm_g, v_w_out_conv, v_w_out_sgu, v_w_out_attn, v_w_o, v_ffn2_norm, v_ffn2_w_gu, v_ffn2_w_down):
    w = dict(zip(WEIGHT_NAMES, (ffn1_norm, ffn1_w_gu, ffn1_w_down, mix_norm, w_in, b_forget, b_gate, conv_w, sgu_ln_g,
                                sgu_ln_b, sgu_w, sgu_b, q_norm_g, k_norm_g, w_out_conv, w_out_sgu, w_out_attn, w_o,
                                ffn2_norm, ffn2_w_gu, ffn2_w_down)))
    mom = dict(zip(WEIGHT_NAMES, (m_ffn1_norm, m_ffn1_w_gu, m_ffn1_w_down, m_mix_norm, m_w_in, m_b_forget, m_b_gate,
                                  m_conv_w, m_sgu_ln_g, m_sgu_ln_b, m_sgu_w, m_sgu_b, m_q_norm_g, m_k_norm_g,
                                  m_w_out_conv, m_w_out_sgu, m_w_out_attn, m_w_o, m_ffn2_norm, m_ffn2_w_gu,
                                  m_ffn2_w_down)))
    var = dict(zip(WEIGHT_NAMES, (v_ffn1_norm, v_ffn1_w_gu, v_ffn1_w_down, v_mix_norm, v_w_in, v_b_forget, v_b_gate,
                                  v_conv_w, v_sgu_ln_g, v_sgu_ln_b, v_sgu_w, v_sgu_b, v_q_norm_g, v_k_norm_g,
                                  v_w_out_conv, v_w_out_sgu, v_w_out_attn, v_w_o, v_ffn2_norm, v_ffn2_w_gu,
                                  v_ffn2_w_down)))
    px, py, pc = _place()
    dev = 4 * px + 2 * py + pc
    chip = 2 * px + py

    big_names = [n for n in WEIGHT_NAMES if n in BIG]
    key_name = {BIG[n]: n for n in big_names}
    group_keys = (("gu1", "d1"), ("in", "oc", "os", "oa", "o", "small"), ("gu2", "d2"))

    def source(key, l):
        if key == "small":
            return jnp.concatenate([w["b_gate"][l], w["conv_w"][l], jnp.zeros((2, 128), f32)], axis=0)
        block = w[key_name[key]][l]
        return (block.T if key in TRANSPOSED else block).astype(bf16)

    def landing(src):
        return lax.dynamic_update_slice(lax.empty((NDEV,) + src.shape, src.dtype), src[None], (dev, 0, 0))

    groups = [[(s, landing(s)) for s in (source(k, l) for k in keys)] for l in range(2) for keys in group_keys]
    (send0, recv0, srcs0, lands0), = _gather_start("gather_start_first", groups[:1])
    rest, src0 = _gather_start("gather_start_rest", groups[1:], carry=srcs0[0])
    started = [(send0, recv0, [src0] + list(srcs0[1:]), lands0)] + rest

    early = {}

    order = REPLICATED_SMALL + SHARDED_SMALL
    wp, mp, vp = (_pack([t[n] for n in order]) for t in (w, mom, var))
    while_idle = [mom["w_in"], var["w_in"], wp, mp, vp]

    def prefetch(l, part, act):
        got = _gather_wait(f"gather_wait_{l}_{part}", started[3 * l + part], act)
        early[l, part], act = _forward_start(f"forward_start_{l}_{part}", got, act)
        return act

    def weights(l, part, after):
        if (l, part) in early:
            lands = _forward_wait(f"forward_wait_{l}_{part}", early.pop((l, part)), after)
        else:
            got = _gather_wait(f"gather_wait_{l}_{part}", started[3 * l + part], after)
            lands = _gather_forward(f"gather_forward_{l}_{part}", got)
        return dict(zip(group_keys[part], lands))

    xl = x[0]
    saved, small, wts = [], [], []
    for l in range(2):
        ga = weights(l, 0, xl if l else None)
        if l:
            xl = prefetch(l, 1, xl)
        wt = {"gu1": ga["gu1"][None], "d1": ga["d1"].reshape(1, 4, GU, D)}
        x1, s1 = _ffn_fwd("ffn1", xl, w["ffn1_norm"][l].reshape(1, D), wt["gu1"], wt["d1"], 0)
        gb = weights(l, 1, x1 if l else [x1] + while_idle)
        if l:
            x1 = prefetch(l, 2, x1)
        p = {n: w[n][l] for n in REPLICATED_SMALL}
        p["b_gate"] = jnp.transpose(gb["small"][:, 0:3, :], (1, 0, 2)).reshape(3, D)
        p["conv_w"] = jnp.transpose(gb["small"][:, 3:6, :], (1, 0, 2)).reshape(3, D)
        p = _small_params(p)
        wt["win"] = _win_kernel_layout("win_layout", gb["in"])
        wt["sq"] = jnp.stack([gb[k].reshape(D, D) for k in ("oc", "os", "oa", "o")])[None]
        x2, sm = _mixer_fwd("mix", x1, p, wt["win"], wt["sq"], 0, None if l else functools.partial(prefetch, 0, 2))
        gc = weights(l, 2, x2)
        if l == 0:
            x2 = prefetch(1, 0, x2)
        wt.update({"gu2": gc["gu2"][None], "d2": gc["d2"].reshape(1, 4, GU, D)})
        x3, s2 = _ffn_fwd("ffn2", x2, p["ffn2_norm"], wt["gu2"], wt["d2"], 0)
        saved.append((xl, x1, x2, s1, sm, s2))
        small.append(p)
        wts.append(wt)
        xl = x3
    loss_row, dx = _loss("loss", xl, loss_target[0])

    core = pc.reshape(1).astype(jnp.int32)
    chip_op = chip.reshape(1).astype(jnp.int32)
    buf = lambda r, c: lax.empty((2, 4, 1, r, c), bf16)
    flights = {}

    pairs = []
    scatter_keys = {"ffn2": ("gu2", "d2"), "sq": ("oc", "os", "oa", "o"), "in": ("in",), "ffn1": ("gu1", "d1")}

    def ship(l, part, bufs, dx):
        started, dx = _pair_start(f"pair_start_{l}_{part}", bufs, dx)
        pairs.append((l, part, started))
        return dx

    def finish(dx, after):
        l, part, started = pairs.pop()
        bufs, r1 = _pair_wait(f"pair_wait_{l}_{part}", started, after)
        ss = [_pair_sum(f"pair_sum_{l}_{k}", core, g, r) for k, g, r in zip(scatter_keys[part], bufs, r1)]
        flights[l, part], dx = _scatter_start(f"scatter_start_{l}_{part}", ss, dx)
        return dx

    def ship_in(l, dwin, dx):
        return ship(l, "in", [_win_device_blocks("dwin_blocks", dwin)], dx)

    sgrads = [None, None]
    for l in (1, 0):
        p, wt = small[l], wts[l]
        x0, x1, x2, s1, sm, s2 = saved[l]
        dx, dn2 = _ffn_bwd("ffn2", dx, x2, p["ffn2_norm"], wt["gu2"], wt["d2"], 0, s2, buf(GU, D), buf(GU // 2, D),
                           functools.partial(ship, l, "ffn2"), finish)
        dx, sg = _mixer_bwd("mix", dx, x1, p, wt["win"], wt["sq"], 0, sm, [buf(128, D) for _ in range(4)],
                            functools.partial(ship, l, "sq"), functools.partial(ship_in, l), finish)
        dx, dn1 = _ffn_bwd("ffn1", dx, x0, p["ffn1_norm"], wt["gu1"], wt["d1"], 0, s1, buf(GU, D), buf(GU // 2, D),
                           functools.partial(ship, l, "ffn1"), finish, hook_at_once=(l == 0))
        sg["ffn1_norm"] = dn1
        sg["ffn2_norm"] = dn2
        sgrads[l] = sg

    nat = [_small_grads_natural(sgrads[l]) for l in range(2)]
    part = _pack([jnp.stack([nat[0][n], nat[1][n]]) for n in order] + [loss_row[0, 0:1]])
    small_flight, dx = _gather_start("small_start", [[(part, landing(part))]], carry=dx)

    grads, delta, new_m, new_v = {}, {}, {}, {}
    after = dx
    for part in ("ffn2", "sq", "in", "ffn1"):
        sets = []
        for l in (1, 0):
            s_all, r2_all = _scatter_wait(f"scatter_wait_{l}_{part}", flights[l, part], after)
            sets.append(list(zip(s_all, r2_all)))
        for i, k in enumerate(scatter_keys[part]):
            n = key_name[k]
            view = (lambda a: jnp.swapaxes(a, 1, 2)) if k in TRANSPOSED else (lambda a: a)
            outs = _adamw("adamw_" + k, chip_op, view(w[n]), view(mom[n]), view(var[n]), [sets[1][i], sets[0][i]])
            grads[n], delta[n], new_m[n], new_v[n] = [view(o) for o in outs]
            after = outs[1]

    blocks = _gather_forward("small_forward", _gather_wait("small_wait", small_flight[0], after))
    total = _sum_blocks("small_sum", blocks[0])
    full_shapes = [(2,) + tuple(nat[0][n].shape) for n in order] + [(1,)]
    summed = dict(zip(order + ("loss",), _unpack(total, full_shapes)))
    for n in REPLICATED_SMALL:
        grads[n] = summed[n]
    for n in SHARDED_SMALL:
        grads[n] = lax.dynamic_slice_in_dim(summed[n], dev * 128, 128, axis=2)
    gp = _pack([grads[n] for n in order])
    dpk, mpk, vpk = _adamw_small("adamw_small", wp, gp, mp, vp)
    local_shapes = [tuple(w[n].shape) for n in order]
    for dst, packed in ((delta, dpk), (new_m, mpk), (new_v, vpk)):
        dst.update(zip(order, _unpack(packed, local_shapes)))

    loss = summed["loss"][0]
    return (loss, dx[None], *[grads[n] for n in WEIGHT_NAMES], *[delta[n] for n in WEIGHT_NAMES],
            *[new_m[n] for n in WEIGHT_NAMES], *[new_v[n] for n in WEIGHT_NAMES])
```

```python
import functools

import jax
import jax.numpy as jnp
from jax import lax
from jax.experimental import pallas as pl
from jax.experimental.pallas import tpu as pltpu

f32 = jnp.float32
bf16 = jnp.bfloat16
S = jax.ShapeDtypeStruct
MESH = pl.DeviceIdType.MESH

D = 1024
NH = 8
HD = 128
NDEV = 8
GU = 704
NIN = 11272
INB = 1409
GATE_OFF = 8192
F_OFF = 11264
NZ = 11776
RMS_EPS = 1e-6
LN_EPS = 1e-5
ATT_SCALE = HD ** -0.5
NEG = -1e30
INV_SQRT2 = 0.7071067811865476
INV_SQRT2PI = 0.3989422804014327

ADAM_LR = 0.001
ADAM_B1 = 0.9
ADAM_B2 = 0.999
ADAM_EPS = 1e-08
ADAM_WD = 0.01
ADAM_STEP = 10

TT = 512
VMEM_LIMIT = 56 * 1024 * 1024


def _cp(*sem):
    return pltpu.CompilerParams(dimension_semantics=sem, vmem_limit_bytes=VMEM_LIMIT)


def _bs(shape, fn):
    return pl.BlockSpec(shape, fn)


NN = (((1,), (0,)), ((), ()))
NT = (((1,), (1,)), ((), ()))
TN = (((0,), (0,)), ((), ()))


def _mm(name, a, b, *, grid, a_spec, b_spec, out_shape, out_spec, dims, acc_shape, res=None, res_spec=None,
        alpha=1.0, alias=None, split_rows=None):
    nk = grid[2]

    def body(*refs):
        a_ref, b_ref = refs[0], refs[1]
        pos = 2
        res_ref = None
        if res is not None:
            res_ref = refs[pos]
            pos += 1
        if alias is not None:
            pos += 1
        o_ref = refs[pos]
        acc_ref = refs[pos + 1] if nk > 1 else None
        part = lax.dot_general(a_ref[...].astype(bf16), b_ref[...].astype(bf16), dims, preferred_element_type=f32)

        def finish(acc):
            if alpha != 1.0:
                acc = alpha * acc
            if res_ref is not None:
                acc = res_ref[...] + acc
            if split_rows is None:
                o_ref[...] = acc.astype(o_ref.dtype)
            else:
                o_ref[0] = acc[:split_rows].astype(o_ref.dtype)
                o_ref[1] = acc[split_rows:].astype(o_ref.dtype)

        if nk == 1:
            finish(part)
        else:
            k = pl.program_id(2)

            @pl.when(k == 0)
            def _():
                acc_ref[...] = part

            @pl.when(k > 0)
            def _():
                acc_ref[...] += part

            @pl.when(k == nk - 1)
            def _():
                finish(acc_ref[...])

    operands = [a, b]
    in_specs = [a_spec, b_spec]
    if res is not None:
        operands.append(res)
        in_specs.append(res_spec)
    aliases = {}
    if alias is not None:
        aliases = {len(operands): 0}
        operands.append(alias)
        in_specs.append(pl.BlockSpec(memory_space=pl.ANY))
    return pl.pallas_call(
        body, name=name, grid=grid, in_specs=in_specs, out_specs=out_spec, out_shape=out_shape,
        scratch_shapes=[pltpu.VMEM(acc_shape, f32)] if nk > 1 else [],
        input_output_aliases=aliases,
        compiler_params=_cp("parallel", "parallel", "arbitrary"),
    )(*operands)


def _tile(n, t):
    return t if n % t == 0 and n >= t else n


DZ_TILE = 512


def _dz_matmul(name, pieces, other, weight_grad):
    T = pieces[0].shape[0]
    counts = [p.shape[1] // DZ_TILE for p in pieces]
    starts = [sum(counts[:i]) for i in range(len(counts))]
    steps = sum(counts)
    npc = len(pieces)

    def body(*refs):
        prefs, o_ref, rest = refs[:npc], refs[npc], refs[npc + 1:]
        k = pl.program_id(0)
        if not weight_grad:
            out_ref, acc_ref = rest

            @pl.when(k == 0)
            def _():
                acc_ref[...] = jnp.zeros_like(acc_ref)

        for p_ref, s, c in zip(prefs, starts, counts):
            @pl.when((k >= s) & (k < s + c))
            def _():
                if weight_grad:
                    rest[0][...] = lax.dot_general(o_ref[...], p_ref[...], TN, preferred_element_type=f32).astype(bf16)
                else:
                    acc_ref[...] += lax.dot_general(p_ref[...], o_ref[...], NT, preferred_element_type=f32)

        if not weight_grad:
            @pl.when(k == steps - 1)
            def _():
                out_ref[...] = acc_ref[...]

    piece_specs = [_bs((T, DZ_TILE), functools.partial(lambda k, s, c: (0, jnp.clip(k - s, 0, c - 1)), s=s, c=c))
                   for s, c in zip(starts, counts)]
    if weight_grad:
        other_spec, out_spec, out_shape, scratch = _bs((T, D), lambda k: (0, 0)), _bs((D, DZ_TILE), lambda k: (0, k)), S((D, NZ), bf16), []
    else:
        other_spec, out_spec, out_shape = _bs((D, DZ_TILE), lambda k: (0, k)), _bs((T, D), lambda k: (0, 0)), S((T, D), f32)
        scratch = [pltpu.VMEM((T, D), f32)]
    return pl.pallas_call(body, name=name, grid=(steps,), in_specs=piece_specs + [other_spec], out_specs=out_spec,
                          out_shape=out_shape, scratch_shapes=scratch, compiler_params=_cp("arbitrary"))(*pieces, other)


def _row(cb=0, w=D):
    return _bs((TT, w), lambda i: (i, cb))


def _vec(rows=1, w=D):
    return _bs((rows, w), lambda i: (0, 0))


def _acc_store(i, ref, val):
    @pl.when(i == 0)
    def _():
        ref[...] = val

    @pl.when(i > 0)
    def _():
        ref[...] += val


def _rms_fwd(name, x, g):
    T = x.shape[0]

    def body(x_ref, g_ref, o_ref):
        xv = x_ref[...]
        r = lax.rsqrt(jnp.mean(xv * xv, axis=-1, keepdims=True) + RMS_EPS)
        o_ref[...] = (xv * r * g_ref[...]).astype(bf16)

    return pl.pallas_call(body, name=name, grid=(T // TT,), in_specs=[_row(), _vec()], out_specs=_row(),
                          out_shape=S((T, D), bf16), compiler_params=_cp("parallel"))(x, g)


def _rms_bwd(name, dh, x, g, dres):
    T = x.shape[0]

    def body(dh_ref, x_ref, g_ref, dres_ref, dx_ref, dg_ref):
        i = pl.program_id(0)
        xv = x_ref[...]
        r = lax.rsqrt(jnp.mean(xv * xv, axis=-1, keepdims=True) + RMS_EPS)
        xhat = xv * r
        dh_v = dh_ref[...]
        dyg = dh_v * g_ref[...]
        m = jnp.mean(dyg * xhat, axis=-1, keepdims=True)
        dx_ref[...] = dres_ref[...] + r * (dyg - xhat * m)
        _acc_store(i, dg_ref, jnp.sum(dh_v * xhat, axis=0, keepdims=True))

    return pl.pallas_call(body, name=name, grid=(T // TT,), in_specs=[_row(), _row(), _vec(), _row()],
                          out_specs=[_row(), _vec()], out_shape=[S((T, D), f32), S((1, D), f32)],
                          compiler_params=_cp("arbitrary"))(dh, x, g, dres)


def _sigmoid(x):
    return 1.0 / (1.0 + jnp.exp(-x))


def _swiglu_fwd(name, h, wgu, l):
    T = h.shape[0]

    def body(h_ref, wg_ref, wu_ref, gu_ref, a_ref):
        hv = h_ref[...]
        g = lax.dot_general(hv, wg_ref[...], NT, preferred_element_type=f32)
        u = lax.dot_general(hv, wu_ref[...], NT, preferred_element_type=f32)
        gu_ref[0] = g
        gu_ref[1] = u
        a_ref[...] = (g * _sigmoid(g) * u).astype(bf16)

    return pl.pallas_call(
        body, name=name, grid=(4,),
        in_specs=[_bs((T, D), lambda j: (0, 0)), _bs((None, None, GU, D), lambda j: (l, j, 0, 0)),
                  _bs((None, None, GU, D), lambda j: (l, j + 4, 0, 0))],
        out_specs=[_bs((2, None, T, GU), lambda j: (0, j, 0, 0)), _bs((None, T, GU), lambda j: (j, 0, 0))],
        out_shape=[S((2, 4, T, GU), f32), S((4, T, GU), bf16)], compiler_params=_cp("parallel"))(h, wgu, wgu)


def _swiglu_bwd(name, dxo, wd, gu, l):
    T = dxo.shape[0]
    tm = _tile(T, 1024)

    def body(dx_ref, wd_ref, g_ref, u_ref, o_ref):
        da = 0.5 * lax.dot_general(dx_ref[...].astype(bf16), wd_ref[...], NT, preferred_element_type=f32)
        g = g_ref[...]
        sg = _sigmoid(g)
        o_ref[0] = (da * u_ref[...] * (sg + g * sg * (1.0 - sg))).astype(bf16)
        o_ref[1] = (da * g * sg).astype(bf16)

    return pl.pallas_call(
        body, name=name, grid=(T // tm, 4),
        in_specs=[_bs((tm, D), lambda i, j: (i, 0)), _bs((None, None, GU, D), lambda i, j: (l, j, 0, 0)),
                  _bs((None, None, tm, GU), lambda i, j: (0, j, i, 0)), _bs((None, None, tm, GU), lambda i, j: (1, j, i, 0))],
        out_specs=_bs((2, None, tm, GU), lambda i, j: (0, j, i, 0)), out_shape=S((2, 4, T, GU), bf16),
        compiler_params=_cp("parallel", "parallel"))(dxo, wd, gu, gu)


def _loss(name, y, tgt):
    T = y.shape[0]

    def body(y_ref, t_ref, l_ref, dy_ref):
        i = pl.program_id(0)
        e = y_ref[...] - t_ref[...]
        dy_ref[...] = e * (1.0 / D)
        s = 0.5 * jnp.sum(jnp.mean(e * e, axis=-1, keepdims=True))
        _acc_store(i, l_ref, jnp.broadcast_to(s, (1, 128)))

    return pl.pallas_call(body, name=name, grid=(T // TT,), in_specs=[_row(), _row()],
                          out_specs=[_vec(1, 128), _row()], out_shape=[S((1, 128), f32), S((T, D), f32)],
                          compiler_params=_cp("arbitrary"))(y, tgt)


def _prev8(T, cb):
    return _bs((8, D), lambda i: (jnp.maximum(i * (TT // 8) - 1, 0), cb))


def _next8(T, cb):
    return _bs((8, D), lambda i: (jnp.minimum((i + 1) * (TT // 8), T // 8 - 1), cb))


def _conv_taps(i, ac_ref, ax_ref, pc_ref, px_ref):
    ca = ac_ref[...] * ax_ref[...]
    keep = (i > 0).astype(f32)
    p1 = pc_ref[7:8, :] * px_ref[7:8, :] * keep
    p2 = pc_ref[6:7, :] * px_ref[6:7, :] * keep
    row = lax.broadcasted_iota(jnp.int32, ca.shape, 0)
    s1 = jnp.where(row == 0, p1, pltpu.roll(ca, 1, 0))
    s2 = jnp.where(row == 0, p2, jnp.where(row == 1, p1, pltpu.roll(ca, 2, 0)))
    return ca, s1, s2


def _conv_fwd(name, z, cw):
    T = z.shape[0]

    def body(ab_ref, ac_ref, ax_ref, pc_ref, px_ref, w_ref, o_ref):
        i = pl.program_id(0)
        ca, s1, s2 = _conv_taps(i, ac_ref, ax_ref, pc_ref, px_ref)
        cv = w_ref[0:1, :] * s2 + w_ref[1:2, :] * s1 + w_ref[2:3, :] * ca
        o_ref[...] = (ab_ref[...] * cv).astype(bf16)

    return pl.pallas_call(
        body, name=name, grid=(T // TT,),
        in_specs=[_row(0), _row(1), _row(2), _prev8(T, 1), _prev8(T, 2), _vec(3)],
        out_specs=_row(), out_shape=S((T, D), bf16), compiler_params=_cp("parallel"))(z, z, z, z, z, cw)


def _conv_bwd(name, dya, z, cw):
    T = z.shape[0]
    n = T // TT

    def body(dya_ref, ab_ref, ac_ref, ax_ref, pc_ref, px_ref, ndya_ref, nab_ref, w_ref, dz_ref, dw_ref):
        i = pl.program_id(0)
        ca, s1, s2 = _conv_taps(i, ac_ref, ax_ref, pc_ref, px_ref)
        w0, w1, w2 = w_ref[0:1, :], w_ref[1:2, :], w_ref[2:3, :]
        cv = w0 * s2 + w1 * s1 + w2 * ca
        dya_v = dya_ref[...]
        ab = ab_ref[...]
        dcv = dya_v * ab
        keep = (i < n - 1).astype(f32)
        n1 = ndya_ref[0:1, :] * nab_ref[0:1, :] * keep
        n2 = ndya_ref[1:2, :] * nab_ref[1:2, :] * keep
        row = lax.broadcasted_iota(jnp.int32, dcv.shape, 0)
        f1 = jnp.where(row == TT - 1, n1, pltpu.roll(dcv, TT - 1, 0))
        f2 = jnp.where(row == TT - 1, n2, jnp.where(row == TT - 2, n1, pltpu.roll(dcv, TT - 2, 0)))
        dca = w2 * dcv + w1 * f1 + w0 * f2
        dz_ref[:, 0:D] = (dya_v * cv).astype(bf16)
        dz_ref[:, D:2 * D] = (dca * ax_ref[...]).astype(bf16)
        dz_ref[:, 2 * D:3 * D] = (dca * ac_ref[...]).astype(bf16)
        dw = jnp.concatenate([jnp.sum(dcv * s2, axis=0, keepdims=True), jnp.sum(dcv * s1, axis=0, keepdims=True),
                              jnp.sum(dcv * ca, axis=0, keepdims=True)], axis=0)
        _acc_store(i, dw_ref, dw)

    return pl.pallas_call(
        body, name=name, grid=(n,),
        in_specs=[_row(), _row(0), _row(1), _row(2), _prev8(T, 1), _prev8(T, 2), _next8(T, 0), _next8(T, 0), _vec(3)],
        out_specs=[_row(0, 3 * D), _vec(3)], out_shape=[S((T, 3 * D), bf16), S((3, D), f32)],
        compiler_params=_cp("arbitrary"))(dya, z, z, z, z, z, dya, z, cw)


def _gelu(x):
    return 0.5 * x * (1.0 + lax.erf(x * INV_SQRT2))


def _gelu_cdf(x):
    return 0.5 * (1.0 + lax.erf(x * INV_SQRT2))


def _gelu_grad(x, cdf):
    return cdf + x * jnp.exp(-0.5 * x * x) * INV_SQRT2PI


def _ln_stats(vv):
    mu = jnp.mean(vv, axis=-1, keepdims=True)
    xc = vv - mu
    rstd = lax.rsqrt(jnp.mean(xc * xc, axis=-1, keepdims=True) + LN_EPS)
    return xc * rstd, rstd


def _tril_w(w_ref, g):
    r = lax.broadcasted_iota(jnp.int32, (HD, HD), 0)
    c = lax.broadcasted_iota(jnp.int32, (HD, HD), 1)
    return jnp.where(c <= r, w_ref[g], 0.0).astype(bf16)


def _sgu_fwd(name, z, ln_g, ln_b, w_s, bmap):
    T = z.shape[0]

    def body(su_ref, sv_ref, lg_ref, lb_ref, w_ref, bm_ref, o_ref, vn_ref):
        xhat, _ = _ln_stats(_gelu(sv_ref[...]))
        vn_ref[...] = (xhat * lg_ref[...] + lb_ref[...]).astype(bf16)
        for g in range(NH):
            w = _tril_w(w_ref, g)
            cs = slice(g * HD, (g + 1) * HD)
            for c in range(TT // HD):
                rs = slice(c * HD, (c + 1) * HD)
                s = jnp.dot(w, vn_ref[rs, cs], preferred_element_type=f32) + bm_ref[:, cs]
                o_ref[rs, cs] = (_gelu(su_ref[rs, cs]) * s).astype(bf16)

    return pl.pallas_call(
        body, name=name, grid=(T // TT,),
        in_specs=[_row(3), _row(4), _vec(), _vec(), _bs((NH, HD, HD), lambda i: (0, 0, 0)), _vec(HD)],
        out_specs=_row(), out_shape=S((T, D), bf16), scratch_shapes=[pltpu.VMEM((TT, D), bf16)],
        compiler_params=_cp("parallel"))(z, z, ln_g, ln_b, w_s, bmap)


def _sgu_bwd(name, dyb, z, ln_g, ln_b, w_s, bmap):
    T = z.shape[0]

    def body(dyb_ref, su_ref, sv_ref, lg_ref, lb_ref, w_ref, bm_ref, dz_ref, dlg_ref, dlb_ref, dw_ref, db_ref,
             vn_ref, du_ref, dvn_ref, cu_ref, cv_ref):
        i = pl.program_id(0)
        sv = sv_ref[...]
        cv_ref[...] = _gelu_cdf(sv)
        cu_ref[...] = _gelu_cdf(su_ref[...])
        xhat, rstd = _ln_stats(sv * cv_ref[...])
        vn_ref[...] = (xhat * lg_ref[...] + lb_ref[...]).astype(bf16)
        r = lax.broadcasted_iota(jnp.int32, (HD, HD), 0)
        cc = lax.broadcasted_iota(jnp.int32, (HD, HD), 1)
        for g in range(NH):
            w = _tril_w(w_ref, g)
            cs = slice(g * HD, (g + 1) * HD)
            dw = jnp.zeros((HD, HD), f32)
            db = jnp.zeros((HD, 1), f32)
            for c in range(TT // HD):
                rs = slice(c * HD, (c + 1) * HD)
                vnb = vn_ref[rs, cs]
                s = jnp.dot(w, vnb, preferred_element_type=f32) + bm_ref[:, cs]
                dy = dyb_ref[rs, cs]
                du_ref[rs, cs] = dy * s
                ds = dy * (su_ref[rs, cs] * cu_ref[rs, cs])
                ds16 = ds.astype(bf16)
                dvn_ref[rs, cs] = lax.dot_general(w, ds16, TN, preferred_element_type=f32)
                dw = dw + lax.dot_general(ds16, vnb, NT, preferred_element_type=f32)
                db = db + jnp.sum(ds, axis=1, keepdims=True)
            dw = jnp.where(cc <= r, dw, 0.0)

            @pl.when(i == 0)
            def _():
                dw_ref[g] = dw
                db_ref[:, g:g + 1] = db

            @pl.when(i > 0)
            def _():
                dw_ref[g] += dw
                db_ref[:, g:g + 1] += db

        dvn = dvn_ref[...]
        dxh = dvn * lg_ref[...]
        m1 = jnp.mean(dxh, axis=-1, keepdims=True)
        m2 = jnp.mean(dxh * xhat, axis=-1, keepdims=True)
        dvv = rstd * (dxh - m1 - xhat * m2)
        dz_ref[:, 0:D] = (du_ref[...] * _gelu_grad(su_ref[...], cu_ref[...])).astype(bf16)
        dz_ref[:, D:2 * D] = (dvv * _gelu_grad(sv, cv_ref[...])).astype(bf16)
        _acc_store(i, dlg_ref, jnp.sum(dvn * xhat, axis=0, keepdims=True))
        _acc_store(i, dlb_ref, jnp.sum(dvn, axis=0, keepdims=True))

    return pl.pallas_call(
        body, name=name, grid=(T // TT,),
        in_specs=[_row(), _row(3), _row(4), _vec(), _vec(), _bs((NH, HD, HD), lambda i: (0, 0, 0)), _vec(HD)],
        out_specs=[_row(0, 2 * D), _vec(), _vec(), _bs((NH, HD, HD), lambda i: (0, 0, 0)), _bs((HD, NH), lambda i: (0, 0))],
        out_shape=[S((T, 2 * D), bf16), S((1, D), f32), S((1, D), f32), S((NH, HD, HD), f32), S((HD, NH), f32)],
        scratch_shapes=[pltpu.VMEM((TT, D), bf16)] + [pltpu.VMEM((TT, D), f32)] * 4,
        compiler_params=_cp("arbitrary"))(dyb, z, z, ln_g, ln_b, w_s, bmap)


def _qk_fwd(name, z, qg, kg, bf):
    T = z.shape[0]

    def body(q_ref, k_ref, v_ref, zf_ref, qg_ref, kg_ref, bf_ref, qn_ref, kn_ref, vb_ref, lf_ref):
        for h in range(NH):
            cs = slice(h * HD, (h + 1) * HD)
            for src, gain, dst in ((q_ref, qg_ref, qn_ref), (k_ref, kg_ref, kn_ref)):
                xv = src[:, cs]
                r = lax.rsqrt(jnp.mean(xv * xv, axis=-1, keepdims=True) + RMS_EPS)
                dst[:, cs] = (xv * r * gain[:, cs]).astype(bf16)
        vb_ref[...] = v_ref[...].astype(bf16)
        xf = zf_ref[...] + bf_ref[...]
        lf_ref[...] = jnp.minimum(xf, 0.0) - jnp.log1p(jnp.exp(-jnp.abs(xf)))

    return pl.pallas_call(
        body, name=name, grid=(T // TT,),
        in_specs=[_row(5), _row(6), _row(7), _bs((TT, 128), lambda i: (i, F_OFF // 128)), _vec(), _vec(), _vec(1, 128)],
        out_specs=[_row(), _row(), _row(), _bs((TT, 128), lambda i: (i, 0))],
        out_shape=[S((T, D), bf16), S((T, D), bf16), S((T, D), bf16), S((T, 128), f32)],
        compiler_params=_cp("parallel"))(z, z, z, z, qg, kg, bf)


def _cum_fwd(name, logf):
    T = logf.shape[0]

    def body(lf_ref, ccol_ref, crow_ref, c_ref):
        c = lf_ref[...]
        row = lax.broadcasted_iota(jnp.int32, c.shape, 0)
        s = 1
        while s < T:
            c = c + jnp.where(row >= s, pltpu.roll(c, s, 0), 0.0)
            s *= 2
        c_ref[...] = c
        crow_ref[...] = c.T[0:NH, :]
        for h in range(NH):
            ccol_ref[h] = jnp.broadcast_to(c_ref[:, h:h + 1], (T, 128))

    return pl.pallas_call(body, name=name, out_shape=[S((NH, T, 128), f32), S((NH, T), f32)],
                          scratch_shapes=[pltpu.VMEM((T, 128), f32)],
                          compiler_params=pltpu.CompilerParams(vmem_limit_bytes=VMEM_LIMIT))(logf)


ATT_TILE = 1024


def _fold(x, op=jnp.add):
    acc = x[:, 0:128]
    for t in range(1, x.shape[1] // 128):
        acc = op(acc, x[:, t * 128:(t + 1) * 128])
    return acc


def _to_row(col):
    return jnp.broadcast_to(col, (col.shape[0], 128)).T[0:1, :]


def _causal(t, keys_down=False):
    r = lax.broadcasted_iota(jnp.int32, (t, t), 0)
    c = lax.broadcasted_iota(jnp.int32, (t, t), 1)
    return r <= c if keys_down else c <= r


def _attn_fwd(name, qn, kn, vb, ccol, crow3):
    T = qn.shape[0]
    tq = _tile(T, ATT_TILE)
    nq = T // tq

    def body(q_ref, k_ref, v_ref, cc_ref, cr_ref, o_ref, lse_ref, lser_ref, s_ref):
        qi = pl.program_id(1)
        q = q_ref[...]
        cq = cc_ref[:, 0:1]

        def logits(off):
            s = lax.dot_general(q, k_ref[pl.ds(off, tq), :], NT, preferred_element_type=f32) * ATT_SCALE
            return s + cq - cr_ref[:, pl.ds(off, tq)]

        def below(j, mvec):
            off = pl.multiple_of(j * tq, tq)
            s = logits(off)
            s_ref[:, pl.ds(off, tq)] = s
            return jnp.maximum(mvec, _fold(s, jnp.maximum))

        mvec = lax.fori_loop(0, qi, below, jnp.full((tq, 128), NEG, f32))
        off = pl.multiple_of(qi * tq, tq)
        s = jnp.where(_causal(tq), logits(off), NEG)
        s_ref[:, pl.ds(off, tq)] = s
        m = jnp.max(jnp.maximum(mvec, _fold(s, jnp.maximum)), axis=1, keepdims=True)

        def weigh(j, carry):
            lvec, acc = carry
            off = pl.multiple_of(j * tq, tq)
            p = jnp.exp(s_ref[:, pl.ds(off, tq)] - m)
            acc = acc + jnp.dot(p.astype(bf16), v_ref[pl.ds(off, tq), :], preferred_element_type=f32)
            return lvec + _fold(p), acc

        lvec, acc = lax.fori_loop(0, qi + 1, weigh, (jnp.zeros((tq, 128), f32), jnp.zeros((tq, HD), f32)))
        l = jnp.sum(lvec, axis=1, keepdims=True)
        o_ref[...] = acc / l
        lse = m + jnp.log(l)
        lse_ref[...] = jnp.broadcast_to(lse, (tq, 128))
        lser_ref[...] = _to_row(lse)

    return pl.pallas_call(
        body, name=name, grid=(NH, nq),
        in_specs=[_bs((tq, HD), lambda h, i: (i, h)), _bs((T, HD), lambda h, i: (0, h)), _bs((T, HD), lambda h, i: (0, h)),
                  _bs((None, tq, 128), lambda h, i: (h, i, 0)), _bs((None, 1, T), lambda h, i: (h, 0, 0))],
        out_specs=[_bs((tq, HD), lambda h, i: (i, h)), _bs((None, tq, 128), lambda h, i: (h, i, 0)),
                   _bs((None, 1, tq), lambda h, i: (h, 0, i))],
        out_shape=[S((T, D), f32), S((NH, T, 128), f32), S((NH, 1, T), f32)],
        scratch_shapes=[pltpu.VMEM((tq, T), f32)],
        compiler_params=_cp("parallel", "parallel"))(qn, kn, vb, ccol, crow3)


def _attn_dq(name, qn, kn, vb, do, lse, ccol, crow3):
    T = qn.shape[0]
    tq = _tile(T, ATT_TILE)
    nq = T // tq

    def body(q_ref, k_ref, v_ref, do_ref, lse_ref, cc_ref, cr_ref, dq_ref, dlr_ref, p_ref, dp_ref):
        qi = pl.program_id(1)
        q = q_ref[...]
        do16 = do_ref[...].astype(bf16)
        base = cc_ref[:, 0:1] - lse_ref[:, 0:1]

        def probs(off):
            s = lax.dot_general(q, k_ref[pl.ds(off, tq), :], NT, preferred_element_type=f32) * ATT_SCALE
            return jnp.exp(s + base - cr_ref[:, pl.ds(off, tq)])

        def keep(off, p, dvec):
            dp = lax.dot_general(do16, v_ref[pl.ds(off, tq), :], NT, preferred_element_type=f32)
            p_ref[:, pl.ds(off, tq)] = p
            dp_ref[:, pl.ds(off, tq)] = dp
            return dvec + _fold(p * dp)

        def below(j, dvec):
            off = pl.multiple_of(j * tq, tq)
            return keep(off, probs(off), dvec)

        dvec = lax.fori_loop(0, qi, below, jnp.zeros((tq, 128), f32))
        off = pl.multiple_of(qi * tq, tq)
        dvec = keep(off, jnp.where(_causal(tq), probs(off), 0.0), dvec)
        delta = jnp.sum(dvec, axis=1, keepdims=True)

        def grad(j, acc):
            off = pl.multiple_of(j * tq, tq)
            ds = p_ref[:, pl.ds(off, tq)] * (dp_ref[:, pl.ds(off, tq)] - delta)
            return acc + jnp.dot(ds.astype(bf16), k_ref[pl.ds(off, tq), :], preferred_element_type=f32)

        dq_ref[...] = lax.fori_loop(0, qi + 1, grad, jnp.zeros((tq, HD), f32)) * ATT_SCALE
        dlr_ref[...] = _to_row(delta)

    qb = lambda h, i: (i, h)
    full = lambda h, i: (0, h)
    col = lambda h, i: (h, i, 0)
    return pl.pallas_call(
        body, name=name, grid=(NH, nq),
        in_specs=[_bs((tq, HD), qb), _bs((T, HD), full), _bs((T, HD), full), _bs((tq, HD), qb),
                  _bs((None, tq, 128), col), _bs((None, tq, 128), col), _bs((None, 1, T), lambda h, i: (h, 0, 0))],
        out_specs=[_bs((tq, HD), qb), _bs((None, 1, tq), lambda h, i: (h, 0, i))],
        out_shape=[S((T, D), f32), S((NH, 1, T), f32)],
        scratch_shapes=[pltpu.VMEM((tq, T), f32), pltpu.VMEM((tq, T), f32)],
        compiler_params=_cp("parallel", "parallel"))(qn, kn, vb, do, lse, ccol, crow3)


def _attn_dkv(name, qn, kn, vb, do, lser3, dlr3, ccol, crow3):
    T = qn.shape[0]
    tk = _tile(T, ATT_TILE)
    nk = T // tk

    def body(q_ref, k_ref, v_ref, do_ref, lser_ref, dlr_ref, cc_ref, cr_ref, dk_ref, dv_ref, cs_ref):
        h = pl.program_id(0)
        kj = pl.program_id(1)

        @pl.when((h == 0) & (kj == 0))
        def _():
            cs_ref[...] = jnp.zeros_like(cs_ref)

        kb = k_ref[...]
        vv = v_ref[...]
        ckey = cc_ref[:, 0:1]

        def block(off, diagonal):
            rows = pl.ds(off, tk)
            qb = q_ref[rows, :]
            do16 = do_ref[rows, :].astype(bf16)
            st = lax.dot_general(kb, qb, NT, preferred_element_type=f32) * ATT_SCALE
            pt = jnp.exp(st + (cr_ref[:, rows] - lser_ref[:, rows]) - ckey)
            if diagonal:
                pt = jnp.where(_causal(tk, keys_down=True), pt, 0.0)
            dpt = lax.dot_general(vv, do16, NT, preferred_element_type=f32)
            dst = pt * (dpt - dlr_ref[:, rows])
            ddv = jnp.dot(pt.astype(bf16), do16, preferred_element_type=f32)
            ddk = jnp.dot(dst.astype(bf16), qb, preferred_element_type=f32)
            return ddk, ddv, _fold(dst)

        def above(i, carry):
            ddk, ddv, dcs = block(pl.multiple_of(i * tk, tk), False)
            return carry[0] + ddk, carry[1] + ddv, carry[2] + dcs

        off = pl.multiple_of(kj * tk, tk)
        dk, dv, cs = lax.fori_loop(kj + 1, nk, above, block(off, True))
        dk_ref[...] = dk * ATT_SCALE
        dv_ref[...] = dv
        lane = lax.broadcasted_iota(jnp.int32, (tk, 128), 1)
        cs_ref[pl.ds(off, tk), :] += jnp.where(lane == h, jnp.sum(cs, axis=1, keepdims=True), 0.0)

    full = lambda h, j: (0, h)
    blk = lambda h, j: (j, h)
    row = lambda h, j: (h, 0, 0)
    return pl.pallas_call(
        body, name=name, grid=(NH, nk),
        in_specs=[_bs((T, HD), full), _bs((tk, HD), blk), _bs((tk, HD), blk), _bs((T, HD), full), _bs((None, 1, T), row),
                  _bs((None, 1, T), row), _bs((None, tk, 128), lambda h, j: (h, j, 0)), _bs((None, 1, T), row)],
        out_specs=[_bs((tk, HD), blk), _bs((tk, HD), blk), _bs((T, 128), lambda h, j: (0, 0))],
        out_shape=[S((T, D), f32), S((T, D), f32), S((T, 128), f32)],
        compiler_params=_cp("arbitrary", "arbitrary"))(qn, kn, vb, do, lser3, dlr3, ccol, crow3)


def _forget_bwd(name, cs, z, bf):
    T = cs.shape[0]

    def body(cs_ref, zf_ref, bf_ref, dz_ref, db_ref):
        c = -cs_ref[...]
        row = lax.broadcasted_iota(jnp.int32, c.shape, 0)
        s = 1
        while s < T:
            c = c + jnp.where(row + s < T, pltpu.roll(c, T - s, 0), 0.0)
            s *= 2
        xf = zf_ref[...] + bf_ref[...]
        lane = lax.broadcasted_iota(jnp.int32, c.shape, 1)
        dxf = jnp.where(lane < NH, c / (1.0 + jnp.exp(xf)), 0.0)
        dz_ref[...] = jnp.zeros_like(dz_ref)
        dz_ref[:, 0:128] = dxf.astype(bf16)
        db_ref[...] = jnp.sum(dxf, axis=0, keepdims=True)

    return pl.pallas_call(
        body, name=name, grid=(1,),
        in_specs=[_bs((T, 128), lambda i: (0, 0)), _bs((T, 128), lambda i: (0, F_OFF // 128)), _vec(1, 128)],
        out_specs=[_bs((T, NZ - F_OFF), lambda i: (0, 0)), _vec(1, 128)],
        out_shape=[S((T, NZ - F_OFF), bf16), S((1, 128), f32)], compiler_params=_cp("arbitrary"))(cs, z, bf)


def _qk_bwd(name, dqn, dkn, dv, z, qg, kg):
    T = z.shape[0]

    def body(dq_ref, dk_ref, dv_ref, q_ref, k_ref, qg_ref, kg_ref, dz_ref, dqg_ref, dkg_ref, g_ref):
        i = pl.program_id(0)
        for n, (src, dsrc, gain, dgain) in enumerate(((q_ref, dq_ref, qg_ref, dqg_ref), (k_ref, dk_ref, kg_ref, dkg_ref))):
            for h in range(NH):
                cs = slice(h * HD, (h + 1) * HD)
                xv = src[:, cs]
                r = lax.rsqrt(jnp.mean(xv * xv, axis=-1, keepdims=True) + RMS_EPS)
                xhat = xv * r
                dy = dsrc[:, cs]
                dyg = dy * gain[:, cs]
                m = jnp.mean(dyg * xhat, axis=-1, keepdims=True)
                dz_ref[:, n * D + h * HD:n * D + (h + 1) * HD] = (r * (dyg - xhat * m)).astype(bf16)
                g_ref[:, cs] = jnp.sum(dy * xhat, axis=0, keepdims=True)
            _acc_store(i, dgain, g_ref[...])
        dz_ref[:, 2 * D:3 * D] = dv_ref[...].astype(bf16)

    return pl.pallas_call(
        body, name=name, grid=(T // TT,),
        in_specs=[_row(), _row(), _row(), _row(5), _row(6), _vec(), _vec()],
        out_specs=[_row(0, 3 * D), _vec(), _vec()], out_shape=[S((T, 3 * D), bf16), S((1, D), f32), S((1, D), f32)],
        scratch_shapes=[pltpu.VMEM((1, D), f32)], compiler_params=_cp("arbitrary"))(dqn, dkn, dv, z, z, qg, kg)


GB = GATE_OFF // D


def _merge_fwd(name, ya, yb, yc, z, bg):
    T = z.shape[0]

    def body(ya_ref, yb_ref, yc_ref, g0_ref, g1_ref, g2_ref, bg_ref, o_ref):
        acc = _sigmoid(g0_ref[...] + bg_ref[0:1, :]) * ya_ref[...]
        acc = acc + _sigmoid(g1_ref[...] + bg_ref[1:2, :]) * yb_ref[...]
        acc = acc + _sigmoid(g2_ref[...] + bg_ref[2:3, :]) * yc_ref[...]
        o_ref[...] = acc.astype(bf16)

    return pl.pallas_call(
        body, name=name, grid=(T // TT,),
        in_specs=[_row(), _row(), _row(), _row(GB), _row(GB + 1), _row(GB + 2), _vec(3)],
        out_specs=_row(), out_shape=S((T, D), bf16), compiler_params=_cp("parallel"))(ya, yb, yc, z, z, z, bg)


def _merge_bwd(name, dm, ya, yb, yc, z, bg):
    T = z.shape[0]

    def body(dm_ref, ya_ref, yb_ref, yc_ref, g0_ref, g1_ref, g2_ref, bg_ref, dya_ref, dyb_ref, dyc_ref, dz_ref, db_ref):
        i = pl.program_id(0)
        dm_v = dm_ref[...]
        dbs = []
        for n, (y_ref, g_ref, dy_ref) in enumerate(((ya_ref, g0_ref, dya_ref), (yb_ref, g1_ref, dyb_ref),
                                                    (yc_ref, g2_ref, dyc_ref))):
            gate = _sigmoid(g_ref[...] + bg_ref[n:n + 1, :])
            dy_ref[...] = (dm_v * gate).astype(bf16)
            dl = dm_v * y_ref[...] * gate * (1.0 - gate)
            dz_ref[:, n * D:(n + 1) * D] = dl.astype(bf16)
            dbs.append(jnp.sum(dl, axis=0, keepdims=True))
        _acc_store(i, db_ref, jnp.concatenate(dbs, axis=0))

    return pl.pallas_call(
        body, name=name, grid=(T // TT,),
        in_specs=[_row(), _row(), _row(), _row(), _row(GB), _row(GB + 1), _row(GB + 2), _vec(3)],
        out_specs=[_row(), _row(), _row(), _row(0, 3 * D), _vec(3)],
        out_shape=[S((T, D), bf16), S((T, D), bf16), S((T, D), bf16), S((T, 3 * D), bf16), S((3, D), f32)],
        compiler_params=_cp("arbitrary"))(dm, ya, yb, yc, z, z, z, bg)


SMALL_NAMES = ("ffn1_norm", "mix_norm", "b_forget", "b_gate", "conv_w", "sgu_ln_g", "sgu_ln_b", "sgu_w", "sgu_b",
               "q_norm_g", "k_norm_g", "ffn2_norm")


def _small_params(p):
    out = {n: p[n].reshape(1, D) for n in ("ffn1_norm", "mix_norm", "ffn2_norm", "sgu_ln_g", "sgu_ln_b", "q_norm_g", "k_norm_g")}
    out["b_forget"] = jnp.pad(p["b_forget"].reshape(1, NH), ((0, 0), (0, 128 - NH)))
    out["b_gate"] = p["b_gate"]
    out["conv_w"] = p["conv_w"]
    out["sgu_w"] = p["sgu_w"]
    out["bmap"] = jnp.repeat(p["sgu_b"].T, HD, axis=1)
    return out


def _small_grads_natural(sg):
    out = {n: sg[n].reshape(D) for n in ("ffn1_norm", "mix_norm", "ffn2_norm", "sgu_ln_g", "sgu_ln_b")}
    out["q_norm_g"] = sg["q_norm_g"].reshape(NH, HD)
    out["k_norm_g"] = sg["k_norm_g"].reshape(NH, HD)
    out["b_forget"] = sg["b_forget"][0, :NH]
    out["b_gate"] = sg["b_gate"]
    out["conv_w"] = sg["conv_w"]
    out["sgu_w"] = sg["sgu_w"]
    out["sgu_b"] = sg["sgu_b"]
    return out


SQ_TM = 1024


def _sq_fwd(name, a, wsq, l, n, res=None):
    T = a.shape[0]
    tm = _tile(T, SQ_TM)
    return _mm(name, a, wsq, grid=(T // tm, 1, 1), a_spec=_bs((tm, D), lambda i, j, k: (i, 0)),
               b_spec=_bs((None, None, D, D), lambda i, j, k: (l, n, 0, 0)),
               out_shape=S((T, D), f32), out_spec=_bs((tm, D), lambda i, j, k: (i, 0)), dims=NN, acc_shape=None,
               res=res, res_spec=_bs((tm, D), lambda i, j, k: (i, 0)))


def _sq_bwd_in(name, dy, wsq, l, n):
    T = dy.shape[0]
    tm = _tile(T, SQ_TM)
    return _mm(name, dy, wsq, grid=(T // tm, 1, 1), a_spec=_bs((tm, D), lambda i, j, k: (i, 0)),
               b_spec=_bs((None, None, D, D), lambda i, j, k: (l, n, 0, 0)),
               out_shape=S((T, D), f32), out_spec=_bs((tm, D), lambda i, j, k: (i, 0)), dims=NT, acc_shape=None)


def _sq_bwd_w(name, a, dy, gbuf, l):
    T = a.shape[0]
    return _mm(name, a, dy, grid=(NDEV // 2, 1, 1), a_spec=_bs((T, 256), lambda i, j, k: (0, i)),
               b_spec=_bs((T, D), lambda i, j, k: (0, 0)), out_shape=S(gbuf.shape, bf16),
               out_spec=_bs((2, None, None, 128, D), lambda i, j, k: (0, i, l, 0, 0)),
               dims=TN, acc_shape=None, alias=gbuf, split_rows=128)


def _ffn_fwd(tag, x, g, wgu, wd, l):
    T = x.shape[0]
    h = _rms_fwd(tag + "_rms", x, g)
    gu, a = _swiglu_fwd(tag + "_gu", h, wgu, l)
    tm = _tile(T, 1024)
    xo = _mm(tag + "_down", a, wd, grid=(T // tm, 1, 4), a_spec=_bs((None, tm, GU), lambda i, j, k: (k, i, 0)),
             b_spec=_bs((None, None, GU, D), lambda i, j, k: (l, k, 0, 0)), out_shape=S((T, D), f32),
             out_spec=_bs((tm, D), lambda i, j, k: (i, 0)), dims=NN, acc_shape=(tm, D), res=x,
             res_spec=_bs((tm, D), lambda i, j, k: (i, 0)), alpha=0.5)
    return xo, (h, gu, a)


def _ffn_bwd(tag, dxo, x, g, wgu, wd, l, saved, g_gu, g_d, ship, hook, hook_at_once=False):
    h, gu, a = saved
    T = x.shape[0]
    g_d = _mm(tag + "_dwd", a, dxo, grid=(4, 1, 1), a_spec=_bs((None, T, GU), lambda i, j, k: (i, 0, 0)),
              b_spec=_bs((T, D), lambda i, j, k: (0, 0)), out_shape=S(g_d.shape, bf16),
              out_spec=_bs((2, None, None, GU // 2, D), lambda i, j, k: (0, i, l, 0, 0)), dims=TN, acc_shape=None,
              alpha=0.5, alias=g_d, split_rows=GU // 2)
    dgu = _swiglu_bwd(tag + "_dgu", dxo, wd, gu, l).reshape(NDEV, T, GU)
    g_gu = _mm(tag + "_dwgu", dgu, h, grid=(NDEV, 1, 1), a_spec=_bs((None, T, GU), lambda i, j, k: (i, 0, 0)),
               b_spec=_bs((T, D), lambda i, j, k: (0, 0)), out_shape=S(g_gu.shape, bf16),
               out_spec=_bs((None, None, None, GU, D), lambda i, j, k: (i % 2, i // 2, l, 0, 0)), dims=TN,
               acc_shape=None, alias=g_gu)
    dxo = ship([g_gu, g_d], dxo)
    if hook_at_once:
        dxo = hook(dxo, g_gu)
    dh = _mm(tag + "_dh", dgu, wgu, grid=(1, 1, NDEV), a_spec=_bs((None, T, GU), lambda i, j, k: (k, 0, 0)),
             b_spec=_bs((None, None, GU, D), lambda i, j, k: (l, k, 0, 0)), out_shape=S((T, D), f32),
             out_spec=_bs((T, D), lambda i, j, k: (0, 0)), dims=NN, acc_shape=(T, D))
    if not hook_at_once:
        dxo = hook(dxo, dh)
    return _rms_bwd(tag + "_drms", dh, x, g, dxo)


def _mixer_fwd(tag, x, p, win, wsq, l, after_in=None):
    T = x.shape[0]
    h = _rms_fwd(tag + "_rms", x, p["mix_norm"])
    tn = 512
    z = _mm(tag + "_in", h, win, grid=(1, NZ // tn, 1), a_spec=_bs((T, D), lambda i, j, k: (0, 0)),
            b_spec=_bs((D, tn), lambda i, j, k: (0, j)), out_shape=S((T, NZ), f32),
            out_spec=_bs((T, tn), lambda i, j, k: (0, j)), dims=NN, acc_shape=None)
    if after_in is not None:
        z = after_in(z)
    ya_in = _conv_fwd(tag + "_conv", z, p["conv_w"])
    yb_in = _sgu_fwd(tag + "_sgu", z, p["sgu_ln_g"], p["sgu_ln_b"], p["sgu_w"], p["bmap"])
    qn, kn, vb, logf = _qk_fwd(tag + "_qk", z, p["q_norm_g"], p["k_norm_g"], p["b_forget"])
    ccol, crow = _cum_fwd(tag + "_cum", logf)
    crow3 = crow.reshape(NH, 1, T)
    o, lse, lser = _attn_fwd(tag + "_attn", qn, kn, vb, ccol, crow3)
    if callable(wsq):
        wsq = wsq(o)
    ya = _sq_fwd(tag + "_oconv", ya_in, wsq, l, 0)
    yb = _sq_fwd(tag + "_osgu", yb_in, wsq, l, 1)
    yc = _sq_fwd(tag + "_oattn", o, wsq, l, 2)
    merged = _merge_fwd(tag + "_merge", ya, yb, yc, z, p["b_gate"])
    xo = _sq_fwd(tag + "_o", merged, wsq, l, 3, res=x)
    return xo, (h, z, ya_in, yb_in, qn, kn, vb, ccol, crow3, o, lse, lser, ya, yb, yc, merged)


def _mixer_bwd(tag, dxo, x, p, win, wsq, l, saved, gsq, ship_sq, ship_in, hook):
    h, z, ya_in, yb_in, qn, kn, vb, ccol, crow3, o, lse, lser, ya, yb, yc, merged = saved
    T = x.shape[0]
    sg = {}
    dm = _sq_bwd_in(tag + "_dmerged", dxo, wsq, l, 3)
    gsq[3] = _sq_bwd_w(tag + "_dwo", merged, dxo, gsq[3], l)
    dya, dyb, dyc, dz_g, sg["b_gate"] = _merge_bwd(tag + "_dmerge", dm, ya, yb, yc, z, p["b_gate"])
    d_ya_in = _sq_bwd_in(tag + "_dconv_in", dya, wsq, l, 0)
    gsq[0] = _sq_bwd_w(tag + "_dwoc", ya_in, dya, gsq[0], l)
    d_yb_in = _sq_bwd_in(tag + "_dsgu_in", dyb, wsq, l, 1)
    gsq[1] = _sq_bwd_w(tag + "_dwos", yb_in, dyb, gsq[1], l)
    d_o = _sq_bwd_in(tag + "_dattn_in", dyc, wsq, l, 2)
    gsq[2] = _sq_bwd_w(tag + "_dwoa", o, dyc, gsq[2], l)
    dxo = ship_sq(gsq, dxo)
    dz_c, sg["conv_w"] = _conv_bwd(tag + "_dconv", d_ya_in, z, p["conv_w"])
    dxo = hook(dxo, dz_c)
    dz_s, sg["sgu_ln_g"], sg["sgu_ln_b"], sg["sgu_w"], db_t = _sgu_bwd(
        tag + "_dsgu", d_yb_in, z, p["sgu_ln_g"], p["sgu_ln_b"], p["sgu_w"], p["bmap"])
    sg["sgu_b"] = db_t.T
    dqn, dlr = _attn_dq(tag + "_dattn_q", qn, kn, vb, d_o, lse, ccol, crow3)
    dkn, dv, cs = _attn_dkv(tag + "_dattn_kv", qn, kn, vb, d_o, lser, dlr, ccol, crow3)
    dz_f, sg["b_forget"] = _forget_bwd(tag + "_dforget", cs, z, p["b_forget"])
    dz_q, sg["q_norm_g"], sg["k_norm_g"] = _qk_bwd(tag + "_dqk", dqn, dkn, dv, z, p["q_norm_g"], p["k_norm_g"])
    dz = [dz_c, dz_s, dz_q, dz_g, dz_f]
    dwin = _dz_matmul(tag + "_dwin", dz, h, True)
    dxo = ship_in(dwin, dxo)
    dh = _dz_matmul(tag + "_dh", dz, win, False)
    dxo = hook(dxo, dh)
    dx, sg["mix_norm"] = _rms_bwd(tag + "_drms", dh, x, p["mix_norm"], dxo)
    return dx, sg


ANY = pl.BlockSpec(memory_space=pl.ANY)
HBM = pl.BlockSpec(memory_space=pltpu.HBM)
SEM = pl.BlockSpec(memory_space=pltpu.SEMAPHORE)
EFFECT = pltpu.SideEffectType.DATAFLOW_SIDE_EFFECTING


def _place():
    return lax.axis_index("x"), lax.axis_index("y"), lax.axis_index("c")


NEAR = 4


def _others(x, y, c):
    return [(x, y, 1 - c), (1 - x, y, c), (x, 1 - y, c), (1 - x, 1 - y, c)]


def _gather_start(name, groups, carry=None):
    sizes = [len(g) for g in groups]
    srcs = [s for g in groups for s, _ in g]
    lands = [b for g in groups for _, b in g]
    n, ng = len(srcs), len(groups)
    held = srcs + lands + ([] if carry is None else [carry])
    nh = len(held)

    def body(*refs):
        src_refs, land_refs = refs[:n], refs[n:2 * n]
        send, recv = refs[nh:nh + ng], refs[nh + ng:nh + 2 * ng]
        x, y, c = _place()
        me = 4 * x + 2 * y + c
        u = 0
        for g, size in enumerate(sizes):
            for i in range(size):
                for k, peer in enumerate(_others(x, y, c)):
                    pltpu.make_async_remote_copy(src_ref=src_refs[u], dst_ref=land_refs[u].at[me],
                                                 send_sem=send[g].at[i * NEAR + k], recv_sem=recv[g].at[i * NEAR + k],
                                                 device_id=peer, device_id_type=MESH).start()
                u += 1

    sems = [pltpu.SemaphoreType.DMA((size * NEAR,)) for size in sizes]
    out = pl.pallas_call(
        body, name=name, in_specs=[HBM] * nh, out_specs=[SEM] * (2 * ng) + [HBM] * nh,
        out_shape=sems + sems + [pltpu.HBM(a.shape, a.dtype) for a in held],
        input_output_aliases={i: 2 * ng + i for i in range(nh)},
        compiler_params=pltpu.CompilerParams(has_side_effects=EFFECT),
    )(*[pltpu.with_memory_space_constraint(a, pltpu.HBM) for a in held])
    res, pos = [], 0
    for g, size in enumerate(sizes):
        res.append((out[g], out[ng + g], out[2 * ng + pos:2 * ng + pos + size], out[2 * ng + n + pos:2 * ng + n + pos + size]))
        pos += size
    return res if carry is None else (res, out[2 * ng + 2 * n])


def _gather_wait(name, started, after=None):
    send, recv, srcs, lands = started
    n = len(srcs)

    def body(*refs):
        src_refs, land_refs = refs[:n], refs[n:2 * n]
        send_ref, recv_ref = refs[2 * n], refs[2 * n + 1]
        x, y, c = _place()
        for i in range(n):
            for k, (px, py, pc) in enumerate(_others(x, y, c)):
                cp = pltpu.make_async_remote_copy(src_ref=src_refs[i], dst_ref=land_refs[i].at[4 * px + 2 * py + pc],
                                                  send_sem=send_ref.at[i * NEAR + k], recv_sem=recv_ref.at[i * NEAR + k],
                                                  device_id=(px, py, pc), device_id_type=MESH)
                cp.wait_send()
                cp.wait_recv()

    extra = [] if after is None else (list(after) if isinstance(after, (list, tuple)) else [after])
    out = pl.pallas_call(
        body, name=name, in_specs=[HBM] * (2 * n) + [SEM, SEM] + [ANY] * len(extra), out_specs=[HBM] * (2 * n),
        out_shape=[pltpu.HBM(a.shape, a.dtype) for a in list(srcs) + list(lands)],
        input_output_aliases={i: i for i in range(2 * n)},
        compiler_params=pltpu.CompilerParams(has_side_effects=EFFECT),
    )(*srcs, *lands, send, recv, *extra)
    return out[n:]


def _forward_copy(have, full, send, recv, i, j, core):
    x, y, c = _place()
    chip = [(1 - x, y), (x, 1 - y), (1 - x, 1 - y)][j]
    slot = 4 * chip[0] + 2 * chip[1] + core
    return pltpu.make_async_remote_copy(src_ref=have[i].at[slot], dst_ref=full[i].at[slot], send_sem=send.at[i * 3 + j],
                                        recv_sem=recv.at[i * 3 + j], device_id=(x, y, 1 - c), device_id_type=MESH)


def _gather_forward(name, lands):
    n = len(lands)

    def body(*refs):
        have, full = refs[:n], refs[n:2 * n]
        send, recv = refs[2 * n], refs[2 * n + 1]
        c = lax.axis_index("c")
        for i in range(n):
            for j in range(3):
                _forward_copy(have, full, send, recv, i, j, c).start()
        for i in range(n):
            for j in range(3):
                _forward_copy(have, full, send, recv, i, j, c).wait_send()
                _forward_copy(have, full, send, recv, i, j, 1 - c).wait_recv()

    return pl.pallas_call(
        body, name=name, in_specs=[ANY] * n, out_specs=[ANY] * n, out_shape=[S(a.shape, a.dtype) for a in lands],
        input_output_aliases={i: i for i in range(n)},
        scratch_shapes=[pltpu.SemaphoreType.DMA((n * 3,)), pltpu.SemaphoreType.DMA((n * 3,))],
    )(*lands)


def _forward_start(name, lands, carry):
    n = len(lands)
    held = list(lands) + [carry]

    def body(*refs):
        c = lax.axis_index("c")
        for i in range(n):
            for j in range(3):
                _forward_copy(refs[:n], refs[:n], refs[n + 1], refs[n + 2], i, j, c).start()

    sems = [pltpu.SemaphoreType.DMA((n * 3,))] * 2
    out = pl.pallas_call(
        body, name=name, in_specs=[HBM] * (n + 1), out_specs=[SEM, SEM] + [HBM] * (n + 1),
        out_shape=sems + [pltpu.HBM(a.shape, a.dtype) for a in held],
        input_output_aliases={i: 2 + i for i in range(n + 1)},
        compiler_params=pltpu.CompilerParams(has_side_effects=EFFECT),
    )(*[pltpu.with_memory_space_constraint(a, pltpu.HBM) for a in held])
    return (out[0], out[1], out[2:2 + n]), out[2 + n]


def _forward_wait(name, started, after):
    send, recv, lands = started
    n = len(lands)

    def body(*refs):
        c = lax.axis_index("c")
        for i in range(n):
            for j in range(3):
                _forward_copy(refs[:n], refs[:n], refs[n], refs[n + 1], i, j, c).wait_send()
                _forward_copy(refs[:n], refs[:n], refs[n], refs[n + 1], i, j, 1 - c).wait_recv()

    return pl.pallas_call(
        body, name=name, in_specs=[HBM] * n + [SEM, SEM, ANY], out_specs=[HBM] * n,
        out_shape=[pltpu.HBM(a.shape, a.dtype) for a in lands],
        input_output_aliases={i: i for i in range(n)},
        compiler_params=pltpu.CompilerParams(has_side_effects=EFFECT),
    )(*lands, send, recv, after)


def _pair_copies(ins, outs, send, recv):
    x, y, c = _place()
    return [pltpu.make_async_remote_copy(src_ref=ins[u].at[1 - c], dst_ref=outs[u], send_sem=send.at[u],
                                         recv_sem=recv.at[u], device_id=(x, y, 1 - c), device_id_type=MESH)
            for u in range(len(ins))]


def _pair_start(name, gs, carry):
    n = len(gs)
    held = list(gs) + [lax.empty(g.shape[1:], g.dtype) for g in gs] + [carry]

    def body(*refs):
        for cp in _pair_copies(refs[:n], refs[n:2 * n], refs[2 * n + 1], refs[2 * n + 2]):
            cp.start()

    sems = [pltpu.SemaphoreType.DMA((n,))] * 2
    out = pl.pallas_call(
        body, name=name, in_specs=[HBM] * len(held), out_specs=[SEM, SEM] + [HBM] * len(held),
        out_shape=sems + [pltpu.HBM(a.shape, a.dtype) for a in held],
        input_output_aliases={i: 2 + i for i in range(len(held))},
        compiler_params=pltpu.CompilerParams(has_side_effects=EFFECT),
    )(*[pltpu.with_memory_space_constraint(a, pltpu.HBM) for a in held])
    return (out[0], out[1], out[2:2 + n], out[2 + n:2 + 2 * n]), out[2 + 2 * n]


def _pair_wait(name, started, after):
    send, recv, gs, lands = started
    n = len(gs)

    def body(*refs):
        for cp in _pair_copies(refs[:n], refs[n:2 * n], refs[2 * n], refs[2 * n + 1]):
            cp.wait_send()
            cp.wait_recv()

    out = pl.pallas_call(
        body, name=name, in_specs=[HBM] * (2 * n) + [SEM, SEM, ANY], out_specs=[HBM] * (2 * n),
        out_shape=[pltpu.HBM(a.shape, a.dtype) for a in list(gs) + list(lands)],
        input_output_aliases={i: i for i in range(2 * n)},
        compiler_params=pltpu.CompilerParams(has_side_effects=EFFECT),
    )(*gs, *lands, send, recv, after)
    return out[:n], out[n:]


def _row_tile(r, c):
    if c > D and r % 128 == 0:
        return 128
    return 256 if r % 256 == 0 else (GU // 2 if r % (GU // 2) == 0 else r)


def _pair_sum(name, core, g, r1):
    _, nq, nl, r, c = g.shape
    tr = r
    g4 = g.reshape(2, nq * nl, r, c)
    r3 = r1.reshape(nq * nl, r, c)

    def body(core_ref, g_ref, r_ref, o_ref):
        o_ref[...] = (g_ref[...].astype(f32) + r_ref[...].astype(f32)).astype(bf16)

    out = pl.pallas_call(
        body, name=name,
        grid_spec=pltpu.PrefetchScalarGridSpec(
            num_scalar_prefetch=1, grid=(nq * nl, r // tr),
            in_specs=[_bs((None, None, tr, c), lambda b, i, cr: (cr[0], b, i, 0)), _bs((None, tr, c), lambda b, i, cr: (b, i, 0))],
            out_specs=_bs((None, tr, c), lambda b, i, cr: (b, i, 0))),
        out_shape=S((nq * nl, r, c), bf16), compiler_params=_cp("parallel", "parallel"))(core, g4, r3)
    return out.reshape(nq, nl, r, c)


def _scatter_copies(ins, outs, send, recv):
    x, y, c = _place()
    chips = [(1 - x, y), (x, 1 - y), (1 - x, 1 - y)]
    return [pltpu.make_async_remote_copy(src_ref=ins[u].at[2 * chip[0] + chip[1]], dst_ref=outs[u].at[k],
                                         send_sem=send.at[u * 3 + k], recv_sem=recv.at[u * 3 + k],
                                         device_id=(*chip, c), device_id_type=MESH)
            for u in range(len(ins)) for k, chip in enumerate(chips)]


def _scatter_start(name, ss, carry):
    n = len(ss)
    lands = [lax.empty((3,) + s.shape[1:], s.dtype) for s in ss]
    held = list(ss) + lands + [carry]

    def body(*refs):
        for cp in _scatter_copies(refs[:n], refs[n:2 * n], refs[2 * n + 1], refs[2 * n + 2]):
            cp.start()

    sems = [pltpu.SemaphoreType.DMA((n * 3,))] * 2
    out = pl.pallas_call(
        body, name=name, in_specs=[HBM] * len(held), out_specs=[SEM, SEM] + [HBM] * len(held),
        out_shape=sems + [pltpu.HBM(a.shape, a.dtype) for a in held],
        input_output_aliases={i: 2 + i for i in range(len(held))},
        compiler_params=pltpu.CompilerParams(has_side_effects=EFFECT),
    )(*[pltpu.with_memory_space_constraint(a, pltpu.HBM) for a in held])
    return (out[0], out[1], out[2:2 + n], out[2 + n:2 + 2 * n]), out[2 + 2 * n]


def _scatter_wait(name, started, after):
    send, recv, srcs, lands = started
    n = len(srcs)

    def body(*refs):
        for cp in _scatter_copies(refs[:n], refs[n:2 * n], refs[2 * n], refs[2 * n + 1]):
            cp.wait_send()
            cp.wait_recv()

    out = pl.pallas_call(
        body, name=name, in_specs=[HBM] * (2 * n) + [SEM, SEM, ANY], out_specs=[HBM] * (2 * n),
        out_shape=[pltpu.HBM(a.shape, a.dtype) for a in list(srcs) + list(lands)],
        input_output_aliases={i: i for i in range(2 * n)},
        compiler_params=pltpu.CompilerParams(has_side_effects=EFFECT),
    )(*srcs, *lands, send, recv, after)
    return out[:n], out[n:]


def _sum_blocks(name, blocks):
    def body(b_ref, o_ref):
        acc = b_ref[0]
        for d in range(1, NDEV):
            acc = acc + b_ref[d]
        o_ref[...] = acc

    return pl.pallas_call(body, name=name, out_shape=S(blocks.shape[1:], f32),
                          compiler_params=pltpu.CompilerParams(vmem_limit_bytes=VMEM_LIMIT))(blocks)


def _adam_math(w, g, m, v):
    m = ADAM_B1 * m + (1.0 - ADAM_B1) * g
    v = ADAM_B2 * v + (1.0 - ADAM_B2) * (g * g)
    m_hat = m / (1.0 - ADAM_B1 ** ADAM_STEP)
    v_hat = v / (1.0 - ADAM_B2 ** ADAM_STEP)
    delta = -ADAM_LR * (m_hat / (jnp.sqrt(v_hat) + ADAM_EPS) + ADAM_WD * w)
    return delta, m, v


def _adamw(name, chip, w, m, v, parts):
    _, r, c = w.shape
    tr = _row_tile(r, c)

    def body(chip_ref, w_ref, m_ref, v_ref, *refs):
        sets, (g_ref, d_ref, mo_ref, vo_ref) = (refs[0:4], refs[4:8]), refs[8:]
        for l in range(2):
            @pl.when(pl.program_id(0) == l)
            def _():
                s_ref, r0_ref, r1_ref, r2_ref = sets[l]
                g = ((s_ref[...].astype(f32) + r0_ref[...].astype(f32)) + r1_ref[...].astype(f32)) + r2_ref[...].astype(f32)
                g_ref[...] = g
                d_ref[...], mo_ref[...], vo_ref[...] = _adam_math(w_ref[...], g, m_ref[...], v_ref[...])

    blk = _bs((None, tr, c), lambda l, i, cr: (l, i, 0))
    operands, specs = [], []
    for n in range(2):
        row = (lambda l, i: i * (1 - l)) if n == 0 else (lambda l, i: i * l)
        sums, r2 = parts[n]
        operands += [sums, r2, r2, r2]
        specs.append(_bs((None, None, tr, c), functools.partial(lambda l, i, cr, row: (cr[0], 0, row(l, i), 0), row=row)))
        specs += [_bs((None, None, tr, c), functools.partial(lambda l, i, cr, k, row: (k, 0, row(l, i), 0), k=k, row=row))
                  for k in range(3)]
    return pl.pallas_call(
        body, name=name,
        grid_spec=pltpu.PrefetchScalarGridSpec(num_scalar_prefetch=1, grid=(2, r // tr), in_specs=[blk, blk, blk] + specs,
                                               out_specs=[blk] * 4),
        out_shape=[S(w.shape, f32)] * 4, compiler_params=_cp("arbitrary", "arbitrary"),
    )(chip, w, m, v, *operands)


def _adamw_small(name, w, g, m, v):
    def body(w_ref, g_ref, m_ref, v_ref, d_ref, mo_ref, vo_ref):
        d_ref[...], mo_ref[...], vo_ref[...] = _adam_math(w_ref[...], g_ref[...], m_ref[...], v_ref[...])

    return pl.pallas_call(body, name=name, out_shape=[S(w.shape, f32)] * 3,
                          compiler_params=pltpu.CompilerParams(vmem_limit_bytes=VMEM_LIMIT))(w, g, m, v)


WEIGHT_NAMES = ("ffn1_norm", "ffn1_w_gu", "ffn1_w_down", "mix_norm", "w_in", "b_forget", "b_gate", "conv_w", "sgu_ln_g",
                "sgu_ln_b", "sgu_w", "sgu_b", "q_norm_g", "k_norm_g", "w_out_conv", "w_out_sgu", "w_out_attn", "w_o",
                "ffn2_norm", "ffn2_w_gu", "ffn2_w_down")
BIG = {"ffn1_w_gu": "gu1", "ffn2_w_gu": "gu2", "ffn1_w_down": "d1", "ffn2_w_down": "d2", "w_in": "in",
       "w_out_conv": "oc", "w_out_sgu": "os", "w_out_attn": "oa", "w_o": "o"}
BIG_KEYS = ("gu1", "gu2", "d1", "d2", "in", "oc", "os", "oa", "o")
REPLICATED_SMALL = ("ffn1_norm", "mix_norm", "b_forget", "sgu_ln_g", "sgu_ln_b", "sgu_w", "sgu_b", "q_norm_g",
                    "k_norm_g", "ffn2_norm")
SHARDED_SMALL = ("b_gate", "conv_w")
TRANSPOSED = ("gu1", "gu2")


def _packed_rows(shape):
    size = 1
    for s_ in shape:
        size *= s_
    return size, -(-size // 1024) * 8


def _pack(arrays):
    pieces = []
    for a in arrays:
        size, rows = _packed_rows(a.shape)
        pieces.append(jnp.pad(a.reshape(-1).astype(f32), (0, rows * 128 - size)).reshape(rows, 128))
    return jnp.concatenate(pieces, axis=0)


def _unpack(packed, shapes):
    out, pos = [], 0
    for shp in shapes:
        size, rows = _packed_rows(shp)
        out.append(packed[pos:pos + rows].reshape(-1)[:size].reshape(shp))
        pos += rows
    return out


def _natural_runs(a, b):
    runs = []
    while a < b:
        d = a // INB
        e = min(b, (d + 1) * INB)
        runs.append((d, a - d * INB, e - d * INB))
        a = e
    return runs


def _win_runs():
    runs = _natural_runs(0, GATE_OFF) + _natural_runs(GATE_OFF + NH, NIN) + _natural_runs(GATE_OFF, GATE_OFF + NH)
    out, pos = [], 0
    for d, a, b in runs:
        out.append((d, a, b, pos))
        pos += b - a
    return out


RELAYOUT_ROWS = 128


def _win_kernel_layout(name, wg):
    def body(w_ref, o_ref):
        for d, a, b, pos in _win_runs():
            o_ref[:, pos:pos + (b - a)] = w_ref[d, :, a:b]
        o_ref[:, NIN:NZ] = jnp.zeros((RELAYOUT_ROWS, NZ - NIN), wg.dtype)

    return pl.pallas_call(
        body, name=name, grid=(D // RELAYOUT_ROWS,), in_specs=[_bs((NDEV, RELAYOUT_ROWS, INB), lambda i: (0, i, 0))],
        out_specs=_bs((RELAYOUT_ROWS, NZ), lambda i: (i, 0)), out_shape=S((D, NZ), wg.dtype),
        compiler_params=_cp("parallel"))(wg)


def _win_device_blocks(name, dw):
    def body(dw_ref, o_ref):
        for d, a, b, pos in _win_runs():
            o_ref[d % 2, d // 2, :, a:b] = dw_ref[:, pos:pos + (b - a)]

    out = pl.pallas_call(
        body, name=name, grid=(D // RELAYOUT_ROWS,), in_specs=[_bs((RELAYOUT_ROWS, NZ), lambda i: (i, 0))],
        out_specs=_bs((2, 4, RELAYOUT_ROWS, INB), lambda i: (0, 0, i, 0)), out_shape=S((2, 4, D, INB), dw.dtype),
        compiler_params=_cp("parallel"))(dw)
    return out.reshape(2, 4, 1, D, INB)


def kernel(x, ffn1_norm, ffn1_w_gu, ffn1_w_down, mix_norm, w_in, b_forget, b_gate, conv_w, sgu_ln_g, sgu_ln_b, sgu_w, sgu_b, q_norm_g, k_norm_g, w_out_conv, w_out_sgu, w_out_attn, w_o, ffn2_norm, ffn2_w_gu, ffn2_w_down, loss_target, m_ffn1_norm, m_ffn1_w_gu, m_ffn1_w_down, m_mix_norm, m_w_in, m_b_forget, m_b_gate, m_conv_w, m_sgu_ln_g, m_sgu_ln_b, m_sgu_w, m_sgu_b, m_q_norm_g, m_k_norm_g, m_w_out_conv, m_w_out_sgu, m_w_out_attn, m_w_o, m_ffn2_norm, m_ffn2_w_gu, m_ffn2_w_down, v_ffn1_norm, v_ffn1_w_gu, v_ffn1_w_down, v_mix_norm, v_w_in, v_b_forget, v_b_gate, v_conv_w, v_sgu_ln_g, v_sgu_ln_b, v_sgu_w, v_sgu_b, v_q_norm_g, v_k_norm_g, v_w_out_conv, v_w_out_sgu, v_w_out_attn, v_w_o, v_ffn2_norm, v_ffn2_w_gu, v_ffn2_w_down):
    w = dict(zip(WEIGHT_NAMES, (ffn1_norm, ffn1_w_gu, ffn1_w_down, mix_norm, w_in, b_forget, b_gate, conv_w, sgu_ln_g,
                                sgu_ln_b, sgu_w, sgu_b, q_norm_g, k_norm_g, w_out_conv, w_out_sgu, w_out_attn, w_o,
                                ffn2_norm, ffn2_w_gu, ffn2_w_down)))
    mom = dict(zip(WEIGHT_NAMES, (m_ffn1_norm, m_ffn1_w_gu, m_ffn1_w_down, m_mix_norm, m_w_in, m_b_forget, m_b_gate,
                                  m_conv_w, m_sgu_ln_g, m_sgu_ln_b, m_sgu_w, m_sgu_b, m_q_norm_g, m_k_norm_g,
                                  m_w_out_conv, m_w_out_sgu, m_w_out_attn, m_w_o, m_ffn2_norm, m_ffn2_w_gu,
                                  m_ffn2_w_down)))
    var = dict(zip(WEIGHT_NAMES, (v_ffn1_norm, v_ffn1_w_gu, v_ffn1_w_down, v_mix_norm, v_w_in, v_b_forget, v_b_gate,
                                  v_conv_w, v_sgu_ln_g, v_sgu_ln_b, v_sgu_w, v_sgu_b, v_q_norm_g, v_k_norm_g,
                                  v_w_out_conv, v_w_out_sgu, v_w_out_attn, v_w_o, v_ffn2_norm, v_ffn2_w_gu,
                                  v_ffn2_w_down)))
    px, py, pc = _place()
    dev = 4 * px + 2 * py + pc
    chip = 2 * px + py

    big_names = [n for n in WEIGHT_NAMES if n in BIG]
    key_name = {BIG[n]: n for n in big_names}
    group_keys = (("gu1", "d1"), ("in", "small"), ("gu2", "d2", "oc", "os", "oa", "o"))

    def source(key, l):
        if key == "small":
            return jnp.concatenate([w["b_gate"][l], w["conv_w"][l], jnp.zeros((2, 128), f32)], axis=0)
        block = w[key_name[key]][l]
        return (block.T if key in TRANSPOSED else block).astype(bf16)

    def landing(src):
        return lax.dynamic_update_slice(lax.empty((NDEV,) + src.shape, src.dtype), src[None], (dev, 0, 0))

    groups = [[(s, landing(s)) for s in (source(k, l) for k in keys)] for l in range(2) for keys in group_keys]
    (send0, recv0, srcs0, lands0), = _gather_start("gather_start_first", groups[:1])
    rest, src0 = _gather_start("gather_start_rest", groups[1:], carry=srcs0[0])
    started = [(send0, recv0, [src0] + list(srcs0[1:]), lands0)] + rest

    early = {}

    order = REPLICATED_SMALL + SHARDED_SMALL
    wp, mp, vp = (_pack([t[n] for n in order]) for t in (w, mom, var))
    while_idle = [mom["w_in"], var["w_in"], wp, mp, vp]

    def prefetch(l, part, act):
        got = _gather_wait(f"gather_wait_{l}_{part}", started[3 * l + part], act)
        early[l, part], act = _forward_start(f"forward_start_{l}_{part}", got, act)
        return act

    def weights(l, part, after):
        if (l, part) in early:
            lands = _forward_wait(f"forward_wait_{l}_{part}", early.pop((l, part)), after)
        else:
            got = _gather_wait(f"gather_wait_{l}_{part}", started[3 * l + part], after)
            lands = _gather_forward(f"gather_forward_{l}_{part}", got)
        return dict(zip(group_keys[part], lands))

    xl = x[0]
    saved, small, wts = [], [], []
    for l in range(2):
        ga = weights(l, 0, xl if l else None)
        if l:
            xl = prefetch(l, 1, xl)
        wt = {"gu1": ga["gu1"][None], "d1": ga["d1"].reshape(1, 4, GU, D)}
        x1, s1 = _ffn_fwd("ffn1", xl, w["ffn1_norm"][l].reshape(1, D), wt["gu1"], wt["d1"], 0)
        gb = weights(l, 1, x1 if l else [x1] + while_idle)
        if l:
            x1 = prefetch(l, 2, x1)
        p = {n: w[n][l] for n in REPLICATED_SMALL}
        p["b_gate"] = jnp.transpose(gb["small"][:, 0:3, :], (1, 0, 2)).reshape(3, D)
        p["conv_w"] = jnp.transpose(gb["small"][:, 3:6, :], (1, 0, 2)).reshape(3, D)
        p = _small_params(p)
        wt["win"] = _win_kernel_layout("win_layout", gb["in"])

        def third_group(after, l=l, wt=wt):
            gc = weights(l, 2, after)
            wt.update({"gu2": gc["gu2"][None], "d2": gc["d2"].reshape(1, 4, GU, D),
                       "sq": jnp.stack([gc[k].reshape(D, D) for k in ("oc", "os", "oa", "o")])[None]})
            return wt["sq"]

        x2, sm = _mixer_fwd("mix", x1, p, wt["win"], third_group, 0, None if l else functools.partial(prefetch, 0, 2))
        if l == 0:
            x2 = prefetch(1, 0, x2)
        x3, s2 = _ffn_fwd("ffn2", x2, p["ffn2_norm"], wt["gu2"], wt["d2"], 0)
        saved.append((xl, x1, x2, s1, sm, s2))
        small.append(p)
        wts.append(wt)
        xl = x3
    loss_row, dx = _loss("loss", xl, loss_target[0])

    core = pc.reshape(1).astype(jnp.int32)
    chip_op = chip.reshape(1).astype(jnp.int32)
    buf = lambda r, c: lax.empty((2, 4, 1, r, c), bf16)
    flights = {}

    pairs = []
    scatter_keys = {"ffn2": ("gu2", "d2"), "sq": ("oc", "os", "oa", "o"), "in": ("in",), "ffn1": ("gu1", "d1")}

    def ship(l, part, bufs, dx):
        started, dx = _pair_start(f"pair_start_{l}_{part}", bufs, dx)
        pairs.append((l, part, started))
        return dx

    def finish(dx, after):
        l, part, started = pairs.pop()
        bufs, r1 = _pair_wait(f"pair_wait_{l}_{part}", started, after)
        ss = [_pair_sum(f"pair_sum_{l}_{k}", core, g, r) for k, g, r in zip(scatter_keys[part], bufs, r1)]
        flights[l, part], dx = _scatter_start(f"scatter_start_{l}_{part}", ss, dx)
        return dx

    def ship_in(l, dwin, dx):
        return ship(l, "in", [_win_device_blocks("dwin_blocks", dwin)], dx)

    sgrads = [None, None]
    for l in (1, 0):
        p, wt = small[l], wts[l]
        x0, x1, x2, s1, sm, s2 = saved[l]
        dx, dn2 = _ffn_bwd("ffn2", dx, x2, p["ffn2_norm"], wt["gu2"], wt["d2"], 0, s2, buf(GU, D), buf(GU // 2, D),
                           functools.partial(ship, l, "ffn2"), finish)
        dx, sg = _mixer_bwd("mix", dx, x1, p, wt["win"], wt["sq"], 0, sm, [buf(128, D) for _ in range(4)],
                            functools.partial(ship, l, "sq"), functools.partial(ship_in, l), finish)
        dx, dn1 = _ffn_bwd("ffn1", dx, x0, p["ffn1_norm"], wt["gu1"], wt["d1"], 0, s1, buf(GU, D), buf(GU // 2, D),
                           functools.partial(ship, l, "ffn1"), finish, hook_at_once=(l == 0))
        sg["ffn1_norm"] = dn1
        sg["ffn2_norm"] = dn2
        sgrads[l] = sg

    nat = [_small_grads_natural(sgrads[l]) for l in range(2)]
    part = _pack([jnp.stack([nat[0][n], nat[1][n]]) for n in order] + [loss_row[0, 0:1]])
    small_flight, dx = _gather_start("small_start", [[(part, landing(part))]], carry=dx)

    grads, delta, new_m, new_v = {}, {}, {}, {}
    after = dx
    for part in ("ffn2", "sq", "in", "ffn1"):
        sets = []
        for l in (1, 0):
            s_all, r2_all = _scatter_wait(f"scatter_wait_{l}_{part}", flights[l, part], after)
            sets.append(list(zip(s_all, r2_all)))
        for i, k in enumerate(scatter_keys[part]):
            n = key_name[k]
            view = (lambda a: jnp.swapaxes(a, 1, 2)) if k in TRANSPOSED else (lambda a: a)
            outs = _adamw("adamw_" + k, chip_op, view(w[n]), view(mom[n]), view(var[n]), [sets[1][i], sets[0][i]])
            grads[n], delta[n], new_m[n], new_v[n] = [view(o) for o in outs]
            after = outs[1]

    blocks = _gather_forward("small_forward", _gather_wait("small_wait", small_flight[0], after))
    total = _sum_blocks("small_sum", blocks[0])
    full_shapes = [(2,) + tuple(nat[0][n].shape) for n in order] + [(1,)]
    summed = dict(zip(order + ("loss",), _unpack(total, full_shapes)))
    for n in REPLICATED_SMALL:
        grads[n] = summed[n]
    for n in SHARDED_SMALL:
        grads[n] = lax.dynamic_slice_in_dim(summed[n], dev * 128, 128, axis=2)
    gp = _pack([grads[n] for n in order])
    dpk, mpk, vpk = _adamw_small("adamw_small", wp, gp, mp, vp)
    local_shapes = [tuple(w[n].shape) for n in order]
    for dst, packed in ((delta, dpk), (new_m, mpk), (new_v, vpk)):
        dst.update(zip(order, _unpack(packed, local_shapes)))

    loss = summed["loss"][0]
    return (loss, dx[None], *[grads[n] for n in WEIGHT_NAMES], *[delta[n] for n in WEIGHT_NAMES],
            *[new_m[n] for n in WEIGHT_NAMES], *[new_v[n] for n in WEIGHT_NAMES])
```

```python
import functools

import jax
import jax.numpy as jnp
from jax import lax
from jax.experimental import pallas as pl
from jax.experimental.pallas import tpu as pltpu

f32 = jnp.float32
bf16 = jnp.bfloat16
S = jax.ShapeDtypeStruct
MESH = pl.DeviceIdType.MESH

D = 1024
NH = 8
HD = 128
NDEV = 8
GU = 704
NIN = 11272
INB = 1409
GATE_OFF = 8192
F_OFF = 11264
NZ = 11776
RMS_EPS = 1e-6
LN_EPS = 1e-5
ATT_SCALE = HD ** -0.5
NEG = -1e30
INV_SQRT2 = 0.7071067811865476
INV_SQRT2PI = 0.3989422804014327

ADAM_LR = 0.001
ADAM_B1 = 0.9
ADAM_B2 = 0.999
ADAM_EPS = 1e-08
ADAM_WD = 0.01
ADAM_STEP = 10

TT = 512
VMEM_LIMIT = 56 * 1024 * 1024


def _cp(*sem):
    return pltpu.CompilerParams(dimension_semantics=sem, vmem_limit_bytes=VMEM_LIMIT)


def _bs(shape, fn):
    return pl.BlockSpec(shape, fn)


NN = (((1,), (0,)), ((), ()))
NT = (((1,), (1,)), ((), ()))
TN = (((0,), (0,)), ((), ()))


def _mm(name, a, b, *, grid, a_spec, b_spec, out_shape, out_spec, dims, acc_shape, res=None, res_spec=None,
        alpha=1.0, alias=None, split_rows=None):
    nk = grid[2]

    def body(*refs):
        a_ref, b_ref = refs[0], refs[1]
        pos = 2
        res_ref = None
        if res is not None:
            res_ref = refs[pos]
            pos += 1
        if alias is not None:
            pos += 1
        o_ref = refs[pos]
        acc_ref = refs[pos + 1] if nk > 1 else None
        part = lax.dot_general(a_ref[...].astype(bf16), b_ref[...].astype(bf16), dims, preferred_element_type=f32)

        def finish(acc):
            if alpha != 1.0:
                acc = alpha * acc
            if res_ref is not None:
                acc = res_ref[...] + acc
            if split_rows is None:
                o_ref[...] = acc.astype(o_ref.dtype)
            else:
                o_ref[0] = acc[:split_rows].astype(o_ref.dtype)
                o_ref[1] = acc[split_rows:].astype(o_ref.dtype)

        if nk == 1:
            finish(part)
        else:
            k = pl.program_id(2)

            @pl.when(k == 0)
            def _():
                acc_ref[...] = part

            @pl.when(k > 0)
            def _():
                acc_ref[...] += part

            @pl.when(k == nk - 1)
            def _():
                finish(acc_ref[...])

    operands = [a, b]
    in_specs = [a_spec, b_spec]
    if res is not None:
        operands.append(res)
        in_specs.append(res_spec)
    aliases = {}
    if alias is not None:
        aliases = {len(operands): 0}
        operands.append(alias)
        in_specs.append(pl.BlockSpec(memory_space=pl.ANY))
    return pl.pallas_call(
        body, name=name, grid=grid, in_specs=in_specs, out_specs=out_spec, out_shape=out_shape,
        scratch_shapes=[pltpu.VMEM(acc_shape, f32)] if nk > 1 else [],
        input_output_aliases=aliases,
        compiler_params=_cp("parallel", "parallel", "arbitrary"),
    )(*operands)


def _tile(n, t):
    return t if n % t == 0 and n >= t else n


DZ_TILE = 512


def _dz_matmul(name, pieces, other, weight_grad):
    T = pieces[0].shape[0]
    counts = [p.shape[1] // DZ_TILE for p in pieces]
    starts = [sum(counts[:i]) for i in range(len(counts))]
    steps = sum(counts)
    npc = len(pieces)

    def body(*refs):
        prefs, o_ref, rest = refs[:npc], refs[npc], refs[npc + 1:]
        k = pl.program_id(0)
        if not weight_grad:
            out_ref, acc_ref = rest

            @pl.when(k == 0)
            def _():
                acc_ref[...] = jnp.zeros_like(acc_ref)

        for p_ref, s, c in zip(prefs, starts, counts):
            @pl.when((k >= s) & (k < s + c))
            def _():
                if weight_grad:
                    rest[0][...] = lax.dot_general(o_ref[...], p_ref[...], TN, preferred_element_type=f32).astype(bf16)
                else:
                    acc_ref[...] += lax.dot_general(p_ref[...], o_ref[...], NT, preferred_element_type=f32)

        if not weight_grad:
            @pl.when(k == steps - 1)
            def _():
                out_ref[...] = acc_ref[...]

    piece_specs = [_bs((T, DZ_TILE), functools.partial(lambda k, s, c: (0, jnp.clip(k - s, 0, c - 1)), s=s, c=c))
                   for s, c in zip(starts, counts)]
    if weight_grad:
        other_spec, out_spec, out_shape, scratch = _bs((T, D), lambda k: (0, 0)), _bs((D, DZ_TILE), lambda k: (0, k)), S((D, NZ), bf16), []
    else:
        other_spec, out_spec, out_shape = _bs((D, DZ_TILE), lambda k: (0, k)), _bs((T, D), lambda k: (0, 0)), S((T, D), f32)
        scratch = [pltpu.VMEM((T, D), f32)]
    return pl.pallas_call(body, name=name, grid=(steps,), in_specs=piece_specs + [other_spec], out_specs=out_spec,
                          out_shape=out_shape, scratch_shapes=scratch, compiler_params=_cp("arbitrary"))(*pieces, other)


def _row(cb=0, w=D):
    return _bs((TT, w), lambda i: (i, cb))


def _vec(rows=1, w=D):
    return _bs((rows, w), lambda i: (0, 0))


def _acc_store(i, ref, val):
    @pl.when(i == 0)
    def _():
        ref[...] = val

    @pl.when(i > 0)
    def _():
        ref[...] += val


def _rms_fwd(name, x, g):
    T = x.shape[0]

    def body(x_ref, g_ref, o_ref):
        xv = x_ref[...]
        r = lax.rsqrt(jnp.mean(xv * xv, axis=-1, keepdims=True) + RMS_EPS)
        o_ref[...] = (xv * r * g_ref[...]).astype(bf16)

    return pl.pallas_call(body, name=name, grid=(T // TT,), in_specs=[_row(), _vec()], out_specs=_row(),
                          out_shape=S((T, D), bf16), compiler_params=_cp("parallel"))(x, g)


def _rms_bwd(name, dh, x, g, dres):
    T = x.shape[0]

    def body(dh_ref, x_ref, g_ref, dres_ref, dx_ref, dg_ref):
        i = pl.program_id(0)
        xv = x_ref[...]
        r = lax.rsqrt(jnp.mean(xv * xv, axis=-1, keepdims=True) + RMS_EPS)
        xhat = xv * r
        dh_v = dh_ref[...]
        dyg = dh_v * g_ref[...]
        m = jnp.mean(dyg * xhat, axis=-1, keepdims=True)
        dx_ref[...] = dres_ref[...] + r * (dyg - xhat * m)
        _acc_store(i, dg_ref, jnp.sum(dh_v * xhat, axis=0, keepdims=True))

    return pl.pallas_call(body, name=name, grid=(T // TT,), in_specs=[_row(), _row(), _vec(), _row()],
                          out_specs=[_row(), _vec()], out_shape=[S((T, D), f32), S((1, D), f32)],
                          compiler_params=_cp("arbitrary"))(dh, x, g, dres)


def _sigmoid(x):
    return 1.0 / (1.0 + jnp.exp(-x))


def _swiglu_fwd(name, h, wgu, l):
    T = h.shape[0]

    def body(h_ref, wg_ref, wu_ref, gu_ref, a_ref):
        hv = h_ref[...]
        g = lax.dot_general(hv, wg_ref[...], NT, preferred_element_type=f32)
        u = lax.dot_general(hv, wu_ref[...], NT, preferred_element_type=f32)
        gu_ref[0] = g
        gu_ref[1] = u
        a_ref[...] = (g * _sigmoid(g) * u).astype(bf16)

    return pl.pallas_call(
        body, name=name, grid=(4,),
        in_specs=[_bs((T, D), lambda j: (0, 0)), _bs((None, None, GU, D), lambda j: (l, j, 0, 0)),
                  _bs((None, None, GU, D), lambda j: (l, j + 4, 0, 0))],
        out_specs=[_bs((2, None, T, GU), lambda j: (0, j, 0, 0)), _bs((None, T, GU), lambda j: (j, 0, 0))],
        out_shape=[S((2, 4, T, GU), f32), S((4, T, GU), bf16)], compiler_params=_cp("parallel"))(h, wgu, wgu)


def _swiglu_bwd(name, dxo, wd, gu, l):
    T = dxo.shape[0]
    tm = _tile(T, 1024)

    def body(dx_ref, wd_ref, g_ref, u_ref, o_ref):
        da = 0.5 * lax.dot_general(dx_ref[...].astype(bf16), wd_ref[...], NT, preferred_element_type=f32)
        g = g_ref[...]
        sg = _sigmoid(g)
        o_ref[0] = (da * u_ref[...] * (sg + g * sg * (1.0 - sg))).astype(bf16)
        o_ref[1] = (da * g * sg).astype(bf16)

    return pl.pallas_call(
        body, name=name, grid=(T // tm, 4),
        in_specs=[_bs((tm, D), lambda i, j: (i, 0)), _bs((None, None, GU, D), lambda i, j: (l, j, 0, 0)),
                  _bs((None, None, tm, GU), lambda i, j: (0, j, i, 0)), _bs((None, None, tm, GU), lambda i, j: (1, j, i, 0))],
        out_specs=_bs((2, None, tm, GU), lambda i, j: (0, j, i, 0)), out_shape=S((2, 4, T, GU), bf16),
        compiler_params=_cp("parallel", "parallel"))(dxo, wd, gu, gu)


def _loss(name, y, tgt):
    T = y.shape[0]

    def body(y_ref, t_ref, l_ref, dy_ref):
        i = pl.program_id(0)
        e = y_ref[...] - t_ref[...]
        dy_ref[...] = e * (1.0 / D)
        s = 0.5 * jnp.sum(jnp.mean(e * e, axis=-1, keepdims=True))
        _acc_store(i, l_ref, jnp.broadcast_to(s, (1, 128)))

    return pl.pallas_call(body, name=name, grid=(T // TT,), in_specs=[_row(), _row()],
                          out_specs=[_vec(1, 128), _row()], out_shape=[S((1, 128), f32), S((T, D), f32)],
                          compiler_params=_cp("arbitrary"))(y, tgt)


def _prev8(T, cb):
    return _bs((8, D), lambda i: (jnp.maximum(i * (TT // 8) - 1, 0), cb))


def _next8(T, cb):
    return _bs((8, D), lambda i: (jnp.minimum((i + 1) * (TT // 8), T // 8 - 1), cb))


def _conv_taps(i, ac_ref, ax_ref, pc_ref, px_ref):
    ca = ac_ref[...] * ax_ref[...]
    keep = (i > 0).astype(f32)
    p1 = pc_ref[7:8, :] * px_ref[7:8, :] * keep
    p2 = pc_ref[6:7, :] * px_ref[6:7, :] * keep
    row = lax.broadcasted_iota(jnp.int32, ca.shape, 0)
    s1 = jnp.where(row == 0, p1, pltpu.roll(ca, 1, 0))
    s2 = jnp.where(row == 0, p2, jnp.where(row == 1, p1, pltpu.roll(ca, 2, 0)))
    return ca, s1, s2


def _conv_fwd(name, z, cw):
    T = z.shape[0]

    def body(ab_ref, ac_ref, ax_ref, pc_ref, px_ref, w_ref, o_ref):
        i = pl.program_id(0)
        ca, s1, s2 = _conv_taps(i, ac_ref, ax_ref, pc_ref, px_ref)
        cv = w_ref[0:1, :] * s2 + w_ref[1:2, :] * s1 + w_ref[2:3, :] * ca
        o_ref[...] = (ab_ref[...] * cv).astype(bf16)

    return pl.pallas_call(
        body, name=name, grid=(T // TT,),
        in_specs=[_row(0), _row(1), _row(2), _prev8(T, 1), _prev8(T, 2), _vec(3)],
        out_specs=_row(), out_shape=S((T, D), bf16), compiler_params=_cp("parallel"))(z, z, z, z, z, cw)


def _conv_bwd(name, dya, z, cw):
    T = z.shape[0]
    n = T // TT

    def body(dya_ref, ab_ref, ac_ref, ax_ref, pc_ref, px_ref, ndya_ref, nab_ref, w_ref, dz_ref, dw_ref):
        i = pl.program_id(0)
        ca, s1, s2 = _conv_taps(i, ac_ref, ax_ref, pc_ref, px_ref)
        w0, w1, w2 = w_ref[0:1, :], w_ref[1:2, :], w_ref[2:3, :]
        cv = w0 * s2 + w1 * s1 + w2 * ca
        dya_v = dya_ref[...]
        ab = ab_ref[...]
        dcv = dya_v * ab
        keep = (i < n - 1).astype(f32)
        n1 = ndya_ref[0:1, :] * nab_ref[0:1, :] * keep
        n2 = ndya_ref[1:2, :] * nab_ref[1:2, :] * keep
        row = lax.broadcasted_iota(jnp.int32, dcv.shape, 0)
        f1 = jnp.where(row == TT - 1, n1, pltpu.roll(dcv, TT - 1, 0))
        f2 = jnp.where(row == TT - 1, n2, jnp.where(row == TT - 2, n1, pltpu.roll(dcv, TT - 2, 0)))
        dca = w2 * dcv + w1 * f1 + w0 * f2
        dz_ref[:, 0:D] = (dya_v * cv).astype(bf16)
        dz_ref[:, D:2 * D] = (dca * ax_ref[...]).astype(bf16)
        dz_ref[:, 2 * D:3 * D] = (dca * ac_ref[...]).astype(bf16)
        dw = jnp.concatenate([jnp.sum(dcv * s2, axis=0, keepdims=True), jnp.sum(dcv * s1, axis=0, keepdims=True),
                              jnp.sum(dcv * ca, axis=0, keepdims=True)], axis=0)
        _acc_store(i, dw_ref, dw)

    return pl.pallas_call(
        body, name=name, grid=(n,),
        in_specs=[_row(), _row(0), _row(1), _row(2), _prev8(T, 1), _prev8(T, 2), _next8(T, 0), _next8(T, 0), _vec(3)],
        out_specs=[_row(0, 3 * D), _vec(3)], out_shape=[S((T, 3 * D), bf16), S((3, D), f32)],
        compiler_params=_cp("arbitrary"))(dya, z, z, z, z, z, dya, z, cw)


def _gelu(x):
    return 0.5 * x * (1.0 + lax.erf(x * INV_SQRT2))


def _gelu_cdf(x):
    return 0.5 * (1.0 + lax.erf(x * INV_SQRT2))


def _gelu_grad(x, cdf):
    return cdf + x * jnp.exp(-0.5 * x * x) * INV_SQRT2PI


def _ln_stats(vv):
    mu = jnp.mean(vv, axis=-1, keepdims=True)
    xc = vv - mu
    rstd = lax.rsqrt(jnp.mean(xc * xc, axis=-1, keepdims=True) + LN_EPS)
    return xc * rstd, rstd


def _tril_w(w_ref, g):
    r = lax.broadcasted_iota(jnp.int32, (HD, HD), 0)
    c = lax.broadcasted_iota(jnp.int32, (HD, HD), 1)
    return jnp.where(c <= r, w_ref[g], 0.0).astype(bf16)


def _sgu_fwd(name, z, ln_g, ln_b, w_s, bmap):
    T = z.shape[0]

    def body(su_ref, sv_ref, lg_ref, lb_ref, w_ref, bm_ref, o_ref, vn_ref):
        xhat, _ = _ln_stats(_gelu(sv_ref[...]))
        vn_ref[...] = (xhat * lg_ref[...] + lb_ref[...]).astype(bf16)
        for g in range(NH):
            w = _tril_w(w_ref, g)
            cs = slice(g * HD, (g + 1) * HD)
            for c in range(TT // HD):
                rs = slice(c * HD, (c + 1) * HD)
                s = jnp.dot(w, vn_ref[rs, cs], preferred_element_type=f32) + bm_ref[:, cs]
                o_ref[rs, cs] = (_gelu(su_ref[rs, cs]) * s).astype(bf16)

    return pl.pallas_call(
        body, name=name, grid=(T // TT,),
        in_specs=[_row(3), _row(4), _vec(), _vec(), _bs((NH, HD, HD), lambda i: (0, 0, 0)), _vec(HD)],
        out_specs=_row(), out_shape=S((T, D), bf16), scratch_shapes=[pltpu.VMEM((TT, D), bf16)],
        compiler_params=_cp("parallel"))(z, z, ln_g, ln_b, w_s, bmap)


def _sgu_bwd(name, dyb, z, ln_g, ln_b, w_s, bmap):
    T = z.shape[0]

    def body(dyb_ref, su_ref, sv_ref, lg_ref, lb_ref, w_ref, bm_ref, dz_ref, dlg_ref, dlb_ref, dw_ref, db_ref,
             vn_ref, du_ref, dvn_ref, cu_ref, cv_ref):
        i = pl.program_id(0)
        sv = sv_ref[...]
        cv_ref[...] = _gelu_cdf(sv)
        cu_ref[...] = _gelu_cdf(su_ref[...])
        xhat, rstd = _ln_stats(sv * cv_ref[...])
        vn_ref[...] = (xhat * lg_ref[...] + lb_ref[...]).astype(bf16)
        r = lax.broadcasted_iota(jnp.int32, (HD, HD), 0)
        cc = lax.broadcasted_iota(jnp.int32, (HD, HD), 1)
        for g in range(NH):
            w = _tril_w(w_ref, g)
            cs = slice(g * HD, (g + 1) * HD)
            dw = jnp.zeros((HD, HD), f32)
            db = jnp.zeros((HD, 1), f32)
            for c in range(TT // HD):
                rs = slice(c * HD, (c + 1) * HD)
                vnb = vn_ref[rs, cs]
                s = jnp.dot(w, vnb, preferred_element_type=f32) + bm_ref[:, cs]
                dy = dyb_ref[rs, cs]
                du_ref[rs, cs] = dy * s
                ds = dy * (su_ref[rs, cs] * cu_ref[rs, cs])
                ds16 = ds.astype(bf16)
                dvn_ref[rs, cs] = lax.dot_general(w, ds16, TN, preferred_element_type=f32)
                dw = dw + lax.dot_general(ds16, vnb, NT, preferred_element_type=f32)
                db = db + jnp.sum(ds, axis=1, keepdims=True)
            dw = jnp.where(cc <= r, dw, 0.0)

            @pl.when(i == 0)
            def _():
                dw_ref[g] = dw
                db_ref[:, g:g + 1] = db

            @pl.when(i > 0)
            def _():
                dw_ref[g] += dw
                db_ref[:, g:g + 1] += db

        dvn = dvn_ref[...]
        dxh = dvn * lg_ref[...]
        m1 = jnp.mean(dxh, axis=-1, keepdims=True)
        m2 = jnp.mean(dxh * xhat, axis=-1, keepdims=True)
        dvv = rstd * (dxh - m1 - xhat * m2)
        dz_ref[:, 0:D] = (du_ref[...] * _gelu_grad(su_ref[...], cu_ref[...])).astype(bf16)
        dz_ref[:, D:2 * D] = (dvv * _gelu_grad(sv, cv_ref[...])).astype(bf16)
        _acc_store(i, dlg_ref, jnp.sum(dvn * xhat, axis=0, keepdims=True))
        _acc_store(i, dlb_ref, jnp.sum(dvn, axis=0, keepdims=True))

    return pl.pallas_call(
        body, name=name, grid=(T // TT,),
        in_specs=[_row(), _row(3), _row(4), _vec(), _vec(), _bs((NH, HD, HD), lambda i: (0, 0, 0)), _vec(HD)],
        out_specs=[_row(0, 2 * D), _vec(), _vec(), _bs((NH, HD, HD), lambda i: (0, 0, 0)), _bs((HD, NH), lambda i: (0, 0))],
        out_shape=[S((T, 2 * D), bf16), S((1, D), f32), S((1, D), f32), S((NH, HD, HD), f32), S((HD, NH), f32)],
        scratch_shapes=[pltpu.VMEM((TT, D), bf16)] + [pltpu.VMEM((TT, D), f32)] * 4,
        compiler_params=_cp("arbitrary"))(dyb, z, z, ln_g, ln_b, w_s, bmap)


def _qk_fwd(name, z, qg, kg, bf):
    T = z.shape[0]

    def body(q_ref, k_ref, v_ref, zf_ref, qg_ref, kg_ref, bf_ref, qn_ref, kn_ref, vb_ref, lf_ref):
        for h in range(NH):
            cs = slice(h * HD, (h + 1) * HD)
            for src, gain, dst in ((q_ref, qg_ref, qn_ref), (k_ref, kg_ref, kn_ref)):
                xv = src[:, cs]
                r = lax.rsqrt(jnp.mean(xv * xv, axis=-1, keepdims=True) + RMS_EPS)
                dst[:, cs] = (xv * r * gain[:, cs]).astype(bf16)
        vb_ref[...] = v_ref[...].astype(bf16)
        xf = zf_ref[...] + bf_ref[...]
        lf_ref[...] = jnp.minimum(xf, 0.0) - jnp.log1p(jnp.exp(-jnp.abs(xf)))

    return pl.pallas_call(
        body, name=name, grid=(T // TT,),
        in_specs=[_row(5), _row(6), _row(7), _bs((TT, 128), lambda i: (i, F_OFF // 128)), _vec(), _vec(), _vec(1, 128)],
        out_specs=[_row(), _row(), _row(), _bs((TT, 128), lambda i: (i, 0))],
        out_shape=[S((T, D), bf16), S((T, D), bf16), S((T, D), bf16), S((T, 128), f32)],
        compiler_params=_cp("parallel"))(z, z, z, z, qg, kg, bf)


def _cum_fwd(name, logf):
    T = logf.shape[0]

    def body(lf_ref, ccol_ref, crow_ref, c_ref):
        c = lf_ref[...]
        row = lax.broadcasted_iota(jnp.int32, c.shape, 0)
        s = 1
        while s < T:
            c = c + jnp.where(row >= s, pltpu.roll(c, s, 0), 0.0)
            s *= 2
        c_ref[...] = c
        crow_ref[...] = c.T[0:NH, :]
        for h in range(NH):
            ccol_ref[h] = jnp.broadcast_to(c_ref[:, h:h + 1], (T, 128))

    return pl.pallas_call(body, name=name, out_shape=[S((NH, T, 128), f32), S((NH, T), f32)],
                          scratch_shapes=[pltpu.VMEM((T, 128), f32)],
                          compiler_params=pltpu.CompilerParams(vmem_limit_bytes=VMEM_LIMIT))(logf)


ATT_TILE = 1024


def _fold(x, op=jnp.add):
    acc = x[:, 0:128]
    for t in range(1, x.shape[1] // 128):
        acc = op(acc, x[:, t * 128:(t + 1) * 128])
    return acc


def _to_row(col):
    return jnp.broadcast_to(col, (col.shape[0], 128)).T[0:1, :]


def _causal(t, keys_down=False):
    r = lax.broadcasted_iota(jnp.int32, (t, t), 0)
    c = lax.broadcasted_iota(jnp.int32, (t, t), 1)
    return r <= c if keys_down else c <= r


def _attn_fwd(name, qn, kn, vb, ccol, crow3):
    T = qn.shape[0]
    tq = _tile(T, ATT_TILE)
    nq = T // tq

    def body(q_ref, k_ref, v_ref, cc_ref, cr_ref, o_ref, lse_ref, lser_ref, s_ref):
        qi = pl.program_id(1)
        q = q_ref[...]
        cq = cc_ref[:, 0:1]

        def logits(off):
            s = lax.dot_general(q, k_ref[pl.ds(off, tq), :], NT, preferred_element_type=f32) * ATT_SCALE
            return s + cq - cr_ref[:, pl.ds(off, tq)]

        def below(j, mvec):
            off = pl.multiple_of(j * tq, tq)
            s = logits(off)
            s_ref[:, pl.ds(off, tq)] = s
            return jnp.maximum(mvec, _fold(s, jnp.maximum))

        mvec = lax.fori_loop(0, qi, below, jnp.full((tq, 128), NEG, f32))
        off = pl.multiple_of(qi * tq, tq)
        s = jnp.where(_causal(tq), logits(off), NEG)
        s_ref[:, pl.ds(off, tq)] = s
        m = jnp.max(jnp.maximum(mvec, _fold(s, jnp.maximum)), axis=1, keepdims=True)

        def weigh(j, carry):
            lvec, acc = carry
            off = pl.multiple_of(j * tq, tq)
            p = jnp.exp(s_ref[:, pl.ds(off, tq)] - m)
            acc = acc + jnp.dot(p.astype(bf16), v_ref[pl.ds(off, tq), :], preferred_element_type=f32)
            return lvec + _fold(p), acc

        lvec, acc = lax.fori_loop(0, qi + 1, weigh, (jnp.zeros((tq, 128), f32), jnp.zeros((tq, HD), f32)))
        l = jnp.sum(lvec, axis=1, keepdims=True)
        o_ref[...] = acc / l
        lse = m + jnp.log(l)
        lse_ref[...] = jnp.broadcast_to(lse, (tq, 128))
        lser_ref[...] = _to_row(lse)

    return pl.pallas_call(
        body, name=name, grid=(NH, nq),
        in_specs=[_bs((tq, HD), lambda h, i: (i, h)), _bs((T, HD), lambda h, i: (0, h)), _bs((T, HD), lambda h, i: (0, h)),
                  _bs((None, tq, 128), lambda h, i: (h, i, 0)), _bs((None, 1, T), lambda h, i: (h, 0, 0))],
        out_specs=[_bs((tq, HD), lambda h, i: (i, h)), _bs((None, tq, 128), lambda h, i: (h, i, 0)),
                   _bs((None, 1, tq), lambda h, i: (h, 0, i))],
        out_shape=[S((T, D), f32), S((NH, T, 128), f32), S((NH, 1, T), f32)],
        scratch_shapes=[pltpu.VMEM((tq, T), f32)],
        compiler_params=_cp("parallel", "parallel"))(qn, kn, vb, ccol, crow3)


def _attn_dq(name, qn, kn, vb, do, lse, ccol, crow3):
    T = qn.shape[0]
    tq = _tile(T, ATT_TILE)
    nq = T // tq

    def body(q_ref, k_ref, v_ref, do_ref, lse_ref, cc_ref, cr_ref, dq_ref, dlr_ref, p_ref, dp_ref):
        qi = pl.program_id(1)
        q = q_ref[...]
        do16 = do_ref[...].astype(bf16)
        base = cc_ref[:, 0:1] - lse_ref[:, 0:1]

        def probs(off):
            s = lax.dot_general(q, k_ref[pl.ds(off, tq), :], NT, preferred_element_type=f32) * ATT_SCALE
            return jnp.exp(s + base - cr_ref[:, pl.ds(off, tq)])

        def keep(off, p, dvec):
            dp = lax.dot_general(do16, v_ref[pl.ds(off, tq), :], NT, preferred_element_type=f32)
            p_ref[:, pl.ds(off, tq)] = p
            dp_ref[:, pl.ds(off, tq)] = dp
            return dvec + _fold(p * dp)

        def below(j, dvec):
            off = pl.multiple_of(j * tq, tq)
            return keep(off, probs(off), dvec)

        dvec = lax.fori_loop(0, qi, below, jnp.zeros((tq, 128), f32))
        off = pl.multiple_of(qi * tq, tq)
        dvec = keep(off, jnp.where(_causal(tq), probs(off), 0.0), dvec)
        delta = jnp.sum(dvec, axis=1, keepdims=True)

        def grad(j, acc):
            off = pl.multiple_of(j * tq, tq)
            ds = p_ref[:, pl.ds(off, tq)] * (dp_ref[:, pl.ds(off, tq)] - delta)
            return acc + jnp.dot(ds.astype(bf16), k_ref[pl.ds(off, tq), :], preferred_element_type=f32)

        dq_ref[...] = lax.fori_loop(0, qi + 1, grad, jnp.zeros((tq, HD), f32)) * ATT_SCALE
        dlr_ref[...] = _to_row(delta)

    qb = lambda h, i: (i, h)
    full = lambda h, i: (0, h)
    col = lambda h, i: (h, i, 0)
    return pl.pallas_call(
        body, name=name, grid=(NH, nq),
        in_specs=[_bs((tq, HD), qb), _bs((T, HD), full), _bs((T, HD), full), _bs((tq, HD), qb),
                  _bs((None, tq, 128), col), _bs((None, tq, 128), col), _bs((None, 1, T), lambda h, i: (h, 0, 0))],
        out_specs=[_bs((tq, HD), qb), _bs((None, 1, tq), lambda h, i: (h, 0, i))],
        out_shape=[S((T, D), f32), S((NH, 1, T), f32)],
        scratch_shapes=[pltpu.VMEM((tq, T), f32), pltpu.VMEM((tq, T), f32)],
        compiler_params=_cp("parallel", "parallel"))(qn, kn, vb, do, lse, ccol, crow3)


def _attn_dkv(name, qn, kn, vb, do, lser3, dlr3, ccol, crow3):
    T = qn.shape[0]
    tk = _tile(T, ATT_TILE)
    nk = T // tk

    def body(q_ref, k_ref, v_ref, do_ref, lser_ref, dlr_ref, cc_ref, cr_ref, dk_ref, dv_ref, cs_ref):
        h = pl.program_id(0)
        kj = pl.program_id(1)

        @pl.when((h == 0) & (kj == 0))
        def _():
            cs_ref[...] = jnp.zeros_like(cs_ref)

        kb = k_ref[...]
        vv = v_ref[...]
        ckey = cc_ref[:, 0:1]

        def block(off, diagonal):
            rows = pl.ds(off, tk)
            qb = q_ref[rows, :]
            do16 = do_ref[rows, :].astype(bf16)
            st = lax.dot_general(kb, qb, NT, preferred_element_type=f32) * ATT_SCALE
            pt = jnp.exp(st + (cr_ref[:, rows] - lser_ref[:, rows]) - ckey)
            if diagonal:
                pt = jnp.where(_causal(tk, keys_down=True), pt, 0.0)
            dpt = lax.dot_general(vv, do16, NT, preferred_element_type=f32)
            dst = pt * (dpt - dlr_ref[:, rows])
            ddv = jnp.dot(pt.astype(bf16), do16, preferred_element_type=f32)
            ddk = jnp.dot(dst.astype(bf16), qb, preferred_element_type=f32)
            return ddk, ddv, _fold(dst)

        def above(i, carry):
            ddk, ddv, dcs = block(pl.multiple_of(i * tk, tk), False)
            return carry[0] + ddk, carry[1] + ddv, carry[2] + dcs

        off = pl.multiple_of(kj * tk, tk)
        dk, dv, cs = lax.fori_loop(kj + 1, nk, above, block(off, True))
        dk_ref[...] = dk * ATT_SCALE
        dv_ref[...] = dv
        lane = lax.broadcasted_iota(jnp.int32, (tk, 128), 1)
        cs_ref[pl.ds(off, tk), :] += jnp.where(lane == h, jnp.sum(cs, axis=1, keepdims=True), 0.0)

    full = lambda h, j: (0, h)
    blk = lambda h, j: (j, h)
    row = lambda h, j: (h, 0, 0)
    return pl.pallas_call(
        body, name=name, grid=(NH, nk),
        in_specs=[_bs((T, HD), full), _bs((tk, HD), blk), _bs((tk, HD), blk), _bs((T, HD), full), _bs((None, 1, T), row),
                  _bs((None, 1, T), row), _bs((None, tk, 128), lambda h, j: (h, j, 0)), _bs((None, 1, T), row)],
        out_specs=[_bs((tk, HD), blk), _bs((tk, HD), blk), _bs((T, 128), lambda h, j: (0, 0))],
        out_shape=[S((T, D), f32), S((T, D), f32), S((T, 128), f32)],
        compiler_params=_cp("arbitrary", "arbitrary"))(qn, kn, vb, do, lser3, dlr3, ccol, crow3)


def _forget_bwd(name, cs, z, bf):
    T = cs.shape[0]

    def body(cs_ref, zf_ref, bf_ref, dz_ref, db_ref):
        c = -cs_ref[...]
        row = lax.broadcasted_iota(jnp.int32, c.shape, 0)
        s = 1
        while s < T:
            c = c + jnp.where(row + s < T, pltpu.roll(c, T - s, 0), 0.0)
            s *= 2
        xf = zf_ref[...] + bf_ref[...]
        lane = lax.broadcasted_iota(jnp.int32, c.shape, 1)
        dxf = jnp.where(lane < NH, c / (1.0 + jnp.exp(xf)), 0.0)
        dz_ref[...] = jnp.zeros_like(dz_ref)
        dz_ref[:, 0:128] = dxf.astype(bf16)
        db_ref[...] = jnp.sum(dxf, axis=0, keepdims=True)

    return pl.pallas_call(
        body, name=name, grid=(1,),
        in_specs=[_bs((T, 128), lambda i: (0, 0)), _bs((T, 128), lambda i: (0, F_OFF // 128)), _vec(1, 128)],
        out_specs=[_bs((T, NZ - F_OFF), lambda i: (0, 0)), _vec(1, 128)],
        out_shape=[S((T, NZ - F_OFF), bf16), S((1, 128), f32)], compiler_params=_cp("arbitrary"))(cs, z, bf)


def _qk_bwd(name, dqn, dkn, dv, z, qg, kg):
    T = z.shape[0]

    def body(dq_ref, dk_ref, dv_ref, q_ref, k_ref, qg_ref, kg_ref, dz_ref, dqg_ref, dkg_ref, g_ref):
        i = pl.program_id(0)
        for n, (src, dsrc, gain, dgain) in enumerate(((q_ref, dq_ref, qg_ref, dqg_ref), (k_ref, dk_ref, kg_ref, dkg_ref))):
            for h in range(NH):
                cs = slice(h * HD, (h + 1) * HD)
                xv = src[:, cs]
                r = lax.rsqrt(jnp.mean(xv * xv, axis=-1, keepdims=True) + RMS_EPS)
                xhat = xv * r
                dy = dsrc[:, cs]
                dyg = dy * gain[:, cs]
                m = jnp.mean(dyg * xhat, axis=-1, keepdims=True)
                dz_ref[:, n * D + h * HD:n * D + (h + 1) * HD] = (r * (dyg - xhat * m)).astype(bf16)
                g_ref[:, cs] = jnp.sum(dy * xhat, axis=0, keepdims=True)
            _acc_store(i, dgain, g_ref[...])
        dz_ref[:, 2 * D:3 * D] = dv_ref[...].astype(bf16)

    return pl.pallas_call(
        body, name=name, grid=(T // TT,),
        in_specs=[_row(), _row(), _row(), _row(5), _row(6), _vec(), _vec()],
        out_specs=[_row(0, 3 * D), _vec(), _vec()], out_shape=[S((T, 3 * D), bf16), S((1, D), f32), S((1, D), f32)],
        scratch_shapes=[pltpu.VMEM((1, D), f32)], compiler_params=_cp("arbitrary"))(dqn, dkn, dv, z, z, qg, kg)


GB = GATE_OFF // D


def _merge_fwd(name, ya, yb, yc, z, bg):
    T = z.shape[0]

    def body(ya_ref, yb_ref, yc_ref, g0_ref, g1_ref, g2_ref, bg_ref, o_ref):
        acc = _sigmoid(g0_ref[...] + bg_ref[0:1, :]) * ya_ref[...]
        acc = acc + _sigmoid(g1_ref[...] + bg_ref[1:2, :]) * yb_ref[...]
        acc = acc + _sigmoid(g2_ref[...] + bg_ref[2:3, :]) * yc_ref[...]
        o_ref[...] = acc.astype(bf16)

    return pl.pallas_call(
        body, name=name, grid=(T // TT,),
        in_specs=[_row(), _row(), _row(), _row(GB), _row(GB + 1), _row(GB + 2), _vec(3)],
        out_specs=_row(), out_shape=S((T, D), bf16), compiler_params=_cp("parallel"))(ya, yb, yc, z, z, z, bg)


def _merge_bwd(name, dm, ya, yb, yc, z, bg):
    T = z.shape[0]

    def body(dm_ref, ya_ref, yb_ref, yc_ref, g0_ref, g1_ref, g2_ref, bg_ref, dya_ref, dyb_ref, dyc_ref, dz_ref, db_ref):
        i = pl.program_id(0)
        dm_v = dm_ref[...]
        dbs = []
        for n, (y_ref, g_ref, dy_ref) in enumerate(((ya_ref, g0_ref, dya_ref), (yb_ref, g1_ref, dyb_ref),
                                                    (yc_ref, g2_ref, dyc_ref))):
            gate = _sigmoid(g_ref[...] + bg_ref[n:n + 1, :])
            dy_ref[...] = (dm_v * gate).astype(bf16)
            dl = dm_v * y_ref[...] * gate * (1.0 - gate)
            dz_ref[:, n * D:(n + 1) * D] = dl.astype(bf16)
            dbs.append(jnp.sum(dl, axis=0, keepdims=True))
        _acc_store(i, db_ref, jnp.concatenate(dbs, axis=0))

    return pl.pallas_call(
        body, name=name, grid=(T // TT,),
        in_specs=[_row(), _row(), _row(), _row(), _row(GB), _row(GB + 1), _row(GB + 2), _vec(3)],
        out_specs=[_row(), _row(), _row(), _row(0, 3 * D), _vec(3)],
        out_shape=[S((T, D), bf16), S((T, D), bf16), S((T, D), bf16), S((T, 3 * D), bf16), S((3, D), f32)],
        compiler_params=_cp("arbitrary"))(dm, ya, yb, yc, z, z, z, bg)


SMALL_NAMES = ("ffn1_norm", "mix_norm", "b_forget", "b_gate", "conv_w", "sgu_ln_g", "sgu_ln_b", "sgu_w", "sgu_b",
               "q_norm_g", "k_norm_g", "ffn2_norm")


def _small_params(p):
    out = {n: p[n].reshape(1, D) for n in ("ffn1_norm", "mix_norm", "ffn2_norm", "sgu_ln_g", "sgu_ln_b", "q_norm_g", "k_norm_g")}
    out["b_forget"] = jnp.pad(p["b_forget"].reshape(1, NH), ((0, 0), (0, 128 - NH)))
    out["b_gate"] = p["b_gate"]
    out["conv_w"] = p["conv_w"]
    out["sgu_w"] = p["sgu_w"]
    out["bmap"] = jnp.repeat(p["sgu_b"].T, HD, axis=1)
    return out


def _small_grads_natural(sg):
    out = {n: sg[n].reshape(D) for n in ("ffn1_norm", "mix_norm", "ffn2_norm", "sgu_ln_g", "sgu_ln_b")}
    out["q_norm_g"] = sg["q_norm_g"].reshape(NH, HD)
    out["k_norm_g"] = sg["k_norm_g"].reshape(NH, HD)
    out["b_forget"] = sg["b_forget"][0, :NH]
    out["b_gate"] = sg["b_gate"]
    out["conv_w"] = sg["conv_w"]
    out["sgu_w"] = sg["sgu_w"]
    out["sgu_b"] = sg["sgu_b"]
    return out


SQ_TM = 1024


def _sq_fwd(name, a, wsq, l, n, res=None):
    T = a.shape[0]
    tm = _tile(T, SQ_TM)
    return _mm(name, a, wsq, grid=(T // tm, 1, 1), a_spec=_bs((tm, D), lambda i, j, k: (i, 0)),
               b_spec=_bs((None, None, D, D), lambda i, j, k: (l, n, 0, 0)),
               out_shape=S((T, D), f32), out_spec=_bs((tm, D), lambda i, j, k: (i, 0)), dims=NN, acc_shape=None,
               res=res, res_spec=_bs((tm, D), lambda i, j, k: (i, 0)))


def _sq_bwd_in(name, dy, wsq, l, n):
    T = dy.shape[0]
    tm = _tile(T, SQ_TM)
    return _mm(name, dy, wsq, grid=(T // tm, 1, 1), a_spec=_bs((tm, D), lambda i, j, k: (i, 0)),
               b_spec=_bs((None, None, D, D), lambda i, j, k: (l, n, 0, 0)),
               out_shape=S((T, D), f32), out_spec=_bs((tm, D), lambda i, j, k: (i, 0)), dims=NT, acc_shape=None)


def _sq_bwd_w(name, a, dy, gbuf, l):
    T = a.shape[0]
    return _mm(name, a, dy, grid=(NDEV // 2, 1, 1), a_spec=_bs((T, 256), lambda i, j, k: (0, i)),
               b_spec=_bs((T, D), lambda i, j, k: (0, 0)), out_shape=S(gbuf.shape, bf16),
               out_spec=_bs((2, None, None, 128, D), lambda i, j, k: (0, i, l, 0, 0)),
               dims=TN, acc_shape=None, alias=gbuf, split_rows=128)


def _ffn_fwd(tag, x, g, wgu, wd, l):
    T = x.shape[0]
    h = _rms_fwd(tag + "_rms", x, g)
    gu, a = _swiglu_fwd(tag + "_gu", h, wgu, l)
    tm = _tile(T, 1024)
    xo = _mm(tag + "_down", a, wd, grid=(T // tm, 1, 4), a_spec=_bs((None, tm, GU), lambda i, j, k: (k, i, 0)),
             b_spec=_bs((None, None, GU, D), lambda i, j, k: (l, k, 0, 0)), out_shape=S((T, D), f32),
             out_spec=_bs((tm, D), lambda i, j, k: (i, 0)), dims=NN, acc_shape=(tm, D), res=x,
             res_spec=_bs((tm, D), lambda i, j, k: (i, 0)), alpha=0.5)
    return xo, (h, gu, a)


def _ffn_bwd(tag, dxo, x, g, wgu, wd, l, saved, g_gu, g_d, ship, hook, hook_at_once=False):
    h, gu, a = saved
    T = x.shape[0]
    g_d = _mm(tag + "_dwd", a, dxo, grid=(4, 1, 1), a_spec=_bs((None, T, GU), lambda i, j, k: (i, 0, 0)),
              b_spec=_bs((T, D), lambda i, j, k: (0, 0)), out_shape=S(g_d.shape, bf16),
              out_spec=_bs((2, None, None, GU // 2, D), lambda i, j, k: (0, i, l, 0, 0)), dims=TN, acc_shape=None,
              alpha=0.5, alias=g_d, split_rows=GU // 2)
    dgu = _swiglu_bwd(tag + "_dgu", dxo, wd, gu, l).reshape(NDEV, T, GU)
    g_gu = _mm(tag + "_dwgu", dgu, h, grid=(NDEV, 1, 1), a_spec=_bs((None, T, GU), lambda i, j, k: (i, 0, 0)),
               b_spec=_bs((T, D), lambda i, j, k: (0, 0)), out_shape=S(g_gu.shape, bf16),
               out_spec=_bs((None, None, None, GU, D), lambda i, j, k: (i % 2, i // 2, l, 0, 0)), dims=TN,
               acc_shape=None, alias=g_gu)
    dxo = ship([g_gu, g_d], dxo)
    if hook_at_once:
        dxo = hook(dxo, g_gu)
    dh = _mm(tag + "_dh", dgu, wgu, grid=(1, 1, NDEV), a_spec=_bs((None, T, GU), lambda i, j, k: (k, 0, 0)),
             b_spec=_bs((None, None, GU, D), lambda i, j, k: (l, k, 0, 0)), out_shape=S((T, D), f32),
             out_spec=_bs((T, D), lambda i, j, k: (0, 0)), dims=NN, acc_shape=(T, D))
    if not hook_at_once:
        dxo = hook(dxo, dh)
    return _rms_bwd(tag + "_drms", dh, x, g, dxo)


def _mixer_fwd(tag, x, p, win, wsq, l, after_in=None):
    T = x.shape[0]
    h = _rms_fwd(tag + "_rms", x, p["mix_norm"])
    tn = 512
    z = _mm(tag + "_in", h, win, grid=(1, NZ // tn, 1), a_spec=_bs((T, D), lambda i, j, k: (0, 0)),
            b_spec=_bs((D, tn), lambda i, j, k: (0, j)), out_shape=S((T, NZ), f32),
            out_spec=_bs((T, tn), lambda i, j, k: (0, j)), dims=NN, acc_shape=None)
    if after_in is not None:
        z = after_in(z)
    ya_in = _conv_fwd(tag + "_conv", z, p["conv_w"])
    yb_in = _sgu_fwd(tag + "_sgu", z, p["sgu_ln_g"], p["sgu_ln_b"], p["sgu_w"], p["bmap"])
    qn, kn, vb, logf = _qk_fwd(tag + "_qk", z, p["q_norm_g"], p["k_norm_g"], p["b_forget"])
    ccol, crow = _cum_fwd(tag + "_cum", logf)
    crow3 = crow.reshape(NH, 1, T)
    o, lse, lser = _attn_fwd(tag + "_attn", qn, kn, vb, ccol, crow3)
    if callable(wsq):
        wsq = wsq(o)
    ya = _sq_fwd(tag + "_oconv", ya_in, wsq, l, 0)
    yb = _sq_fwd(tag + "_osgu", yb_in, wsq, l, 1)
    yc = _sq_fwd(tag + "_oattn", o, wsq, l, 2)
    merged = _merge_fwd(tag + "_merge", ya, yb, yc, z, p["b_gate"])
    xo = _sq_fwd(tag + "_o", merged, wsq, l, 3, res=x)
    return xo, (h, z, ya_in, yb_in, qn, kn, vb, ccol, crow3, o, lse, lser, ya, yb, yc, merged)


def _mixer_bwd(tag, dxo, x, p, win, wsq, l, saved, gsq, ship_sq, ship_in, hook):
    h, z, ya_in, yb_in, qn, kn, vb, ccol, crow3, o, lse, lser, ya, yb, yc, merged = saved
    T = x.shape[0]
    sg = {}
    dm = _sq_bwd_in(tag + "_dmerged", dxo, wsq, l, 3)
    gsq[3] = _sq_bwd_w(tag + "_dwo", merged, dxo, gsq[3], l)
    dya, dyb, dyc, dz_g, sg["b_gate"] = _merge_bwd(tag + "_dmerge", dm, ya, yb, yc, z, p["b_gate"])
    d_ya_in = _sq_bwd_in(tag + "_dconv_in", dya, wsq, l, 0)
    gsq[0] = _sq_bwd_w(tag + "_dwoc", ya_in, dya, gsq[0], l)
    d_yb_in = _sq_bwd_in(tag + "_dsgu_in", dyb, wsq, l, 1)
    gsq[1] = _sq_bwd_w(tag + "_dwos", yb_in, dyb, gsq[1], l)
    d_o = _sq_bwd_in(tag + "_dattn_in", dyc, wsq, l, 2)
    gsq[2] = _sq_bwd_w(tag + "_dwoa", o, dyc, gsq[2], l)
    dxo = ship_sq(gsq, dxo)
    dz_c, sg["conv_w"] = _conv_bwd(tag + "_dconv", d_ya_in, z, p["conv_w"])
    dxo = hook(dxo, dz_c)
    dz_s, sg["sgu_ln_g"], sg["sgu_ln_b"], sg["sgu_w"], db_t = _sgu_bwd(
        tag + "_dsgu", d_yb_in, z, p["sgu_ln_g"], p["sgu_ln_b"], p["sgu_w"], p["bmap"])
    sg["sgu_b"] = db_t.T
    dqn, dlr = _attn_dq(tag + "_dattn_q", qn, kn, vb, d_o, lse, ccol, crow3)
    dkn, dv, cs = _attn_dkv(tag + "_dattn_kv", qn, kn, vb, d_o, lser, dlr, ccol, crow3)
    dz_f, sg["b_forget"] = _forget_bwd(tag + "_dforget", cs, z, p["b_forget"])
    dz_q, sg["q_norm_g"], sg["k_norm_g"] = _qk_bwd(tag + "_dqk", dqn, dkn, dv, z, p["q_norm_g"], p["k_norm_g"])
    dz = [dz_c, dz_s, dz_q, dz_g, dz_f]
    dwin = _dz_matmul(tag + "_dwin", dz, h, True)
    dxo = ship_in(dwin, dxo)
    dh = _dz_matmul(tag + "_dh", dz, win, False)
    dxo = hook(dxo, dh)
    dx, sg["mix_norm"] = _rms_bwd(tag + "_drms", dh, x, p["mix_norm"], dxo)
    return dx, sg


ANY = pl.BlockSpec(memory_space=pl.ANY)
HBM = pl.BlockSpec(memory_space=pltpu.HBM)
SEM = pl.BlockSpec(memory_space=pltpu.SEMAPHORE)
EFFECT = pltpu.SideEffectType.DATAFLOW_SIDE_EFFECTING


def _place():
    return lax.axis_index("x"), lax.axis_index("y"), lax.axis_index("c")


NEAR = 4


def _others(x, y, c):
    return [(x, y, 1 - c), (1 - x, y, c), (x, 1 - y, c), (1 - x, 1 - y, c)]


def _gather_start(name, groups, carry=None):
    sizes = [len(g) for g in groups]
    srcs = [s for g in groups for s, _ in g]
    lands = [b for g in groups for _, b in g]
    n, ng = len(srcs), len(groups)
    held = srcs + lands + ([] if carry is None else [carry])
    nh = len(held)

    def body(*refs):
        src_refs, land_refs = refs[:n], refs[n:2 * n]
        send, recv = refs[nh:nh + ng], refs[nh + ng:nh + 2 * ng]
        x, y, c = _place()
        me = 4 * x + 2 * y + c
        u = 0
        for g, size in enumerate(sizes):
            for i in range(size):
                for k, peer in enumerate(_others(x, y, c)):
                    pltpu.make_async_remote_copy(src_ref=src_refs[u], dst_ref=land_refs[u].at[me],
                                                 send_sem=send[g].at[i * NEAR + k], recv_sem=recv[g].at[i * NEAR + k],
                                                 device_id=peer, device_id_type=MESH).start()
                u += 1

    sems = [pltpu.SemaphoreType.DMA((size * NEAR,)) for size in sizes]
    out = pl.pallas_call(
        body, name=name, in_specs=[HBM] * nh, out_specs=[SEM] * (2 * ng) + [HBM] * nh,
        out_shape=sems + sems + [pltpu.HBM(a.shape, a.dtype) for a in held],
        input_output_aliases={i: 2 * ng + i for i in range(nh)},
        compiler_params=pltpu.CompilerParams(has_side_effects=EFFECT),
    )(*[pltpu.with_memory_space_constraint(a, pltpu.HBM) for a in held])
    res, pos = [], 0
    for g, size in enumerate(sizes):
        res.append((out[g], out[ng + g], out[2 * ng + pos:2 * ng + pos + size], out[2 * ng + n + pos:2 * ng + n + pos + size]))
        pos += size
    return res if carry is None else (res, out[2 * ng + 2 * n])


def _gather_wait(name, started, after=None):
    send, recv, srcs, lands = started
    n = len(srcs)

    def body(*refs):
        src_refs, land_refs = refs[:n], refs[n:2 * n]
        send_ref, recv_ref = refs[2 * n], refs[2 * n + 1]
        x, y, c = _place()
        for i in range(n):
            for k, (px, py, pc) in enumerate(_others(x, y, c)):
                cp = pltpu.make_async_remote_copy(src_ref=src_refs[i], dst_ref=land_refs[i].at[4 * px + 2 * py + pc],
                                                  send_sem=send_ref.at[i * NEAR + k], recv_sem=recv_ref.at[i * NEAR + k],
                                                  device_id=(px, py, pc), device_id_type=MESH)
                cp.wait_send()
                cp.wait_recv()

    extra = [] if after is None else (list(after) if isinstance(after, (list, tuple)) else [after])
    out = pl.pallas_call(
        body, name=name, in_specs=[HBM] * (2 * n) + [SEM, SEM] + [ANY] * len(extra), out_specs=[HBM] * (2 * n),
        out_shape=[pltpu.HBM(a.shape, a.dtype) for a in list(srcs) + list(lands)],
        input_output_aliases={i: i for i in range(2 * n)},
        compiler_params=pltpu.CompilerParams(has_side_effects=EFFECT),
    )(*srcs, *lands, send, recv, *extra)
    return out[n:]


def _forward_copy(have, full, send, recv, i, j, core):
    x, y, c = _place()
    chip = [(1 - x, y), (x, 1 - y), (1 - x, 1 - y)][j]
    slot = 4 * chip[0] + 2 * chip[1] + core
    return pltpu.make_async_remote_copy(src_ref=have[i].at[slot], dst_ref=full[i].at[slot], send_sem=send.at[i * 3 + j],
                                        recv_sem=recv.at[i * 3 + j], device_id=(x, y, 1 - c), device_id_type=MESH)


def _gather_forward(name, lands):
    n = len(lands)

    def body(*refs):
        have, full = refs[:n], refs[n:2 * n]
        send, recv = refs[2 * n], refs[2 * n + 1]
        c = lax.axis_index("c")
        for i in range(n):
            for j in range(3):
                _forward_copy(have, full, send, recv, i, j, c).start()
        for i in range(n):
            for j in range(3):
                _forward_copy(have, full, send, recv, i, j, c).wait_send()
                _forward_copy(have, full, send, recv, i, j, 1 - c).wait_recv()

    return pl.pallas_call(
        body, name=name, in_specs=[ANY] * n, out_specs=[ANY] * n, out_shape=[S(a.shape, a.dtype) for a in lands],
        input_output_aliases={i: i for i in range(n)},
        scratch_shapes=[pltpu.SemaphoreType.DMA((n * 3,)), pltpu.SemaphoreType.DMA((n * 3,))],
    )(*lands)


def _forward_start(name, lands, carry):
    n = len(lands)
    held = list(lands) + [carry]

    def body(*refs):
        c = lax.axis_index("c")
        for i in range(n):
            for j in range(3):
                _forward_copy(refs[:n], refs[:n], refs[n + 1], refs[n + 2], i, j, c).start()

    sems = [pltpu.SemaphoreType.DMA((n * 3,))] * 2
    out = pl.pallas_call(
        body, name=name, in_specs=[HBM] * (n + 1), out_specs=[SEM, SEM] + [HBM] * (n + 1),
        out_shape=sems + [pltpu.HBM(a.shape, a.dtype) for a in held],
        input_output_aliases={i: 2 + i for i in range(n + 1)},
        compiler_params=pltpu.CompilerParams(has_side_effects=EFFECT),
    )(*[pltpu.with_memory_space_constraint(a, pltpu.HBM) for a in held])
    return (out[0], out[1], out[2:2 + n]), out[2 + n]


def _forward_wait(name, started, after):
    send, recv, lands = started
    n = len(lands)

    def body(*refs):
        c = lax.axis_index("c")
        for i in range(n):
            for j in range(3):
                _forward_copy(refs[:n], refs[:n], refs[n], refs[n + 1], i, j, c).wait_send()
                _forward_copy(refs[:n], refs[:n], refs[n], refs[n + 1], i, j, 1 - c).wait_recv()

    return pl.pallas_call(
        body, name=name, in_specs=[HBM] * n + [SEM, SEM, ANY], out_specs=[HBM] * n,
        out_shape=[pltpu.HBM(a.shape, a.dtype) for a in lands],
        input_output_aliases={i: i for i in range(n)},
        compiler_params=pltpu.CompilerParams(has_side_effects=EFFECT),
    )(*lands, send, recv, after)


def _pair_copies(ins, outs, send, recv):
    x, y, c = _place()
    return [pltpu.make_async_remote_copy(src_ref=ins[u].at[1 - c], dst_ref=outs[u], send_sem=send.at[u],
                                         recv_sem=recv.at[u], device_id=(x, y, 1 - c), device_id_type=MESH)
            for u in range(len(ins))]


def _pair_start(name, gs, carry):
    n = len(gs)
    held = list(gs) + [lax.empty(g.shape[1:], g.dtype) for g in gs] + [carry]

    def body(*refs):
        for cp in _pair_copies(refs[:n], refs[n:2 * n], refs[2 * n + 1], refs[2 * n + 2]):
            cp.start()

    sems = [pltpu.SemaphoreType.DMA((n,))] * 2
    out = pl.pallas_call(
        body, name=name, in_specs=[HBM] * len(held), out_specs=[SEM, SEM] + [HBM] * len(held),
        out_shape=sems + [pltpu.HBM(a.shape, a.dtype) for a in held],
        input_output_aliases={i: 2 + i for i in range(len(held))},
        compiler_params=pltpu.CompilerParams(has_side_effects=EFFECT),
    )(*[pltpu.with_memory_space_constraint(a, pltpu.HBM) for a in held])
    return (out[0], out[1], out[2:2 + n], out[2 + n:2 + 2 * n]), out[2 + 2 * n]


def _pair_wait(name, started, after):
    send, recv, gs, lands = started
    n = len(gs)

    def body(*refs):
        for cp in _pair_copies(refs[:n], refs[n:2 * n], refs[2 * n], refs[2 * n + 1]):
            cp.wait_send()
            cp.wait_recv()

    out = pl.pallas_call(
        body, name=name, in_specs=[HBM] * (2 * n) + [SEM, SEM, ANY], out_specs=[HBM] * (2 * n),
        out_shape=[pltpu.HBM(a.shape, a.dtype) for a in list(gs) + list(lands)],
        input_output_aliases={i: i for i in range(2 * n)},
        compiler_params=pltpu.CompilerParams(has_side_effects=EFFECT),
    )(*gs, *lands, send, recv, after)
    return out[:n], out[n:]


def _row_tile(r, c):
    if c > D and r % 128 == 0:
        return 128
    return 256 if r % 256 == 0 else (GU // 2 if r % (GU // 2) == 0 else r)


def _pair_sums(name, core, gs, r1s):
    n = len(gs)

    def body(core_ref, *refs):
        for g_ref, r_ref, o_ref in zip(refs[:n], refs[n:2 * n], refs[2 * n:]):
            o_ref[...] = (g_ref[...].astype(f32) + r_ref[...].astype(f32)).astype(bf16)

    shapes = [g.shape[3:] for g in gs]
    outs = pl.pallas_call(
        body, name=name,
        grid_spec=pltpu.PrefetchScalarGridSpec(
            num_scalar_prefetch=1, grid=(4,),
            in_specs=[_bs((None, None, None) + s, lambda b, cr: (cr[0], b, 0, 0, 0)) for s in shapes]
            + [_bs((None, None) + s, lambda b, cr: (b, 0, 0, 0)) for s in shapes],
            out_specs=[_bs((None, None) + s, lambda b, cr: (b, 0, 0, 0)) for s in shapes]),
        out_shape=[S((4, 1) + s, bf16) for s in shapes], compiler_params=_cp("parallel"))(core, *gs, *r1s)
    return list(outs)


def _scatter_copies(ins, outs, send, recv):
    x, y, c = _place()
    chips = [(1 - x, y), (x, 1 - y), (1 - x, 1 - y)]
    return [pltpu.make_async_remote_copy(src_ref=ins[u].at[2 * chip[0] + chip[1]], dst_ref=outs[u].at[k],
                                         send_sem=send.at[u * 3 + k], recv_sem=recv.at[u * 3 + k],
                                         device_id=(*chip, c), device_id_type=MESH)
            for u in range(len(ins)) for k, chip in enumerate(chips)]


def _scatter_start(name, ss, carry):
    n = len(ss)
    lands = [lax.empty((3,) + s.shape[1:], s.dtype) for s in ss]
    held = list(ss) + lands + [carry]

    def body(*refs):
        for cp in _scatter_copies(refs[:n], refs[n:2 * n], refs[2 * n + 1], refs[2 * n + 2]):
            cp.start()

    sems = [pltpu.SemaphoreType.DMA((n * 3,))] * 2
    out = pl.pallas_call(
        body, name=name, in_specs=[HBM] * len(held), out_specs=[SEM, SEM] + [HBM] * len(held),
        out_shape=sems + [pltpu.HBM(a.shape, a.dtype) for a in held],
        input_output_aliases={i: 2 + i for i in range(len(held))},
        compiler_params=pltpu.CompilerParams(has_side_effects=EFFECT),
    )(*[pltpu.with_memory_space_constraint(a, pltpu.HBM) for a in held])
    return (out[0], out[1], out[2:2 + n], out[2 + n:2 + 2 * n]), out[2 + 2 * n]


def _scatter_wait(name, started, after):
    send, recv, srcs, lands = started
    n = len(srcs)

    def body(*refs):
        for cp in _scatter_copies(refs[:n], refs[n:2 * n], refs[2 * n], refs[2 * n + 1]):
            cp.wait_send()
            cp.wait_recv()

    out = pl.pallas_call(
        body, name=name, in_specs=[HBM] * (2 * n) + [SEM, SEM, ANY], out_specs=[HBM] * (2 * n),
        out_shape=[pltpu.HBM(a.shape, a.dtype) for a in list(srcs) + list(lands)],
        input_output_aliases={i: i for i in range(2 * n)},
        compiler_params=pltpu.CompilerParams(has_side_effects=EFFECT),
    )(*srcs, *lands, send, recv, after)
    return out[:n], out[n:]


def _sum_blocks(name, blocks):
    def body(b_ref, o_ref):
        acc = b_ref[0]
        for d in range(1, NDEV):
            acc = acc + b_ref[d]
        o_ref[...] = acc

    return pl.pallas_call(body, name=name, out_shape=S(blocks.shape[1:], f32),
                          compiler_params=pltpu.CompilerParams(vmem_limit_bytes=VMEM_LIMIT))(blocks)


def _adam_math(w, g, m, v):
    m = ADAM_B1 * m + (1.0 - ADAM_B1) * g
    v = ADAM_B2 * v + (1.0 - ADAM_B2) * (g * g)
    m_hat = m / (1.0 - ADAM_B1 ** ADAM_STEP)
    v_hat = v / (1.0 - ADAM_B2 ** ADAM_STEP)
    delta = -ADAM_LR * (m_hat / (jnp.sqrt(v_hat) + ADAM_EPS) + ADAM_WD * w)
    return delta, m, v


def _adamw(name, chip, w, m, v, parts):
    _, r, c = w.shape
    tr = _row_tile(r, c)

    def body(chip_ref, w_ref, m_ref, v_ref, *refs):
        sets, (g_ref, d_ref, mo_ref, vo_ref) = (refs[0:4], refs[4:8]), refs[8:]
        for l in range(2):
            @pl.when(pl.program_id(0) == l)
            def _():
                s_ref, r0_ref, r1_ref, r2_ref = sets[l]
                g = ((s_ref[...].astype(f32) + r0_ref[...].astype(f32)) + r1_ref[...].astype(f32)) + r2_ref[...].astype(f32)
                g_ref[...] = g
                d_ref[...], mo_ref[...], vo_ref[...] = _adam_math(w_ref[...], g, m_ref[...], v_ref[...])

    blk = _bs((None, tr, c), lambda l, i, cr: (l, i, 0))
    operands, specs = [], []
    for n in range(2):
        row = (lambda l, i: i * (1 - l)) if n == 0 else (lambda l, i: i * l)
        sums, r2 = parts[n]
        operands += [sums, r2, r2, r2]
        specs.append(_bs((None, None, tr, c), functools.partial(lambda l, i, cr, row: (cr[0], 0, row(l, i), 0), row=row)))
        specs += [_bs((None, None, tr, c), functools.partial(lambda l, i, cr, k, row: (k, 0, row(l, i), 0), k=k, row=row))
                  for k in range(3)]
    return pl.pallas_call(
        body, name=name,
        grid_spec=pltpu.PrefetchScalarGridSpec(num_scalar_prefetch=1, grid=(2, r // tr), in_specs=[blk, blk, blk] + specs,
                                               out_specs=[blk] * 4),
        out_shape=[S(w.shape, f32)] * 4, compiler_params=_cp("arbitrary", "arbitrary"),
    )(chip, w, m, v, *operands)


def _adamw_small(name, w, g, m, v):
    def body(w_ref, g_ref, m_ref, v_ref, d_ref, mo_ref, vo_ref):
        d_ref[...], mo_ref[...], vo_ref[...] = _adam_math(w_ref[...], g_ref[...], m_ref[...], v_ref[...])

    return pl.pallas_call(body, name=name, out_shape=[S(w.shape, f32)] * 3,
                          compiler_params=pltpu.CompilerParams(vmem_limit_bytes=VMEM_LIMIT))(w, g, m, v)


WEIGHT_NAMES = ("ffn1_norm", "ffn1_w_gu", "ffn1_w_down", "mix_norm", "w_in", "b_forget", "b_gate", "conv_w", "sgu_ln_g",
                "sgu_ln_b", "sgu_w", "sgu_b", "q_norm_g", "k_norm_g", "w_out_conv", "w_out_sgu", "w_out_attn", "w_o",
                "ffn2_norm", "ffn2_w_gu", "ffn2_w_down")
BIG = {"ffn1_w_gu": "gu1", "ffn2_w_gu": "gu2", "ffn1_w_down": "d1", "ffn2_w_down": "d2", "w_in": "in",
       "w_out_conv": "oc", "w_out_sgu": "os", "w_out_attn": "oa", "w_o": "o"}
BIG_KEYS = ("gu1", "gu2", "d1", "d2", "in", "oc", "os", "oa", "o")
REPLICATED_SMALL = ("ffn1_norm", "mix_norm", "b_forget", "sgu_ln_g", "sgu_ln_b", "sgu_w", "sgu_b", "q_norm_g",
                    "k_norm_g", "ffn2_norm")
SHARDED_SMALL = ("b_gate", "conv_w")
TRANSPOSED = ("gu1", "gu2")


def _packed_rows(shape):
    size = 1
    for s_ in shape:
        size *= s_
    return size, -(-size // 1024) * 8


def _pack(arrays):
    pieces = []
    for a in arrays:
        size, rows = _packed_rows(a.shape)
        pieces.append(jnp.pad(a.reshape(-1).astype(f32), (0, rows * 128 - size)).reshape(rows, 128))
    return jnp.concatenate(pieces, axis=0)


def _unpack(packed, shapes):
    out, pos = [], 0
    for shp in shapes:
        size, rows = _packed_rows(shp)
        out.append(packed[pos:pos + rows].reshape(-1)[:size].reshape(shp))
        pos += rows
    return out


def _natural_runs(a, b):
    runs = []
    while a < b:
        d = a // INB
        e = min(b, (d + 1) * INB)
        runs.append((d, a - d * INB, e - d * INB))
        a = e
    return runs


def _win_runs():
    runs = _natural_runs(0, GATE_OFF) + _natural_runs(GATE_OFF + NH, NIN) + _natural_runs(GATE_OFF, GATE_OFF + NH)
    out, pos = [], 0
    for d, a, b in runs:
        out.append((d, a, b, pos))
        pos += b - a
    return out


RELAYOUT_ROWS = 128


def _win_kernel_layout(name, wg):
    def body(w_ref, o_ref):
        for d, a, b, pos in _win_runs():
            o_ref[:, pos:pos + (b - a)] = w_ref[d, :, a:b]
        o_ref[:, NIN:NZ] = jnp.zeros((RELAYOUT_ROWS, NZ - NIN), wg.dtype)

    return pl.pallas_call(
        body, name=name, grid=(D // RELAYOUT_ROWS,), in_specs=[_bs((NDEV, RELAYOUT_ROWS, INB), lambda i: (0, i, 0))],
        out_specs=_bs((RELAYOUT_ROWS, NZ), lambda i: (i, 0)), out_shape=S((D, NZ), wg.dtype),
        compiler_params=_cp("parallel"))(wg)


def _win_device_blocks(name, dw):
    def body(dw_ref, o_ref):
        for d, a, b, pos in _win_runs():
            o_ref[d % 2, d // 2, :, a:b] = dw_ref[:, pos:pos + (b - a)]

    out = pl.pallas_call(
        body, name=name, grid=(D // RELAYOUT_ROWS,), in_specs=[_bs((RELAYOUT_ROWS, NZ), lambda i: (i, 0))],
        out_specs=_bs((2, 4, RELAYOUT_ROWS, INB), lambda i: (0, 0, i, 0)), out_shape=S((2, 4, D, INB), dw.dtype),
        compiler_params=_cp("parallel"))(dw)
    return out.reshape(2, 4, 1, D, INB)


def kernel(x, ffn1_norm, ffn1_w_gu, ffn1_w_down, mix_norm, w_in, b_forget, b_gate, conv_w, sgu_ln_g, sgu_ln_b, sgu_w, sgu_b, q_norm_g, k_norm_g, w_out_conv, w_out_sgu, w_out_attn, w_o, ffn2_norm, ffn2_w_gu, ffn2_w_down, loss_target, m_ffn1_norm, m_ffn1_w_gu, m_ffn1_w_down, m_mix_norm, m_w_in, m_b_forget, m_b_gate, m_conv_w, m_sgu_ln_g, m_sgu_ln_b, m_sgu_w, m_sgu_b, m_q_norm_g, m_k_norm_g, m_w_out_conv, m_w_out_sgu, m_w_out_attn, m_w_o, m_ffn2_norm, m_ffn2_w_gu, m_ffn2_w_down, v_ffn1_norm, v_ffn1_w_gu, v_ffn1_w_down, v_mix_norm, v_w_in, v_b_forget, v_b_gate, v_conv_w, v_sgu_ln_g, v_sgu_ln_b, v_sgu_w, v_sgu_b, v_q_norm_g, v_k_norm_g, v_w_out_conv, v_w_out_sgu, v_w_out_attn, v_w_o, v_ffn2_norm, v_ffn2_w_gu, v_ffn2_w_down):
    w = dict(zip(WEIGHT_NAMES, (ffn1_norm, ffn1_w_gu, ffn1_w_down, mix_norm, w_in, b_forget, b_gate, conv_w, sgu_ln_g,
                                sgu_ln_b, sgu_w, sgu_b, q_norm_g, k_norm_g, w_out_conv, w_out_sgu, w_out_attn, w_o,
                                ffn2_norm, ffn2_w_gu, ffn2_w_down)))
    mom = dict(zip(WEIGHT_NAMES, (m_ffn1_norm, m_ffn1_w_gu, m_ffn1_w_down, m_mix_norm, m_w_in, m_b_forget, m_b_gate,
                                  m_conv_w, m_sgu_ln_g, m_sgu_ln_b, m_sgu_w, m_sgu_b, m_q_norm_g, m_k_norm_g,
                                  m_w_out_conv, m_w_out_sgu, m_w_out_attn, m_w_o, m_ffn2_norm, m_ffn2_w_gu,
                                  m_ffn2_w_down)))
    var = dict(zip(WEIGHT_NAMES, (v_ffn1_norm, v_ffn1_w_gu, v_ffn1_w_down, v_mix_norm, v_w_in, v_b_forget, v_b_gate,
                                  v_conv_w, v_sgu_ln_g, v_sgu_ln_b, v_sgu_w, v_sgu_b, v_q_norm_g, v_k_norm_g,
                                  v_w_out_conv, v_w_out_sgu, v_w_out_attn, v_w_o, v_ffn2_norm, v_ffn2_w_gu,
                                  v_ffn2_w_down)))
    px, py, pc = _place()
    dev = 4 * px + 2 * py + pc
    chip = 2 * px + py

    big_names = [n for n in WEIGHT_NAMES if n in BIG]
    key_name = {BIG[n]: n for n in big_names}
    group_keys = (("gu1", "d1"), ("in", "small"), ("gu2", "d2", "oc", "os", "oa", "o"))

    def source(key, l):
        if key == "small":
            return jnp.concatenate([w["b_gate"][l], w["conv_w"][l], jnp.zeros((2, 128), f32)], axis=0)
        block = w[key_name[key]][l]
        return (block.T if key in TRANSPOSED else block).astype(bf16)

    def landing(src):
        return lax.dynamic_update_slice(lax.empty((NDEV,) + src.shape, src.dtype), src[None], (dev, 0, 0))

    groups = [[(s, landing(s)) for s in (source(k, l) for k in keys)] for l in range(2) for keys in group_keys]
    (send0, recv0, srcs0, lands0), = _gather_start("gather_start_first", groups[:1])
    rest, src0 = _gather_start("gather_start_rest", groups[1:], carry=srcs0[0])
    started = [(send0, recv0, [src0] + list(srcs0[1:]), lands0)] + rest

    early = {}

    order = REPLICATED_SMALL + SHARDED_SMALL
    wp, mp, vp = (_pack([t[n] for n in order]) for t in (w, mom, var))
    while_idle = [mom["w_in"], var["w_in"], wp, mp, vp]

    def prefetch(l, part, act):
        got = _gather_wait(f"gather_wait_{l}_{part}", started[3 * l + part], act)
        early[l, part], act = _forward_start(f"forward_start_{l}_{part}", got, act)
        return act

    def weights(l, part, after):
        if (l, part) in early:
            lands = _forward_wait(f"forward_wait_{l}_{part}", early.pop((l, part)), after)
        else:
            got = _gather_wait(f"gather_wait_{l}_{part}", started[3 * l + part], after)
            lands = _gather_forward(f"gather_forward_{l}_{part}", got)
        return dict(zip(group_keys[part], lands))

    xl = x[0]
    saved, small, wts = [], [], []
    for l in range(2):
        ga = weights(l, 0, xl if l else None)
        if l:
            xl = prefetch(l, 1, xl)
        wt = {"gu1": ga["gu1"][None], "d1": ga["d1"].reshape(1, 4, GU, D)}
        x1, s1 = _ffn_fwd("ffn1", xl, w["ffn1_norm"][l].reshape(1, D), wt["gu1"], wt["d1"], 0)
        gb = weights(l, 1, x1 if l else [x1] + while_idle)
        if l:
            x1 = prefetch(l, 2, x1)
        p = {n: w[n][l] for n in REPLICATED_SMALL}
        p["b_gate"] = jnp.transpose(gb["small"][:, 0:3, :], (1, 0, 2)).reshape(3, D)
        p["conv_w"] = jnp.transpose(gb["small"][:, 3:6, :], (1, 0, 2)).reshape(3, D)
        p = _small_params(p)
        wt["win"] = _win_kernel_layout("win_layout", gb["in"])

        def third_group(after, l=l, wt=wt):
            gc = weights(l, 2, after)
            wt.update({"gu2": gc["gu2"][None], "d2": gc["d2"].reshape(1, 4, GU, D),
                       "sq": jnp.stack([gc[k].reshape(D, D) for k in ("oc", "os", "oa", "o")])[None]})
            return wt["sq"]

        x2, sm = _mixer_fwd("mix", x1, p, wt["win"], third_group, 0, None if l else functools.partial(prefetch, 0, 2))
        if l == 0:
            x2 = prefetch(1, 0, x2)
        x3, s2 = _ffn_fwd("ffn2", x2, p["ffn2_norm"], wt["gu2"], wt["d2"], 0)
        saved.append((xl, x1, x2, s1, sm, s2))
        small.append(p)
        wts.append(wt)
        xl = x3
    loss_row, dx = _loss("loss", xl, loss_target[0])

    core = pc.reshape(1).astype(jnp.int32)
    chip_op = chip.reshape(1).astype(jnp.int32)
    buf = lambda r, c: lax.empty((2, 4, 1, r, c), bf16)
    flights = {}

    pairs = []
    scatter_keys = {"ffn2": ("gu2", "d2"), "sq": ("oc", "os", "oa", "o"), "in": ("in",), "ffn1": ("gu1", "d1")}

    def ship(l, part, bufs, dx):
        started, dx = _pair_start(f"pair_start_{l}_{part}", bufs, dx)
        pairs.append((l, part, started))
        return dx

    def finish(dx, after):
        l, part, started = pairs.pop()
        bufs, r1 = _pair_wait(f"pair_wait_{l}_{part}", started, after)
        ss = _pair_sums(f"pair_sums_{l}_{part}", core, list(bufs), list(r1))
        flights[l, part], dx = _scatter_start(f"scatter_start_{l}_{part}", ss, dx)
        return dx

    def ship_in(l, dwin, dx):
        return ship(l, "in", [_win_device_blocks("dwin_blocks", dwin)], dx)

    sgrads = [None, None]
    for l in (1, 0):
        p, wt = small[l], wts[l]
        x0, x1, x2, s1, sm, s2 = saved[l]
        dx, dn2 = _ffn_bwd("ffn2", dx, x2, p["ffn2_norm"], wt["gu2"], wt["d2"], 0, s2, buf(GU, D), buf(GU // 2, D),
                           functools.partial(ship, l, "ffn2"), finish)
        dx, sg = _mixer_bwd("mix", dx, x1, p, wt["win"], wt["sq"], 0, sm, [buf(128, D) for _ in range(4)],
                            functools.partial(ship, l, "sq"), functools.partial(ship_in, l), finish)
        dx, dn1 = _ffn_bwd("ffn1", dx, x0, p["ffn1_norm"], wt["gu1"], wt["d1"], 0, s1, buf(GU, D), buf(GU // 2, D),
                           functools.partial(ship, l, "ffn1"), finish, hook_at_once=(l == 0))
        sg["ffn1_norm"] = dn1
        sg["ffn2_norm"] = dn2
        sgrads[l] = sg

    nat = [_small_grads_natural(sgrads[l]) for l in range(2)]
    part = _pack([jnp.stack([nat[0][n], nat[1][n]]) for n in order] + [loss_row[0, 0:1]])
    small_flight, dx = _gather_start("small_start", [[(part, landing(part))]], carry=dx)

    grads, delta, new_m, new_v = {}, {}, {}, {}
    after = dx
    for part in ("ffn2", "sq", "in", "ffn1"):
        sets = []
        for l in (1, 0):
            s_all, r2_all = _scatter_wait(f"scatter_wait_{l}_{part}", flights[l, part], after)
            sets.append(list(zip(s_all, r2_all)))
        for i, k in enumerate(scatter_keys[part]):
            n = key_name[k]
            view = (lambda a: jnp.swapaxes(a, 1, 2)) if k in TRANSPOSED else (lambda a: a)
            outs = _adamw("adamw_" + k, chip_op, view(w[n]), view(mom[n]), view(var[n]), [sets[1][i], sets[0][i]])
            grads[n], delta[n], new_m[n], new_v[n] = [view(o) for o in outs]
            after = outs[1]

    blocks = _gather_forward("small_forward", _gather_wait("small_wait", small_flight[0], after))
    total = _sum_blocks("small_sum", blocks[0])
    full_shapes = [(2,) + tuple(nat[0][n].shape) for n in order] + [(1,)]
    summed = dict(zip(order + ("loss",), _unpack(total, full_shapes)))
    for n in REPLICATED_SMALL:
        grads[n] = summed[n]
    for n in SHARDED_SMALL:
        grads[n] = lax.dynamic_slice_in_dim(summed[n], dev * 128, 128, axis=2)
    gp = _pack([grads[n] for n in order])
    dpk, mpk, vpk = _adamw_small("adamw_small", wp, gp, mp, vp)
    local_shapes = [tuple(w[n].shape) for n in order]
    for dst, packed in ((delta, dpk), (new_m, mpk), (new_v, vpk)):
        dst.update(zip(order, _unpack(packed, local_shapes)))

    loss = summed["loss"][0]
    return (loss, dx[None], *[grads[n] for n in WEIGHT_NAMES], *[delta[n] for n in WEIGHT_NAMES],
            *[new_m[n] for n in WEIGHT_NAMES], *[new_v[n] for n in WEIGHT_NAMES])
```
